```python
import jax, jax.numpy as jnp
from jax import lax
import numpy as np

D_MODEL = 1024
BATCH = 8
SEQ = 2048
DEPTH = 1
DEC_BATCH = 128
DEC_SEQ = 4
PAST_LEN = 16384
PAGE_SIZE = 128

D_MIX = D_MODEL
GLA_WIDTH = D_MIX // 2
GLA_HEADS = 4
GLA_DV = GLA_WIDTH // GLA_HEADS
GLA_KDIM = GLA_WIDTH // 2
GLA_DK = GLA_KDIM // GLA_HEADS
GLA_LOWRANK = 16
GLA_NORMALIZER = 16.0
HGRN_WIDTH = D_MIX - GLA_WIDTH
HGRN_EXPAND = 128
HGRN_HEADS = HGRN_WIDTH // HGRN_EXPAND
HGRN_DK = HGRN_EXPAND
HGRN_FDIM = HGRN_HEADS * HGRN_DK
HGRN_DV = HGRN_WIDTH // HGRN_HEADS
D_IN = 2 * GLA_KDIM + 2 * GLA_WIDTH + GLA_LOWRANK + 2 * HGRN_FDIM + 2 * HGRN_WIDTH
CHUNK = 64
N_EXPERTS = 32
TOP_K = 4
D_FF = D_MODEL
SWIGLU_LIMIT = 7.0
SWIGLU_ALPHA = 1.702
MOE_BLOCK = 128
PLE_DIM = 256
EPS = 1e-6

kernel_name = 'hymba_gla_hgrn2_moe_ple_step'


def rms_norm(x, g):
    xf = x.astype(jnp.float32)
    y = xf * lax.rsqrt(jnp.mean(xf * xf, axis=-1, keepdims=True) + EPS)
    return (y * g.astype(jnp.float32)).astype(x.dtype)


def head_rms_norm(o, g):
    return o * lax.rsqrt(jnp.mean(o * o, axis=-1, keepdims=True) + EPS) * g.astype(jnp.float32)


def chunk_gated_linear(q, k, v, log_a, s0, scale):
    B, L, H, DK = q.shape
    DV = v.shape[-1]
    C = CHUNK if L % CHUNK == 0 else L
    N = L // C
    f32 = jnp.float32
    q = q.astype(f32).reshape(B, N, C, H, DK) * scale
    k = k.astype(f32).reshape(B, N, C, H, DK)
    v = v.astype(f32).reshape(B, N, C, H, DV)
    b = lax.cumsum(log_a.astype(f32).reshape(B, N, C, H, DK), axis=2)
    b_last = b[:, :, -1]
    ref = b[:, :, C // 2][:, :, None]
    q_r = q * jnp.exp(b - ref)
    k_r = k * jnp.exp(ref - b)
    att = jnp.einsum('bnchk,bnshk->bnhcs', q_r, k_r)
    causal = jnp.tril(jnp.ones((C, C), dtype=bool))
    att = jnp.where(causal, att, 0.0)
    o_intra = jnp.einsum('bnhcs,bnshv->bnchv', att, v)
    k_end = k * jnp.exp(b_last[:, :, None] - b)
    u = jnp.einsum('bnchk,bnchv->bnhkv', k_end, v)

    def step(s, inp):
        decay, du = inp
        return decay[..., None] * s + du, s

    s_final, s_prev = lax.scan(step, s0.astype(f32),
                               (jnp.moveaxis(jnp.exp(b_last), 1, 0), jnp.moveaxis(u, 1, 0)))
    s_prev = jnp.moveaxis(s_prev, 0, 1)
    o_inter = jnp.einsum('bnchk,bnhkv->bnchv', q * jnp.exp(b), s_prev)
    return (o_inter + o_intra).reshape(B, L, H, DV), s_final


def parallel_mixers(a, s_gla, s_hgrn, lb, w_in, w_gk2, b_gk2, gn_gla, gn_hgrn, w_out):
    B, L, _ = a.shape
    f32 = jnp.float32
    z = a @ w_in
    splits = np.cumsum([GLA_KDIM, GLA_KDIM, GLA_WIDTH, GLA_WIDTH, GLA_LOWRANK,
                        HGRN_FDIM, HGRN_FDIM, HGRN_WIDTH]).tolist()
    q1, k1, v1, g1, r1, q2, f2, i2, g2 = jnp.split(z, splits, axis=-1)
    log_a = jax.nn.log_sigmoid((r1 @ w_gk2 + b_gk2).astype(f32)) / GLA_NORMALIZER
    o1, s1 = chunk_gated_linear(q1.reshape(B, L, GLA_HEADS, GLA_DK),
                                k1.reshape(B, L, GLA_HEADS, GLA_DK),
                                v1.reshape(B, L, GLA_HEADS, GLA_DV),
                                log_a.reshape(B, L, GLA_HEADS, GLA_DK),
                                s_gla, GLA_DK ** -0.5)
    o1 = head_rms_norm(o1, gn_gla) * jax.nn.silu(g1.astype(f32).reshape(B, L, GLA_HEADS, GLA_DV))
    forget = lb + (1.0 - lb) * jax.nn.sigmoid(f2.astype(f32))
    q2h = jax.nn.silu(q2.astype(f32))
    i2h = jax.nn.silu(i2.astype(f32))
    o2, s2 = chunk_gated_linear(q2h.reshape(B, L, HGRN_HEADS, HGRN_DK),
                                (1.0 - forget).reshape(B, L, HGRN_HEADS, HGRN_DK),
                                i2h.reshape(B, L, HGRN_HEADS, HGRN_DV),
                                jnp.log(forget).reshape(B, L, HGRN_HEADS, HGRN_DK),
                                s_hgrn, 1.0)
    o2 = head_rms_norm(o2, gn_hgrn) * jax.nn.sigmoid(g2.astype(f32).reshape(B, L, HGRN_HEADS, HGRN_DV))
    o = jnp.concatenate([o1.reshape(B, L, GLA_WIDTH), o2.reshape(B, L, HGRN_WIDTH)], axis=-1).astype(a.dtype)
    return o @ w_out, s1, s2


def moe(x, w_router, b_router, w_gu, b_gu, w_dn, b_dn):
    B, L, D = x.shape
    T = B * L
    TK = T * TOP_K
    f32 = jnp.float32
    xt = x.reshape(T, D)
    logits = (xt @ w_router + b_router).astype(f32)
    top_v, top_i = lax.top_k(logits, TOP_K)
    gates = jax.nn.softmax(top_v, axis=-1)
    flat_e = top_i.reshape(-1).astype(jnp.int32)
    order = jnp.argsort(flat_e).astype(jnp.int32)
    sorted_e = flat_e[order]
    counts = jnp.bincount(flat_e, length=N_EXPERTS).astype(jnp.int32)
    padded = (counts + MOE_BLOCK - 1) // MOE_BLOCK * MOE_BLOCK
    start = jnp.cumsum(counts) - counts
    pend = jnp.cumsum(padded)
    pstart = pend - padded
    rank = jnp.arange(TK, dtype=jnp.int32) - start[sorted_e]
    dest = pstart[sorted_e] + rank
    n_blocks = -(-TK // MOE_BLOCK) + N_EXPERTS
    R = n_blocks * MOE_BLOCK
    row_tok = jnp.full((R,), T, dtype=jnp.int32).at[dest].set(order // TOP_K)
    row_gate = jnp.zeros((R,), f32).at[dest].set(gates.reshape(-1)[order])
    block_rows = jnp.arange(n_blocks, dtype=jnp.int32) * MOE_BLOCK
    block_e = jnp.minimum(jnp.searchsorted(pend, block_rows, side='right'), N_EXPERTS - 1)
    x_pad = jnp.concatenate([xt, jnp.zeros((1, D), xt.dtype)], axis=0)
    xb = x_pad[row_tok].reshape(n_blocks, MOE_BLOCK, D)

    def expert_block(args):
        xblk, e = args
        gu = xblk @ w_gu[e] + b_gu[e]
        gate, up = jnp.split(gu, 2, axis=-1)
        gate = jnp.minimum(gate, SWIGLU_LIMIT)
        up = jnp.clip(up, -SWIGLU_LIMIT, SWIGLU_LIMIT)
        hid = (up + 1.0) * (gate * jax.nn.sigmoid(gate * SWIGLU_ALPHA))
        return hid @ w_dn[e] + b_dn[e]

    y = lax.map(expert_block, (xb, block_e)).reshape(R, D)
    out = jnp.zeros((T + 1, D), f32).at[row_tok].add(y.astype(f32) * row_gate[:, None])[:T]
    return out.reshape(B, L, D).astype(x.dtype)


def setup_inputs(seed: int = 0) -> dict:
    key = jax.random.key(seed)
    ks = jax.random.split(key, 25)
    f32 = jnp.float32

    def nrm(k, shape, scale):
        return jax.random.normal(k, shape, f32) * scale

    return {
        'x_prompt': nrm(ks[0], (BATCH, SEQ, D_MODEL), 1.0),
        'x_sample': nrm(ks[1], (DEC_BATCH, DEC_SEQ, D_MODEL), 1.0),
        'p_prompt': nrm(ks[2], (DEPTH, BATCH, SEQ, PLE_DIM), 1.0),
        'p_sample': nrm(ks[3], (DEPTH, DEC_BATCH, DEC_SEQ, PLE_DIM), 1.0),
        'state_gla': nrm(ks[4], (DEPTH, DEC_BATCH, GLA_HEADS, GLA_DK, GLA_DV), 0.5),
        'state_hgrn': nrm(ks[5], (DEPTH, DEC_BATCH, HGRN_HEADS, HGRN_DK, HGRN_DV), 0.5),
        'g_mix': 1.0 + nrm(ks[6], (DEPTH, D_MODEL), 0.1),
        'w_in': nrm(ks[7], (DEPTH, D_MODEL, D_IN), D_MODEL ** -0.5),
        'w_gk2': nrm(ks[8], (DEPTH, GLA_LOWRANK, GLA_KDIM), GLA_LOWRANK ** -0.5),
        'b_gk2': nrm(ks[9], (DEPTH, GLA_KDIM), 0.1),
        'gn_gla': 1.0 + nrm(ks[10], (DEPTH, GLA_DV), 0.1),
        'lb_logits': nrm(ks[11], (DEPTH + 1, HGRN_FDIM), 0.1),
        'gn_hgrn': 1.0 + nrm(ks[12], (DEPTH, HGRN_DV), 0.1),
        'w_out': nrm(ks[13], (DEPTH, D_MIX, D_MODEL), D_MIX ** -0.5),
        'g_ffn': 1.0 + nrm(ks[14], (DEPTH, D_MODEL), 0.1),
        'w_router': nrm(ks[15], (DEPTH, D_MODEL, N_EXPERTS), D_MODEL ** -0.5),
        'b_router': nrm(ks[16], (DEPTH, N_EXPERTS), 0.01),
        'w_gu': nrm(ks[17], (DEPTH, N_EXPERTS, D_MODEL, 2 * D_FF), D_MODEL ** -0.5),
        'b_gu': nrm(ks[18], (DEPTH, N_EXPERTS, 2 * D_FF), 0.02),
        'w_dn': nrm(ks[19], (DEPTH, N_EXPERTS, D_FF, D_MODEL), D_FF ** -0.5),
        'b_dn': nrm(ks[20], (DEPTH, N_EXPERTS, D_MODEL), 0.02),
        'g_ple': 1.0 + nrm(ks[21], (DEPTH, D_MODEL), 0.1),
        'w_ple_gate': nrm(ks[22], (DEPTH, D_MODEL, D_MODEL), D_MODEL ** -0.5),
        'w_ple_proj': nrm(ks[23], (DEPTH, PLE_DIM, D_MODEL), PLE_DIM ** -0.5),
        'g_final': 1.0 + nrm(ks[24], (D_MODEL,), 0.1),
    }


def reference(x_prompt, x_sample, p_prompt, p_sample, state_gla, state_hgrn,
              g_mix, w_in, w_gk2, b_gk2, gn_gla, lb_logits, gn_hgrn, w_out,
              g_ffn, w_router, b_router, w_gu, b_gu, w_dn, b_dn,
              g_ple, w_ple_gate, w_ple_proj, g_final):
    f32 = jnp.float32
    lb_all = jnp.cumsum(jax.nn.softmax(lb_logits.astype(f32), axis=0), axis=0)

    def trunk(x, p, init_gla, init_hgrn):
        h = x
        new_gla = []
        new_hgrn = []
        for i in range(DEPTH):
            a = rms_norm(h, g_mix[i])
            o, sg, sh = parallel_mixers(a, init_gla[i], init_hgrn[i], lb_all[i], w_in[i], w_gk2[i],
                                        b_gk2[i], gn_gla[i], gn_hgrn[i], w_out[i])
            h = h + o
            h = h + moe(rms_norm(h, g_ffn[i]), w_router[i], b_router[i], w_gu[i], b_gu[i], w_dn[i], b_dn[i])
            gate = jax.nn.sigmoid((rms_norm(h, g_ple[i]) @ w_ple_gate[i]).astype(f32))
            h = h + (gate * (p[i] @ w_ple_proj[i]).astype(f32)).astype(h.dtype)
            new_gla.append(sg.astype(x.dtype))
            new_hgrn.append(sh.astype(x.dtype))
        return rms_norm(h, g_final), jnp.stack(new_gla), jnp.stack(new_hgrn)

    b_p = x_prompt.shape[0]
    zero_gla = jnp.zeros((DEPTH, b_p, GLA_HEADS, GLA_DK, GLA_DV), f32)
    zero_hgrn = jnp.zeros((DEPTH, b_p, HGRN_HEADS, HGRN_DK, HGRN_DV), f32)
    y_prompt, new_gla_prompt, new_hgrn_prompt = trunk(x_prompt, p_prompt, zero_gla, zero_hgrn)
    y_sample, new_gla_sample, new_hgrn_sample = trunk(x_sample, p_sample, state_gla, state_hgrn)
    return (y_prompt, y_sample, new_gla_prompt, new_hgrn_prompt, new_gla_sample, new_hgrn_sample)
```

```python
import functools

import jax
import jax.numpy as jnp
from jax import lax
from jax.experimental import pallas as pl
from jax.experimental.pallas import tpu as pltpu

f32 = jnp.float32
bf16 = jnp.bfloat16
i32 = jnp.int32

D_MODEL = 1024
GLA_HEADS = 4
GLA_DK = 64
GLA_DV = 128
GLA_KDIM = GLA_HEADS * GLA_DK
GLA_WIDTH = GLA_HEADS * GLA_DV
GLA_LOWRANK = 16
GLA_NORMALIZER = 16.0
HGRN_HEADS = 4
HGRN_DK = 128
HGRN_DV = 128
HGRN_FDIM = HGRN_HEADS * HGRN_DK
HGRN_WIDTH = HGRN_HEADS * HGRN_DV
CHUNK = 64
N_EXPERTS = 32
TOP_K = 4
D_FF = 1024
SWIGLU_LIMIT = 7.0
SWIGLU_ALPHA = 1.702
PLE_DIM = 256
EPS = 1e-6

LANES = 128
SAMPLE_PAD_T = 8

R1_PAD = LANES
Z_Q1 = 0
Z_K1 = Z_Q1 + GLA_KDIM
Z_V1 = Z_K1 + GLA_KDIM
Z_G1 = Z_V1 + GLA_WIDTH
Z_R1 = Z_G1 + GLA_WIDTH
Z_Q2 = Z_R1 + R1_PAD
Z_F2 = Z_Q2 + HGRN_FDIM
Z_I2 = Z_F2 + HGRN_FDIM
Z_G2 = Z_I2 + HGRN_WIDTH
Z_W = Z_G2 + HGRN_WIDTH

TOKEN_TILE = 512
MOE_TILE = 256
COMBINE_TILE = 256
VMEM_LIMIT = 56 * 1024 * 1024


def _rms(x, g):
    return x * lax.rsqrt(jnp.mean(x * x, axis=-1, keepdims=True) + EPS) * g


def _sigmoid(x):
    return 1.0 / (1.0 + jnp.exp(-x))


def _dot(a, b):
    return jnp.dot(a, b, preferred_element_type=f32)


def _dot_nt(a, b):
    return lax.dot_general(a, b, (((1,), (1,)), ((), ())), preferred_element_type=f32)


def _dot_tn(a, b):
    return lax.dot_general(a, b, (((0,), (0,)), ((), ())), preferred_element_type=f32)


def _cumsum_rows(tri, x):
    hi = x.astype(bf16)
    r1 = x - hi.astype(f32)
    mid = r1.astype(bf16)
    lo = (r1 - mid.astype(f32)).astype(bf16)
    return _dot(tri, hi) + _dot(tri, mid) + _dot(tri, lo)


def _select_rows(is_prompt, a_ref, b_ref):
    return jnp.where(is_prompt, a_ref[...], b_ref[...])


def _inproj_kernel(xp_ref, xs_ref, g_ref, w_ref, z_ref, *, n_prompt_tiles, col_chunk):
    is_prompt = pl.program_id(0) < n_prompt_tiles
    x = _select_rows(is_prompt, xp_ref, xs_ref)
    a = _rms(x, g_ref[...]).astype(bf16)
    for c0 in range(0, Z_W, col_chunk):
        c1 = min(c0 + col_chunk, Z_W)
        z_ref[:, c0:c1] = _dot(a, w_ref[:, c0:c1]).astype(bf16)


def _in_proj(xp, xs, g, w):
    tp, ts = xp.shape[0], xs.shape[0]
    npt = tp // TOKEN_TILE
    nt = npt + ts // TOKEN_TILE
    return pl.pallas_call(
        functools.partial(_inproj_kernel, n_prompt_tiles=npt, col_chunk=512),
        grid=(nt,),
        in_specs=[
            pl.BlockSpec((TOKEN_TILE, D_MODEL), lambda i: (jnp.minimum(i, npt - 1), 0)),
            pl.BlockSpec((TOKEN_TILE, D_MODEL), lambda i: (jnp.maximum(i - npt, 0), 0)),
            pl.BlockSpec((1, D_MODEL), lambda i: (0, 0)),
            pl.BlockSpec((D_MODEL, Z_W), lambda i: (0, 0)),
        ],
        out_specs=pl.BlockSpec((TOKEN_TILE, Z_W), lambda i: (i, 0)),
        out_shape=jax.ShapeDtypeStruct((tp + ts, Z_W), bf16),
        compiler_params=pltpu.CompilerParams(
            dimension_semantics=("arbitrary",), vmem_limit_bytes=VMEM_LIMIT),
        name="in_proj",
    )(xp, xs, g, w)


def _gated_chunk(q, k, v, log_a, states, *, dk, scale, ref_row, tri, causal, eye):
    c = q.shape[0]
    b = _cumsum_rows(tri, log_a)
    b_ref = b[ref_row:ref_row + 1, :]
    b_last = b[c - 1:c, :]
    qs = q * scale
    q_r = (qs * jnp.exp(b - b_ref)).astype(bf16)
    k_r = (k * jnp.exp(b_ref - b)).astype(bf16)
    k_end = (k * jnp.exp(b_last - b)).astype(bf16)
    q_b = (qs * jnp.exp(b)).astype(bf16)
    decay = jnp.exp(b_last)
    vb = v.astype(bf16)
    dv = v.shape[1] // len(states)
    outs, new_states = [], []
    for h, s in enumerate(states):
        ks = slice(h * dk, (h + 1) * dk)
        vs = slice(h * dv, (h + 1) * dv)
        att = _dot_nt(q_r[:, ks], k_r[:, ks])
        att = jnp.where(causal, att, 0.0).astype(bf16)
        o = _dot(att, vb[:, vs]) + _dot(q_b[:, ks], s.astype(bf16))
        u = _dot_tn(k_end[:, ks], vb[:, vs])
        d_col = jnp.sum(jnp.where(eye, decay[:, ks], 0.0), axis=1, keepdims=True)
        outs.append(o)
        new_states.append(d_col * s + u)
    return outs, new_states


def _head_norm(o, g):
    return o * lax.rsqrt(jnp.mean(o * o, axis=-1, keepdims=True) + EPS) * g


def _mixer_chunk(zc, sg, sh, wgk, bgk, gng, lb, gnh, *, ref_row, tri, causal, eye_g, eye_h, valid):
    z = zc.astype(f32)
    gk = _dot(zc[:, Z_R1:Z_R1 + R1_PAD], wgk) + bgk
    log_a = (jnp.minimum(gk, 0.0) - jnp.log(1.0 + jnp.exp(-jnp.abs(gk)))) / GLA_NORMALIZER
    k1 = z[:, Z_K1:Z_K1 + GLA_KDIM]
    if valid is not None:
        log_a = jnp.where(valid, log_a, 0.0)
        k1 = jnp.where(valid, k1, 0.0)
    o1, sg_new = _gated_chunk(
        z[:, Z_Q1:Z_Q1 + GLA_KDIM], k1, z[:, Z_V1:Z_V1 + GLA_WIDTH], log_a, sg,
        dk=GLA_DK, scale=GLA_DK ** -0.5, ref_row=ref_row, tri=tri, causal=causal, eye=eye_g)
    forget = lb + (1.0 - lb) * _sigmoid(z[:, Z_F2:Z_F2 + HGRN_FDIM])
    k2 = 1.0 - forget
    log_f = jnp.log(forget)
    if valid is not None:
        log_f = jnp.where(valid, log_f, 0.0)
        k2 = jnp.where(valid, k2, 0.0)
    q2 = z[:, Z_Q2:Z_Q2 + HGRN_FDIM]
    i2 = z[:, Z_I2:Z_I2 + HGRN_WIDTH]
    o2, sh_new = _gated_chunk(
        q2 * _sigmoid(q2), k2, i2 * _sigmoid(i2), log_f, sh,
        dk=HGRN_DK, scale=1.0, ref_row=ref_row, tri=tri, causal=causal, eye=eye_h)
    cols = []
    for h in range(GLA_HEADS):
        g1 = z[:, Z_G1 + h * GLA_DV:Z_G1 + (h + 1) * GLA_DV]
        cols.append(_head_norm(o1[h], gng) * (g1 * _sigmoid(g1)))
    for h in range(HGRN_HEADS):
        g2 = z[:, Z_G2 + h * HGRN_DV:Z_G2 + (h + 1) * HGRN_DV]
        cols.append(_head_norm(o2[h], gnh) * _sigmoid(g2))
    return jnp.concatenate(cols, axis=1), sg_new, sh_new


def _chunk_consts(c):
    row = lax.broadcasted_iota(i32, (c, c), 0)
    col = lax.broadcasted_iota(i32, (c, c), 1)
    causal = row >= col
    tri = causal.astype(bf16)
    eye_g = (lax.broadcasted_iota(i32, (GLA_DK, GLA_DK), 0)
             == lax.broadcasted_iota(i32, (GLA_DK, GLA_DK), 1))
    eye_h = (lax.broadcasted_iota(i32, (HGRN_DK, HGRN_DK), 0)
             == lax.broadcasted_iota(i32, (HGRN_DK, HGRN_DK), 1))
    return tri, causal, eye_g, eye_h


def _lower_bound(lbl_ref):
    l = lbl_ref[...]
    m = jnp.max(l, axis=0, keepdims=True)
    e = jnp.exp(l - m)
    return e[0:1, :] / jnp.sum(e, axis=0, keepdims=True)


def _mixer_prompt_kernel(z_ref, wgk_ref, bgk_ref, gng_ref, lbl_ref, gnh_ref,
                         o_ref, sg_out, sh_out, sg, sh, *, chunks):
    j = pl.program_id(1)

    @pl.when(j == 0)
    def _():
        sg[...] = jnp.zeros_like(sg)
        sh[...] = jnp.zeros_like(sh)

    tri, causal, eye_g, eye_h = _chunk_consts(CHUNK)
    lb = _lower_bound(lbl_ref)
    wgk, bgk, gng, gnh = wgk_ref[...], bgk_ref[...], gng_ref[...], gnh_ref[...]

    def body(c, carry):
        r0 = pl.multiple_of(c * CHUNK, CHUNK)
        zc = z_ref[pl.ds(r0, CHUNK), :]
        o, sg_new, sh_new = _mixer_chunk(
            zc, [sg[h] for h in range(GLA_HEADS)], [sh[h] for h in range(HGRN_HEADS)],
            wgk, bgk, gng, lb, gnh, ref_row=CHUNK // 2, tri=tri, causal=causal,
            eye_g=eye_g, eye_h=eye_h, valid=None)
        o_ref[pl.ds(r0, CHUNK), :] = o.astype(bf16)
        for h in range(GLA_HEADS):
            sg[h] = sg_new[h]
        for h in range(HGRN_HEADS):
            sh[h] = sh_new[h]
        return carry

    lax.fori_loop(0, chunks, body, 0)

    @pl.when(j == pl.num_programs(1) - 1)
    def _():
        sg_out[0, 0] = sg[...]
        sh_out[0, 0] = sh[...]


def _mixer_prompt(z, wgk, bgk, gng, lbl, gnh, *, batch, seq, rows_per_step):
    nj = seq // rows_per_step
    full = lambda shape: pl.BlockSpec(shape, lambda b, j: (0,) * len(shape))
    return pl.pallas_call(
        functools.partial(_mixer_prompt_kernel, chunks=rows_per_step // CHUNK),
        grid=(batch, nj),
        in_specs=[
            pl.BlockSpec((rows_per_step, Z_W), lambda b, j: (b * nj + j, 0)),
            full((R1_PAD, GLA_KDIM)), full((1, GLA_KDIM)), full((1, GLA_DV)),
            full((2, HGRN_FDIM)), full((1, HGRN_DV)),
        ],
        out_specs=[
            pl.BlockSpec((rows_per_step, D_MODEL), lambda b, j: (b * nj + j, 0)),
            pl.BlockSpec((1, 1, GLA_HEADS, GLA_DK, GLA_DV), lambda b, j: (0, b, 0, 0, 0)),
            pl.BlockSpec((1, 1, HGRN_HEADS, HGRN_DK, HGRN_DV), lambda b, j: (0, b, 0, 0, 0)),
        ],
        out_shape=[
            jax.ShapeDtypeStruct((batch * seq, D_MODEL), bf16),
            jax.ShapeDtypeStruct((1, batch, GLA_HEADS, GLA_DK, GLA_DV), f32),
            jax.ShapeDtypeStruct((1, batch, HGRN_HEADS, HGRN_DK, HGRN_DV), f32),
        ],
        scratch_shapes=[
            pltpu.VMEM((GLA_HEADS, GLA_DK, GLA_DV), f32),
            pltpu.VMEM((HGRN_HEADS, HGRN_DK, HGRN_DV), f32),
        ],
        compiler_params=pltpu.CompilerParams(
            dimension_semantics=("arbitrary", "arbitrary"), vmem_limit_bytes=VMEM_LIMIT),
        name="mixer_prompt",
    )(z, wgk, bgk, gng, lbl, gnh)


def _mixer_sample_kernel(z_ref, sg_in, sh_in, wgk_ref, bgk_ref, gng_ref, lbl_ref, gnh_ref,
                         o_ref, sg_out, sh_out, *, batch_block, steps):
    tri, causal, eye_g, eye_h = _chunk_consts(SAMPLE_PAD_T)
    valid = lax.broadcasted_iota(i32, (SAMPLE_PAD_T, 1), 0) < steps
    lb = _lower_bound(lbl_ref)
    wgk, bgk, gng, gnh = wgk_ref[...], bgk_ref[...], gng_ref[...], gnh_ref[...]

    def body(i, carry):
        o, sg_new, sh_new = _mixer_chunk(
            z_ref[i], [sg_in[0, i, h] for h in range(GLA_HEADS)],
            [sh_in[0, i, h] for h in range(HGRN_HEADS)],
            wgk, bgk, gng, lb, gnh, ref_row=steps // 2, tri=tri, causal=causal,
            eye_g=eye_g, eye_h=eye_h, valid=valid)
        o_ref[i] = o
        for h in range(GLA_HEADS):
            sg_out[0, i, h] = sg_new[h]
        for h in range(HGRN_HEADS):
            sh_out[0, i, h] = sh_new[h]
        return carry

    lax.fori_loop(0, batch_block, body, 0)


def _mixer_sample(z3, state_gla, state_hgrn, wgk, bgk, gng, lbl, gnh, *, steps, batch_block):
    batch = z3.shape[0]
    full = lambda shape: pl.BlockSpec(shape, lambda i: (0,) * len(shape))
    sg_spec = pl.BlockSpec((1, batch_block, GLA_HEADS, GLA_DK, GLA_DV), lambda i: (0, i, 0, 0, 0))
    sh_spec = pl.BlockSpec((1, batch_block, HGRN_HEADS, HGRN_DK, HGRN_DV), lambda i: (0, i, 0, 0, 0))
    return pl.pallas_call(
        functools.partial(_mixer_sample_kernel, batch_block=batch_block, steps=steps),
        grid=(batch // batch_block,),
        in_specs=[
            pl.BlockSpec((batch_block, SAMPLE_PAD_T, Z_W), lambda i: (i, 0, 0)),
            sg_spec, sh_spec,
            full((R1_PAD, GLA_KDIM)), full((1, GLA_KDIM)), full((1, GLA_DV)),
            full((2, HGRN_FDIM)), full((1, HGRN_DV)),
        ],
        out_specs=[
            pl.BlockSpec((batch_block, SAMPLE_PAD_T, D_MODEL), lambda i: (i, 0, 0)),
            sg_spec, sh_spec,
        ],
        out_shape=[
            jax.ShapeDtypeStruct((batch, SAMPLE_PAD_T, D_MODEL), f32),
            jax.ShapeDtypeStruct(state_gla.shape, f32),
            jax.ShapeDtypeStruct(state_hgrn.shape, f32),
        ],
        compiler_params=pltpu.CompilerParams(
            dimension_semantics=("arbitrary",), vmem_limit_bytes=VMEM_LIMIT),
        name="mixer_sample",
    )(z3, state_gla, state_hgrn, wgk, bgk, gng, lbl, gnh)


def _outproj_kernel(xp_ref, xs_ref, o_ref, wo_ref, g_ref, wr_ref, br_ref,
                    h_ref, xn_ref, ti_ref, tg_ref, *, n_prompt_tiles):
    is_prompt = pl.program_id(0) < n_prompt_tiles
    x = _select_rows(is_prompt, xp_ref, xs_ref)
    h = x + _dot(o_ref[...], wo_ref[...])
    h_ref[...] = h
    xn = _rms(h, g_ref[...])
    xn_ref[...] = xn
    logits = _dot(xn.astype(bf16), wr_ref[...]) + br_ref[...]
    lane = lax.broadcasted_iota(i32, logits.shape, 1)
    neg = jnp.float32(-jnp.inf)
    l = jnp.where(lane < N_EXPERTS, logits, neg)
    vals, idxs = [], []
    for _ in range(TOP_K):
        m = jnp.max(l, axis=-1, keepdims=True)
        idx = jnp.min(jnp.where(l == m, lane, LANES), axis=-1, keepdims=True)
        vals.append(m)
        idxs.append(idx)
        l = jnp.where(lane == idx, neg, l)
    es = [jnp.exp(v - vals[0]) for v in vals]
    denom = es[0] + es[1] + es[2] + es[3]
    ti = jnp.zeros(logits.shape, i32)
    tg = jnp.zeros(logits.shape, f32)
    for k in range(TOP_K):
        ti = jnp.where(lane == k, idxs[k], ti)
        tg = jnp.where(lane == k, es[k] / denom, tg)
    ti_ref[...] = ti
    tg_ref[...] = tg


def _out_proj(xp, xs, o, wo, g, wr, br):
    tp, ts = xp.shape[0], xs.shape[0]
    npt = tp // TOKEN_TILE
    nt = npt + ts // TOKEN_TILE
    t = tp + ts
    row = lambda w: pl.BlockSpec((TOKEN_TILE, w), lambda i: (i, 0))
    full = lambda shape: pl.BlockSpec(shape, lambda i: (0,) * len(shape))
    return pl.pallas_call(
        functools.partial(_outproj_kernel, n_prompt_tiles=npt),
        grid=(nt,),
        in_specs=[
            pl.BlockSpec((TOKEN_TILE, D_MODEL), lambda i: (jnp.minimum(i, npt - 1), 0)),
            pl.BlockSpec((TOKEN_TILE, D_MODEL), lambda i: (jnp.maximum(i - npt, 0), 0)),
            row(D_MODEL), full((D_MODEL, D_MODEL)), full((1, D_MODEL)),
            full((D_MODEL, LANES)), full((1, LANES)),
        ],
        out_specs=[row(D_MODEL), row(D_MODEL), row(LANES), row(LANES)],
        out_shape=[
            jax.ShapeDtypeStruct((t, D_MODEL), f32),
            jax.ShapeDtypeStruct((t, D_MODEL), f32),
            jax.ShapeDtypeStruct((t, LANES), i32),
            jax.ShapeDtypeStruct((t, LANES), f32),
        ],
        compiler_params=pltpu.CompilerParams(
            dimension_semantics=("arbitrary",), vmem_limit_bytes=VMEM_LIMIT),
        name="out_proj_router",
    )(xp, xs, o, wo, g, wr, br)


def _row_copy(src_hbm, dst_ref, sem, src_row, dst_row):
    return pltpu.make_async_copy(
        src_hbm.at[pl.ds(src_row, 1), :], dst_ref.at[pl.ds(dst_row, 1), :], sem)


def _dispatch_kernel(tok_ref, xn_hbm, xs_ref, sem):
    def start(r, carry):
        _row_copy(xn_hbm, xs_ref, sem, tok_ref[0, 0, r], r).start()
        return carry

    def wait(r, carry):
        _row_copy(xn_hbm, xs_ref, sem, 0, r).wait()
        return carry

    lax.fori_loop(0, MOE_TILE, start, 0)
    lax.fori_loop(0, MOE_TILE, wait, 0)


def _dispatch(row_tok, xn):
    n_blocks = row_tok.shape[0]
    return pl.pallas_call(
        _dispatch_kernel,
        grid=(n_blocks,),
        in_specs=[
            pl.BlockSpec((1, 1, MOE_TILE), lambda i: (i, 0, 0), memory_space=pltpu.SMEM),
            pl.BlockSpec(memory_space=pl.ANY),
        ],
        out_specs=pl.BlockSpec((MOE_TILE, D_MODEL), lambda i: (i, 0)),
        out_shape=jax.ShapeDtypeStruct((n_blocks * MOE_TILE, D_MODEL), f32),
        scratch_shapes=[pltpu.SemaphoreType.DMA(())],
        compiler_params=pltpu.CompilerParams(
            dimension_semantics=("arbitrary",), vmem_limit_bytes=VMEM_LIMIT),
        name="moe_dispatch",
    )(row_tok, xn)


def _expert_kernel(be_ref, nb_ref, x_ref, wgu_ref, bgu_ref, wdn_ref, bdn_ref, y_ref, *, ff_chunk):
    @pl.when(pl.program_id(0) < nb_ref[0])
    def _():
        xb = x_ref[...].astype(bf16)
        acc = jnp.zeros((MOE_TILE, D_MODEL), f32)
        for c0 in range(0, D_FF, ff_chunk):
            gate = _dot(xb, wgu_ref[0, :, c0:c0 + ff_chunk]) + bgu_ref[0, :, c0:c0 + ff_chunk]
            up = (_dot(xb, wgu_ref[0, :, D_FF + c0:D_FF + c0 + ff_chunk])
                  + bgu_ref[0, :, D_FF + c0:D_FF + c0 + ff_chunk])
            gate = jnp.minimum(gate, SWIGLU_LIMIT)
            up = jnp.clip(up, -SWIGLU_LIMIT, SWIGLU_LIMIT)
            hid = (up + 1.0) * (gate * _sigmoid(gate * SWIGLU_ALPHA))
            acc = acc + _dot(hid.astype(bf16), wdn_ref[0, c0:c0 + ff_chunk, :])
        y_ref[...] = acc + bdn_ref[0]

    @pl.when(pl.program_id(0) >= nb_ref[0])
    def _():
        y_ref[...] = jnp.zeros_like(y_ref)


def _experts(block_e, n_used, xs, wgu, bgu, wdn, bdn):
    n_blocks = block_e.shape[0]
    grid_spec = pltpu.PrefetchScalarGridSpec(
        num_scalar_prefetch=2,
        grid=(n_blocks,),
        in_specs=[
            pl.BlockSpec((MOE_TILE, D_MODEL), lambda i, be, nb: (i, 0)),
            pl.BlockSpec((1, D_MODEL, 2 * D_FF), lambda i, be, nb: (be[i], 0, 0)),
            pl.BlockSpec((1, 1, 2 * D_FF), lambda i, be, nb: (be[i], 0, 0)),
            pl.BlockSpec((1, D_FF, D_MODEL), lambda i, be, nb: (be[i], 0, 0)),
            pl.BlockSpec((1, 1, D_MODEL), lambda i, be, nb: (be[i], 0, 0)),
        ],
        out_specs=pl.BlockSpec((MOE_TILE, D_MODEL), lambda i, be, nb: (i, 0)),
    )
    return pl.pallas_call(
        functools.partial(_expert_kernel, ff_chunk=256),
        grid_spec=grid_spec,
        out_shape=jax.ShapeDtypeStruct((n_blocks * MOE_TILE, D_MODEL), f32),
        compiler_params=pltpu.CompilerParams(
            dimension_semantics=("arbitrary",), vmem_limit_bytes=VMEM_LIMIT),
        name="moe_experts",
    )(block_e, n_used, xs, wgu, bgu, wdn, bdn)


def _combine_kernel(pos_ref, ys_hbm, h_ref, tg_ref, pp_ref, ps_ref, gple_ref, wpg_ref, wpp_ref,
                    gfin_ref, y_ref, rows, sem, *, n_prompt_tiles):
    n = COMBINE_TILE * TOP_K

    def start(r, carry):
        _row_copy(ys_hbm, rows, sem, pos_ref[0, 0, r], r).start()
        return carry

    def wait(r, carry):
        _row_copy(ys_hbm, rows, sem, 0, r).wait()
        return carry

    lax.fori_loop(0, n, start, 0)
    lax.fori_loop(0, n, wait, 0)

    tg = tg_ref[...]
    moe = jnp.zeros((COMBINE_TILE, D_MODEL), f32)
    for k in range(TOP_K):
        moe = moe + tg[:, k:k + 1] * rows[k * COMBINE_TILE:(k + 1) * COMBINE_TILE, :]
    h = h_ref[...] + moe
    gate = _sigmoid(_dot(_rms(h, gple_ref[...]).astype(bf16), wpg_ref[...]))
    is_prompt = pl.program_id(0) < n_prompt_tiles
    p = _select_rows(is_prompt, pp_ref, ps_ref).astype(bf16)
    h = h + gate * _dot(p, wpp_ref[...])
    y_ref[...] = _rms(h, gfin_ref[...])


def _combine(pos, ys, h1, tg, pp, ps, gple, wpg, wpp, gfin):
    t = h1.shape[0]
    nt = t // COMBINE_TILE
    npt = pp.shape[0] // COMBINE_TILE
    row = lambda w: pl.BlockSpec((COMBINE_TILE, w), lambda i: (i, 0))
    full = lambda shape: pl.BlockSpec(shape, lambda i: (0,) * len(shape))
    return pl.pallas_call(
        functools.partial(_combine_kernel, n_prompt_tiles=npt),
        grid=(nt,),
        in_specs=[
            pl.BlockSpec((1, 1, COMBINE_TILE * TOP_K), lambda i: (i, 0, 0), memory_space=pltpu.SMEM),
            pl.BlockSpec(memory_space=pl.ANY),
            row(D_MODEL), row(LANES),
            pl.BlockSpec((COMBINE_TILE, PLE_DIM), lambda i: (jnp.minimum(i, npt - 1), 0)),
            pl.BlockSpec((COMBINE_TILE, PLE_DIM), lambda i: (jnp.maximum(i - npt, 0), 0)),
            full((1, D_MODEL)), full((D_MODEL, D_MODEL)), full((PLE_DIM, D_MODEL)),
            full((1, D_MODEL)),
        ],
        out_specs=row(D_MODEL),
        out_shape=jax.ShapeDtypeStruct((t, D_MODEL), f32),
        scratch_shapes=[
            pltpu.VMEM((COMBINE_TILE * TOP_K, D_MODEL), f32),
            pltpu.SemaphoreType.DMA(()),
        ],
        compiler_params=pltpu.CompilerParams(
            dimension_semantics=("arbitrary",), vmem_limit_bytes=VMEM_LIMIT),
        name="moe_combine_ple",
    )(pos, ys, h1, tg, pp, ps, gple, wpg, wpp, gfin)


def _routing_tables(top_i):
    t = top_i.shape[0]
    tk = t * TOP_K
    flat_e = top_i.reshape(-1)
    order = jnp.argsort(flat_e).astype(i32)
    sorted_e = flat_e[order]
    counts = jnp.bincount(flat_e, length=N_EXPERTS).astype(i32)
    padded = (counts + MOE_TILE - 1) // MOE_TILE * MOE_TILE
    start = jnp.cumsum(counts) - counts
    pend = jnp.cumsum(padded)
    pstart = pend - padded
    dest = pstart[sorted_e] + jnp.arange(tk, dtype=i32) - start[sorted_e]
    n_blocks = -(-tk // MOE_TILE) + N_EXPERTS
    row_tok = jnp.zeros((n_blocks * MOE_TILE,), i32).at[dest].set(order // TOP_K)
    pos = jnp.zeros((tk,), i32).at[order].set(dest)
    block_rows = jnp.arange(n_blocks, dtype=i32) * MOE_TILE
    block_e = jnp.minimum(jnp.searchsorted(pend, block_rows, side="right"), N_EXPERTS - 1).astype(i32)
    n_used = (pend[-1:] // MOE_TILE).astype(i32)
    return row_tok.reshape(n_blocks, 1, MOE_TILE), pos, block_e, n_used


def kernel(x_prompt, x_sample, p_prompt, p_sample, state_gla, state_hgrn, g_mix, w_in, w_gk2, b_gk2,
           gn_gla, lb_logits, gn_hgrn, w_out, g_ffn, w_router, b_router, w_gu, b_gu, w_dn, b_dn,
           g_ple, w_ple_gate, w_ple_proj, g_final):
    batch, seq, _ = x_prompt.shape
    dec_batch, dec_seq, _ = x_sample.shape
    tp, ts = batch * seq, dec_batch * dec_seq
    xp = x_prompt.reshape(tp, D_MODEL)
    xs = x_sample.reshape(ts, D_MODEL)

    r1_end = Z_R1 + GLA_LOWRANK
    w_in_p = jnp.concatenate(
        [w_in[0, :, :r1_end], jnp.zeros((D_MODEL, R1_PAD - GLA_LOWRANK), f32), w_in[0, :, r1_end:]],
        axis=1).astype(bf16)
    wgk = jnp.concatenate(
        [w_gk2[0], jnp.zeros((R1_PAD - GLA_LOWRANK, GLA_KDIM), f32)], axis=0).astype(bf16)
    wr = jnp.concatenate(
        [w_router[0], jnp.zeros((D_MODEL, LANES - N_EXPERTS), f32)], axis=1).astype(bf16)
    br = jnp.concatenate([b_router[0], jnp.zeros((LANES - N_EXPERTS,), f32)]).reshape(1, LANES)

    z = _in_proj(xp, xs, g_mix[0].reshape(1, D_MODEL), w_in_p)

    mixer_params = (wgk, b_gk2[0].reshape(1, GLA_KDIM), gn_gla[0].reshape(1, GLA_DV),
                    lb_logits[0:2], gn_hgrn[0].reshape(1, HGRN_DV))
    o_p, new_gla_p, new_hgrn_p = _mixer_prompt(z, *mixer_params, batch=batch, seq=seq, rows_per_step=512)
    z_s = jnp.pad(z[tp:].reshape(dec_batch, dec_seq, Z_W), ((0, 0), (0, SAMPLE_PAD_T - dec_seq), (0, 0)))
    o_s, new_gla_s, new_hgrn_s = _mixer_sample(
        z_s, state_gla, state_hgrn, *mixer_params, steps=dec_seq, batch_block=8)
    o = jnp.concatenate([o_p, o_s[:, :dec_seq].reshape(ts, D_MODEL).astype(bf16)], axis=0)

    h1, xn, top_i, top_g = _out_proj(xp, xs, o, w_out[0].astype(bf16), g_ffn[0].reshape(1, D_MODEL), wr, br)

    row_tok, pos, block_e, n_used = _routing_tables(top_i[:, :TOP_K])
    x_sorted = _dispatch(row_tok, xn)
    y_sorted = _experts(block_e, n_used, x_sorted, w_gu[0].astype(bf16), b_gu[0].reshape(N_EXPERTS, 1, 2 * D_FF),
                        w_dn[0].astype(bf16), b_dn[0].reshape(N_EXPERTS, 1, D_MODEL))

    t = tp + ts
    pos_tiles = pos.reshape(t // COMBINE_TILE, COMBINE_TILE, TOP_K).transpose(0, 2, 1).reshape(
        t // COMBINE_TILE, 1, COMBINE_TILE * TOP_K)
    y = _combine(pos_tiles, y_sorted, h1, top_g, p_prompt[0].reshape(tp, PLE_DIM),
                 p_sample[0].reshape(ts, PLE_DIM), g_ple[0].reshape(1, D_MODEL),
                 w_ple_gate[0].astype(bf16), w_ple_proj[0].astype(bf16), g_final.reshape(1, D_MODEL))

    return (y[:tp].reshape(batch, seq, D_MODEL), y[tp:].reshape(dec_batch, dec_seq, D_MODEL),
            new_gla_p, new_hgrn_p, new_gla_s, new_hgrn_s)
```

```python
import functools

import jax
import jax.numpy as jnp
from jax import lax
from jax.experimental import pallas as pl
from jax.experimental.pallas import tpu as pltpu

f32 = jnp.float32
bf16 = jnp.bfloat16
i32 = jnp.int32

D_MODEL = 1024
GLA_HEADS = 4
GLA_DK = 64
GLA_DV = 128
GLA_KDIM = GLA_HEADS * GLA_DK
GLA_WIDTH = GLA_HEADS * GLA_DV
GLA_LOWRANK = 16
GLA_NORMALIZER = 16.0
HGRN_HEADS = 4
HGRN_DK = 128
HGRN_DV = 128
HGRN_FDIM = HGRN_HEADS * HGRN_DK
HGRN_WIDTH = HGRN_HEADS * HGRN_DV
CHUNK = 64
N_EXPERTS = 32
TOP_K = 4
D_FF = 1024
SWIGLU_LIMIT = 7.0
SWIGLU_ALPHA = 1.702
PLE_DIM = 256
EPS = 1e-6

LANES = 128
SAMPLE_PAD_T = 8
SLABS = D_MODEL // LANES

R1_PAD = LANES
Z_Q1 = 0
Z_K1 = Z_Q1 + GLA_KDIM
Z_V1 = Z_K1 + GLA_KDIM
Z_G1 = Z_V1 + GLA_WIDTH
Z_R1 = Z_G1 + GLA_WIDTH
Z_Q2 = Z_R1 + R1_PAD
Z_F2 = Z_Q2 + HGRN_FDIM
Z_I2 = Z_F2 + HGRN_FDIM
Z_G2 = Z_I2 + HGRN_WIDTH
Z_W = Z_G2 + HGRN_WIDTH

TOKEN_TILE = 512
MOE_TILE = 256
DISPATCH_TILE = 256
COMBINE_TILE = 256
VMEM_LIMIT = 56 * 1024 * 1024


def _rms(x, g):
    return x * lax.rsqrt(jnp.mean(x * x, axis=-1, keepdims=True) + EPS) * g


def _sigmoid(x):
    return 1.0 / (1.0 + jnp.exp(-x))


def _dot(a, b):
    return jnp.dot(a, b, preferred_element_type=f32)


def _dot_nt(a, b):
    return lax.dot_general(a, b, (((1,), (1,)), ((), ())), preferred_element_type=f32)


def _dot_tn(a, b):
    return lax.dot_general(a, b, (((0,), (0,)), ((), ())), preferred_element_type=f32)


def _cumsum_rows(tri, x):
    hi = x.astype(bf16)
    r1 = x - hi.astype(f32)
    mid = r1.astype(bf16)
    lo = (r1 - mid.astype(f32)).astype(bf16)
    return _dot(tri, hi) + _dot(tri, mid) + _dot(tri, lo)


def _select_rows(is_prompt, a_ref, b_ref):
    return jnp.where(is_prompt, a_ref[...], b_ref[...])


def _inproj_kernel(xp_ref, xs_ref, g_ref, w_ref, z_ref, *, n_prompt_tiles, col_chunk):
    is_prompt = pl.program_id(0) < n_prompt_tiles
    x = _select_rows(is_prompt, xp_ref, xs_ref)
    a = _rms(x, g_ref[...]).astype(bf16)
    for c0 in range(0, Z_W, col_chunk):
        c1 = min(c0 + col_chunk, Z_W)
        z_ref[:, c0:c1] = _dot(a, w_ref[:, c0:c1]).astype(bf16)


def _in_proj(xp, xs, g, w):
    tp, ts = xp.shape[0], xs.shape[0]
    npt = tp // TOKEN_TILE
    nt = npt + ts // TOKEN_TILE
    return pl.pallas_call(
        functools.partial(_inproj_kernel, n_prompt_tiles=npt, col_chunk=512),
        grid=(nt,),
        in_specs=[
            pl.BlockSpec((TOKEN_TILE, D_MODEL), lambda i: (jnp.minimum(i, npt - 1), 0)),
            pl.BlockSpec((TOKEN_TILE, D_MODEL), lambda i: (jnp.maximum(i - npt, 0), 0)),
            pl.BlockSpec((1, D_MODEL), lambda i: (0, 0)),
            pl.BlockSpec((D_MODEL, Z_W), lambda i: (0, 0)),
        ],
        out_specs=pl.BlockSpec((TOKEN_TILE, Z_W), lambda i: (i, 0)),
        out_shape=jax.ShapeDtypeStruct((tp + ts, Z_W), bf16),
        compiler_params=pltpu.CompilerParams(
            dimension_semantics=("arbitrary",), vmem_limit_bytes=VMEM_LIMIT),
        name="in_proj",
    )(xp, xs, g, w)


def _gated_chunk(q, k, v, log_a, states, *, dk, scale, ref_row, tri, causal, eye):
    c = q.shape[0]
    b = _cumsum_rows(tri, log_a)
    b_ref = b[ref_row:ref_row + 1, :]
    b_last = b[c - 1:c, :]
    qs = q * scale
    q_r = (qs * jnp.exp(b - b_ref)).astype(bf16)
    k_r = (k * jnp.exp(b_ref - b)).astype(bf16)
    k_end = (k * jnp.exp(b_last - b)).astype(bf16)
    q_b = (qs * jnp.exp(b)).astype(bf16)
    decay = jnp.exp(b_last)
    vb = v.astype(bf16)
    dv = v.shape[1] // len(states)
    outs, new_states = [], []
    for h, s in enumerate(states):
        ks = slice(h * dk, (h + 1) * dk)
        vs = slice(h * dv, (h + 1) * dv)
        att = _dot_nt(q_r[:, ks], k_r[:, ks])
        att = jnp.where(causal, att, 0.0).astype(bf16)
        o = _dot(att, vb[:, vs]) + _dot(q_b[:, ks], s.astype(bf16))
        u = _dot_tn(k_end[:, ks], vb[:, vs])
        d_col = jnp.sum(jnp.where(eye, decay[:, ks], 0.0), axis=1, keepdims=True)
        outs.append(o)
        new_states.append(d_col * s + u)
    return outs, new_states


def _head_norm(o, g):
    return o * lax.rsqrt(jnp.mean(o * o, axis=-1, keepdims=True) + EPS) * g


def _mixer_chunk(zc, sg, sh, wgk, bgk, gng, lb, gnh, *, ref_row, tri, causal, eye_g, eye_h, valid):
    z = zc.astype(f32)
    gk = _dot(zc[:, Z_R1:Z_R1 + R1_PAD], wgk) + bgk
    log_a = (jnp.minimum(gk, 0.0) - jnp.log(1.0 + jnp.exp(-jnp.abs(gk)))) / GLA_NORMALIZER
    k1 = z[:, Z_K1:Z_K1 + GLA_KDIM]
    if valid is not None:
        log_a = jnp.where(valid, log_a, 0.0)
        k1 = jnp.where(valid, k1, 0.0)
    o1, sg_new = _gated_chunk(
        z[:, Z_Q1:Z_Q1 + GLA_KDIM], k1, z[:, Z_V1:Z_V1 + GLA_WIDTH], log_a, sg,
        dk=GLA_DK, scale=GLA_DK ** -0.5, ref_row=ref_row, tri=tri, causal=causal, eye=eye_g)
    forget = lb + (1.0 - lb) * _sigmoid(z[:, Z_F2:Z_F2 + HGRN_FDIM])
    k2 = 1.0 - forget
    log_f = jnp.log(forget)
    if valid is not None:
        log_f = jnp.where(valid, log_f, 0.0)
        k2 = jnp.where(valid, k2, 0.0)
    q2 = z[:, Z_Q2:Z_Q2 + HGRN_FDIM]
    i2 = z[:, Z_I2:Z_I2 + HGRN_WIDTH]
    o2, sh_new = _gated_chunk(
        q2 * _sigmoid(q2), k2, i2 * _sigmoid(i2), log_f, sh,
        dk=HGRN_DK, scale=1.0, ref_row=ref_row, tri=tri, causal=causal, eye=eye_h)
    cols = []
    for h in range(GLA_HEADS):
        g1 = z[:, Z_G1 + h * GLA_DV:Z_G1 + (h + 1) * GLA_DV]
        cols.append(_head_norm(o1[h], gng) * (g1 * _sigmoid(g1)))
    for h in range(HGRN_HEADS):
        g2 = z[:, Z_G2 + h * HGRN_DV:Z_G2 + (h + 1) * HGRN_DV]
        cols.append(_head_norm(o2[h], gnh) * _sigmoid(g2))
    return jnp.concatenate(cols, axis=1), sg_new, sh_new


def _chunk_consts(c):
    row = lax.broadcasted_iota(i32, (c, c), 0)
    col = lax.broadcasted_iota(i32, (c, c), 1)
    causal = row >= col
    tri = causal.astype(bf16)
    eye_g = (lax.broadcasted_iota(i32, (GLA_DK, GLA_DK), 0)
             == lax.broadcasted_iota(i32, (GLA_DK, GLA_DK), 1))
    eye_h = (lax.broadcasted_iota(i32, (HGRN_DK, HGRN_DK), 0)
             == lax.broadcasted_iota(i32, (HGRN_DK, HGRN_DK), 1))
    return tri, causal, eye_g, eye_h


def _lower_bound(lbl_ref):
    l = lbl_ref[...]
    m = jnp.max(l, axis=0, keepdims=True)
    e = jnp.exp(l - m)
    return e[0:1, :] / jnp.sum(e, axis=0, keepdims=True)


def _mixer_prompt_kernel(z_ref, wgk_ref, bgk_ref, gng_ref, lbl_ref, gnh_ref,
                         o_ref, sg_out, sh_out, sg, sh, *, chunks):
    j = pl.program_id(1)

    @pl.when(j == 0)
    def _():
        sg[...] = jnp.zeros_like(sg)
        sh[...] = jnp.zeros_like(sh)

    tri, causal, eye_g, eye_h = _chunk_consts(CHUNK)
    lb = _lower_bound(lbl_ref)
    wgk, bgk, gng, gnh = wgk_ref[...], bgk_ref[...], gng_ref[...], gnh_ref[...]

    def body(c, carry):
        r0 = pl.multiple_of(c * CHUNK, CHUNK)
        zc = z_ref[pl.ds(r0, CHUNK), :]
        o, sg_new, sh_new = _mixer_chunk(
            zc, [sg[h] for h in range(GLA_HEADS)], [sh[h] for h in range(HGRN_HEADS)],
            wgk, bgk, gng, lb, gnh, ref_row=CHUNK // 2, tri=tri, causal=causal,
            eye_g=eye_g, eye_h=eye_h, valid=None)
        o_ref[pl.ds(r0, CHUNK), :] = o.astype(bf16)
        for h in range(GLA_HEADS):
            sg[h] = sg_new[h]
        for h in range(HGRN_HEADS):
            sh[h] = sh_new[h]
        return carry

    lax.fori_loop(0, chunks, body, 0)

    @pl.when(j == pl.num_programs(1) - 1)
    def _():
        sg_out[0, 0] = sg[...]
        sh_out[0, 0] = sh[...]


def _mixer_prompt(z, wgk, bgk, gng, lbl, gnh, *, batch, seq, rows_per_step):
    nj = seq // rows_per_step
    full = lambda shape: pl.BlockSpec(shape, lambda b, j: (0,) * len(shape))
    return pl.pallas_call(
        functools.partial(_mixer_prompt_kernel, chunks=rows_per_step // CHUNK),
        grid=(batch, nj),
        in_specs=[
            pl.BlockSpec((rows_per_step, Z_W), lambda b, j: (b * nj + j, 0)),
            full((R1_PAD, GLA_KDIM)), full((1, GLA_KDIM)), full((1, GLA_DV)),
            full((2, HGRN_FDIM)), full((1, HGRN_DV)),
        ],
        out_specs=[
            pl.BlockSpec((rows_per_step, D_MODEL), lambda b, j: (b * nj + j, 0)),
            pl.BlockSpec((1, 1, GLA_HEADS, GLA_DK, GLA_DV), lambda b, j: (0, b, 0, 0, 0)),
            pl.BlockSpec((1, 1, HGRN_HEADS, HGRN_DK, HGRN_DV), lambda b, j: (0, b, 0, 0, 0)),
        ],
        out_shape=[
            jax.ShapeDtypeStruct((batch * seq, D_MODEL), bf16),
            jax.ShapeDtypeStruct((1, batch, GLA_HEADS, GLA_DK, GLA_DV), f32),
            jax.ShapeDtypeStruct((1, batch, HGRN_HEADS, HGRN_DK, HGRN_DV), f32),
        ],
        scratch_shapes=[
            pltpu.VMEM((GLA_HEADS, GLA_DK, GLA_DV), f32),
            pltpu.VMEM((HGRN_HEADS, HGRN_DK, HGRN_DV), f32),
        ],
        compiler_params=pltpu.CompilerParams(
            dimension_semantics=("arbitrary", "arbitrary"), vmem_limit_bytes=VMEM_LIMIT),
        name="mixer_prompt",
    )(z, wgk, bgk, gng, lbl, gnh)


def _mixer_sample_kernel(z_ref, sg_in, sh_in, wgk_ref, bgk_ref, gng_ref, lbl_ref, gnh_ref,
                         o_ref, sg_out, sh_out, *, batch_block, steps):
    tri, causal, eye_g, eye_h = _chunk_consts(SAMPLE_PAD_T)
    valid = lax.broadcasted_iota(i32, (SAMPLE_PAD_T, 1), 0) < steps
    lb = _lower_bound(lbl_ref)
    wgk, bgk, gng, gnh = wgk_ref[...], bgk_ref[...], gng_ref[...], gnh_ref[...]

    def body(i, carry):
        o, sg_new, sh_new = _mixer_chunk(
            z_ref[i], [sg_in[0, i, h] for h in range(GLA_HEADS)],
            [sh_in[0, i, h] for h in range(HGRN_HEADS)],
            wgk, bgk, gng, lb, gnh, ref_row=steps // 2, tri=tri, causal=causal,
            eye_g=eye_g, eye_h=eye_h, valid=valid)
        o_ref[i] = o
        for h in range(GLA_HEADS):
            sg_out[0, i, h] = sg_new[h]
        for h in range(HGRN_HEADS):
            sh_out[0, i, h] = sh_new[h]
        return carry

    lax.fori_loop(0, batch_block, body, 0)


def _mixer_sample(z3, state_gla, state_hgrn, wgk, bgk, gng, lbl, gnh, *, steps, batch_block):
    batch = z3.shape[0]
    full = lambda shape: pl.BlockSpec(shape, lambda i: (0,) * len(shape))
    sg_spec = pl.BlockSpec((1, batch_block, GLA_HEADS, GLA_DK, GLA_DV), lambda i: (0, i, 0, 0, 0))
    sh_spec = pl.BlockSpec((1, batch_block, HGRN_HEADS, HGRN_DK, HGRN_DV), lambda i: (0, i, 0, 0, 0))
    return pl.pallas_call(
        functools.partial(_mixer_sample_kernel, batch_block=batch_block, steps=steps),
        grid=(batch // batch_block,),
        in_specs=[
            pl.BlockSpec((batch_block, SAMPLE_PAD_T, Z_W), lambda i: (i, 0, 0)),
            sg_spec, sh_spec,
            full((R1_PAD, GLA_KDIM)), full((1, GLA_KDIM)), full((1, GLA_DV)),
            full((2, HGRN_FDIM)), full((1, HGRN_DV)),
        ],
        out_specs=[
            pl.BlockSpec((batch_block, SAMPLE_PAD_T, D_MODEL), lambda i: (i, 0, 0)),
            sg_spec, sh_spec,
        ],
        out_shape=[
            jax.ShapeDtypeStruct((batch, SAMPLE_PAD_T, D_MODEL), f32),
            jax.ShapeDtypeStruct(state_gla.shape, f32),
            jax.ShapeDtypeStruct(state_hgrn.shape, f32),
        ],
        compiler_params=pltpu.CompilerParams(
            dimension_semantics=("arbitrary",), vmem_limit_bytes=VMEM_LIMIT),
        name="mixer_sample",
    )(z3, state_gla, state_hgrn, wgk, bgk, gng, lbl, gnh)


def _store_slabs(ref, x):
    for s in range(SLABS):
        ref[:, s, :] = x[:, s * LANES:(s + 1) * LANES]


def _outproj_kernel(xp_ref, xs_ref, o_ref, wo_ref, g_ref, wr_ref, br_ref,
                    h_ref, xn_ref, ti_ref, tg_ref, rank_ref, cnt_ref, run_cnt, *, n_prompt_tiles):
    i = pl.program_id(0)

    @pl.when(i == 0)
    def _():
        run_cnt[...] = jnp.zeros_like(run_cnt)

    x = _select_rows(i < n_prompt_tiles, xp_ref, xs_ref)
    h = x + _dot(o_ref[...], wo_ref[...])
    h_ref[...] = h
    xn = _rms(h, g_ref[...])
    _store_slabs(xn_ref, xn)
    logits = _dot(xn.astype(bf16), wr_ref[...]) + br_ref[...]
    lane = lax.broadcasted_iota(i32, logits.shape, 1)
    neg = jnp.float32(-jnp.inf)
    l = jnp.where(lane < N_EXPERTS, logits, neg)
    vals, idxs = [], []
    for _ in range(TOP_K):
        m = jnp.max(l, axis=-1, keepdims=True)
        idx = jnp.min(jnp.where(l == m, lane, LANES), axis=-1, keepdims=True)
        vals.append(m)
        idxs.append(idx)
        l = jnp.where(lane == idx, neg, l)
    es = [jnp.exp(v - vals[0]) for v in vals]
    denom = es[0] + es[1] + es[2] + es[3]

    onehot = jnp.zeros(logits.shape, f32)
    for k in range(TOP_K):
        onehot = onehot + (lane == idxs[k]).astype(f32)
    rows = logits.shape[0]
    lower = (lax.broadcasted_iota(i32, (rows, rows), 0)
             > lax.broadcasted_iota(i32, (rows, rows), 1)).astype(bf16)
    ahead = _dot(lower, onehot.astype(bf16)) + run_cnt[...]
    run_cnt[...] = run_cnt[...] + jnp.sum(onehot, axis=0, keepdims=True)
    cnt_ref[...] = run_cnt[...].astype(i32)

    ti = jnp.zeros(logits.shape, i32)
    tg = jnp.zeros(logits.shape, f32)
    rank = jnp.zeros(logits.shape, i32)
    for k in range(TOP_K):
        rk = jnp.sum(jnp.where(lane == idxs[k], ahead, 0.0), axis=-1, keepdims=True).astype(i32)
        ti = jnp.where(lane == k, idxs[k], ti)
        tg = jnp.where(lane == k, es[k] / denom, tg)
        rank = jnp.where(lane == k, rk, rank)
    ti_ref[...] = ti
    tg_ref[...] = tg
    rank_ref[...] = rank


def _out_proj(xp, xs, o, wo, g, wr, br):
    tp, ts = xp.shape[0], xs.shape[0]
    npt = tp // TOKEN_TILE
    nt = npt + ts // TOKEN_TILE
    t = tp + ts
    row = lambda w: pl.BlockSpec((TOKEN_TILE, w), lambda i: (i, 0))
    full = lambda shape: pl.BlockSpec(shape, lambda i: (0,) * len(shape))
    return pl.pallas_call(
        functools.partial(_outproj_kernel, n_prompt_tiles=npt),
        grid=(nt,),
        in_specs=[
            pl.BlockSpec((TOKEN_TILE, D_MODEL), lambda i: (jnp.minimum(i, npt - 1), 0)),
            pl.BlockSpec((TOKEN_TILE, D_MODEL), lambda i: (jnp.maximum(i - npt, 0), 0)),
            row(D_MODEL), full((D_MODEL, D_MODEL)), full((1, D_MODEL)),
            full((D_MODEL, LANES)), full((1, LANES)),
        ],
        out_specs=[row(D_MODEL), pl.BlockSpec((TOKEN_TILE, SLABS, LANES), lambda i: (i, 0, 0)),
                   row(LANES), row(LANES), row(LANES), full((1, LANES))],
        out_shape=[
            jax.ShapeDtypeStruct((t, D_MODEL), f32),
            jax.ShapeDtypeStruct((t, SLABS, LANES), f32),
            jax.ShapeDtypeStruct((t, LANES), i32),
            jax.ShapeDtypeStruct((t, LANES), f32),
            jax.ShapeDtypeStruct((t, LANES), i32),
            jax.ShapeDtypeStruct((1, LANES), i32),
        ],
        scratch_shapes=[pltpu.VMEM((1, LANES), f32)],
        compiler_params=pltpu.CompilerParams(
            dimension_semantics=("arbitrary",), vmem_limit_bytes=VMEM_LIMIT),
        name="out_proj_router",
    )(xp, xs, o, wo, g, wr, br)


def _dispatch_kernel(bv_ref, nz_ref, pos_ref, xn_ref, xs_hbm, zeros, sem, zsem, *, n_blocks):
    @pl.when(pl.program_id(0) == 0)
    def _():
        zeros[...] = jnp.zeros_like(zeros)

        def fill(b, carry):
            @pl.when(bv_ref[b] < MOE_TILE)
            def _():
                r0 = pl.multiple_of(b * MOE_TILE, MOE_TILE)
                pltpu.make_async_copy(zeros, xs_hbm.at[pl.ds(r0, MOE_TILE)], zsem).start()
            return carry

        def drain(b, carry):
            pltpu.make_async_copy(zeros, xs_hbm.at[pl.ds(0, MOE_TILE)], zsem).wait()
            return carry

        lax.fori_loop(0, n_blocks, fill, 0)
        lax.fori_loop(0, nz_ref[0], drain, 0)

    def start(r, carry):
        for k in range(TOP_K):
            pltpu.make_async_copy(xn_ref.at[r], xs_hbm.at[pos_ref[0, 0, r * TOP_K + k]], sem).start()
        return carry

    lax.fori_loop(0, DISPATCH_TILE, start, 0, unroll=2)
    for _ in range(TOP_K):
        pltpu.make_async_copy(xn_ref, xs_hbm.at[pl.ds(0, DISPATCH_TILE)], sem).wait()


def _dispatch(block_valid, pos, xn3):
    t = xn3.shape[0]
    nt = t // DISPATCH_TILE
    n_blocks = block_valid.shape[0]
    n_zero = jnp.sum((block_valid < MOE_TILE).astype(i32)).reshape(1)
    grid_spec = pltpu.PrefetchScalarGridSpec(
        num_scalar_prefetch=2,
        grid=(nt,),
        in_specs=[
            pl.BlockSpec((1, 1, DISPATCH_TILE * TOP_K), lambda i, bv, nz: (i, 0, 0),
                         memory_space=pltpu.SMEM),
            pl.BlockSpec((DISPATCH_TILE, SLABS, LANES), lambda i, bv, nz: (i, 0, 0)),
        ],
        out_specs=pl.BlockSpec(memory_space=pl.ANY),
        scratch_shapes=[
            pltpu.VMEM((MOE_TILE, SLABS, LANES), f32),
            pltpu.SemaphoreType.DMA(()),
            pltpu.SemaphoreType.DMA(()),
        ],
    )
    return pl.pallas_call(
        functools.partial(_dispatch_kernel, n_blocks=n_blocks),
        grid_spec=grid_spec,
        out_shape=jax.ShapeDtypeStruct((n_blocks * MOE_TILE, SLABS, LANES), f32),
        compiler_params=pltpu.CompilerParams(
            dimension_semantics=("arbitrary",), vmem_limit_bytes=VMEM_LIMIT),
        name="moe_dispatch",
    )(block_valid, n_zero, pos.reshape(nt, 1, DISPATCH_TILE * TOP_K), xn3)


def _expert_kernel(be_ref, nb_ref, x_ref, wgu_ref, bgu_ref, wdn_ref, bdn_ref, y_ref, *, ff_chunk):
    i = pl.program_id(0)

    @pl.when(i >= nb_ref[0])
    def _():
        y_ref[...] = jnp.zeros_like(y_ref)

    @pl.when(i < nb_ref[0])
    def _():
        xb = jnp.concatenate([x_ref[:, s, :].astype(bf16) for s in range(SLABS)], axis=1)
        acc = jnp.zeros((MOE_TILE, D_MODEL), f32)
        for c0 in range(0, D_FF, ff_chunk):
            gate = _dot(xb, wgu_ref[0, :, c0:c0 + ff_chunk]) + bgu_ref[0, :, c0:c0 + ff_chunk]
            up = (_dot(xb, wgu_ref[0, :, D_FF + c0:D_FF + c0 + ff_chunk])
                  + bgu_ref[0, :, D_FF + c0:D_FF + c0 + ff_chunk])
            gate = jnp.minimum(gate, SWIGLU_LIMIT)
            up = jnp.clip(up, -SWIGLU_LIMIT, SWIGLU_LIMIT)
            hid = (up + 1.0) * (gate * _sigmoid(gate * SWIGLU_ALPHA))
            acc = acc + _dot(hid.astype(bf16), wdn_ref[0, c0:c0 + ff_chunk, :])
        _store_slabs(y_ref, acc + bdn_ref[0])


def _experts(block_e, n_used, xs, wgu, bgu, wdn, bdn):
    n_blocks = block_e.shape[0]
    grid_spec = pltpu.PrefetchScalarGridSpec(
        num_scalar_prefetch=2,
        grid=(n_blocks,),
        in_specs=[
            pl.BlockSpec((MOE_TILE, SLABS, LANES), lambda i, be, nb: (jnp.minimum(i, nb[0] - 1), 0, 0)),
            pl.BlockSpec((1, D_MODEL, 2 * D_FF), lambda i, be, nb: (be[i], 0, 0)),
            pl.BlockSpec((1, 1, 2 * D_FF), lambda i, be, nb: (be[i], 0, 0)),
            pl.BlockSpec((1, D_FF, D_MODEL), lambda i, be, nb: (be[i], 0, 0)),
            pl.BlockSpec((1, 1, D_MODEL), lambda i, be, nb: (be[i], 0, 0)),
        ],
        out_specs=pl.BlockSpec((MOE_TILE, SLABS, LANES), lambda i, be, nb: (i, 0, 0)),
    )
    return pl.pallas_call(
        functools.partial(_expert_kernel, ff_chunk=256),
        grid_spec=grid_spec,
        out_shape=jax.ShapeDtypeStruct((n_blocks * MOE_TILE, SLABS, LANES), f32),
        compiler_params=pltpu.CompilerParams(
            dimension_semantics=("arbitrary",), vmem_limit_bytes=VMEM_LIMIT),
        name="moe_experts",
    )(block_e, n_used, xs, wgu, bgu, wdn, bdn)


def _combine_kernel(pos_cur, pos_nxt, ys_hbm, h_ref, tg_ref, pp_ref, ps_ref, gple_ref, wpg_ref, wpp_ref,
                    gfin_ref, y_ref, rows, sems, *, n_prompt_tiles):
    i = pl.program_id(0)
    nt = pl.num_programs(0)
    n = COMBINE_TILE * TOP_K

    def gather(pos_ref, slot):
        def start(r, carry):
            pltpu.make_async_copy(ys_hbm.at[pos_ref[0, 0, r]], rows.at[slot, r], sems.at[slot]).start()
            return carry
        lax.fori_loop(0, n, start, 0, unroll=8)

    @pl.when(i == 0)
    def _():
        gather(pos_cur, 0)

    @pl.when(i + 1 < nt)
    def _():
        gather(pos_nxt, (i + 1) % 2)

    slot = i % 2
    pltpu.make_async_copy(ys_hbm.at[pl.ds(0, n)], rows.at[slot], sems.at[slot]).wait()

    tg = tg_ref[...]
    cols = []
    for s in range(SLABS):
        moe = jnp.zeros((COMBINE_TILE, LANES), f32)
        for k in range(TOP_K):
            moe = moe + tg[:, k:k + 1] * rows[slot, pl.ds(k * COMBINE_TILE, COMBINE_TILE), s, :]
        cols.append(h_ref[:, s * LANES:(s + 1) * LANES] + moe)
    h = jnp.concatenate(cols, axis=1)
    gate = _sigmoid(_dot(_rms(h, gple_ref[...]).astype(bf16), wpg_ref[...]))
    p = _select_rows(i < n_prompt_tiles, pp_ref, ps_ref).astype(bf16)
    h = h + gate * _dot(p, wpp_ref[...])
    y_ref[...] = _rms(h, gfin_ref[...])


def _combine(pos, ys, h1, tg, pp, ps, gple, wpg, wpp, gfin):
    t = h1.shape[0]
    nt = t // COMBINE_TILE
    npt = pp.shape[0] // COMBINE_TILE
    n = COMBINE_TILE * TOP_K
    pos_tiles = pos.reshape(nt, COMBINE_TILE, TOP_K).transpose(0, 2, 1).reshape(nt, 1, n)
    row = lambda w: pl.BlockSpec((COMBINE_TILE, w), lambda i: (i, 0))
    full = lambda shape: pl.BlockSpec(shape, lambda i: (0,) * len(shape))
    return pl.pallas_call(
        functools.partial(_combine_kernel, n_prompt_tiles=npt),
        grid=(nt,),
        in_specs=[
            pl.BlockSpec((1, 1, n), lambda i: (i, 0, 0), memory_space=pltpu.SMEM),
            pl.BlockSpec((1, 1, n), lambda i: (jnp.minimum(i + 1, nt - 1), 0, 0), memory_space=pltpu.SMEM),
            pl.BlockSpec(memory_space=pl.ANY),
            row(D_MODEL), row(LANES),
            pl.BlockSpec((COMBINE_TILE, PLE_DIM), lambda i: (jnp.minimum(i, npt - 1), 0)),
            pl.BlockSpec((COMBINE_TILE, PLE_DIM), lambda i: (jnp.maximum(i - npt, 0), 0)),
            full((1, D_MODEL)), full((D_MODEL, D_MODEL)), full((PLE_DIM, D_MODEL)),
            full((1, D_MODEL)),
        ],
        out_specs=row(D_MODEL),
        out_shape=jax.ShapeDtypeStruct((t, D_MODEL), f32),
        scratch_shapes=[
            pltpu.VMEM((2, n, SLABS, LANES), f32),
            pltpu.SemaphoreType.DMA((2,)),
        ],
        compiler_params=pltpu.CompilerParams(
            dimension_semantics=("arbitrary",), vmem_limit_bytes=VMEM_LIMIT),
        name="moe_combine_ple",
    )(pos_tiles, pos_tiles, ys, h1, tg, pp, ps, gple, wpg, wpp, gfin)


def _routing_tables(top_i, rank, counts, n_blocks):
    padded = (counts + MOE_TILE - 1) // MOE_TILE * MOE_TILE
    pend = jnp.cumsum(padded)
    pstart = pend - padded
    block_rows = jnp.arange(n_blocks, dtype=i32) * MOE_TILE
    block_e = jnp.minimum(
        jnp.sum((block_rows[:, None] >= pend[None, :]).astype(i32), axis=1), N_EXPERTS - 1)
    block_valid = jnp.clip(pstart[block_e] + counts[block_e] - block_rows, 0, MOE_TILE)
    n_used = pend[-1:] // MOE_TILE
    onehot = top_i[:, :, None] == jnp.arange(N_EXPERTS, dtype=i32)[None, None, :]
    pos = rank + jnp.sum(jnp.where(onehot, pstart[None, None, :], 0), axis=-1)
    return pos.astype(i32), block_e.astype(i32), block_valid.astype(i32), n_used.astype(i32)


def kernel(x_prompt, x_sample, p_prompt, p_sample, state_gla, state_hgrn, g_mix, w_in, w_gk2, b_gk2,
           gn_gla, lb_logits, gn_hgrn, w_out, g_ffn, w_router, b_router, w_gu, b_gu, w_dn, b_dn,
           g_ple, w_ple_gate, w_ple_proj, g_final):
    batch, seq, _ = x_prompt.shape
    dec_batch, dec_seq, _ = x_sample.shape
    tp, ts = batch * seq, dec_batch * dec_seq
    xp = x_prompt.reshape(tp, D_MODEL)
    xs = x_sample.reshape(ts, D_MODEL)

    r1_end = Z_R1 + GLA_LOWRANK
    w_in_p = jnp.concatenate(
        [w_in[0, :, :r1_end], jnp.zeros((D_MODEL, R1_PAD - GLA_LOWRANK), f32), w_in[0, :, r1_end:]],
        axis=1).astype(bf16)
    wgk = jnp.concatenate(
        [w_gk2[0], jnp.zeros((R1_PAD - GLA_LOWRANK, GLA_KDIM), f32)], axis=0).astype(bf16)
    wr = jnp.concatenate(
        [w_router[0], jnp.zeros((D_MODEL, LANES - N_EXPERTS), f32)], axis=1).astype(bf16)
    br = jnp.concatenate([b_router[0], jnp.zeros((LANES - N_EXPERTS,), f32)]).reshape(1, LANES)

    z = _in_proj(xp, xs, g_mix[0].reshape(1, D_MODEL), w_in_p)

    mixer_params = (wgk, b_gk2[0].reshape(1, GLA_KDIM), gn_gla[0].reshape(1, GLA_DV),
                    lb_logits[0:2], gn_hgrn[0].reshape(1, HGRN_DV))
    o_p, new_gla_p, new_hgrn_p = _mixer_prompt(z, *mixer_params, batch=batch, seq=seq, rows_per_step=512)
    z_s = jnp.pad(z[tp:].reshape(dec_batch, dec_seq, Z_W), ((0, 0), (0, SAMPLE_PAD_T - dec_seq), (0, 0)))
    o_s, new_gla_s, new_hgrn_s = _mixer_sample(
        z_s, state_gla, state_hgrn, *mixer_params, steps=dec_seq, batch_block=8)
    o = jnp.concatenate([o_p, o_s[:, :dec_seq].reshape(ts, D_MODEL).astype(bf16)], axis=0)

    h1, xn3, top_i, top_g, rank, counts = _out_proj(
        xp, xs, o, w_out[0].astype(bf16), g_ffn[0].reshape(1, D_MODEL), wr, br)

    t = tp + ts
    n_blocks = -(-(t * TOP_K) // MOE_TILE) + N_EXPERTS
    pos, block_e, block_valid, n_used = _routing_tables(
        top_i[:, :TOP_K], rank[:, :TOP_K], counts[0, :N_EXPERTS], n_blocks)
    x_sorted = _dispatch(block_valid, pos, xn3)
    y_sorted = _experts(block_e, n_used, x_sorted,
                        w_gu[0].astype(bf16), b_gu[0].reshape(N_EXPERTS, 1, 2 * D_FF),
                        w_dn[0].astype(bf16), b_dn[0].reshape(N_EXPERTS, 1, D_MODEL))

    y = _combine(pos, y_sorted, h1, top_g, p_prompt[0].reshape(tp, PLE_DIM),
                 p_sample[0].reshape(ts, PLE_DIM), g_ple[0].reshape(1, D_MODEL),
                 w_ple_gate[0].astype(bf16), w_ple_proj[0].astype(bf16), g_final.reshape(1, D_MODEL))

    return (y[:tp].reshape(batch, seq, D_MODEL), y[tp:].reshape(dec_batch, dec_seq, D_MODEL),
            new_gla_p, new_hgrn_p, new_gla_s, new_hgrn_s)
```

```python
import functools

import jax
import jax.numpy as jnp
from jax import lax
from jax.experimental import pallas as pl
from jax.experimental.pallas import tpu as pltpu

f32 = jnp.float32
bf16 = jnp.bfloat16
i32 = jnp.int32

D_MODEL = 1024
GLA_HEADS = 4
GLA_DK = 64
GLA_DV = 128
GLA_KDIM = GLA_HEADS * GLA_DK
GLA_WIDTH = GLA_HEADS * GLA_DV
GLA_LOWRANK = 16
GLA_NORMALIZER = 16.0
HGRN_HEADS = 4
HGRN_DK = 128
HGRN_DV = 128
HGRN_FDIM = HGRN_HEADS * HGRN_DK
HGRN_WIDTH = HGRN_HEADS * HGRN_DV
CHUNK = 64
N_EXPERTS = 32
TOP_K = 4
D_FF = 1024
SWIGLU_LIMIT = 7.0
SWIGLU_ALPHA = 1.702
PLE_DIM = 256
EPS = 1e-6

LANES = 128
SAMPLE_PAD_T = 8
SLABS = D_MODEL // LANES

R1_PAD = LANES
Z_Q1 = 0
Z_K1 = Z_Q1 + GLA_KDIM
Z_V1 = Z_K1 + GLA_KDIM
Z_G1 = Z_V1 + GLA_WIDTH
Z_R1 = Z_G1 + GLA_WIDTH
Z_Q2 = Z_R1 + R1_PAD
Z_F2 = Z_Q2 + HGRN_FDIM
Z_I2 = Z_F2 + HGRN_FDIM
Z_G2 = Z_I2 + HGRN_WIDTH
Z_W = Z_G2 + HGRN_WIDTH

TOKEN_TILE = 512
MOE_TILE = 256
DISPATCH_TILE = 256
COMBINE_TILE = 256
VMEM_LIMIT = 56 * 1024 * 1024


def _rms(x, g):
    return x * lax.rsqrt(jnp.mean(x * x, axis=-1, keepdims=True) + EPS) * g


def _sigmoid(x):
    return 1.0 / (1.0 + jnp.exp(-x))


def _dot(a, b):
    return jnp.dot(a, b, preferred_element_type=f32)


def _dot_nt(a, b):
    return lax.dot_general(a, b, (((1,), (1,)), ((), ())), preferred_element_type=f32)


def _dot_tn(a, b):
    return lax.dot_general(a, b, (((0,), (0,)), ((), ())), preferred_element_type=f32)


def _cumsum_rows(tri, x):
    hi = x.astype(bf16)
    r1 = x - hi.astype(f32)
    mid = r1.astype(bf16)
    lo = (r1 - mid.astype(f32)).astype(bf16)
    return _dot(tri, hi) + _dot(tri, mid) + _dot(tri, lo)


def _select_rows(is_prompt, a_ref, b_ref):
    return jnp.where(is_prompt, a_ref[...], b_ref[...])


def _inproj_kernel(xp_ref, xs_ref, g_ref, w_ref, z_ref, *, n_prompt_tiles, col_chunk):
    is_prompt = pl.program_id(0) < n_prompt_tiles
    x = _select_rows(is_prompt, xp_ref, xs_ref)
    a = _rms(x, g_ref[...]).astype(bf16)
    for c0 in range(0, Z_W, col_chunk):
        c1 = min(c0 + col_chunk, Z_W)
        z_ref[:, c0:c1] = _dot(a, w_ref[:, c0:c1]).astype(bf16)


def _in_proj(xp, xs, g, w):
    tp, ts = xp.shape[0], xs.shape[0]
    npt = tp // TOKEN_TILE
    nt = npt + ts // TOKEN_TILE
    return pl.pallas_call(
        functools.partial(_inproj_kernel, n_prompt_tiles=npt, col_chunk=512),
        grid=(nt,),
        in_specs=[
            pl.BlockSpec((TOKEN_TILE, D_MODEL), lambda i: (jnp.minimum(i, npt - 1), 0)),
            pl.BlockSpec((TOKEN_TILE, D_MODEL), lambda i: (jnp.maximum(i - npt, 0), 0)),
            pl.BlockSpec((1, D_MODEL), lambda i: (0, 0)),
            pl.BlockSpec((D_MODEL, Z_W), lambda i: (0, 0)),
        ],
        out_specs=pl.BlockSpec((TOKEN_TILE, Z_W), lambda i: (i, 0)),
        out_shape=jax.ShapeDtypeStruct((tp + ts, Z_W), bf16),
        compiler_params=pltpu.CompilerParams(
            dimension_semantics=("arbitrary",), vmem_limit_bytes=VMEM_LIMIT),
        name="in_proj",
    )(xp, xs, g, w)


def _gated_chunk(q, k, v, log_a, states, *, dk, scale, ref_row, tri, causal, eye):
    c = q.shape[0]
    b = _cumsum_rows(tri, log_a)
    b_ref = b[ref_row:ref_row + 1, :]
    b_last = b[c - 1:c, :]
    qs = q * scale
    q_r = (qs * jnp.exp(b - b_ref)).astype(bf16)
    k_r = (k * jnp.exp(b_ref - b)).astype(bf16)
    k_end = (k * jnp.exp(b_last - b)).astype(bf16)
    q_b = (qs * jnp.exp(b)).astype(bf16)
    decay = jnp.exp(b_last)
    vb = v.astype(bf16)
    dv = v.shape[1] // len(states)
    outs, new_states = [], []
    for h, s in enumerate(states):
        ks = slice(h * dk, (h + 1) * dk)
        vs = slice(h * dv, (h + 1) * dv)
        att = _dot_nt(q_r[:, ks], k_r[:, ks])
        att = jnp.where(causal, att, 0.0).astype(bf16)
        o = _dot(att, vb[:, vs]) + _dot(q_b[:, ks], s.astype(bf16))
        u = _dot_tn(k_end[:, ks], vb[:, vs])
        d_col = jnp.sum(jnp.where(eye, decay[:, ks], 0.0), axis=1, keepdims=True)
        outs.append(o)
        new_states.append(d_col * s + u)
    return outs, new_states


def _head_norm(o, g):
    return o * lax.rsqrt(jnp.mean(o * o, axis=-1, keepdims=True) + EPS) * g


def _mixer_chunk(zc, sg, sh, wgk, bgk, gng, lb, gnh, *, ref_row, tri, causal, eye_g, eye_h, valid):
    z = zc.astype(f32)
    gk = _dot(zc[:, Z_R1:Z_R1 + R1_PAD], wgk) + bgk
    log_a = (jnp.minimum(gk, 0.0) - jnp.log(1.0 + jnp.exp(-jnp.abs(gk)))) / GLA_NORMALIZER
    k1 = z[:, Z_K1:Z_K1 + GLA_KDIM]
    if valid is not None:
        log_a = jnp.where(valid, log_a, 0.0)
        k1 = jnp.where(valid, k1, 0.0)
    o1, sg_new = _gated_chunk(
        z[:, Z_Q1:Z_Q1 + GLA_KDIM], k1, z[:, Z_V1:Z_V1 + GLA_WIDTH], log_a, sg,
        dk=GLA_DK, scale=GLA_DK ** -0.5, ref_row=ref_row, tri=tri, causal=causal, eye=eye_g)
    forget = lb + (1.0 - lb) * _sigmoid(z[:, Z_F2:Z_F2 + HGRN_FDIM])
    k2 = 1.0 - forget
    log_f = jnp.log(forget)
    if valid is not None:
        log_f = jnp.where(valid, log_f, 0.0)
        k2 = jnp.where(valid, k2, 0.0)
    q2 = z[:, Z_Q2:Z_Q2 + HGRN_FDIM]
    i2 = z[:, Z_I2:Z_I2 + HGRN_WIDTH]
    o2, sh_new = _gated_chunk(
        q2 * _sigmoid(q2), k2, i2 * _sigmoid(i2), log_f, sh,
        dk=HGRN_DK, scale=1.0, ref_row=ref_row, tri=tri, causal=causal, eye=eye_h)
    cols = []
    for h in range(GLA_HEADS):
        g1 = z[:, Z_G1 + h * GLA_DV:Z_G1 + (h + 1) * GLA_DV]
        cols.append(_head_norm(o1[h], gng) * (g1 * _sigmoid(g1)))
    for h in range(HGRN_HEADS):
        g2 = z[:, Z_G2 + h * HGRN_DV:Z_G2 + (h + 1) * HGRN_DV]
        cols.append(_head_norm(o2[h], gnh) * _sigmoid(g2))
    return jnp.concatenate(cols, axis=1), sg_new, sh_new


def _chunk_consts(c):
    row = lax.broadcasted_iota(i32, (c, c), 0)
    col = lax.broadcasted_iota(i32, (c, c), 1)
    causal = row >= col
    tri = causal.astype(bf16)
    eye_g = (lax.broadcasted_iota(i32, (GLA_DK, GLA_DK), 0)
             == lax.broadcasted_iota(i32, (GLA_DK, GLA_DK), 1))
    eye_h = (lax.broadcasted_iota(i32, (HGRN_DK, HGRN_DK), 0)
             == lax.broadcasted_iota(i32, (HGRN_DK, HGRN_DK), 1))
    return tri, causal, eye_g, eye_h


def _lower_bound(lbl_ref):
    l = lbl_ref[...]
    m = jnp.max(l, axis=0, keepdims=True)
    e = jnp.exp(l - m)
    return e[0:1, :] / jnp.sum(e, axis=0, keepdims=True)


def _mixer_prompt_kernel(z_ref, wgk_ref, bgk_ref, gng_ref, lbl_ref, gnh_ref,
                         o_ref, sg_out, sh_out, sg, sh, *, chunks):
    j = pl.program_id(1)

    @pl.when(j == 0)
    def _():
        sg[...] = jnp.zeros_like(sg)
        sh[...] = jnp.zeros_like(sh)

    tri, causal, eye_g, eye_h = _chunk_consts(CHUNK)
    lb = _lower_bound(lbl_ref)
    wgk, bgk, gng, gnh = wgk_ref[...], bgk_ref[...], gng_ref[...], gnh_ref[...]

    def body(c, carry):
        r0 = pl.multiple_of(c * CHUNK, CHUNK)
        zc = z_ref[pl.ds(r0, CHUNK), :]
        o, sg_new, sh_new = _mixer_chunk(
            zc, [sg[h] for h in range(GLA_HEADS)], [sh[h] for h in range(HGRN_HEADS)],
            wgk, bgk, gng, lb, gnh, ref_row=CHUNK // 2, tri=tri, causal=causal,
            eye_g=eye_g, eye_h=eye_h, valid=None)
        o_ref[pl.ds(r0, CHUNK), :] = o.astype(bf16)
        for h in range(GLA_HEADS):
            sg[h] = sg_new[h]
        for h in range(HGRN_HEADS):
            sh[h] = sh_new[h]
        return carry

    lax.fori_loop(0, chunks, body, 0)

    @pl.when(j == pl.num_programs(1) - 1)
    def _():
        sg_out[0, 0] = sg[...]
        sh_out[0, 0] = sh[...]


def _mixer_prompt(z, wgk, bgk, gng, lbl, gnh, *, batch, seq, rows_per_step):
    nj = seq // rows_per_step
    full = lambda shape: pl.BlockSpec(shape, lambda b, j: (0,) * len(shape))
    return pl.pallas_call(
        functools.partial(_mixer_prompt_kernel, chunks=rows_per_step // CHUNK),
        grid=(batch, nj),
        in_specs=[
            pl.BlockSpec((rows_per_step, Z_W), lambda b, j: (b * nj + j, 0)),
            full((R1_PAD, GLA_KDIM)), full((1, GLA_KDIM)), full((1, GLA_DV)),
            full((2, HGRN_FDIM)), full((1, HGRN_DV)),
        ],
        out_specs=[
            pl.BlockSpec((rows_per_step, D_MODEL), lambda b, j: (b * nj + j, 0)),
            pl.BlockSpec((1, 1, GLA_HEADS, GLA_DK, GLA_DV), lambda b, j: (0, b, 0, 0, 0)),
            pl.BlockSpec((1, 1, HGRN_HEADS, HGRN_DK, HGRN_DV), lambda b, j: (0, b, 0, 0, 0)),
        ],
        out_shape=[
            jax.ShapeDtypeStruct((batch * seq, D_MODEL), bf16),
            jax.ShapeDtypeStruct((1, batch, GLA_HEADS, GLA_DK, GLA_DV), f32),
            jax.ShapeDtypeStruct((1, batch, HGRN_HEADS, HGRN_DK, HGRN_DV), f32),
        ],
        scratch_shapes=[
            pltpu.VMEM((GLA_HEADS, GLA_DK, GLA_DV), f32),
            pltpu.VMEM((HGRN_HEADS, HGRN_DK, HGRN_DV), f32),
        ],
        compiler_params=pltpu.CompilerParams(
            dimension_semantics=("arbitrary", "arbitrary"), vmem_limit_bytes=VMEM_LIMIT),
        name="mixer_prompt",
    )(z, wgk, bgk, gng, lbl, gnh)


def _mixer_sample_kernel(z_ref, sg_in, sh_in, wgk_ref, bgk_ref, gng_ref, lbl_ref, gnh_ref,
                         o_ref, sg_out, sh_out, *, batch_block, steps):
    tri, causal, eye_g, eye_h = _chunk_consts(SAMPLE_PAD_T)
    valid = lax.broadcasted_iota(i32, (SAMPLE_PAD_T, 1), 0) < steps
    lb = _lower_bound(lbl_ref)
    wgk, bgk, gng, gnh = wgk_ref[...], bgk_ref[...], gng_ref[...], gnh_ref[...]

    def body(i, carry):
        o, sg_new, sh_new = _mixer_chunk(
            z_ref[i], [sg_in[0, i, h] for h in range(GLA_HEADS)],
            [sh_in[0, i, h] for h in range(HGRN_HEADS)],
            wgk, bgk, gng, lb, gnh, ref_row=steps // 2, tri=tri, causal=causal,
            eye_g=eye_g, eye_h=eye_h, valid=valid)
        o_ref[i] = o
        for h in range(GLA_HEADS):
            sg_out[0, i, h] = sg_new[h]
        for h in range(HGRN_HEADS):
            sh_out[0, i, h] = sh_new[h]
        return carry

    lax.fori_loop(0, batch_block, body, 0)


def _mixer_sample(z3, state_gla, state_hgrn, wgk, bgk, gng, lbl, gnh, *, steps, batch_block):
    batch = z3.shape[0]
    full = lambda shape: pl.BlockSpec(shape, lambda i: (0,) * len(shape))
    sg_spec = pl.BlockSpec((1, batch_block, GLA_HEADS, GLA_DK, GLA_DV), lambda i: (0, i, 0, 0, 0))
    sh_spec = pl.BlockSpec((1, batch_block, HGRN_HEADS, HGRN_DK, HGRN_DV), lambda i: (0, i, 0, 0, 0))
    return pl.pallas_call(
        functools.partial(_mixer_sample_kernel, batch_block=batch_block, steps=steps),
        grid=(batch // batch_block,),
        in_specs=[
            pl.BlockSpec((batch_block, SAMPLE_PAD_T, Z_W), lambda i: (i, 0, 0)),
            sg_spec, sh_spec,
            full((R1_PAD, GLA_KDIM)), full((1, GLA_KDIM)), full((1, GLA_DV)),
            full((2, HGRN_FDIM)), full((1, HGRN_DV)),
        ],
        out_specs=[
            pl.BlockSpec((batch_block, SAMPLE_PAD_T, D_MODEL), lambda i: (i, 0, 0)),
            sg_spec, sh_spec,
        ],
        out_shape=[
            jax.ShapeDtypeStruct((batch, SAMPLE_PAD_T, D_MODEL), f32),
            jax.ShapeDtypeStruct(state_gla.shape, f32),
            jax.ShapeDtypeStruct(state_hgrn.shape, f32),
        ],
        compiler_params=pltpu.CompilerParams(
            dimension_semantics=("arbitrary",), vmem_limit_bytes=VMEM_LIMIT),
        name="mixer_sample",
    )(z3, state_gla, state_hgrn, wgk, bgk, gng, lbl, gnh)


def _store_slabs(ref, x):
    for s in range(SLABS):
        ref[pl.ds(s, x.shape[0], stride=SLABS), :] = x[:, s * LANES:(s + 1) * LANES]


def _load_slab_cols(ref, start, rows, s):
    return ref[pl.ds(start + s, rows, stride=SLABS), :]


def _outproj_kernel(xp_ref, xs_ref, o_ref, wo_ref, g_ref, wr_ref, br_ref,
                    h_ref, xn_ref, ti_ref, tg_ref, rank_ref, cnt_ref, run_cnt, *, n_prompt_tiles):
    i = pl.program_id(0)

    @pl.when(i == 0)
    def _():
        run_cnt[...] = jnp.zeros_like(run_cnt)

    x = _select_rows(i < n_prompt_tiles, xp_ref, xs_ref)
    h = x + _dot(o_ref[...], wo_ref[...])
    h_ref[...] = h
    xn = _rms(h, g_ref[...])
    _store_slabs(xn_ref, xn)
    logits = _dot(xn.astype(bf16), wr_ref[...]) + br_ref[...]
    lane = lax.broadcasted_iota(i32, logits.shape, 1)
    neg = jnp.float32(-jnp.inf)
    l = jnp.where(lane < N_EXPERTS, logits, neg)
    vals, idxs = [], []
    for _ in range(TOP_K):
        m = jnp.max(l, axis=-1, keepdims=True)
        idx = jnp.min(jnp.where(l == m, lane, LANES), axis=-1, keepdims=True)
        vals.append(m)
        idxs.append(idx)
        l = jnp.where(lane == idx, neg, l)
    es = [jnp.exp(v - vals[0]) for v in vals]
    denom = es[0] + es[1] + es[2] + es[3]

    onehot = jnp.zeros(logits.shape, f32)
    for k in range(TOP_K):
        onehot = onehot + (lane == idxs[k]).astype(f32)
    rows = logits.shape[0]
    lower = (lax.broadcasted_iota(i32, (rows, rows), 0)
             > lax.broadcasted_iota(i32, (rows, rows), 1)).astype(bf16)
    ahead = _dot(lower, onehot.astype(bf16)) + run_cnt[...]
    run_cnt[...] = run_cnt[...] + jnp.sum(onehot, axis=0, keepdims=True)
    cnt_ref[...] = run_cnt[...].astype(i32)

    ti = jnp.zeros(logits.shape, i32)
    tg = jnp.zeros(logits.shape, f32)
    rank = jnp.zeros(logits.shape, i32)
    for k in range(TOP_K):
        rk = jnp.sum(jnp.where(lane == idxs[k], ahead, 0.0), axis=-1, keepdims=True).astype(i32)
        ti = jnp.where(lane == k, idxs[k], ti)
        tg = jnp.where(lane == k, es[k] / denom, tg)
        rank = jnp.where(lane == k, rk, rank)
    ti_ref[...] = ti
    tg_ref[...] = tg
    rank_ref[...] = rank


def _out_proj(xp, xs, o, wo, g, wr, br):
    tp, ts = xp.shape[0], xs.shape[0]
    npt = tp // TOKEN_TILE
    nt = npt + ts // TOKEN_TILE
    t = tp + ts
    row = lambda w: pl.BlockSpec((TOKEN_TILE, w), lambda i: (i, 0))
    full = lambda shape: pl.BlockSpec(shape, lambda i: (0,) * len(shape))
    return pl.pallas_call(
        functools.partial(_outproj_kernel, n_prompt_tiles=npt),
        grid=(nt,),
        in_specs=[
            pl.BlockSpec((TOKEN_TILE, D_MODEL), lambda i: (jnp.minimum(i, npt - 1), 0)),
            pl.BlockSpec((TOKEN_TILE, D_MODEL), lambda i: (jnp.maximum(i - npt, 0), 0)),
            row(D_MODEL), full((D_MODEL, D_MODEL)), full((1, D_MODEL)),
            full((D_MODEL, LANES)), full((1, LANES)),
        ],
        out_specs=[row(D_MODEL), pl.BlockSpec((TOKEN_TILE * SLABS, LANES), lambda i: (i, 0)),
                   row(LANES), row(LANES), row(LANES), full((1, LANES))],
        out_shape=[
            jax.ShapeDtypeStruct((t, D_MODEL), f32),
            jax.ShapeDtypeStruct((t * SLABS, LANES), f32),
            jax.ShapeDtypeStruct((t, LANES), i32),
            jax.ShapeDtypeStruct((t, LANES), f32),
            jax.ShapeDtypeStruct((t, LANES), i32),
            jax.ShapeDtypeStruct((1, LANES), i32),
        ],
        scratch_shapes=[pltpu.VMEM((1, LANES), f32)],
        compiler_params=pltpu.CompilerParams(
            dimension_semantics=("arbitrary",), vmem_limit_bytes=VMEM_LIMIT),
        name="out_proj_router",
    )(xp, xs, o, wo, g, wr, br)


def _dispatch_kernel(bv_ref, nz_ref, pos_ref, xn_ref, xs_hbm, zeros, sem, zsem, *, n_blocks):
    @pl.when(pl.program_id(0) == 0)
    def _():
        zeros[...] = jnp.zeros_like(zeros)

        def fill(b, carry):
            @pl.when(bv_ref[b] < MOE_TILE)
            def _():
                r0 = pl.multiple_of(b * (MOE_TILE * SLABS), MOE_TILE * SLABS)
                pltpu.make_async_copy(zeros, xs_hbm.at[pl.ds(r0, MOE_TILE * SLABS)], zsem).start()
            return carry

        def drain(b, carry):
            pltpu.make_async_copy(zeros, xs_hbm.at[pl.ds(0, MOE_TILE * SLABS)], zsem).wait()
            return carry

        lax.fori_loop(0, n_blocks, fill, 0)
        lax.fori_loop(0, nz_ref[0], drain, 0)

    def start(r, carry):
        src = xn_ref.at[pl.ds(pl.multiple_of(r * SLABS, SLABS), SLABS)]
        for k in range(TOP_K):
            dst0 = pl.multiple_of(pos_ref[0, 0, r * TOP_K + k], SLABS)
            pltpu.make_async_copy(src, xs_hbm.at[pl.ds(dst0, SLABS)], sem).start(priority=k % 2)
        return carry

    lax.fori_loop(0, DISPATCH_TILE, start, 0, unroll=2)
    for _ in range(TOP_K):
        pltpu.make_async_copy(xn_ref, xs_hbm.at[pl.ds(0, DISPATCH_TILE * SLABS)], sem).wait()


def _dispatch(block_valid, pos, xn2):
    t = xn2.shape[0] // SLABS
    nt = t // DISPATCH_TILE
    n_blocks = block_valid.shape[0]
    n_zero = jnp.sum((block_valid < MOE_TILE).astype(i32)).reshape(1)
    grid_spec = pltpu.PrefetchScalarGridSpec(
        num_scalar_prefetch=2,
        grid=(nt,),
        in_specs=[
            pl.BlockSpec((1, 1, DISPATCH_TILE * TOP_K), lambda i, bv, nz: (i, 0, 0),
                         memory_space=pltpu.SMEM),
            pl.BlockSpec((DISPATCH_TILE * SLABS, LANES), lambda i, bv, nz: (i, 0)),
        ],
        out_specs=pl.BlockSpec(memory_space=pl.ANY),
        scratch_shapes=[
            pltpu.VMEM((MOE_TILE * SLABS, LANES), f32),
            pltpu.SemaphoreType.DMA(()),
            pltpu.SemaphoreType.DMA(()),
        ],
    )
    return pl.pallas_call(
        functools.partial(_dispatch_kernel, n_blocks=n_blocks),
        grid_spec=grid_spec,
        out_shape=jax.ShapeDtypeStruct((n_blocks * MOE_TILE * SLABS, LANES), f32),
        compiler_params=pltpu.CompilerParams(
            dimension_semantics=("arbitrary",), vmem_limit_bytes=VMEM_LIMIT),
        name="moe_dispatch",
    )(block_valid, n_zero, pos.reshape(nt, 1, DISPATCH_TILE * TOP_K), xn2)


def _expert_kernel(be_ref, nb_ref, x_ref, wgu_ref, bgu_ref, wdn_ref, bdn_ref, y_ref, wgu_bf, wdn_bf,
                   *, ff_chunk, cast_rows):
    i = pl.program_id(0)

    @pl.when(i >= nb_ref[0])
    def _():
        y_ref[...] = jnp.zeros_like(y_ref)

    @pl.when((i < nb_ref[0]) & ((i == 0) | (be_ref[i] != be_ref[jnp.maximum(i - 1, 0)])))
    def _():
        def cast(j, carry):
            r0 = pl.multiple_of(j * cast_rows, cast_rows)
            wgu_bf[pl.ds(r0, cast_rows), :] = wgu_ref[0, pl.ds(r0, cast_rows), :].astype(bf16)
            wdn_bf[pl.ds(r0, cast_rows), :] = wdn_ref[0, pl.ds(r0, cast_rows), :].astype(bf16)
            return carry
        lax.fori_loop(0, D_MODEL // cast_rows, cast, 0)

    @pl.when(i < nb_ref[0])
    def _():
        xb = jnp.concatenate(
            [_load_slab_cols(x_ref, 0, MOE_TILE, s).astype(bf16) for s in range(SLABS)], axis=1)
        acc = jnp.zeros((MOE_TILE, D_MODEL), f32)
        for c0 in range(0, D_FF, ff_chunk):
            gate = _dot(xb, wgu_bf[:, c0:c0 + ff_chunk]) + bgu_ref[0, :, c0:c0 + ff_chunk]
            up = (_dot(xb, wgu_bf[:, D_FF + c0:D_FF + c0 + ff_chunk])
                  + bgu_ref[0, :, D_FF + c0:D_FF + c0 + ff_chunk])
            gate = jnp.minimum(gate, SWIGLU_LIMIT)
            up = jnp.clip(up, -SWIGLU_LIMIT, SWIGLU_LIMIT)
            hid = (up + 1.0) * (gate * _sigmoid(gate * SWIGLU_ALPHA))
            acc = acc + _dot(hid.astype(bf16), wdn_bf[c0:c0 + ff_chunk, :])
        _store_slabs(y_ref, acc + bdn_ref[0])


def _experts(block_e, n_used, xs, wgu, bgu, wdn, bdn):
    n_blocks = block_e.shape[0]
    grid_spec = pltpu.PrefetchScalarGridSpec(
        num_scalar_prefetch=2,
        grid=(n_blocks,),
        in_specs=[
            pl.BlockSpec((MOE_TILE * SLABS, LANES), lambda i, be, nb: (jnp.minimum(i, nb[0] - 1), 0)),
            pl.BlockSpec((1, D_MODEL, 2 * D_FF), lambda i, be, nb: (be[i], 0, 0)),
            pl.BlockSpec((1, 1, 2 * D_FF), lambda i, be, nb: (be[i], 0, 0)),
            pl.BlockSpec((1, D_FF, D_MODEL), lambda i, be, nb: (be[i], 0, 0)),
            pl.BlockSpec((1, 1, D_MODEL), lambda i, be, nb: (be[i], 0, 0)),
        ],
        out_specs=pl.BlockSpec((MOE_TILE * SLABS, LANES), lambda i, be, nb: (i, 0)),
        scratch_shapes=[
            pltpu.VMEM((D_MODEL, 2 * D_FF), bf16),
            pltpu.VMEM((D_FF, D_MODEL), bf16),
        ],
    )
    return pl.pallas_call(
        functools.partial(_expert_kernel, ff_chunk=256, cast_rows=128),
        grid_spec=grid_spec,
        out_shape=jax.ShapeDtypeStruct((n_blocks * MOE_TILE * SLABS, LANES), f32),
        compiler_params=pltpu.CompilerParams(
            dimension_semantics=("arbitrary",), vmem_limit_bytes=VMEM_LIMIT),
        name="moe_experts",
    )(block_e, n_used, xs, wgu, bgu, wdn, bdn)


def _combine_kernel(pos_cur, pos_nxt, ys_hbm, h_ref, tg_ref, pp_ref, ps_ref, gple_ref, wpg_ref, wpp_ref,
                    gfin_ref, yp_ref, ys_ref, rows, sems, *, n_prompt_tiles):
    i = pl.program_id(0)
    nt = pl.num_programs(0)
    n = COMBINE_TILE * TOP_K
    slot_rows = n * SLABS

    def gather(pos_ref, slot):
        def start(j, carry):
            for u in range(2):
                r = 2 * j + u
                src0 = pl.multiple_of(pos_ref[0, 0, r], SLABS)
                dst0 = pl.multiple_of(slot * slot_rows + r * SLABS, SLABS)
                pltpu.make_async_copy(ys_hbm.at[pl.ds(src0, SLABS)], rows.at[pl.ds(dst0, SLABS)],
                                      sems.at[slot]).start(priority=u)
            return carry
        lax.fori_loop(0, n // 2, start, 0, unroll=4)

    @pl.when(i == 0)
    def _():
        gather(pos_cur, 0)

    @pl.when(i + 1 < nt)
    def _():
        gather(pos_nxt, (i + 1) % 2)

    slot = i % 2
    base = pl.multiple_of(slot * slot_rows, slot_rows)
    pltpu.make_async_copy(ys_hbm.at[pl.ds(0, slot_rows)], rows.at[pl.ds(base, slot_rows)],
                          sems.at[slot]).wait()

    tg = tg_ref[...]
    cols = []
    for s in range(SLABS):
        moe = jnp.zeros((COMBINE_TILE, LANES), f32)
        for k in range(TOP_K):
            moe = moe + tg[:, k:k + 1] * _load_slab_cols(
                rows, base + k * COMBINE_TILE * SLABS, COMBINE_TILE, s)
        cols.append(h_ref[:, s * LANES:(s + 1) * LANES] + moe)
    h = jnp.concatenate(cols, axis=1)
    gate = _sigmoid(_dot(_rms(h, gple_ref[...]).astype(bf16), wpg_ref[...]))
    p = _select_rows(i < n_prompt_tiles, pp_ref, ps_ref).astype(bf16)
    h = h + gate * _dot(p, wpp_ref[...])
    y = _rms(h, gfin_ref[...])

    @pl.when(i < n_prompt_tiles)
    def _():
        yp_ref[...] = y

    @pl.when(i >= n_prompt_tiles)
    def _():
        ys_ref[...] = y


def _combine(pos, ys, h1, tg, pp, ps, gple, wpg, wpp, gfin):
    t = h1.shape[0]
    nt = t // COMBINE_TILE
    npt = pp.shape[0] // COMBINE_TILE
    n = COMBINE_TILE * TOP_K
    pos_tiles = pos.reshape(nt, COMBINE_TILE, TOP_K).transpose(0, 2, 1).reshape(nt, 1, n)
    row = lambda w: pl.BlockSpec((COMBINE_TILE, w), lambda i: (i, 0))
    full = lambda shape: pl.BlockSpec(shape, lambda i: (0,) * len(shape))
    return pl.pallas_call(
        functools.partial(_combine_kernel, n_prompt_tiles=npt),
        grid=(nt,),
        in_specs=[
            pl.BlockSpec((1, 1, n), lambda i: (i, 0, 0), memory_space=pltpu.SMEM),
            pl.BlockSpec((1, 1, n), lambda i: (jnp.minimum(i + 1, nt - 1), 0, 0), memory_space=pltpu.SMEM),
            pl.BlockSpec(memory_space=pl.ANY),
            row(D_MODEL), row(LANES),
            pl.BlockSpec((COMBINE_TILE, PLE_DIM), lambda i: (jnp.minimum(i, npt - 1), 0)),
            pl.BlockSpec((COMBINE_TILE, PLE_DIM), lambda i: (jnp.maximum(i - npt, 0), 0)),
            full((1, D_MODEL)), full((D_MODEL, D_MODEL)), full((PLE_DIM, D_MODEL)),
            full((1, D_MODEL)),
        ],
        out_specs=[
            pl.BlockSpec((COMBINE_TILE, D_MODEL), lambda i: (jnp.minimum(i, npt - 1), 0)),
            pl.BlockSpec((COMBINE_TILE, D_MODEL), lambda i: (jnp.maximum(i - npt, 0), 0)),
        ],
        out_shape=[
            jax.ShapeDtypeStruct((pp.shape[0], D_MODEL), f32),
            jax.ShapeDtypeStruct((ps.shape[0], D_MODEL), f32),
        ],
        scratch_shapes=[
            pltpu.VMEM((2 * n * SLABS, LANES), f32),
            pltpu.SemaphoreType.DMA((2,)),
        ],
        compiler_params=pltpu.CompilerParams(
            dimension_semantics=("arbitrary",), vmem_limit_bytes=VMEM_LIMIT),
        name="moe_combine_ple",
    )(pos_tiles, pos_tiles, ys, h1, tg, pp, ps, gple, wpg, wpp, gfin)


def _routing_tables(top_i, rank, counts, n_blocks):
    padded = (counts + MOE_TILE - 1) // MOE_TILE * MOE_TILE
    pend = jnp.cumsum(padded)
    pstart = pend - padded
    block_rows = jnp.arange(n_blocks, dtype=i32) * MOE_TILE
    block_e = jnp.minimum(
        jnp.sum((block_rows[:, None] >= pend[None, :]).astype(i32), axis=1), N_EXPERTS - 1)
    block_valid = jnp.clip(pstart[block_e] + counts[block_e] - block_rows, 0, MOE_TILE)
    n_used = pend[-1:] // MOE_TILE
    onehot = top_i[:, :, None] == jnp.arange(N_EXPERTS, dtype=i32)[None, None, :]
    pos = (rank + jnp.sum(jnp.where(onehot, pstart[None, None, :], 0), axis=-1)) * SLABS
    return pos.astype(i32), block_e.astype(i32), block_valid.astype(i32), n_used.astype(i32)


def kernel(x_prompt, x_sample, p_prompt, p_sample, state_gla, state_hgrn, g_mix, w_in, w_gk2, b_gk2,
           gn_gla, lb_logits, gn_hgrn, w_out, g_ffn, w_router, b_router, w_gu, b_gu, w_dn, b_dn,
           g_ple, w_ple_gate, w_ple_proj, g_final):
    batch, seq, _ = x_prompt.shape
    dec_batch, dec_seq, _ = x_sample.shape
    tp, ts = batch * seq, dec_batch * dec_seq
    xp = x_prompt.reshape(tp, D_MODEL)
    xs = x_sample.reshape(ts, D_MODEL)

    r1_end = Z_R1 + GLA_LOWRANK
    w_in_p = jnp.concatenate(
        [w_in[0, :, :r1_end], jnp.zeros((D_MODEL, R1_PAD - GLA_LOWRANK), f32), w_in[0, :, r1_end:]],
        axis=1).astype(bf16)
    wgk = jnp.concatenate(
        [w_gk2[0], jnp.zeros((R1_PAD - GLA_LOWRANK, GLA_KDIM), f32)], axis=0).astype(bf16)
    wr = jnp.concatenate(
        [w_router[0], jnp.zeros((D_MODEL, LANES - N_EXPERTS), f32)], axis=1).astype(bf16)
    br = jnp.concatenate([b_router[0], jnp.zeros((LANES - N_EXPERTS,), f32)]).reshape(1, LANES)

    z = _in_proj(xp, xs, g_mix[0].reshape(1, D_MODEL), w_in_p)

    mixer_params = (wgk, b_gk2[0].reshape(1, GLA_KDIM), gn_gla[0].reshape(1, GLA_DV),
                    lb_logits[0:2], gn_hgrn[0].reshape(1, HGRN_DV))
    o_p, new_gla_p, new_hgrn_p = _mixer_prompt(z, *mixer_params, batch=batch, seq=seq, rows_per_step=512)
    z_s = jnp.pad(z[tp:].reshape(dec_batch, dec_seq, Z_W), ((0, 0), (0, SAMPLE_PAD_T - dec_seq), (0, 0)))
    o_s, new_gla_s, new_hgrn_s = _mixer_sample(
        z_s, state_gla, state_hgrn, *mixer_params, steps=dec_seq, batch_block=8)
    o = jnp.concatenate([o_p, o_s[:, :dec_seq].reshape(ts, D_MODEL).astype(bf16)], axis=0)

    h1, xn2, top_i, top_g, rank, counts = _out_proj(
        xp, xs, o, w_out[0].astype(bf16), g_ffn[0].reshape(1, D_MODEL), wr, br)

    t = tp + ts
    n_blocks = -(-(t * TOP_K) // MOE_TILE) + N_EXPERTS
    pos, block_e, block_valid, n_used = _routing_tables(
        top_i[:, :TOP_K], rank[:, :TOP_K], counts[0, :N_EXPERTS], n_blocks)
    x_sorted = _dispatch(block_valid, pos, xn2)
    y_sorted = _experts(block_e, n_used, x_sorted, w_gu[0], b_gu[0].reshape(N_EXPERTS, 1, 2 * D_FF),
                        w_dn[0], b_dn[0].reshape(N_EXPERTS, 1, D_MODEL))

    y_p, y_s = _combine(pos, y_sorted, h1, top_g, p_prompt[0].reshape(tp, PLE_DIM),
                        p_sample[0].reshape(ts, PLE_DIM), g_ple[0].reshape(1, D_MODEL),
                        w_ple_gate[0].astype(bf16), w_ple_proj[0].astype(bf16), g_final.reshape(1, D_MODEL))

    return (y_p.reshape(batch, seq, D_MODEL), y_s.reshape(dec_batch, dec_seq, D_MODEL),
            new_gla_p, new_hgrn_p, new_gla_s, new_hgrn_s)
```

```python
import functools

import jax
import jax.numpy as jnp
from jax import lax
from jax.experimental import pallas as pl
from jax.experimental.pallas import tpu as pltpu

f32 = jnp.float32
bf16 = jnp.bfloat16
i32 = jnp.int32

D_MODEL = 1024
GLA_HEADS = 4
GLA_DK = 64
GLA_DV = 128
GLA_KDIM = GLA_HEADS * GLA_DK
GLA_WIDTH = GLA_HEADS * GLA_DV
GLA_LOWRANK = 16
GLA_NORMALIZER = 16.0
HGRN_HEADS = 4
HGRN_DK = 128
HGRN_DV = 128
HGRN_FDIM = HGRN_HEADS * HGRN_DK
HGRN_WIDTH = HGRN_HEADS * HGRN_DV
CHUNK = 64
N_EXPERTS = 32
TOP_K = 4
D_FF = 1024
SWIGLU_LIMIT = 7.0
SWIGLU_ALPHA = 1.702
PLE_DIM = 256
EPS = 1e-6

LANES = 128
SAMPLE_PAD_T = 8
SLABS = D_MODEL // LANES

R1_PAD = LANES
Z_Q1 = 0
Z_K1 = Z_Q1 + GLA_KDIM
Z_V1 = Z_K1 + GLA_KDIM
Z_G1 = Z_V1 + GLA_WIDTH
Z_R1 = Z_G1 + GLA_WIDTH
Z_Q2 = Z_R1 + R1_PAD
Z_F2 = Z_Q2 + HGRN_FDIM
Z_I2 = Z_F2 + HGRN_FDIM
Z_G2 = Z_I2 + HGRN_WIDTH
Z_W = Z_G2 + HGRN_WIDTH

TOKEN_TILE = 512
MOE_TILE = 256
DISPATCH_TILE = 256
COMBINE_TILE = 256
VMEM_LIMIT = 56 * 1024 * 1024


def _rms(x, g):
    return x * lax.rsqrt(jnp.mean(x * x, axis=-1, keepdims=True) + EPS) * g


def _sigmoid(x):
    return 1.0 / (1.0 + jnp.exp(-x))


def _dot(a, b):
    return jnp.dot(a, b, preferred_element_type=f32)


def _dot_nt(a, b):
    return lax.dot_general(a, b, (((1,), (1,)), ((), ())), preferred_element_type=f32)


def _dot_tn(a, b):
    return lax.dot_general(a, b, (((0,), (0,)), ((), ())), preferred_element_type=f32)


def _cumsum_rows(tri, x):
    hi = x.astype(bf16)
    r1 = x - hi.astype(f32)
    mid = r1.astype(bf16)
    lo = (r1 - mid.astype(f32)).astype(bf16)
    return _dot(tri, hi) + _dot(tri, mid) + _dot(tri, lo)


def _select_rows(is_prompt, a_ref, b_ref):
    return jnp.where(is_prompt, a_ref[...], b_ref[...])


def _inproj_kernel(xp_ref, xs_ref, g_ref, w_ref, z_ref, *, n_prompt_tiles, col_chunk):
    is_prompt = pl.program_id(0) < n_prompt_tiles
    x = _select_rows(is_prompt, xp_ref, xs_ref)
    a = _rms(x, g_ref[...]).astype(bf16)
    for c0 in range(0, Z_W, col_chunk):
        c1 = min(c0 + col_chunk, Z_W)
        z_ref[:, c0:c1] = _dot(a, w_ref[:, c0:c1]).astype(bf16)


def _in_proj(xp, xs, g, w):
    tp, ts = xp.shape[0], xs.shape[0]
    npt = tp // TOKEN_TILE
    nt = npt + ts // TOKEN_TILE
    return pl.pallas_call(
        functools.partial(_inproj_kernel, n_prompt_tiles=npt, col_chunk=512),
        grid=(nt,),
        in_specs=[
            pl.BlockSpec((TOKEN_TILE, D_MODEL), lambda i: (jnp.minimum(i, npt - 1), 0)),
            pl.BlockSpec((TOKEN_TILE, D_MODEL), lambda i: (jnp.maximum(i - npt, 0), 0)),
            pl.BlockSpec((1, D_MODEL), lambda i: (0, 0)),
            pl.BlockSpec((D_MODEL, Z_W), lambda i: (0, 0)),
        ],
        out_specs=pl.BlockSpec((TOKEN_TILE, Z_W), lambda i: (i, 0)),
        out_shape=jax.ShapeDtypeStruct((tp + ts, Z_W), bf16),
        compiler_params=pltpu.CompilerParams(
            dimension_semantics=("arbitrary",), vmem_limit_bytes=VMEM_LIMIT),
        name="in_proj",
    )(xp, xs, g, w)


def _block_consts(r, c):
    shift = c.bit_length() - 1
    row = lax.broadcasted_iota(i32, (r, r), 0)
    col = lax.broadcasted_iota(i32, (r, r), 1)
    causal = (lax.shift_right_logical(row, shift) == lax.shift_right_logical(col, shift)) & (row >= col)
    return causal.astype(bf16), causal


def _subchunk_row(x, c, idx):
    parts = [jnp.broadcast_to(x[j * c + idx:j * c + idx + 1, :], (c, x.shape[1]))
             for j in range(x.shape[0] // c)]
    return jnp.concatenate(parts, axis=0)


def _gated_block(q, k, v, log_a, states, *, c, scale, ref_row, tri, causal, heads_per_group, scan):
    r = q.shape[0]
    nsub = r // c
    groups = q.shape[1] // LANES
    b = _cumsum_rows(tri, log_a)
    b_ref = _subchunk_row(b, c, ref_row)
    b_last = _subchunk_row(b, c, c - 1)
    qs = q * scale
    q_r = qs * jnp.exp(b - b_ref)
    k_r = (k * jnp.exp(b_ref - b)).astype(bf16)
    k_end = k * jnp.exp(b_last - b)
    q_b = qs * jnp.exp(b)
    decay = jnp.exp(b_last)
    vb = v.astype(bf16)
    lane = lax.broadcasted_iota(i32, (r, LANES), 1)
    width = LANES // heads_per_group
    outs, new_states = [], []
    for g in range(groups):
        gl = slice(g * LANES, (g + 1) * LANES)
        masks = [None] if heads_per_group == 1 else [
            (lane >= j * width) & (lane < (j + 1) * width) for j in range(heads_per_group)]
        pick = lambda x, m: x if m is None else jnp.where(m, x, 0.0)
        q_r_h = [pick(q_r[:, gl], m).astype(bf16) for m in masks]
        q_b_h = [pick(q_b[:, gl], m).astype(bf16) for m in masks]
        k_end_h = [pick(k_end[:, gl], m).astype(bf16) for m in masks]
        v_h = [vb[:, (g * heads_per_group + j) * LANES:(g * heads_per_group + j + 1) * LANES]
               for j in range(heads_per_group)]
        o_intra = []
        for j in range(heads_per_group):
            att = _dot_nt(q_r_h[j], k_r[:, gl])
            att = jnp.where(causal, att, 0.0).astype(bf16)
            o_intra.append(_dot(att, v_h[j]))
        upd = []
        for s in range(nsub):
            rs = slice(s * c, (s + 1) * c)
            u = _dot_tn(v_h[0][rs], k_end_h[0][rs])
            for j in range(1, heads_per_group):
                u = u + _dot_tn(v_h[j][rs], k_end_h[j][rs])
            upd.append(u)
        o_inter = [[] for _ in range(heads_per_group)]
        st = states[g] if scan else None
        group_states = []
        for s in range(nsub):
            rs = slice(s * c, (s + 1) * c)
            if not scan:
                st = states[s][g]
            st_b = st.astype(bf16)
            for j in range(heads_per_group):
                o_inter[j].append(_dot_nt(q_b_h[j][rs], st_b))
            st = st * decay[s * c:s * c + 1, gl] + upd[s]
            if not scan:
                group_states.append(st)
        for j in range(heads_per_group):
            outs.append(o_intra[j] + jnp.concatenate(o_inter[j], axis=0))
        new_states.append(st if scan else group_states)
    if not scan:
        new_states = [[new_states[g][s] for g in range(groups)] for s in range(nsub)]
    return outs, new_states


def _head_norm(o, g):
    return o * lax.rsqrt(jnp.mean(o * o, axis=-1, keepdims=True) + EPS) * g


def _mixer_block(zb, st_gla, st_hgrn, wgk, bgk, gng, lb, gnh, *, c, ref_row, tri, causal, valid, scan):
    z = zb.astype(f32)
    gk = _dot(zb[:, Z_R1:Z_R1 + R1_PAD], wgk) + bgk
    log_a = (jnp.minimum(gk, 0.0) - jnp.log(1.0 + jnp.exp(-jnp.abs(gk)))) / GLA_NORMALIZER
    k1 = z[:, Z_K1:Z_K1 + GLA_KDIM]
    if valid is not None:
        log_a = jnp.where(valid, log_a, 0.0)
        k1 = jnp.where(valid, k1, 0.0)
    o1, st_gla = _gated_block(
        z[:, Z_Q1:Z_Q1 + GLA_KDIM], k1, z[:, Z_V1:Z_V1 + GLA_WIDTH], log_a, st_gla,
        c=c, scale=GLA_DK ** -0.5, ref_row=ref_row, tri=tri, causal=causal,
        heads_per_group=LANES // GLA_DK, scan=scan)
    forget = lb + (1.0 - lb) * _sigmoid(z[:, Z_F2:Z_F2 + HGRN_FDIM])
    k2 = 1.0 - forget
    log_f = jnp.log(forget)
    if valid is not None:
        log_f = jnp.where(valid, log_f, 0.0)
        k2 = jnp.where(valid, k2, 0.0)
    q2 = z[:, Z_Q2:Z_Q2 + HGRN_FDIM]
    i2 = z[:, Z_I2:Z_I2 + HGRN_WIDTH]
    o2, st_hgrn = _gated_block(
        q2 * _sigmoid(q2), k2, i2 * _sigmoid(i2), log_f, st_hgrn,
        c=c, scale=1.0, ref_row=ref_row, tri=tri, causal=causal, heads_per_group=1, scan=scan)
    cols = []
    for h in range(GLA_HEADS):
        g1 = z[:, Z_G1 + h * GLA_DV:Z_G1 + (h + 1) * GLA_DV]
        cols.append(_head_norm(o1[h], gng) * (g1 * _sigmoid(g1)))
    for h in range(HGRN_HEADS):
        g2 = z[:, Z_G2 + h * HGRN_DV:Z_G2 + (h + 1) * HGRN_DV]
        cols.append(_head_norm(o2[h], gnh) * _sigmoid(g2))
    return jnp.concatenate(cols, axis=1), st_gla, st_hgrn


def _lower_bound(lbl_ref):
    l = lbl_ref[...]
    m = jnp.max(l, axis=0, keepdims=True)
    e = jnp.exp(l - m)
    return e[0:1, :] / jnp.sum(e, axis=0, keepdims=True)


GLA_GROUPS = GLA_KDIM // LANES


def _gla_state_t(s_ref_heads):
    return s_ref_heads.reshape(LANES, GLA_DV).T


def _gla_state_from_t(st):
    return st.T.reshape(LANES // GLA_DK, GLA_DK, GLA_DV)


def _mixer_prompt_kernel(z_ref, wgk_ref, bgk_ref, gng_ref, lbl_ref, gnh_ref,
                         o_ref, sg_out, sh_out, sg_t, sh_t, *, block_rows):
    j = pl.program_id(1)

    @pl.when(j == 0)
    def _():
        sg_t[...] = jnp.zeros_like(sg_t)
        sh_t[...] = jnp.zeros_like(sh_t)

    tri, causal = _block_consts(block_rows, CHUNK)
    lb = _lower_bound(lbl_ref)
    wgk, bgk, gng, gnh = wgk_ref[...], bgk_ref[...], gng_ref[...], gnh_ref[...]

    def body(blk, carry):
        r0 = pl.multiple_of(blk * block_rows, block_rows)
        o, st_g, st_h = _mixer_block(
            z_ref[pl.ds(r0, block_rows), :], [sg_t[g] for g in range(GLA_GROUPS)],
            [sh_t[h] for h in range(HGRN_HEADS)], wgk, bgk, gng, lb, gnh,
            c=CHUNK, ref_row=CHUNK // 2, tri=tri, causal=causal, valid=None, scan=True)
        o_ref[pl.ds(r0, block_rows), :] = o.astype(bf16)
        for g in range(GLA_GROUPS):
            sg_t[g] = st_g[g]
        for h in range(HGRN_HEADS):
            sh_t[h] = st_h[h]
        return carry

    lax.fori_loop(0, z_ref.shape[0] // block_rows, body, 0)

    @pl.when(j == pl.num_programs(1) - 1)
    def _():
        per = LANES // GLA_DK
        for g in range(GLA_GROUPS):
            sg_out[0, 0, g * per:(g + 1) * per] = _gla_state_from_t(sg_t[g])
        for h in range(HGRN_HEADS):
            sh_out[0, 0, h] = sh_t[h].T


def _mixer_prompt(z, wgk, bgk, gng, lbl, gnh, *, batch, seq, rows_per_step, block_rows):
    nj = seq // rows_per_step
    full = lambda shape: pl.BlockSpec(shape, lambda b, j: (0,) * len(shape))
    return pl.pallas_call(
        functools.partial(_mixer_prompt_kernel, block_rows=block_rows),
        grid=(batch, nj),
        in_specs=[
            pl.BlockSpec((rows_per_step, Z_W), lambda b, j: (b * nj + j, 0)),
            full((R1_PAD, GLA_KDIM)), full((1, GLA_KDIM)), full((1, GLA_DV)),
            full((2, HGRN_FDIM)), full((1, HGRN_DV)),
        ],
        out_specs=[
            pl.BlockSpec((rows_per_step, D_MODEL), lambda b, j: (b * nj + j, 0)),
            pl.BlockSpec((1, 1, GLA_HEADS, GLA_DK, GLA_DV), lambda b, j: (0, b, 0, 0, 0)),
            pl.BlockSpec((1, 1, HGRN_HEADS, HGRN_DK, HGRN_DV), lambda b, j: (0, b, 0, 0, 0)),
        ],
        out_shape=[
            jax.ShapeDtypeStruct((batch * seq, D_MODEL), bf16),
            jax.ShapeDtypeStruct((1, batch, GLA_HEADS, GLA_DK, GLA_DV), f32),
            jax.ShapeDtypeStruct((1, batch, HGRN_HEADS, HGRN_DK, HGRN_DV), f32),
        ],
        scratch_shapes=[
            pltpu.VMEM((GLA_GROUPS, GLA_DV, LANES), f32),
            pltpu.VMEM((HGRN_HEADS, HGRN_DV, HGRN_DK), f32),
        ],
        compiler_params=pltpu.CompilerParams(
            dimension_semantics=("arbitrary", "arbitrary"), vmem_limit_bytes=VMEM_LIMIT),
        name="mixer_prompt",
    )(z, wgk, bgk, gng, lbl, gnh)


def _mixer_sample_kernel(z_ref, sg_in, sh_in, wgk_ref, bgk_ref, gng_ref, lbl_ref, gnh_ref,
                         o_ref, sg_out, sh_out, *, batch_block, steps):
    rows = batch_block * SAMPLE_PAD_T
    tri, causal = _block_consts(rows, SAMPLE_PAD_T)
    step = lax.broadcasted_iota(i32, (rows, 1), 0) & (SAMPLE_PAD_T - 1)
    valid = step < steps
    lb = _lower_bound(lbl_ref)
    per = LANES // GLA_DK
    st_g = [[_gla_state_t(sg_in[0, s, g * per:(g + 1) * per]) for g in range(GLA_GROUPS)]
            for s in range(batch_block)]
    st_h = [[sh_in[0, s, h].T for h in range(HGRN_HEADS)] for s in range(batch_block)]
    o, st_g, st_h = _mixer_block(
        z_ref[...], st_g, st_h, wgk_ref[...], bgk_ref[...], gng_ref[...], lb, gnh_ref[...],
        c=SAMPLE_PAD_T, ref_row=steps // 2, tri=tri, causal=causal, valid=valid, scan=False)
    o_ref[...] = o
    for s in range(batch_block):
        for g in range(GLA_GROUPS):
            sg_out[0, s, g * per:(g + 1) * per] = _gla_state_from_t(st_g[s][g])
        for h in range(HGRN_HEADS):
            sh_out[0, s, h] = st_h[s][h].T


def _mixer_sample(z2, state_gla, state_hgrn, wgk, bgk, gng, lbl, gnh, *, steps, batch_block):
    batch = z2.shape[0] // SAMPLE_PAD_T
    rows = batch_block * SAMPLE_PAD_T
    full = lambda shape: pl.BlockSpec(shape, lambda i: (0,) * len(shape))
    sg_spec = pl.BlockSpec((1, batch_block, GLA_HEADS, GLA_DK, GLA_DV), lambda i: (0, i, 0, 0, 0))
    sh_spec = pl.BlockSpec((1, batch_block, HGRN_HEADS, HGRN_DK, HGRN_DV), lambda i: (0, i, 0, 0, 0))
    return pl.pallas_call(
        functools.partial(_mixer_sample_kernel, batch_block=batch_block, steps=steps),
        grid=(batch // batch_block,),
        in_specs=[
            pl.BlockSpec((rows, Z_W), lambda i: (i, 0)),
            sg_spec, sh_spec,
            full((R1_PAD, GLA_KDIM)), full((1, GLA_KDIM)), full((1, GLA_DV)),
            full((2, HGRN_FDIM)), full((1, HGRN_DV)),
        ],
        out_specs=[pl.BlockSpec((rows, D_MODEL), lambda i: (i, 0)), sg_spec, sh_spec],
        out_shape=[
            jax.ShapeDtypeStruct((batch * SAMPLE_PAD_T, D_MODEL), f32),
            jax.ShapeDtypeStruct(state_gla.shape, f32),
            jax.ShapeDtypeStruct(state_hgrn.shape, f32),
        ],
        compiler_params=pltpu.CompilerParams(
            dimension_semantics=("arbitrary",), vmem_limit_bytes=VMEM_LIMIT),
        name="mixer_sample",
    )(z2, state_gla, state_hgrn, wgk, bgk, gng, lbl, gnh)


def _store_slabs(ref, x):
    for s in range(SLABS):
        ref[pl.ds(s, x.shape[0], stride=SLABS), :] = x[:, s * LANES:(s + 1) * LANES]


def _load_slab_cols(ref, start, rows, s):
    return ref[pl.ds(start + s, rows, stride=SLABS), :]


def _outproj_kernel(xp_ref, xs_ref, op_ref, os_ref, wo_ref, g_ref, wr_ref, br_ref,
                    h_ref, xn_ref, ti_ref, tg_ref, rank_ref, cnt_ref, run_cnt, *, n_prompt_tiles):
    i = pl.program_id(0)

    @pl.when(i == 0)
    def _():
        run_cnt[...] = jnp.zeros_like(run_cnt)

    x = _select_rows(i < n_prompt_tiles, xp_ref, xs_ref)
    o = _select_rows(i < n_prompt_tiles, op_ref, os_ref)
    h = x + _dot(o, wo_ref[...])
    h_ref[...] = h
    xn = _rms(h, g_ref[...])
    _store_slabs(xn_ref, xn)
    logits = _dot(xn.astype(bf16), wr_ref[...]) + br_ref[...]
    lane = lax.broadcasted_iota(i32, logits.shape, 1)
    neg = jnp.float32(-jnp.inf)
    l = jnp.where(lane < N_EXPERTS, logits, neg)
    vals, idxs = [], []
    for _ in range(TOP_K):
        m = jnp.max(l, axis=-1, keepdims=True)
        idx = jnp.min(jnp.where(l == m, lane, LANES), axis=-1, keepdims=True)
        vals.append(m)
        idxs.append(idx)
        l = jnp.where(lane == idx, neg, l)
    es = [jnp.exp(v - vals[0]) for v in vals]
    denom = es[0] + es[1] + es[2] + es[3]

    onehot = jnp.zeros(logits.shape, f32)
    for k in range(TOP_K):
        onehot = onehot + (lane == idxs[k]).astype(f32)
    rows = logits.shape[0]
    lower = (lax.broadcasted_iota(i32, (rows, rows), 0)
             > lax.broadcasted_iota(i32, (rows, rows), 1)).astype(bf16)
    ahead = _dot(lower, onehot.astype(bf16)) + run_cnt[...]
    run_cnt[...] = run_cnt[...] + jnp.sum(onehot, axis=0, keepdims=True)
    cnt_ref[...] = run_cnt[...].astype(i32)

    ti = jnp.zeros(logits.shape, i32)
    tg = jnp.zeros(logits.shape, f32)
    rank = jnp.zeros(logits.shape, i32)
    for k in range(TOP_K):
        rk = jnp.sum(jnp.where(lane == idxs[k], ahead, 0.0), axis=-1, keepdims=True).astype(i32)
        ti = jnp.where(lane == k, idxs[k], ti)
        tg = jnp.where(lane == k, es[k] / denom, tg)
        rank = jnp.where(lane == k, rk, rank)
    ti_ref[...] = ti
    tg_ref[...] = tg
    rank_ref[...] = rank


def _out_proj(xp, xs, o_p, o_s, wo, g, wr, br):
    tp, ts = xp.shape[0], xs.shape[0]
    npt = tp // TOKEN_TILE
    nt = npt + ts // TOKEN_TILE
    t = tp + ts
    row = lambda w: pl.BlockSpec((TOKEN_TILE, w), lambda i: (i, 0))
    full = lambda shape: pl.BlockSpec(shape, lambda i: (0,) * len(shape))
    prompt_rows = pl.BlockSpec((TOKEN_TILE, D_MODEL), lambda i: (jnp.minimum(i, npt - 1), 0))
    sample_rows = pl.BlockSpec((TOKEN_TILE, D_MODEL), lambda i: (jnp.maximum(i - npt, 0), 0))
    return pl.pallas_call(
        functools.partial(_outproj_kernel, n_prompt_tiles=npt),
        grid=(nt,),
        in_specs=[
            prompt_rows, sample_rows, prompt_rows, sample_rows,
            full((D_MODEL, D_MODEL)), full((1, D_MODEL)),
            full((D_MODEL, LANES)), full((1, LANES)),
        ],
        out_specs=[row(D_MODEL), pl.BlockSpec((TOKEN_TILE * SLABS, LANES), lambda i: (i, 0)),
                   row(LANES), row(LANES), row(LANES), full((1, LANES))],
        out_shape=[
            jax.ShapeDtypeStruct((t, D_MODEL), f32),
            jax.ShapeDtypeStruct((t * SLABS, LANES), f32),
            jax.ShapeDtypeStruct((t, LANES), i32),
            jax.ShapeDtypeStruct((t, LANES), f32),
            jax.ShapeDtypeStruct((t, LANES), i32),
            jax.ShapeDtypeStruct((1, LANES), i32),
        ],
        scratch_shapes=[pltpu.VMEM((1, LANES), f32)],
        compiler_params=pltpu.CompilerParams(
            dimension_semantics=("arbitrary",), vmem_limit_bytes=VMEM_LIMIT),
        name="out_proj_router",
    )(xp, xs, o_p, o_s, wo, g, wr, br)


def _dispatch_kernel(bv_ref, nz_ref, pos_ref, xn_hbm, xs_hbm, zeros, stage, in_sems, out_sems, zsem,
                     *, n_blocks):
    @pl.when(pl.program_id(0) == 0)
    def _():
        zeros[...] = jnp.zeros_like(zeros)

        def fill(b, carry):
            @pl.when(bv_ref[b] < MOE_TILE)
            def _():
                r0 = pl.multiple_of(b * (MOE_TILE * SLABS), MOE_TILE * SLABS)
                pltpu.make_async_copy(zeros, xs_hbm.at[pl.ds(r0, MOE_TILE * SLABS)], zsem).start()
            return carry

        def drain(b, carry):
            pltpu.make_async_copy(zeros, xs_hbm.at[pl.ds(0, MOE_TILE * SLABS)], zsem).wait()
            return carry

        lax.fori_loop(0, n_blocks, fill, 0)
        lax.fori_loop(0, nz_ref[0], drain, 0)

    i = pl.program_id(0)
    nt = pl.num_programs(0)
    tile_rows = DISPATCH_TILE * SLABS

    def fetch(t, slot):
        return pltpu.make_async_copy(
            xn_hbm.at[pl.ds(pl.multiple_of(t * tile_rows, tile_rows), tile_rows)],
            stage.at[pl.ds(pl.multiple_of(slot * tile_rows, tile_rows), tile_rows)], in_sems.at[slot])

    def wait_rows(slot):
        for _ in range(TOP_K):
            pltpu.make_async_copy(stage.at[pl.ds(0, tile_rows)], xs_hbm.at[pl.ds(0, tile_rows)],
                                  out_sems.at[slot]).wait()

    @pl.when(i == 0)
    def _():
        fetch(0, 0).start()

    @pl.when(i + 1 < nt)
    def _():
        fetch(i + 1, (i + 1) % 3).start()

    fetch(i, i % 3).wait()
    base = (i % 3) * tile_rows

    def start(r, carry):
        src = stage.at[pl.ds(pl.multiple_of(base + r * SLABS, SLABS), SLABS)]
        for k in range(TOP_K):
            dst0 = pl.multiple_of(pos_ref[0, 0, r * TOP_K + k], SLABS)
            pltpu.make_async_copy(src, xs_hbm.at[pl.ds(dst0, SLABS)], out_sems.at[i % 2]).start(priority=k % 2)
        return carry

    lax.fori_loop(0, DISPATCH_TILE, start, 0, unroll=2)

    @pl.when(i > 0)
    def _():
        wait_rows((i + 1) % 2)

    @pl.when(i == nt - 1)
    def _():
        wait_rows(i % 2)


def _dispatch(block_valid, pos, xn2):
    t = xn2.shape[0] // SLABS
    nt = t // DISPATCH_TILE
    n_blocks = block_valid.shape[0]
    n_zero = jnp.sum((block_valid < MOE_TILE).astype(i32)).reshape(1)
    grid_spec = pltpu.PrefetchScalarGridSpec(
        num_scalar_prefetch=2,
        grid=(nt,),
        in_specs=[
            pl.BlockSpec((1, 1, DISPATCH_TILE * TOP_K), lambda i, bv, nz: (i, 0, 0),
                         memory_space=pltpu.SMEM),
            pl.BlockSpec(memory_space=pl.ANY),
        ],
        out_specs=pl.BlockSpec(memory_space=pl.ANY),
        scratch_shapes=[
            pltpu.VMEM((MOE_TILE * SLABS, LANES), f32),
            pltpu.VMEM((3 * DISPATCH_TILE * SLABS, LANES), f32),
            pltpu.SemaphoreType.DMA((3,)),
            pltpu.SemaphoreType.DMA((2,)),
            pltpu.SemaphoreType.DMA(()),
        ],
    )
    return pl.pallas_call(
        functools.partial(_dispatch_kernel, n_blocks=n_blocks),
        grid_spec=grid_spec,
        out_shape=jax.ShapeDtypeStruct((n_blocks * MOE_TILE * SLABS, LANES), f32),
        compiler_params=pltpu.CompilerParams(
            dimension_semantics=("arbitrary",), vmem_limit_bytes=VMEM_LIMIT),
        name="moe_dispatch",
    )(block_valid, n_zero, pos.reshape(nt, 1, DISPATCH_TILE * TOP_K), xn2)


def _expert_kernel(be_ref, nb_ref, x_ref, wgu_ref, bgu_ref, wdn_ref, bdn_ref, y_ref, wgu_bf, wdn_bf,
                   *, ff_chunk, cast_rows):
    i = pl.program_id(0)

    @pl.when(i >= nb_ref[0])
    def _():
        y_ref[...] = jnp.zeros_like(y_ref)

    @pl.when((i < nb_ref[0]) & ((i == 0) | (be_ref[i] != be_ref[jnp.maximum(i - 1, 0)])))
    def _():
        def cast(j, carry):
            r0 = pl.multiple_of(j * cast_rows, cast_rows)
            wgu_bf[pl.ds(r0, cast_rows), :] = wgu_ref[0, pl.ds(r0, cast_rows), :].astype(bf16)
            wdn_bf[pl.ds(r0, cast_rows), :] = wdn_ref[0, pl.ds(r0, cast_rows), :].astype(bf16)
            return carry
        lax.fori_loop(0, D_MODEL // cast_rows, cast, 0)

    @pl.when(i < nb_ref[0])
    def _():
        xb = jnp.concatenate(
            [_load_slab_cols(x_ref, 0, MOE_TILE, s).astype(bf16) for s in range(SLABS)], axis=1)
        acc = jnp.zeros((MOE_TILE, D_MODEL), f32)
        for c0 in range(0, D_FF, ff_chunk):
            gate = _dot(xb, wgu_bf[:, c0:c0 + ff_chunk]) + bgu_ref[0, :, c0:c0 + ff_chunk]
            up = (_dot(xb, wgu_bf[:, D_FF + c0:D_FF + c0 + ff_chunk])
                  + bgu_ref[0, :, D_FF + c0:D_FF + c0 + ff_chunk])
            gate = jnp.minimum(gate, SWIGLU_LIMIT)
            up = jnp.clip(up, -SWIGLU_LIMIT, SWIGLU_LIMIT)
            hid = (up + 1.0) * (gate * _sigmoid(gate * SWIGLU_ALPHA))
            acc = acc + _dot(hid.astype(bf16), wdn_bf[c0:c0 + ff_chunk, :])
        _store_slabs(y_ref, acc + bdn_ref[0])


def _experts(block_e, n_used, xs, wgu, bgu, wdn, bdn):
    n_blocks = block_e.shape[0]
    grid_spec = pltpu.PrefetchScalarGridSpec(
        num_scalar_prefetch=2,
        grid=(n_blocks,),
        in_specs=[
            pl.BlockSpec((MOE_TILE * SLABS, LANES), lambda i, be, nb: (jnp.minimum(i, nb[0] - 1), 0)),
            pl.BlockSpec((1, D_MODEL, 2 * D_FF), lambda i, be, nb: (be[i], 0, 0)),
            pl.BlockSpec((1, 1, 2 * D_FF), lambda i, be, nb: (be[i], 0, 0)),
            pl.BlockSpec((1, D_FF, D_MODEL), lambda i, be, nb: (be[i], 0, 0)),
            pl.BlockSpec((1, 1, D_MODEL), lambda i, be, nb: (be[i], 0, 0)),
        ],
        out_specs=pl.BlockSpec((MOE_TILE * SLABS, LANES), lambda i, be, nb: (i, 0)),
        scratch_shapes=[
            pltpu.VMEM((D_MODEL, 2 * D_FF), bf16),
            pltpu.VMEM((D_FF, D_MODEL), bf16),
        ],
    )
    return pl.pallas_call(
        functools.partial(_expert_kernel, ff_chunk=256, cast_rows=128),
        grid_spec=grid_spec,
        out_shape=jax.ShapeDtypeStruct((n_blocks * MOE_TILE * SLABS, LANES), f32),
        compiler_params=pltpu.CompilerParams(
            dimension_semantics=("arbitrary",), vmem_limit_bytes=VMEM_LIMIT),
        name="moe_experts",
    )(block_e, n_used, xs, wgu, bgu, wdn, bdn)


def _combine_kernel(pos_cur, pos_nxt, ys_hbm, h_ref, tg_ref, pp_ref, ps_ref, gple_ref, wpg_ref, wpp_ref,
                    gfin_ref, yp_ref, ys_ref, rows, sems, *, n_prompt_tiles):
    i = pl.program_id(0)
    nt = pl.num_programs(0)
    n = COMBINE_TILE * TOP_K
    slot_rows = n * SLABS

    def gather(pos_ref, slot):
        def start(j, carry):
            for u in range(2):
                r = 2 * j + u
                src0 = pl.multiple_of(pos_ref[0, 0, r], SLABS)
                dst0 = pl.multiple_of(slot * slot_rows + r * SLABS, SLABS)
                pltpu.make_async_copy(ys_hbm.at[pl.ds(src0, SLABS)], rows.at[pl.ds(dst0, SLABS)],
                                      sems.at[slot]).start(priority=u)
            return carry
        lax.fori_loop(0, n // 2, start, 0, unroll=4)

    @pl.when(i == 0)
    def _():
        gather(pos_cur, 0)

    @pl.when(i + 1 < nt)
    def _():
        gather(pos_nxt, (i + 1) % 2)

    slot = i % 2
    base = pl.multiple_of(slot * slot_rows, slot_rows)
    pltpu.make_async_copy(ys_hbm.at[pl.ds(0, slot_rows)], rows.at[pl.ds(base, slot_rows)],
                          sems.at[slot]).wait()

    tg = tg_ref[...]
    cols = []
    for s in range(SLABS):
        moe = jnp.zeros((COMBINE_TILE, LANES), f32)
        for k in range(TOP_K):
            moe = moe + tg[:, k:k + 1] * _load_slab_cols(
                rows, base + k * COMBINE_TILE * SLABS, COMBINE_TILE, s)
        cols.append(h_ref[:, s * LANES:(s + 1) * LANES] + moe)
    h = jnp.concatenate(cols, axis=1)
    gate = _sigmoid(_dot(_rms(h, gple_ref[...]).astype(bf16), wpg_ref[...]))
    p = _select_rows(i < n_prompt_tiles, pp_ref, ps_ref).astype(bf16)
    h = h + gate * _dot(p, wpp_ref[...])
    y = _rms(h, gfin_ref[...])

    @pl.when(i < n_prompt_tiles)
    def _():
        yp_ref[...] = y

    @pl.when(i >= n_prompt_tiles)
    def _():
        ys_ref[...] = y


def _combine(pos, ys, h1, tg, pp, ps, gple, wpg, wpp, gfin):
    t = h1.shape[0]
    nt = t // COMBINE_TILE
    npt = pp.shape[0] // COMBINE_TILE
    n = COMBINE_TILE * TOP_K
    pos_tiles = pos.reshape(nt, COMBINE_TILE, TOP_K).transpose(0, 2, 1).reshape(nt, 1, n)
    row = lambda w: pl.BlockSpec((COMBINE_TILE, w), lambda i: (i, 0))
    full = lambda shape: pl.BlockSpec(shape, lambda i: (0,) * len(shape))
    return pl.pallas_call(
        functools.partial(_combine_kernel, n_prompt_tiles=npt),
        grid=(nt,),
        in_specs=[
            pl.BlockSpec((1, 1, n), lambda i: (i, 0, 0), memory_space=pltpu.SMEM),
            pl.BlockSpec((1, 1, n), lambda i: (jnp.minimum(i + 1, nt - 1), 0, 0), memory_space=pltpu.SMEM),
            pl.BlockSpec(memory_space=pl.ANY),
            row(D_MODEL), row(LANES),
            pl.BlockSpec((COMBINE_TILE, PLE_DIM), lambda i: (jnp.minimum(i, npt - 1), 0)),
            pl.BlockSpec((COMBINE_TILE, PLE_DIM), lambda i: (jnp.maximum(i - npt, 0), 0)),
            full((1, D_MODEL)), full((D_MODEL, D_MODEL)), full((PLE_DIM, D_MODEL)),
            full((1, D_MODEL)),
        ],
        out_specs=[
            pl.BlockSpec((COMBINE_TILE, D_MODEL), lambda i: (jnp.minimum(i, npt - 1), 0)),
            pl.BlockSpec((COMBINE_TILE, D_MODEL), lambda i: (jnp.maximum(i - npt, 0), 0)),
        ],
        out_shape=[
            jax.ShapeDtypeStruct((pp.shape[0], D_MODEL), f32),
            jax.ShapeDtypeStruct((ps.shape[0], D_MODEL), f32),
        ],
        scratch_shapes=[
            pltpu.VMEM((2 * n * SLABS, LANES), f32),
            pltpu.SemaphoreType.DMA((2,)),
        ],
        compiler_params=pltpu.CompilerParams(
            dimension_semantics=("arbitrary",), vmem_limit_bytes=VMEM_LIMIT),
        name="moe_combine_ple",
    )(pos_tiles, pos_tiles, ys, h1, tg, pp, ps, gple, wpg, wpp, gfin)


def _routing_tables(top_i, rank, counts, n_blocks):
    padded = (counts + MOE_TILE - 1) // MOE_TILE * MOE_TILE
    pend = jnp.cumsum(padded)
    pstart = pend - padded
    block_rows = jnp.arange(n_blocks, dtype=i32) * MOE_TILE
    block_e = jnp.minimum(
        jnp.sum((block_rows[:, None] >= pend[None, :]).astype(i32), axis=1), N_EXPERTS - 1)
    block_valid = jnp.clip(pstart[block_e] + counts[block_e] - block_rows, 0, MOE_TILE)
    n_used = pend[-1:] // MOE_TILE
    seg = jnp.zeros_like(rank)
    for e in range(N_EXPERTS):
        seg = jnp.where(top_i == e, pstart[e], seg)
    pos = (rank + seg) * SLABS
    return pos.astype(i32), block_e.astype(i32), block_valid.astype(i32), n_used.astype(i32)


def kernel(x_prompt, x_sample, p_prompt, p_sample, state_gla, state_hgrn, g_mix, w_in, w_gk2, b_gk2,
           gn_gla, lb_logits, gn_hgrn, w_out, g_ffn, w_router, b_router, w_gu, b_gu, w_dn, b_dn,
           g_ple, w_ple_gate, w_ple_proj, g_final):
    batch, seq, _ = x_prompt.shape
    dec_batch, dec_seq, _ = x_sample.shape
    tp, ts = batch * seq, dec_batch * dec_seq
    xp = x_prompt.reshape(tp, D_MODEL)
    xs = x_sample.reshape(ts, D_MODEL)

    r1_end = Z_R1 + GLA_LOWRANK
    w_in_p = jnp.concatenate(
        [w_in[0, :, :r1_end], jnp.zeros((D_MODEL, R1_PAD - GLA_LOWRANK), f32), w_in[0, :, r1_end:]],
        axis=1).astype(bf16)
    wgk = jnp.concatenate(
        [w_gk2[0], jnp.zeros((R1_PAD - GLA_LOWRANK, GLA_KDIM), f32)], axis=0).astype(bf16)
    wr = jnp.concatenate(
        [w_router[0], jnp.zeros((D_MODEL, LANES - N_EXPERTS), f32)], axis=1).astype(bf16)
    br = jnp.concatenate([b_router[0], jnp.zeros((LANES - N_EXPERTS,), f32)]).reshape(1, LANES)

    z = _in_proj(xp, xs, g_mix[0].reshape(1, D_MODEL), w_in_p)

    mixer_params = (wgk, b_gk2[0].reshape(1, GLA_KDIM), gn_gla[0].reshape(1, GLA_DV),
                    lb_logits[0:2], gn_hgrn[0].reshape(1, HGRN_DV))
    o_p, new_gla_p, new_hgrn_p = _mixer_prompt(
        z, *mixer_params, batch=batch, seq=seq, rows_per_step=512, block_rows=256)
    z_s = jnp.pad(z[tp:].reshape(dec_batch, dec_seq, Z_W), ((0, 0), (0, SAMPLE_PAD_T - dec_seq), (0, 0)))
    o_s, new_gla_s, new_hgrn_s = _mixer_sample(
        z_s.reshape(dec_batch * SAMPLE_PAD_T, Z_W), state_gla, state_hgrn, *mixer_params,
        steps=dec_seq, batch_block=8)
    o_s = o_s.reshape(dec_batch, SAMPLE_PAD_T, D_MODEL)[:, :dec_seq].reshape(ts, D_MODEL).astype(bf16)

    h1, xn2, top_i, top_g, rank, counts = _out_proj(
        xp, xs, o_p, o_s, w_out[0].astype(bf16), g_ffn[0].reshape(1, D_MODEL), wr, br)

    t = tp + ts
    n_blocks = -(-(t * TOP_K) // MOE_TILE) + N_EXPERTS
    pos, block_e, block_valid, n_used = _routing_tables(
        top_i[:, :TOP_K], rank[:, :TOP_K], counts[0, :N_EXPERTS], n_blocks)
    x_sorted = _dispatch(block_valid, pos, xn2)
    y_sorted = _experts(block_e, n_used, x_sorted, w_gu[0], b_gu[0].reshape(N_EXPERTS, 1, 2 * D_FF),
                        w_dn[0], b_dn[0].reshape(N_EXPERTS, 1, D_MODEL))

    y_p, y_s = _combine(pos, y_sorted, h1, top_g, p_prompt[0].reshape(tp, PLE_DIM),
                        p_sample[0].reshape(ts, PLE_DIM), g_ple[0].reshape(1, D_MODEL),
                        w_ple_gate[0].astype(bf16), w_ple_proj[0].astype(bf16), g_final.reshape(1, D_MODEL))

    return (y_p.reshape(batch, seq, D_MODEL), y_s.reshape(dec_batch, dec_seq, D_MODEL),
            new_gla_p, new_hgrn_p, new_gla_s, new_hgrn_s)
```

```python
import functools

import jax
import jax.numpy as jnp
from jax import lax
from jax.experimental import pallas as pl
from jax.experimental.pallas import tpu as pltpu

f32 = jnp.float32
bf16 = jnp.bfloat16
i32 = jnp.int32

D_MODEL = 1024
GLA_HEADS = 4
GLA_DK = 64
GLA_DV = 128
GLA_KDIM = GLA_HEADS * GLA_DK
GLA_WIDTH = GLA_HEADS * GLA_DV
GLA_LOWRANK = 16
GLA_NORMALIZER = 16.0
HGRN_HEADS = 4
HGRN_DK = 128
HGRN_DV = 128
HGRN_FDIM = HGRN_HEADS * HGRN_DK
HGRN_WIDTH = HGRN_HEADS * HGRN_DV
CHUNK = 64
N_EXPERTS = 32
TOP_K = 4
D_FF = 1024
SWIGLU_LIMIT = 7.0
SWIGLU_ALPHA = 1.702
PLE_DIM = 256
EPS = 1e-6

LANES = 128
SAMPLE_PAD_T = 8
SLABS = D_MODEL // LANES

R1_PAD = LANES
Z_Q1 = 0
Z_K1 = Z_Q1 + GLA_KDIM
Z_V1 = Z_K1 + GLA_KDIM
Z_G1 = Z_V1 + GLA_WIDTH
Z_R1 = Z_G1 + GLA_WIDTH
Z_Q2 = Z_R1 + R1_PAD
Z_F2 = Z_Q2 + HGRN_FDIM
Z_I2 = Z_F2 + HGRN_FDIM
Z_G2 = Z_I2 + HGRN_WIDTH
Z_W = Z_G2 + HGRN_WIDTH

TOKEN_TILE = 512
MOE_TILE = 512
DISPATCH_TILE = 256
COMBINE_TILE = 256
VMEM_LIMIT = 56 * 1024 * 1024


def _rms(x, g):
    return x * lax.rsqrt(jnp.mean(x * x, axis=-1, keepdims=True) + EPS) * g


def _sigmoid(x):
    return 1.0 / (1.0 + jnp.exp(-x))


def _dot(a, b):
    return jnp.dot(a, b, preferred_element_type=f32)


def _dot_nt(a, b):
    return lax.dot_general(a, b, (((1,), (1,)), ((), ())), preferred_element_type=f32)


def _dot_tn(a, b):
    return lax.dot_general(a, b, (((0,), (0,)), ((), ())), preferred_element_type=f32)


def _cumsum_rows(tri, x):
    hi = x.astype(bf16)
    r1 = x - hi.astype(f32)
    mid = r1.astype(bf16)
    lo = (r1 - mid.astype(f32)).astype(bf16)
    return _dot(tri, hi) + _dot(tri, mid) + _dot(tri, lo)


def _select_rows(is_prompt, a_ref, b_ref):
    return jnp.where(is_prompt, a_ref[...], b_ref[...])


def _inproj_kernel(xp_ref, xs_ref, g_ref, w_ref, z_ref, *, n_prompt_tiles, col_chunk):
    is_prompt = pl.program_id(0) < n_prompt_tiles
    x = _select_rows(is_prompt, xp_ref, xs_ref)
    a = _rms(x, g_ref[...]).astype(bf16)
    for c0 in range(0, Z_W, col_chunk):
        c1 = min(c0 + col_chunk, Z_W)
        z_ref[:, c0:c1] = _dot(a, w_ref[:, c0:c1]).astype(bf16)


def _in_proj(xp, xs, g, w):
    tp, ts = xp.shape[0], xs.shape[0]
    npt = tp // TOKEN_TILE
    nt = npt + ts // TOKEN_TILE
    return pl.pallas_call(
        functools.partial(_inproj_kernel, n_prompt_tiles=npt, col_chunk=512),
        grid=(nt,),
        in_specs=[
            pl.BlockSpec((TOKEN_TILE, D_MODEL), lambda i: (jnp.minimum(i, npt - 1), 0)),
            pl.BlockSpec((TOKEN_TILE, D_MODEL), lambda i: (jnp.maximum(i - npt, 0), 0)),
            pl.BlockSpec((1, D_MODEL), lambda i: (0, 0)),
            pl.BlockSpec((D_MODEL, Z_W), lambda i: (0, 0)),
        ],
        out_specs=pl.BlockSpec((TOKEN_TILE, Z_W), lambda i: (i, 0)),
        out_shape=jax.ShapeDtypeStruct((tp + ts, Z_W), bf16),
        compiler_params=pltpu.CompilerParams(
            dimension_semantics=("arbitrary",), vmem_limit_bytes=VMEM_LIMIT),
        name="in_proj",
    )(xp, xs, g, w)


def _block_consts(r, c):
    shift = c.bit_length() - 1
    row = lax.broadcasted_iota(i32, (r, r), 0)
    col = lax.broadcasted_iota(i32, (r, r), 1)
    causal = (lax.shift_right_logical(row, shift) == lax.shift_right_logical(col, shift)) & (row >= col)
    return causal.astype(bf16), causal


def _subchunk_row(x, c, idx):
    parts = [jnp.broadcast_to(x[j * c + idx:j * c + idx + 1, :], (c, x.shape[1]))
             for j in range(x.shape[0] // c)]
    return jnp.concatenate(parts, axis=0)


def _gated_block(q, k, v, log_a, states, *, c, scale, ref_row, tri, causal, heads_per_group, scan):
    r = q.shape[0]
    nsub = r // c
    groups = q.shape[1] // LANES
    b = _cumsum_rows(tri, log_a)
    b_ref = _subchunk_row(b, c, ref_row)
    b_last = _subchunk_row(b, c, c - 1)
    qs = q * scale
    q_r = qs * jnp.exp(b - b_ref)
    k_r = (k * jnp.exp(b_ref - b)).astype(bf16)
    k_end = k * jnp.exp(b_last - b)
    q_b = qs * jnp.exp(b)
    decay = jnp.exp(b_last)
    vb = v.astype(bf16)
    lane = lax.broadcasted_iota(i32, (r, LANES), 1)
    width = LANES // heads_per_group
    outs, new_states = [], []
    for g in range(groups):
        gl = slice(g * LANES, (g + 1) * LANES)
        masks = [None] if heads_per_group == 1 else [
            (lane >= j * width) & (lane < (j + 1) * width) for j in range(heads_per_group)]
        pick = lambda x, m: x if m is None else jnp.where(m, x, 0.0)
        q_r_h = [pick(q_r[:, gl], m).astype(bf16) for m in masks]
        q_b_h = [pick(q_b[:, gl], m).astype(bf16) for m in masks]
        k_end_h = [pick(k_end[:, gl], m).astype(bf16) for m in masks]
        v_h = [vb[:, (g * heads_per_group + j) * LANES:(g * heads_per_group + j + 1) * LANES]
               for j in range(heads_per_group)]
        o_intra = []
        for j in range(heads_per_group):
            att = _dot_nt(q_r_h[j], k_r[:, gl])
            att = jnp.where(causal, att, 0.0).astype(bf16)
            o_intra.append(_dot(att, v_h[j]))
        upd = []
        for s in range(nsub):
            rs = slice(s * c, (s + 1) * c)
            u = _dot_tn(v_h[0][rs], k_end_h[0][rs])
            for j in range(1, heads_per_group):
                u = u + _dot_tn(v_h[j][rs], k_end_h[j][rs])
            upd.append(u)
        o_inter = [[] for _ in range(heads_per_group)]
        st = states[g] if scan else None
        group_states = []
        for s in range(nsub):
            rs = slice(s * c, (s + 1) * c)
            if not scan:
                st = states[s][g]
            st_b = st.astype(bf16)
            for j in range(heads_per_group):
                o_inter[j].append(_dot_nt(q_b_h[j][rs], st_b))
            st = st * decay[s * c:s * c + 1, gl] + upd[s]
            if not scan:
                group_states.append(st)
        for j in range(heads_per_group):
            outs.append(o_intra[j] + jnp.concatenate(o_inter[j], axis=0))
        new_states.append(st if scan else group_states)
    if not scan:
        new_states = [[new_states[g][s] for g in range(groups)] for s in range(nsub)]
    return outs, new_states


def _head_norm(o, g):
    return o * lax.rsqrt(jnp.mean(o * o, axis=-1, keepdims=True) + EPS) * g


def _mixer_block(zb, st_gla, st_hgrn, wgk, bgk, gng, lb, gnh, *, c, ref_row, tri, causal, valid, scan):
    z = zb.astype(f32)
    gk = _dot(zb[:, Z_R1:Z_R1 + R1_PAD], wgk) + bgk
    log_a = (jnp.minimum(gk, 0.0) - jnp.log(1.0 + jnp.exp(-jnp.abs(gk)))) / GLA_NORMALIZER
    k1 = z[:, Z_K1:Z_K1 + GLA_KDIM]
    if valid is not None:
        log_a = jnp.where(valid, log_a, 0.0)
        k1 = jnp.where(valid, k1, 0.0)
    o1, st_gla = _gated_block(
        z[:, Z_Q1:Z_Q1 + GLA_KDIM], k1, z[:, Z_V1:Z_V1 + GLA_WIDTH], log_a, st_gla,
        c=c, scale=GLA_DK ** -0.5, ref_row=ref_row, tri=tri, causal=causal,
        heads_per_group=LANES // GLA_DK, scan=scan)
    forget = lb + (1.0 - lb) * _sigmoid(z[:, Z_F2:Z_F2 + HGRN_FDIM])
    k2 = 1.0 - forget
    log_f = jnp.log(forget)
    if valid is not None:
        log_f = jnp.where(valid, log_f, 0.0)
        k2 = jnp.where(valid, k2, 0.0)
    q2 = z[:, Z_Q2:Z_Q2 + HGRN_FDIM]
    i2 = z[:, Z_I2:Z_I2 + HGRN_WIDTH]
    o2, st_hgrn = _gated_block(
        q2 * _sigmoid(q2), k2, i2 * _sigmoid(i2), log_f, st_hgrn,
        c=c, scale=1.0, ref_row=ref_row, tri=tri, causal=causal, heads_per_group=1, scan=scan)
    cols = []
    for h in range(GLA_HEADS):
        g1 = z[:, Z_G1 + h * GLA_DV:Z_G1 + (h + 1) * GLA_DV]
        cols.append(_head_norm(o1[h], gng) * (g1 * _sigmoid(g1)))
    for h in range(HGRN_HEADS):
        g2 = z[:, Z_G2 + h * HGRN_DV:Z_G2 + (h + 1) * HGRN_DV]
        cols.append(_head_norm(o2[h], gnh) * _sigmoid(g2))
    return jnp.concatenate(cols, axis=1), st_gla, st_hgrn


def _lower_bound(lbl_ref):
    l = lbl_ref[...]
    m = jnp.max(l, axis=0, keepdims=True)
    e = jnp.exp(l - m)
    return e[0:1, :] / jnp.sum(e, axis=0, keepdims=True)


GLA_GROUPS = GLA_KDIM // LANES


def _gla_state_t(s_ref_heads):
    return s_ref_heads.reshape(LANES, GLA_DV).T


def _gla_state_from_t(st):
    return st.T.reshape(LANES // GLA_DK, GLA_DK, GLA_DV)


def _mixer_prompt_kernel(z_ref, wgk_ref, bgk_ref, gng_ref, lbl_ref, gnh_ref,
                         o_ref, sg_out, sh_out, sg_t, sh_t, *, block_rows):
    j = pl.program_id(1)

    @pl.when(j == 0)
    def _():
        sg_t[...] = jnp.zeros_like(sg_t)
        sh_t[...] = jnp.zeros_like(sh_t)

    tri, causal = _block_consts(block_rows, CHUNK)
    lb = _lower_bound(lbl_ref)
    wgk, bgk, gng, gnh = wgk_ref[...], bgk_ref[...], gng_ref[...], gnh_ref[...]

    def body(blk, carry):
        r0 = pl.multiple_of(blk * block_rows, block_rows)
        o, st_g, st_h = _mixer_block(
            z_ref[pl.ds(r0, block_rows), :], [sg_t[g] for g in range(GLA_GROUPS)],
            [sh_t[h] for h in range(HGRN_HEADS)], wgk, bgk, gng, lb, gnh,
            c=CHUNK, ref_row=CHUNK // 2, tri=tri, causal=causal, valid=None, scan=True)
        o_ref[pl.ds(r0, block_rows), :] = o.astype(bf16)
        for g in range(GLA_GROUPS):
            sg_t[g] = st_g[g]
        for h in range(HGRN_HEADS):
            sh_t[h] = st_h[h]
        return carry

    lax.fori_loop(0, z_ref.shape[0] // block_rows, body, 0)

    @pl.when(j == pl.num_programs(1) - 1)
    def _():
        per = LANES // GLA_DK
        for g in range(GLA_GROUPS):
            sg_out[0, 0, g * per:(g + 1) * per] = _gla_state_from_t(sg_t[g])
        for h in range(HGRN_HEADS):
            sh_out[0, 0, h] = sh_t[h].T


def _mixer_prompt(z, wgk, bgk, gng, lbl, gnh, *, batch, seq, rows_per_step, block_rows):
    nj = seq // rows_per_step
    full = lambda shape: pl.BlockSpec(shape, lambda b, j: (0,) * len(shape))
    return pl.pallas_call(
        functools.partial(_mixer_prompt_kernel, block_rows=block_rows),
        grid=(batch, nj),
        in_specs=[
            pl.BlockSpec((rows_per_step, Z_W), lambda b, j: (b * nj + j, 0)),
            full((R1_PAD, GLA_KDIM)), full((1, GLA_KDIM)), full((1, GLA_DV)),
            full((2, HGRN_FDIM)), full((1, HGRN_DV)),
        ],
        out_specs=[
            pl.BlockSpec((rows_per_step, D_MODEL), lambda b, j: (b * nj + j, 0)),
            pl.BlockSpec((1, 1, GLA_HEADS, GLA_DK, GLA_DV), lambda b, j: (0, b, 0, 0, 0)),
            pl.BlockSpec((1, 1, HGRN_HEADS, HGRN_DK, HGRN_DV), lambda b, j: (0, b, 0, 0, 0)),
        ],
        out_shape=[
            jax.ShapeDtypeStruct((batch * seq, D_MODEL), bf16),
            jax.ShapeDtypeStruct((1, batch, GLA_HEADS, GLA_DK, GLA_DV), f32),
            jax.ShapeDtypeStruct((1, batch, HGRN_HEADS, HGRN_DK, HGRN_DV), f32),
        ],
        scratch_shapes=[
            pltpu.VMEM((GLA_GROUPS, GLA_DV, LANES), f32),
            pltpu.VMEM((HGRN_HEADS, HGRN_DV, HGRN_DK), f32),
        ],
        compiler_params=pltpu.CompilerParams(
            dimension_semantics=("arbitrary", "arbitrary"), vmem_limit_bytes=VMEM_LIMIT),
        name="mixer_prompt",
    )(z, wgk, bgk, gng, lbl, gnh)


def _mixer_sample_kernel(z_ref, sg_in, sh_in, wgk_ref, bgk_ref, gng_ref, lbl_ref, gnh_ref,
                         o_ref, sg_out, sh_out, *, batch_block, steps):
    rows = batch_block * SAMPLE_PAD_T
    tri, causal = _block_consts(rows, SAMPLE_PAD_T)
    step = lax.broadcasted_iota(i32, (rows, 1), 0) & (SAMPLE_PAD_T - 1)
    valid = step < steps
    lb = _lower_bound(lbl_ref)
    per = LANES // GLA_DK
    st_g = [[_gla_state_t(sg_in[0, s, g * per:(g + 1) * per]) for g in range(GLA_GROUPS)]
            for s in range(batch_block)]
    st_h = [[sh_in[0, s, h].T for h in range(HGRN_HEADS)] for s in range(batch_block)]
    o, st_g, st_h = _mixer_block(
        z_ref[...], st_g, st_h, wgk_ref[...], bgk_ref[...], gng_ref[...], lb, gnh_ref[...],
        c=SAMPLE_PAD_T, ref_row=steps // 2, tri=tri, causal=causal, valid=valid, scan=False)
    o_ref[...] = o
    for s in range(batch_block):
        for g in range(GLA_GROUPS):
            sg_out[0, s, g * per:(g + 1) * per] = _gla_state_from_t(st_g[s][g])
        for h in range(HGRN_HEADS):
            sh_out[0, s, h] = st_h[s][h].T


def _mixer_sample(z2, state_gla, state_hgrn, wgk, bgk, gng, lbl, gnh, *, steps, batch_block):
    batch = z2.shape[0] // SAMPLE_PAD_T
    rows = batch_block * SAMPLE_PAD_T
    full = lambda shape: pl.BlockSpec(shape, lambda i: (0,) * len(shape))
    sg_spec = pl.BlockSpec((1, batch_block, GLA_HEADS, GLA_DK, GLA_DV), lambda i: (0, i, 0, 0, 0))
    sh_spec = pl.BlockSpec((1, batch_block, HGRN_HEADS, HGRN_DK, HGRN_DV), lambda i: (0, i, 0, 0, 0))
    return pl.pallas_call(
        functools.partial(_mixer_sample_kernel, batch_block=batch_block, steps=steps),
        grid=(batch // batch_block,),
        in_specs=[
            pl.BlockSpec((rows, Z_W), lambda i: (i, 0)),
            sg_spec, sh_spec,
            full((R1_PAD, GLA_KDIM)), full((1, GLA_KDIM)), full((1, GLA_DV)),
            full((2, HGRN_FDIM)), full((1, HGRN_DV)),
        ],
        out_specs=[pl.BlockSpec((rows, D_MODEL), lambda i: (i, 0)), sg_spec, sh_spec],
        out_shape=[
            jax.ShapeDtypeStruct((batch * SAMPLE_PAD_T, D_MODEL), f32),
            jax.ShapeDtypeStruct(state_gla.shape, f32),
            jax.ShapeDtypeStruct(state_hgrn.shape, f32),
        ],
        compiler_params=pltpu.CompilerParams(
            dimension_semantics=("arbitrary",), vmem_limit_bytes=VMEM_LIMIT),
        name="mixer_sample",
    )(z2, state_gla, state_hgrn, wgk, bgk, gng, lbl, gnh)


def _store_slabs(ref, x):
    for s in range(SLABS):
        ref[pl.ds(s, x.shape[0], stride=SLABS), :] = x[:, s * LANES:(s + 1) * LANES]


def _load_slab_cols(ref, start, rows, s):
    return ref[pl.ds(start + s, rows, stride=SLABS), :]


def _outproj_kernel(xp_ref, xs_ref, op_ref, os_ref, wo_ref, g_ref, wr_ref, br_ref,
                    h_ref, xn_ref, ti_ref, tg_ref, rank_ref, cnt_ref, run_cnt, *, n_prompt_tiles):
    i = pl.program_id(0)

    @pl.when(i == 0)
    def _():
        run_cnt[...] = jnp.zeros_like(run_cnt)

    x = _select_rows(i < n_prompt_tiles, xp_ref, xs_ref)
    o = _select_rows(i < n_prompt_tiles, op_ref, os_ref)
    h = x + _dot(o, wo_ref[...])
    h_ref[...] = h
    xn = _rms(h, g_ref[...])
    _store_slabs(xn_ref, xn)
    logits = _dot(xn.astype(bf16), wr_ref[...]) + br_ref[...]
    lane = lax.broadcasted_iota(i32, logits.shape, 1)
    neg = jnp.float32(-jnp.inf)
    l = jnp.where(lane < N_EXPERTS, logits, neg)
    vals, idxs = [], []
    for _ in range(TOP_K):
        m = jnp.max(l, axis=-1, keepdims=True)
        idx = jnp.min(jnp.where(l == m, lane, LANES), axis=-1, keepdims=True)
        vals.append(m)
        idxs.append(idx)
        l = jnp.where(lane == idx, neg, l)
    es = [jnp.exp(v - vals[0]) for v in vals]
    denom = es[0] + es[1] + es[2] + es[3]

    onehot = jnp.zeros(logits.shape, f32)
    for k in range(TOP_K):
        onehot = onehot + (lane == idxs[k]).astype(f32)
    rows = logits.shape[0]
    lower = (lax.broadcasted_iota(i32, (rows, rows), 0)
             > lax.broadcasted_iota(i32, (rows, rows), 1)).astype(bf16)
    ahead = _dot(lower, onehot.astype(bf16)) + run_cnt[...]
    run_cnt[...] = run_cnt[...] + jnp.sum(onehot, axis=0, keepdims=True)
    cnt_ref[...] = run_cnt[...].astype(i32)

    ti = jnp.zeros(logits.shape, i32)
    tg = jnp.zeros(logits.shape, f32)
    rank = jnp.zeros(logits.shape, i32)
    for k in range(TOP_K):
        rk = jnp.sum(jnp.where(lane == idxs[k], ahead, 0.0), axis=-1, keepdims=True).astype(i32)
        ti = jnp.where(lane == k, idxs[k], ti)
        tg = jnp.where(lane == k, es[k] / denom, tg)
        rank = jnp.where(lane == k, rk, rank)
    ti_ref[...] = ti
    tg_ref[...] = tg
    rank_ref[...] = rank


def _out_proj(xp, xs, o_p, o_s, wo, g, wr, br):
    tp, ts = xp.shape[0], xs.shape[0]
    npt = tp // TOKEN_TILE
    nt = npt + ts // TOKEN_TILE
    t = tp + ts
    row = lambda w: pl.BlockSpec((TOKEN_TILE, w), lambda i: (i, 0))
    full = lambda shape: pl.BlockSpec(shape, lambda i: (0,) * len(shape))
    prompt_rows = pl.BlockSpec((TOKEN_TILE, D_MODEL), lambda i: (jnp.minimum(i, npt - 1), 0))
    sample_rows = pl.BlockSpec((TOKEN_TILE, D_MODEL), lambda i: (jnp.maximum(i - npt, 0), 0))
    return pl.pallas_call(
        functools.partial(_outproj_kernel, n_prompt_tiles=npt),
        grid=(nt,),
        in_specs=[
            prompt_rows, sample_rows, prompt_rows, sample_rows,
            full((D_MODEL, D_MODEL)), full((1, D_MODEL)),
            full((D_MODEL, LANES)), full((1, LANES)),
        ],
        out_specs=[row(D_MODEL), pl.BlockSpec((TOKEN_TILE * SLABS, LANES), lambda i: (i, 0)),
                   row(LANES), row(LANES), row(LANES), full((1, LANES))],
        out_shape=[
            jax.ShapeDtypeStruct((t, D_MODEL), f32),
            jax.ShapeDtypeStruct((t * SLABS, LANES), f32),
            jax.ShapeDtypeStruct((t, LANES), i32),
            jax.ShapeDtypeStruct((t, LANES), f32),
            jax.ShapeDtypeStruct((t, LANES), i32),
            jax.ShapeDtypeStruct((1, LANES), i32),
        ],
        scratch_shapes=[pltpu.VMEM((1, LANES), f32)],
        compiler_params=pltpu.CompilerParams(
            dimension_semantics=("arbitrary",), vmem_limit_bytes=VMEM_LIMIT),
        name="out_proj_router",
    )(xp, xs, o_p, o_s, wo, g, wr, br)


def _dispatch_kernel(bv_ref, nz_ref, pos_ref, xn_hbm, xs_hbm, zeros, stage, in_sems, out_sems, zsem,
                     *, n_blocks):
    @pl.when(pl.program_id(0) == 0)
    def _():
        zeros[...] = jnp.zeros_like(zeros)

        def fill(b, carry):
            @pl.when(bv_ref[b] < MOE_TILE)
            def _():
                r0 = pl.multiple_of(b * (MOE_TILE * SLABS), MOE_TILE * SLABS)
                pltpu.make_async_copy(zeros, xs_hbm.at[pl.ds(r0, MOE_TILE * SLABS)], zsem).start()
            return carry

        def drain(b, carry):
            pltpu.make_async_copy(zeros, xs_hbm.at[pl.ds(0, MOE_TILE * SLABS)], zsem).wait()
            return carry

        lax.fori_loop(0, n_blocks, fill, 0)
        lax.fori_loop(0, nz_ref[0], drain, 0)

    i = pl.program_id(0)
    nt = pl.num_programs(0)
    tile_rows = DISPATCH_TILE * SLABS

    def fetch(t, slot):
        return pltpu.make_async_copy(
            xn_hbm.at[pl.ds(pl.multiple_of(t * tile_rows, tile_rows), tile_rows)],
            stage.at[pl.ds(pl.multiple_of(slot * tile_rows, tile_rows), tile_rows)], in_sems.at[slot])

    def wait_rows(slot):
        for _ in range(TOP_K):
            pltpu.make_async_copy(stage.at[pl.ds(0, tile_rows)], xs_hbm.at[pl.ds(0, tile_rows)],
                                  out_sems.at[slot]).wait()

    @pl.when(i == 0)
    def _():
        fetch(0, 0).start()

    @pl.when(i + 1 < nt)
    def _():
        fetch(i + 1, (i + 1) % 3).start()

    fetch(i, i % 3).wait()
    base = (i % 3) * tile_rows

    def start(r, carry):
        src = stage.at[pl.ds(pl.multiple_of(base + r * SLABS, SLABS), SLABS)]
        for k in range(TOP_K):
            dst0 = pl.multiple_of(pos_ref[0, 0, r * TOP_K + k], SLABS)
            pltpu.make_async_copy(src, xs_hbm.at[pl.ds(dst0, SLABS)], out_sems.at[i % 2]).start(priority=k % 2)
        return carry

    lax.fori_loop(0, DISPATCH_TILE, start, 0, unroll=4)

    @pl.when(i > 0)
    def _():
        wait_rows((i + 1) % 2)

    @pl.when(i == nt - 1)
    def _():
        wait_rows(i % 2)


def _dispatch(block_valid, pos, xn2):
    t = xn2.shape[0] // SLABS
    nt = t // DISPATCH_TILE
    n_blocks = block_valid.shape[0]
    n_zero = jnp.sum((block_valid < MOE_TILE).astype(i32)).reshape(1)
    grid_spec = pltpu.PrefetchScalarGridSpec(
        num_scalar_prefetch=2,
        grid=(nt,),
        in_specs=[
            pl.BlockSpec((1, 1, DISPATCH_TILE * TOP_K), lambda i, bv, nz: (i, 0, 0),
                         memory_space=pltpu.SMEM),
            pl.BlockSpec(memory_space=pl.ANY),
        ],
        out_specs=pl.BlockSpec(memory_space=pl.ANY),
        scratch_shapes=[
            pltpu.VMEM((MOE_TILE * SLABS, LANES), f32),
            pltpu.VMEM((3 * DISPATCH_TILE * SLABS, LANES), f32),
            pltpu.SemaphoreType.DMA((3,)),
            pltpu.SemaphoreType.DMA((2,)),
            pltpu.SemaphoreType.DMA(()),
        ],
    )
    return pl.pallas_call(
        functools.partial(_dispatch_kernel, n_blocks=n_blocks),
        grid_spec=grid_spec,
        out_shape=jax.ShapeDtypeStruct((n_blocks * MOE_TILE * SLABS, LANES), f32),
        compiler_params=pltpu.CompilerParams(
            dimension_semantics=("arbitrary",), vmem_limit_bytes=VMEM_LIMIT),
        name="moe_dispatch",
    )(block_valid, n_zero, pos.reshape(nt, 1, DISPATCH_TILE * TOP_K), xn2)


def _expert_kernel(be_ref, seg_ref, nxt_ref, nb_ref, x_ref, wgu_hbm, bgu_ref, wdn_hbm, bdn_ref, y_ref,
                   wgu_f32, wdn_f32, wgu_bf, wdn_bf, sems, *, ff_chunk, cast_rows):
    i = pl.program_id(0)

    @pl.when(i >= nb_ref[0])
    def _():
        y_ref[...] = jnp.zeros_like(y_ref)

    def fetch(e, slot):
        return (pltpu.make_async_copy(wgu_hbm.at[e], wgu_f32.at[slot], sems.at[slot, 0]),
                pltpu.make_async_copy(wdn_hbm.at[e], wdn_f32.at[slot], sems.at[slot, 1]))

    @pl.when((i < nb_ref[0]) & ((i == 0) | (be_ref[i] != be_ref[jnp.maximum(i - 1, 0)])))
    def _():
        slot = seg_ref[i] % 2

        @pl.when(i == 0)
        def _():
            for c in fetch(be_ref[0], 0):
                c.start()

        for c in fetch(be_ref[i], slot):
            c.wait()

        @pl.when(nxt_ref[i] >= 0)
        def _():
            for c in fetch(nxt_ref[i], 1 - slot):
                c.start()

        def cast(j, carry):
            r0 = pl.multiple_of(j * cast_rows, cast_rows)
            wgu_bf[pl.ds(r0, cast_rows), :] = wgu_f32[slot, pl.ds(r0, cast_rows), :].astype(bf16)
            wdn_bf[pl.ds(r0, cast_rows), :] = wdn_f32[slot, pl.ds(r0, cast_rows), :].astype(bf16)
            return carry
        lax.fori_loop(0, D_MODEL // cast_rows, cast, 0)

    @pl.when(i < nb_ref[0])
    def _():
        xb = jnp.concatenate(
            [_load_slab_cols(x_ref, 0, MOE_TILE, s).astype(bf16) for s in range(SLABS)], axis=1)
        acc = jnp.zeros((MOE_TILE, D_MODEL), f32)
        for c0 in range(0, D_FF, ff_chunk):
            gate = _dot(xb, wgu_bf[:, c0:c0 + ff_chunk]) + bgu_ref[0, :, c0:c0 + ff_chunk]
            up = (_dot(xb, wgu_bf[:, D_FF + c0:D_FF + c0 + ff_chunk])
                  + bgu_ref[0, :, D_FF + c0:D_FF + c0 + ff_chunk])
            gate = jnp.minimum(gate, SWIGLU_LIMIT)
            up = jnp.clip(up, -SWIGLU_LIMIT, SWIGLU_LIMIT)
            hid = (up + 1.0) * (gate * _sigmoid(gate * SWIGLU_ALPHA))
            acc = acc + _dot(hid.astype(bf16), wdn_bf[c0:c0 + ff_chunk, :])
        _store_slabs(y_ref, acc + bdn_ref[0])


def _experts(block_e, block_seg, next_e, n_used, xs, wgu, bgu, wdn, bdn):
    n_blocks = block_e.shape[0]
    grid_spec = pltpu.PrefetchScalarGridSpec(
        num_scalar_prefetch=4,
        grid=(n_blocks,),
        in_specs=[
            pl.BlockSpec((MOE_TILE * SLABS, LANES),
                         lambda i, be, sg, nx, nb: (jnp.minimum(i, nb[0] - 1), 0)),
            pl.BlockSpec(memory_space=pl.ANY),
            pl.BlockSpec((1, 1, 2 * D_FF), lambda i, be, sg, nx, nb: (be[i], 0, 0)),
            pl.BlockSpec(memory_space=pl.ANY),
            pl.BlockSpec((1, 1, D_MODEL), lambda i, be, sg, nx, nb: (be[i], 0, 0)),
        ],
        out_specs=pl.BlockSpec((MOE_TILE * SLABS, LANES), lambda i, be, sg, nx, nb: (i, 0)),
        scratch_shapes=[
            pltpu.VMEM((2, D_MODEL, 2 * D_FF), f32),
            pltpu.VMEM((2, D_FF, D_MODEL), f32),
            pltpu.VMEM((D_MODEL, 2 * D_FF), bf16),
            pltpu.VMEM((D_FF, D_MODEL), bf16),
            pltpu.SemaphoreType.DMA((2, 2)),
        ],
    )
    return pl.pallas_call(
        functools.partial(_expert_kernel, ff_chunk=256, cast_rows=128),
        grid_spec=grid_spec,
        out_shape=jax.ShapeDtypeStruct((n_blocks * MOE_TILE * SLABS, LANES), f32),
        compiler_params=pltpu.CompilerParams(
            dimension_semantics=("arbitrary",), vmem_limit_bytes=VMEM_LIMIT),
        name="moe_experts",
    )(block_e, block_seg, next_e, n_used, xs, wgu, bgu, wdn, bdn)


def _combine_kernel(pos_cur, pos_nxt, ys_hbm, h_ref, tg_ref, pp_ref, ps_ref, gple_ref, wpg_ref, wpp_ref,
                    gfin_ref, yp_ref, ys_ref, rows, sems, *, n_prompt_tiles, issue_unroll):
    i = pl.program_id(0)
    nt = pl.num_programs(0)
    n = COMBINE_TILE * TOP_K
    slot_rows = n * SLABS

    def row_copy(pos_ref, slot_base, r, slot):
        src0 = pl.multiple_of(pos_ref[0, 0, r], SLABS)
        dst0 = pl.multiple_of(slot_base + r * SLABS, SLABS)
        return pltpu.make_async_copy(ys_hbm.at[pl.ds(src0, SLABS)], rows.at[pl.ds(dst0, SLABS)],
                                     sems.at[slot])

    def wait_slot(slot):
        pltpu.make_async_copy(ys_hbm.at[pl.ds(0, slot_rows)],
                              rows.at[pl.ds(pl.multiple_of(slot * slot_rows, slot_rows), slot_rows)],
                              sems.at[slot]).wait()

    def gather(pos_ref, slot):
        slot_base = slot * slot_rows

        def start(j, carry):
            for u in range(issue_unroll):
                row_copy(pos_ref, slot_base, issue_unroll * j + u, slot).start(priority=u % 2)
            return carry
        lax.fori_loop(0, n // issue_unroll, start, 0)

    @pl.when(i == 0)
    def _():
        gather(pos_cur, 0)

    @pl.when(i + 1 < nt)
    def _():
        gather(pos_nxt, (i + 1) % 2)

    slot = i % 2
    base = pl.multiple_of(slot * slot_rows, slot_rows)
    wait_slot(slot)

    tg = tg_ref[...]
    cols = []
    for s in range(SLABS):
        moe = jnp.zeros((COMBINE_TILE, LANES), f32)
        for k in range(TOP_K):
            moe = moe + tg[:, k:k + 1] * _load_slab_cols(
                rows, base + k * COMBINE_TILE * SLABS, COMBINE_TILE, s)
        cols.append(h_ref[:, s * LANES:(s + 1) * LANES] + moe)
    h = jnp.concatenate(cols, axis=1)
    gate = _sigmoid(_dot(_rms(h, gple_ref[...]).astype(bf16), wpg_ref[...]))
    p = _select_rows(i < n_prompt_tiles, pp_ref, ps_ref).astype(bf16)
    h = h + gate * _dot(p, wpp_ref[...])
    y = _rms(h, gfin_ref[...])

    @pl.when(i < n_prompt_tiles)
    def _():
        yp_ref[...] = y

    @pl.when(i >= n_prompt_tiles)
    def _():
        ys_ref[...] = y


def _combine(pos, ys, h1, tg, pp, ps, gple, wpg, wpp, gfin):
    t = h1.shape[0]
    nt = t // COMBINE_TILE
    npt = pp.shape[0] // COMBINE_TILE
    n = COMBINE_TILE * TOP_K
    pos_tiles = pos.reshape(nt, COMBINE_TILE, TOP_K).transpose(0, 2, 1).reshape(nt, 1, n)
    row = lambda w: pl.BlockSpec((COMBINE_TILE, w), lambda i: (i, 0))
    full = lambda shape: pl.BlockSpec(shape, lambda i: (0,) * len(shape))
    return pl.pallas_call(
        functools.partial(_combine_kernel, n_prompt_tiles=npt, issue_unroll=16),
        grid=(nt,),
        in_specs=[
            pl.BlockSpec((1, 1, n), lambda i: (i, 0, 0), memory_space=pltpu.SMEM),
            pl.BlockSpec((1, 1, n), lambda i: (jnp.minimum(i + 1, nt - 1), 0, 0), memory_space=pltpu.SMEM),
            pl.BlockSpec(memory_space=pl.ANY),
            row(D_MODEL), row(LANES),
            pl.BlockSpec((COMBINE_TILE, PLE_DIM), lambda i: (jnp.minimum(i, npt - 1), 0)),
            pl.BlockSpec((COMBINE_TILE, PLE_DIM), lambda i: (jnp.maximum(i - npt, 0), 0)),
            full((1, D_MODEL)), full((D_MODEL, D_MODEL)), full((PLE_DIM, D_MODEL)),
            full((1, D_MODEL)),
        ],
        out_specs=[
            pl.BlockSpec((COMBINE_TILE, D_MODEL), lambda i: (jnp.minimum(i, npt - 1), 0)),
            pl.BlockSpec((COMBINE_TILE, D_MODEL), lambda i: (jnp.maximum(i - npt, 0), 0)),
        ],
        out_shape=[
            jax.ShapeDtypeStruct((pp.shape[0], D_MODEL), f32),
            jax.ShapeDtypeStruct((ps.shape[0], D_MODEL), f32),
        ],
        scratch_shapes=[
            pltpu.VMEM((2 * n * SLABS, LANES), f32),
            pltpu.SemaphoreType.DMA((2,)),
        ],
        compiler_params=pltpu.CompilerParams(
            dimension_semantics=("arbitrary",), vmem_limit_bytes=VMEM_LIMIT),
        name="moe_combine_ple",
    )(pos_tiles, pos_tiles, ys, h1, tg, pp, ps, gple, wpg, wpp, gfin)


def _routing_tables(top_i, rank, counts, n_blocks):
    padded = (counts + MOE_TILE - 1) // MOE_TILE * MOE_TILE
    pend = jnp.cumsum(padded)
    pstart = pend - padded
    block_rows = jnp.arange(n_blocks, dtype=i32) * MOE_TILE
    block_e = jnp.minimum(
        jnp.sum((block_rows[:, None] >= pend[None, :]).astype(i32), axis=1), N_EXPERTS - 1)
    block_valid = jnp.clip(pstart[block_e] + counts[block_e] - block_rows, 0, MOE_TILE)
    n_used = pend[-1:] // MOE_TILE
    blk = jnp.arange(n_blocks, dtype=i32)
    first = jnp.concatenate([jnp.ones((1,), bool), block_e[1:] != block_e[:-1]])
    block_seg = jnp.cumsum(first.astype(i32)) - 1
    later = (blk[None, :] > blk[:, None]) & (block_e[None, :] != block_e[:, None]) & (blk[None, :] < n_used)
    next_e = jnp.where(jnp.any(later, axis=1), block_e[jnp.argmax(later, axis=1)], -1)
    seg = jnp.zeros_like(rank)
    for e in range(N_EXPERTS):
        seg = jnp.where(top_i == e, pstart[e], seg)
    pos = (rank + seg) * SLABS
    tables = (block_e, block_seg, next_e, block_valid, n_used)
    return (pos.astype(i32),) + tuple(t.astype(i32) for t in tables)


def kernel(x_prompt, x_sample, p_prompt, p_sample, state_gla, state_hgrn, g_mix, w_in, w_gk2, b_gk2,
           gn_gla, lb_logits, gn_hgrn, w_out, g_ffn, w_router, b_router, w_gu, b_gu, w_dn, b_dn,
           g_ple, w_ple_gate, w_ple_proj, g_final):
    batch, seq, _ = x_prompt.shape
    dec_batch, dec_seq, _ = x_sample.shape
    tp, ts = batch * seq, dec_batch * dec_seq
    xp = x_prompt.reshape(tp, D_MODEL)
    xs = x_sample.reshape(ts, D_MODEL)

    r1_end = Z_R1 + GLA_LOWRANK
    w_in_p = jnp.concatenate(
        [w_in[0, :, :r1_end], jnp.zeros((D_MODEL, R1_PAD - GLA_LOWRANK), f32), w_in[0, :, r1_end:]],
        axis=1).astype(bf16)
    wgk = jnp.concatenate(
        [w_gk2[0], jnp.zeros((R1_PAD - GLA_LOWRANK, GLA_KDIM), f32)], axis=0).astype(bf16)
    wr = jnp.concatenate(
        [w_router[0], jnp.zeros((D_MODEL, LANES - N_EXPERTS), f32)], axis=1).astype(bf16)
    br = jnp.concatenate([b_router[0], jnp.zeros((LANES - N_EXPERTS,), f32)]).reshape(1, LANES)

    z = _in_proj(xp, xs, g_mix[0].reshape(1, D_MODEL), w_in_p)

    mixer_params = (wgk, b_gk2[0].reshape(1, GLA_KDIM), gn_gla[0].reshape(1, GLA_DV),
                    lb_logits[0:2], gn_hgrn[0].reshape(1, HGRN_DV))
    o_p, new_gla_p, new_hgrn_p = _mixer_prompt(
        z, *mixer_params, batch=batch, seq=seq, rows_per_step=512, block_rows=256)
    z_s = jnp.pad(z[tp:].reshape(dec_batch, dec_seq, Z_W), ((0, 0), (0, SAMPLE_PAD_T - dec_seq), (0, 0)))
    o_s, new_gla_s, new_hgrn_s = _mixer_sample(
        z_s.reshape(dec_batch * SAMPLE_PAD_T, Z_W), state_gla, state_hgrn, *mixer_params,
        steps=dec_seq, batch_block=8)
    o_s = o_s.reshape(dec_batch, SAMPLE_PAD_T, D_MODEL)[:, :dec_seq].reshape(ts, D_MODEL).astype(bf16)

    h1, xn2, top_i, top_g, rank, counts = _out_proj(
        xp, xs, o_p, o_s, w_out[0].astype(bf16), g_ffn[0].reshape(1, D_MODEL), wr, br)

    t = tp + ts
    n_blocks = -(-(t * TOP_K) // MOE_TILE) + N_EXPERTS
    pos, block_e, block_seg, next_e, block_valid, n_used = _routing_tables(
        top_i[:, :TOP_K], rank[:, :TOP_K], counts[0, :N_EXPERTS], n_blocks)
    x_sorted = _dispatch(block_valid, pos, xn2)
    y_sorted = _experts(block_e, block_seg, next_e, n_used, x_sorted,
                        w_gu[0], b_gu[0].reshape(N_EXPERTS, 1, 2 * D_FF),
                        w_dn[0], b_dn[0].reshape(N_EXPERTS, 1, D_MODEL))

    y_p, y_s = _combine(pos, y_sorted, h1, top_g, p_prompt[0].reshape(tp, PLE_DIM),
                        p_sample[0].reshape(ts, PLE_DIM), g_ple[0].reshape(1, D_MODEL),
                        w_ple_gate[0].astype(bf16), w_ple_proj[0].astype(bf16), g_final.reshape(1, D_MODEL))

    return (y_p.reshape(batch, seq, D_MODEL), y_s.reshape(dec_batch, dec_seq, D_MODEL),
            new_gla_p, new_hgrn_p, new_gla_s, new_hgrn_s)
```

```python
import functools

import jax
import jax.numpy as jnp
from jax import lax
from jax.experimental import pallas as pl
from jax.experimental.pallas import tpu as pltpu

f32 = jnp.float32
bf16 = jnp.bfloat16
i32 = jnp.int32

D_MODEL = 1024
GLA_HEADS = 4
GLA_DK = 64
GLA_DV = 128
GLA_KDIM = GLA_HEADS * GLA_DK
GLA_WIDTH = GLA_HEADS * GLA_DV
GLA_LOWRANK = 16
GLA_NORMALIZER = 16.0
HGRN_HEADS = 4
HGRN_DK = 128
HGRN_DV = 128
HGRN_FDIM = HGRN_HEADS * HGRN_DK
HGRN_WIDTH = HGRN_HEADS * HGRN_DV
CHUNK = 64
N_EXPERTS = 32
TOP_K = 4
D_FF = 1024
SWIGLU_LIMIT = 7.0
SWIGLU_ALPHA = 1.702
PLE_DIM = 256
EPS = 1e-6

LANES = 128
SAMPLE_PAD_T = 8
SLABS = D_MODEL // LANES
ROUTE_ROWS = 8

R1_PAD = LANES
Z_Q1 = 0
Z_K1 = Z_Q1 + GLA_KDIM
Z_V1 = Z_K1 + GLA_KDIM
Z_G1 = Z_V1 + GLA_WIDTH
Z_R1 = Z_G1 + GLA_WIDTH
Z_Q2 = Z_R1 + R1_PAD
Z_F2 = Z_Q2 + HGRN_FDIM
Z_I2 = Z_F2 + HGRN_FDIM
Z_G2 = Z_I2 + HGRN_WIDTH
Z_W = Z_G2 + HGRN_WIDTH

TOKEN_TILE = 512
MOE_TILE = 512
DISPATCH_TILE = 256
COMBINE_TILE = DISPATCH_TILE
VMEM_LIMIT = 56 * 1024 * 1024


def _rms(x, g):
    return x * lax.rsqrt(jnp.mean(x * x, axis=-1, keepdims=True) + EPS) * g


def _sigmoid(x):
    return 0.5 * jnp.tanh(0.5 * x) + 0.5


def _dot(a, b):
    return jnp.dot(a, b, preferred_element_type=f32)


def _dot_nt(a, b):
    return lax.dot_general(a, b, (((1,), (1,)), ((), ())), preferred_element_type=f32)


def _dot_tn(a, b):
    return lax.dot_general(a, b, (((0,), (0,)), ((), ())), preferred_element_type=f32)


def _cumsum_rows(tri, x):
    hi = x.astype(bf16)
    r1 = x - hi.astype(f32)
    mid = r1.astype(bf16)
    lo = (r1 - mid.astype(f32)).astype(bf16)
    return _dot(tri, hi) + _dot(tri, mid) + _dot(tri, lo)


def _select_rows(is_prompt, a_ref, b_ref):
    return jnp.where(is_prompt, a_ref[...], b_ref[...])


def _inproj_kernel(xp_ref, xs_ref, g_ref, w_hbm, z_ref, w_f32, w_bf, sem,
                   *, n_prompt_tiles, col_chunk, cast_rows):
    @pl.when(pl.program_id(0) == 0)
    def _():
        copy = pltpu.make_async_copy(w_hbm, w_f32, sem)
        copy.start()
        copy.wait()
        r1_end = Z_R1 + GLA_LOWRANK

        def cast(j, carry):
            rows = pl.ds(pl.multiple_of(j * cast_rows, cast_rows), cast_rows)
            w_bf[rows, 0:Z_R1] = w_f32[rows, 0:Z_R1].astype(bf16)
            w_bf[rows, Z_R1:Z_Q2] = jnp.concatenate(
                [w_f32[rows, Z_R1:r1_end], jnp.zeros((cast_rows, R1_PAD - GLA_LOWRANK), f32)],
                axis=1).astype(bf16)
            w_bf[rows, Z_Q2:Z_W] = w_f32[rows, r1_end:].astype(bf16)
            return carry
        lax.fori_loop(0, D_MODEL // cast_rows, cast, 0)

    is_prompt = pl.program_id(0) < n_prompt_tiles
    x = _select_rows(is_prompt, xp_ref, xs_ref)
    a = _rms(x, g_ref[...]).astype(bf16)
    for c0 in range(0, Z_W, col_chunk):
        c1 = min(c0 + col_chunk, Z_W)
        z_ref[:, c0:c1] = _dot(a, w_bf[:, c0:c1]).astype(bf16)


def _in_proj(xp, xs, g, w):
    tp, ts = xp.shape[0], xs.shape[0]
    npt = tp // TOKEN_TILE
    nt = npt + ts // TOKEN_TILE
    return pl.pallas_call(
        functools.partial(_inproj_kernel, n_prompt_tiles=npt, col_chunk=512, cast_rows=64),
        grid=(nt,),
        in_specs=[
            pl.BlockSpec((TOKEN_TILE, D_MODEL), lambda i: (jnp.minimum(i, npt - 1), 0)),
            pl.BlockSpec((TOKEN_TILE, D_MODEL), lambda i: (jnp.maximum(i - npt, 0), 0)),
            pl.BlockSpec((1, D_MODEL), lambda i: (0, 0)),
            pl.BlockSpec(memory_space=pl.ANY),
        ],
        out_specs=pl.BlockSpec((TOKEN_TILE, Z_W), lambda i: (i, 0)),
        out_shape=jax.ShapeDtypeStruct((tp + ts, Z_W), bf16),
        scratch_shapes=[
            pltpu.VMEM(w.shape, f32),
            pltpu.VMEM((D_MODEL, Z_W), bf16),
            pltpu.SemaphoreType.DMA(()),
        ],
        compiler_params=pltpu.CompilerParams(
            dimension_semantics=("arbitrary",), vmem_limit_bytes=VMEM_LIMIT),
        name="in_proj",
    )(xp, xs, g, w)


def _block_consts(r, c):
    shift = c.bit_length() - 1
    row = lax.broadcasted_iota(i32, (r, r), 0)
    col = lax.broadcasted_iota(i32, (r, r), 1)
    causal = (lax.shift_right_logical(row, shift) == lax.shift_right_logical(col, shift)) & (row >= col)
    return causal.astype(bf16), causal


def _subchunk_row(x, c, idx):
    parts = [jnp.broadcast_to(x[j * c + idx:j * c + idx + 1, :], (c, x.shape[1]))
             for j in range(x.shape[0] // c)]
    return jnp.concatenate(parts, axis=0)


def _gated_block(q, k, v, log_a, states, *, c, scale, ref_row, tri, causal, heads_per_group, scan):
    r = q.shape[0]
    nsub = r // c
    groups = q.shape[1] // LANES
    b = _cumsum_rows(tri, log_a)
    b_ref = _subchunk_row(b, c, ref_row)
    b_last = _subchunk_row(b, c, c - 1)
    qs = q * scale
    q_r = qs * jnp.exp(b - b_ref)
    k_r = (k * jnp.exp(b_ref - b)).astype(bf16)
    k_end = k * jnp.exp(b_last - b)
    q_b = qs * jnp.exp(b)
    decay = jnp.exp(b_last)
    vb = v.astype(bf16)
    lane = lax.broadcasted_iota(i32, (r, LANES), 1)
    width = LANES // heads_per_group
    outs, new_states = [], []
    for g in range(groups):
        gl = slice(g * LANES, (g + 1) * LANES)
        masks = [None] if heads_per_group == 1 else [
            (lane >= j * width) & (lane < (j + 1) * width) for j in range(heads_per_group)]
        pick = lambda x, m: x if m is None else jnp.where(m, x, 0.0)
        q_r_h = [pick(q_r[:, gl], m).astype(bf16) for m in masks]
        q_b_h = [pick(q_b[:, gl], m).astype(bf16) for m in masks]
        k_end_h = [pick(k_end[:, gl], m).astype(bf16) for m in masks]
        v_h = [vb[:, (g * heads_per_group + j) * LANES:(g * heads_per_group + j + 1) * LANES]
               for j in range(heads_per_group)]
        o_intra = []
        for j in range(heads_per_group):
            att = _dot_nt(q_r_h[j], k_r[:, gl])
            att = jnp.where(causal, att, 0.0).astype(bf16)
            o_intra.append(_dot(att, v_h[j]))
        upd = []
        for s in range(nsub):
            rs = slice(s * c, (s + 1) * c)
            u = _dot_tn(v_h[0][rs], k_end_h[0][rs])
            for j in range(1, heads_per_group):
                u = u + _dot_tn(v_h[j][rs], k_end_h[j][rs])
            upd.append(u)
        o_inter = [[] for _ in range(heads_per_group)]
        st = states[g] if scan else None
        group_states = []
        for s in range(nsub):
            rs = slice(s * c, (s + 1) * c)
            if not scan:
                st = states[s][g]
            st_b = st.astype(bf16)
            for j in range(heads_per_group):
                o_inter[j].append(_dot_nt(q_b_h[j][rs], st_b))
            st = st * decay[s * c:s * c + 1, gl] + upd[s]
            if not scan:
                group_states.append(st)
        for j in range(heads_per_group):
            outs.append(o_intra[j] + jnp.concatenate(o_inter[j], axis=0))
        new_states.append(st if scan else group_states)
    if not scan:
        new_states = [[new_states[g][s] for g in range(groups)] for s in range(nsub)]
    return outs, new_states


def _head_norm(o, g):
    return o * lax.rsqrt(jnp.mean(o * o, axis=-1, keepdims=True) + EPS) * g


def _mixer_block(zb, st_gla, st_hgrn, wgk, bgk, gng, lb, gnh, *, c, ref_row, tri, causal, valid, scan):
    z = zb.astype(f32)
    gk = _dot(zb[:, Z_R1:Z_R1 + R1_PAD], wgk) + bgk
    log_a = (jnp.minimum(gk, 0.0) - jnp.log(1.0 + jnp.exp(-jnp.abs(gk)))) / GLA_NORMALIZER
    k1 = z[:, Z_K1:Z_K1 + GLA_KDIM]
    if valid is not None:
        log_a = jnp.where(valid, log_a, 0.0)
        k1 = jnp.where(valid, k1, 0.0)
    o1, st_gla = _gated_block(
        z[:, Z_Q1:Z_Q1 + GLA_KDIM], k1, z[:, Z_V1:Z_V1 + GLA_WIDTH], log_a, st_gla,
        c=c, scale=GLA_DK ** -0.5, ref_row=ref_row, tri=tri, causal=causal,
        heads_per_group=LANES // GLA_DK, scan=scan)
    forget = lb + (1.0 - lb) * _sigmoid(z[:, Z_F2:Z_F2 + HGRN_FDIM])
    k2 = 1.0 - forget
    log_f = jnp.log(forget)
    if valid is not None:
        log_f = jnp.where(valid, log_f, 0.0)
        k2 = jnp.where(valid, k2, 0.0)
    q2 = z[:, Z_Q2:Z_Q2 + HGRN_FDIM]
    i2 = z[:, Z_I2:Z_I2 + HGRN_WIDTH]
    o2, st_hgrn = _gated_block(
        q2 * _sigmoid(q2), k2, i2 * _sigmoid(i2), log_f, st_hgrn,
        c=c, scale=1.0, ref_row=ref_row, tri=tri, causal=causal, heads_per_group=1, scan=scan)
    cols = []
    for h in range(GLA_HEADS):
        g1 = z[:, Z_G1 + h * GLA_DV:Z_G1 + (h + 1) * GLA_DV]
        cols.append(_head_norm(o1[h], gng) * (g1 * _sigmoid(g1)))
    for h in range(HGRN_HEADS):
        g2 = z[:, Z_G2 + h * HGRN_DV:Z_G2 + (h + 1) * HGRN_DV]
        cols.append(_head_norm(o2[h], gnh) * _sigmoid(g2))
    return jnp.concatenate(cols, axis=1), st_gla, st_hgrn


def _lower_bound(lbl_ref):
    l = lbl_ref[...]
    m = jnp.max(l, axis=0, keepdims=True)
    e = jnp.exp(l - m)
    return e[0:1, :] / jnp.sum(e, axis=0, keepdims=True)


GLA_GROUPS = GLA_KDIM // LANES


def _gla_state_t(s_ref_heads):
    return s_ref_heads.reshape(LANES, GLA_DV).T


def _gla_state_from_t(st):
    return st.T.reshape(LANES // GLA_DK, GLA_DK, GLA_DV)


def _mixer_prompt_kernel(z_ref, wgk_ref, bgk_ref, gng_ref, lbl_ref, gnh_ref,
                         o_ref, sg_out, sh_out, sg_t, sh_t, *, block_rows):
    j = pl.program_id(1)

    @pl.when(j == 0)
    def _():
        sg_t[...] = jnp.zeros_like(sg_t)
        sh_t[...] = jnp.zeros_like(sh_t)

    tri, causal = _block_consts(block_rows, CHUNK)
    lb = _lower_bound(lbl_ref)
    wgk, bgk, gng, gnh = wgk_ref[...], bgk_ref[...], gng_ref[...], gnh_ref[...]

    def body(blk, carry):
        r0 = pl.multiple_of(blk * block_rows, block_rows)
        o, st_g, st_h = _mixer_block(
            z_ref[pl.ds(r0, block_rows), :], [sg_t[g] for g in range(GLA_GROUPS)],
            [sh_t[h] for h in range(HGRN_HEADS)], wgk, bgk, gng, lb, gnh,
            c=CHUNK, ref_row=CHUNK // 2, tri=tri, causal=causal, valid=None, scan=True)
        o_ref[pl.ds(r0, block_rows), :] = o.astype(bf16)
        for g in range(GLA_GROUPS):
            sg_t[g] = st_g[g]
        for h in range(HGRN_HEADS):
            sh_t[h] = st_h[h]
        return carry

    lax.fori_loop(0, z_ref.shape[0] // block_rows, body, 0)

    @pl.when(j == pl.num_programs(1) - 1)
    def _():
        per = LANES // GLA_DK
        for g in range(GLA_GROUPS):
            sg_out[0, 0, g * per:(g + 1) * per] = _gla_state_from_t(sg_t[g])
        for h in range(HGRN_HEADS):
            sh_out[0, 0, h] = sh_t[h].T


def _mixer_prompt(z, wgk, bgk, gng, lbl, gnh, *, batch, seq, rows_per_step, block_rows):
    nj = seq // rows_per_step
    full = lambda shape: pl.BlockSpec(shape, lambda b, j: (0,) * len(shape))
    return pl.pallas_call(
        functools.partial(_mixer_prompt_kernel, block_rows=block_rows),
        grid=(batch, nj),
        in_specs=[
            pl.BlockSpec((rows_per_step, Z_W), lambda b, j: (b * nj + j, 0)),
            full((R1_PAD, GLA_KDIM)), full((1, GLA_KDIM)), full((1, GLA_DV)),
            full((2, HGRN_FDIM)), full((1, HGRN_DV)),
        ],
        out_specs=[
            pl.BlockSpec((rows_per_step, D_MODEL), lambda b, j: (b * nj + j, 0)),
            pl.BlockSpec((1, 1, GLA_HEADS, GLA_DK, GLA_DV), lambda b, j: (0, b, 0, 0, 0)),
            pl.BlockSpec((1, 1, HGRN_HEADS, HGRN_DK, HGRN_DV), lambda b, j: (0, b, 0, 0, 0)),
        ],
        out_shape=[
            jax.ShapeDtypeStruct((batch * seq, D_MODEL), bf16),
            jax.ShapeDtypeStruct((1, batch, GLA_HEADS, GLA_DK, GLA_DV), f32),
            jax.ShapeDtypeStruct((1, batch, HGRN_HEADS, HGRN_DK, HGRN_DV), f32),
        ],
        scratch_shapes=[
            pltpu.VMEM((GLA_GROUPS, GLA_DV, LANES), f32),
            pltpu.VMEM((HGRN_HEADS, HGRN_DV, HGRN_DK), f32),
        ],
        compiler_params=pltpu.CompilerParams(
            dimension_semantics=("arbitrary", "arbitrary"), vmem_limit_bytes=VMEM_LIMIT),
        name="mixer_prompt",
    )(z, wgk, bgk, gng, lbl, gnh)


def _mixer_sample_kernel(z_ref, sg_in, sh_in, wgk_ref, bgk_ref, gng_ref, lbl_ref, gnh_ref,
                         o_ref, sg_out, sh_out, *, batch_block, steps):
    rows = batch_block * SAMPLE_PAD_T
    tri, causal = _block_consts(rows, SAMPLE_PAD_T)
    step = lax.broadcasted_iota(i32, (rows, 1), 0) & (SAMPLE_PAD_T - 1)
    valid = step < steps
    lb = _lower_bound(lbl_ref)
    per = LANES // GLA_DK
    st_g = [[_gla_state_t(sg_in[0, s, g * per:(g + 1) * per]) for g in range(GLA_GROUPS)]
            for s in range(batch_block)]
    st_h = [[sh_in[0, s, h].T for h in range(HGRN_HEADS)] for s in range(batch_block)]
    o, st_g, st_h = _mixer_block(
        z_ref[...], st_g, st_h, wgk_ref[...], bgk_ref[...], gng_ref[...], lb, gnh_ref[...],
        c=SAMPLE_PAD_T, ref_row=steps // 2, tri=tri, causal=causal, valid=valid, scan=False)
    o_ref[...] = o
    for s in range(batch_block):
        for g in range(GLA_GROUPS):
            sg_out[0, s, g * per:(g + 1) * per] = _gla_state_from_t(st_g[s][g])
        for h in range(HGRN_HEADS):
            sh_out[0, s, h] = st_h[s][h].T


def _mixer_sample(z2, state_gla, state_hgrn, wgk, bgk, gng, lbl, gnh, *, steps, batch_block):
    batch = z2.shape[0] // SAMPLE_PAD_T
    rows = batch_block * SAMPLE_PAD_T
    full = lambda shape: pl.BlockSpec(shape, lambda i: (0,) * len(shape))
    sg_spec = pl.BlockSpec((1, batch_block, GLA_HEADS, GLA_DK, GLA_DV), lambda i: (0, i, 0, 0, 0))
    sh_spec = pl.BlockSpec((1, batch_block, HGRN_HEADS, HGRN_DK, HGRN_DV), lambda i: (0, i, 0, 0, 0))
    return pl.pallas_call(
        functools.partial(_mixer_sample_kernel, batch_block=batch_block, steps=steps),
        grid=(batch // batch_block,),
        in_specs=[
            pl.BlockSpec((rows, Z_W), lambda i: (i, 0)),
            sg_spec, sh_spec,
            full((R1_PAD, GLA_KDIM)), full((1, GLA_KDIM)), full((1, GLA_DV)),
            full((2, HGRN_FDIM)), full((1, HGRN_DV)),
        ],
        out_specs=[pl.BlockSpec((rows, D_MODEL), lambda i: (i, 0)), sg_spec, sh_spec],
        out_shape=[
            jax.ShapeDtypeStruct((batch * SAMPLE_PAD_T, D_MODEL), f32),
            jax.ShapeDtypeStruct(state_gla.shape, f32),
            jax.ShapeDtypeStruct(state_hgrn.shape, f32),
        ],
        compiler_params=pltpu.CompilerParams(
            dimension_semantics=("arbitrary",), vmem_limit_bytes=VMEM_LIMIT),
        name="mixer_sample",
    )(z2, state_gla, state_hgrn, wgk, bgk, gng, lbl, gnh)


def _store_slabs(ref, x):
    for s in range(SLABS):
        ref[pl.ds(s, x.shape[0], stride=SLABS), :] = x[:, s * LANES:(s + 1) * LANES]


def _load_slab_cols(ref, start, rows, s):
    return ref[pl.ds(start + s, rows, stride=SLABS), :]


def _outproj_kernel(xp_ref, xs_ref, op_ref, os_ref, wo_ref, g_ref, wr_ref, br_ref,
                    h_ref, xn_ref, ti_ref, tg_ref, rank_ref, cnt_ref, run_cnt, *, n_prompt_tiles):
    i = pl.program_id(0)

    @pl.when(i == 0)
    def _():
        run_cnt[...] = jnp.zeros_like(run_cnt)

    x = _select_rows(i < n_prompt_tiles, xp_ref, xs_ref)
    o = _select_rows(i < n_prompt_tiles, op_ref, os_ref)
    h = x + _dot(o, wo_ref[...])
    h_ref[...] = h
    xn = _rms(h, g_ref[...])
    _store_slabs(xn_ref, xn)
    l = _dot_nt(wr_ref[...], xn.astype(bf16)) + br_ref[...]
    expert = lax.broadcasted_iota(i32, l.shape, 0)
    neg = jnp.float32(-jnp.inf)
    vals, idxs = [], []
    for _ in range(TOP_K):
        m = jnp.max(l, axis=0, keepdims=True)
        idx = jnp.min(jnp.where(l == m, expert, N_EXPERTS), axis=0, keepdims=True)
        vals.append(m)
        idxs.append(idx)
        l = jnp.where(expert == idx, neg, l)
    es = [jnp.exp(v - vals[0]) for v in vals]
    denom = es[0] + es[1] + es[2] + es[3]

    onehot = jnp.zeros(l.shape, f32)
    for k in range(TOP_K):
        onehot = onehot + (expert == idxs[k]).astype(f32)
    n = l.shape[1]
    earlier = (lax.broadcasted_iota(i32, (n, n), 0) < lax.broadcasted_iota(i32, (n, n), 1)).astype(bf16)
    ahead = _dot(onehot.astype(bf16), earlier) + run_cnt[:, 0:1]
    run_cnt[...] = run_cnt[...] + jnp.sum(onehot, axis=1, keepdims=True)
    cnt_ref[...] = run_cnt[...].astype(i32)

    ranks = [jnp.sum(jnp.where(expert == idxs[k], ahead, 0.0), axis=0, keepdims=True).astype(i32)
             for k in range(TOP_K)]
    pad_i = jnp.zeros((ROUTE_ROWS - TOP_K, n), i32)
    ti_ref[...] = jnp.concatenate(idxs + [pad_i], axis=0)
    rank_ref[...] = jnp.concatenate(ranks + [pad_i], axis=0)
    tg_ref[...] = jnp.concatenate([e / denom for e in es] + [pad_i.astype(f32)], axis=0)


def _out_proj(xp, xs, o_p, o_s, wo, g, wr, br):
    tp, ts = xp.shape[0], xs.shape[0]
    npt = tp // TOKEN_TILE
    nt = npt + ts // TOKEN_TILE
    t = tp + ts
    row = lambda w: pl.BlockSpec((TOKEN_TILE, w), lambda i: (i, 0))
    route = pl.BlockSpec((ROUTE_ROWS, TOKEN_TILE), lambda i: (0, i))
    full = lambda shape: pl.BlockSpec(shape, lambda i: (0,) * len(shape))
    prompt_rows = pl.BlockSpec((TOKEN_TILE, D_MODEL), lambda i: (jnp.minimum(i, npt - 1), 0))
    sample_rows = pl.BlockSpec((TOKEN_TILE, D_MODEL), lambda i: (jnp.maximum(i - npt, 0), 0))
    return pl.pallas_call(
        functools.partial(_outproj_kernel, n_prompt_tiles=npt),
        grid=(nt,),
        in_specs=[
            prompt_rows, sample_rows, prompt_rows, sample_rows,
            full((D_MODEL, D_MODEL)), full((1, D_MODEL)),
            full((N_EXPERTS, D_MODEL)), full((N_EXPERTS, 1)),
        ],
        out_specs=[row(D_MODEL), pl.BlockSpec((TOKEN_TILE * SLABS, LANES), lambda i: (i, 0)),
                   route, route, route, full((N_EXPERTS, LANES))],
        out_shape=[
            jax.ShapeDtypeStruct((t, D_MODEL), f32),
            jax.ShapeDtypeStruct((t * SLABS, LANES), f32),
            jax.ShapeDtypeStruct((ROUTE_ROWS, t), i32),
            jax.ShapeDtypeStruct((ROUTE_ROWS, t), f32),
            jax.ShapeDtypeStruct((ROUTE_ROWS, t), i32),
            jax.ShapeDtypeStruct((N_EXPERTS, LANES), i32),
        ],
        scratch_shapes=[pltpu.VMEM((N_EXPERTS, LANES), f32)],
        compiler_params=pltpu.CompilerParams(
            dimension_semantics=("arbitrary",), vmem_limit_bytes=VMEM_LIMIT),
        name="out_proj_router",
    )(xp, xs, o_p, o_s, wo, g, wr, br)


def _dispatch_kernel(bv_ref, nz_ref, pos_ref, xn_hbm, xs_hbm, zeros, stage, in_sems, out_sems, zsem,
                     *, n_blocks):
    @pl.when(pl.program_id(0) == 0)
    def _():
        zeros[...] = jnp.zeros_like(zeros)

        def fill(b, carry):
            @pl.when(bv_ref[b] < MOE_TILE)
            def _():
                r0 = pl.multiple_of(b * (MOE_TILE * SLABS), MOE_TILE * SLABS)
                pltpu.make_async_copy(zeros, xs_hbm.at[pl.ds(r0, MOE_TILE * SLABS)], zsem).start()
            return carry

        def drain(b, carry):
            pltpu.make_async_copy(zeros, xs_hbm.at[pl.ds(0, MOE_TILE * SLABS)], zsem).wait()
            return carry

        lax.fori_loop(0, n_blocks, fill, 0)
        lax.fori_loop(0, nz_ref[0], drain, 0)

    i = pl.program_id(0)
    nt = pl.num_programs(0)
    tile_rows = DISPATCH_TILE * SLABS

    def fetch(t, slot):
        return pltpu.make_async_copy(
            xn_hbm.at[pl.ds(pl.multiple_of(t * tile_rows, tile_rows), tile_rows)],
            stage.at[pl.ds(pl.multiple_of(slot * tile_rows, tile_rows), tile_rows)], in_sems.at[slot])

    def wait_rows(slot):
        for _ in range(TOP_K):
            pltpu.make_async_copy(stage.at[pl.ds(0, tile_rows)], xs_hbm.at[pl.ds(0, tile_rows)],
                                  out_sems.at[slot]).wait()

    @pl.when(i == 0)
    def _():
        fetch(0, 0).start()

    @pl.when(i + 1 < nt)
    def _():
        fetch(i + 1, (i + 1) % 3).start()

    fetch(i, i % 3).wait()
    base = (i % 3) * tile_rows

    def start(r, carry):
        src = stage.at[pl.ds(pl.multiple_of(base + r * SLABS, SLABS), SLABS)]
        for k in range(TOP_K):
            dst0 = pl.multiple_of(pos_ref[0, 0, k * DISPATCH_TILE + r], SLABS)
            pltpu.make_async_copy(src, xs_hbm.at[pl.ds(dst0, SLABS)], out_sems.at[i % 2]).start(priority=k % 2)
        return carry

    lax.fori_loop(0, DISPATCH_TILE, start, 0, unroll=4)

    @pl.when(i > 0)
    def _():
        wait_rows((i + 1) % 2)

    @pl.when(i == nt - 1)
    def _():
        wait_rows(i % 2)


def _dispatch(block_valid, pos_tiles, xn2):
    t = xn2.shape[0] // SLABS
    nt = t // DISPATCH_TILE
    n_blocks = block_valid.shape[0]
    n_zero = jnp.sum((block_valid < MOE_TILE).astype(i32)).reshape(1)
    grid_spec = pltpu.PrefetchScalarGridSpec(
        num_scalar_prefetch=2,
        grid=(nt,),
        in_specs=[
            pl.BlockSpec((1, 1, DISPATCH_TILE * TOP_K), lambda i, bv, nz: (i, 0, 0),
                         memory_space=pltpu.SMEM),
            pl.BlockSpec(memory_space=pl.ANY),
        ],
        out_specs=pl.BlockSpec(memory_space=pl.ANY),
        scratch_shapes=[
            pltpu.VMEM((MOE_TILE * SLABS, LANES), f32),
            pltpu.VMEM((3 * DISPATCH_TILE * SLABS, LANES), f32),
            pltpu.SemaphoreType.DMA((3,)),
            pltpu.SemaphoreType.DMA((2,)),
            pltpu.SemaphoreType.DMA(()),
        ],
    )
    return pl.pallas_call(
        functools.partial(_dispatch_kernel, n_blocks=n_blocks),
        grid_spec=grid_spec,
        out_shape=jax.ShapeDtypeStruct((n_blocks * MOE_TILE * SLABS, LANES), f32),
        compiler_params=pltpu.CompilerParams(
            dimension_semantics=("arbitrary",), vmem_limit_bytes=VMEM_LIMIT),
        name="moe_dispatch",
    )(block_valid, n_zero, pos_tiles, xn2)


def _expert_kernel(be_ref, seg_ref, nxt_ref, nb_ref, x_ref, wgu_hbm, bgu_ref, wdn_hbm, bdn_ref, y_ref,
                   wgu_f32, wdn_f32, wgu_bf, wdn_bf, sems, *, ff_chunk, cast_rows):
    i = pl.program_id(0)

    @pl.when(i >= nb_ref[0])
    def _():
        y_ref[...] = jnp.zeros_like(y_ref)

    def fetch(e, slot):
        return (pltpu.make_async_copy(wgu_hbm.at[e], wgu_f32.at[slot], sems.at[slot, 0]),
                pltpu.make_async_copy(wdn_hbm.at[e], wdn_f32.at[slot], sems.at[slot, 1]))

    @pl.when((i < nb_ref[0]) & ((i == 0) | (be_ref[i] != be_ref[jnp.maximum(i - 1, 0)])))
    def _():
        slot = seg_ref[i] % 2

        @pl.when(i == 0)
        def _():
            for c in fetch(be_ref[0], 0):
                c.start()

        for c in fetch(be_ref[i], slot):
            c.wait()

        @pl.when(nxt_ref[i] >= 0)
        def _():
            for c in fetch(nxt_ref[i], 1 - slot):
                c.start()

        def cast(j, carry):
            r0 = pl.multiple_of(j * cast_rows, cast_rows)
            wgu_bf[pl.ds(r0, cast_rows), :] = wgu_f32[slot, pl.ds(r0, cast_rows), :].astype(bf16)
            wdn_bf[pl.ds(r0, cast_rows), :] = wdn_f32[slot, pl.ds(r0, cast_rows), :].astype(bf16)
            return carry
        lax.fori_loop(0, D_MODEL // cast_rows, cast, 0)

    @pl.when(i < nb_ref[0])
    def _():
        xb = jnp.concatenate(
            [_load_slab_cols(x_ref, 0, MOE_TILE, s).astype(bf16) for s in range(SLABS)], axis=1)
        acc = jnp.zeros((MOE_TILE, D_MODEL), f32)
        for c0 in range(0, D_FF, ff_chunk):
            gate = _dot(xb, wgu_bf[:, c0:c0 + ff_chunk]) + bgu_ref[0, :, c0:c0 + ff_chunk]
            up = (_dot(xb, wgu_bf[:, D_FF + c0:D_FF + c0 + ff_chunk])
                  + bgu_ref[0, :, D_FF + c0:D_FF + c0 + ff_chunk])
            gate = jnp.minimum(gate, SWIGLU_LIMIT)
            up = jnp.clip(up, -SWIGLU_LIMIT, SWIGLU_LIMIT)
            hid = (up + 1.0) * (gate * _sigmoid(gate * SWIGLU_ALPHA))
            acc = acc + _dot(hid.astype(bf16), wdn_bf[c0:c0 + ff_chunk, :])
        _store_slabs(y_ref, acc + bdn_ref[0])


def _experts(block_e, block_seg, next_e, n_used, xs, wgu, bgu, wdn, bdn):
    n_blocks = block_e.shape[0]
    grid_spec = pltpu.PrefetchScalarGridSpec(
        num_scalar_prefetch=4,
        grid=(n_blocks,),
        in_specs=[
            pl.BlockSpec((MOE_TILE * SLABS, LANES),
                         lambda i, be, sg, nx, nb: (jnp.minimum(i, nb[0] - 1), 0)),
            pl.BlockSpec(memory_space=pl.ANY),
            pl.BlockSpec((1, 1, 2 * D_FF), lambda i, be, sg, nx, nb: (be[i], 0, 0)),
            pl.BlockSpec(memory_space=pl.ANY),
            pl.BlockSpec((1, 1, D_MODEL), lambda i, be, sg, nx, nb: (be[i], 0, 0)),
        ],
        out_specs=pl.BlockSpec((MOE_TILE * SLABS, LANES), lambda i, be, sg, nx, nb: (i, 0)),
        scratch_shapes=[
            pltpu.VMEM((2, D_MODEL, 2 * D_FF), f32),
            pltpu.VMEM((2, D_FF, D_MODEL), f32),
            pltpu.VMEM((D_MODEL, 2 * D_FF), bf16),
            pltpu.VMEM((D_FF, D_MODEL), bf16),
            pltpu.SemaphoreType.DMA((2, 2)),
        ],
    )
    return pl.pallas_call(
        functools.partial(_expert_kernel, ff_chunk=256, cast_rows=128),
        grid_spec=grid_spec,
        out_shape=jax.ShapeDtypeStruct((n_blocks * MOE_TILE * SLABS, LANES), f32),
        compiler_params=pltpu.CompilerParams(
            dimension_semantics=("arbitrary",), vmem_limit_bytes=VMEM_LIMIT),
        name="moe_experts",
    )(block_e, block_seg, next_e, n_used, xs, wgu, bgu, wdn, bdn)


def _combine_kernel(pos_cur, pos_nxt, ys_hbm, h_ref, tg_ref, pp_ref, ps_ref, gple_ref, wpg_ref, wpp_ref,
                    gfin_ref, yp_ref, ys_ref, rows, sems, *, n_prompt_tiles, issue_unroll):
    i = pl.program_id(0)
    nt = pl.num_programs(0)
    n = COMBINE_TILE * TOP_K
    slot_rows = n * SLABS

    def row_copy(pos_ref, slot_base, r, slot):
        src0 = pl.multiple_of(pos_ref[0, 0, r], SLABS)
        dst0 = pl.multiple_of(slot_base + r * SLABS, SLABS)
        return pltpu.make_async_copy(ys_hbm.at[pl.ds(src0, SLABS)], rows.at[pl.ds(dst0, SLABS)],
                                     sems.at[slot])

    def wait_slot(slot):
        pltpu.make_async_copy(ys_hbm.at[pl.ds(0, slot_rows)],
                              rows.at[pl.ds(pl.multiple_of(slot * slot_rows, slot_rows), slot_rows)],
                              sems.at[slot]).wait()

    def gather(pos_ref, slot):
        slot_base = slot * slot_rows

        def start(j, carry):
            for u in range(issue_unroll):
                row_copy(pos_ref, slot_base, issue_unroll * j + u, slot).start(priority=u % 2)
            return carry
        lax.fori_loop(0, n // issue_unroll, start, 0)

    @pl.when(i == 0)
    def _():
        gather(pos_cur, 0)

    @pl.when(i + 1 < nt)
    def _():
        gather(pos_nxt, (i + 1) % 2)

    slot = i % 2
    base = pl.multiple_of(slot * slot_rows, slot_rows)
    wait_slot(slot)

    tg = tg_ref[...]
    cols = []
    for s in range(SLABS):
        moe = jnp.zeros((COMBINE_TILE, LANES), f32)
        for k in range(TOP_K):
            moe = moe + tg[:, k:k + 1] * _load_slab_cols(
                rows, base + k * COMBINE_TILE * SLABS, COMBINE_TILE, s)
        cols.append(h_ref[:, s * LANES:(s + 1) * LANES] + moe)
    h = jnp.concatenate(cols, axis=1)
    gate = _sigmoid(_dot(_rms(h, gple_ref[...]).astype(bf16), wpg_ref[...]))
    p = _select_rows(i < n_prompt_tiles, pp_ref, ps_ref).astype(bf16)
    h = h + gate * _dot(p, wpp_ref[...])
    y = _rms(h, gfin_ref[...])

    @pl.when(i < n_prompt_tiles)
    def _():
        yp_ref[...] = y

    @pl.when(i >= n_prompt_tiles)
    def _():
        ys_ref[...] = y


def _combine(pos_tiles, ys, h1, tg, pp, ps, gple, wpg, wpp, gfin):
    t = h1.shape[0]
    nt = t // COMBINE_TILE
    npt = pp.shape[0] // COMBINE_TILE
    n = COMBINE_TILE * TOP_K
    row = lambda w: pl.BlockSpec((COMBINE_TILE, w), lambda i: (i, 0))
    full = lambda shape: pl.BlockSpec(shape, lambda i: (0,) * len(shape))
    return pl.pallas_call(
        functools.partial(_combine_kernel, n_prompt_tiles=npt, issue_unroll=16),
        grid=(nt,),
        in_specs=[
            pl.BlockSpec((1, 1, n), lambda i: (i, 0, 0), memory_space=pltpu.SMEM),
            pl.BlockSpec((1, 1, n), lambda i: (jnp.minimum(i + 1, nt - 1), 0, 0), memory_space=pltpu.SMEM),
            pl.BlockSpec(memory_space=pl.ANY),
            row(D_MODEL), row(LANES),
            pl.BlockSpec((COMBINE_TILE, PLE_DIM), lambda i: (jnp.minimum(i, npt - 1), 0)),
            pl.BlockSpec((COMBINE_TILE, PLE_DIM), lambda i: (jnp.maximum(i - npt, 0), 0)),
            full((1, D_MODEL)), full((D_MODEL, D_MODEL)), full((PLE_DIM, D_MODEL)),
            full((1, D_MODEL)),
        ],
        out_specs=[
            pl.BlockSpec((COMBINE_TILE, D_MODEL), lambda i: (jnp.minimum(i, npt - 1), 0)),
            pl.BlockSpec((COMBINE_TILE, D_MODEL), lambda i: (jnp.maximum(i - npt, 0), 0)),
        ],
        out_shape=[
            jax.ShapeDtypeStruct((pp.shape[0], D_MODEL), f32),
            jax.ShapeDtypeStruct((ps.shape[0], D_MODEL), f32),
        ],
        scratch_shapes=[
            pltpu.VMEM((2 * n * SLABS, LANES), f32),
            pltpu.SemaphoreType.DMA((2,)),
        ],
        compiler_params=pltpu.CompilerParams(
            dimension_semantics=("arbitrary",), vmem_limit_bytes=VMEM_LIMIT),
        name="moe_combine_ple",
    )(pos_tiles, pos_tiles, ys, h1, tg, pp, ps, gple, wpg, wpp, gfin)


def _routing_tables(top_i, rank, counts, n_blocks):
    padded = (counts + MOE_TILE - 1) // MOE_TILE * MOE_TILE
    pend = jnp.cumsum(padded)
    pstart = pend - padded
    block_rows = jnp.arange(n_blocks, dtype=i32) * MOE_TILE
    block_e = jnp.minimum(
        jnp.sum((block_rows[:, None] >= pend[None, :]).astype(i32), axis=1), N_EXPERTS - 1)
    block_valid = jnp.clip(pstart[block_e] + counts[block_e] - block_rows, 0, MOE_TILE)
    n_used = pend[-1:] // MOE_TILE
    blk = jnp.arange(n_blocks, dtype=i32)
    first = jnp.concatenate([jnp.ones((1,), bool), block_e[1:] != block_e[:-1]])
    block_seg = jnp.cumsum(first.astype(i32)) - 1
    later = (blk[None, :] > blk[:, None]) & (block_e[None, :] != block_e[:, None]) & (blk[None, :] < n_used)
    next_e = jnp.where(jnp.any(later, axis=1), block_e[jnp.argmax(later, axis=1)], -1)
    seg = jnp.zeros_like(rank)
    for e in range(N_EXPERTS):
        seg = jnp.where(top_i == e, pstart[e], seg)
    pos = (rank + seg) * SLABS
    nt = pos.shape[1] // DISPATCH_TILE
    pos_tiles = pos.reshape(TOP_K, nt, DISPATCH_TILE).transpose(1, 0, 2).reshape(nt, 1, TOP_K * DISPATCH_TILE)
    tables = (block_e, block_seg, next_e, block_valid, n_used)
    return (pos_tiles.astype(i32),) + tuple(t.astype(i32) for t in tables)


def kernel(x_prompt, x_sample, p_prompt, p_sample, state_gla, state_hgrn, g_mix, w_in, w_gk2, b_gk2,
           gn_gla, lb_logits, gn_hgrn, w_out, g_ffn, w_router, b_router, w_gu, b_gu, w_dn, b_dn,
           g_ple, w_ple_gate, w_ple_proj, g_final):
    batch, seq, _ = x_prompt.shape
    dec_batch, dec_seq, _ = x_sample.shape
    tp, ts = batch * seq, dec_batch * dec_seq
    xp = x_prompt.reshape(tp, D_MODEL)
    xs = x_sample.reshape(ts, D_MODEL)

    wgk = jnp.concatenate(
        [w_gk2[0], jnp.zeros((R1_PAD - GLA_LOWRANK, GLA_KDIM), f32)], axis=0).astype(bf16)
    wr = w_router[0].T.astype(bf16)
    br = b_router[0].reshape(N_EXPERTS, 1)

    z = _in_proj(xp, xs, g_mix[0].reshape(1, D_MODEL), w_in[0])

    mixer_params = (wgk, b_gk2[0].reshape(1, GLA_KDIM), gn_gla[0].reshape(1, GLA_DV),
                    lb_logits[0:2], gn_hgrn[0].reshape(1, HGRN_DV))
    o_p, new_gla_p, new_hgrn_p = _mixer_prompt(
        z, *mixer_params, batch=batch, seq=seq, rows_per_step=512, block_rows=256)
    z_s = jnp.pad(z[tp:].reshape(dec_batch, dec_seq, Z_W), ((0, 0), (0, SAMPLE_PAD_T - dec_seq), (0, 0)))
    o_s, new_gla_s, new_hgrn_s = _mixer_sample(
        z_s.reshape(dec_batch * SAMPLE_PAD_T, Z_W), state_gla, state_hgrn, *mixer_params,
        steps=dec_seq, batch_block=8)
    o_s = o_s.reshape(dec_batch, SAMPLE_PAD_T, D_MODEL)[:, :dec_seq].reshape(ts, D_MODEL).astype(bf16)

    h1, xn2, top_i, top_g, rank, counts = _out_proj(
        xp, xs, o_p, o_s, w_out[0].astype(bf16), g_ffn[0].reshape(1, D_MODEL), wr, br)

    t = tp + ts
    n_blocks = -(-(t * TOP_K) // MOE_TILE) + N_EXPERTS
    pos_tiles, block_e, block_seg, next_e, block_valid, n_used = _routing_tables(
        top_i[:TOP_K], rank[:TOP_K], counts[:, 0], n_blocks)
    x_sorted = _dispatch(block_valid, pos_tiles, xn2)
    gates = jnp.pad(top_g[:TOP_K].T, ((0, 0), (0, LANES - TOP_K)))
    y_sorted = _experts(block_e, block_seg, next_e, n_used, x_sorted,
                        w_gu[0], b_gu[0].reshape(N_EXPERTS, 1, 2 * D_FF),
                        w_dn[0], b_dn[0].reshape(N_EXPERTS, 1, D_MODEL))

    y_p, y_s = _combine(pos_tiles, y_sorted, h1, gates, p_prompt[0].reshape(tp, PLE_DIM),
                        p_sample[0].reshape(ts, PLE_DIM), g_ple[0].reshape(1, D_MODEL),
                        w_ple_gate[0].astype(bf16), w_ple_proj[0].astype(bf16), g_final.reshape(1, D_MODEL))

    return (y_p.reshape(batch, seq, D_MODEL), y_s.reshape(dec_batch, dec_seq, D_MODEL),
            new_gla_p, new_hgrn_p, new_gla_s, new_hgrn_s)
```

```python
import functools

import jax
import jax.numpy as jnp
from jax import lax
from jax.experimental import pallas as pl
from jax.experimental.pallas import tpu as pltpu

f32 = jnp.float32
bf16 = jnp.bfloat16
i32 = jnp.int32

D_MODEL = 1024
GLA_HEADS = 4
GLA_DK = 64
GLA_DV = 128
GLA_KDIM = GLA_HEADS * GLA_DK
GLA_WIDTH = GLA_HEADS * GLA_DV
GLA_LOWRANK = 16
GLA_NORMALIZER = 16.0
HGRN_HEADS = 4
HGRN_DK = 128
HGRN_DV = 128
HGRN_FDIM = HGRN_HEADS * HGRN_DK
HGRN_WIDTH = HGRN_HEADS * HGRN_DV
CHUNK = 64
N_EXPERTS = 32
TOP_K = 4
D_FF = 1024
SWIGLU_LIMIT = 7.0
SWIGLU_ALPHA = 1.702
PLE_DIM = 256
EPS = 1e-6

LANES = 128
SAMPLE_PAD_T = 8
SLABS = D_MODEL // LANES
ROUTE_ROWS = 8
R1_PAD = LANES
Z_Q1 = 0
Z_K1 = Z_Q1 + GLA_KDIM
Z_V1 = Z_K1 + GLA_KDIM
Z_G1 = Z_V1 + GLA_WIDTH
Z_R1 = Z_G1 + GLA_WIDTH
Z_Q2 = Z_R1 + R1_PAD
Z_F2 = Z_Q2 + HGRN_FDIM
Z_I2 = Z_F2 + HGRN_FDIM
Z_G2 = Z_I2 + HGRN_WIDTH
Z_W = Z_G2 + HGRN_WIDTH

TOKEN_TILE = 512
MOE_TILE = 512
DISPATCH_TILE = 256
COMBINE_TILE = DISPATCH_TILE
VMEM_LIMIT = 56 * 1024 * 1024


def _rms(x, g):
    return x * lax.rsqrt(jnp.mean(x * x, axis=-1, keepdims=True) + EPS) * g


def _sigmoid(x):
    return 0.5 * jnp.tanh(0.5 * x) + 0.5


def _dot(a, b):
    return jnp.dot(a, b, preferred_element_type=f32)


def _dot_nt(a, b):
    return lax.dot_general(a, b, (((1,), (1,)), ((), ())), preferred_element_type=f32)


def _dot_tn(a, b):
    return lax.dot_general(a, b, (((0,), (0,)), ((), ())), preferred_element_type=f32)


def _cumsum_rows(tri, x):
    hi = x.astype(bf16)
    r1 = x - hi.astype(f32)
    mid = r1.astype(bf16)
    lo = (r1 - mid.astype(f32)).astype(bf16)
    return _dot(tri, hi) + _dot(tri, mid) + _dot(tri, lo)


def _select_rows(is_prompt, a_ref, b_ref):
    return jnp.where(is_prompt, a_ref[...], b_ref[...])


def _inproj_kernel(xp_ref, xs_ref, g_ref, w_hbm, z_ref, w_f32, w_bf, sem,
                   *, n_prompt_tiles, col_chunk, cast_rows):
    @pl.when(pl.program_id(0) == 0)
    def _():
        copy = pltpu.make_async_copy(w_hbm, w_f32, sem)
        copy.start()
        copy.wait()
        r1_end = Z_R1 + GLA_LOWRANK

        def cast(j, carry):
            rows = pl.ds(pl.multiple_of(j * cast_rows, cast_rows), cast_rows)
            w_bf[rows, 0:Z_R1] = w_f32[rows, 0:Z_R1].astype(bf16)
            w_bf[rows, Z_R1:Z_Q2] = jnp.concatenate(
                [w_f32[rows, Z_R1:r1_end], jnp.zeros((cast_rows, R1_PAD - GLA_LOWRANK), f32)],
                axis=1).astype(bf16)
            w_bf[rows, Z_Q2:Z_W] = w_f32[rows, r1_end:].astype(bf16)
            return carry
        lax.fori_loop(0, D_MODEL // cast_rows, cast, 0)

    is_prompt = pl.program_id(0) < n_prompt_tiles
    x = _select_rows(is_prompt, xp_ref, xs_ref)
    a = _rms(x, g_ref[...]).astype(bf16)
    for c0 in range(0, Z_W, col_chunk):
        c1 = min(c0 + col_chunk, Z_W)
        z_ref[:, c0:c1] = _dot(a, w_bf[:, c0:c1]).astype(bf16)


def _in_proj(xp, xs, g, w):
    tp, ts = xp.shape[0], xs.shape[0]
    npt = tp // TOKEN_TILE
    nt = npt + ts // TOKEN_TILE
    return pl.pallas_call(
        functools.partial(_inproj_kernel, n_prompt_tiles=npt, col_chunk=512, cast_rows=64),
        grid=(nt,),
        in_specs=[
            pl.BlockSpec((TOKEN_TILE, D_MODEL), lambda i: (jnp.minimum(i, npt - 1), 0)),
            pl.BlockSpec((TOKEN_TILE, D_MODEL), lambda i: (jnp.maximum(i - npt, 0), 0)),
            pl.BlockSpec((1, D_MODEL), lambda i: (0, 0)),
            pl.BlockSpec(memory_space=pl.ANY),
        ],
        out_specs=pl.BlockSpec((TOKEN_TILE, Z_W), lambda i: (i, 0)),
        out_shape=jax.ShapeDtypeStruct((tp + ts, Z_W), bf16),
        scratch_shapes=[
            pltpu.VMEM(w.shape, f32),
            pltpu.VMEM((D_MODEL, Z_W), bf16),
            pltpu.SemaphoreType.DMA(()),
        ],
        compiler_params=pltpu.CompilerParams(
            dimension_semantics=("arbitrary",), vmem_limit_bytes=VMEM_LIMIT),
        name="in_proj",
    )(xp, xs, g, w)


def _block_consts(r, c):
    shift = c.bit_length() - 1
    row = lax.broadcasted_iota(i32, (r, r), 0)
    col = lax.broadcasted_iota(i32, (r, r), 1)
    causal = (lax.shift_right_logical(row, shift) == lax.shift_right_logical(col, shift)) & (row >= col)
    return causal.astype(bf16), causal


def _subchunk_row(x, c, idx):
    parts = [jnp.broadcast_to(x[j * c + idx:j * c + idx + 1, :], (c, x.shape[1]))
             for j in range(x.shape[0] // c)]
    return jnp.concatenate(parts, axis=0)


def _gated_block(q, k, v, log_a, states, *, c, scale, ref_row, tri, causal, heads_per_group, scan):
    r = q.shape[0]
    nsub = r // c
    groups = q.shape[1] // LANES
    b = _cumsum_rows(tri, log_a)
    b_ref = _subchunk_row(b, c, ref_row)
    b_last = _subchunk_row(b, c, c - 1)
    qs = q * scale
    q_r = qs * jnp.exp(b - b_ref)
    k_r = (k * jnp.exp(b_ref - b)).astype(bf16)
    k_end = k * jnp.exp(b_last - b)
    q_b = qs * jnp.exp(b)
    decay = jnp.exp(b_last)
    vb = v.astype(bf16)
    lane = lax.broadcasted_iota(i32, (r, LANES), 1)
    width = LANES // heads_per_group
    outs, new_states = [], []
    for g in range(groups):
        gl = slice(g * LANES, (g + 1) * LANES)
        masks = [None] if heads_per_group == 1 else [
            (lane >= j * width) & (lane < (j + 1) * width) for j in range(heads_per_group)]
        pick = lambda x, m: x if m is None else jnp.where(m, x, 0.0)
        q_r_h = [pick(q_r[:, gl], m).astype(bf16) for m in masks]
        q_b_h = [pick(q_b[:, gl], m).astype(bf16) for m in masks]
        k_end_h = [pick(k_end[:, gl], m).astype(bf16) for m in masks]
        v_h = [vb[:, (g * heads_per_group + j) * LANES:(g * heads_per_group + j + 1) * LANES]
               for j in range(heads_per_group)]
        o_intra = []
        for j in range(heads_per_group):
            att = _dot_nt(q_r_h[j], k_r[:, gl])
            att = jnp.where(causal, att, 0.0).astype(bf16)
            o_intra.append(_dot(att, v_h[j]))
        upd = []
        for s in range(nsub):
            rs = slice(s * c, (s + 1) * c)
            u = _dot_tn(v_h[0][rs], k_end_h[0][rs])
            for j in range(1, heads_per_group):
                u = u + _dot_tn(v_h[j][rs], k_end_h[j][rs])
            upd.append(u)
        o_inter = [[] for _ in range(heads_per_group)]
        st = states[g] if scan else None
        group_states = []
        for s in range(nsub):
            rs = slice(s * c, (s + 1) * c)
            if not scan:
                st = states[s][g]
            st_b = st.astype(bf16)
            for j in range(heads_per_group):
                o_inter[j].append(_dot_nt(q_b_h[j][rs], st_b))
            st = st * decay[s * c:s * c + 1, gl] + upd[s]
            if not scan:
                group_states.append(st)
        for j in range(heads_per_group):
            outs.append(o_intra[j] + jnp.concatenate(o_inter[j], axis=0))
        new_states.append(st if scan else group_states)
    if not scan:
        new_states = [[new_states[g][s] for g in range(groups)] for s in range(nsub)]
    return outs, new_states


def _head_norm(o, g):
    return o * lax.rsqrt(jnp.mean(o * o, axis=-1, keepdims=True) + EPS) * g


def _mixer_block(zb, st_gla, st_hgrn, wgk, bgk, gng, lb, gnh, *, c, ref_row, tri, causal, valid, scan):
    z = zb.astype(f32)
    gk = _dot(zb[:, Z_R1:Z_R1 + R1_PAD], wgk) + bgk
    log_a = (jnp.minimum(gk, 0.0) - jnp.log(1.0 + jnp.exp(-jnp.abs(gk)))) / GLA_NORMALIZER
    k1 = z[:, Z_K1:Z_K1 + GLA_KDIM]
    if valid is not None:
        log_a = jnp.where(valid, log_a, 0.0)
        k1 = jnp.where(valid, k1, 0.0)
    o1, st_gla = _gated_block(
        z[:, Z_Q1:Z_Q1 + GLA_KDIM], k1, z[:, Z_V1:Z_V1 + GLA_WIDTH], log_a, st_gla,
        c=c, scale=GLA_DK ** -0.5, ref_row=ref_row, tri=tri, causal=causal,
        heads_per_group=LANES // GLA_DK, scan=scan)
    forget = lb + (1.0 - lb) * _sigmoid(z[:, Z_F2:Z_F2 + HGRN_FDIM])
    k2 = 1.0 - forget
    log_f = jnp.log(forget)
    if valid is not None:
        log_f = jnp.where(valid, log_f, 0.0)
        k2 = jnp.where(valid, k2, 0.0)
    q2 = z[:, Z_Q2:Z_Q2 + HGRN_FDIM]
    i2 = z[:, Z_I2:Z_I2 + HGRN_WIDTH]
    o2, st_hgrn = _gated_block(
        q2 * _sigmoid(q2), k2, i2 * _sigmoid(i2), log_f, st_hgrn,
        c=c, scale=1.0, ref_row=ref_row, tri=tri, causal=causal, heads_per_group=1, scan=scan)
    cols = []
    for h in range(GLA_HEADS):
        g1 = z[:, Z_G1 + h * GLA_DV:Z_G1 + (h + 1) * GLA_DV]
        cols.append(_head_norm(o1[h], gng) * (g1 * _sigmoid(g1)))
    for h in range(HGRN_HEADS):
        g2 = z[:, Z_G2 + h * HGRN_DV:Z_G2 + (h + 1) * HGRN_DV]
        cols.append(_head_norm(o2[h], gnh) * _sigmoid(g2))
    return jnp.concatenate(cols, axis=1), st_gla, st_hgrn


def _lower_bound(lbl_ref):
    l = lbl_ref[...]
    m = jnp.max(l, axis=0, keepdims=True)
    e = jnp.exp(l - m)
    return e[0:1, :] / jnp.sum(e, axis=0, keepdims=True)


GLA_GROUPS = GLA_KDIM // LANES


def _gla_state_t(s_ref_heads):
    return s_ref_heads.reshape(LANES, GLA_DV).T


def _gla_state_from_t(st):
    return st.T.reshape(LANES // GLA_DK, GLA_DK, GLA_DV)


def _mixer_prompt_kernel(z_ref, wgk_ref, bgk_ref, gng_ref, lbl_ref, gnh_ref,
                         o_ref, sg_out, sh_out, sg_t, sh_t, *, block_rows):
    j = pl.program_id(1)

    @pl.when(j == 0)
    def _():
        sg_t[...] = jnp.zeros_like(sg_t)
        sh_t[...] = jnp.zeros_like(sh_t)

    tri, causal = _block_consts(block_rows, CHUNK)
    lb = _lower_bound(lbl_ref)
    wgk, bgk, gng, gnh = wgk_ref[...], bgk_ref[...], gng_ref[...], gnh_ref[...]

    def body(blk, carry):
        r0 = pl.multiple_of(blk * block_rows, block_rows)
        o, st_g, st_h = _mixer_block(
            z_ref[pl.ds(r0, block_rows), :], [sg_t[g] for g in range(GLA_GROUPS)],
            [sh_t[h] for h in range(HGRN_HEADS)], wgk, bgk, gng, lb, gnh,
            c=CHUNK, ref_row=CHUNK // 2, tri=tri, causal=causal, valid=None, scan=True)
        o_ref[pl.ds(r0, block_rows), :] = o.astype(bf16)
        for g in range(GLA_GROUPS):
            sg_t[g] = st_g[g]
        for h in range(HGRN_HEADS):
            sh_t[h] = st_h[h]
        return carry

    lax.fori_loop(0, z_ref.shape[0] // block_rows, body, 0)

    @pl.when(j == pl.num_programs(1) - 1)
    def _():
        per = LANES // GLA_DK
        for g in range(GLA_GROUPS):
            sg_out[0, 0, g * per:(g + 1) * per] = _gla_state_from_t(sg_t[g])
        for h in range(HGRN_HEADS):
            sh_out[0, 0, h] = sh_t[h].T


def _mixer_prompt(z, wgk, bgk, gng, lbl, gnh, *, batch, seq, rows_per_step, block_rows):
    nj = seq // rows_per_step
    full = lambda shape: pl.BlockSpec(shape, lambda b, j: (0,) * len(shape))
    return pl.pallas_call(
        functools.partial(_mixer_prompt_kernel, block_rows=block_rows),
        grid=(batch, nj),
        in_specs=[
            pl.BlockSpec((rows_per_step, Z_W), lambda b, j: (b * nj + j, 0)),
            full((R1_PAD, GLA_KDIM)), full((1, GLA_KDIM)), full((1, GLA_DV)),
            full((2, HGRN_FDIM)), full((1, HGRN_DV)),
        ],
        out_specs=[
            pl.BlockSpec((rows_per_step, D_MODEL), lambda b, j: (b * nj + j, 0)),
            pl.BlockSpec((1, 1, GLA_HEADS, GLA_DK, GLA_DV), lambda b, j: (0, b, 0, 0, 0)),
            pl.BlockSpec((1, 1, HGRN_HEADS, HGRN_DK, HGRN_DV), lambda b, j: (0, b, 0, 0, 0)),
        ],
        out_shape=[
            jax.ShapeDtypeStruct((batch * seq, D_MODEL), bf16),
            jax.ShapeDtypeStruct((1, batch, GLA_HEADS, GLA_DK, GLA_DV), f32),
            jax.ShapeDtypeStruct((1, batch, HGRN_HEADS, HGRN_DK, HGRN_DV), f32),
        ],
        scratch_shapes=[
            pltpu.VMEM((GLA_GROUPS, GLA_DV, LANES), f32),
            pltpu.VMEM((HGRN_HEADS, HGRN_DV, HGRN_DK), f32),
        ],
        compiler_params=pltpu.CompilerParams(
            dimension_semantics=("arbitrary", "arbitrary"), vmem_limit_bytes=VMEM_LIMIT),
        name="mixer_prompt",
    )(z, wgk, bgk, gng, lbl, gnh)


def _mixer_sample_kernel(z_ref, sg_in, sh_in, wgk_ref, bgk_ref, gng_ref, lbl_ref, gnh_ref,
                         o_ref, sg_out, sh_out, *, batch_block, steps):
    rows = batch_block * SAMPLE_PAD_T
    tri, causal = _block_consts(rows, SAMPLE_PAD_T)
    step = lax.broadcasted_iota(i32, (rows, 1), 0) & (SAMPLE_PAD_T - 1)
    valid = step < steps
    lb = _lower_bound(lbl_ref)
    per = LANES // GLA_DK
    st_g = [[_gla_state_t(sg_in[0, s, g * per:(g + 1) * per]) for g in range(GLA_GROUPS)]
            for s in range(batch_block)]
    st_h = [[sh_in[0, s, h].T for h in range(HGRN_HEADS)] for s in range(batch_block)]
    o, st_g, st_h = _mixer_block(
        z_ref[...], st_g, st_h, wgk_ref[...], bgk_ref[...], gng_ref[...], lb, gnh_ref[...],
        c=SAMPLE_PAD_T, ref_row=steps // 2, tri=tri, causal=causal, valid=valid, scan=False)
    o_ref[...] = o
    for s in range(batch_block):
        for g in range(GLA_GROUPS):
            sg_out[0, s, g * per:(g + 1) * per] = _gla_state_from_t(st_g[s][g])
        for h in range(HGRN_HEADS):
            sh_out[0, s, h] = st_h[s][h].T


def _mixer_sample(z2, state_gla, state_hgrn, wgk, bgk, gng, lbl, gnh, *, steps, batch_block):
    batch = z2.shape[0] // SAMPLE_PAD_T
    rows = batch_block * SAMPLE_PAD_T
    full = lambda shape: pl.BlockSpec(shape, lambda i: (0,) * len(shape))
    sg_spec = pl.BlockSpec((1, batch_block, GLA_HEADS, GLA_DK, GLA_DV), lambda i: (0, i, 0, 0, 0))
    sh_spec = pl.BlockSpec((1, batch_block, HGRN_HEADS, HGRN_DK, HGRN_DV), lambda i: (0, i, 0, 0, 0))
    return pl.pallas_call(
        functools.partial(_mixer_sample_kernel, batch_block=batch_block, steps=steps),
        grid=(batch // batch_block,),
        in_specs=[
            pl.BlockSpec((rows, Z_W), lambda i: (i, 0)),
            sg_spec, sh_spec,
            full((R1_PAD, GLA_KDIM)), full((1, GLA_KDIM)), full((1, GLA_DV)),
            full((2, HGRN_FDIM)), full((1, HGRN_DV)),
        ],
        out_specs=[pl.BlockSpec((rows, D_MODEL), lambda i: (i, 0)), sg_spec, sh_spec],
        out_shape=[
            jax.ShapeDtypeStruct((batch * SAMPLE_PAD_T, D_MODEL), f32),
            jax.ShapeDtypeStruct(state_gla.shape, f32),
            jax.ShapeDtypeStruct(state_hgrn.shape, f32),
        ],
        compiler_params=pltpu.CompilerParams(
            dimension_semantics=("arbitrary",), vmem_limit_bytes=VMEM_LIMIT),
        name="mixer_sample",
    )(z2, state_gla, state_hgrn, wgk, bgk, gng, lbl, gnh)


def _store_slabs(ref, x):
    for s in range(SLABS):
        ref[pl.ds(s, x.shape[0], stride=SLABS), :] = x[:, s * LANES:(s + 1) * LANES]


def _load_slab_cols(ref, start, rows, s):
    return ref[pl.ds(start + s, rows, stride=SLABS), :]


def _outproj_kernel(xp_ref, xs_ref, op_ref, os_ref, wo_ref, g_ref, wr_ref, br_ref,
                    h_ref, xn_ref, ti_ref, tg_ref, rank_ref, cnt_ref, run_cnt, *, n_prompt_tiles):
    i = pl.program_id(0)

    @pl.when(i == 0)
    def _():
        run_cnt[...] = jnp.zeros_like(run_cnt)

    x = _select_rows(i < n_prompt_tiles, xp_ref, xs_ref)
    o = _select_rows(i < n_prompt_tiles, op_ref, os_ref)
    h = x + _dot(o, wo_ref[...])
    h_ref[...] = h
    xn = _rms(h, g_ref[...])
    _store_slabs(xn_ref, xn)
    l = _dot_nt(wr_ref[...], xn.astype(bf16)) + br_ref[...]
    expert = lax.broadcasted_iota(i32, l.shape, 0)
    neg = jnp.float32(-jnp.inf)
    vals, idxs = [], []
    for _ in range(TOP_K):
        m = jnp.max(l, axis=0, keepdims=True)
        idx = jnp.min(jnp.where(l == m, expert, N_EXPERTS), axis=0, keepdims=True)
        vals.append(m)
        idxs.append(idx)
        l = jnp.where(expert == idx, neg, l)
    es = [jnp.exp(v - vals[0]) for v in vals]
    denom = es[0] + es[1] + es[2] + es[3]

    onehot = jnp.zeros(l.shape, f32)
    for k in range(TOP_K):
        onehot = onehot + (expert == idxs[k]).astype(f32)
    n = l.shape[1]
    earlier = (lax.broadcasted_iota(i32, (n, n), 0) < lax.broadcasted_iota(i32, (n, n), 1)).astype(bf16)
    ahead = _dot(onehot.astype(bf16), earlier) + run_cnt[:, 0:1]
    run_cnt[...] = run_cnt[...] + jnp.sum(onehot, axis=1, keepdims=True)
    cnt_ref[...] = run_cnt[...].astype(i32)

    ranks = [jnp.sum(jnp.where(expert == idxs[k], ahead, 0.0), axis=0, keepdims=True).astype(i32)
             for k in range(TOP_K)]
    pad_i = jnp.zeros((ROUTE_ROWS - TOP_K, n), i32)
    ti_ref[...] = jnp.concatenate(idxs + [pad_i], axis=0)
    rank_ref[...] = jnp.concatenate(ranks + [pad_i], axis=0)
    tg_ref[...] = jnp.concatenate([e / denom for e in es] + [pad_i.astype(f32)], axis=0)


def _out_proj(xp, xs, o_p, o_s, wo, g, wr, br):
    tp, ts = xp.shape[0], xs.shape[0]
    npt = tp // TOKEN_TILE
    nt = npt + ts // TOKEN_TILE
    t = tp + ts
    row = lambda w: pl.BlockSpec((TOKEN_TILE, w), lambda i: (i, 0))
    route = pl.BlockSpec((ROUTE_ROWS, TOKEN_TILE), lambda i: (0, i))
    full = lambda shape: pl.BlockSpec(shape, lambda i: (0,) * len(shape))
    prompt_rows = pl.BlockSpec((TOKEN_TILE, D_MODEL), lambda i: (jnp.minimum(i, npt - 1), 0))
    sample_rows = pl.BlockSpec((TOKEN_TILE, D_MODEL), lambda i: (jnp.maximum(i - npt, 0), 0))
    return pl.pallas_call(
        functools.partial(_outproj_kernel, n_prompt_tiles=npt),
        grid=(nt,),
        in_specs=[
            prompt_rows, sample_rows, prompt_rows, sample_rows,
            full((D_MODEL, D_MODEL)), full((1, D_MODEL)),
            full((N_EXPERTS, D_MODEL)), full((N_EXPERTS, 1)),
        ],
        out_specs=[row(D_MODEL), pl.BlockSpec((TOKEN_TILE * SLABS, LANES), lambda i: (i, 0)),
                   route, route, route, full((N_EXPERTS, LANES))],
        out_shape=[
            jax.ShapeDtypeStruct((t, D_MODEL), f32),
            jax.ShapeDtypeStruct((t * SLABS, LANES), f32),
            jax.ShapeDtypeStruct((ROUTE_ROWS, t), i32),
            jax.ShapeDtypeStruct((ROUTE_ROWS, t), f32),
            jax.ShapeDtypeStruct((ROUTE_ROWS, t), i32),
            jax.ShapeDtypeStruct((N_EXPERTS, LANES), i32),
        ],
        scratch_shapes=[pltpu.VMEM((N_EXPERTS, LANES), f32)],
        compiler_params=pltpu.CompilerParams(
            dimension_semantics=("arbitrary",), vmem_limit_bytes=VMEM_LIMIT),
        name="out_proj_router",
    )(xp, xs, o_p, o_s, wo, g, wr, br)


def _dispatch_kernel(bv_ref, nz_ref, pos_ref, xn_hbm, xs_hbm, zeros, stage, in_sems, out_sems, zsem,
                     *, n_blocks):
    @pl.when(pl.program_id(0) == 0)
    def _():
        zeros[...] = jnp.zeros_like(zeros)

        def fill(b, carry):
            @pl.when(bv_ref[b] < MOE_TILE)
            def _():
                r0 = pl.multiple_of(b * (MOE_TILE * SLABS), MOE_TILE * SLABS)
                pltpu.make_async_copy(zeros, xs_hbm.at[pl.ds(r0, MOE_TILE * SLABS)], zsem).start()
            return carry

        def drain(b, carry):
            pltpu.make_async_copy(zeros, xs_hbm.at[pl.ds(0, MOE_TILE * SLABS)], zsem).wait()
            return carry

        lax.fori_loop(0, n_blocks, fill, 0)
        lax.fori_loop(0, nz_ref[0], drain, 0)

    i = pl.program_id(0)
    nt = pl.num_programs(0)
    tile_rows = DISPATCH_TILE * SLABS

    def fetch(t, slot):
        return pltpu.make_async_copy(
            xn_hbm.at[pl.ds(pl.multiple_of(t * tile_rows, tile_rows), tile_rows)],
            stage.at[pl.ds(pl.multiple_of(slot * tile_rows, tile_rows), tile_rows)], in_sems.at[slot])

    def wait_rows(slot):
        for _ in range(TOP_K):
            pltpu.make_async_copy(stage.at[pl.ds(0, tile_rows)], xs_hbm.at[pl.ds(0, tile_rows)],
                                  out_sems.at[slot]).wait()

    @pl.when(i == 0)
    def _():
        fetch(0, 0).start()

    @pl.when(i + 1 < nt)
    def _():
        fetch(i + 1, (i + 1) % 3).start()

    fetch(i, i % 3).wait()
    base = (i % 3) * tile_rows

    def start(r, carry):
        src = stage.at[pl.ds(pl.multiple_of(base + r * SLABS, SLABS), SLABS)]
        for k in range(TOP_K):
            dst0 = pl.multiple_of(pos_ref[0, 0, k * DISPATCH_TILE + r], SLABS)
            pltpu.make_async_copy(src, xs_hbm.at[pl.ds(dst0, SLABS)], out_sems.at[i % 2]).start(priority=k % 2)
        return carry

    lax.fori_loop(0, DISPATCH_TILE, start, 0, unroll=4)

    @pl.when(i > 0)
    def _():
        wait_rows((i + 1) % 2)

    @pl.when(i == nt - 1)
    def _():
        wait_rows(i % 2)


def _dispatch(block_valid, pos_tiles, xn2):
    t = xn2.shape[0] // SLABS
    nt = t // DISPATCH_TILE
    n_blocks = block_valid.shape[0]
    n_zero = jnp.sum((block_valid < MOE_TILE).astype(i32)).reshape(1)
    grid_spec = pltpu.PrefetchScalarGridSpec(
        num_scalar_prefetch=2,
        grid=(nt,),
        in_specs=[
            pl.BlockSpec((1, 1, DISPATCH_TILE * TOP_K), lambda i, bv, nz: (i, 0, 0),
                         memory_space=pltpu.SMEM),
            pl.BlockSpec(memory_space=pl.ANY),
        ],
        out_specs=pl.BlockSpec(memory_space=pl.ANY),
        scratch_shapes=[
            pltpu.VMEM((MOE_TILE * SLABS, LANES), f32),
            pltpu.VMEM((3 * DISPATCH_TILE * SLABS, LANES), f32),
            pltpu.SemaphoreType.DMA((3,)),
            pltpu.SemaphoreType.DMA((2,)),
            pltpu.SemaphoreType.DMA(()),
        ],
    )
    return pl.pallas_call(
        functools.partial(_dispatch_kernel, n_blocks=n_blocks),
        grid_spec=grid_spec,
        out_shape=jax.ShapeDtypeStruct((n_blocks * MOE_TILE * SLABS, LANES), f32),
        compiler_params=pltpu.CompilerParams(
            dimension_semantics=("arbitrary",), vmem_limit_bytes=VMEM_LIMIT),
        name="moe_dispatch",
    )(block_valid, n_zero, pos_tiles, xn2)


def _expert_kernel(be_ref, seg_ref, nxt_ref, bv_ref, nb_ref, x_ref, wgu_hbm, bgu_ref, wdn_hbm, bdn_ref, y_ref,
                   wgu_f32, wdn_f32, wgu_bf, wdn_bf, sems, *, ff_chunk, cast_rows):
    i = pl.program_id(0)

    @pl.when(i >= nb_ref[0])
    def _():
        y_ref[...] = jnp.zeros_like(y_ref)

    def fetch(e, slot):
        return (pltpu.make_async_copy(wgu_hbm.at[e], wgu_f32.at[slot], sems.at[slot, 0]),
                pltpu.make_async_copy(wdn_hbm.at[e], wdn_f32.at[slot], sems.at[slot, 1]))

    @pl.when((i < nb_ref[0]) & ((i == 0) | (be_ref[i] != be_ref[jnp.maximum(i - 1, 0)])))
    def _():
        slot = seg_ref[i] % 2

        @pl.when(i == 0)
        def _():
            for c in fetch(be_ref[0], 0):
                c.start()

        for c in fetch(be_ref[i], slot):
            c.wait()

        @pl.when(nxt_ref[i] >= 0)
        def _():
            for c in fetch(nxt_ref[i], 1 - slot):
                c.start()

        def cast(j, carry):
            r0 = pl.multiple_of(j * cast_rows, cast_rows)
            wgu_bf[pl.ds(r0, cast_rows), :] = wgu_f32[slot, pl.ds(r0, cast_rows), :].astype(bf16)
            wdn_bf[pl.ds(r0, cast_rows), :] = wdn_f32[slot, pl.ds(r0, cast_rows), :].astype(bf16)
            return carry
        lax.fori_loop(0, D_MODEL // cast_rows, cast, 0)

    def mlp(rows):
        xb = jnp.concatenate(
            [_load_slab_cols(x_ref, 0, rows, s).astype(bf16) for s in range(SLABS)], axis=1)
        acc = jnp.zeros((rows, D_MODEL), f32)
        for c0 in range(0, D_FF, ff_chunk):
            gate = _dot(xb, wgu_bf[:, c0:c0 + ff_chunk]) + bgu_ref[0, :, c0:c0 + ff_chunk]
            up = (_dot(xb, wgu_bf[:, D_FF + c0:D_FF + c0 + ff_chunk])
                  + bgu_ref[0, :, D_FF + c0:D_FF + c0 + ff_chunk])
            gate = jnp.minimum(gate, SWIGLU_LIMIT)
            up = jnp.clip(up, -SWIGLU_LIMIT, SWIGLU_LIMIT)
            hid = (up + 1.0) * (gate * _sigmoid(gate * SWIGLU_ALPHA))
            acc = acc + _dot(hid.astype(bf16), wdn_bf[c0:c0 + ff_chunk, :])
        return acc + bdn_ref[0]

    half = MOE_TILE // 2

    @pl.when((i < nb_ref[0]) & (bv_ref[i] > half))
    def _():
        _store_slabs(y_ref, mlp(MOE_TILE))

    @pl.when((i < nb_ref[0]) & (bv_ref[i] <= half))
    def _():
        _store_slabs(y_ref, mlp(half))
        y_ref[pl.ds(half * SLABS, half * SLABS), :] = jnp.zeros((half * SLABS, LANES), f32)


def _experts(block_e, block_seg, next_e, block_valid, n_used, xs, wgu, bgu, wdn, bdn):
    n_blocks = block_e.shape[0]
    grid_spec = pltpu.PrefetchScalarGridSpec(
        num_scalar_prefetch=5,
        grid=(n_blocks,),
        in_specs=[
            pl.BlockSpec((MOE_TILE * SLABS, LANES),
                         lambda i, be, sg, nx, bv, nb: (jnp.minimum(i, nb[0] - 1), 0)),
            pl.BlockSpec(memory_space=pl.ANY),
            pl.BlockSpec((1, 1, 2 * D_FF), lambda i, be, sg, nx, bv, nb: (be[i], 0, 0)),
            pl.BlockSpec(memory_space=pl.ANY),
            pl.BlockSpec((1, 1, D_MODEL), lambda i, be, sg, nx, bv, nb: (be[i], 0, 0)),
        ],
        out_specs=pl.BlockSpec((MOE_TILE * SLABS, LANES), lambda i, be, sg, nx, bv, nb: (i, 0)),
        scratch_shapes=[
            pltpu.VMEM((2, D_MODEL, 2 * D_FF), f32),
            pltpu.VMEM((2, D_FF, D_MODEL), f32),
            pltpu.VMEM((D_MODEL, 2 * D_FF), bf16),
            pltpu.VMEM((D_FF, D_MODEL), bf16),
            pltpu.SemaphoreType.DMA((2, 2)),
        ],
    )
    return pl.pallas_call(
        functools.partial(_expert_kernel, ff_chunk=256, cast_rows=128),
        grid_spec=grid_spec,
        out_shape=jax.ShapeDtypeStruct((n_blocks * MOE_TILE * SLABS, LANES), f32),
        compiler_params=pltpu.CompilerParams(
            dimension_semantics=("arbitrary",), vmem_limit_bytes=VMEM_LIMIT),
        name="moe_experts",
    )(block_e, block_seg, next_e, block_valid, n_used, xs, wgu, bgu, wdn, bdn)


def _combine_kernel(pos_cur, pos_nxt, ys_hbm, h_ref, tg_ref, pp_ref, ps_ref, gple_ref, wpg_ref, wpp_ref,
                    gfin_ref, yp_ref, ys_ref, rows, sems, *, n_prompt_tiles, issue_unroll):
    i = pl.program_id(0)
    nt = pl.num_programs(0)
    n = COMBINE_TILE * TOP_K
    slot_rows = n * SLABS

    def row_copy(pos_ref, slot_base, r, slot):
        src0 = pl.multiple_of(pos_ref[0, 0, r], SLABS)
        dst0 = pl.multiple_of(slot_base + r * SLABS, SLABS)
        return pltpu.make_async_copy(ys_hbm.at[pl.ds(src0, SLABS)], rows.at[pl.ds(dst0, SLABS)],
                                     sems.at[slot])

    def wait_slot(slot):
        pltpu.make_async_copy(ys_hbm.at[pl.ds(0, slot_rows)],
                              rows.at[pl.ds(pl.multiple_of(slot * slot_rows, slot_rows), slot_rows)],
                              sems.at[slot]).wait()

    def gather(pos_ref, slot):
        slot_base = slot * slot_rows

        def start(j, carry):
            for u in range(issue_unroll):
                row_copy(pos_ref, slot_base, issue_unroll * j + u, slot).start(priority=u % 2)
            return carry
        lax.fori_loop(0, n // issue_unroll, start, 0)

    @pl.when(i == 0)
    def _():
        gather(pos_cur, 0)

    @pl.when(i + 1 < nt)
    def _():
        gather(pos_nxt, (i + 1) % 2)

    slot = i % 2
    base = pl.multiple_of(slot * slot_rows, slot_rows)
    wait_slot(slot)

    tg = tg_ref[...].T
    cols = []
    for s in range(SLABS):
        moe = jnp.zeros((COMBINE_TILE, LANES), f32)
        for k in range(TOP_K):
            moe = moe + tg[:, k:k + 1] * _load_slab_cols(
                rows, base + k * COMBINE_TILE * SLABS, COMBINE_TILE, s)
        cols.append(h_ref[:, s * LANES:(s + 1) * LANES] + moe)
    h = jnp.concatenate(cols, axis=1)
    gate = _sigmoid(_dot(_rms(h, gple_ref[...]).astype(bf16), wpg_ref[...]))
    p = _select_rows(i < n_prompt_tiles, pp_ref, ps_ref).astype(bf16)
    h = h + gate * _dot(p, wpp_ref[...])
    y = _rms(h, gfin_ref[...])

    @pl.when(i < n_prompt_tiles)
    def _():
        yp_ref[...] = y

    @pl.when(i >= n_prompt_tiles)
    def _():
        ys_ref[...] = y


def _combine(pos_tiles, ys, h1, tg, pp, ps, gple, wpg, wpp, gfin):
    t = h1.shape[0]
    nt = t // COMBINE_TILE
    npt = pp.shape[0] // COMBINE_TILE
    n = COMBINE_TILE * TOP_K
    row = lambda w: pl.BlockSpec((COMBINE_TILE, w), lambda i: (i, 0))
    full = lambda shape: pl.BlockSpec(shape, lambda i: (0,) * len(shape))
    return pl.pallas_call(
        functools.partial(_combine_kernel, n_prompt_tiles=npt, issue_unroll=16),
        grid=(nt,),
        in_specs=[
            pl.BlockSpec((1, 1, n), lambda i: (i, 0, 0), memory_space=pltpu.SMEM),
            pl.BlockSpec((1, 1, n), lambda i: (jnp.minimum(i + 1, nt - 1), 0, 0), memory_space=pltpu.SMEM),
            pl.BlockSpec(memory_space=pl.ANY),
            row(D_MODEL), pl.BlockSpec((ROUTE_ROWS, COMBINE_TILE), lambda i: (0, i)),
            pl.BlockSpec((COMBINE_TILE, PLE_DIM), lambda i: (jnp.minimum(i, npt - 1), 0)),
            pl.BlockSpec((COMBINE_TILE, PLE_DIM), lambda i: (jnp.maximum(i - npt, 0), 0)),
            full((1, D_MODEL)), full((D_MODEL, D_MODEL)), full((PLE_DIM, D_MODEL)),
            full((1, D_MODEL)),
        ],
        out_specs=[
            pl.BlockSpec((COMBINE_TILE, D_MODEL), lambda i: (jnp.minimum(i, npt - 1), 0)),
            pl.BlockSpec((COMBINE_TILE, D_MODEL), lambda i: (jnp.maximum(i - npt, 0), 0)),
        ],
        out_shape=[
            jax.ShapeDtypeStruct((pp.shape[0], D_MODEL), f32),
            jax.ShapeDtypeStruct((ps.shape[0], D_MODEL), f32),
        ],
        scratch_shapes=[
            pltpu.VMEM((2 * n * SLABS, LANES), f32),
            pltpu.SemaphoreType.DMA((2,)),
        ],
        compiler_params=pltpu.CompilerParams(
            dimension_semantics=("arbitrary",), vmem_limit_bytes=VMEM_LIMIT),
        name="moe_combine_ple",
    )(pos_tiles, pos_tiles, ys, h1, tg, pp, ps, gple, wpg, wpp, gfin)


def _pos_kernel(pstart_ref, ti_ref, rank_ref, pos_ref):
    ti = ti_ref[...]
    seg = jnp.zeros(ti.shape, i32)
    for e in range(N_EXPERTS):
        seg = jnp.where(ti == e, pstart_ref[e], seg)
    pos = (rank_ref[...] + seg) * SLABS
    for j in range(pos_ref.shape[0]):
        for k in range(TOP_K):
            pos_ref[j:j + 1, k * DISPATCH_TILE:(k + 1) * DISPATCH_TILE] = (
                pos[k:k + 1, j * DISPATCH_TILE:(j + 1) * DISPATCH_TILE])


def _routing_tables(top_i, rank, counts, n_blocks):
    padded = (counts + MOE_TILE - 1) // MOE_TILE * MOE_TILE
    pend = jnp.cumsum(padded)
    pstart = pend - padded
    block_rows = jnp.arange(n_blocks, dtype=i32) * MOE_TILE
    block_e = jnp.minimum(
        jnp.sum((block_rows[:, None] >= pend[None, :]).astype(i32), axis=1), N_EXPERTS - 1)
    experts = jnp.arange(N_EXPERTS, dtype=i32)
    of_block = lambda table: jnp.sum(jnp.where(block_e[:, None] == experts[None, :], table[None, :], 0), axis=1)
    block_valid = jnp.clip(of_block(pstart + counts) - block_rows, 0, MOE_TILE)
    n_used = pend[-1:] // MOE_TILE
    owns = padded > 0
    later = owns[None, :] & (experts[None, :] > experts[:, None])
    next_of_e = jnp.where(jnp.any(later, axis=1), jnp.argmax(later, axis=1), -1)
    block_seg = of_block(jnp.cumsum(owns.astype(i32)) - 1)
    next_e = of_block(next_of_e.astype(i32))

    nt = top_i.shape[1] // DISPATCH_TILE
    pos_tiles = pl.pallas_call(
        _pos_kernel,
        in_specs=[pl.BlockSpec(memory_space=pltpu.SMEM),
                  pl.BlockSpec(memory_space=pltpu.VMEM), pl.BlockSpec(memory_space=pltpu.VMEM)],
        out_specs=pl.BlockSpec(memory_space=pltpu.VMEM),
        out_shape=jax.ShapeDtypeStruct((nt, TOP_K * DISPATCH_TILE), i32),
        name="route_rows",
    )(pstart.astype(i32), top_i, rank).reshape(nt, 1, TOP_K * DISPATCH_TILE)
    tables = (block_e, block_seg, next_e, block_valid, n_used)
    return (pos_tiles,) + tuple(t.astype(i32) for t in tables)


def kernel(x_prompt, x_sample, p_prompt, p_sample, state_gla, state_hgrn, g_mix, w_in, w_gk2, b_gk2,
           gn_gla, lb_logits, gn_hgrn, w_out, g_ffn, w_router, b_router, w_gu, b_gu, w_dn, b_dn,
           g_ple, w_ple_gate, w_ple_proj, g_final):
    batch, seq, _ = x_prompt.shape
    dec_batch, dec_seq, _ = x_sample.shape
    tp, ts = batch * seq, dec_batch * dec_seq
    xp = x_prompt.reshape(tp, D_MODEL)
    xs = x_sample.reshape(ts, D_MODEL)

    wgk = jnp.concatenate(
        [w_gk2[0], jnp.zeros((R1_PAD - GLA_LOWRANK, GLA_KDIM), f32)], axis=0).astype(bf16)
    wr = w_router[0].T.astype(bf16)
    br = b_router[0].reshape(N_EXPERTS, 1)

    z = _in_proj(xp, xs, g_mix[0].reshape(1, D_MODEL), w_in[0])

    mixer_params = (wgk, b_gk2[0].reshape(1, GLA_KDIM), gn_gla[0].reshape(1, GLA_DV),
                    lb_logits[0:2], gn_hgrn[0].reshape(1, HGRN_DV))
    o_p, new_gla_p, new_hgrn_p = _mixer_prompt(
        z, *mixer_params, batch=batch, seq=seq, rows_per_step=512, block_rows=256)
    z_s = jnp.pad(z[tp:].reshape(dec_batch, dec_seq, Z_W), ((0, 0), (0, SAMPLE_PAD_T - dec_seq), (0, 0)))
    o_s, new_gla_s, new_hgrn_s = _mixer_sample(
        z_s.reshape(dec_batch * SAMPLE_PAD_T, Z_W), state_gla, state_hgrn, *mixer_params,
        steps=dec_seq, batch_block=8)
    o_s = o_s.reshape(dec_batch, SAMPLE_PAD_T, D_MODEL)[:, :dec_seq].reshape(ts, D_MODEL).astype(bf16)

    h1, xn2, top_i, top_g, rank, counts = _out_proj(
        xp, xs, o_p, o_s, w_out[0].astype(bf16), g_ffn[0].reshape(1, D_MODEL), wr, br)

    t = tp + ts
    n_blocks = -(-(t * TOP_K) // MOE_TILE) + N_EXPERTS
    pos_tiles, block_e, block_seg, next_e, block_valid, n_used = _routing_tables(
        top_i, rank, counts[:, 0], n_blocks)
    x_sorted = _dispatch(block_valid, pos_tiles, xn2)
    y_sorted = _experts(block_e, block_seg, next_e, block_valid, n_used, x_sorted,
                        w_gu[0], b_gu[0].reshape(N_EXPERTS, 1, 2 * D_FF),
                        w_dn[0], b_dn[0].reshape(N_EXPERTS, 1, D_MODEL))

    y_p, y_s = _combine(pos_tiles, y_sorted, h1, top_g, p_prompt[0].reshape(tp, PLE_DIM),
                        p_sample[0].reshape(ts, PLE_DIM), g_ple[0].reshape(1, D_MODEL),
                        w_ple_gate[0].astype(bf16), w_ple_proj[0].astype(bf16), g_final.reshape(1, D_MODEL))

    return (y_p.reshape(batch, seq, D_MODEL), y_s.reshape(dec_batch, dec_seq, D_MODEL),
            new_gla_p, new_hgrn_p, new_gla_s, new_hgrn_s)
```

```python
import functools

import jax
import jax.numpy as jnp
from jax import lax
from jax.experimental import pallas as pl
from jax.experimental.pallas import tpu as pltpu

f32 = jnp.float32
bf16 = jnp.bfloat16
i32 = jnp.int32

D_MODEL = 1024
GLA_HEADS = 4
GLA_DK = 64
GLA_DV = 128
GLA_KDIM = GLA_HEADS * GLA_DK
GLA_WIDTH = GLA_HEADS * GLA_DV
GLA_LOWRANK = 16
GLA_NORMALIZER = 16.0
HGRN_HEADS = 4
HGRN_DK = 128
HGRN_DV = 128
HGRN_FDIM = HGRN_HEADS * HGRN_DK
HGRN_WIDTH = HGRN_HEADS * HGRN_DV
CHUNK = 64
N_EXPERTS = 32
TOP_K = 4
D_FF = 1024
SWIGLU_LIMIT = 7.0
SWIGLU_ALPHA = 1.702
PLE_DIM = 256
EPS = 1e-6

LANES = 128
SAMPLE_PAD_T = 8
SLABS = D_MODEL // LANES
ROUTE_ROWS = 8
R1_PAD = LANES
Z_Q1 = 0
Z_K1 = Z_Q1 + GLA_KDIM
Z_V1 = Z_K1 + GLA_KDIM
Z_G1 = Z_V1 + GLA_WIDTH
Z_R1 = Z_G1 + GLA_WIDTH
Z_Q2 = Z_R1 + R1_PAD
Z_F2 = Z_Q2 + HGRN_FDIM
Z_I2 = Z_F2 + HGRN_FDIM
Z_G2 = Z_I2 + HGRN_WIDTH
Z_W = Z_G2 + HGRN_WIDTH

TOKEN_TILE = 512
MOE_TILE = 512
DISPATCH_TILE = 256
COMBINE_TILE = DISPATCH_TILE
VMEM_LIMIT = 56 * 1024 * 1024


def _rms(x, g):
    return x * lax.rsqrt(jnp.mean(x * x, axis=-1, keepdims=True) + EPS) * g


def _sigmoid(x):
    return 0.5 * jnp.tanh(0.5 * x) + 0.5


def _dot(a, b):
    return jnp.dot(a, b, preferred_element_type=f32)


def _dot_nt(a, b):
    return lax.dot_general(a, b, (((1,), (1,)), ((), ())), preferred_element_type=f32)


def _dot_tn(a, b):
    return lax.dot_general(a, b, (((0,), (0,)), ((), ())), preferred_element_type=f32)


def _cumsum_rows(tri, x):
    hi = x.astype(bf16)
    r1 = x - hi.astype(f32)
    mid = r1.astype(bf16)
    lo = (r1 - mid.astype(f32)).astype(bf16)
    return _dot(tri, hi) + _dot(tri, mid) + _dot(tri, lo)


def _select_rows(is_prompt, a_ref, b_ref):
    return jnp.where(is_prompt, a_ref[...], b_ref[...])


def _inproj_kernel(xp_ref, xs_ref, g_ref, w_hbm, z_ref, w_f32, w_bf, sem,
                   *, n_prompt_tiles, col_chunk, cast_rows):
    @pl.when(pl.program_id(0) == 0)
    def _():
        copy = pltpu.make_async_copy(w_hbm, w_f32, sem)
        copy.start()
        copy.wait()
        r1_end = Z_R1 + GLA_LOWRANK

        def cast(j, carry):
            rows = pl.ds(pl.multiple_of(j * cast_rows, cast_rows), cast_rows)
            w_bf[rows, 0:Z_R1] = w_f32[rows, 0:Z_R1].astype(bf16)
            w_bf[rows, Z_R1:Z_Q2] = jnp.concatenate(
                [w_f32[rows, Z_R1:r1_end], jnp.zeros((cast_rows, R1_PAD - GLA_LOWRANK), f32)],
                axis=1).astype(bf16)
            w_bf[rows, Z_Q2:Z_W] = w_f32[rows, r1_end:].astype(bf16)
            return carry
        lax.fori_loop(0, D_MODEL // cast_rows, cast, 0)

    is_prompt = pl.program_id(0) < n_prompt_tiles
    x = _select_rows(is_prompt, xp_ref, xs_ref)
    a = _rms(x, g_ref[...]).astype(bf16)
    for c0 in range(0, Z_W, col_chunk):
        c1 = min(c0 + col_chunk, Z_W)
        z_ref[:, c0:c1] = _dot(a, w_bf[:, c0:c1]).astype(bf16)


def _in_proj(xp, xs, g, w):
    tp, ts = xp.shape[0], xs.shape[0]
    npt = tp // TOKEN_TILE
    nt = npt + ts // TOKEN_TILE
    return pl.pallas_call(
        functools.partial(_inproj_kernel, n_prompt_tiles=npt, col_chunk=512, cast_rows=64),
        grid=(nt,),
        in_specs=[
            pl.BlockSpec((TOKEN_TILE, D_MODEL), lambda i: (jnp.minimum(i, npt - 1), 0)),
            pl.BlockSpec((TOKEN_TILE, D_MODEL), lambda i: (jnp.maximum(i - npt, 0), 0)),
            pl.BlockSpec((1, D_MODEL), lambda i: (0, 0)),
            pl.BlockSpec(memory_space=pl.ANY),
        ],
        out_specs=pl.BlockSpec((TOKEN_TILE, Z_W), lambda i: (i, 0)),
        out_shape=jax.ShapeDtypeStruct((tp + ts, Z_W), bf16),
        scratch_shapes=[
            pltpu.VMEM(w.shape, f32),
            pltpu.VMEM((D_MODEL, Z_W), bf16),
            pltpu.SemaphoreType.DMA(()),
        ],
        compiler_params=pltpu.CompilerParams(
            dimension_semantics=("arbitrary",), vmem_limit_bytes=VMEM_LIMIT),
        name="in_proj",
    )(xp, xs, g, w)


def _block_consts(r, c):
    shift = c.bit_length() - 1
    row = lax.broadcasted_iota(i32, (r, r), 0)
    col = lax.broadcasted_iota(i32, (r, r), 1)
    causal = (lax.shift_right_logical(row, shift) == lax.shift_right_logical(col, shift)) & (row >= col)
    return causal.astype(bf16), causal


def _subchunk_row(x, c, idx):
    parts = [jnp.broadcast_to(x[j * c + idx:j * c + idx + 1, :], (c, x.shape[1]))
             for j in range(x.shape[0] // c)]
    return jnp.concatenate(parts, axis=0)


def _gated_block(q, k, v, log_a, states, *, c, scale, ref_row, tri, causal, heads_per_group, scan):
    r = q.shape[0]
    nsub = r // c
    groups = q.shape[1] // LANES
    b = _cumsum_rows(tri, log_a)
    b_ref = _subchunk_row(b, c, ref_row)
    b_last = _subchunk_row(b, c, c - 1)
    qs = q * scale
    q_r = qs * jnp.exp(b - b_ref)
    k_r = (k * jnp.exp(b_ref - b)).astype(bf16)
    k_end = k * jnp.exp(b_last - b)
    q_b = qs * jnp.exp(b)
    decay = jnp.exp(b_last)
    vb = v.astype(bf16)
    lane = lax.broadcasted_iota(i32, (r, LANES), 1)
    width = LANES // heads_per_group
    outs, new_states = [], []
    for g in range(groups):
        gl = slice(g * LANES, (g + 1) * LANES)
        masks = [None] if heads_per_group == 1 else [
            (lane >= j * width) & (lane < (j + 1) * width) for j in range(heads_per_group)]
        pick = lambda x, m: x if m is None else jnp.where(m, x, 0.0)
        q_r_h = [pick(q_r[:, gl], m).astype(bf16) for m in masks]
        q_b_h = [pick(q_b[:, gl], m).astype(bf16) for m in masks]
        k_end_h = [pick(k_end[:, gl], m).astype(bf16) for m in masks]
        v_h = [vb[:, (g * heads_per_group + j) * LANES:(g * heads_per_group + j + 1) * LANES]
               for j in range(heads_per_group)]
        o_intra = []
        for j in range(heads_per_group):
            att = _dot_nt(q_r_h[j], k_r[:, gl])
            att = jnp.where(causal, att, 0.0).astype(bf16)
            o_intra.append(_dot(att, v_h[j]))
        upd = []
        for s in range(nsub):
            rs = slice(s * c, (s + 1) * c)
            u = _dot_tn(v_h[0][rs], k_end_h[0][rs])
            for j in range(1, heads_per_group):
                u = u + _dot_tn(v_h[j][rs], k_end_h[j][rs])
            upd.append(u)
        o_inter = [[] for _ in range(heads_per_group)]
        st = states[g] if scan else None
        group_states = []
        for s in range(nsub):
            rs = slice(s * c, (s + 1) * c)
            if not scan:
                st = states[s][g]
            st_b = st.astype(bf16)
            for j in range(heads_per_group):
                o_inter[j].append(_dot_nt(q_b_h[j][rs], st_b))
            st = st * decay[s * c:s * c + 1, gl] + upd[s]
            if not scan:
                group_states.append(st)
        for j in range(heads_per_group):
            outs.append(o_intra[j] + jnp.concatenate(o_inter[j], axis=0))
        new_states.append(st if scan else group_states)
    if not scan:
        new_states = [[new_states[g][s] for g in range(groups)] for s in range(nsub)]
    return outs, new_states


def _head_norm(o, g):
    return o * lax.rsqrt(jnp.mean(o * o, axis=-1, keepdims=True) + EPS) * g


def _mixer_block(zb, st_gla, st_hgrn, wgk, bgk, gng, lb, gnh, *, c, ref_row, tri, causal, valid, scan):
    z = zb.astype(f32)
    gk = _dot(zb[:, Z_R1:Z_R1 + R1_PAD], wgk) + bgk
    log_a = (jnp.minimum(gk, 0.0) - jnp.log(1.0 + jnp.exp(-jnp.abs(gk)))) / GLA_NORMALIZER
    k1 = z[:, Z_K1:Z_K1 + GLA_KDIM]
    if valid is not None:
        log_a = jnp.where(valid, log_a, 0.0)
        k1 = jnp.where(valid, k1, 0.0)
    o1, st_gla = _gated_block(
        z[:, Z_Q1:Z_Q1 + GLA_KDIM], k1, z[:, Z_V1:Z_V1 + GLA_WIDTH], log_a, st_gla,
        c=c, scale=GLA_DK ** -0.5, ref_row=ref_row, tri=tri, causal=causal,
        heads_per_group=LANES // GLA_DK, scan=scan)
    forget = lb + (1.0 - lb) * _sigmoid(z[:, Z_F2:Z_F2 + HGRN_FDIM])
    k2 = 1.0 - forget
    log_f = jnp.log(forget)
    if valid is not None:
        log_f = jnp.where(valid, log_f, 0.0)
        k2 = jnp.where(valid, k2, 0.0)
    q2 = z[:, Z_Q2:Z_Q2 + HGRN_FDIM]
    i2 = z[:, Z_I2:Z_I2 + HGRN_WIDTH]
    o2, st_hgrn = _gated_block(
        q2 * _sigmoid(q2), k2, i2 * _sigmoid(i2), log_f, st_hgrn,
        c=c, scale=1.0, ref_row=ref_row, tri=tri, causal=causal, heads_per_group=1, scan=scan)
    cols = []
    for h in range(GLA_HEADS):
        g1 = z[:, Z_G1 + h * GLA_DV:Z_G1 + (h + 1) * GLA_DV]
        cols.append(_head_norm(o1[h], gng) * (g1 * _sigmoid(g1)))
    for h in range(HGRN_HEADS):
        g2 = z[:, Z_G2 + h * HGRN_DV:Z_G2 + (h + 1) * HGRN_DV]
        cols.append(_head_norm(o2[h], gnh) * _sigmoid(g2))
    return jnp.concatenate(cols, axis=1), st_gla, st_hgrn


def _lower_bound(lbl_ref):
    l = lbl_ref[...]
    m = jnp.max(l, axis=0, keepdims=True)
    e = jnp.exp(l - m)
    return e[0:1, :] / jnp.sum(e, axis=0, keepdims=True)


GLA_GROUPS = GLA_KDIM // LANES


def _gla_state_t(s_ref_heads):
    return s_ref_heads.reshape(LANES, GLA_DV).T


def _gla_state_from_t(st):
    return st.T.reshape(LANES // GLA_DK, GLA_DK, GLA_DV)


def _mixer_prompt_kernel(z_ref, wgk_ref, bgk_ref, gng_ref, lbl_ref, gnh_ref,
                         o_ref, sg_out, sh_out, sg_t, sh_t, *, block_rows):
    j = pl.program_id(1)

    @pl.when(j == 0)
    def _():
        sg_t[...] = jnp.zeros_like(sg_t)
        sh_t[...] = jnp.zeros_like(sh_t)

    tri, causal = _block_consts(block_rows, CHUNK)
    lb = _lower_bound(lbl_ref)
    wgk, bgk, gng, gnh = wgk_ref[...], bgk_ref[...], gng_ref[...], gnh_ref[...]

    def body(blk, carry):
        r0 = pl.multiple_of(blk * block_rows, block_rows)
        o, st_g, st_h = _mixer_block(
            z_ref[pl.ds(r0, block_rows), :], [sg_t[g] for g in range(GLA_GROUPS)],
            [sh_t[h] for h in range(HGRN_HEADS)], wgk, bgk, gng, lb, gnh,
            c=CHUNK, ref_row=CHUNK // 2, tri=tri, causal=causal, valid=None, scan=True)
        o_ref[pl.ds(r0, block_rows), :] = o.astype(bf16)
        for g in range(GLA_GROUPS):
            sg_t[g] = st_g[g]
        for h in range(HGRN_HEADS):
            sh_t[h] = st_h[h]
        return carry

    lax.fori_loop(0, z_ref.shape[0] // block_rows, body, 0)

    @pl.when(j == pl.num_programs(1) - 1)
    def _():
        per = LANES // GLA_DK
        for g in range(GLA_GROUPS):
            sg_out[0, 0, g * per:(g + 1) * per] = _gla_state_from_t(sg_t[g])
        for h in range(HGRN_HEADS):
            sh_out[0, 0, h] = sh_t[h].T


def _mixer_prompt(z, wgk, bgk, gng, lbl, gnh, *, batch, seq, rows_per_step, block_rows):
    nj = seq // rows_per_step
    full = lambda shape: pl.BlockSpec(shape, lambda b, j: (0,) * len(shape))
    return pl.pallas_call(
        functools.partial(_mixer_prompt_kernel, block_rows=block_rows),
        grid=(batch, nj),
        in_specs=[
            pl.BlockSpec((rows_per_step, Z_W), lambda b, j: (b * nj + j, 0)),
            full((R1_PAD, GLA_KDIM)), full((1, GLA_KDIM)), full((1, GLA_DV)),
            full((2, HGRN_FDIM)), full((1, HGRN_DV)),
        ],
        out_specs=[
            pl.BlockSpec((rows_per_step, D_MODEL), lambda b, j: (b * nj + j, 0)),
            pl.BlockSpec((1, 1, GLA_HEADS, GLA_DK, GLA_DV), lambda b, j: (0, b, 0, 0, 0)),
            pl.BlockSpec((1, 1, HGRN_HEADS, HGRN_DK, HGRN_DV), lambda b, j: (0, b, 0, 0, 0)),
        ],
        out_shape=[
            jax.ShapeDtypeStruct((batch * seq, D_MODEL), bf16),
            jax.ShapeDtypeStruct((1, batch, GLA_HEADS, GLA_DK, GLA_DV), f32),
            jax.ShapeDtypeStruct((1, batch, HGRN_HEADS, HGRN_DK, HGRN_DV), f32),
        ],
        scratch_shapes=[
            pltpu.VMEM((GLA_GROUPS, GLA_DV, LANES), f32),
            pltpu.VMEM((HGRN_HEADS, HGRN_DV, HGRN_DK), f32),
        ],
        compiler_params=pltpu.CompilerParams(
            dimension_semantics=("arbitrary", "arbitrary"), vmem_limit_bytes=VMEM_LIMIT),
        name="mixer_prompt",
    )(z, wgk, bgk, gng, lbl, gnh)


def _mixer_sample_kernel(z_ref, sg_in, sh_in, wgk_ref, bgk_ref, gng_ref, lbl_ref, gnh_ref,
                         o_ref, sg_out, sh_out, *, batch_block, steps):
    rows = batch_block * SAMPLE_PAD_T
    tri, causal = _block_consts(rows, SAMPLE_PAD_T)
    step = lax.broadcasted_iota(i32, (rows, 1), 0) & (SAMPLE_PAD_T - 1)
    valid = step < steps
    lb = _lower_bound(lbl_ref)
    per = LANES // GLA_DK
    st_g = [[_gla_state_t(sg_in[0, s, g * per:(g + 1) * per]) for g in range(GLA_GROUPS)]
            for s in range(batch_block)]
    st_h = [[sh_in[0, s, h].T for h in range(HGRN_HEADS)] for s in range(batch_block)]
    o, st_g, st_h = _mixer_block(
        z_ref[...], st_g, st_h, wgk_ref[...], bgk_ref[...], gng_ref[...], lb, gnh_ref[...],
        c=SAMPLE_PAD_T, ref_row=steps // 2, tri=tri, causal=causal, valid=valid, scan=False)
    o_ref[...] = o
    for s in range(batch_block):
        for g in range(GLA_GROUPS):
            sg_out[0, s, g * per:(g + 1) * per] = _gla_state_from_t(st_g[s][g])
        for h in range(HGRN_HEADS):
            sh_out[0, s, h] = st_h[s][h].T


def _mixer_sample(z2, state_gla, state_hgrn, wgk, bgk, gng, lbl, gnh, *, steps, batch_block):
    batch = z2.shape[0] // SAMPLE_PAD_T
    rows = batch_block * SAMPLE_PAD_T
    full = lambda shape: pl.BlockSpec(shape, lambda i: (0,) * len(shape))
    sg_spec = pl.BlockSpec((1, batch_block, GLA_HEADS, GLA_DK, GLA_DV), lambda i: (0, i, 0, 0, 0))
    sh_spec = pl.BlockSpec((1, batch_block, HGRN_HEADS, HGRN_DK, HGRN_DV), lambda i: (0, i, 0, 0, 0))
    return pl.pallas_call(
        functools.partial(_mixer_sample_kernel, batch_block=batch_block, steps=steps),
        grid=(batch // batch_block,),
        in_specs=[
            pl.BlockSpec((rows, Z_W), lambda i: (i, 0)),
            sg_spec, sh_spec,
            full((R1_PAD, GLA_KDIM)), full((1, GLA_KDIM)), full((1, GLA_DV)),
            full((2, HGRN_FDIM)), full((1, HGRN_DV)),
        ],
        out_specs=[pl.BlockSpec((rows, D_MODEL), lambda i: (i, 0)), sg_spec, sh_spec],
        out_shape=[
            jax.ShapeDtypeStruct((batch * SAMPLE_PAD_T, D_MODEL), f32),
            jax.ShapeDtypeStruct(state_gla.shape, f32),
            jax.ShapeDtypeStruct(state_hgrn.shape, f32),
        ],
        compiler_params=pltpu.CompilerParams(
            dimension_semantics=("arbitrary",), vmem_limit_bytes=VMEM_LIMIT),
        name="mixer_sample",
    )(z2, state_gla, state_hgrn, wgk, bgk, gng, lbl, gnh)


def _store_slabs(ref, x):
    for s in range(SLABS):
        ref[pl.ds(s, x.shape[0], stride=SLABS), :] = x[:, s * LANES:(s + 1) * LANES]


def _load_slab_cols(ref, start, rows, s):
    return ref[pl.ds(start + s, rows, stride=SLABS), :]


def _outproj_kernel(xp_ref, xs_ref, op_ref, os_ref, wo_ref, g_ref, wr_ref, br_ref,
                    h_ref, xn_ref, ti_ref, tg_ref, rank_ref, cnt_ref, run_cnt, *, n_prompt_tiles):
    i = pl.program_id(0)

    @pl.when(i == 0)
    def _():
        run_cnt[...] = jnp.zeros_like(run_cnt)

    x = _select_rows(i < n_prompt_tiles, xp_ref, xs_ref)
    o = _select_rows(i < n_prompt_tiles, op_ref, os_ref)
    h = x + _dot(o, wo_ref[...])
    h_ref[...] = h
    xn = _rms(h, g_ref[...])
    _store_slabs(xn_ref, xn)
    l = _dot_nt(wr_ref[...], xn.astype(bf16)) + br_ref[...]
    expert = lax.broadcasted_iota(i32, l.shape, 0)
    neg = jnp.float32(-jnp.inf)
    vals, idxs = [], []
    for _ in range(TOP_K):
        m = jnp.max(l, axis=0, keepdims=True)
        idx = jnp.min(jnp.where(l == m, expert, N_EXPERTS), axis=0, keepdims=True)
        vals.append(m)
        idxs.append(idx)
        l = jnp.where(expert == idx, neg, l)
    es = [jnp.exp(v - vals[0]) for v in vals]
    denom = es[0] + es[1] + es[2] + es[3]

    onehot = jnp.zeros(l.shape, f32)
    for k in range(TOP_K):
        onehot = onehot + (expert == idxs[k]).astype(f32)
    n = l.shape[1]
    earlier = (lax.broadcasted_iota(i32, (n, n), 0) < lax.broadcasted_iota(i32, (n, n), 1)).astype(bf16)
    ahead = _dot(onehot.astype(bf16), earlier) + run_cnt[:, 0:1]
    run_cnt[...] = run_cnt[...] + jnp.sum(onehot, axis=1, keepdims=True)
    cnt_ref[...] = run_cnt[...].astype(i32)

    ranks = [jnp.sum(jnp.where(expert == idxs[k], ahead, 0.0), axis=0, keepdims=True).astype(i32)
             for k in range(TOP_K)]
    pad_i = jnp.zeros((ROUTE_ROWS - TOP_K, n), i32)
    ti_ref[...] = jnp.concatenate(idxs + [pad_i], axis=0)
    rank_ref[...] = jnp.concatenate(ranks + [pad_i], axis=0)
    tg_ref[...] = jnp.concatenate([e / denom for e in es] + [pad_i.astype(f32)], axis=0)


def _out_proj(xp, xs, o_p, o_s, wo, g, wr, br):
    tp, ts = xp.shape[0], xs.shape[0]
    npt = tp // TOKEN_TILE
    nt = npt + ts // TOKEN_TILE
    t = tp + ts
    row = lambda w: pl.BlockSpec((TOKEN_TILE, w), lambda i: (i, 0))
    route = pl.BlockSpec((ROUTE_ROWS, TOKEN_TILE), lambda i: (0, i))
    full = lambda shape: pl.BlockSpec(shape, lambda i: (0,) * len(shape))
    prompt_rows = pl.BlockSpec((TOKEN_TILE, D_MODEL), lambda i: (jnp.minimum(i, npt - 1), 0))
    sample_rows = pl.BlockSpec((TOKEN_TILE, D_MODEL), lambda i: (jnp.maximum(i - npt, 0), 0))
    return pl.pallas_call(
        functools.partial(_outproj_kernel, n_prompt_tiles=npt),
        grid=(nt,),
        in_specs=[
            prompt_rows, sample_rows, prompt_rows, sample_rows,
            full((D_MODEL, D_MODEL)), full((1, D_MODEL)),
            full((N_EXPERTS, D_MODEL)), full((N_EXPERTS, 1)),
        ],
        out_specs=[row(D_MODEL), pl.BlockSpec((TOKEN_TILE * SLABS, LANES), lambda i: (i, 0)),
                   route, route, route, full((N_EXPERTS, LANES))],
        out_shape=[
            jax.ShapeDtypeStruct((t, D_MODEL), f32),
            jax.ShapeDtypeStruct((t * SLABS, LANES), f32),
            jax.ShapeDtypeStruct((ROUTE_ROWS, t), i32),
            jax.ShapeDtypeStruct((ROUTE_ROWS, t), f32),
            jax.ShapeDtypeStruct((ROUTE_ROWS, t), i32),
            jax.ShapeDtypeStruct((N_EXPERTS, LANES), i32),
        ],
        scratch_shapes=[pltpu.VMEM((N_EXPERTS, LANES), f32)],
        compiler_params=pltpu.CompilerParams(
            dimension_semantics=("arbitrary",), vmem_limit_bytes=VMEM_LIMIT),
        name="out_proj_router",
    )(xp, xs, o_p, o_s, wo, g, wr, br)


def _dispatch_kernel(bv_ref, nz_ref, pos_ref, xn_hbm, xs_hbm, zeros, stage, in_sems, out_sems, zsem,
                     *, n_blocks):
    @pl.when(pl.program_id(0) == 0)
    def _():
        zeros[...] = jnp.zeros_like(zeros)

        def fill(b, carry):
            @pl.when(bv_ref[b] < MOE_TILE)
            def _():
                r0 = pl.multiple_of(b * (MOE_TILE * SLABS), MOE_TILE * SLABS)
                pltpu.make_async_copy(zeros, xs_hbm.at[pl.ds(r0, MOE_TILE * SLABS)], zsem).start()
            return carry

        def drain(b, carry):
            pltpu.make_async_copy(zeros, xs_hbm.at[pl.ds(0, MOE_TILE * SLABS)], zsem).wait()
            return carry

        lax.fori_loop(0, n_blocks, fill, 0)
        lax.fori_loop(0, nz_ref[0], drain, 0)

    i = pl.program_id(0)
    nt = pl.num_programs(0)
    tile_rows = DISPATCH_TILE * SLABS

    def fetch(t, slot):
        return pltpu.make_async_copy(
            xn_hbm.at[pl.ds(pl.multiple_of(t * tile_rows, tile_rows), tile_rows)],
            stage.at[pl.ds(pl.multiple_of(slot * tile_rows, tile_rows), tile_rows)], in_sems.at[slot])

    def wait_rows(slot):
        for _ in range(TOP_K):
            pltpu.make_async_copy(stage.at[pl.ds(0, tile_rows)], xs_hbm.at[pl.ds(0, tile_rows)],
                                  out_sems.at[slot]).wait()

    @pl.when(i == 0)
    def _():
        fetch(0, 0).start()

    @pl.when(i + 1 < nt)
    def _():
        fetch(i + 1, (i + 1) % 3).start()

    fetch(i, i % 3).wait()
    base = (i % 3) * tile_rows

    def start(r, carry):
        src = stage.at[pl.ds(pl.multiple_of(base + r * SLABS, SLABS), SLABS)]
        for k in range(TOP_K):
            dst0 = pl.multiple_of(pos_ref[0, 0, k * DISPATCH_TILE + r], SLABS)
            pltpu.make_async_copy(src, xs_hbm.at[pl.ds(dst0, SLABS)], out_sems.at[i % 2]).start(priority=k % 2)
        return carry

    lax.fori_loop(0, DISPATCH_TILE, start, 0, unroll=4)

    @pl.when(i > 0)
    def _():
        wait_rows((i + 1) % 2)

    @pl.when(i == nt - 1)
    def _():
        wait_rows(i % 2)


def _dispatch(block_valid, pos_tiles, xn2):
    t = xn2.shape[0] // SLABS
    nt = t // DISPATCH_TILE
    n_blocks = block_valid.shape[0]
    n_zero = jnp.sum((block_valid < MOE_TILE).astype(i32)).reshape(1)
    grid_spec = pltpu.PrefetchScalarGridSpec(
        num_scalar_prefetch=2,
        grid=(nt,),
        in_specs=[
            pl.BlockSpec((1, 1, DISPATCH_TILE * TOP_K), lambda i, bv, nz: (i, 0, 0),
                         memory_space=pltpu.SMEM),
            pl.BlockSpec(memory_space=pl.ANY),
        ],
        out_specs=pl.BlockSpec(memory_space=pl.ANY),
        scratch_shapes=[
            pltpu.VMEM((MOE_TILE * SLABS, LANES), f32),
            pltpu.VMEM((3 * DISPATCH_TILE * SLABS, LANES), f32),
            pltpu.SemaphoreType.DMA((3,)),
            pltpu.SemaphoreType.DMA((2,)),
            pltpu.SemaphoreType.DMA(()),
        ],
    )
    return pl.pallas_call(
        functools.partial(_dispatch_kernel, n_blocks=n_blocks),
        grid_spec=grid_spec,
        out_shape=jax.ShapeDtypeStruct((n_blocks * MOE_TILE * SLABS, LANES), f32),
        compiler_params=pltpu.CompilerParams(
            dimension_semantics=("arbitrary",), vmem_limit_bytes=VMEM_LIMIT),
        name="moe_dispatch",
    )(block_valid, n_zero, pos_tiles, xn2)


def _expert_kernel(be_ref, seg_ref, nxt_ref, bv_ref, nb_ref, x_ref, wgu_hbm, bgu_ref, wdn_hbm, bdn_ref, y_ref,
                   wgu_f32, wdn_f32, wgu_bf, wdn_bf, sems, *, ff_chunk, cast_rows):
    i = pl.program_id(0)

    @pl.when(i >= nb_ref[0])
    def _():
        y_ref[...] = jnp.zeros_like(y_ref)

    def fetch(e, slot):
        return (pltpu.make_async_copy(wgu_hbm.at[e], wgu_f32.at[slot], sems.at[slot, 0]),
                pltpu.make_async_copy(wdn_hbm.at[e], wdn_f32.at[slot], sems.at[slot, 1]))

    @pl.when((i < nb_ref[0]) & ((i == 0) | (be_ref[i] != be_ref[jnp.maximum(i - 1, 0)])))
    def _():
        slot = seg_ref[i] % 2

        @pl.when(i == 0)
        def _():
            for c in fetch(be_ref[0], 0):
                c.start()

        for c in fetch(be_ref[i], slot):
            c.wait()

        @pl.when(nxt_ref[i] >= 0)
        def _():
            for c in fetch(nxt_ref[i], 1 - slot):
                c.start()

        def cast(j, carry):
            r0 = pl.multiple_of(j * cast_rows, cast_rows)
            wgu_bf[pl.ds(r0, cast_rows), :] = wgu_f32[slot, pl.ds(r0, cast_rows), :].astype(bf16)
            wdn_bf[pl.ds(r0, cast_rows), :] = wdn_f32[slot, pl.ds(r0, cast_rows), :].astype(bf16)
            return carry
        lax.fori_loop(0, D_MODEL // cast_rows, cast, 0)

    def mlp(rows):
        xb = jnp.concatenate(
            [_load_slab_cols(x_ref, 0, rows, s).astype(bf16) for s in range(SLABS)], axis=1)
        hid = []
        for c0 in range(0, D_FF, ff_chunk):
            gate = _dot(xb, wgu_bf[:, c0:c0 + ff_chunk]) + bgu_ref[0, :, c0:c0 + ff_chunk]
            up = (_dot(xb, wgu_bf[:, D_FF + c0:D_FF + c0 + ff_chunk])
                  + bgu_ref[0, :, D_FF + c0:D_FF + c0 + ff_chunk])
            gate = jnp.minimum(gate, SWIGLU_LIMIT)
            up = jnp.clip(up, -SWIGLU_LIMIT, SWIGLU_LIMIT)
            hid.append(((up + 1.0) * (gate * _sigmoid(gate * SWIGLU_ALPHA))).astype(bf16))
        hid = jnp.concatenate(hid, axis=1)
        for n0 in range(0, D_MODEL, ff_chunk):
            y = _dot(hid, wdn_bf[:, n0:n0 + ff_chunk]) + bdn_ref[0, :, n0:n0 + ff_chunk]
            for s in range(n0 // LANES, (n0 + ff_chunk) // LANES):
                y_ref[pl.ds(s, rows, stride=SLABS), :] = y[:, s * LANES - n0:(s + 1) * LANES - n0]

    half = MOE_TILE // 2

    @pl.when((i < nb_ref[0]) & (bv_ref[i] > half))
    def _():
        mlp(MOE_TILE)

    @pl.when((i < nb_ref[0]) & (bv_ref[i] <= half))
    def _():
        mlp(half)
        y_ref[pl.ds(half * SLABS, half * SLABS), :] = jnp.zeros((half * SLABS, LANES), f32)


def _experts(block_e, block_seg, next_e, block_valid, n_used, xs, wgu, bgu, wdn, bdn):
    n_blocks = block_e.shape[0]
    grid_spec = pltpu.PrefetchScalarGridSpec(
        num_scalar_prefetch=5,
        grid=(n_blocks,),
        in_specs=[
            pl.BlockSpec((MOE_TILE * SLABS, LANES),
                         lambda i, be, sg, nx, bv, nb: (jnp.minimum(i, nb[0] - 1), 0)),
            pl.BlockSpec(memory_space=pl.ANY),
            pl.BlockSpec((1, 1, 2 * D_FF), lambda i, be, sg, nx, bv, nb: (be[i], 0, 0)),
            pl.BlockSpec(memory_space=pl.ANY),
            pl.BlockSpec((1, 1, D_MODEL), lambda i, be, sg, nx, bv, nb: (be[i], 0, 0)),
        ],
        out_specs=pl.BlockSpec((MOE_TILE * SLABS, LANES), lambda i, be, sg, nx, bv, nb: (i, 0)),
        scratch_shapes=[
            pltpu.VMEM((2, D_MODEL, 2 * D_FF), f32),
            pltpu.VMEM((2, D_FF, D_MODEL), f32),
            pltpu.VMEM((D_MODEL, 2 * D_FF), bf16),
            pltpu.VMEM((D_FF, D_MODEL), bf16),
            pltpu.SemaphoreType.DMA((2, 2)),
        ],
    )
    return pl.pallas_call(
        functools.partial(_expert_kernel, ff_chunk=256, cast_rows=128),
        grid_spec=grid_spec,
        out_shape=jax.ShapeDtypeStruct((n_blocks * MOE_TILE * SLABS, LANES), f32),
        compiler_params=pltpu.CompilerParams(
            dimension_semantics=("arbitrary",), vmem_limit_bytes=VMEM_LIMIT),
        name="moe_experts",
    )(block_e, block_seg, next_e, block_valid, n_used, xs, wgu, bgu, wdn, bdn)


def _combine_kernel(pos_cur, pos_nxt, ys_hbm, h_ref, tg_ref, pp_ref, ps_ref, gple_ref, wpg_ref, wpp_ref,
                    gfin_ref, yp_ref, ys_ref, rows, sems, *, n_prompt_tiles, issue_unroll):
    i = pl.program_id(0)
    nt = pl.num_programs(0)
    n = COMBINE_TILE * TOP_K
    slot_rows = n * SLABS

    def row_copy(pos_ref, slot_base, r, slot):
        src0 = pl.multiple_of(pos_ref[0, 0, r], SLABS)
        dst0 = pl.multiple_of(slot_base + r * SLABS, SLABS)
        return pltpu.make_async_copy(ys_hbm.at[pl.ds(src0, SLABS)], rows.at[pl.ds(dst0, SLABS)],
                                     sems.at[slot])

    def wait_slot(slot):
        pltpu.make_async_copy(ys_hbm.at[pl.ds(0, slot_rows)],
                              rows.at[pl.ds(pl.multiple_of(slot * slot_rows, slot_rows), slot_rows)],
                              sems.at[slot]).wait()

    def gather(pos_ref, slot):
        slot_base = slot * slot_rows

        def start(j, carry):
            for u in range(issue_unroll):
                row_copy(pos_ref, slot_base, issue_unroll * j + u, slot).start(priority=u % 2)
            return carry
        lax.fori_loop(0, n // issue_unroll, start, 0)

    @pl.when(i == 0)
    def _():
        gather(pos_cur, 0)

    @pl.when(i + 1 < nt)
    def _():
        gather(pos_nxt, (i + 1) % 2)

    slot = i % 2
    base = pl.multiple_of(slot * slot_rows, slot_rows)
    wait_slot(slot)

    tg = tg_ref[...]
    gate_k = [jnp.broadcast_to(tg[:, k:k + 1], (COMBINE_TILE, LANES)) for k in range(TOP_K)]
    cols = []
    for s in range(SLABS):
        moe = jnp.zeros((COMBINE_TILE, LANES), f32)
        for k in range(TOP_K):
            moe = moe + gate_k[k] * _load_slab_cols(
                rows, base + k * COMBINE_TILE * SLABS, COMBINE_TILE, s)
        cols.append(h_ref[:, s * LANES:(s + 1) * LANES] + moe)
    h = jnp.concatenate(cols, axis=1)
    gate = _sigmoid(_dot(_rms(h, gple_ref[...]).astype(bf16), wpg_ref[...]))
    p = _select_rows(i < n_prompt_tiles, pp_ref, ps_ref).astype(bf16)
    h = h + gate * _dot(p, wpp_ref[...])
    y = _rms(h, gfin_ref[...])

    @pl.when(i < n_prompt_tiles)
    def _():
        yp_ref[...] = y

    @pl.when(i >= n_prompt_tiles)
    def _():
        ys_ref[...] = y


def _combine(pos_tiles, ys, h1, tg, pp, ps, gple, wpg, wpp, gfin):
    t = h1.shape[0]
    nt = t // COMBINE_TILE
    npt = pp.shape[0] // COMBINE_TILE
    n = COMBINE_TILE * TOP_K
    row = lambda w: pl.BlockSpec((COMBINE_TILE, w), lambda i: (i, 0))
    full = lambda shape: pl.BlockSpec(shape, lambda i: (0,) * len(shape))
    return pl.pallas_call(
        functools.partial(_combine_kernel, n_prompt_tiles=npt, issue_unroll=16),
        grid=(nt,),
        in_specs=[
            pl.BlockSpec((1, 1, n), lambda i: (i, 0, 0), memory_space=pltpu.SMEM),
            pl.BlockSpec((1, 1, n), lambda i: (jnp.minimum(i + 1, nt - 1), 0, 0), memory_space=pltpu.SMEM),
            pl.BlockSpec(memory_space=pl.ANY),
            row(D_MODEL), row(LANES),
            pl.BlockSpec((COMBINE_TILE, PLE_DIM), lambda i: (jnp.minimum(i, npt - 1), 0)),
            pl.BlockSpec((COMBINE_TILE, PLE_DIM), lambda i: (jnp.maximum(i - npt, 0), 0)),
            full((1, D_MODEL)), full((D_MODEL, D_MODEL)), full((PLE_DIM, D_MODEL)),
            full((1, D_MODEL)),
        ],
        out_specs=[
            pl.BlockSpec((COMBINE_TILE, D_MODEL), lambda i: (jnp.minimum(i, npt - 1), 0)),
            pl.BlockSpec((COMBINE_TILE, D_MODEL), lambda i: (jnp.maximum(i - npt, 0), 0)),
        ],
        out_shape=[
            jax.ShapeDtypeStruct((pp.shape[0], D_MODEL), f32),
            jax.ShapeDtypeStruct((ps.shape[0], D_MODEL), f32),
        ],
        scratch_shapes=[
            pltpu.VMEM((2 * n * SLABS, LANES), f32),
            pltpu.SemaphoreType.DMA((2,)),
        ],
        compiler_params=pltpu.CompilerParams(
            dimension_semantics=("arbitrary",), vmem_limit_bytes=VMEM_LIMIT),
        name="moe_combine_ple",
    )(pos_tiles, pos_tiles, ys, h1, tg, pp, ps, gple, wpg, wpp, gfin)


def _pos_kernel(pstart_ref, ti_ref, rank_ref, pos_ref):
    ti = ti_ref[...]
    seg = jnp.zeros(ti.shape, i32)
    for e in range(N_EXPERTS):
        seg = jnp.where(ti == e, pstart_ref[e], seg)
    pos = (rank_ref[...] + seg) * SLABS
    for j in range(pos_ref.shape[0]):
        for k in range(TOP_K):
            pos_ref[j:j + 1, k * DISPATCH_TILE:(k + 1) * DISPATCH_TILE] = (
                pos[k:k + 1, j * DISPATCH_TILE:(j + 1) * DISPATCH_TILE])


def _routing_tables(top_i, rank, counts, n_blocks):
    padded = (counts + MOE_TILE - 1) // MOE_TILE * MOE_TILE
    pend = jnp.cumsum(padded)
    pstart = pend - padded
    block_rows = jnp.arange(n_blocks, dtype=i32) * MOE_TILE
    block_e = jnp.minimum(
        jnp.sum((block_rows[:, None] >= pend[None, :]).astype(i32), axis=1), N_EXPERTS - 1)
    experts = jnp.arange(N_EXPERTS, dtype=i32)
    of_block = lambda table: jnp.sum(jnp.where(block_e[:, None] == experts[None, :], table[None, :], 0), axis=1)
    block_valid = jnp.clip(of_block(pstart + counts) - block_rows, 0, MOE_TILE)
    n_used = pend[-1:] // MOE_TILE
    owns = padded > 0
    later = owns[None, :] & (experts[None, :] > experts[:, None])
    next_of_e = jnp.where(jnp.any(later, axis=1), jnp.argmax(later, axis=1), -1)
    block_seg = of_block(jnp.cumsum(owns.astype(i32)) - 1)
    next_e = of_block(next_of_e.astype(i32))

    nt = top_i.shape[1] // DISPATCH_TILE
    pos_tiles = pl.pallas_call(
        _pos_kernel,
        in_specs=[pl.BlockSpec(memory_space=pltpu.SMEM),
                  pl.BlockSpec(memory_space=pltpu.VMEM), pl.BlockSpec(memory_space=pltpu.VMEM)],
        out_specs=pl.BlockSpec(memory_space=pltpu.VMEM),
        out_shape=jax.ShapeDtypeStruct((nt, TOP_K * DISPATCH_TILE), i32),
        name="route_rows",
    )(pstart.astype(i32), top_i, rank).reshape(nt, 1, TOP_K * DISPATCH_TILE)
    tables = (block_e, block_seg, next_e, block_valid, n_used)
    return (pos_tiles,) + tuple(t.astype(i32) for t in tables)


def kernel(x_prompt, x_sample, p_prompt, p_sample, state_gla, state_hgrn, g_mix, w_in, w_gk2, b_gk2,
           gn_gla, lb_logits, gn_hgrn, w_out, g_ffn, w_router, b_router, w_gu, b_gu, w_dn, b_dn,
           g_ple, w_ple_gate, w_ple_proj, g_final):
    batch, seq, _ = x_prompt.shape
    dec_batch, dec_seq, _ = x_sample.shape
    tp, ts = batch * seq, dec_batch * dec_seq
    xp = x_prompt.reshape(tp, D_MODEL)
    xs = x_sample.reshape(ts, D_MODEL)

    wgk = jnp.concatenate(
        [w_gk2[0], jnp.zeros((R1_PAD - GLA_LOWRANK, GLA_KDIM), f32)], axis=0).astype(bf16)
    wr = w_router[0].T.astype(bf16)
    br = b_router[0].reshape(N_EXPERTS, 1)

    z = _in_proj(xp, xs, g_mix[0].reshape(1, D_MODEL), w_in[0])

    mixer_params = (wgk, b_gk2[0].reshape(1, GLA_KDIM), gn_gla[0].reshape(1, GLA_DV),
                    lb_logits[0:2], gn_hgrn[0].reshape(1, HGRN_DV))
    o_p, new_gla_p, new_hgrn_p = _mixer_prompt(
        z, *mixer_params, batch=batch, seq=seq, rows_per_step=512, block_rows=256)
    z_s = jnp.pad(z[tp:].reshape(dec_batch, dec_seq, Z_W), ((0, 0), (0, SAMPLE_PAD_T - dec_seq), (0, 0)))
    o_s, new_gla_s, new_hgrn_s = _mixer_sample(
        z_s.reshape(dec_batch * SAMPLE_PAD_T, Z_W), state_gla, state_hgrn, *mixer_params,
        steps=dec_seq, batch_block=8)
    o_s = o_s.reshape(dec_batch, SAMPLE_PAD_T, D_MODEL)[:, :dec_seq].reshape(ts, D_MODEL).astype(bf16)

    h1, xn2, top_i, top_g, rank, counts = _out_proj(
        xp, xs, o_p, o_s, w_out[0].astype(bf16), g_ffn[0].reshape(1, D_MODEL), wr, br)

    t = tp + ts
    n_blocks = -(-(t * TOP_K) // MOE_TILE) + N_EXPERTS
    pos_tiles, block_e, block_seg, next_e, block_valid, n_used = _routing_tables(
        top_i, rank, counts[:, 0], n_blocks)
    x_sorted = _dispatch(block_valid, pos_tiles, xn2)
    y_sorted = _experts(block_e, block_seg, next_e, block_valid, n_used, x_sorted,
                        w_gu[0], b_gu[0].reshape(N_EXPERTS, 1, 2 * D_FF),
                        w_dn[0], b_dn[0].reshape(N_EXPERTS, 1, D_MODEL))

    gates = jnp.pad(top_g.T, ((0, 0), (0, LANES - ROUTE_ROWS)))
    y_p, y_s = _combine(pos_tiles, y_sorted, h1, gates, p_prompt[0].reshape(tp, PLE_DIM),
                        p_sample[0].reshape(ts, PLE_DIM), g_ple[0].reshape(1, D_MODEL),
                        w_ple_gate[0].astype(bf16), w_ple_proj[0].astype(bf16), g_final.reshape(1, D_MODEL))

    return (y_p.reshape(batch, seq, D_MODEL), y_s.reshape(dec_batch, dec_seq, D_MODEL),
            new_gla_p, new_hgrn_p, new_gla_s, new_hgrn_s)
```

```python
import functools

import jax
import jax.numpy as jnp
from jax import lax
from jax.experimental import pallas as pl
from jax.experimental.pallas import tpu as pltpu

f32 = jnp.float32
bf16 = jnp.bfloat16
i32 = jnp.int32

D_MODEL = 1024
GLA_HEADS = 4
GLA_DK = 64
GLA_DV = 128
GLA_KDIM = GLA_HEADS * GLA_DK
GLA_WIDTH = GLA_HEADS * GLA_DV
GLA_LOWRANK = 16
GLA_NORMALIZER = 16.0
HGRN_HEADS = 4
HGRN_DK = 128
HGRN_DV = 128
HGRN_FDIM = HGRN_HEADS * HGRN_DK
HGRN_WIDTH = HGRN_HEADS * HGRN_DV
CHUNK = 64
N_EXPERTS = 32
TOP_K = 4
D_FF = 1024
SWIGLU_LIMIT = 7.0
SWIGLU_ALPHA = 1.702
PLE_DIM = 256
EPS = 1e-6

LANES = 128
SAMPLE_PAD_T = 8
SLABS = D_MODEL // LANES
ROUTE_ROWS = 8
R1_PAD = LANES
Z_Q1 = 0
Z_K1 = Z_Q1 + GLA_KDIM
Z_V1 = Z_K1 + GLA_KDIM
Z_G1 = Z_V1 + GLA_WIDTH
Z_R1 = Z_G1 + GLA_WIDTH
Z_Q2 = Z_R1 + R1_PAD
Z_F2 = Z_Q2 + HGRN_FDIM
Z_I2 = Z_F2 + HGRN_FDIM
Z_G2 = Z_I2 + HGRN_WIDTH
Z_W = Z_G2 + HGRN_WIDTH

TOKEN_TILE = 512
MOE_TILE = 512
DISPATCH_TILE = 512
COMBINE_TILE = DISPATCH_TILE
VMEM_LIMIT = 56 * 1024 * 1024


def _rms(x, g):
    return x * lax.rsqrt(jnp.mean(x * x, axis=-1, keepdims=True) + EPS) * g


def _sigmoid(x):
    return 0.5 * jnp.tanh(0.5 * x) + 0.5


def _dot(a, b):
    return jnp.dot(a, b, preferred_element_type=f32)


def _dot_nt(a, b):
    return lax.dot_general(a, b, (((1,), (1,)), ((), ())), preferred_element_type=f32)


def _dot_tn(a, b):
    return lax.dot_general(a, b, (((0,), (0,)), ((), ())), preferred_element_type=f32)


def _cumsum_rows(tri, x):
    hi = x.astype(bf16)
    r1 = x - hi.astype(f32)
    mid = r1.astype(bf16)
    lo = (r1 - mid.astype(f32)).astype(bf16)
    return _dot(tri, hi) + _dot(tri, mid) + _dot(tri, lo)


def _select_rows(is_prompt, a_ref, b_ref):
    return jnp.where(is_prompt, a_ref[...], b_ref[...])


def _inproj_kernel(xp_ref, xs_ref, g_ref, w_hbm, z_ref, w_f32, w_bf, sem,
                   *, n_prompt_tiles, col_chunk, cast_rows):
    @pl.when(pl.program_id(0) == 0)
    def _():
        copy = pltpu.make_async_copy(w_hbm, w_f32, sem)
        copy.start()
        copy.wait()
        r1_end = Z_R1 + GLA_LOWRANK

        def cast(j, carry):
            rows = pl.ds(pl.multiple_of(j * cast_rows, cast_rows), cast_rows)
            w_bf[rows, 0:Z_R1] = w_f32[rows, 0:Z_R1].astype(bf16)
            w_bf[rows, Z_R1:Z_Q2] = jnp.concatenate(
                [w_f32[rows, Z_R1:r1_end], jnp.zeros((cast_rows, R1_PAD - GLA_LOWRANK), f32)],
                axis=1).astype(bf16)
            w_bf[rows, Z_Q2:Z_W] = w_f32[rows, r1_end:].astype(bf16)
            return carry
        lax.fori_loop(0, D_MODEL // cast_rows, cast, 0)

    is_prompt = pl.program_id(0) < n_prompt_tiles
    x = _select_rows(is_prompt, xp_ref, xs_ref)
    a = _rms(x, g_ref[...]).astype(bf16)
    for c0 in range(0, Z_W, col_chunk):
        c1 = min(c0 + col_chunk, Z_W)
        z_ref[:, c0:c1] = _dot(a, w_bf[:, c0:c1]).astype(bf16)


def _in_proj(xp, xs, g, w):
    tp, ts = xp.shape[0], xs.shape[0]
    npt = tp // TOKEN_TILE
    nt = npt + ts // TOKEN_TILE
    return pl.pallas_call(
        functools.partial(_inproj_kernel, n_prompt_tiles=npt, col_chunk=512, cast_rows=64),
        grid=(nt,),
        in_specs=[
            pl.BlockSpec((TOKEN_TILE, D_MODEL), lambda i: (jnp.minimum(i, npt - 1), 0)),
            pl.BlockSpec((TOKEN_TILE, D_MODEL), lambda i: (jnp.maximum(i - npt, 0), 0)),
            pl.BlockSpec((1, D_MODEL), lambda i: (0, 0)),
            pl.BlockSpec(memory_space=pl.ANY),
        ],
        out_specs=pl.BlockSpec((TOKEN_TILE, Z_W), lambda i: (i, 0)),
        out_shape=jax.ShapeDtypeStruct((tp + ts, Z_W), bf16),
        scratch_shapes=[
            pltpu.VMEM(w.shape, f32),
            pltpu.VMEM((D_MODEL, Z_W), bf16),
            pltpu.SemaphoreType.DMA(()),
        ],
        compiler_params=pltpu.CompilerParams(
            dimension_semantics=("arbitrary",), vmem_limit_bytes=VMEM_LIMIT),
        name="in_proj",
    )(xp, xs, g, w)


def _block_consts(r, c):
    shift = c.bit_length() - 1
    row = lax.broadcasted_iota(i32, (r, r), 0)
    col = lax.broadcasted_iota(i32, (r, r), 1)
    causal = (lax.shift_right_logical(row, shift) == lax.shift_right_logical(col, shift)) & (row >= col)
    return causal.astype(bf16), causal


def _subchunk_row(x, c, idx):
    parts = [jnp.broadcast_to(x[j * c + idx:j * c + idx + 1, :], (c, x.shape[1]))
             for j in range(x.shape[0] // c)]
    return jnp.concatenate(parts, axis=0)


def _gated_block(q, k, v, log_a, states, *, c, scale, ref_row, tri, causal, heads_per_group, scan):
    r = q.shape[0]
    nsub = r // c
    groups = q.shape[1] // LANES
    b = _cumsum_rows(tri, log_a)
    b_ref = _subchunk_row(b, c, ref_row)
    b_last = _subchunk_row(b, c, c - 1)
    qs = q * scale
    q_r = qs * jnp.exp(b - b_ref)
    k_r = (k * jnp.exp(b_ref - b)).astype(bf16)
    k_end = k * jnp.exp(b_last - b)
    q_b = qs * jnp.exp(b)
    decay = jnp.exp(b_last)
    vb = v.astype(bf16)
    lane = lax.broadcasted_iota(i32, (r, LANES), 1)
    width = LANES // heads_per_group
    outs, new_states = [], []
    for g in range(groups):
        gl = slice(g * LANES, (g + 1) * LANES)
        masks = [None] if heads_per_group == 1 else [
            (lane >= j * width) & (lane < (j + 1) * width) for j in range(heads_per_group)]
        pick = lambda x, m: x if m is None else jnp.where(m, x, 0.0)
        q_r_h = [pick(q_r[:, gl], m).astype(bf16) for m in masks]
        q_b_h = [pick(q_b[:, gl], m).astype(bf16) for m in masks]
        k_end_h = [pick(k_end[:, gl], m).astype(bf16) for m in masks]
        v_h = [vb[:, (g * heads_per_group + j) * LANES:(g * heads_per_group + j + 1) * LANES]
               for j in range(heads_per_group)]
        o_intra = []
        for j in range(heads_per_group):
            att = _dot_nt(q_r_h[j], k_r[:, gl])
            att = jnp.where(causal, att, 0.0).astype(bf16)
            o_intra.append(_dot(att, v_h[j]))
        upd = []
        for s in range(nsub):
            rs = slice(s * c, (s + 1) * c)
            u = _dot_tn(v_h[0][rs], k_end_h[0][rs])
            for j in range(1, heads_per_group):
                u = u + _dot_tn(v_h[j][rs], k_end_h[j][rs])
            upd.append(u)
        o_inter = [[] for _ in range(heads_per_group)]
        st = states[g] if scan else None
        group_states = []
        for s in range(nsub):
            rs = slice(s * c, (s + 1) * c)
            if not scan:
                st = states[s][g]
            st_b = st.astype(bf16)
            for j in range(heads_per_group):
                o_inter[j].append(_dot_nt(q_b_h[j][rs], st_b))
            st = st * decay[s * c:s * c + 1, gl] + upd[s]
            if not scan:
                group_states.append(st)
        for j in range(heads_per_group):
            outs.append(o_intra[j] + jnp.concatenate(o_inter[j], axis=0))
        new_states.append(st if scan else group_states)
    if not scan:
        new_states = [[new_states[g][s] for g in range(groups)] for s in range(nsub)]
    return outs, new_states


def _head_norm(o, g):
    return o * lax.rsqrt(jnp.mean(o * o, axis=-1, keepdims=True) + EPS) * g


def _mixer_block(zb, st_gla, st_hgrn, wgk, bgk, gng, lb, gnh, *, c, ref_row, tri, causal, valid, scan):
    z = zb.astype(f32)
    gk = _dot(zb[:, Z_R1:Z_R1 + R1_PAD], wgk) + bgk
    log_a = (jnp.minimum(gk, 0.0) - jnp.log(1.0 + jnp.exp(-jnp.abs(gk)))) / GLA_NORMALIZER
    k1 = z[:, Z_K1:Z_K1 + GLA_KDIM]
    if valid is not None:
        log_a = jnp.where(valid, log_a, 0.0)
        k1 = jnp.where(valid, k1, 0.0)
    o1, st_gla = _gated_block(
        z[:, Z_Q1:Z_Q1 + GLA_KDIM], k1, z[:, Z_V1:Z_V1 + GLA_WIDTH], log_a, st_gla,
        c=c, scale=GLA_DK ** -0.5, ref_row=ref_row, tri=tri, causal=causal,
        heads_per_group=LANES // GLA_DK, scan=scan)
    forget = lb + (1.0 - lb) * _sigmoid(z[:, Z_F2:Z_F2 + HGRN_FDIM])
    k2 = 1.0 - forget
    log_f = jnp.log(forget)
    if valid is not None:
        log_f = jnp.where(valid, log_f, 0.0)
        k2 = jnp.where(valid, k2, 0.0)
    q2 = z[:, Z_Q2:Z_Q2 + HGRN_FDIM]
    i2 = z[:, Z_I2:Z_I2 + HGRN_WIDTH]
    o2, st_hgrn = _gated_block(
        q2 * _sigmoid(q2), k2, i2 * _sigmoid(i2), log_f, st_hgrn,
        c=c, scale=1.0, ref_row=ref_row, tri=tri, causal=causal, heads_per_group=1, scan=scan)
    cols = []
    for h in range(GLA_HEADS):
        g1 = z[:, Z_G1 + h * GLA_DV:Z_G1 + (h + 1) * GLA_DV]
        cols.append(_head_norm(o1[h], gng) * (g1 * _sigmoid(g1)))
    for h in range(HGRN_HEADS):
        g2 = z[:, Z_G2 + h * HGRN_DV:Z_G2 + (h + 1) * HGRN_DV]
        cols.append(_head_norm(o2[h], gnh) * _sigmoid(g2))
    return jnp.concatenate(cols, axis=1), st_gla, st_hgrn


def _lower_bound(lbl_ref):
    l = lbl_ref[...]
    m = jnp.max(l, axis=0, keepdims=True)
    e = jnp.exp(l - m)
    return e[0:1, :] / jnp.sum(e, axis=0, keepdims=True)


GLA_GROUPS = GLA_KDIM // LANES


def _gla_state_t(s_ref_heads):
    return s_ref_heads.reshape(LANES, GLA_DV).T


def _gla_state_from_t(st):
    return st.T.reshape(LANES // GLA_DK, GLA_DK, GLA_DV)


def _mixer_prompt_kernel(z_ref, wgk_ref, bgk_ref, gng_ref, lbl_ref, gnh_ref,
                         o_ref, sg_out, sh_out, sg_t, sh_t, *, block_rows):
    j = pl.program_id(1)

    @pl.when(j == 0)
    def _():
        sg_t[...] = jnp.zeros_like(sg_t)
        sh_t[...] = jnp.zeros_like(sh_t)

    tri, causal = _block_consts(block_rows, CHUNK)
    lb = _lower_bound(lbl_ref)
    wgk, bgk, gng, gnh = wgk_ref[...], bgk_ref[...], gng_ref[...], gnh_ref[...]

    def body(blk, carry):
        r0 = pl.multiple_of(blk * block_rows, block_rows)
        o, st_g, st_h = _mixer_block(
            z_ref[pl.ds(r0, block_rows), :], [sg_t[g] for g in range(GLA_GROUPS)],
            [sh_t[h] for h in range(HGRN_HEADS)], wgk, bgk, gng, lb, gnh,
            c=CHUNK, ref_row=CHUNK // 2, tri=tri, causal=causal, valid=None, scan=True)
        o_ref[pl.ds(r0, block_rows), :] = o.astype(bf16)
        for g in range(GLA_GROUPS):
            sg_t[g] = st_g[g]
        for h in range(HGRN_HEADS):
            sh_t[h] = st_h[h]
        return carry

    lax.fori_loop(0, z_ref.shape[0] // block_rows, body, 0)

    @pl.when(j == pl.num_programs(1) - 1)
    def _():
        per = LANES // GLA_DK
        for g in range(GLA_GROUPS):
            sg_out[0, 0, g * per:(g + 1) * per] = _gla_state_from_t(sg_t[g])
        for h in range(HGRN_HEADS):
            sh_out[0, 0, h] = sh_t[h].T


def _mixer_prompt(z, wgk, bgk, gng, lbl, gnh, *, batch, seq, rows_per_step, block_rows):
    nj = seq // rows_per_step
    full = lambda shape: pl.BlockSpec(shape, lambda b, j: (0,) * len(shape))
    return pl.pallas_call(
        functools.partial(_mixer_prompt_kernel, block_rows=block_rows),
        grid=(batch, nj),
        in_specs=[
            pl.BlockSpec((rows_per_step, Z_W), lambda b, j: (b * nj + j, 0)),
            full((R1_PAD, GLA_KDIM)), full((1, GLA_KDIM)), full((1, GLA_DV)),
            full((2, HGRN_FDIM)), full((1, HGRN_DV)),
        ],
        out_specs=[
            pl.BlockSpec((rows_per_step, D_MODEL), lambda b, j: (b * nj + j, 0)),
            pl.BlockSpec((1, 1, GLA_HEADS, GLA_DK, GLA_DV), lambda b, j: (0, b, 0, 0, 0)),
            pl.BlockSpec((1, 1, HGRN_HEADS, HGRN_DK, HGRN_DV), lambda b, j: (0, b, 0, 0, 0)),
        ],
        out_shape=[
            jax.ShapeDtypeStruct((batch * seq, D_MODEL), bf16),
            jax.ShapeDtypeStruct((1, batch, GLA_HEADS, GLA_DK, GLA_DV), f32),
            jax.ShapeDtypeStruct((1, batch, HGRN_HEADS, HGRN_DK, HGRN_DV), f32),
        ],
        scratch_shapes=[
            pltpu.VMEM((GLA_GROUPS, GLA_DV, LANES), f32),
            pltpu.VMEM((HGRN_HEADS, HGRN_DV, HGRN_DK), f32),
        ],
        compiler_params=pltpu.CompilerParams(
            dimension_semantics=("arbitrary", "arbitrary"), vmem_limit_bytes=VMEM_LIMIT),
        name="mixer_prompt",
    )(z, wgk, bgk, gng, lbl, gnh)


def _mixer_sample_kernel(z_ref, sg_in, sh_in, wgk_ref, bgk_ref, gng_ref, lbl_ref, gnh_ref,
                         o_ref, sg_out, sh_out, *, batch_block, steps):
    rows = batch_block * SAMPLE_PAD_T
    tri, causal = _block_consts(rows, SAMPLE_PAD_T)
    step = lax.broadcasted_iota(i32, (rows, 1), 0) & (SAMPLE_PAD_T - 1)
    valid = step < steps
    lb = _lower_bound(lbl_ref)
    per = LANES // GLA_DK
    st_g = [[_gla_state_t(sg_in[0, s, g * per:(g + 1) * per]) for g in range(GLA_GROUPS)]
            for s in range(batch_block)]
    st_h = [[sh_in[0, s, h].T for h in range(HGRN_HEADS)] for s in range(batch_block)]
    o, st_g, st_h = _mixer_block(
        z_ref[...], st_g, st_h, wgk_ref[...], bgk_ref[...], gng_ref[...], lb, gnh_ref[...],
        c=SAMPLE_PAD_T, ref_row=steps // 2, tri=tri, causal=causal, valid=valid, scan=False)
    o_ref[...] = o
    for s in range(batch_block):
        for g in range(GLA_GROUPS):
            sg_out[0, s, g * per:(g + 1) * per] = _gla_state_from_t(st_g[s][g])
        for h in range(HGRN_HEADS):
            sh_out[0, s, h] = st_h[s][h].T


def _mixer_sample(z2, state_gla, state_hgrn, wgk, bgk, gng, lbl, gnh, *, steps, batch_block):
    batch = z2.shape[0] // SAMPLE_PAD_T
    rows = batch_block * SAMPLE_PAD_T
    full = lambda shape: pl.BlockSpec(shape, lambda i: (0,) * len(shape))
    sg_spec = pl.BlockSpec((1, batch_block, GLA_HEADS, GLA_DK, GLA_DV), lambda i: (0, i, 0, 0, 0))
    sh_spec = pl.BlockSpec((1, batch_block, HGRN_HEADS, HGRN_DK, HGRN_DV), lambda i: (0, i, 0, 0, 0))
    return pl.pallas_call(
        functools.partial(_mixer_sample_kernel, batch_block=batch_block, steps=steps),
        grid=(batch // batch_block,),
        in_specs=[
            pl.BlockSpec((rows, Z_W), lambda i: (i, 0)),
            sg_spec, sh_spec,
            full((R1_PAD, GLA_KDIM)), full((1, GLA_KDIM)), full((1, GLA_DV)),
            full((2, HGRN_FDIM)), full((1, HGRN_DV)),
        ],
        out_specs=[pl.BlockSpec((rows, D_MODEL), lambda i: (i, 0)), sg_spec, sh_spec],
        out_shape=[
            jax.ShapeDtypeStruct((batch * SAMPLE_PAD_T, D_MODEL), f32),
            jax.ShapeDtypeStruct(state_gla.shape, f32),
            jax.ShapeDtypeStruct(state_hgrn.shape, f32),
        ],
        compiler_params=pltpu.CompilerParams(
            dimension_semantics=("arbitrary",), vmem_limit_bytes=VMEM_LIMIT),
        name="mixer_sample",
    )(z2, state_gla, state_hgrn, wgk, bgk, gng, lbl, gnh)


def _store_slabs(ref, x):
    for s in range(SLABS):
        ref[pl.ds(s, x.shape[0], stride=SLABS), :] = x[:, s * LANES:(s + 1) * LANES]


def _load_slab_cols(ref, start, rows, s):
    return ref[pl.ds(start + s, rows, stride=SLABS), :]


def _outproj_kernel(xp_ref, xs_ref, op_ref, os_ref, wo_ref, g_ref, wr_ref, br_ref,
                    h_ref, xn_ref, ti_ref, tg_ref, rank_ref, cnt_ref, run_cnt, *, n_prompt_tiles):
    i = pl.program_id(0)

    @pl.when(i == 0)
    def _():
        run_cnt[...] = jnp.zeros_like(run_cnt)

    x = _select_rows(i < n_prompt_tiles, xp_ref, xs_ref)
    o = _select_rows(i < n_prompt_tiles, op_ref, os_ref)
    h = x + _dot(o, wo_ref[...])
    h_ref[...] = h
    xn = _rms(h, g_ref[...])
    _store_slabs(xn_ref, xn)
    l = _dot_nt(wr_ref[...], xn.astype(bf16)) + br_ref[...]
    expert = lax.broadcasted_iota(i32, l.shape, 0)
    neg = jnp.float32(-jnp.inf)
    vals, idxs = [], []
    for _ in range(TOP_K):
        m = jnp.max(l, axis=0, keepdims=True)
        idx = jnp.min(jnp.where(l == m, expert, N_EXPERTS), axis=0, keepdims=True)
        vals.append(m)
        idxs.append(idx)
        l = jnp.where(expert == idx, neg, l)
    es = [jnp.exp(v - vals[0]) for v in vals]
    denom = es[0] + es[1] + es[2] + es[3]

    onehot = jnp.zeros(l.shape, f32)
    for k in range(TOP_K):
        onehot = onehot + (expert == idxs[k]).astype(f32)
    n = l.shape[1]
    earlier = (lax.broadcasted_iota(i32, (n, n), 0) < lax.broadcasted_iota(i32, (n, n), 1)).astype(bf16)
    ahead = _dot(onehot.astype(bf16), earlier) + run_cnt[:, 0:1]
    run_cnt[...] = run_cnt[...] + jnp.sum(onehot, axis=1, keepdims=True)
    cnt_ref[...] = run_cnt[...].astype(i32)

    ranks = [jnp.sum(jnp.where(expert == idxs[k], ahead, 0.0), axis=0, keepdims=True).astype(i32)
             for k in range(TOP_K)]
    pad_i = jnp.zeros((ROUTE_ROWS - TOP_K, n), i32)
    ti_ref[...] = jnp.concatenate(idxs + [pad_i], axis=0)
    rank_ref[...] = jnp.concatenate(ranks + [pad_i], axis=0)
    tg_ref[...] = jnp.concatenate([e / denom for e in es] + [pad_i.astype(f32)], axis=0)


def _out_proj(xp, xs, o_p, o_s, wo, g, wr, br):
    tp, ts = xp.shape[0], xs.shape[0]
    npt = tp // TOKEN_TILE
    nt = npt + ts // TOKEN_TILE
    t = tp + ts
    row = lambda w: pl.BlockSpec((TOKEN_TILE, w), lambda i: (i, 0))
    route = pl.BlockSpec((ROUTE_ROWS, TOKEN_TILE), lambda i: (0, i))
    full = lambda shape: pl.BlockSpec(shape, lambda i: (0,) * len(shape))
    prompt_rows = pl.BlockSpec((TOKEN_TILE, D_MODEL), lambda i: (jnp.minimum(i, npt - 1), 0))
    sample_rows = pl.BlockSpec((TOKEN_TILE, D_MODEL), lambda i: (jnp.maximum(i - npt, 0), 0))
    return pl.pallas_call(
        functools.partial(_outproj_kernel, n_prompt_tiles=npt),
        grid=(nt,),
        in_specs=[
            prompt_rows, sample_rows, prompt_rows, sample_rows,
            full((D_MODEL, D_MODEL)), full((1, D_MODEL)),
            full((N_EXPERTS, D_MODEL)), full((N_EXPERTS, 1)),
        ],
        out_specs=[row(D_MODEL), pl.BlockSpec((TOKEN_TILE * SLABS, LANES), lambda i: (i, 0)),
                   route, route, route, full((N_EXPERTS, LANES))],
        out_shape=[
            jax.ShapeDtypeStruct((t, D_MODEL), f32),
            jax.ShapeDtypeStruct((t * SLABS, LANES), f32),
            jax.ShapeDtypeStruct((ROUTE_ROWS, t), i32),
            jax.ShapeDtypeStruct((ROUTE_ROWS, t), f32),
            jax.ShapeDtypeStruct((ROUTE_ROWS, t), i32),
            jax.ShapeDtypeStruct((N_EXPERTS, LANES), i32),
        ],
        scratch_shapes=[pltpu.VMEM((N_EXPERTS, LANES), f32)],
        compiler_params=pltpu.CompilerParams(
            dimension_semantics=("arbitrary",), vmem_limit_bytes=VMEM_LIMIT),
        name="out_proj_router",
    )(xp, xs, o_p, o_s, wo, g, wr, br)


def _dispatch_kernel(bv_ref, nz_ref, pos_ref, xn_hbm, xs_hbm, zeros, stage, in_sems, out_sems, zsem,
                     *, n_blocks):
    @pl.when(pl.program_id(0) == 0)
    def _():
        zeros[...] = jnp.zeros_like(zeros)

        def fill(b, carry):
            @pl.when(bv_ref[b] < MOE_TILE)
            def _():
                r0 = pl.multiple_of(b * (MOE_TILE * SLABS), MOE_TILE * SLABS)
                pltpu.make_async_copy(zeros, xs_hbm.at[pl.ds(r0, MOE_TILE * SLABS)], zsem).start()
            return carry

        def drain(b, carry):
            pltpu.make_async_copy(zeros, xs_hbm.at[pl.ds(0, MOE_TILE * SLABS)], zsem).wait()
            return carry

        lax.fori_loop(0, n_blocks, fill, 0)
        lax.fori_loop(0, nz_ref[0], drain, 0)

    i = pl.program_id(0)
    nt = pl.num_programs(0)
    tile_rows = DISPATCH_TILE * SLABS

    def fetch(t, slot):
        return pltpu.make_async_copy(
            xn_hbm.at[pl.ds(pl.multiple_of(t * tile_rows, tile_rows), tile_rows)],
            stage.at[pl.ds(pl.multiple_of(slot * tile_rows, tile_rows), tile_rows)], in_sems.at[slot])

    def wait_rows(slot):
        for _ in range(TOP_K):
            pltpu.make_async_copy(stage.at[pl.ds(0, tile_rows)], xs_hbm.at[pl.ds(0, tile_rows)],
                                  out_sems.at[slot]).wait()

    @pl.when(i == 0)
    def _():
        fetch(0, 0).start()

    @pl.when(i + 1 < nt)
    def _():
        fetch(i + 1, (i + 1) % 3).start()

    fetch(i, i % 3).wait()
    base = (i % 3) * tile_rows

    def start(r, carry):
        src = stage.at[pl.ds(pl.multiple_of(base + r * SLABS, SLABS), SLABS)]
        for k in range(TOP_K):
            dst0 = pl.multiple_of(pos_ref[0, 0, k * DISPATCH_TILE + r], SLABS)
            pltpu.make_async_copy(src, xs_hbm.at[pl.ds(dst0, SLABS)], out_sems.at[i % 2]).start(priority=k % 2)
        return carry

    lax.fori_loop(0, DISPATCH_TILE, start, 0, unroll=4)

    @pl.when(i > 0)
    def _():
        wait_rows((i + 1) % 2)

    @pl.when(i == nt - 1)
    def _():
        wait_rows(i % 2)


def _dispatch(block_valid, pos_tiles, xn2):
    t = xn2.shape[0] // SLABS
    nt = t // DISPATCH_TILE
    n_blocks = block_valid.shape[0]
    n_zero = jnp.sum((block_valid < MOE_TILE).astype(i32)).reshape(1)
    grid_spec = pltpu.PrefetchScalarGridSpec(
        num_scalar_prefetch=2,
        grid=(nt,),
        in_specs=[
            pl.BlockSpec((1, 1, DISPATCH_TILE * TOP_K), lambda i, bv, nz: (i, 0, 0),
                         memory_space=pltpu.SMEM),
            pl.BlockSpec(memory_space=pl.ANY),
        ],
        out_specs=pl.BlockSpec(memory_space=pl.ANY),
        scratch_shapes=[
            pltpu.VMEM((MOE_TILE * SLABS, LANES), f32),
            pltpu.VMEM((3 * DISPATCH_TILE * SLABS, LANES), f32),
            pltpu.SemaphoreType.DMA((3,)),
            pltpu.SemaphoreType.DMA((2,)),
            pltpu.SemaphoreType.DMA(()),
        ],
    )
    return pl.pallas_call(
        functools.partial(_dispatch_kernel, n_blocks=n_blocks),
        grid_spec=grid_spec,
        out_shape=jax.ShapeDtypeStruct((n_blocks * MOE_TILE * SLABS, LANES), f32),
        compiler_params=pltpu.CompilerParams(
            dimension_semantics=("arbitrary",), vmem_limit_bytes=VMEM_LIMIT),
        name="moe_dispatch",
    )(block_valid, n_zero, pos_tiles, xn2)


def _expert_kernel(be_ref, seg_ref, nxt_ref, bv_ref, nb_ref, x_ref, wgu_hbm, bgu_ref, wdn_hbm, bdn_ref, y_ref,
                   wgu_f32, wdn_f32, wgu_bf, wdn_bf, sems, *, ff_chunk, cast_rows):
    i = pl.program_id(0)

    @pl.when(i >= nb_ref[0])
    def _():
        y_ref[...] = jnp.zeros_like(y_ref)

    def fetch(e, slot):
        return (pltpu.make_async_copy(wgu_hbm.at[e], wgu_f32.at[slot], sems.at[slot, 0]),
                pltpu.make_async_copy(wdn_hbm.at[e], wdn_f32.at[slot], sems.at[slot, 1]))

    @pl.when((i < nb_ref[0]) & ((i == 0) | (be_ref[i] != be_ref[jnp.maximum(i - 1, 0)])))
    def _():
        slot = seg_ref[i] % 2

        @pl.when(i == 0)
        def _():
            for c in fetch(be_ref[0], 0):
                c.start()

        for c in fetch(be_ref[i], slot):
            c.wait()

        @pl.when(nxt_ref[i] >= 0)
        def _():
            for c in fetch(nxt_ref[i], 1 - slot):
                c.start()

        def cast(j, carry):
            r0 = pl.multiple_of(j * cast_rows, cast_rows)
            wgu_bf[pl.ds(r0, cast_rows), :] = wgu_f32[slot, pl.ds(r0, cast_rows), :].astype(bf16)
            wdn_bf[pl.ds(r0, cast_rows), :] = wdn_f32[slot, pl.ds(r0, cast_rows), :].astype(bf16)
            return carry
        lax.fori_loop(0, D_MODEL // cast_rows, cast, 0)

    def mlp(rows):
        xb = jnp.concatenate(
            [_load_slab_cols(x_ref, 0, rows, s).astype(bf16) for s in range(SLABS)], axis=1)
        hid = []
        for c0 in range(0, D_FF, ff_chunk):
            gate = _dot(xb, wgu_bf[:, c0:c0 + ff_chunk]) + bgu_ref[0, :, c0:c0 + ff_chunk]
            up = (_dot(xb, wgu_bf[:, D_FF + c0:D_FF + c0 + ff_chunk])
                  + bgu_ref[0, :, D_FF + c0:D_FF + c0 + ff_chunk])
            gate = jnp.minimum(gate, SWIGLU_LIMIT)
            up = jnp.clip(up, -SWIGLU_LIMIT, SWIGLU_LIMIT)
            hid.append(((up + 1.0) * (gate * _sigmoid(gate * SWIGLU_ALPHA))).astype(bf16))
        hid = jnp.concatenate(hid, axis=1)
        for n0 in range(0, D_MODEL, ff_chunk):
            y = _dot(hid, wdn_bf[:, n0:n0 + ff_chunk]) + bdn_ref[0, :, n0:n0 + ff_chunk]
            for s in range(n0 // LANES, (n0 + ff_chunk) // LANES):
                y_ref[pl.ds(s, rows, stride=SLABS), :] = y[:, s * LANES - n0:(s + 1) * LANES - n0]

    half = MOE_TILE // 2

    @pl.when((i < nb_ref[0]) & (bv_ref[i] > half))
    def _():
        mlp(MOE_TILE)

    @pl.when((i < nb_ref[0]) & (bv_ref[i] <= half))
    def _():
        mlp(half)
        y_ref[pl.ds(half * SLABS, half * SLABS), :] = jnp.zeros((half * SLABS, LANES), f32)


def _experts(block_e, block_seg, next_e, block_valid, n_used, xs, wgu, bgu, wdn, bdn):
    n_blocks = block_e.shape[0]
    grid_spec = pltpu.PrefetchScalarGridSpec(
        num_scalar_prefetch=5,
        grid=(n_blocks,),
        in_specs=[
            pl.BlockSpec((MOE_TILE * SLABS, LANES),
                         lambda i, be, sg, nx, bv, nb: (jnp.minimum(i, nb[0] - 1), 0)),
            pl.BlockSpec(memory_space=pl.ANY),
            pl.BlockSpec((1, 1, 2 * D_FF), lambda i, be, sg, nx, bv, nb: (be[i], 0, 0)),
            pl.BlockSpec(memory_space=pl.ANY),
            pl.BlockSpec((1, 1, D_MODEL), lambda i, be, sg, nx, bv, nb: (be[i], 0, 0)),
        ],
        out_specs=pl.BlockSpec((MOE_TILE * SLABS, LANES), lambda i, be, sg, nx, bv, nb: (i, 0)),
        scratch_shapes=[
            pltpu.VMEM((2, D_MODEL, 2 * D_FF), f32),
            pltpu.VMEM((2, D_FF, D_MODEL), f32),
            pltpu.VMEM((D_MODEL, 2 * D_FF), bf16),
            pltpu.VMEM((D_FF, D_MODEL), bf16),
            pltpu.SemaphoreType.DMA((2, 2)),
        ],
    )
    return pl.pallas_call(
        functools.partial(_expert_kernel, ff_chunk=256, cast_rows=128),
        grid_spec=grid_spec,
        out_shape=jax.ShapeDtypeStruct((n_blocks * MOE_TILE * SLABS, LANES), f32),
        compiler_params=pltpu.CompilerParams(
            dimension_semantics=("arbitrary",), vmem_limit_bytes=VMEM_LIMIT),
        name="moe_experts",
    )(block_e, block_seg, next_e, block_valid, n_used, xs, wgu, bgu, wdn, bdn)


def _combine_kernel(pos_cur, pos_nxt, ys_hbm, h_ref, tg_ref, pp_ref, ps_ref, gple_ref, wpg_ref, wpp_ref,
                    gfin_ref, yp_ref, ys_ref, rows, sems, *, n_prompt_tiles, issue_unroll):
    i = pl.program_id(0)
    nt = pl.num_programs(0)
    n = COMBINE_TILE * TOP_K
    slot_rows = n * SLABS

    def row_copy(pos_ref, slot_base, r, slot):
        src0 = pl.multiple_of(pos_ref[0, 0, r], SLABS)
        dst0 = pl.multiple_of(slot_base + r * SLABS, SLABS)
        return pltpu.make_async_copy(ys_hbm.at[pl.ds(src0, SLABS)], rows.at[pl.ds(dst0, SLABS)],
                                     sems.at[slot])

    def wait_slot(slot):
        pltpu.make_async_copy(ys_hbm.at[pl.ds(0, slot_rows)],
                              rows.at[pl.ds(pl.multiple_of(slot * slot_rows, slot_rows), slot_rows)],
                              sems.at[slot]).wait()

    def gather(pos_ref, slot):
        slot_base = slot * slot_rows

        def start(j, carry):
            for u in range(issue_unroll):
                row_copy(pos_ref, slot_base, issue_unroll * j + u, slot).start(priority=u % 2)
            return carry
        lax.fori_loop(0, n // issue_unroll, start, 0)

    @pl.when(i == 0)
    def _():
        gather(pos_cur, 0)

    @pl.when(i + 1 < nt)
    def _():
        gather(pos_nxt, (i + 1) % 2)

    slot = i % 2
    base = pl.multiple_of(slot * slot_rows, slot_rows)
    wait_slot(slot)

    tg = tg_ref[...]
    gate_k = [jnp.broadcast_to(tg[:, k:k + 1], (COMBINE_TILE, LANES)) for k in range(TOP_K)]
    cols = []
    for s in range(SLABS):
        moe = jnp.zeros((COMBINE_TILE, LANES), f32)
        for k in range(TOP_K):
            moe = moe + gate_k[k] * _load_slab_cols(
                rows, base + k * COMBINE_TILE * SLABS, COMBINE_TILE, s)
        cols.append(h_ref[:, s * LANES:(s + 1) * LANES] + moe)
    h = jnp.concatenate(cols, axis=1)
    gate = _sigmoid(_dot(_rms(h, gple_ref[...]).astype(bf16), wpg_ref[...]))
    p = _select_rows(i < n_prompt_tiles, pp_ref, ps_ref).astype(bf16)
    h = h + gate * _dot(p, wpp_ref[...])
    y = _rms(h, gfin_ref[...])

    @pl.when(i < n_prompt_tiles)
    def _():
        yp_ref[...] = y

    @pl.when(i >= n_prompt_tiles)
    def _():
        ys_ref[...] = y


def _combine(pos_tiles, ys, h1, tg, pp, ps, gple, wpg, wpp, gfin):
    t = h1.shape[0]
    nt = t // COMBINE_TILE
    npt = pp.shape[0] // COMBINE_TILE
    n = COMBINE_TILE * TOP_K
    row = lambda w: pl.BlockSpec((COMBINE_TILE, w), lambda i: (i, 0))
    full = lambda shape: pl.BlockSpec(shape, lambda i: (0,) * len(shape))
    return pl.pallas_call(
        functools.partial(_combine_kernel, n_prompt_tiles=npt, issue_unroll=16),
        grid=(nt,),
        in_specs=[
            pl.BlockSpec((1, 1, n), lambda i: (i, 0, 0), memory_space=pltpu.SMEM),
            pl.BlockSpec((1, 1, n), lambda i: (jnp.minimum(i + 1, nt - 1), 0, 0), memory_space=pltpu.SMEM),
            pl.BlockSpec(memory_space=pl.ANY),
            row(D_MODEL), row(LANES),
            pl.BlockSpec((COMBINE_TILE, PLE_DIM), lambda i: (jnp.minimum(i, npt - 1), 0)),
            pl.BlockSpec((COMBINE_TILE, PLE_DIM), lambda i: (jnp.maximum(i - npt, 0), 0)),
            full((1, D_MODEL)), full((D_MODEL, D_MODEL)), full((PLE_DIM, D_MODEL)),
            full((1, D_MODEL)),
        ],
        out_specs=[
            pl.BlockSpec((COMBINE_TILE, D_MODEL), lambda i: (jnp.minimum(i, npt - 1), 0)),
            pl.BlockSpec((COMBINE_TILE, D_MODEL), lambda i: (jnp.maximum(i - npt, 0), 0)),
        ],
        out_shape=[
            jax.ShapeDtypeStruct((pp.shape[0], D_MODEL), f32),
            jax.ShapeDtypeStruct((ps.shape[0], D_MODEL), f32),
        ],
        scratch_shapes=[
            pltpu.VMEM((2 * n * SLABS, LANES), f32),
            pltpu.SemaphoreType.DMA((2,)),
        ],
        compiler_params=pltpu.CompilerParams(
            dimension_semantics=("arbitrary",), vmem_limit_bytes=VMEM_LIMIT),
        name="moe_combine_ple",
    )(pos_tiles, pos_tiles, ys, h1, tg, pp, ps, gple, wpg, wpp, gfin)


def _pos_kernel(pstart_ref, ti_ref, rank_ref, pos_ref):
    ti = ti_ref[...]
    seg = jnp.zeros(ti.shape, i32)
    for e in range(N_EXPERTS):
        seg = jnp.where(ti == e, pstart_ref[e], seg)
    pos = (rank_ref[...] + seg) * SLABS
    for j in range(pos_ref.shape[0]):
        for k in range(TOP_K):
            pos_ref[j:j + 1, k * DISPATCH_TILE:(k + 1) * DISPATCH_TILE] = (
                pos[k:k + 1, j * DISPATCH_TILE:(j + 1) * DISPATCH_TILE])


def _routing_tables(top_i, rank, counts, n_blocks):
    padded = (counts + MOE_TILE - 1) // MOE_TILE * MOE_TILE
    pend = jnp.cumsum(padded)
    pstart = pend - padded
    block_rows = jnp.arange(n_blocks, dtype=i32) * MOE_TILE
    block_e = jnp.minimum(
        jnp.sum((block_rows[:, None] >= pend[None, :]).astype(i32), axis=1), N_EXPERTS - 1)
    experts = jnp.arange(N_EXPERTS, dtype=i32)
    of_block = lambda table: jnp.sum(jnp.where(block_e[:, None] == experts[None, :], table[None, :], 0), axis=1)
    block_valid = jnp.clip(of_block(pstart + counts) - block_rows, 0, MOE_TILE)
    n_used = pend[-1:] // MOE_TILE
    owns = padded > 0
    later = owns[None, :] & (experts[None, :] > experts[:, None])
    next_of_e = jnp.where(jnp.any(later, axis=1), jnp.argmax(later, axis=1), -1)
    block_seg = of_block(jnp.cumsum(owns.astype(i32)) - 1)
    next_e = of_block(next_of_e.astype(i32))

    nt = top_i.shape[1] // DISPATCH_TILE
    pos_tiles = pl.pallas_call(
        _pos_kernel,
        in_specs=[pl.BlockSpec(memory_space=pltpu.SMEM),
                  pl.BlockSpec(memory_space=pltpu.VMEM), pl.BlockSpec(memory_space=pltpu.VMEM)],
        out_specs=pl.BlockSpec(memory_space=pltpu.VMEM),
        out_shape=jax.ShapeDtypeStruct((nt, TOP_K * DISPATCH_TILE), i32),
        name="route_rows",
    )(pstart.astype(i32), top_i, rank).reshape(nt, 1, TOP_K * DISPATCH_TILE)
    tables = (block_e, block_seg, next_e, block_valid, n_used)
    return (pos_tiles,) + tuple(t.astype(i32) for t in tables)


def kernel(x_prompt, x_sample, p_prompt, p_sample, state_gla, state_hgrn, g_mix, w_in, w_gk2, b_gk2,
           gn_gla, lb_logits, gn_hgrn, w_out, g_ffn, w_router, b_router, w_gu, b_gu, w_dn, b_dn,
           g_ple, w_ple_gate, w_ple_proj, g_final):
    batch, seq, _ = x_prompt.shape
    dec_batch, dec_seq, _ = x_sample.shape
    tp, ts = batch * seq, dec_batch * dec_seq
    xp = x_prompt.reshape(tp, D_MODEL)
    xs = x_sample.reshape(ts, D_MODEL)

    wgk = jnp.concatenate(
        [w_gk2[0], jnp.zeros((R1_PAD - GLA_LOWRANK, GLA_KDIM), f32)], axis=0).astype(bf16)
    wr = w_router[0].T.astype(bf16)
    br = b_router[0].reshape(N_EXPERTS, 1)

    z = _in_proj(xp, xs, g_mix[0].reshape(1, D_MODEL), w_in[0])

    mixer_params = (wgk, b_gk2[0].reshape(1, GLA_KDIM), gn_gla[0].reshape(1, GLA_DV),
                    lb_logits[0:2], gn_hgrn[0].reshape(1, HGRN_DV))
    o_p, new_gla_p, new_hgrn_p = _mixer_prompt(
        z, *mixer_params, batch=batch, seq=seq, rows_per_step=1024, block_rows=256)
    z_s = jnp.pad(z[tp:].reshape(dec_batch, dec_seq, Z_W), ((0, 0), (0, SAMPLE_PAD_T - dec_seq), (0, 0)))
    o_s, new_gla_s, new_hgrn_s = _mixer_sample(
        z_s.reshape(dec_batch * SAMPLE_PAD_T, Z_W), state_gla, state_hgrn, *mixer_params,
        steps=dec_seq, batch_block=8)
    o_s = o_s.reshape(dec_batch, SAMPLE_PAD_T, D_MODEL)[:, :dec_seq].reshape(ts, D_MODEL).astype(bf16)

    h1, xn2, top_i, top_g, rank, counts = _out_proj(
        xp, xs, o_p, o_s, w_out[0].astype(bf16), g_ffn[0].reshape(1, D_MODEL), wr, br)

    t = tp + ts
    n_blocks = -(-(t * TOP_K) // MOE_TILE) + N_EXPERTS
    pos_tiles, block_e, block_seg, next_e, block_valid, n_used = _routing_tables(
        top_i, rank, counts[:, 0], n_blocks)
    x_sorted = _dispatch(block_valid, pos_tiles, xn2)
    y_sorted = _experts(block_e, block_seg, next_e, block_valid, n_used, x_sorted,
                        w_gu[0], b_gu[0].reshape(N_EXPERTS, 1, 2 * D_FF),
                        w_dn[0], b_dn[0].reshape(N_EXPERTS, 1, D_MODEL))

    gates = jnp.pad(top_g.T, ((0, 0), (0, LANES - ROUTE_ROWS)))
    y_p, y_s = _combine(pos_tiles, y_sorted, h1, gates, p_prompt[0].reshape(tp, PLE_DIM),
                        p_sample[0].reshape(ts, PLE_DIM), g_ple[0].reshape(1, D_MODEL),
                        w_ple_gate[0].astype(bf16), w_ple_proj[0].astype(bf16), g_final.reshape(1, D_MODEL))

    return (y_p.reshape(batch, seq, D_MODEL), y_s.reshape(dec_batch, dec_seq, D_MODEL),
            new_gla_p, new_hgrn_p, new_gla_s, new_hgrn_s)
```

```python
import functools

import jax
import jax.numpy as jnp
from jax import lax
from jax.experimental import pallas as pl
from jax.experimental.pallas import tpu as pltpu

f32 = jnp.float32
bf16 = jnp.bfloat16
i32 = jnp.int32

D_MODEL = 1024
GLA_HEADS = 4
GLA_DK = 64
GLA_DV = 128
GLA_KDIM = GLA_HEADS * GLA_DK
GLA_WIDTH = GLA_HEADS * GLA_DV
GLA_LOWRANK = 16
GLA_NORMALIZER = 16.0
HGRN_HEADS = 4
HGRN_DK = 128
HGRN_DV = 128
HGRN_FDIM = HGRN_HEADS * HGRN_DK
HGRN_WIDTH = HGRN_HEADS * HGRN_DV
CHUNK = 64
N_EXPERTS = 32
TOP_K = 4
D_FF = 1024
SWIGLU_LIMIT = 7.0
SWIGLU_ALPHA = 1.702
PLE_DIM = 256
EPS = 1e-6

LANES = 128
SAMPLE_PAD_T = 8
SLABS = D_MODEL // LANES
ROUTE_ROWS = 8
R1_PAD = LANES
Z_Q1 = 0
Z_K1 = Z_Q1 + GLA_KDIM
Z_V1 = Z_K1 + GLA_KDIM
Z_G1 = Z_V1 + GLA_WIDTH
Z_R1 = Z_G1 + GLA_WIDTH
Z_Q2 = Z_R1 + R1_PAD
Z_F2 = Z_Q2 + HGRN_FDIM
Z_I2 = Z_F2 + HGRN_FDIM
Z_G2 = Z_I2 + HGRN_WIDTH
Z_W = Z_G2 + HGRN_WIDTH

TOKEN_TILE = 512
MOE_TILE = 512
DISPATCH_TILE = 256
COMBINE_TILE = DISPATCH_TILE
VMEM_LIMIT = 56 * 1024 * 1024


def _rms(x, g):
    return x * lax.rsqrt(jnp.mean(x * x, axis=-1, keepdims=True) + EPS) * g


def _sigmoid(x):
    return 0.5 * jnp.tanh(0.5 * x) + 0.5


def _dot(a, b):
    return jnp.dot(a, b, preferred_element_type=f32)


def _dot_nt(a, b):
    return lax.dot_general(a, b, (((1,), (1,)), ((), ())), preferred_element_type=f32)


def _dot_tn(a, b):
    return lax.dot_general(a, b, (((0,), (0,)), ((), ())), preferred_element_type=f32)


def _cumsum_rows(tri, x):
    hi = x.astype(bf16)
    r1 = x - hi.astype(f32)
    mid = r1.astype(bf16)
    lo = (r1 - mid.astype(f32)).astype(bf16)
    return _dot(tri, hi) + _dot(tri, mid) + _dot(tri, lo)


def _select_rows(is_prompt, a_ref, b_ref):
    return jnp.where(is_prompt, a_ref[...], b_ref[...])


def _inproj_kernel(xp_ref, xs_ref, g_ref, w_hbm, z_ref, w_f32, w_bf, sem,
                   *, n_prompt_tiles, col_chunk, cast_rows):
    @pl.when(pl.program_id(0) == 0)
    def _():
        copy = pltpu.make_async_copy(w_hbm, w_f32, sem)
        copy.start()
        copy.wait()
        r1_end = Z_R1 + GLA_LOWRANK

        def cast(j, carry):
            rows = pl.ds(pl.multiple_of(j * cast_rows, cast_rows), cast_rows)
            w_bf[rows, 0:Z_R1] = w_f32[rows, 0:Z_R1].astype(bf16)
            w_bf[rows, Z_R1:Z_Q2] = jnp.concatenate(
                [w_f32[rows, Z_R1:r1_end], jnp.zeros((cast_rows, R1_PAD - GLA_LOWRANK), f32)],
                axis=1).astype(bf16)
            w_bf[rows, Z_Q2:Z_W] = w_f32[rows, r1_end:].astype(bf16)
            return carry
        lax.fori_loop(0, D_MODEL // cast_rows, cast, 0)

    is_prompt = pl.program_id(0) < n_prompt_tiles
    x = _select_rows(is_prompt, xp_ref, xs_ref)
    a = _rms(x, g_ref[...]).astype(bf16)
    for c0 in range(0, Z_W, col_chunk):
        c1 = min(c0 + col_chunk, Z_W)
        z_ref[:, c0:c1] = _dot(a, w_bf[:, c0:c1]).astype(bf16)


def _in_proj(xp, xs, g, w):
    tp, ts = xp.shape[0], xs.shape[0]
    npt = tp // TOKEN_TILE
    nt = npt + ts // TOKEN_TILE
    return pl.pallas_call(
        functools.partial(_inproj_kernel, n_prompt_tiles=npt, col_chunk=512, cast_rows=64),
        grid=(nt,),
        in_specs=[
            pl.BlockSpec((TOKEN_TILE, D_MODEL), lambda i: (jnp.minimum(i, npt - 1), 0)),
            pl.BlockSpec((TOKEN_TILE, D_MODEL), lambda i: (jnp.maximum(i - npt, 0), 0)),
            pl.BlockSpec((1, D_MODEL), lambda i: (0, 0)),
            pl.BlockSpec(memory_space=pl.ANY),
        ],
        out_specs=pl.BlockSpec((TOKEN_TILE, Z_W), lambda i: (i, 0)),
        out_shape=jax.ShapeDtypeStruct((tp + ts, Z_W), bf16),
        scratch_shapes=[
            pltpu.VMEM(w.shape, f32),
            pltpu.VMEM((D_MODEL, Z_W), bf16),
            pltpu.SemaphoreType.DMA(()),
        ],
        compiler_params=pltpu.CompilerParams(
            dimension_semantics=("arbitrary",), vmem_limit_bytes=VMEM_LIMIT),
        name="in_proj",
    )(xp, xs, g, w)


def _block_consts(r, c):
    shift = c.bit_length() - 1
    row = lax.broadcasted_iota(i32, (r, r), 0)
    col = lax.broadcasted_iota(i32, (r, r), 1)
    causal = (lax.shift_right_logical(row, shift) == lax.shift_right_logical(col, shift)) & (row >= col)
    return causal.astype(bf16), causal


def _subchunk_row(x, c, idx):
    parts = [jnp.broadcast_to(x[j * c + idx:j * c + idx + 1, :], (c, x.shape[1]))
             for j in range(x.shape[0] // c)]
    return jnp.concatenate(parts, axis=0)


def _gated_block(q, k, v, log_a, states, *, c, scale, ref_row, tri, causal, heads_per_group, scan):
    r = q.shape[0]
    nsub = r // c
    groups = q.shape[1] // LANES
    b = _cumsum_rows(tri, log_a)
    b_ref = _subchunk_row(b, c, ref_row)
    b_last = _subchunk_row(b, c, c - 1)
    qs = q * scale
    q_r = qs * jnp.exp(b - b_ref)
    k_r = (k * jnp.exp(b_ref - b)).astype(bf16)
    k_end = k * jnp.exp(b_last - b)
    q_b = qs * jnp.exp(b)
    decay = jnp.exp(b_last)
    vb = v.astype(bf16)
    lane = lax.broadcasted_iota(i32, (r, LANES), 1)
    width = LANES // heads_per_group
    outs, new_states = [], []
    for g in range(groups):
        gl = slice(g * LANES, (g + 1) * LANES)
        masks = [None] if heads_per_group == 1 else [
            (lane >= j * width) & (lane < (j + 1) * width) for j in range(heads_per_group)]
        pick = lambda x, m: x if m is None else jnp.where(m, x, 0.0)
        q_r_h = [pick(q_r[:, gl], m).astype(bf16) for m in masks]
        q_b_h = [pick(q_b[:, gl], m).astype(bf16) for m in masks]
        k_end_h = [pick(k_end[:, gl], m).astype(bf16) for m in masks]
        v_h = [vb[:, (g * heads_per_group + j) * LANES:(g * heads_per_group + j + 1) * LANES]
               for j in range(heads_per_group)]
        o_intra = []
        for j in range(heads_per_group):
            att = _dot_nt(q_r_h[j], k_r[:, gl])
            att = jnp.where(causal, att, 0.0).astype(bf16)
            o_intra.append(_dot(att, v_h[j]))
        upd = []
        for s in range(nsub):
            rs = slice(s * c, (s + 1) * c)
            u = _dot_tn(v_h[0][rs], k_end_h[0][rs])
            for j in range(1, heads_per_group):
                u = u + _dot_tn(v_h[j][rs], k_end_h[j][rs])
            upd.append(u)
        o_inter = [[] for _ in range(heads_per_group)]
        st = states[g] if scan else None
        group_states = []
        for s in range(nsub):
            rs = slice(s * c, (s + 1) * c)
            if not scan:
                st = states[s][g]
            st_b = st.astype(bf16)
            for j in range(heads_per_group):
                o_inter[j].append(_dot_nt(q_b_h[j][rs], st_b))
            st = st * decay[s * c:s * c + 1, gl] + upd[s]
            if not scan:
                group_states.append(st)
        for j in range(heads_per_group):
            outs.append(o_intra[j] + jnp.concatenate(o_inter[j], axis=0))
        new_states.append(st if scan else group_states)
    if not scan:
        new_states = [[new_states[g][s] for g in range(groups)] for s in range(nsub)]
    return outs, new_states


def _head_norm(o, g):
    return o * lax.rsqrt(jnp.mean(o * o, axis=-1, keepdims=True) + EPS) * g


def _mixer_block(zb, st_gla, st_hgrn, wgk, bgk, gng, lb, gnh, *, c, ref_row, tri, causal, valid, scan):
    z = zb.astype(f32)
    gk = _dot(zb[:, Z_R1:Z_R1 + R1_PAD], wgk) + bgk
    log_a = (jnp.minimum(gk, 0.0) - jnp.log(1.0 + jnp.exp(-jnp.abs(gk)))) / GLA_NORMALIZER
    k1 = z[:, Z_K1:Z_K1 + GLA_KDIM]
    if valid is not None:
        log_a = jnp.where(valid, log_a, 0.0)
        k1 = jnp.where(valid, k1, 0.0)
    o1, st_gla = _gated_block(
        z[:, Z_Q1:Z_Q1 + GLA_KDIM], k1, z[:, Z_V1:Z_V1 + GLA_WIDTH], log_a, st_gla,
        c=c, scale=GLA_DK ** -0.5, ref_row=ref_row, tri=tri, causal=causal,
        heads_per_group=LANES // GLA_DK, scan=scan)
    forget = lb + (1.0 - lb) * _sigmoid(z[:, Z_F2:Z_F2 + HGRN_FDIM])
    k2 = 1.0 - forget
    log_f = jnp.log(forget)
    if valid is not None:
        log_f = jnp.where(valid, log_f, 0.0)
        k2 = jnp.where(valid, k2, 0.0)
    q2 = z[:, Z_Q2:Z_Q2 + HGRN_FDIM]
    i2 = z[:, Z_I2:Z_I2 + HGRN_WIDTH]
    o2, st_hgrn = _gated_block(
        q2 * _sigmoid(q2), k2, i2 * _sigmoid(i2), log_f, st_hgrn,
        c=c, scale=1.0, ref_row=ref_row, tri=tri, causal=causal, heads_per_group=1, scan=scan)
    cols = []
    for h in range(GLA_HEADS):
        g1 = z[:, Z_G1 + h * GLA_DV:Z_G1 + (h + 1) * GLA_DV]
        cols.append(_head_norm(o1[h], gng) * (g1 * _sigmoid(g1)))
    for h in range(HGRN_HEADS):
        g2 = z[:, Z_G2 + h * HGRN_DV:Z_G2 + (h + 1) * HGRN_DV]
        cols.append(_head_norm(o2[h], gnh) * _sigmoid(g2))
    return jnp.concatenate(cols, axis=1), st_gla, st_hgrn


def _lower_bound(lbl_ref):
    l = lbl_ref[...]
    m = jnp.max(l, axis=0, keepdims=True)
    e = jnp.exp(l - m)
    return e[0:1, :] / jnp.sum(e, axis=0, keepdims=True)


GLA_GROUPS = GLA_KDIM // LANES


def _gla_state_t(s_ref_heads):
    return s_ref_heads.reshape(LANES, GLA_DV).T


def _gla_state_from_t(st):
    return st.T.reshape(LANES // GLA_DK, GLA_DK, GLA_DV)


def _mixer_prompt_kernel(z_ref, wgk_ref, bgk_ref, gng_ref, lbl_ref, gnh_ref,
                         o_ref, sg_out, sh_out, sg_t, sh_t, *, block_rows):
    j = pl.program_id(1)

    @pl.when(j == 0)
    def _():
        sg_t[...] = jnp.zeros_like(sg_t)
        sh_t[...] = jnp.zeros_like(sh_t)

    tri, causal = _block_consts(block_rows, CHUNK)
    lb = _lower_bound(lbl_ref)
    wgk, bgk, gng, gnh = wgk_ref[...], bgk_ref[...], gng_ref[...], gnh_ref[...]

    def body(blk, carry):
        r0 = pl.multiple_of(blk * block_rows, block_rows)
        o, st_g, st_h = _mixer_block(
            z_ref[pl.ds(r0, block_rows), :], [sg_t[g] for g in range(GLA_GROUPS)],
            [sh_t[h] for h in range(HGRN_HEADS)], wgk, bgk, gng, lb, gnh,
            c=CHUNK, ref_row=CHUNK // 2, tri=tri, causal=causal, valid=None, scan=True)
        o_ref[pl.ds(r0, block_rows), :] = o.astype(bf16)
        for g in range(GLA_GROUPS):
            sg_t[g] = st_g[g]
        for h in range(HGRN_HEADS):
            sh_t[h] = st_h[h]
        return carry

    lax.fori_loop(0, z_ref.shape[0] // block_rows, body, 0)

    @pl.when(j == pl.num_programs(1) - 1)
    def _():
        per = LANES // GLA_DK
        for g in range(GLA_GROUPS):
            sg_out[0, 0, g * per:(g + 1) * per] = _gla_state_from_t(sg_t[g])
        for h in range(HGRN_HEADS):
            sh_out[0, 0, h] = sh_t[h].T


def _mixer_prompt(z, wgk, bgk, gng, lbl, gnh, *, batch, seq, rows_per_step, block_rows):
    nj = seq // rows_per_step
    full = lambda shape: pl.BlockSpec(shape, lambda b, j: (0,) * len(shape))
    return pl.pallas_call(
        functools.partial(_mixer_prompt_kernel, block_rows=block_rows),
        grid=(batch, nj),
        in_specs=[
            pl.BlockSpec((rows_per_step, Z_W), lambda b, j: (b * nj + j, 0)),
            full((R1_PAD, GLA_KDIM)), full((1, GLA_KDIM)), full((1, GLA_DV)),
            full((2, HGRN_FDIM)), full((1, HGRN_DV)),
        ],
        out_specs=[
            pl.BlockSpec((rows_per_step, D_MODEL), lambda b, j: (b * nj + j, 0)),
            pl.BlockSpec((1, 1, GLA_HEADS, GLA_DK, GLA_DV), lambda b, j: (0, b, 0, 0, 0)),
            pl.BlockSpec((1, 1, HGRN_HEADS, HGRN_DK, HGRN_DV), lambda b, j: (0, b, 0, 0, 0)),
        ],
        out_shape=[
            jax.ShapeDtypeStruct((batch * seq, D_MODEL), bf16),
            jax.ShapeDtypeStruct((1, batch, GLA_HEADS, GLA_DK, GLA_DV), f32),
            jax.ShapeDtypeStruct((1, batch, HGRN_HEADS, HGRN_DK, HGRN_DV), f32),
        ],
        scratch_shapes=[
            pltpu.VMEM((GLA_GROUPS, GLA_DV, LANES), f32),
            pltpu.VMEM((HGRN_HEADS, HGRN_DV, HGRN_DK), f32),
        ],
        compiler_params=pltpu.CompilerParams(
            dimension_semantics=("arbitrary", "arbitrary"), vmem_limit_bytes=VMEM_LIMIT),
        name="mixer_prompt",
    )(z, wgk, bgk, gng, lbl, gnh)


def _mixer_sample_kernel(z_ref, sg_in, sh_in, wgk_ref, bgk_ref, gng_ref, lbl_ref, gnh_ref,
                         o_ref, sg_out, sh_out, *, batch_block, steps):
    rows = batch_block * SAMPLE_PAD_T
    tri, causal = _block_consts(rows, SAMPLE_PAD_T)
    step = lax.broadcasted_iota(i32, (rows, 1), 0) & (SAMPLE_PAD_T - 1)
    valid = step < steps
    lb = _lower_bound(lbl_ref)
    per = LANES // GLA_DK
    st_g = [[_gla_state_t(sg_in[0, s, g * per:(g + 1) * per]) for g in range(GLA_GROUPS)]
            for s in range(batch_block)]
    st_h = [[sh_in[0, s, h].T for h in range(HGRN_HEADS)] for s in range(batch_block)]
    o, st_g, st_h = _mixer_block(
        z_ref[...], st_g, st_h, wgk_ref[...], bgk_ref[...], gng_ref[...], lb, gnh_ref[...],
        c=SAMPLE_PAD_T, ref_row=steps // 2, tri=tri, causal=causal, valid=valid, scan=False)
    o_ref[...] = o
    for s in range(batch_block):
        for g in range(GLA_GROUPS):
            sg_out[0, s, g * per:(g + 1) * per] = _gla_state_from_t(st_g[s][g])
        for h in range(HGRN_HEADS):
            sh_out[0, s, h] = st_h[s][h].T


def _mixer_sample(z2, state_gla, state_hgrn, wgk, bgk, gng, lbl, gnh, *, steps, batch_block):
    batch = z2.shape[0] // SAMPLE_PAD_T
    rows = batch_block * SAMPLE_PAD_T
    full = lambda shape: pl.BlockSpec(shape, lambda i: (0,) * len(shape))
    sg_spec = pl.BlockSpec((1, batch_block, GLA_HEADS, GLA_DK, GLA_DV), lambda i: (0, i, 0, 0, 0))
    sh_spec = pl.BlockSpec((1, batch_block, HGRN_HEADS, HGRN_DK, HGRN_DV), lambda i: (0, i, 0, 0, 0))
    return pl.pallas_call(
        functools.partial(_mixer_sample_kernel, batch_block=batch_block, steps=steps),
        grid=(batch // batch_block,),
        in_specs=[
            pl.BlockSpec((rows, Z_W), lambda i: (i, 0)),
            sg_spec, sh_spec,
            full((R1_PAD, GLA_KDIM)), full((1, GLA_KDIM)), full((1, GLA_DV)),
            full((2, HGRN_FDIM)), full((1, HGRN_DV)),
        ],
        out_specs=[pl.BlockSpec((rows, D_MODEL), lambda i: (i, 0)), sg_spec, sh_spec],
        out_shape=[
            jax.ShapeDtypeStruct((batch * SAMPLE_PAD_T, D_MODEL), f32),
            jax.ShapeDtypeStruct(state_gla.shape, f32),
            jax.ShapeDtypeStruct(state_hgrn.shape, f32),
        ],
        compiler_params=pltpu.CompilerParams(
            dimension_semantics=("arbitrary",), vmem_limit_bytes=VMEM_LIMIT),
        name="mixer_sample",
    )(z2, state_gla, state_hgrn, wgk, bgk, gng, lbl, gnh)


def _load_slab_cols(ref, start, rows, s):
    return ref[pl.ds(start + s, rows, stride=SLABS), :]


def _outproj_kernel(xp_ref, xs_ref, op_ref, os_ref, wo_ref, g_ref, wr_ref, br_ref,
                    h_ref, xn_ref, ti_ref, tg_ref, rank_ref, cnt_ref, part_ref, run_cnt, *, n_prompt_tiles):
    i = pl.program_id(0)

    @pl.when(i == 0)
    def _():
        run_cnt[...] = jnp.zeros_like(run_cnt)

    x = _select_rows(i < n_prompt_tiles, xp_ref, xs_ref)
    o = _select_rows(i < n_prompt_tiles, op_ref, os_ref)
    h = x + _dot(o, wo_ref[...])
    h_ref[...] = h
    xn = _rms(h, g_ref[...])
    xn_ref[...] = xn.astype(bf16)
    l = _dot_nt(wr_ref[...], xn.astype(bf16)) + br_ref[...]
    expert = lax.broadcasted_iota(i32, l.shape, 0)
    neg = jnp.float32(-jnp.inf)
    vals, idxs = [], []
    for _ in range(TOP_K):
        m = jnp.max(l, axis=0, keepdims=True)
        idx = jnp.min(jnp.where(l == m, expert, N_EXPERTS), axis=0, keepdims=True)
        vals.append(m)
        idxs.append(idx)
        l = jnp.where(expert == idx, neg, l)
    es = [jnp.exp(v - vals[0]) for v in vals]
    denom = es[0] + es[1] + es[2] + es[3]

    onehot = jnp.zeros(l.shape, f32)
    for k in range(TOP_K):
        onehot = onehot + (expert == idxs[k]).astype(f32)
    n = l.shape[1]
    earlier = (lax.broadcasted_iota(i32, (n, n), 0) < lax.broadcasted_iota(i32, (n, n), 1)).astype(bf16)
    ahead = _dot(onehot.astype(bf16), earlier) + run_cnt[:, 0:1]
    run_cnt[...] = run_cnt[...] + jnp.sum(onehot, axis=1, keepdims=True)
    cnt_ref[...] = run_cnt[...].astype(i32)
    lane = lax.broadcasted_iota(i32, (N_EXPERTS, LANES), 1)
    parts = jnp.zeros((N_EXPERTS, LANES), f32)
    for p in range(n // DISPATCH_TILE):
        c = jnp.sum(onehot[:, p * DISPATCH_TILE:(p + 1) * DISPATCH_TILE], axis=1, keepdims=True)
        parts = jnp.where(lane == p, c, parts)
    part_ref[0] = parts.astype(i32)

    ranks = [jnp.sum(jnp.where(expert == idxs[k], ahead, 0.0), axis=0, keepdims=True).astype(i32)
             for k in range(TOP_K)]
    pad_i = jnp.zeros((ROUTE_ROWS - TOP_K, n), i32)
    ti_ref[...] = jnp.concatenate(idxs + [pad_i], axis=0)
    rank_ref[...] = jnp.concatenate(ranks + [pad_i], axis=0)
    tg_ref[...] = jnp.concatenate([e / denom for e in es] + [pad_i.astype(f32)], axis=0)


def _out_proj(xp, xs, o_p, o_s, wo, g, wr, br):
    tp, ts = xp.shape[0], xs.shape[0]
    npt = tp // TOKEN_TILE
    nt = npt + ts // TOKEN_TILE
    t = tp + ts
    row = lambda w: pl.BlockSpec((TOKEN_TILE, w), lambda i: (i, 0))
    route = pl.BlockSpec((ROUTE_ROWS, TOKEN_TILE), lambda i: (0, i))
    full = lambda shape: pl.BlockSpec(shape, lambda i: (0,) * len(shape))
    prompt_rows = pl.BlockSpec((TOKEN_TILE, D_MODEL), lambda i: (jnp.minimum(i, npt - 1), 0))
    sample_rows = pl.BlockSpec((TOKEN_TILE, D_MODEL), lambda i: (jnp.maximum(i - npt, 0), 0))
    return pl.pallas_call(
        functools.partial(_outproj_kernel, n_prompt_tiles=npt),
        grid=(nt,),
        in_specs=[
            prompt_rows, sample_rows, prompt_rows, sample_rows,
            full((D_MODEL, D_MODEL)), full((1, D_MODEL)),
            full((N_EXPERTS, D_MODEL)), full((N_EXPERTS, 1)),
        ],
        out_specs=[row(D_MODEL), row(D_MODEL),
                   route, route, route, full((N_EXPERTS, LANES)),
                   pl.BlockSpec((1, N_EXPERTS, LANES), lambda i: (i, 0, 0))],
        out_shape=[
            jax.ShapeDtypeStruct((t, D_MODEL), f32),
            jax.ShapeDtypeStruct((t, D_MODEL), bf16),
            jax.ShapeDtypeStruct((ROUTE_ROWS, t), i32),
            jax.ShapeDtypeStruct((ROUTE_ROWS, t), f32),
            jax.ShapeDtypeStruct((ROUTE_ROWS, t), i32),
            jax.ShapeDtypeStruct((N_EXPERTS, LANES), i32),
            jax.ShapeDtypeStruct((nt, N_EXPERTS, LANES), i32),
        ],
        scratch_shapes=[pltpu.VMEM((N_EXPERTS, LANES), f32)],
        compiler_params=pltpu.CompilerParams(
            dimension_semantics=("arbitrary",), vmem_limit_bytes=VMEM_LIMIT),
        name="out_proj_router",
    )(xp, xs, o_p, o_s, wo, g, wr, br)


def _dispatch_kernel(bv_ref, nz_ref, tab_ref, ti_ref, rank_ref, xn_ref, xs_hbm, zeros, stage, sems, zsem,
                     *, n_blocks):
    @pl.when(pl.program_id(0) == 0)
    def _():
        zeros[...] = jnp.zeros_like(zeros)

        def fill(b, carry):
            @pl.when(bv_ref[b] < MOE_TILE)
            def _():
                r0 = pl.multiple_of(b * (MOE_TILE * SLABS), MOE_TILE * SLABS)
                pltpu.make_async_copy(zeros, xs_hbm.at[pl.ds(r0, MOE_TILE * SLABS)], zsem).start()
            return carry

        def drain(b, carry):
            pltpu.make_async_copy(zeros, xs_hbm.at[pl.ds(0, MOE_TILE * SLABS)], zsem).wait()
            return carry

        lax.fori_loop(0, n_blocks, fill, 0)
        lax.fori_loop(0, nz_ref[0], drain, 0)

    i = pl.program_id(0)
    nt = pl.num_programs(0)
    n = DISPATCH_TILE * TOP_K
    slot_rows = n * SLABS
    slot = i % 2
    base = pl.multiple_of(slot * slot_rows, slot_rows)

    def wait_rows(s):
        pltpu.make_async_copy(stage.at[pl.ds(0, slot_rows)], xs_hbm.at[pl.ds(0, slot_rows)], sems.at[s]).wait()

    ti = ti_ref[...]
    shift = jnp.zeros(ti.shape, i32)
    for e in range(N_EXPERTS):
        shift = jnp.where(ti == e, tab_ref[0, 0, 3 * N_EXPERTS + e], shift)
    staged = rank_ref[...] + shift
    row = lax.broadcasted_iota(i32, (n, DISPATCH_TILE), 0)
    pick = row == staged[0:1, :]
    for k in range(1, TOP_K):
        pick = pick | (row == staged[k:k + 1, :])
    rows_sorted = _dot(pick.astype(bf16), xn_ref[...])

    for s in range(SLABS):
        stage[pl.ds(base + s, n, stride=SLABS), :] = rows_sorted[:, s * LANES:(s + 1) * LANES]

    max_bit = DISPATCH_TILE.bit_length() - 1
    split_bit = max_bit - 2
    for e in range(N_EXPERTS):
        cnt = tab_ref[0, 0, e]
        dst = tab_ref[0, 0, N_EXPERTS + e]
        src = tab_ref[0, 0, 2 * N_EXPERTS + e]

        def pieces(bits, cnt=cnt, dst=dst, src=src, e=e):
            for bit in bits:
                size = 1 << bit
                done = (cnt >> (bit + 1)) << (bit + 1)

                @pl.when(((cnt >> bit) & 1) == 1)
                def _(size=size, done=done):
                    pltpu.make_async_copy(
                        stage.at[pl.ds(pl.multiple_of(base + (src + done) * SLABS, SLABS), size * SLABS)],
                        xs_hbm.at[pl.ds(pl.multiple_of((dst + done) * SLABS, SLABS), size * SLABS)],
                        sems.at[slot]).start(priority=e % 2)

        @pl.when(cnt >= (1 << split_bit))
        def _(pieces=pieces):
            pieces(range(max_bit, split_bit - 1, -1))

        pieces(range(split_bit - 1, -1, -1))

    @pl.when(i > 0)
    def _():
        wait_rows(1 - slot)

    @pl.when(i == nt - 1)
    def _():
        wait_rows(slot)


def _dispatch(block_valid, tile_table, top_i, rank, xn):
    t = xn.shape[0]
    nt = t // DISPATCH_TILE
    n_blocks = block_valid.shape[0]
    n_zero = jnp.sum((block_valid < MOE_TILE).astype(i32)).reshape(1)
    route = pl.BlockSpec((ROUTE_ROWS, DISPATCH_TILE), lambda i, bv, nz: (0, i))
    grid_spec = pltpu.PrefetchScalarGridSpec(
        num_scalar_prefetch=2,
        grid=(nt,),
        in_specs=[
            pl.BlockSpec((1, 1, 4 * N_EXPERTS), lambda i, bv, nz: (i, 0, 0), memory_space=pltpu.SMEM),
            route, route,
            pl.BlockSpec((DISPATCH_TILE, D_MODEL), lambda i, bv, nz: (i, 0)),
        ],
        out_specs=pl.BlockSpec(memory_space=pl.ANY),
        scratch_shapes=[
            pltpu.VMEM((MOE_TILE * SLABS, LANES), f32),
            pltpu.VMEM((2 * DISPATCH_TILE * TOP_K * SLABS, LANES), f32),
            pltpu.SemaphoreType.DMA((2,)),
            pltpu.SemaphoreType.DMA(()),
        ],
    )
    return pl.pallas_call(
        functools.partial(_dispatch_kernel, n_blocks=n_blocks),
        grid_spec=grid_spec,
        out_shape=jax.ShapeDtypeStruct((n_blocks * MOE_TILE * SLABS, LANES), f32),
        compiler_params=pltpu.CompilerParams(
            dimension_semantics=("arbitrary",), vmem_limit_bytes=VMEM_LIMIT),
        name="moe_dispatch",
    )(block_valid, n_zero, tile_table, top_i, rank, xn)


def _expert_kernel(be_ref, seg_ref, nxt_ref, bv_ref, nb_ref, x_ref, wgu_hbm, bgu_ref, wdn_hbm, bdn_ref, y_ref,
                   wgu_f32, wdn_f32, wgu_bf, wdn_bf, sems, *, ff_chunk, cast_rows):
    i = pl.program_id(0)

    @pl.when(i >= nb_ref[0])
    def _():
        y_ref[...] = jnp.zeros_like(y_ref)

    def fetch(e, slot):
        return (pltpu.make_async_copy(wgu_hbm.at[e], wgu_f32.at[slot], sems.at[slot, 0]),
                pltpu.make_async_copy(wdn_hbm.at[e], wdn_f32.at[slot], sems.at[slot, 1]))

    @pl.when((i < nb_ref[0]) & ((i == 0) | (be_ref[i] != be_ref[jnp.maximum(i - 1, 0)])))
    def _():
        slot = seg_ref[i] % 2

        @pl.when(i == 0)
        def _():
            for c in fetch(be_ref[0], 0):
                c.start()

        for c in fetch(be_ref[i], slot):
            c.wait()

        @pl.when(nxt_ref[i] >= 0)
        def _():
            for c in fetch(nxt_ref[i], 1 - slot):
                c.start()

        def cast(j, carry):
            r0 = pl.multiple_of(j * cast_rows, cast_rows)
            wgu_bf[pl.ds(r0, cast_rows), :] = wgu_f32[slot, pl.ds(r0, cast_rows), :].astype(bf16)
            wdn_bf[pl.ds(r0, cast_rows), :] = wdn_f32[slot, pl.ds(r0, cast_rows), :].astype(bf16)
            return carry
        lax.fori_loop(0, D_MODEL // cast_rows, cast, 0)

    def mlp(rows):
        xb = jnp.concatenate(
            [_load_slab_cols(x_ref, 0, rows, s).astype(bf16) for s in range(SLABS)], axis=1)
        hid = []
        for c0 in range(0, D_FF, ff_chunk):
            gate = _dot(xb, wgu_bf[:, c0:c0 + ff_chunk]) + bgu_ref[0, :, c0:c0 + ff_chunk]
            up = (_dot(xb, wgu_bf[:, D_FF + c0:D_FF + c0 + ff_chunk])
                  + bgu_ref[0, :, D_FF + c0:D_FF + c0 + ff_chunk])
            gate = jnp.minimum(gate, SWIGLU_LIMIT)
            up = jnp.clip(up, -SWIGLU_LIMIT, SWIGLU_LIMIT)
            hid.append(((up + 1.0) * (gate * _sigmoid(gate * SWIGLU_ALPHA))).astype(bf16))
        hid = jnp.concatenate(hid, axis=1)
        for n0 in range(0, D_MODEL, ff_chunk):
            y = _dot(hid, wdn_bf[:, n0:n0 + ff_chunk]) + bdn_ref[0, :, n0:n0 + ff_chunk]
            for s in range(n0 // LANES, (n0 + ff_chunk) // LANES):
                y_ref[pl.ds(s, rows, stride=SLABS), :] = y[:, s * LANES - n0:(s + 1) * LANES - n0]

    half = MOE_TILE // 2

    @pl.when((i < nb_ref[0]) & (bv_ref[i] > half))
    def _():
        mlp(MOE_TILE)

    @pl.when((i < nb_ref[0]) & (bv_ref[i] <= half))
    def _():
        mlp(half)
        y_ref[pl.ds(half * SLABS, half * SLABS), :] = jnp.zeros((half * SLABS, LANES), f32)


def _experts(block_e, block_seg, next_e, block_valid, n_used, xs, wgu, bgu, wdn, bdn):
    n_blocks = block_e.shape[0]
    grid_spec = pltpu.PrefetchScalarGridSpec(
        num_scalar_prefetch=5,
        grid=(n_blocks,),
        in_specs=[
            pl.BlockSpec((MOE_TILE * SLABS, LANES),
                         lambda i, be, sg, nx, bv, nb: (jnp.minimum(i, nb[0] - 1), 0)),
            pl.BlockSpec(memory_space=pl.ANY),
            pl.BlockSpec((1, 1, 2 * D_FF), lambda i, be, sg, nx, bv, nb: (be[i], 0, 0)),
            pl.BlockSpec(memory_space=pl.ANY),
            pl.BlockSpec((1, 1, D_MODEL), lambda i, be, sg, nx, bv, nb: (be[i], 0, 0)),
        ],
        out_specs=pl.BlockSpec((MOE_TILE * SLABS, LANES), lambda i, be, sg, nx, bv, nb: (i, 0)),
        scratch_shapes=[
            pltpu.VMEM((2, D_MODEL, 2 * D_FF), f32),
            pltpu.VMEM((2, D_FF, D_MODEL), f32),
            pltpu.VMEM((D_MODEL, 2 * D_FF), bf16),
            pltpu.VMEM((D_FF, D_MODEL), bf16),
            pltpu.SemaphoreType.DMA((2, 2)),
        ],
    )
    return pl.pallas_call(
        functools.partial(_expert_kernel, ff_chunk=256, cast_rows=128),
        grid_spec=grid_spec,
        out_shape=jax.ShapeDtypeStruct((n_blocks * MOE_TILE * SLABS, LANES), f32),
        compiler_params=pltpu.CompilerParams(
            dimension_semantics=("arbitrary",), vmem_limit_bytes=VMEM_LIMIT),
        name="moe_experts",
    )(block_e, block_seg, next_e, block_valid, n_used, xs, wgu, bgu, wdn, bdn)


def _combine_kernel(pos_cur, pos_nxt, ys_hbm, h_ref, tg_ref, pp_ref, ps_ref, gple_ref, wpg_ref, wpp_ref,
                    gfin_ref, yp_ref, ys_ref, rows, sems, *, n_prompt_tiles, issue_unroll):
    i = pl.program_id(0)
    nt = pl.num_programs(0)
    n = COMBINE_TILE * TOP_K
    slot_rows = n * SLABS

    def row_copy(pos_ref, slot_base, r, slot):
        src0 = pl.multiple_of(pos_ref[0, 0, r], SLABS)
        dst0 = pl.multiple_of(slot_base + r * SLABS, SLABS)
        return pltpu.make_async_copy(ys_hbm.at[pl.ds(src0, SLABS)], rows.at[pl.ds(dst0, SLABS)],
                                     sems.at[slot])

    def wait_slot(slot):
        pltpu.make_async_copy(ys_hbm.at[pl.ds(0, slot_rows)],
                              rows.at[pl.ds(pl.multiple_of(slot * slot_rows, slot_rows), slot_rows)],
                              sems.at[slot]).wait()

    def gather(pos_ref, slot):
        slot_base = slot * slot_rows

        def start(j, carry):
            for u in range(issue_unroll):
                row_copy(pos_ref, slot_base, issue_unroll * j + u, slot).start(priority=u % 2)
            return carry
        lax.fori_loop(0, n // issue_unroll, start, 0)

    @pl.when(i == 0)
    def _():
        gather(pos_cur, 0)

    @pl.when(i + 1 < nt)
    def _():
        gather(pos_nxt, (i + 1) % 2)

    slot = i % 2
    base = pl.multiple_of(slot * slot_rows, slot_rows)
    wait_slot(slot)

    tg = tg_ref[...]
    gate_k = [jnp.broadcast_to(tg[:, k:k + 1], (COMBINE_TILE, LANES)) for k in range(TOP_K)]
    cols = []
    for s in range(SLABS):
        moe = jnp.zeros((COMBINE_TILE, LANES), f32)
        for k in range(TOP_K):
            moe = moe + gate_k[k] * _load_slab_cols(
                rows, base + k * COMBINE_TILE * SLABS, COMBINE_TILE, s)
        cols.append(h_ref[:, s * LANES:(s + 1) * LANES] + moe)
    h = jnp.concatenate(cols, axis=1)
    gate = _sigmoid(_dot(_rms(h, gple_ref[...]).astype(bf16), wpg_ref[...]))
    p = _select_rows(i < n_prompt_tiles, pp_ref, ps_ref).astype(bf16)
    h = h + gate * _dot(p, wpp_ref[...])
    y = _rms(h, gfin_ref[...])

    @pl.when(i < n_prompt_tiles)
    def _():
        yp_ref[...] = y

    @pl.when(i >= n_prompt_tiles)
    def _():
        ys_ref[...] = y


def _combine(pos_tiles, ys, h1, tg, pp, ps, gple, wpg, wpp, gfin):
    t = h1.shape[0]
    nt = t // COMBINE_TILE
    npt = pp.shape[0] // COMBINE_TILE
    n = COMBINE_TILE * TOP_K
    row = lambda w: pl.BlockSpec((COMBINE_TILE, w), lambda i: (i, 0))
    full = lambda shape: pl.BlockSpec(shape, lambda i: (0,) * len(shape))
    return pl.pallas_call(
        functools.partial(_combine_kernel, n_prompt_tiles=npt, issue_unroll=16),
        grid=(nt,),
        in_specs=[
            pl.BlockSpec((1, 1, n), lambda i: (i, 0, 0), memory_space=pltpu.SMEM),
            pl.BlockSpec((1, 1, n), lambda i: (jnp.minimum(i + 1, nt - 1), 0, 0), memory_space=pltpu.SMEM),
            pl.BlockSpec(memory_space=pl.ANY),
            row(D_MODEL), row(LANES),
            pl.BlockSpec((COMBINE_TILE, PLE_DIM), lambda i: (jnp.minimum(i, npt - 1), 0)),
            pl.BlockSpec((COMBINE_TILE, PLE_DIM), lambda i: (jnp.maximum(i - npt, 0), 0)),
            full((1, D_MODEL)), full((D_MODEL, D_MODEL)), full((PLE_DIM, D_MODEL)),
            full((1, D_MODEL)),
        ],
        out_specs=[
            pl.BlockSpec((COMBINE_TILE, D_MODEL), lambda i: (jnp.minimum(i, npt - 1), 0)),
            pl.BlockSpec((COMBINE_TILE, D_MODEL), lambda i: (jnp.maximum(i - npt, 0), 0)),
        ],
        out_shape=[
            jax.ShapeDtypeStruct((pp.shape[0], D_MODEL), f32),
            jax.ShapeDtypeStruct((ps.shape[0], D_MODEL), f32),
        ],
        scratch_shapes=[
            pltpu.VMEM((2 * n * SLABS, LANES), f32),
            pltpu.SemaphoreType.DMA((2,)),
        ],
        compiler_params=pltpu.CompilerParams(
            dimension_semantics=("arbitrary",), vmem_limit_bytes=VMEM_LIMIT),
        name="moe_combine_ple",
    )(pos_tiles, pos_tiles, ys, h1, tg, pp, ps, gple, wpg, wpp, gfin)


def _pos_kernel(pstart_ref, ti_ref, rank_ref, pos_ref):
    ti = ti_ref[...]
    seg = jnp.zeros(ti.shape, i32)
    for e in range(N_EXPERTS):
        seg = jnp.where(ti == e, pstart_ref[e], seg)
    pos = (rank_ref[...] + seg) * SLABS
    for j in range(pos_ref.shape[0]):
        for k in range(TOP_K):
            pos_ref[j:j + 1, k * DISPATCH_TILE:(k + 1) * DISPATCH_TILE] = (
                pos[k:k + 1, j * DISPATCH_TILE:(j + 1) * DISPATCH_TILE])


def _routing_tables(top_i, rank, counts, part_counts, n_blocks):
    padded = (counts + MOE_TILE - 1) // MOE_TILE * MOE_TILE
    pend = jnp.cumsum(padded)
    pstart = pend - padded
    block_rows = jnp.arange(n_blocks, dtype=i32) * MOE_TILE
    block_e = jnp.minimum(
        jnp.sum((block_rows[:, None] >= pend[None, :]).astype(i32), axis=1), N_EXPERTS - 1)
    experts = jnp.arange(N_EXPERTS, dtype=i32)
    of_block = lambda table: jnp.sum(jnp.where(block_e[:, None] == experts[None, :], table[None, :], 0), axis=1)
    block_valid = jnp.clip(of_block(pstart + counts) - block_rows, 0, MOE_TILE)
    n_used = pend[-1:] // MOE_TILE
    owns = padded > 0
    later = owns[None, :] & (experts[None, :] > experts[:, None])
    next_of_e = jnp.where(jnp.any(later, axis=1), jnp.argmax(later, axis=1), -1)
    block_seg = of_block(jnp.cumsum(owns.astype(i32)) - 1)
    next_e = of_block(next_of_e.astype(i32))

    nt = top_i.shape[1] // DISPATCH_TILE
    pos_tiles = pl.pallas_call(
        _pos_kernel,
        in_specs=[pl.BlockSpec(memory_space=pltpu.SMEM),
                  pl.BlockSpec(memory_space=pltpu.VMEM), pl.BlockSpec(memory_space=pltpu.VMEM)],
        out_specs=pl.BlockSpec(memory_space=pltpu.VMEM),
        out_shape=jax.ShapeDtypeStruct((nt, TOP_K * DISPATCH_TILE), i32),
        name="route_rows",
    )(pstart.astype(i32), top_i, rank).reshape(nt, 1, TOP_K * DISPATCH_TILE)
    tile_n = part_counts[:, :, :TOKEN_TILE // DISPATCH_TILE].transpose(0, 2, 1).reshape(nt, N_EXPERTS)
    seen = jnp.cumsum(tile_n, axis=0) - tile_n
    tile_src = jnp.cumsum(tile_n, axis=1) - tile_n
    tile_table = jnp.concatenate([tile_n, pstart[None, :] + seen, tile_src, tile_src - seen], axis=1)
    tables = (block_e, block_seg, next_e, block_valid, n_used)
    return (pos_tiles, tile_table.reshape(nt, 1, 4 * N_EXPERTS).astype(i32)) + tuple(
        t.astype(i32) for t in tables)


def kernel(x_prompt, x_sample, p_prompt, p_sample, state_gla, state_hgrn, g_mix, w_in, w_gk2, b_gk2,
           gn_gla, lb_logits, gn_hgrn, w_out, g_ffn, w_router, b_router, w_gu, b_gu, w_dn, b_dn,
           g_ple, w_ple_gate, w_ple_proj, g_final):
    batch, seq, _ = x_prompt.shape
    dec_batch, dec_seq, _ = x_sample.shape
    tp, ts = batch * seq, dec_batch * dec_seq
    xp = x_prompt.reshape(tp, D_MODEL)
    xs = x_sample.reshape(ts, D_MODEL)

    wgk = jnp.concatenate(
        [w_gk2[0], jnp.zeros((R1_PAD - GLA_LOWRANK, GLA_KDIM), f32)], axis=0).astype(bf16)
    wr = w_router[0].T.astype(bf16)
    br = b_router[0].reshape(N_EXPERTS, 1)

    z = _in_proj(xp, xs, g_mix[0].reshape(1, D_MODEL), w_in[0])

    mixer_params = (wgk, b_gk2[0].reshape(1, GLA_KDIM), gn_gla[0].reshape(1, GLA_DV),
                    lb_logits[0:2], gn_hgrn[0].reshape(1, HGRN_DV))
    o_p, new_gla_p, new_hgrn_p = _mixer_prompt(
        z, *mixer_params, batch=batch, seq=seq, rows_per_step=512, block_rows=256)
    z_s = jnp.pad(z[tp:].reshape(dec_batch, dec_seq, Z_W), ((0, 0), (0, SAMPLE_PAD_T - dec_seq), (0, 0)))
    o_s, new_gla_s, new_hgrn_s = _mixer_sample(
        z_s.reshape(dec_batch * SAMPLE_PAD_T, Z_W), state_gla, state_hgrn, *mixer_params,
        steps=dec_seq, batch_block=8)
    o_s = o_s.reshape(dec_batch, SAMPLE_PAD_T, D_MODEL)[:, :dec_seq].reshape(ts, D_MODEL).astype(bf16)

    h1, xn, top_i, top_g, rank, counts, part_counts = _out_proj(
        xp, xs, o_p, o_s, w_out[0].astype(bf16), g_ffn[0].reshape(1, D_MODEL), wr, br)

    t = tp + ts
    n_blocks = -(-(t * TOP_K) // MOE_TILE) + N_EXPERTS
    pos_tiles, tile_table, block_e, block_seg, next_e, block_valid, n_used = _routing_tables(
        top_i, rank, counts[:, 0], part_counts, n_blocks)
    x_sorted = _dispatch(block_valid, tile_table, top_i, rank, xn)
    y_sorted = _experts(block_e, block_seg, next_e, block_valid, n_used, x_sorted,
                        w_gu[0], b_gu[0].reshape(N_EXPERTS, 1, 2 * D_FF),
                        w_dn[0], b_dn[0].reshape(N_EXPERTS, 1, D_MODEL))

    gates = jnp.pad(top_g.T, ((0, 0), (0, LANES - ROUTE_ROWS)))
    y_p, y_s = _combine(pos_tiles, y_sorted, h1, gates, p_prompt[0].reshape(tp, PLE_DIM),
                        p_sample[0].reshape(ts, PLE_DIM), g_ple[0].reshape(1, D_MODEL),
                        w_ple_gate[0].astype(bf16), w_ple_proj[0].astype(bf16), g_final.reshape(1, D_MODEL))

    return (y_p.reshape(batch, seq, D_MODEL), y_s.reshape(dec_batch, dec_seq, D_MODEL),
            new_gla_p, new_hgrn_p, new_gla_s, new_hgrn_s)
```

```python
import functools

import jax
import jax.numpy as jnp
from jax import lax
from jax.experimental import pallas as pl
from jax.experimental.pallas import tpu as pltpu

f32 = jnp.float32
bf16 = jnp.bfloat16
i32 = jnp.int32

D_MODEL = 1024
GLA_HEADS = 4
GLA_DK = 64
GLA_DV = 128
GLA_KDIM = GLA_HEADS * GLA_DK
GLA_WIDTH = GLA_HEADS * GLA_DV
GLA_LOWRANK = 16
GLA_NORMALIZER = 16.0
HGRN_HEADS = 4
HGRN_DK = 128
HGRN_DV = 128
HGRN_FDIM = HGRN_HEADS * HGRN_DK
HGRN_WIDTH = HGRN_HEADS * HGRN_DV
CHUNK = 64
N_EXPERTS = 32
TOP_K = 4
D_FF = 1024
SWIGLU_LIMIT = 7.0
SWIGLU_ALPHA = 1.702
PLE_DIM = 256
EPS = 1e-6

LANES = 128
SAMPLE_PAD_T = 8
SLABS = D_MODEL // LANES
ROUTE_ROWS = 8
R1_PAD = LANES
Z_Q1 = 0
Z_K1 = Z_Q1 + GLA_KDIM
Z_V1 = Z_K1 + GLA_KDIM
Z_G1 = Z_V1 + GLA_WIDTH
Z_R1 = Z_G1 + GLA_WIDTH
Z_Q2 = Z_R1 + R1_PAD
Z_F2 = Z_Q2 + HGRN_FDIM
Z_I2 = Z_F2 + HGRN_FDIM
Z_G2 = Z_I2 + HGRN_WIDTH
Z_W = Z_G2 + HGRN_WIDTH

TOKEN_TILE = 512
MOE_TILE = 512
DISPATCH_TILE = 256
COMBINE_TILE = DISPATCH_TILE
RUN_BITS = DISPATCH_TILE.bit_length()
RUN_SHIFT = 0
RUN_COUNT = RUN_SHIFT + N_EXPERTS
RUN_SRC = RUN_COUNT + N_EXPERTS
RUN_DST = RUN_SRC + RUN_BITS * N_EXPERTS
RUN_TABLE = RUN_DST + RUN_BITS * N_EXPERTS
VMEM_LIMIT = 56 * 1024 * 1024


def _rms(x, g):
    return x * lax.rsqrt(jnp.mean(x * x, axis=-1, keepdims=True) + EPS) * g


def _sigmoid(x):
    return 0.5 * jnp.tanh(0.5 * x) + 0.5


def _dot(a, b):
    return jnp.dot(a, b, preferred_element_type=f32)


def _dot_nt(a, b):
    return lax.dot_general(a, b, (((1,), (1,)), ((), ())), preferred_element_type=f32)


def _dot_tn(a, b):
    return lax.dot_general(a, b, (((0,), (0,)), ((), ())), preferred_element_type=f32)


def _cumsum_rows(tri, x):
    hi = x.astype(bf16)
    r1 = x - hi.astype(f32)
    mid = r1.astype(bf16)
    lo = (r1 - mid.astype(f32)).astype(bf16)
    return _dot(tri, hi) + _dot(tri, mid) + _dot(tri, lo)


def _select_rows(is_prompt, a_ref, b_ref):
    return jnp.where(is_prompt, a_ref[...], b_ref[...])


def _inproj_kernel(xp_ref, xs_ref, g_ref, w_hbm, z_ref, w_f32, w_bf, sem,
                   *, n_prompt_tiles, col_chunk, cast_rows):
    @pl.when(pl.program_id(0) == 0)
    def _():
        copy = pltpu.make_async_copy(w_hbm, w_f32, sem)
        copy.start()
        copy.wait()
        r1_end = Z_R1 + GLA_LOWRANK

        def cast(j, carry):
            rows = pl.ds(pl.multiple_of(j * cast_rows, cast_rows), cast_rows)
            w_bf[rows, 0:Z_R1] = w_f32[rows, 0:Z_R1].astype(bf16)
            w_bf[rows, Z_R1:Z_Q2] = jnp.concatenate(
                [w_f32[rows, Z_R1:r1_end], jnp.zeros((cast_rows, R1_PAD - GLA_LOWRANK), f32)],
                axis=1).astype(bf16)
            w_bf[rows, Z_Q2:Z_W] = w_f32[rows, r1_end:].astype(bf16)
            return carry
        lax.fori_loop(0, D_MODEL // cast_rows, cast, 0)

    is_prompt = pl.program_id(0) < n_prompt_tiles
    x = _select_rows(is_prompt, xp_ref, xs_ref)
    a = _rms(x, g_ref[...]).astype(bf16)
    for c0 in range(0, Z_W, col_chunk):
        c1 = min(c0 + col_chunk, Z_W)
        z_ref[:, c0:c1] = _dot(a, w_bf[:, c0:c1]).astype(bf16)


def _in_proj(xp, xs, g, w):
    tp, ts = xp.shape[0], xs.shape[0]
    npt = tp // TOKEN_TILE
    nt = npt + ts // TOKEN_TILE
    return pl.pallas_call(
        functools.partial(_inproj_kernel, n_prompt_tiles=npt, col_chunk=512, cast_rows=64),
        grid=(nt,),
        in_specs=[
            pl.BlockSpec((TOKEN_TILE, D_MODEL), lambda i: (jnp.minimum(i, npt - 1), 0)),
            pl.BlockSpec((TOKEN_TILE, D_MODEL), lambda i: (jnp.maximum(i - npt, 0), 0)),
            pl.BlockSpec((1, D_MODEL), lambda i: (0, 0)),
            pl.BlockSpec(memory_space=pl.ANY),
        ],
        out_specs=pl.BlockSpec((TOKEN_TILE, Z_W), lambda i: (i, 0)),
        out_shape=jax.ShapeDtypeStruct((tp + ts, Z_W), bf16),
        scratch_shapes=[
            pltpu.VMEM(w.shape, f32),
            pltpu.VMEM((D_MODEL, Z_W), bf16),
            pltpu.SemaphoreType.DMA(()),
        ],
        compiler_params=pltpu.CompilerParams(
            dimension_semantics=("arbitrary",), vmem_limit_bytes=VMEM_LIMIT),
        name="in_proj",
    )(xp, xs, g, w)


def _block_consts(r, c):
    shift = c.bit_length() - 1
    row = lax.broadcasted_iota(i32, (r, r), 0)
    col = lax.broadcasted_iota(i32, (r, r), 1)
    causal = (lax.shift_right_logical(row, shift) == lax.shift_right_logical(col, shift)) & (row >= col)
    return causal.astype(bf16), causal


def _subchunk_row(x, c, idx):
    parts = [jnp.broadcast_to(x[j * c + idx:j * c + idx + 1, :], (c, x.shape[1]))
             for j in range(x.shape[0] // c)]
    return jnp.concatenate(parts, axis=0)


def _gated_block(q, k, v, log_a, states, *, c, scale, ref_row, tri, causal, heads_per_group, scan):
    r = q.shape[0]
    nsub = r // c
    groups = q.shape[1] // LANES
    b = _cumsum_rows(tri, log_a)
    b_ref = _subchunk_row(b, c, ref_row)
    b_last = _subchunk_row(b, c, c - 1)
    qs = q * scale
    q_r = qs * jnp.exp(b - b_ref)
    k_r = (k * jnp.exp(b_ref - b)).astype(bf16)
    k_end = k * jnp.exp(b_last - b)
    q_b = qs * jnp.exp(b)
    decay = jnp.exp(b_last)
    vb = v.astype(bf16)
    lane = lax.broadcasted_iota(i32, (r, LANES), 1)
    width = LANES // heads_per_group
    outs, new_states = [], []
    for g in range(groups):
        gl = slice(g * LANES, (g + 1) * LANES)
        masks = [None] if heads_per_group == 1 else [
            (lane >= j * width) & (lane < (j + 1) * width) for j in range(heads_per_group)]
        pick = lambda x, m: x if m is None else jnp.where(m, x, 0.0)
        q_r_h = [pick(q_r[:, gl], m).astype(bf16) for m in masks]
        q_b_h = [pick(q_b[:, gl], m).astype(bf16) for m in masks]
        k_end_h = [pick(k_end[:, gl], m).astype(bf16) for m in masks]
        v_h = [vb[:, (g * heads_per_group + j) * LANES:(g * heads_per_group + j + 1) * LANES]
               for j in range(heads_per_group)]
        o_intra = []
        for j in range(heads_per_group):
            att = _dot_nt(q_r_h[j], k_r[:, gl])
            att = jnp.where(causal, att, 0.0).astype(bf16)
            o_intra.append(_dot(att, v_h[j]))
        upd = []
        for s in range(nsub):
            rs = slice(s * c, (s + 1) * c)
            u = _dot_tn(v_h[0][rs], k_end_h[0][rs])
            for j in range(1, heads_per_group):
                u = u + _dot_tn(v_h[j][rs], k_end_h[j][rs])
            upd.append(u)
        o_inter = [[] for _ in range(heads_per_group)]
        st = states[g] if scan else None
        group_states = []
        for s in range(nsub):
            rs = slice(s * c, (s + 1) * c)
            if not scan:
                st = states[s][g]
            st_b = st.astype(bf16)
            for j in range(heads_per_group):
                o_inter[j].append(_dot_nt(q_b_h[j][rs], st_b))
            st = st * decay[s * c:s * c + 1, gl] + upd[s]
            if not scan:
                group_states.append(st)
        for j in range(heads_per_group):
            outs.append(o_intra[j] + jnp.concatenate(o_inter[j], axis=0))
        new_states.append(st if scan else group_states)
    if not scan:
        new_states = [[new_states[g][s] for g in range(groups)] for s in range(nsub)]
    return outs, new_states


def _head_norm(o, g):
    return o * lax.rsqrt(jnp.mean(o * o, axis=-1, keepdims=True) + EPS) * g


def _mixer_block(zb, st_gla, st_hgrn, wgk, bgk, gng, lb, gnh, *, c, ref_row, tri, causal, valid, scan):
    z = zb.astype(f32)
    gk = _dot(zb[:, Z_R1:Z_R1 + R1_PAD], wgk) + bgk
    log_a = (jnp.minimum(gk, 0.0) - jnp.log(1.0 + jnp.exp(-jnp.abs(gk)))) / GLA_NORMALIZER
    k1 = z[:, Z_K1:Z_K1 + GLA_KDIM]
    if valid is not None:
        log_a = jnp.where(valid, log_a, 0.0)
        k1 = jnp.where(valid, k1, 0.0)
    o1, st_gla = _gated_block(
        z[:, Z_Q1:Z_Q1 + GLA_KDIM], k1, z[:, Z_V1:Z_V1 + GLA_WIDTH], log_a, st_gla,
        c=c, scale=GLA_DK ** -0.5, ref_row=ref_row, tri=tri, causal=causal,
        heads_per_group=LANES // GLA_DK, scan=scan)
    forget = lb + (1.0 - lb) * _sigmoid(z[:, Z_F2:Z_F2 + HGRN_FDIM])
    k2 = 1.0 - forget
    log_f = jnp.log(forget)
    if valid is not None:
        log_f = jnp.where(valid, log_f, 0.0)
        k2 = jnp.where(valid, k2, 0.0)
    q2 = z[:, Z_Q2:Z_Q2 + HGRN_FDIM]
    i2 = z[:, Z_I2:Z_I2 + HGRN_WIDTH]
    o2, st_hgrn = _gated_block(
        q2 * _sigmoid(q2), k2, i2 * _sigmoid(i2), log_f, st_hgrn,
        c=c, scale=1.0, ref_row=ref_row, tri=tri, causal=causal, heads_per_group=1, scan=scan)
    cols = []
    for h in range(GLA_HEADS):
        g1 = z[:, Z_G1 + h * GLA_DV:Z_G1 + (h + 1) * GLA_DV]
        cols.append(_head_norm(o1[h], gng) * (g1 * _sigmoid(g1)))
    for h in range(HGRN_HEADS):
        g2 = z[:, Z_G2 + h * HGRN_DV:Z_G2 + (h + 1) * HGRN_DV]
        cols.append(_head_norm(o2[h], gnh) * _sigmoid(g2))
    return jnp.concatenate(cols, axis=1), st_gla, st_hgrn


def _lower_bound(lbl_ref):
    l = lbl_ref[...]
    m = jnp.max(l, axis=0, keepdims=True)
    e = jnp.exp(l - m)
    return e[0:1, :] / jnp.sum(e, axis=0, keepdims=True)


GLA_GROUPS = GLA_KDIM // LANES


def _gla_state_t(s_ref_heads):
    return s_ref_heads.reshape(LANES, GLA_DV).T


def _gla_state_from_t(st):
    return st.T.reshape(LANES // GLA_DK, GLA_DK, GLA_DV)


def _mixer_prompt_kernel(z_ref, wgk_ref, bgk_ref, gng_ref, lbl_ref, gnh_ref,
                         o_ref, sg_out, sh_out, sg_t, sh_t, *, block_rows):
    j = pl.program_id(1)

    @pl.when(j == 0)
    def _():
        sg_t[...] = jnp.zeros_like(sg_t)
        sh_t[...] = jnp.zeros_like(sh_t)

    tri, causal = _block_consts(block_rows, CHUNK)
    lb = _lower_bound(lbl_ref)
    wgk, bgk, gng, gnh = wgk_ref[...], bgk_ref[...], gng_ref[...], gnh_ref[...]

    def body(blk, carry):
        r0 = pl.multiple_of(blk * block_rows, block_rows)
        o, st_g, st_h = _mixer_block(
            z_ref[pl.ds(r0, block_rows), :], [sg_t[g] for g in range(GLA_GROUPS)],
            [sh_t[h] for h in range(HGRN_HEADS)], wgk, bgk, gng, lb, gnh,
            c=CHUNK, ref_row=CHUNK // 2, tri=tri, causal=causal, valid=None, scan=True)
        o_ref[pl.ds(r0, block_rows), :] = o.astype(bf16)
        for g in range(GLA_GROUPS):
            sg_t[g] = st_g[g]
        for h in range(HGRN_HEADS):
            sh_t[h] = st_h[h]
        return carry

    lax.fori_loop(0, z_ref.shape[0] // block_rows, body, 0)

    @pl.when(j == pl.num_programs(1) - 1)
    def _():
        per = LANES // GLA_DK
        for g in range(GLA_GROUPS):
            sg_out[0, 0, g * per:(g + 1) * per] = _gla_state_from_t(sg_t[g])
        for h in range(HGRN_HEADS):
            sh_out[0, 0, h] = sh_t[h].T


def _mixer_prompt(z, wgk, bgk, gng, lbl, gnh, *, batch, seq, rows_per_step, block_rows):
    nj = seq // rows_per_step
    full = lambda shape: pl.BlockSpec(shape, lambda b, j: (0,) * len(shape))
    return pl.pallas_call(
        functools.partial(_mixer_prompt_kernel, block_rows=block_rows),
        grid=(batch, nj),
        in_specs=[
            pl.BlockSpec((rows_per_step, Z_W), lambda b, j: (b * nj + j, 0)),
            full((R1_PAD, GLA_KDIM)), full((1, GLA_KDIM)), full((1, GLA_DV)),
            full((2, HGRN_FDIM)), full((1, HGRN_DV)),
        ],
        out_specs=[
            pl.BlockSpec((rows_per_step, D_MODEL), lambda b, j: (b * nj + j, 0)),
            pl.BlockSpec((1, 1, GLA_HEADS, GLA_DK, GLA_DV), lambda b, j: (0, b, 0, 0, 0)),
            pl.BlockSpec((1, 1, HGRN_HEADS, HGRN_DK, HGRN_DV), lambda b, j: (0, b, 0, 0, 0)),
        ],
        out_shape=[
            jax.ShapeDtypeStruct((batch * seq, D_MODEL), bf16),
            jax.ShapeDtypeStruct((1, batch, GLA_HEADS, GLA_DK, GLA_DV), f32),
            jax.ShapeDtypeStruct((1, batch, HGRN_HEADS, HGRN_DK, HGRN_DV), f32),
        ],
        scratch_shapes=[
            pltpu.VMEM((GLA_GROUPS, GLA_DV, LANES), f32),
            pltpu.VMEM((HGRN_HEADS, HGRN_DV, HGRN_DK), f32),
        ],
        compiler_params=pltpu.CompilerParams(
            dimension_semantics=("arbitrary", "arbitrary"), vmem_limit_bytes=VMEM_LIMIT),
        name="mixer_prompt",
    )(z, wgk, bgk, gng, lbl, gnh)


def _mixer_sample_kernel(z_ref, sg_in, sh_in, wgk_ref, bgk_ref, gng_ref, lbl_ref, gnh_ref,
                         o_ref, sg_out, sh_out, *, batch_block, steps):
    rows = batch_block * SAMPLE_PAD_T
    tri, causal = _block_consts(rows, SAMPLE_PAD_T)
    step = lax.broadcasted_iota(i32, (rows, 1), 0) & (SAMPLE_PAD_T - 1)
    valid = step < steps
    lb = _lower_bound(lbl_ref)
    per = LANES // GLA_DK
    st_g = [[_gla_state_t(sg_in[0, s, g * per:(g + 1) * per]) for g in range(GLA_GROUPS)]
            for s in range(batch_block)]
    st_h = [[sh_in[0, s, h].T for h in range(HGRN_HEADS)] for s in range(batch_block)]
    o, st_g, st_h = _mixer_block(
        z_ref[...], st_g, st_h, wgk_ref[...], bgk_ref[...], gng_ref[...], lb, gnh_ref[...],
        c=SAMPLE_PAD_T, ref_row=steps // 2, tri=tri, causal=causal, valid=valid, scan=False)
    o_ref[...] = o
    for s in range(batch_block):
        for g in range(GLA_GROUPS):
            sg_out[0, s, g * per:(g + 1) * per] = _gla_state_from_t(st_g[s][g])
        for h in range(HGRN_HEADS):
            sh_out[0, s, h] = st_h[s][h].T


def _mixer_sample(z2, state_gla, state_hgrn, wgk, bgk, gng, lbl, gnh, *, steps, batch_block):
    batch = z2.shape[0] // SAMPLE_PAD_T
    rows = batch_block * SAMPLE_PAD_T
    full = lambda shape: pl.BlockSpec(shape, lambda i: (0,) * len(shape))
    sg_spec = pl.BlockSpec((1, batch_block, GLA_HEADS, GLA_DK, GLA_DV), lambda i: (0, i, 0, 0, 0))
    sh_spec = pl.BlockSpec((1, batch_block, HGRN_HEADS, HGRN_DK, HGRN_DV), lambda i: (0, i, 0, 0, 0))
    return pl.pallas_call(
        functools.partial(_mixer_sample_kernel, batch_block=batch_block, steps=steps),
        grid=(batch // batch_block,),
        in_specs=[
            pl.BlockSpec((rows, Z_W), lambda i: (i, 0)),
            sg_spec, sh_spec,
            full((R1_PAD, GLA_KDIM)), full((1, GLA_KDIM)), full((1, GLA_DV)),
            full((2, HGRN_FDIM)), full((1, HGRN_DV)),
        ],
        out_specs=[pl.BlockSpec((rows, D_MODEL), lambda i: (i, 0)), sg_spec, sh_spec],
        out_shape=[
            jax.ShapeDtypeStruct((batch * SAMPLE_PAD_T, D_MODEL), f32),
            jax.ShapeDtypeStruct(state_gla.shape, f32),
            jax.ShapeDtypeStruct(state_hgrn.shape, f32),
        ],
        compiler_params=pltpu.CompilerParams(
            dimension_semantics=("arbitrary",), vmem_limit_bytes=VMEM_LIMIT),
        name="mixer_sample",
    )(z2, state_gla, state_hgrn, wgk, bgk, gng, lbl, gnh)


def _load_slab_cols(ref, start, rows, s):
    return ref[pl.ds(start + s, rows, stride=SLABS), :]


def _outproj_kernel(xp_ref, xs_ref, op_ref, os_ref, wo_ref, g_ref, wr_ref, br_ref,
                    h_ref, xn_ref, ti_ref, tg_ref, rank_ref, cnt_ref, part_ref, run_cnt, *, n_prompt_tiles):
    i = pl.program_id(0)

    @pl.when(i == 0)
    def _():
        run_cnt[...] = jnp.zeros_like(run_cnt)

    x = _select_rows(i < n_prompt_tiles, xp_ref, xs_ref)
    o = _select_rows(i < n_prompt_tiles, op_ref, os_ref)
    h = x + _dot(o, wo_ref[...])
    h_ref[...] = h
    xn = _rms(h, g_ref[...])
    xn_ref[...] = xn.astype(bf16)
    l = _dot_nt(wr_ref[...], xn.astype(bf16)) + br_ref[...]
    expert = lax.broadcasted_iota(i32, l.shape, 0)
    neg = jnp.float32(-jnp.inf)
    vals, idxs = [], []
    for _ in range(TOP_K):
        m = jnp.max(l, axis=0, keepdims=True)
        idx = jnp.min(jnp.where(l == m, expert, N_EXPERTS), axis=0, keepdims=True)
        vals.append(m)
        idxs.append(idx)
        l = jnp.where(expert == idx, neg, l)
    es = [jnp.exp(v - vals[0]) for v in vals]
    denom = es[0] + es[1] + es[2] + es[3]

    onehot = jnp.zeros(l.shape, f32)
    for k in range(TOP_K):
        onehot = onehot + (expert == idxs[k]).astype(f32)
    n = l.shape[1]
    earlier = (lax.broadcasted_iota(i32, (n, n), 0) < lax.broadcasted_iota(i32, (n, n), 1)).astype(bf16)
    ahead = _dot(onehot.astype(bf16), earlier) + run_cnt[:, 0:1]
    run_cnt[...] = run_cnt[...] + jnp.sum(onehot, axis=1, keepdims=True)
    cnt_ref[...] = run_cnt[...].astype(i32)
    lane = lax.broadcasted_iota(i32, (N_EXPERTS, LANES), 1)
    parts = jnp.zeros((N_EXPERTS, LANES), f32)
    for p in range(n // DISPATCH_TILE):
        c = jnp.sum(onehot[:, p * DISPATCH_TILE:(p + 1) * DISPATCH_TILE], axis=1, keepdims=True)
        parts = jnp.where(lane == p, c, parts)
    part_ref[0] = parts.astype(i32)

    ranks = [jnp.sum(jnp.where(expert == idxs[k], ahead, 0.0), axis=0, keepdims=True).astype(i32)
             for k in range(TOP_K)]
    pad_i = jnp.zeros((ROUTE_ROWS - TOP_K, n), i32)
    ti_ref[...] = jnp.concatenate(idxs + [pad_i], axis=0)
    rank_ref[...] = jnp.concatenate(ranks + [pad_i], axis=0)
    tg_ref[...] = jnp.concatenate([e / denom for e in es] + [pad_i.astype(f32)], axis=0)


def _out_proj(xp, xs, o_p, o_s, wo, g, wr, br):
    tp, ts = xp.shape[0], xs.shape[0]
    npt = tp // TOKEN_TILE
    nt = npt + ts // TOKEN_TILE
    t = tp + ts
    row = lambda w: pl.BlockSpec((TOKEN_TILE, w), lambda i: (i, 0))
    route = pl.BlockSpec((ROUTE_ROWS, TOKEN_TILE), lambda i: (0, i))
    full = lambda shape: pl.BlockSpec(shape, lambda i: (0,) * len(shape))
    prompt_rows = pl.BlockSpec((TOKEN_TILE, D_MODEL), lambda i: (jnp.minimum(i, npt - 1), 0))
    sample_rows = pl.BlockSpec((TOKEN_TILE, D_MODEL), lambda i: (jnp.maximum(i - npt, 0), 0))
    return pl.pallas_call(
        functools.partial(_outproj_kernel, n_prompt_tiles=npt),
        grid=(nt,),
        in_specs=[
            prompt_rows, sample_rows, prompt_rows, sample_rows,
            full((D_MODEL, D_MODEL)), full((1, D_MODEL)),
            full((N_EXPERTS, D_MODEL)), full((N_EXPERTS, 1)),
        ],
        out_specs=[row(D_MODEL), row(D_MODEL),
                   route, route, route, full((N_EXPERTS, LANES)),
                   pl.BlockSpec((1, N_EXPERTS, LANES), lambda i: (i, 0, 0))],
        out_shape=[
            jax.ShapeDtypeStruct((t, D_MODEL), f32),
            jax.ShapeDtypeStruct((t, D_MODEL), bf16),
            jax.ShapeDtypeStruct((ROUTE_ROWS, t), i32),
            jax.ShapeDtypeStruct((ROUTE_ROWS, t), f32),
            jax.ShapeDtypeStruct((ROUTE_ROWS, t), i32),
            jax.ShapeDtypeStruct((N_EXPERTS, LANES), i32),
            jax.ShapeDtypeStruct((nt, N_EXPERTS, LANES), i32),
        ],
        scratch_shapes=[pltpu.VMEM((N_EXPERTS, LANES), f32)],
        compiler_params=pltpu.CompilerParams(
            dimension_semantics=("arbitrary",), vmem_limit_bytes=VMEM_LIMIT),
        name="out_proj_router",
    )(xp, xs, o_p, o_s, wo, g, wr, br)


def _dispatch_kernel(bv_ref, nz_ref, tab_ref, ti_ref, rank_ref, xn_ref, xs_hbm, zeros, stage, sems, zsem,
                     *, n_blocks):
    @pl.when(pl.program_id(0) == 0)
    def _():
        zeros[...] = jnp.zeros_like(zeros)

        def fill(b, carry):
            @pl.when(bv_ref[b] < MOE_TILE)
            def _():
                r0 = pl.multiple_of(b * (MOE_TILE * SLABS), MOE_TILE * SLABS)
                pltpu.make_async_copy(zeros, xs_hbm.at[pl.ds(r0, MOE_TILE * SLABS)], zsem).start()
            return carry

        def drain(b, carry):
            pltpu.make_async_copy(zeros, xs_hbm.at[pl.ds(0, MOE_TILE * SLABS)], zsem).wait()
            return carry

        lax.fori_loop(0, n_blocks, fill, 0)
        lax.fori_loop(0, nz_ref[0], drain, 0)

    i = pl.program_id(0)
    nt = pl.num_programs(0)
    n = DISPATCH_TILE * TOP_K
    slot_rows = n * SLABS
    slot = i % 2
    base = pl.multiple_of(slot * slot_rows, slot_rows)

    def wait_rows(s):
        pltpu.make_async_copy(stage.at[pl.ds(0, slot_rows)], xs_hbm.at[pl.ds(0, slot_rows)], sems.at[s]).wait()

    ti = ti_ref[...]
    shift = jnp.zeros(ti.shape, i32)
    for e in range(N_EXPERTS):
        shift = jnp.where(ti == e, tab_ref[0, 0, RUN_SHIFT + e], shift)
    staged = rank_ref[...] + shift
    row = lax.broadcasted_iota(i32, (n, DISPATCH_TILE), 0)
    pick = row == staged[0:1, :]
    for k in range(1, TOP_K):
        pick = pick | (row == staged[k:k + 1, :])
    rows_sorted = _dot(pick.astype(bf16), xn_ref[...])

    for s in range(SLABS):
        stage[pl.ds(base + s, n, stride=SLABS), :] = rows_sorted[:, s * LANES:(s + 1) * LANES]

    for bit in range(RUN_BITS - 1, -1, -1):
        size = (1 << bit) * SLABS

        def issue(p, carry, bit=bit, size=size):
            src0 = pl.multiple_of(tab_ref[0, 0, RUN_SRC + bit * N_EXPERTS + p], SLABS)
            dst0 = pl.multiple_of(tab_ref[0, 0, RUN_DST + bit * N_EXPERTS + p], SLABS)
            pltpu.make_async_copy(stage.at[pl.ds(base + src0, size)], xs_hbm.at[pl.ds(dst0, size)],
                                  sems.at[slot]).start(priority=bit % 2)
            return carry

        lax.fori_loop(0, tab_ref[0, 0, RUN_COUNT + bit], issue, 0)

    @pl.when(i > 0)
    def _():
        wait_rows(1 - slot)

    @pl.when(i == nt - 1)
    def _():
        wait_rows(slot)


def _dispatch(block_valid, tile_table, top_i, rank, xn):
    t = xn.shape[0]
    nt = t // DISPATCH_TILE
    n_blocks = block_valid.shape[0]
    n_zero = jnp.sum((block_valid < MOE_TILE).astype(i32)).reshape(1)
    route = pl.BlockSpec((ROUTE_ROWS, DISPATCH_TILE), lambda i, bv, nz: (0, i))
    grid_spec = pltpu.PrefetchScalarGridSpec(
        num_scalar_prefetch=2,
        grid=(nt,),
        in_specs=[
            pl.BlockSpec((1, 1, RUN_TABLE), lambda i, bv, nz: (i, 0, 0), memory_space=pltpu.SMEM),
            route, route,
            pl.BlockSpec((DISPATCH_TILE, D_MODEL), lambda i, bv, nz: (i, 0)),
        ],
        out_specs=pl.BlockSpec(memory_space=pl.ANY),
        scratch_shapes=[
            pltpu.VMEM((MOE_TILE * SLABS, LANES), f32),
            pltpu.VMEM((2 * DISPATCH_TILE * TOP_K * SLABS, LANES), f32),
            pltpu.SemaphoreType.DMA((2,)),
            pltpu.SemaphoreType.DMA(()),
        ],
    )
    return pl.pallas_call(
        functools.partial(_dispatch_kernel, n_blocks=n_blocks),
        grid_spec=grid_spec,
        out_shape=jax.ShapeDtypeStruct((n_blocks * MOE_TILE * SLABS, LANES), f32),
        compiler_params=pltpu.CompilerParams(
            dimension_semantics=("arbitrary",), vmem_limit_bytes=VMEM_LIMIT),
        name="moe_dispatch",
    )(block_valid, n_zero, tile_table, top_i, rank, xn)


def _expert_kernel(be_ref, seg_ref, nxt_ref, bv_ref, nb_ref, x_ref, wgu_hbm, bgu_ref, wdn_hbm, bdn_ref, y_ref,
                   wgu_f32, wdn_f32, wgu_bf, wdn_bf, sems, *, ff_chunk, cast_rows):
    i = pl.program_id(0)

    @pl.when(i >= nb_ref[0])
    def _():
        y_ref[...] = jnp.zeros_like(y_ref)

    def fetch(e, slot):
        return (pltpu.make_async_copy(wgu_hbm.at[e], wgu_f32.at[slot], sems.at[slot, 0]),
                pltpu.make_async_copy(wdn_hbm.at[e], wdn_f32.at[slot], sems.at[slot, 1]))

    @pl.when((i < nb_ref[0]) & ((i == 0) | (be_ref[i] != be_ref[jnp.maximum(i - 1, 0)])))
    def _():
        slot = seg_ref[i] % 2

        @pl.when(i == 0)
        def _():
            for c in fetch(be_ref[0], 0):
                c.start()

        for c in fetch(be_ref[i], slot):
            c.wait()

        @pl.when(nxt_ref[i] >= 0)
        def _():
            for c in fetch(nxt_ref[i], 1 - slot):
                c.start()

        def cast(j, carry):
            r0 = pl.multiple_of(j * cast_rows, cast_rows)
            wgu_bf[pl.ds(r0, cast_rows), :] = wgu_f32[slot, pl.ds(r0, cast_rows), :].astype(bf16)
            wdn_bf[pl.ds(r0, cast_rows), :] = wdn_f32[slot, pl.ds(r0, cast_rows), :].astype(bf16)
            return carry
        lax.fori_loop(0, D_MODEL // cast_rows, cast, 0)

    def mlp(rows):
        xb = jnp.concatenate(
            [_load_slab_cols(x_ref, 0, rows, s).astype(bf16) for s in range(SLABS)], axis=1)
        hid = []
        for c0 in range(0, D_FF, ff_chunk):
            gate = _dot(xb, wgu_bf[:, c0:c0 + ff_chunk]) + bgu_ref[0, :, c0:c0 + ff_chunk]
            up = (_dot(xb, wgu_bf[:, D_FF + c0:D_FF + c0 + ff_chunk])
                  + bgu_ref[0, :, D_FF + c0:D_FF + c0 + ff_chunk])
            gate = jnp.minimum(gate, SWIGLU_LIMIT)
            up = jnp.clip(up, -SWIGLU_LIMIT, SWIGLU_LIMIT)
            hid.append(((up + 1.0) * (gate * _sigmoid(gate * SWIGLU_ALPHA))).astype(bf16))
        hid = jnp.concatenate(hid, axis=1)
        for n0 in range(0, D_MODEL, ff_chunk):
            y = _dot(hid, wdn_bf[:, n0:n0 + ff_chunk]) + bdn_ref[0, :, n0:n0 + ff_chunk]
            for s in range(n0 // LANES, (n0 + ff_chunk) // LANES):
                y_ref[pl.ds(s, rows, stride=SLABS), :] = y[:, s * LANES - n0:(s + 1) * LANES - n0]

    half = MOE_TILE // 2

    @pl.when((i < nb_ref[0]) & (bv_ref[i] > half))
    def _():
        mlp(MOE_TILE)

    @pl.when((i < nb_ref[0]) & (bv_ref[i] <= half))
    def _():
        mlp(half)
        y_ref[pl.ds(half * SLABS, half * SLABS), :] = jnp.zeros((half * SLABS, LANES), f32)


def _experts(block_e, block_seg, next_e, block_valid, n_used, xs, wgu, bgu, wdn, bdn):
    n_blocks = block_e.shape[0]
    grid_spec = pltpu.PrefetchScalarGridSpec(
        num_scalar_prefetch=5,
        grid=(n_blocks,),
        in_specs=[
            pl.BlockSpec((MOE_TILE * SLABS, LANES),
                         lambda i, be, sg, nx, bv, nb: (jnp.minimum(i, nb[0] - 1), 0)),
            pl.BlockSpec(memory_space=pl.ANY),
            pl.BlockSpec((1, 1, 2 * D_FF), lambda i, be, sg, nx, bv, nb: (be[i], 0, 0)),
            pl.BlockSpec(memory_space=pl.ANY),
            pl.BlockSpec((1, 1, D_MODEL), lambda i, be, sg, nx, bv, nb: (be[i], 0, 0)),
        ],
        out_specs=pl.BlockSpec((MOE_TILE * SLABS, LANES), lambda i, be, sg, nx, bv, nb: (i, 0)),
        scratch_shapes=[
            pltpu.VMEM((2, D_MODEL, 2 * D_FF), f32),
            pltpu.VMEM((2, D_FF, D_MODEL), f32),
            pltpu.VMEM((D_MODEL, 2 * D_FF), bf16),
            pltpu.VMEM((D_FF, D_MODEL), bf16),
            pltpu.SemaphoreType.DMA((2, 2)),
        ],
    )
    return pl.pallas_call(
        functools.partial(_expert_kernel, ff_chunk=256, cast_rows=128),
        grid_spec=grid_spec,
        out_shape=jax.ShapeDtypeStruct((n_blocks * MOE_TILE * SLABS, LANES), f32),
        compiler_params=pltpu.CompilerParams(
            dimension_semantics=("arbitrary",), vmem_limit_bytes=VMEM_LIMIT),
        name="moe_experts",
    )(block_e, block_seg, next_e, block_valid, n_used, xs, wgu, bgu, wdn, bdn)


def _combine_kernel(pos_cur, pos_nxt, ys_hbm, h_ref, tg_ref, pp_ref, ps_ref, gple_ref, wpg_ref, wpp_ref,
                    gfin_ref, yp_ref, ys_ref, rows, sems, *, n_prompt_tiles, issue_unroll):
    i = pl.program_id(0)
    nt = pl.num_programs(0)
    n = COMBINE_TILE * TOP_K
    slot_rows = n * SLABS

    def row_copy(pos_ref, slot_base, r, slot):
        src0 = pl.multiple_of(pos_ref[0, 0, r], SLABS)
        dst0 = pl.multiple_of(slot_base + r * SLABS, SLABS)
        return pltpu.make_async_copy(ys_hbm.at[pl.ds(src0, SLABS)], rows.at[pl.ds(dst0, SLABS)],
                                     sems.at[slot])

    def wait_slot(slot):
        pltpu.make_async_copy(ys_hbm.at[pl.ds(0, slot_rows)],
                              rows.at[pl.ds(pl.multiple_of(slot * slot_rows, slot_rows), slot_rows)],
                              sems.at[slot]).wait()

    def gather(pos_ref, slot):
        slot_base = slot * slot_rows

        def start(j, carry):
            for u in range(issue_unroll):
                row_copy(pos_ref, slot_base, issue_unroll * j + u, slot).start(priority=u % 2)
            return carry
        lax.fori_loop(0, n // issue_unroll, start, 0)

    @pl.when(i == 0)
    def _():
        gather(pos_cur, 0)

    @pl.when(i + 1 < nt)
    def _():
        gather(pos_nxt, (i + 1) % 2)

    slot = i % 2
    base = pl.multiple_of(slot * slot_rows, slot_rows)
    wait_slot(slot)

    tg = tg_ref[...]
    gate_k = [jnp.broadcast_to(tg[:, k:k + 1], (COMBINE_TILE, LANES)) for k in range(TOP_K)]
    cols = []
    for s in range(SLABS):
        moe = jnp.zeros((COMBINE_TILE, LANES), f32)
        for k in range(TOP_K):
            moe = moe + gate_k[k] * _load_slab_cols(
                rows, base + k * COMBINE_TILE * SLABS, COMBINE_TILE, s)
        cols.append(h_ref[:, s * LANES:(s + 1) * LANES] + moe)
    h = jnp.concatenate(cols, axis=1)
    gate = _sigmoid(_dot(_rms(h, gple_ref[...]).astype(bf16), wpg_ref[...]))
    p = _select_rows(i < n_prompt_tiles, pp_ref, ps_ref).astype(bf16)
    h = h + gate * _dot(p, wpp_ref[...])
    y = _rms(h, gfin_ref[...])

    @pl.when(i < n_prompt_tiles)
    def _():
        yp_ref[...] = y

    @pl.when(i >= n_prompt_tiles)
    def _():
        ys_ref[...] = y


def _combine(pos_tiles, ys, h1, tg, pp, ps, gple, wpg, wpp, gfin):
    t = h1.shape[0]
    nt = t // COMBINE_TILE
    npt = pp.shape[0] // COMBINE_TILE
    n = COMBINE_TILE * TOP_K
    row = lambda w: pl.BlockSpec((COMBINE_TILE, w), lambda i: (i, 0))
    full = lambda shape: pl.BlockSpec(shape, lambda i: (0,) * len(shape))
    return pl.pallas_call(
        functools.partial(_combine_kernel, n_prompt_tiles=npt, issue_unroll=16),
        grid=(nt,),
        in_specs=[
            pl.BlockSpec((1, 1, n), lambda i: (i, 0, 0), memory_space=pltpu.SMEM),
            pl.BlockSpec((1, 1, n), lambda i: (jnp.minimum(i + 1, nt - 1), 0, 0), memory_space=pltpu.SMEM),
            pl.BlockSpec(memory_space=pl.ANY),
            row(D_MODEL), row(LANES),
            pl.BlockSpec((COMBINE_TILE, PLE_DIM), lambda i: (jnp.minimum(i, npt - 1), 0)),
            pl.BlockSpec((COMBINE_TILE, PLE_DIM), lambda i: (jnp.maximum(i - npt, 0), 0)),
            full((1, D_MODEL)), full((D_MODEL, D_MODEL)), full((PLE_DIM, D_MODEL)),
            full((1, D_MODEL)),
        ],
        out_specs=[
            pl.BlockSpec((COMBINE_TILE, D_MODEL), lambda i: (jnp.minimum(i, npt - 1), 0)),
            pl.BlockSpec((COMBINE_TILE, D_MODEL), lambda i: (jnp.maximum(i - npt, 0), 0)),
        ],
        out_shape=[
            jax.ShapeDtypeStruct((pp.shape[0], D_MODEL), f32),
            jax.ShapeDtypeStruct((ps.shape[0], D_MODEL), f32),
        ],
        scratch_shapes=[
            pltpu.VMEM((2 * n * SLABS, LANES), f32),
            pltpu.SemaphoreType.DMA((2,)),
        ],
        compiler_params=pltpu.CompilerParams(
            dimension_semantics=("arbitrary",), vmem_limit_bytes=VMEM_LIMIT),
        name="moe_combine_ple",
    )(pos_tiles, pos_tiles, ys, h1, tg, pp, ps, gple, wpg, wpp, gfin)


def _pos_kernel(pstart_ref, ti_ref, rank_ref, pos_ref):
    ti = ti_ref[...]
    seg = jnp.zeros(ti.shape, i32)
    for e in range(N_EXPERTS):
        seg = jnp.where(ti == e, pstart_ref[e], seg)
    pos = (rank_ref[...] + seg) * SLABS
    for j in range(pos_ref.shape[0]):
        for k in range(TOP_K):
            pos_ref[j:j + 1, k * DISPATCH_TILE:(k + 1) * DISPATCH_TILE] = (
                pos[k:k + 1, j * DISPATCH_TILE:(j + 1) * DISPATCH_TILE])


def _routing_tables(top_i, rank, counts, part_counts, n_blocks):
    padded = (counts + MOE_TILE - 1) // MOE_TILE * MOE_TILE
    pend = jnp.cumsum(padded)
    pstart = pend - padded
    block_rows = jnp.arange(n_blocks, dtype=i32) * MOE_TILE
    block_e = jnp.minimum(
        jnp.sum((block_rows[:, None] >= pend[None, :]).astype(i32), axis=1), N_EXPERTS - 1)
    experts = jnp.arange(N_EXPERTS, dtype=i32)
    of_block = lambda table: jnp.sum(jnp.where(block_e[:, None] == experts[None, :], table[None, :], 0), axis=1)
    block_valid = jnp.clip(of_block(pstart + counts) - block_rows, 0, MOE_TILE)
    n_used = pend[-1:] // MOE_TILE
    owns = padded > 0
    later = owns[None, :] & (experts[None, :] > experts[:, None])
    next_of_e = jnp.where(jnp.any(later, axis=1), jnp.argmax(later, axis=1), -1)
    block_seg = of_block(jnp.cumsum(owns.astype(i32)) - 1)
    next_e = of_block(next_of_e.astype(i32))

    nt = top_i.shape[1] // DISPATCH_TILE
    pos_tiles = pl.pallas_call(
        _pos_kernel,
        in_specs=[pl.BlockSpec(memory_space=pltpu.SMEM),
                  pl.BlockSpec(memory_space=pltpu.VMEM), pl.BlockSpec(memory_space=pltpu.VMEM)],
        out_specs=pl.BlockSpec(memory_space=pltpu.VMEM),
        out_shape=jax.ShapeDtypeStruct((nt, TOP_K * DISPATCH_TILE), i32),
        name="route_rows",
    )(pstart.astype(i32), top_i, rank).reshape(nt, 1, TOP_K * DISPATCH_TILE)
    tile_n = part_counts[:, :, :TOKEN_TILE // DISPATCH_TILE].transpose(0, 2, 1).reshape(nt, N_EXPERTS)
    seen = jnp.cumsum(tile_n, axis=0) - tile_n
    tile_src = jnp.cumsum(tile_n, axis=1) - tile_n
    tile_dst = pstart[None, :] + seen
    bits = jnp.arange(RUN_BITS, dtype=i32)[None, :, None]
    n3 = tile_n[:, None, :]
    has_piece = ((n3 >> bits) & 1) == 1
    covered = (n3 >> (bits + 1)) << (bits + 1)
    place = jnp.cumsum(has_piece.astype(i32), axis=2) - 1
    listed = has_piece[..., None] & (place[..., None] == experts[None, None, None, :])
    compact = lambda v: jnp.sum(jnp.where(listed, v[..., None], 0), axis=2)
    piece_src = compact((tile_src[:, None, :] + covered) * SLABS)
    piece_dst = compact((tile_dst[:, None, :] + covered) * SLABS)
    piece_cnt = jnp.sum(has_piece.astype(i32), axis=2)
    tile_table = jnp.concatenate(
        [tile_src - seen, jnp.pad(piece_cnt, ((0, 0), (0, N_EXPERTS - RUN_BITS))),
         piece_src.reshape(nt, -1), piece_dst.reshape(nt, -1)], axis=1)
    tables = (block_e, block_seg, next_e, block_valid, n_used)
    return (pos_tiles, tile_table.reshape(nt, 1, RUN_TABLE).astype(i32)) + tuple(
        t.astype(i32) for t in tables)


def kernel(x_prompt, x_sample, p_prompt, p_sample, state_gla, state_hgrn, g_mix, w_in, w_gk2, b_gk2,
           gn_gla, lb_logits, gn_hgrn, w_out, g_ffn, w_router, b_router, w_gu, b_gu, w_dn, b_dn,
           g_ple, w_ple_gate, w_ple_proj, g_final):
    batch, seq, _ = x_prompt.shape
    dec_batch, dec_seq, _ = x_sample.shape
    tp, ts = batch * seq, dec_batch * dec_seq
    xp = x_prompt.reshape(tp, D_MODEL)
    xs = x_sample.reshape(ts, D_MODEL)

    wgk = jnp.concatenate(
        [w_gk2[0], jnp.zeros((R1_PAD - GLA_LOWRANK, GLA_KDIM), f32)], axis=0).astype(bf16)
    wr = w_router[0].T.astype(bf16)
    br = b_router[0].reshape(N_EXPERTS, 1)

    z = _in_proj(xp, xs, g_mix[0].reshape(1, D_MODEL), w_in[0])

    mixer_params = (wgk, b_gk2[0].reshape(1, GLA_KDIM), gn_gla[0].reshape(1, GLA_DV),
                    lb_logits[0:2], gn_hgrn[0].reshape(1, HGRN_DV))
    o_p, new_gla_p, new_hgrn_p = _mixer_prompt(
        z, *mixer_params, batch=batch, seq=seq, rows_per_step=512, block_rows=256)
    z_s = jnp.pad(z[tp:].reshape(dec_batch, dec_seq, Z_W), ((0, 0), (0, SAMPLE_PAD_T - dec_seq), (0, 0)))
    o_s, new_gla_s, new_hgrn_s = _mixer_sample(
        z_s.reshape(dec_batch * SAMPLE_PAD_T, Z_W), state_gla, state_hgrn, *mixer_params,
        steps=dec_seq, batch_block=8)
    o_s = o_s.reshape(dec_batch, SAMPLE_PAD_T, D_MODEL)[:, :dec_seq].reshape(ts, D_MODEL).astype(bf16)

    h1, xn, top_i, top_g, rank, counts, part_counts = _out_proj(
        xp, xs, o_p, o_s, w_out[0].astype(bf16), g_ffn[0].reshape(1, D_MODEL), wr, br)

    t = tp + ts
    n_blocks = -(-(t * TOP_K) // MOE_TILE) + N_EXPERTS
    pos_tiles, tile_table, block_e, block_seg, next_e, block_valid, n_used = _routing_tables(
        top_i, rank, counts[:, 0], part_counts, n_blocks)
    x_sorted = _dispatch(block_valid, tile_table, top_i, rank, xn)
    y_sorted = _experts(block_e, block_seg, next_e, block_valid, n_used, x_sorted,
                        w_gu[0], b_gu[0].reshape(N_EXPERTS, 1, 2 * D_FF),
                        w_dn[0], b_dn[0].reshape(N_EXPERTS, 1, D_MODEL))

    gates = jnp.pad(top_g.T, ((0, 0), (0, LANES - ROUTE_ROWS)))
    y_p, y_s = _combine(pos_tiles, y_sorted, h1, gates, p_prompt[0].reshape(tp, PLE_DIM),
                        p_sample[0].reshape(ts, PLE_DIM), g_ple[0].reshape(1, D_MODEL),
                        w_ple_gate[0].astype(bf16), w_ple_proj[0].astype(bf16), g_final.reshape(1, D_MODEL))

    return (y_p.reshape(batch, seq, D_MODEL), y_s.reshape(dec_batch, dec_seq, D_MODEL),
            new_gla_p, new_hgrn_p, new_gla_s, new_hgrn_s)
```

```python
import functools

import jax
import jax.numpy as jnp
from jax import lax
from jax.experimental import pallas as pl
from jax.experimental.pallas import tpu as pltpu

f32 = jnp.float32
bf16 = jnp.bfloat16
i32 = jnp.int32

D_MODEL = 1024
GLA_HEADS = 4
GLA_DK = 64
GLA_DV = 128
GLA_KDIM = GLA_HEADS * GLA_DK
GLA_WIDTH = GLA_HEADS * GLA_DV
GLA_LOWRANK = 16
GLA_NORMALIZER = 16.0
HGRN_HEADS = 4
HGRN_DK = 128
HGRN_DV = 128
HGRN_FDIM = HGRN_HEADS * HGRN_DK
HGRN_WIDTH = HGRN_HEADS * HGRN_DV
CHUNK = 64
N_EXPERTS = 32
TOP_K = 4
D_FF = 1024
SWIGLU_LIMIT = 7.0
SWIGLU_ALPHA = 1.702
PLE_DIM = 256
EPS = 1e-6

LANES = 128
SAMPLE_PAD_T = 8
SLABS = D_MODEL // LANES
ROUTE_ROWS = 8
R1_PAD = LANES
Z_Q1 = 0
Z_K1 = Z_Q1 + GLA_KDIM
Z_V1 = Z_K1 + GLA_KDIM
Z_G1 = Z_V1 + GLA_WIDTH
Z_R1 = Z_G1 + GLA_WIDTH
Z_Q2 = Z_R1 + R1_PAD
Z_F2 = Z_Q2 + HGRN_FDIM
Z_I2 = Z_F2 + HGRN_FDIM
Z_G2 = Z_I2 + HGRN_WIDTH
Z_W = Z_G2 + HGRN_WIDTH

TOKEN_TILE = 512
MOE_TILE = 512
DISPATCH_TILE = 256
COMBINE_TILE = DISPATCH_TILE
RUN_BITS = DISPATCH_TILE.bit_length()
RUN_SHIFT = 0
RUN_COUNT = RUN_SHIFT + N_EXPERTS
RUN_SRC = RUN_COUNT + N_EXPERTS
RUN_DST = RUN_SRC + RUN_BITS * N_EXPERTS
RUN_TABLE = RUN_DST + RUN_BITS * N_EXPERTS
VMEM_LIMIT = 56 * 1024 * 1024


def _rms(x, g):
    return x * lax.rsqrt(jnp.mean(x * x, axis=-1, keepdims=True) + EPS) * g


def _sigmoid(x):
    return 0.5 * jnp.tanh(0.5 * x) + 0.5


def _dot(a, b):
    return jnp.dot(a, b, preferred_element_type=f32)


def _dot_nt(a, b):
    return lax.dot_general(a, b, (((1,), (1,)), ((), ())), preferred_element_type=f32)


def _dot_tn(a, b):
    return lax.dot_general(a, b, (((0,), (0,)), ((), ())), preferred_element_type=f32)


def _cumsum_rows(tri, x):
    hi = x.astype(bf16)
    r1 = x - hi.astype(f32)
    mid = r1.astype(bf16)
    lo = (r1 - mid.astype(f32)).astype(bf16)
    return _dot(tri, hi) + _dot(tri, mid) + _dot(tri, lo)


def _select_rows(is_prompt, a_ref, b_ref):
    return jnp.where(is_prompt, a_ref[...], b_ref[...])


def _inproj_kernel(xp_ref, xs_ref, g_ref, w_hbm, z_ref, w_f32, w_bf, sem,
                   *, n_prompt_tiles, col_chunk, cast_rows):
    @pl.when(pl.program_id(0) == 0)
    def _():
        copy = pltpu.make_async_copy(w_hbm, w_f32, sem)
        copy.start()
        copy.wait()
        r1_end = Z_R1 + GLA_LOWRANK

        def cast(j, carry):
            rows = pl.ds(pl.multiple_of(j * cast_rows, cast_rows), cast_rows)
            w_bf[rows, 0:Z_R1] = w_f32[rows, 0:Z_R1].astype(bf16)
            w_bf[rows, Z_R1:Z_Q2] = jnp.concatenate(
                [w_f32[rows, Z_R1:r1_end], jnp.zeros((cast_rows, R1_PAD - GLA_LOWRANK), f32)],
                axis=1).astype(bf16)
            w_bf[rows, Z_Q2:Z_W] = w_f32[rows, r1_end:].astype(bf16)
            return carry
        lax.fori_loop(0, D_MODEL // cast_rows, cast, 0)

    is_prompt = pl.program_id(0) < n_prompt_tiles
    x = _select_rows(is_prompt, xp_ref, xs_ref)
    a = _rms(x, g_ref[...]).astype(bf16)
    for c0 in range(0, Z_W, col_chunk):
        c1 = min(c0 + col_chunk, Z_W)
        z_ref[:, c0:c1] = _dot(a, w_bf[:, c0:c1]).astype(bf16)


def _in_proj(xp, xs, g, w):
    tp, ts = xp.shape[0], xs.shape[0]
    npt = tp // TOKEN_TILE
    nt = npt + ts // TOKEN_TILE
    return pl.pallas_call(
        functools.partial(_inproj_kernel, n_prompt_tiles=npt, col_chunk=512, cast_rows=64),
        grid=(nt,),
        in_specs=[
            pl.BlockSpec((TOKEN_TILE, D_MODEL), lambda i: (jnp.minimum(i, npt - 1), 0)),
            pl.BlockSpec((TOKEN_TILE, D_MODEL), lambda i: (jnp.maximum(i - npt, 0), 0)),
            pl.BlockSpec((1, D_MODEL), lambda i: (0, 0)),
            pl.BlockSpec(memory_space=pl.ANY),
        ],
        out_specs=pl.BlockSpec((TOKEN_TILE, Z_W), lambda i: (i, 0)),
        out_shape=jax.ShapeDtypeStruct((tp + ts, Z_W), bf16),
        scratch_shapes=[
            pltpu.VMEM(w.shape, f32),
            pltpu.VMEM((D_MODEL, Z_W), bf16),
            pltpu.SemaphoreType.DMA(()),
        ],
        compiler_params=pltpu.CompilerParams(
            dimension_semantics=("arbitrary",), vmem_limit_bytes=VMEM_LIMIT),
        name="in_proj",
    )(xp, xs, g, w)


def _block_consts(r, c):
    shift = c.bit_length() - 1
    row = lax.broadcasted_iota(i32, (r, r), 0)
    col = lax.broadcasted_iota(i32, (r, r), 1)
    causal = (lax.shift_right_logical(row, shift) == lax.shift_right_logical(col, shift)) & (row >= col)
    return causal.astype(bf16), causal


def _subchunk_row(x, c, idx):
    parts = [jnp.broadcast_to(x[j * c + idx:j * c + idx + 1, :], (c, x.shape[1]))
             for j in range(x.shape[0] // c)]
    return jnp.concatenate(parts, axis=0)


def _gated_block(q, k, v, log_a, states, *, c, scale, ref_row, tri, causal, heads_per_group, scan):
    r = q.shape[0]
    nsub = r // c
    groups = q.shape[1] // LANES
    b = _cumsum_rows(tri, log_a)
    b_ref = _subchunk_row(b, c, ref_row)
    b_last = _subchunk_row(b, c, c - 1)
    qs = q * scale
    q_r = qs * jnp.exp(b - b_ref)
    k_r = (k * jnp.exp(b_ref - b)).astype(bf16)
    k_end = k * jnp.exp(b_last - b)
    q_b = qs * jnp.exp(b)
    decay = jnp.exp(b_last)
    vb = v.astype(bf16)
    lane = lax.broadcasted_iota(i32, (r, LANES), 1)
    width = LANES // heads_per_group
    outs, new_states = [], []
    for g in range(groups):
        gl = slice(g * LANES, (g + 1) * LANES)
        masks = [None] if heads_per_group == 1 else [
            (lane >= j * width) & (lane < (j + 1) * width) for j in range(heads_per_group)]
        pick = lambda x, m: x if m is None else jnp.where(m, x, 0.0)
        q_r_h = [pick(q_r[:, gl], m).astype(bf16) for m in masks]
        q_b_h = [pick(q_b[:, gl], m).astype(bf16) for m in masks]
        k_end_h = [pick(k_end[:, gl], m).astype(bf16) for m in masks]
        v_h = [vb[:, (g * heads_per_group + j) * LANES:(g * heads_per_group + j + 1) * LANES]
               for j in range(heads_per_group)]
        o_intra = []
        for j in range(heads_per_group):
            att = _dot_nt(q_r_h[j], k_r[:, gl])
            att = jnp.where(causal, att, 0.0).astype(bf16)
            o_intra.append(_dot(att, v_h[j]))
        upd = []
        for s in range(nsub):
            rs = slice(s * c, (s + 1) * c)
            u = _dot_tn(v_h[0][rs], k_end_h[0][rs])
            for j in range(1, heads_per_group):
                u = u + _dot_tn(v_h[j][rs], k_end_h[j][rs])
            upd.append(u)
        o_inter = [[] for _ in range(heads_per_group)]
        st = states[g] if scan else None
        group_states = []
        for s in range(nsub):
            rs = slice(s * c, (s + 1) * c)
            if not scan:
                st = states[s][g]
            st_b = st.astype(bf16)
            for j in range(heads_per_group):
                o_inter[j].append(_dot_nt(q_b_h[j][rs], st_b))
            st = st * decay[s * c:s * c + 1, gl] + upd[s]
            if not scan:
                group_states.append(st)
        for j in range(heads_per_group):
            outs.append(o_intra[j] + jnp.concatenate(o_inter[j], axis=0))
        new_states.append(st if scan else group_states)
    if not scan:
        new_states = [[new_states[g][s] for g in range(groups)] for s in range(nsub)]
    return outs, new_states


def _head_norm(o, g):
    return o * lax.rsqrt(jnp.mean(o * o, axis=-1, keepdims=True) + EPS) * g


def _mixer_block(zb, st_gla, st_hgrn, wgk, bgk, gng, lb, gnh, *, c, ref_row, tri, causal, valid, scan):
    z = zb.astype(f32)
    gk = _dot(zb[:, Z_R1:Z_R1 + R1_PAD], wgk) + bgk
    log_a = (jnp.minimum(gk, 0.0) - jnp.log(1.0 + jnp.exp(-jnp.abs(gk)))) / GLA_NORMALIZER
    k1 = z[:, Z_K1:Z_K1 + GLA_KDIM]
    if valid is not None:
        log_a = jnp.where(valid, log_a, 0.0)
        k1 = jnp.where(valid, k1, 0.0)
    o1, st_gla = _gated_block(
        z[:, Z_Q1:Z_Q1 + GLA_KDIM], k1, z[:, Z_V1:Z_V1 + GLA_WIDTH], log_a, st_gla,
        c=c, scale=GLA_DK ** -0.5, ref_row=ref_row, tri=tri, causal=causal,
        heads_per_group=LANES // GLA_DK, scan=scan)
    forget = lb + (1.0 - lb) * _sigmoid(z[:, Z_F2:Z_F2 + HGRN_FDIM])
    k2 = 1.0 - forget
    log_f = jnp.log(forget)
    if valid is not None:
        log_f = jnp.where(valid, log_f, 0.0)
        k2 = jnp.where(valid, k2, 0.0)
    q2 = z[:, Z_Q2:Z_Q2 + HGRN_FDIM]
    i2 = z[:, Z_I2:Z_I2 + HGRN_WIDTH]
    o2, st_hgrn = _gated_block(
        q2 * _sigmoid(q2), k2, i2 * _sigmoid(i2), log_f, st_hgrn,
        c=c, scale=1.0, ref_row=ref_row, tri=tri, causal=causal, heads_per_group=1, scan=scan)
    cols = []
    for h in range(GLA_HEADS):
        g1 = z[:, Z_G1 + h * GLA_DV:Z_G1 + (h + 1) * GLA_DV]
        cols.append(_head_norm(o1[h], gng) * (g1 * _sigmoid(g1)))
    for h in range(HGRN_HEADS):
        g2 = z[:, Z_G2 + h * HGRN_DV:Z_G2 + (h + 1) * HGRN_DV]
        cols.append(_head_norm(o2[h], gnh) * _sigmoid(g2))
    return jnp.concatenate(cols, axis=1), st_gla, st_hgrn


def _lower_bound(lbl_ref):
    l = lbl_ref[...]
    m = jnp.max(l, axis=0, keepdims=True)
    e = jnp.exp(l - m)
    return e[0:1, :] / jnp.sum(e, axis=0, keepdims=True)


GLA_GROUPS = GLA_KDIM // LANES


def _gla_state_t(s_ref_heads):
    return s_ref_heads.reshape(LANES, GLA_DV).T


def _gla_state_from_t(st):
    return st.T.reshape(LANES // GLA_DK, GLA_DK, GLA_DV)


def _mixer_prompt_kernel(z_ref, wgk_ref, bgk_ref, gng_ref, lbl_ref, gnh_ref,
                         o_ref, sg_out, sh_out, sg_t, sh_t, *, block_rows):
    j = pl.program_id(1)

    @pl.when(j == 0)
    def _():
        sg_t[...] = jnp.zeros_like(sg_t)
        sh_t[...] = jnp.zeros_like(sh_t)

    tri, causal = _block_consts(block_rows, CHUNK)
    lb = _lower_bound(lbl_ref)
    wgk, bgk, gng, gnh = wgk_ref[...], bgk_ref[...], gng_ref[...], gnh_ref[...]

    def body(blk, carry):
        r0 = pl.multiple_of(blk * block_rows, block_rows)
        o, st_g, st_h = _mixer_block(
            z_ref[pl.ds(r0, block_rows), :], [sg_t[g] for g in range(GLA_GROUPS)],
            [sh_t[h] for h in range(HGRN_HEADS)], wgk, bgk, gng, lb, gnh,
            c=CHUNK, ref_row=CHUNK // 2, tri=tri, causal=causal, valid=None, scan=True)
        o_ref[pl.ds(r0, block_rows), :] = o.astype(bf16)
        for g in range(GLA_GROUPS):
            sg_t[g] = st_g[g]
        for h in range(HGRN_HEADS):
            sh_t[h] = st_h[h]
        return carry

    lax.fori_loop(0, z_ref.shape[0] // block_rows, body, 0)

    @pl.when(j == pl.num_programs(1) - 1)
    def _():
        per = LANES // GLA_DK
        for g in range(GLA_GROUPS):
            sg_out[0, 0, g * per:(g + 1) * per] = _gla_state_from_t(sg_t[g])
        for h in range(HGRN_HEADS):
            sh_out[0, 0, h] = sh_t[h].T


def _mixer_prompt(z, wgk, bgk, gng, lbl, gnh, *, batch, seq, rows_per_step, block_rows):
    nj = seq // rows_per_step
    full = lambda shape: pl.BlockSpec(shape, lambda b, j: (0,) * len(shape))
    return pl.pallas_call(
        functools.partial(_mixer_prompt_kernel, block_rows=block_rows),
        grid=(batch, nj),
        in_specs=[
            pl.BlockSpec((rows_per_step, Z_W), lambda b, j: (b * nj + j, 0)),
            full((R1_PAD, GLA_KDIM)), full((1, GLA_KDIM)), full((1, GLA_DV)),
            full((2, HGRN_FDIM)), full((1, HGRN_DV)),
        ],
        out_specs=[
            pl.BlockSpec((rows_per_step, D_MODEL), lambda b, j: (b * nj + j, 0)),
            pl.BlockSpec((1, 1, GLA_HEADS, GLA_DK, GLA_DV), lambda b, j: (0, b, 0, 0, 0)),
            pl.BlockSpec((1, 1, HGRN_HEADS, HGRN_DK, HGRN_DV), lambda b, j: (0, b, 0, 0, 0)),
        ],
        out_shape=[
            jax.ShapeDtypeStruct((batch * seq, D_MODEL), bf16),
            jax.ShapeDtypeStruct((1, batch, GLA_HEADS, GLA_DK, GLA_DV), f32),
            jax.ShapeDtypeStruct((1, batch, HGRN_HEADS, HGRN_DK, HGRN_DV), f32),
        ],
        scratch_shapes=[
            pltpu.VMEM((GLA_GROUPS, GLA_DV, LANES), f32),
            pltpu.VMEM((HGRN_HEADS, HGRN_DV, HGRN_DK), f32),
        ],
        compiler_params=pltpu.CompilerParams(
            dimension_semantics=("arbitrary", "arbitrary"), vmem_limit_bytes=VMEM_LIMIT),
        name="mixer_prompt",
    )(z, wgk, bgk, gng, lbl, gnh)


def _mixer_sample_kernel(z_ref, sg_in, sh_in, wgk_ref, bgk_ref, gng_ref, lbl_ref, gnh_ref,
                         o_ref, sg_out, sh_out, *, batch_block, steps):
    rows = batch_block * SAMPLE_PAD_T
    tri, causal = _block_consts(rows, SAMPLE_PAD_T)
    step = lax.broadcasted_iota(i32, (rows, 1), 0) & (SAMPLE_PAD_T - 1)
    valid = step < steps
    lb = _lower_bound(lbl_ref)
    per = LANES // GLA_DK
    st_g = [[_gla_state_t(sg_in[0, s, g * per:(g + 1) * per]) for g in range(GLA_GROUPS)]
            for s in range(batch_block)]
    st_h = [[sh_in[0, s, h].T for h in range(HGRN_HEADS)] for s in range(batch_block)]
    o, st_g, st_h = _mixer_block(
        z_ref[...], st_g, st_h, wgk_ref[...], bgk_ref[...], gng_ref[...], lb, gnh_ref[...],
        c=SAMPLE_PAD_T, ref_row=steps // 2, tri=tri, causal=causal, valid=valid, scan=False)
    o_ref[...] = o
    for s in range(batch_block):
        for g in range(GLA_GROUPS):
            sg_out[0, s, g * per:(g + 1) * per] = _gla_state_from_t(st_g[s][g])
        for h in range(HGRN_HEADS):
            sh_out[0, s, h] = st_h[s][h].T


def _mixer_sample(z2, state_gla, state_hgrn, wgk, bgk, gng, lbl, gnh, *, steps, batch_block):
    batch = z2.shape[0] // SAMPLE_PAD_T
    rows = batch_block * SAMPLE_PAD_T
    full = lambda shape: pl.BlockSpec(shape, lambda i: (0,) * len(shape))
    sg_spec = pl.BlockSpec((1, batch_block, GLA_HEADS, GLA_DK, GLA_DV), lambda i: (0, i, 0, 0, 0))
    sh_spec = pl.BlockSpec((1, batch_block, HGRN_HEADS, HGRN_DK, HGRN_DV), lambda i: (0, i, 0, 0, 0))
    return pl.pallas_call(
        functools.partial(_mixer_sample_kernel, batch_block=batch_block, steps=steps),
        grid=(batch // batch_block,),
        in_specs=[
            pl.BlockSpec((rows, Z_W), lambda i: (i, 0)),
            sg_spec, sh_spec,
            full((R1_PAD, GLA_KDIM)), full((1, GLA_KDIM)), full((1, GLA_DV)),
            full((2, HGRN_FDIM)), full((1, HGRN_DV)),
        ],
        out_specs=[pl.BlockSpec((rows, D_MODEL), lambda i: (i, 0)), sg_spec, sh_spec],
        out_shape=[
            jax.ShapeDtypeStruct((batch * SAMPLE_PAD_T, D_MODEL), f32),
            jax.ShapeDtypeStruct(state_gla.shape, f32),
            jax.ShapeDtypeStruct(state_hgrn.shape, f32),
        ],
        compiler_params=pltpu.CompilerParams(
            dimension_semantics=("arbitrary",), vmem_limit_bytes=VMEM_LIMIT),
        name="mixer_sample",
    )(z2, state_gla, state_hgrn, wgk, bgk, gng, lbl, gnh)


def _load_slab_cols(ref, start, rows, s):
    return ref[pl.ds(start + s, rows, stride=SLABS), :]


def _outproj_kernel(xp_ref, xs_ref, op_ref, os_ref, wo_ref, g_ref, wr_ref, br_ref,
                    h_ref, xn_ref, ti_ref, tg_ref, rank_ref, cnt_ref, part_ref, run_cnt, *, n_prompt_tiles):
    i = pl.program_id(0)

    @pl.when(i == 0)
    def _():
        run_cnt[...] = jnp.zeros_like(run_cnt)

    x = _select_rows(i < n_prompt_tiles, xp_ref, xs_ref)
    o = _select_rows(i < n_prompt_tiles, op_ref, os_ref)
    h = x + _dot(o, wo_ref[...])
    h_ref[...] = h
    xn = _rms(h, g_ref[...])
    xn_ref[...] = xn.astype(bf16)
    l = _dot_nt(wr_ref[...], xn.astype(bf16)) + br_ref[...]
    expert = lax.broadcasted_iota(i32, l.shape, 0)
    neg = jnp.float32(-jnp.inf)
    vals, idxs = [], []
    for _ in range(TOP_K):
        m = jnp.max(l, axis=0, keepdims=True)
        idx = jnp.min(jnp.where(l == m, expert, N_EXPERTS), axis=0, keepdims=True)
        vals.append(m)
        idxs.append(idx)
        l = jnp.where(expert == idx, neg, l)
    es = [jnp.exp(v - vals[0]) for v in vals]
    denom = es[0] + es[1] + es[2] + es[3]

    onehot = jnp.zeros(l.shape, f32)
    for k in range(TOP_K):
        onehot = onehot + (expert == idxs[k]).astype(f32)
    n = l.shape[1]
    earlier = (lax.broadcasted_iota(i32, (n, n), 0) < lax.broadcasted_iota(i32, (n, n), 1)).astype(bf16)
    ahead = _dot(onehot.astype(bf16), earlier) + run_cnt[:, 0:1]
    run_cnt[...] = run_cnt[...] + jnp.sum(onehot, axis=1, keepdims=True)
    cnt_ref[...] = run_cnt[...].astype(i32)
    lane = lax.broadcasted_iota(i32, (N_EXPERTS, LANES), 1)
    parts = jnp.zeros((N_EXPERTS, LANES), f32)
    for p in range(n // DISPATCH_TILE):
        c = jnp.sum(onehot[:, p * DISPATCH_TILE:(p + 1) * DISPATCH_TILE], axis=1, keepdims=True)
        parts = jnp.where(lane == p, c, parts)
    part_ref[0] = parts.astype(i32)

    ranks = [jnp.sum(jnp.where(expert == idxs[k], ahead, 0.0), axis=0, keepdims=True).astype(i32)
             for k in range(TOP_K)]
    pad_i = jnp.zeros((ROUTE_ROWS - TOP_K, n), i32)
    ti_ref[...] = jnp.concatenate(idxs + [pad_i], axis=0)
    rank_ref[...] = jnp.concatenate(ranks + [pad_i], axis=0)
    gate_rows = jnp.concatenate([e / denom for e in es] + [jnp.zeros((LANES - TOP_K, n), f32)], axis=0)
    tg_ref[...] = gate_rows.T


def _out_proj(xp, xs, o_p, o_s, wo, g, wr, br):
    tp, ts = xp.shape[0], xs.shape[0]
    npt = tp // TOKEN_TILE
    nt = npt + ts // TOKEN_TILE
    t = tp + ts
    row = lambda w: pl.BlockSpec((TOKEN_TILE, w), lambda i: (i, 0))
    route = pl.BlockSpec((ROUTE_ROWS, TOKEN_TILE), lambda i: (0, i))
    full = lambda shape: pl.BlockSpec(shape, lambda i: (0,) * len(shape))
    prompt_rows = pl.BlockSpec((TOKEN_TILE, D_MODEL), lambda i: (jnp.minimum(i, npt - 1), 0))
    sample_rows = pl.BlockSpec((TOKEN_TILE, D_MODEL), lambda i: (jnp.maximum(i - npt, 0), 0))
    return pl.pallas_call(
        functools.partial(_outproj_kernel, n_prompt_tiles=npt),
        grid=(nt,),
        in_specs=[
            prompt_rows, sample_rows, prompt_rows, sample_rows,
            full((D_MODEL, D_MODEL)), full((1, D_MODEL)),
            full((N_EXPERTS, D_MODEL)), full((N_EXPERTS, 1)),
        ],
        out_specs=[row(D_MODEL), row(D_MODEL),
                   route, row(LANES), route, full((N_EXPERTS, LANES)),
                   pl.BlockSpec((1, N_EXPERTS, LANES), lambda i: (i, 0, 0))],
        out_shape=[
            jax.ShapeDtypeStruct((t, D_MODEL), f32),
            jax.ShapeDtypeStruct((t, D_MODEL), bf16),
            jax.ShapeDtypeStruct((ROUTE_ROWS, t), i32),
            jax.ShapeDtypeStruct((t, LANES), f32),
            jax.ShapeDtypeStruct((ROUTE_ROWS, t), i32),
            jax.ShapeDtypeStruct((N_EXPERTS, LANES), i32),
            jax.ShapeDtypeStruct((nt, N_EXPERTS, LANES), i32),
        ],
        scratch_shapes=[pltpu.VMEM((N_EXPERTS, LANES), f32)],
        compiler_params=pltpu.CompilerParams(
            dimension_semantics=("arbitrary",), vmem_limit_bytes=VMEM_LIMIT),
        name="out_proj_router",
    )(xp, xs, o_p, o_s, wo, g, wr, br)


def _dispatch_kernel(bv_ref, nz_ref, tab_ref, ti_ref, rank_ref, xn_ref, xs_hbm, zeros, stage, sems, zsem,
                     *, n_blocks):
    @pl.when(pl.program_id(0) == 0)
    def _():
        zeros[...] = jnp.zeros_like(zeros)

        def fill(b, carry):
            @pl.when(bv_ref[b] < MOE_TILE)
            def _():
                r0 = pl.multiple_of(b * (MOE_TILE * SLABS), MOE_TILE * SLABS)
                pltpu.make_async_copy(zeros, xs_hbm.at[pl.ds(r0, MOE_TILE * SLABS)], zsem).start()
            return carry

        def drain(b, carry):
            pltpu.make_async_copy(zeros, xs_hbm.at[pl.ds(0, MOE_TILE * SLABS)], zsem).wait()
            return carry

        lax.fori_loop(0, n_blocks, fill, 0)
        lax.fori_loop(0, nz_ref[0], drain, 0)

    i = pl.program_id(0)
    nt = pl.num_programs(0)
    n = DISPATCH_TILE * TOP_K
    slot_rows = n * SLABS
    slot = i % 2
    base = pl.multiple_of(slot * slot_rows, slot_rows)

    def wait_rows(s):
        pltpu.make_async_copy(stage.at[pl.ds(0, slot_rows)], xs_hbm.at[pl.ds(0, slot_rows)], sems.at[s]).wait()

    ti = ti_ref[...]
    shift = jnp.zeros(ti.shape, i32)
    for e in range(N_EXPERTS):
        shift = jnp.where(ti == e, tab_ref[0, 0, RUN_SHIFT + e], shift)
    staged = rank_ref[...] + shift
    row = lax.broadcasted_iota(i32, (n, DISPATCH_TILE), 0)
    pick = row == staged[0:1, :]
    for k in range(1, TOP_K):
        pick = pick | (row == staged[k:k + 1, :])
    rows_sorted = _dot(pick.astype(bf16), xn_ref[...])

    for s in range(SLABS):
        stage[pl.ds(base + s, n, stride=SLABS), :] = rows_sorted[:, s * LANES:(s + 1) * LANES]

    for bit in range(RUN_BITS - 1, -1, -1):
        size = (1 << bit) * SLABS

        def issue(p, carry, bit=bit, size=size):
            src0 = pl.multiple_of(tab_ref[0, 0, RUN_SRC + bit * N_EXPERTS + p], SLABS)
            dst0 = pl.multiple_of(tab_ref[0, 0, RUN_DST + bit * N_EXPERTS + p], SLABS)
            pltpu.make_async_copy(stage.at[pl.ds(base + src0, size)], xs_hbm.at[pl.ds(dst0, size)],
                                  sems.at[slot]).start(priority=bit % 2)
            return carry

        lax.fori_loop(0, tab_ref[0, 0, RUN_COUNT + bit], issue, 0)

    @pl.when(i > 0)
    def _():
        wait_rows(1 - slot)

    @pl.when(i == nt - 1)
    def _():
        wait_rows(slot)


def _dispatch(block_valid, tile_table, top_i, rank, xn):
    t = xn.shape[0]
    nt = t // DISPATCH_TILE
    n_blocks = block_valid.shape[0]
    n_zero = jnp.sum((block_valid < MOE_TILE).astype(i32)).reshape(1)
    route = pl.BlockSpec((ROUTE_ROWS, DISPATCH_TILE), lambda i, bv, nz: (0, i))
    grid_spec = pltpu.PrefetchScalarGridSpec(
        num_scalar_prefetch=2,
        grid=(nt,),
        in_specs=[
            pl.BlockSpec((1, 1, RUN_TABLE), lambda i, bv, nz: (i, 0, 0), memory_space=pltpu.SMEM),
            route, route,
            pl.BlockSpec((DISPATCH_TILE, D_MODEL), lambda i, bv, nz: (i, 0)),
        ],
        out_specs=pl.BlockSpec(memory_space=pl.ANY),
        scratch_shapes=[
            pltpu.VMEM((MOE_TILE * SLABS, LANES), f32),
            pltpu.VMEM((2 * DISPATCH_TILE * TOP_K * SLABS, LANES), f32),
            pltpu.SemaphoreType.DMA((2,)),
            pltpu.SemaphoreType.DMA(()),
        ],
    )
    return pl.pallas_call(
        functools.partial(_dispatch_kernel, n_blocks=n_blocks),
        grid_spec=grid_spec,
        out_shape=jax.ShapeDtypeStruct((n_blocks * MOE_TILE * SLABS, LANES), f32),
        compiler_params=pltpu.CompilerParams(
            dimension_semantics=("arbitrary",), vmem_limit_bytes=VMEM_LIMIT),
        name="moe_dispatch",
    )(block_valid, n_zero, tile_table, top_i, rank, xn)


def _expert_kernel(be_ref, seg_ref, nxt_ref, bv_ref, nb_ref, x_ref, wgu_hbm, bgu_ref, wdn_hbm, bdn_ref, y_ref,
                   wgu_f32, wdn_f32, wgu_bf, wdn_bf, sems, *, ff_chunk, cast_rows):
    i = pl.program_id(0)

    @pl.when(i >= nb_ref[0])
    def _():
        y_ref[...] = jnp.zeros_like(y_ref)

    def fetch(e, slot):
        return (pltpu.make_async_copy(wgu_hbm.at[e], wgu_f32.at[slot], sems.at[slot, 0]),
                pltpu.make_async_copy(wdn_hbm.at[e], wdn_f32.at[slot], sems.at[slot, 1]))

    @pl.when((i < nb_ref[0]) & ((i == 0) | (be_ref[i] != be_ref[jnp.maximum(i - 1, 0)])))
    def _():
        slot = seg_ref[i] % 2

        @pl.when(i == 0)
        def _():
            for c in fetch(be_ref[0], 0):
                c.start()

        for c in fetch(be_ref[i], slot):
            c.wait()

        @pl.when(nxt_ref[i] >= 0)
        def _():
            for c in fetch(nxt_ref[i], 1 - slot):
                c.start()

        def cast(j, carry):
            r0 = pl.multiple_of(j * cast_rows, cast_rows)
            wgu_bf[pl.ds(r0, cast_rows), :] = wgu_f32[slot, pl.ds(r0, cast_rows), :].astype(bf16)
            wdn_bf[pl.ds(r0, cast_rows), :] = wdn_f32[slot, pl.ds(r0, cast_rows), :].astype(bf16)
            return carry
        lax.fori_loop(0, D_MODEL // cast_rows, cast, 0)

    def mlp(rows):
        xb = jnp.concatenate(
            [_load_slab_cols(x_ref, 0, rows, s).astype(bf16) for s in range(SLABS)], axis=1)
        hid = []
        for c0 in range(0, D_FF, ff_chunk):
            gate = _dot(xb, wgu_bf[:, c0:c0 + ff_chunk]) + bgu_ref[0, :, c0:c0 + ff_chunk]
            up = (_dot(xb, wgu_bf[:, D_FF + c0:D_FF + c0 + ff_chunk])
                  + bgu_ref[0, :, D_FF + c0:D_FF + c0 + ff_chunk])
            gate = jnp.minimum(gate, SWIGLU_LIMIT)
            up = jnp.clip(up, -SWIGLU_LIMIT, SWIGLU_LIMIT)
            hid.append(((up + 1.0) * (gate * _sigmoid(gate * SWIGLU_ALPHA))).astype(bf16))
        hid = jnp.concatenate(hid, axis=1)
        for n0 in range(0, D_MODEL, ff_chunk):
            y = _dot(hid, wdn_bf[:, n0:n0 + ff_chunk]) + bdn_ref[0, :, n0:n0 + ff_chunk]
            for s in range(n0 // LANES, (n0 + ff_chunk) // LANES):
                y_ref[pl.ds(s, rows, stride=SLABS), :] = y[:, s * LANES - n0:(s + 1) * LANES - n0]

    half = MOE_TILE // 2

    @pl.when((i < nb_ref[0]) & (bv_ref[i] > half))
    def _():
        mlp(MOE_TILE)

    @pl.when((i < nb_ref[0]) & (bv_ref[i] <= half))
    def _():
        mlp(half)
        y_ref[pl.ds(half * SLABS, half * SLABS), :] = jnp.zeros((half * SLABS, LANES), f32)


def _experts(block_e, block_seg, next_e, block_valid, n_used, xs, wgu, bgu, wdn, bdn):
    n_blocks = block_e.shape[0]
    grid_spec = pltpu.PrefetchScalarGridSpec(
        num_scalar_prefetch=5,
        grid=(n_blocks,),
        in_specs=[
            pl.BlockSpec((MOE_TILE * SLABS, LANES),
                         lambda i, be, sg, nx, bv, nb: (jnp.minimum(i, nb[0] - 1), 0)),
            pl.BlockSpec(memory_space=pl.ANY),
            pl.BlockSpec((1, 1, 2 * D_FF), lambda i, be, sg, nx, bv, nb: (be[i], 0, 0)),
            pl.BlockSpec(memory_space=pl.ANY),
            pl.BlockSpec((1, 1, D_MODEL), lambda i, be, sg, nx, bv, nb: (be[i], 0, 0)),
        ],
        out_specs=pl.BlockSpec((MOE_TILE * SLABS, LANES), lambda i, be, sg, nx, bv, nb: (i, 0)),
        scratch_shapes=[
            pltpu.VMEM((2, D_MODEL, 2 * D_FF), f32),
            pltpu.VMEM((2, D_FF, D_MODEL), f32),
            pltpu.VMEM((D_MODEL, 2 * D_FF), bf16),
            pltpu.VMEM((D_FF, D_MODEL), bf16),
            pltpu.SemaphoreType.DMA((2, 2)),
        ],
    )
    return pl.pallas_call(
        functools.partial(_expert_kernel, ff_chunk=256, cast_rows=128),
        grid_spec=grid_spec,
        out_shape=jax.ShapeDtypeStruct((n_blocks * MOE_TILE * SLABS, LANES), f32),
        compiler_params=pltpu.CompilerParams(
            dimension_semantics=("arbitrary",), vmem_limit_bytes=VMEM_LIMIT),
        name="moe_experts",
    )(block_e, block_seg, next_e, block_valid, n_used, xs, wgu, bgu, wdn, bdn)


def _combine_kernel(pos_cur, pos_nxt, ys_hbm, h_ref, tg_ref, pp_ref, ps_ref, gple_ref, wpg_ref, wpp_ref,
                    gfin_ref, yp_ref, ys_ref, rows, sems, *, n_prompt_tiles, issue_unroll):
    i = pl.program_id(0)
    nt = pl.num_programs(0)
    n = COMBINE_TILE * TOP_K
    slot_rows = n * SLABS

    def row_copy(pos_ref, slot_base, r, slot):
        src0 = pl.multiple_of(pos_ref[0, 0, r], SLABS)
        dst0 = pl.multiple_of(slot_base + r * SLABS, SLABS)
        return pltpu.make_async_copy(ys_hbm.at[pl.ds(src0, SLABS)], rows.at[pl.ds(dst0, SLABS)],
                                     sems.at[slot])

    def wait_slot(slot):
        pltpu.make_async_copy(ys_hbm.at[pl.ds(0, slot_rows)],
                              rows.at[pl.ds(pl.multiple_of(slot * slot_rows, slot_rows), slot_rows)],
                              sems.at[slot]).wait()

    def gather(pos_ref, slot):
        slot_base = slot * slot_rows

        def start(j, carry):
            for u in range(issue_unroll):
                row_copy(pos_ref, slot_base, issue_unroll * j + u, slot).start(priority=u % 2)
            return carry
        lax.fori_loop(0, n // issue_unroll, start, 0)

    @pl.when(i == 0)
    def _():
        gather(pos_cur, 0)

    @pl.when(i + 1 < nt)
    def _():
        gather(pos_nxt, (i + 1) % 2)

    slot = i % 2
    base = pl.multiple_of(slot * slot_rows, slot_rows)
    wait_slot(slot)

    tg = tg_ref[...]
    gate_k = [jnp.broadcast_to(tg[:, k:k + 1], (COMBINE_TILE, LANES)) for k in range(TOP_K)]
    cols = []
    for s in range(SLABS):
        moe = jnp.zeros((COMBINE_TILE, LANES), f32)
        for k in range(TOP_K):
            moe = moe + gate_k[k] * _load_slab_cols(
                rows, base + k * COMBINE_TILE * SLABS, COMBINE_TILE, s)
        cols.append(h_ref[:, s * LANES:(s + 1) * LANES] + moe)
    h = jnp.concatenate(cols, axis=1)
    gate = _sigmoid(_dot(_rms(h, gple_ref[...]).astype(bf16), wpg_ref[...]))
    p = _select_rows(i < n_prompt_tiles, pp_ref, ps_ref).astype(bf16)
    h = h + gate * _dot(p, wpp_ref[...])
    y = _rms(h, gfin_ref[...])

    @pl.when(i < n_prompt_tiles)
    def _():
        yp_ref[...] = y

    @pl.when(i >= n_prompt_tiles)
    def _():
        ys_ref[...] = y


def _combine(pos_tiles, ys, h1, tg, pp, ps, gple, wpg, wpp, gfin):
    t = h1.shape[0]
    nt = t // COMBINE_TILE
    npt = pp.shape[0] // COMBINE_TILE
    n = COMBINE_TILE * TOP_K
    row = lambda w: pl.BlockSpec((COMBINE_TILE, w), lambda i: (i, 0))
    full = lambda shape: pl.BlockSpec(shape, lambda i: (0,) * len(shape))
    return pl.pallas_call(
        functools.partial(_combine_kernel, n_prompt_tiles=npt, issue_unroll=16),
        grid=(nt,),
        in_specs=[
            pl.BlockSpec((1, 1, n), lambda i: (i, 0, 0), memory_space=pltpu.SMEM),
            pl.BlockSpec((1, 1, n), lambda i: (jnp.minimum(i + 1, nt - 1), 0, 0), memory_space=pltpu.SMEM),
            pl.BlockSpec(memory_space=pl.ANY),
            row(D_MODEL), row(LANES),
            pl.BlockSpec((COMBINE_TILE, PLE_DIM), lambda i: (jnp.minimum(i, npt - 1), 0)),
            pl.BlockSpec((COMBINE_TILE, PLE_DIM), lambda i: (jnp.maximum(i - npt, 0), 0)),
            full((1, D_MODEL)), full((D_MODEL, D_MODEL)), full((PLE_DIM, D_MODEL)),
            full((1, D_MODEL)),
        ],
        out_specs=[
            pl.BlockSpec((COMBINE_TILE, D_MODEL), lambda i: (jnp.minimum(i, npt - 1), 0)),
            pl.BlockSpec((COMBINE_TILE, D_MODEL), lambda i: (jnp.maximum(i - npt, 0), 0)),
        ],
        out_shape=[
            jax.ShapeDtypeStruct((pp.shape[0], D_MODEL), f32),
            jax.ShapeDtypeStruct((ps.shape[0], D_MODEL), f32),
        ],
        scratch_shapes=[
            pltpu.VMEM((2 * n * SLABS, LANES), f32),
            pltpu.SemaphoreType.DMA((2,)),
        ],
        compiler_params=pltpu.CompilerParams(
            dimension_semantics=("arbitrary",), vmem_limit_bytes=VMEM_LIMIT),
        name="moe_combine_ple",
    )(pos_tiles, pos_tiles, ys, h1, tg, pp, ps, gple, wpg, wpp, gfin)


def _pos_kernel(pstart_ref, ti_ref, rank_ref, pos_ref):
    ti = ti_ref[...]
    seg = jnp.zeros(ti.shape, i32)
    for e in range(N_EXPERTS):
        seg = jnp.where(ti == e, pstart_ref[e], seg)
    pos = (rank_ref[...] + seg) * SLABS
    for j in range(pos_ref.shape[0]):
        for k in range(TOP_K):
            pos_ref[j:j + 1, k * DISPATCH_TILE:(k + 1) * DISPATCH_TILE] = (
                pos[k:k + 1, j * DISPATCH_TILE:(j + 1) * DISPATCH_TILE])


def _routing_tables(top_i, rank, counts, part_counts, n_blocks):
    padded = (counts + MOE_TILE - 1) // MOE_TILE * MOE_TILE
    pend = jnp.cumsum(padded)
    pstart = pend - padded
    block_rows = jnp.arange(n_blocks, dtype=i32) * MOE_TILE
    block_e = jnp.minimum(
        jnp.sum((block_rows[:, None] >= pend[None, :]).astype(i32), axis=1), N_EXPERTS - 1)
    experts = jnp.arange(N_EXPERTS, dtype=i32)
    of_block = lambda table: jnp.sum(jnp.where(block_e[:, None] == experts[None, :], table[None, :], 0), axis=1)
    block_valid = jnp.clip(of_block(pstart + counts) - block_rows, 0, MOE_TILE)
    n_used = pend[-1:] // MOE_TILE
    owns = padded > 0
    later = owns[None, :] & (experts[None, :] > experts[:, None])
    next_of_e = jnp.where(jnp.any(later, axis=1), jnp.argmax(later, axis=1), -1)
    block_seg = of_block(jnp.cumsum(owns.astype(i32)) - 1)
    next_e = of_block(next_of_e.astype(i32))

    nt = top_i.shape[1] // DISPATCH_TILE
    pos_tiles = pl.pallas_call(
        _pos_kernel,
        in_specs=[pl.BlockSpec(memory_space=pltpu.SMEM),
                  pl.BlockSpec(memory_space=pltpu.VMEM), pl.BlockSpec(memory_space=pltpu.VMEM)],
        out_specs=pl.BlockSpec(memory_space=pltpu.VMEM),
        out_shape=jax.ShapeDtypeStruct((nt, TOP_K * DISPATCH_TILE), i32),
        name="route_rows",
    )(pstart.astype(i32), top_i, rank).reshape(nt, 1, TOP_K * DISPATCH_TILE)
    tile_n = part_counts[:, :, :TOKEN_TILE // DISPATCH_TILE].transpose(0, 2, 1).reshape(nt, N_EXPERTS)
    seen = jnp.cumsum(tile_n, axis=0) - tile_n
    tile_src = jnp.cumsum(tile_n, axis=1) - tile_n
    tile_dst = pstart[None, :] + seen
    bits = jnp.arange(RUN_BITS, dtype=i32)[None, :, None]
    n3 = tile_n[:, None, :]
    has_piece = ((n3 >> bits) & 1) == 1
    covered = (n3 >> (bits + 1)) << (bits + 1)
    earlier = (experts[:, None] < experts[None, :])[None, None]
    place = jnp.sum((has_piece[..., None] & earlier).astype(i32), axis=2)
    listed = has_piece[..., None] & (place[..., None] == experts[None, None, None, :])
    compact = lambda v: jnp.sum(jnp.where(listed, v[..., None], 0), axis=2)
    piece_src = compact((tile_src[:, None, :] + covered) * SLABS)
    piece_dst = compact((tile_dst[:, None, :] + covered) * SLABS)
    piece_cnt = jnp.sum(has_piece.astype(i32), axis=2)
    tile_table = jnp.concatenate(
        [tile_src - seen, jnp.pad(piece_cnt, ((0, 0), (0, N_EXPERTS - RUN_BITS))),
         piece_src.reshape(nt, -1), piece_dst.reshape(nt, -1)], axis=1)
    tables = (block_e, block_seg, next_e, block_valid, n_used)
    return (pos_tiles, tile_table.reshape(nt, 1, RUN_TABLE).astype(i32)) + tuple(
        t.astype(i32) for t in tables)


def kernel(x_prompt, x_sample, p_prompt, p_sample, state_gla, state_hgrn, g_mix, w_in, w_gk2, b_gk2,
           gn_gla, lb_logits, gn_hgrn, w_out, g_ffn, w_router, b_router, w_gu, b_gu, w_dn, b_dn,
           g_ple, w_ple_gate, w_ple_proj, g_final):
    batch, seq, _ = x_prompt.shape
    dec_batch, dec_seq, _ = x_sample.shape
    tp, ts = batch * seq, dec_batch * dec_seq
    xp = x_prompt.reshape(tp, D_MODEL)
    xs = x_sample.reshape(ts, D_MODEL)

    wgk = jnp.concatenate(
        [w_gk2[0], jnp.zeros((R1_PAD - GLA_LOWRANK, GLA_KDIM), f32)], axis=0).astype(bf16)
    wr = w_router[0].T.astype(bf16)
    br = b_router[0].reshape(N_EXPERTS, 1)

    z = _in_proj(xp, xs, g_mix[0].reshape(1, D_MODEL), w_in[0])

    mixer_params = (wgk, b_gk2[0].reshape(1, GLA_KDIM), gn_gla[0].reshape(1, GLA_DV),
                    lb_logits[0:2], gn_hgrn[0].reshape(1, HGRN_DV))
    o_p, new_gla_p, new_hgrn_p = _mixer_prompt(
        z, *mixer_params, batch=batch, seq=seq, rows_per_step=512, block_rows=256)
    z_s = jnp.pad(z[tp:].reshape(dec_batch, dec_seq, Z_W), ((0, 0), (0, SAMPLE_PAD_T - dec_seq), (0, 0)))
    o_s, new_gla_s, new_hgrn_s = _mixer_sample(
        z_s.reshape(dec_batch * SAMPLE_PAD_T, Z_W), state_gla, state_hgrn, *mixer_params,
        steps=dec_seq, batch_block=8)
    o_s = o_s.reshape(dec_batch, SAMPLE_PAD_T, D_MODEL)[:, :dec_seq].reshape(ts, D_MODEL).astype(bf16)

    h1, xn, top_i, top_g, rank, counts, part_counts = _out_proj(
        xp, xs, o_p, o_s, w_out[0].astype(bf16), g_ffn[0].reshape(1, D_MODEL), wr, br)

    t = tp + ts
    n_blocks = -(-(t * TOP_K) // MOE_TILE) + N_EXPERTS
    pos_tiles, tile_table, block_e, block_seg, next_e, block_valid, n_used = _routing_tables(
        top_i, rank, counts[:, 0], part_counts, n_blocks)
    x_sorted = _dispatch(block_valid, tile_table, top_i, rank, xn)
    y_sorted = _experts(block_e, block_seg, next_e, block_valid, n_used, x_sorted,
                        w_gu[0], b_gu[0].reshape(N_EXPERTS, 1, 2 * D_FF),
                        w_dn[0], b_dn[0].reshape(N_EXPERTS, 1, D_MODEL))

    y_p, y_s = _combine(pos_tiles, y_sorted, h1, top_g, p_prompt[0].reshape(tp, PLE_DIM),
                        p_sample[0].reshape(ts, PLE_DIM), g_ple[0].reshape(1, D_MODEL),
                        w_ple_gate[0].astype(bf16), w_ple_proj[0].astype(bf16), g_final.reshape(1, D_MODEL))

    return (y_p.reshape(batch, seq, D_MODEL), y_s.reshape(dec_batch, dec_seq, D_MODEL),
            new_gla_p, new_hgrn_p, new_gla_s, new_hgrn_s)
```

```python
import functools

import jax
import jax.numpy as jnp
from jax import lax
from jax.experimental import pallas as pl
from jax.experimental.pallas import tpu as pltpu

f32 = jnp.float32
bf16 = jnp.bfloat16
i32 = jnp.int32

D_MODEL = 1024
GLA_HEADS = 4
GLA_DK = 64
GLA_DV = 128
GLA_KDIM = GLA_HEADS * GLA_DK
GLA_WIDTH = GLA_HEADS * GLA_DV
GLA_LOWRANK = 16
GLA_NORMALIZER = 16.0
HGRN_HEADS = 4
HGRN_DK = 128
HGRN_DV = 128
HGRN_FDIM = HGRN_HEADS * HGRN_DK
HGRN_WIDTH = HGRN_HEADS * HGRN_DV
CHUNK = 64
N_EXPERTS = 32
TOP_K = 4
D_FF = 1024
SWIGLU_LIMIT = 7.0
SWIGLU_ALPHA = 1.702
PLE_DIM = 256
EPS = 1e-6

LANES = 128
SAMPLE_PAD_T = 8
SLABS = D_MODEL // LANES
ROUTE_ROWS = 8
R1_PAD = LANES
Z_Q1 = 0
Z_K1 = Z_Q1 + GLA_KDIM
Z_V1 = Z_K1 + GLA_KDIM
Z_G1 = Z_V1 + GLA_WIDTH
Z_R1 = Z_G1 + GLA_WIDTH
Z_Q2 = Z_R1 + R1_PAD
Z_F2 = Z_Q2 + HGRN_FDIM
Z_I2 = Z_F2 + HGRN_FDIM
Z_G2 = Z_I2 + HGRN_WIDTH
Z_W = Z_G2 + HGRN_WIDTH

TOKEN_TILE = 512
MOE_TILE = 512
DISPATCH_TILE = 256
COMBINE_TILE = DISPATCH_TILE
RUN_BITS = DISPATCH_TILE.bit_length()
RUN_SHIFT = 0
RUN_COUNT = RUN_SHIFT + N_EXPERTS
RUN_SRC = RUN_COUNT + N_EXPERTS
RUN_DST = RUN_SRC + RUN_BITS * N_EXPERTS
RUN_TABLE = RUN_DST + RUN_BITS * N_EXPERTS
VMEM_LIMIT = 56 * 1024 * 1024


def _rms(x, g):
    return x * lax.rsqrt(jnp.mean(x * x, axis=-1, keepdims=True) + EPS) * g


def _sigmoid(x):
    return 0.5 * jnp.tanh(0.5 * x) + 0.5


def _dot(a, b):
    return jnp.dot(a, b, preferred_element_type=f32)


def _dot_nt(a, b):
    return lax.dot_general(a, b, (((1,), (1,)), ((), ())), preferred_element_type=f32)


def _dot_tn(a, b):
    return lax.dot_general(a, b, (((0,), (0,)), ((), ())), preferred_element_type=f32)


def _cumsum_rows(tri, x):
    hi = x.astype(bf16)
    r1 = x - hi.astype(f32)
    mid = r1.astype(bf16)
    lo = (r1 - mid.astype(f32)).astype(bf16)
    return _dot(tri, hi) + _dot(tri, mid) + _dot(tri, lo)


def _select_rows(is_prompt, a_ref, b_ref):
    return jnp.where(is_prompt, a_ref[...], b_ref[...])


def _inproj_kernel(xp_ref, xs_ref, g_ref, w_hbm, z_ref, w_f32, w_bf, sem,
                   *, n_prompt_tiles, col_chunk, cast_rows):
    @pl.when(pl.program_id(0) == 0)
    def _():
        copy = pltpu.make_async_copy(w_hbm, w_f32, sem)
        copy.start()
        copy.wait()
        r1_end = Z_R1 + GLA_LOWRANK

        def cast(j, carry):
            rows = pl.ds(pl.multiple_of(j * cast_rows, cast_rows), cast_rows)
            w_bf[rows, 0:Z_R1] = w_f32[rows, 0:Z_R1].astype(bf16)
            w_bf[rows, Z_R1:Z_Q2] = jnp.concatenate(
                [w_f32[rows, Z_R1:r1_end], jnp.zeros((cast_rows, R1_PAD - GLA_LOWRANK), f32)],
                axis=1).astype(bf16)
            w_bf[rows, Z_Q2:Z_W] = w_f32[rows, r1_end:].astype(bf16)
            return carry
        lax.fori_loop(0, D_MODEL // cast_rows, cast, 0)

    is_prompt = pl.program_id(0) < n_prompt_tiles
    x = _select_rows(is_prompt, xp_ref, xs_ref)
    a = _rms(x, g_ref[...]).astype(bf16)
    for c0 in range(0, Z_W, col_chunk):
        c1 = min(c0 + col_chunk, Z_W)
        z_ref[:, c0:c1] = _dot(a, w_bf[:, c0:c1]).astype(bf16)


def _in_proj(xp, xs, g, w):
    tp, ts = xp.shape[0], xs.shape[0]
    npt = tp // TOKEN_TILE
    nt = npt + ts // TOKEN_TILE
    return pl.pallas_call(
        functools.partial(_inproj_kernel, n_prompt_tiles=npt, col_chunk=512, cast_rows=64),
        grid=(nt,),
        in_specs=[
            pl.BlockSpec((TOKEN_TILE, D_MODEL), lambda i: (jnp.minimum(i, npt - 1), 0)),
            pl.BlockSpec((TOKEN_TILE, D_MODEL), lambda i: (jnp.maximum(i - npt, 0), 0)),
            pl.BlockSpec((1, D_MODEL), lambda i: (0, 0)),
            pl.BlockSpec(memory_space=pl.ANY),
        ],
        out_specs=pl.BlockSpec((TOKEN_TILE, Z_W), lambda i: (i, 0)),
        out_shape=jax.ShapeDtypeStruct((tp + ts, Z_W), bf16),
        scratch_shapes=[
            pltpu.VMEM(w.shape, f32),
            pltpu.VMEM((D_MODEL, Z_W), bf16),
            pltpu.SemaphoreType.DMA(()),
        ],
        compiler_params=pltpu.CompilerParams(
            dimension_semantics=("arbitrary",), vmem_limit_bytes=VMEM_LIMIT),
        name="in_proj",
    )(xp, xs, g, w)


def _block_consts(r, c):
    shift = c.bit_length() - 1
    row = lax.broadcasted_iota(i32, (r, r), 0)
    col = lax.broadcasted_iota(i32, (r, r), 1)
    causal = (lax.shift_right_logical(row, shift) == lax.shift_right_logical(col, shift)) & (row >= col)
    return causal.astype(bf16), causal


def _subchunk_row(x, c, idx):
    parts = [jnp.broadcast_to(x[j * c + idx:j * c + idx + 1, :], (c, x.shape[1]))
             for j in range(x.shape[0] // c)]
    return jnp.concatenate(parts, axis=0)


def _gated_block(q, k, v, log_a, states, *, c, scale, ref_row, tri, causal, heads_per_group, scan):
    r = q.shape[0]
    nsub = r // c
    groups = q.shape[1] // LANES
    b = _cumsum_rows(tri, log_a)
    b_ref = _subchunk_row(b, c, ref_row)
    b_last = _subchunk_row(b, c, c - 1)
    qs = q * scale
    q_r = qs * jnp.exp(b - b_ref)
    k_r = (k * jnp.exp(b_ref - b)).astype(bf16)
    k_end = k * jnp.exp(b_last - b)
    q_b = qs * jnp.exp(b)
    decay = jnp.exp(b_last)
    vb = v.astype(bf16)
    lane = lax.broadcasted_iota(i32, (r, LANES), 1)
    width = LANES // heads_per_group
    outs, new_states = [], []
    for g in range(groups):
        gl = slice(g * LANES, (g + 1) * LANES)
        masks = [None] if heads_per_group == 1 else [
            (lane >= j * width) & (lane < (j + 1) * width) for j in range(heads_per_group)]
        pick = lambda x, m: x if m is None else jnp.where(m, x, 0.0)
        q_r_h = [pick(q_r[:, gl], m).astype(bf16) for m in masks]
        q_b_h = [pick(q_b[:, gl], m).astype(bf16) for m in masks]
        k_end_h = [pick(k_end[:, gl], m).astype(bf16) for m in masks]
        v_h = [vb[:, (g * heads_per_group + j) * LANES:(g * heads_per_group + j + 1) * LANES]
               for j in range(heads_per_group)]
        o_intra = []
        for j in range(heads_per_group):
            att = _dot_nt(q_r_h[j], k_r[:, gl])
            att = jnp.where(causal, att, 0.0).astype(bf16)
            o_intra.append(_dot(att, v_h[j]))
        upd = []
        for s in range(nsub):
            rs = slice(s * c, (s + 1) * c)
            u = _dot_tn(v_h[0][rs], k_end_h[0][rs])
            for j in range(1, heads_per_group):
                u = u + _dot_tn(v_h[j][rs], k_end_h[j][rs])
            upd.append(u)
        o_inter = [[] for _ in range(heads_per_group)]
        st = states[g] if scan else None
        group_states = []
        for s in range(nsub):
            rs = slice(s * c, (s + 1) * c)
            if not scan:
                st = states[s][g]
            st_b = st.astype(bf16)
            for j in range(heads_per_group):
                o_inter[j].append(_dot_nt(q_b_h[j][rs], st_b))
            st = st * decay[s * c:s * c + 1, gl] + upd[s]
            if not scan:
                group_states.append(st)
        for j in range(heads_per_group):
            outs.append(o_intra[j] + jnp.concatenate(o_inter[j], axis=0))
        new_states.append(st if scan else group_states)
    if not scan:
        new_states = [[new_states[g][s] for g in range(groups)] for s in range(nsub)]
    return outs, new_states


def _head_norm(o, g):
    return o * lax.rsqrt(jnp.mean(o * o, axis=-1, keepdims=True) + EPS) * g


def _mixer_block(zb, st_gla, st_hgrn, wgk, bgk, gng, lb, gnh, *, c, ref_row, tri, causal, valid, scan):
    z = zb.astype(f32)
    gk = _dot(zb[:, Z_R1:Z_R1 + R1_PAD], wgk) + bgk
    log_a = (jnp.minimum(gk, 0.0) - jnp.log(1.0 + jnp.exp(-jnp.abs(gk)))) / GLA_NORMALIZER
    k1 = z[:, Z_K1:Z_K1 + GLA_KDIM]
    if valid is not None:
        log_a = jnp.where(valid, log_a, 0.0)
        k1 = jnp.where(valid, k1, 0.0)
    o1, st_gla = _gated_block(
        z[:, Z_Q1:Z_Q1 + GLA_KDIM], k1, z[:, Z_V1:Z_V1 + GLA_WIDTH], log_a, st_gla,
        c=c, scale=GLA_DK ** -0.5, ref_row=ref_row, tri=tri, causal=causal,
        heads_per_group=LANES // GLA_DK, scan=scan)
    forget = lb + (1.0 - lb) * _sigmoid(z[:, Z_F2:Z_F2 + HGRN_FDIM])
    k2 = 1.0 - forget
    log_f = jnp.log(forget)
    if valid is not None:
        log_f = jnp.where(valid, log_f, 0.0)
        k2 = jnp.where(valid, k2, 0.0)
    q2 = z[:, Z_Q2:Z_Q2 + HGRN_FDIM]
    i2 = z[:, Z_I2:Z_I2 + HGRN_WIDTH]
    o2, st_hgrn = _gated_block(
        q2 * _sigmoid(q2), k2, i2 * _sigmoid(i2), log_f, st_hgrn,
        c=c, scale=1.0, ref_row=ref_row, tri=tri, causal=causal, heads_per_group=1, scan=scan)
    cols = []
    for h in range(GLA_HEADS):
        g1 = z[:, Z_G1 + h * GLA_DV:Z_G1 + (h + 1) * GLA_DV]
        cols.append(_head_norm(o1[h], gng) * (g1 * _sigmoid(g1)))
    for h in range(HGRN_HEADS):
        g2 = z[:, Z_G2 + h * HGRN_DV:Z_G2 + (h + 1) * HGRN_DV]
        cols.append(_head_norm(o2[h], gnh) * _sigmoid(g2))
    return jnp.concatenate(cols, axis=1), st_gla, st_hgrn


def _lower_bound(lbl_ref):
    l = lbl_ref[...]
    m = jnp.max(l, axis=0, keepdims=True)
    e = jnp.exp(l - m)
    return e[0:1, :] / jnp.sum(e, axis=0, keepdims=True)


GLA_GROUPS = GLA_KDIM // LANES


def _gla_state_t(s_ref_heads):
    return s_ref_heads.reshape(LANES, GLA_DV).T


def _gla_state_from_t(st):
    return st.T.reshape(LANES // GLA_DK, GLA_DK, GLA_DV)


def _mixer_prompt_kernel(z_ref, wgk_ref, bgk_ref, gng_ref, lbl_ref, gnh_ref,
                         o_ref, sg_out, sh_out, sg_t, sh_t, *, block_rows):
    j = pl.program_id(1)

    @pl.when(j == 0)
    def _():
        sg_t[...] = jnp.zeros_like(sg_t)
        sh_t[...] = jnp.zeros_like(sh_t)

    tri, causal = _block_consts(block_rows, CHUNK)
    lb = _lower_bound(lbl_ref)
    wgk, bgk, gng, gnh = wgk_ref[...], bgk_ref[...], gng_ref[...], gnh_ref[...]

    def body(blk, carry):
        r0 = pl.multiple_of(blk * block_rows, block_rows)
        o, st_g, st_h = _mixer_block(
            z_ref[pl.ds(r0, block_rows), :], [sg_t[g] for g in range(GLA_GROUPS)],
            [sh_t[h] for h in range(HGRN_HEADS)], wgk, bgk, gng, lb, gnh,
            c=CHUNK, ref_row=CHUNK // 2, tri=tri, causal=causal, valid=None, scan=True)
        o_ref[pl.ds(r0, block_rows), :] = o.astype(bf16)
        for g in range(GLA_GROUPS):
            sg_t[g] = st_g[g]
        for h in range(HGRN_HEADS):
            sh_t[h] = st_h[h]
        return carry

    lax.fori_loop(0, z_ref.shape[0] // block_rows, body, 0)

    @pl.when(j == pl.num_programs(1) - 1)
    def _():
        per = LANES // GLA_DK
        for g in range(GLA_GROUPS):
            sg_out[0, 0, g * per:(g + 1) * per] = _gla_state_from_t(sg_t[g])
        for h in range(HGRN_HEADS):
            sh_out[0, 0, h] = sh_t[h].T


def _mixer_prompt(z, wgk, bgk, gng, lbl, gnh, *, batch, seq, rows_per_step, block_rows):
    nj = seq // rows_per_step
    full = lambda shape: pl.BlockSpec(shape, lambda b, j: (0,) * len(shape))
    return pl.pallas_call(
        functools.partial(_mixer_prompt_kernel, block_rows=block_rows),
        grid=(batch, nj),
        in_specs=[
            pl.BlockSpec((rows_per_step, Z_W), lambda b, j: (b * nj + j, 0)),
            full((R1_PAD, GLA_KDIM)), full((1, GLA_KDIM)), full((1, GLA_DV)),
            full((2, HGRN_FDIM)), full((1, HGRN_DV)),
        ],
        out_specs=[
            pl.BlockSpec((rows_per_step, D_MODEL), lambda b, j: (b * nj + j, 0)),
            pl.BlockSpec((1, 1, GLA_HEADS, GLA_DK, GLA_DV), lambda b, j: (0, b, 0, 0, 0)),
            pl.BlockSpec((1, 1, HGRN_HEADS, HGRN_DK, HGRN_DV), lambda b, j: (0, b, 0, 0, 0)),
        ],
        out_shape=[
            jax.ShapeDtypeStruct((batch * seq, D_MODEL), bf16),
            jax.ShapeDtypeStruct((1, batch, GLA_HEADS, GLA_DK, GLA_DV), f32),
            jax.ShapeDtypeStruct((1, batch, HGRN_HEADS, HGRN_DK, HGRN_DV), f32),
        ],
        scratch_shapes=[
            pltpu.VMEM((GLA_GROUPS, GLA_DV, LANES), f32),
            pltpu.VMEM((HGRN_HEADS, HGRN_DV, HGRN_DK), f32),
        ],
        compiler_params=pltpu.CompilerParams(
            dimension_semantics=("arbitrary", "arbitrary"), vmem_limit_bytes=VMEM_LIMIT),
        name="mixer_prompt",
    )(z, wgk, bgk, gng, lbl, gnh)


def _mixer_sample_kernel(z_ref, sg_in, sh_in, wgk_ref, bgk_ref, gng_ref, lbl_ref, gnh_ref,
                         o_ref, sg_out, sh_out, *, batch_block, steps):
    rows = batch_block * SAMPLE_PAD_T
    tri, causal = _block_consts(rows, SAMPLE_PAD_T)
    step = lax.broadcasted_iota(i32, (rows, 1), 0) & (SAMPLE_PAD_T - 1)
    valid = step < steps
    lb = _lower_bound(lbl_ref)
    per = LANES // GLA_DK
    st_g = [[_gla_state_t(sg_in[0, s, g * per:(g + 1) * per]) for g in range(GLA_GROUPS)]
            for s in range(batch_block)]
    st_h = [[sh_in[0, s, h].T for h in range(HGRN_HEADS)] for s in range(batch_block)]
    o, st_g, st_h = _mixer_block(
        z_ref[...], st_g, st_h, wgk_ref[...], bgk_ref[...], gng_ref[...], lb, gnh_ref[...],
        c=SAMPLE_PAD_T, ref_row=steps // 2, tri=tri, causal=causal, valid=valid, scan=False)
    o_ref[...] = o
    for s in range(batch_block):
        for g in range(GLA_GROUPS):
            sg_out[0, s, g * per:(g + 1) * per] = _gla_state_from_t(st_g[s][g])
        for h in range(HGRN_HEADS):
            sh_out[0, s, h] = st_h[s][h].T


def _mixer_sample(z2, state_gla, state_hgrn, wgk, bgk, gng, lbl, gnh, *, steps, batch_block):
    batch = z2.shape[0] // SAMPLE_PAD_T
    rows = batch_block * SAMPLE_PAD_T
    full = lambda shape: pl.BlockSpec(shape, lambda i: (0,) * len(shape))
    sg_spec = pl.BlockSpec((1, batch_block, GLA_HEADS, GLA_DK, GLA_DV), lambda i: (0, i, 0, 0, 0))
    sh_spec = pl.BlockSpec((1, batch_block, HGRN_HEADS, HGRN_DK, HGRN_DV), lambda i: (0, i, 0, 0, 0))
    return pl.pallas_call(
        functools.partial(_mixer_sample_kernel, batch_block=batch_block, steps=steps),
        grid=(batch // batch_block,),
        in_specs=[
            pl.BlockSpec((rows, Z_W), lambda i: (i, 0)),
            sg_spec, sh_spec,
            full((R1_PAD, GLA_KDIM)), full((1, GLA_KDIM)), full((1, GLA_DV)),
            full((2, HGRN_FDIM)), full((1, HGRN_DV)),
        ],
        out_specs=[pl.BlockSpec((rows, D_MODEL), lambda i: (i, 0)), sg_spec, sh_spec],
        out_shape=[
            jax.ShapeDtypeStruct((batch * SAMPLE_PAD_T, D_MODEL), f32),
            jax.ShapeDtypeStruct(state_gla.shape, f32),
            jax.ShapeDtypeStruct(state_hgrn.shape, f32),
        ],
        compiler_params=pltpu.CompilerParams(
            dimension_semantics=("arbitrary",), vmem_limit_bytes=VMEM_LIMIT),
        name="mixer_sample",
    )(z2, state_gla, state_hgrn, wgk, bgk, gng, lbl, gnh)


def _load_slab_cols(ref, start, rows, s):
    return ref[pl.ds(start + s, rows, stride=SLABS), :]


def _outproj_kernel(xp_ref, xs_ref, op_ref, os_ref, wo_ref, g_ref, wr_ref, br_ref,
                    h_ref, xn_ref, ti_ref, tg_ref, rank_ref, tic_ref, rankc_ref, cnt_ref, part_ref, run_cnt,
                    *, n_prompt_tiles):
    i = pl.program_id(0)

    @pl.when(i == 0)
    def _():
        run_cnt[...] = jnp.zeros_like(run_cnt)

    x = _select_rows(i < n_prompt_tiles, xp_ref, xs_ref)
    o = _select_rows(i < n_prompt_tiles, op_ref, os_ref)
    h = x + _dot(o, wo_ref[...])
    h_ref[...] = h
    xn = _rms(h, g_ref[...])
    xn_ref[...] = xn.astype(bf16)
    l = _dot_nt(wr_ref[...], xn.astype(bf16)) + br_ref[...]
    expert = lax.broadcasted_iota(i32, l.shape, 0)
    neg = jnp.float32(-jnp.inf)
    vals, idxs = [], []
    for _ in range(TOP_K):
        m = jnp.max(l, axis=0, keepdims=True)
        idx = jnp.min(jnp.where(l == m, expert, N_EXPERTS), axis=0, keepdims=True)
        vals.append(m)
        idxs.append(idx)
        l = jnp.where(expert == idx, neg, l)
    es = [jnp.exp(v - vals[0]) for v in vals]
    denom = es[0] + es[1] + es[2] + es[3]

    onehot = jnp.zeros(l.shape, f32)
    for k in range(TOP_K):
        onehot = onehot + (expert == idxs[k]).astype(f32)
    n = l.shape[1]
    earlier = (lax.broadcasted_iota(i32, (n, n), 0) < lax.broadcasted_iota(i32, (n, n), 1)).astype(bf16)
    ahead = _dot(onehot.astype(bf16), earlier) + run_cnt[:, 0:1]
    run_cnt[...] = run_cnt[...] + jnp.sum(onehot, axis=1, keepdims=True)
    cnt_ref[...] = run_cnt[...].astype(i32)
    lane = lax.broadcasted_iota(i32, (N_EXPERTS, LANES), 1)
    parts = jnp.zeros((N_EXPERTS, LANES), f32)
    for p in range(n // DISPATCH_TILE):
        c = jnp.sum(onehot[:, p * DISPATCH_TILE:(p + 1) * DISPATCH_TILE], axis=1, keepdims=True)
        parts = jnp.where(lane == p, c, parts)
    part_ref[0] = parts.astype(i32)

    ranks = [jnp.sum(jnp.where(expert == idxs[k], ahead, 0.0), axis=0, keepdims=True).astype(i32)
             for k in range(TOP_K)]
    pad_i = jnp.zeros((ROUTE_ROWS - TOP_K, n), i32)
    ti_ref[...] = jnp.concatenate(idxs + [pad_i], axis=0)
    rank_ref[...] = jnp.concatenate(ranks + [pad_i], axis=0)
    gate_rows = jnp.concatenate([e / denom for e in es] + [jnp.zeros((LANES - TOP_K, n), f32)], axis=0)
    tg_ref[...] = gate_rows.T
    pad_wide = jnp.zeros((LANES - TOP_K, n), i32)
    tic_ref[...] = jnp.concatenate(idxs + [pad_wide], axis=0).T
    rankc_ref[...] = jnp.concatenate(ranks + [pad_wide], axis=0).T


def _out_proj(xp, xs, o_p, o_s, wo, g, wr, br):
    tp, ts = xp.shape[0], xs.shape[0]
    npt = tp // TOKEN_TILE
    nt = npt + ts // TOKEN_TILE
    t = tp + ts
    row = lambda w: pl.BlockSpec((TOKEN_TILE, w), lambda i: (i, 0))
    route = pl.BlockSpec((ROUTE_ROWS, TOKEN_TILE), lambda i: (0, i))
    full = lambda shape: pl.BlockSpec(shape, lambda i: (0,) * len(shape))
    prompt_rows = pl.BlockSpec((TOKEN_TILE, D_MODEL), lambda i: (jnp.minimum(i, npt - 1), 0))
    sample_rows = pl.BlockSpec((TOKEN_TILE, D_MODEL), lambda i: (jnp.maximum(i - npt, 0), 0))
    return pl.pallas_call(
        functools.partial(_outproj_kernel, n_prompt_tiles=npt),
        grid=(nt,),
        in_specs=[
            prompt_rows, sample_rows, prompt_rows, sample_rows,
            full((D_MODEL, D_MODEL)), full((1, D_MODEL)),
            full((N_EXPERTS, D_MODEL)), full((N_EXPERTS, 1)),
        ],
        out_specs=[row(D_MODEL), row(D_MODEL),
                   route, row(LANES), route, row(LANES), row(LANES), full((N_EXPERTS, LANES)),
                   pl.BlockSpec((1, N_EXPERTS, LANES), lambda i: (i, 0, 0))],
        out_shape=[
            jax.ShapeDtypeStruct((t, D_MODEL), f32),
            jax.ShapeDtypeStruct((t, D_MODEL), bf16),
            jax.ShapeDtypeStruct((ROUTE_ROWS, t), i32),
            jax.ShapeDtypeStruct((t, LANES), f32),
            jax.ShapeDtypeStruct((ROUTE_ROWS, t), i32),
            jax.ShapeDtypeStruct((t, LANES), i32),
            jax.ShapeDtypeStruct((t, LANES), i32),
            jax.ShapeDtypeStruct((N_EXPERTS, LANES), i32),
            jax.ShapeDtypeStruct((nt, N_EXPERTS, LANES), i32),
        ],
        scratch_shapes=[pltpu.VMEM((N_EXPERTS, LANES), f32)],
        compiler_params=pltpu.CompilerParams(
            dimension_semantics=("arbitrary",), vmem_limit_bytes=VMEM_LIMIT),
        name="out_proj_router",
    )(xp, xs, o_p, o_s, wo, g, wr, br)


def _dispatch_kernel(bv_ref, nz_ref, tab_ref, ti_ref, rank_ref, xn_ref, xs_hbm, zeros, stage, sems, zsem,
                     *, n_blocks):
    @pl.when(pl.program_id(0) == 0)
    def _():
        zeros[...] = jnp.zeros_like(zeros)

        def fill(b, carry):
            @pl.when(bv_ref[b] < MOE_TILE)
            def _():
                r0 = pl.multiple_of(b * (MOE_TILE * SLABS), MOE_TILE * SLABS)
                pltpu.make_async_copy(zeros, xs_hbm.at[pl.ds(r0, MOE_TILE * SLABS)], zsem).start()
            return carry

        def drain(b, carry):
            pltpu.make_async_copy(zeros, xs_hbm.at[pl.ds(0, MOE_TILE * SLABS)], zsem).wait()
            return carry

        lax.fori_loop(0, n_blocks, fill, 0)
        lax.fori_loop(0, nz_ref[0], drain, 0)

    i = pl.program_id(0)
    nt = pl.num_programs(0)
    n = DISPATCH_TILE * TOP_K
    slot_rows = n * SLABS
    slot = i % 2
    base = pl.multiple_of(slot * slot_rows, slot_rows)

    def wait_rows(s):
        pltpu.make_async_copy(stage.at[pl.ds(0, slot_rows)], xs_hbm.at[pl.ds(0, slot_rows)], sems.at[s]).wait()

    ti = ti_ref[...]
    shift = jnp.zeros(ti.shape, i32)
    for e in range(N_EXPERTS):
        shift = jnp.where(ti == e, tab_ref[0, 0, RUN_SHIFT + e], shift)
    staged = rank_ref[...] + shift
    row = lax.broadcasted_iota(i32, (n, DISPATCH_TILE), 0)
    pick = row == staged[0:1, :]
    for k in range(1, TOP_K):
        pick = pick | (row == staged[k:k + 1, :])
    rows_sorted = _dot(pick.astype(bf16), xn_ref[...])

    for s in range(SLABS):
        stage[pl.ds(base + s, n, stride=SLABS), :] = rows_sorted[:, s * LANES:(s + 1) * LANES]

    for bit in range(RUN_BITS - 1, -1, -1):
        size = (1 << bit) * SLABS

        def issue(p, carry, bit=bit, size=size):
            src0 = pl.multiple_of(tab_ref[0, 0, RUN_SRC + bit * N_EXPERTS + p], SLABS)
            dst0 = pl.multiple_of(tab_ref[0, 0, RUN_DST + bit * N_EXPERTS + p], SLABS)
            pltpu.make_async_copy(stage.at[pl.ds(base + src0, size)], xs_hbm.at[pl.ds(dst0, size)],
                                  sems.at[slot]).start(priority=bit % 2)
            return carry

        lax.fori_loop(0, tab_ref[0, 0, RUN_COUNT + bit], issue, 0)

    @pl.when(i > 0)
    def _():
        wait_rows(1 - slot)

    @pl.when(i == nt - 1)
    def _():
        wait_rows(slot)


def _dispatch(block_valid, tile_table, top_i, rank, xn):
    t = xn.shape[0]
    nt = t // DISPATCH_TILE
    n_blocks = block_valid.shape[0]
    n_zero = jnp.sum((block_valid < MOE_TILE).astype(i32)).reshape(1)
    route = pl.BlockSpec((ROUTE_ROWS, DISPATCH_TILE), lambda i, bv, nz: (0, i))
    grid_spec = pltpu.PrefetchScalarGridSpec(
        num_scalar_prefetch=2,
        grid=(nt,),
        in_specs=[
            pl.BlockSpec((1, 1, RUN_TABLE), lambda i, bv, nz: (i, 0, 0), memory_space=pltpu.SMEM),
            route, route,
            pl.BlockSpec((DISPATCH_TILE, D_MODEL), lambda i, bv, nz: (i, 0)),
        ],
        out_specs=pl.BlockSpec(memory_space=pl.ANY),
        scratch_shapes=[
            pltpu.VMEM((MOE_TILE * SLABS, LANES), f32),
            pltpu.VMEM((2 * DISPATCH_TILE * TOP_K * SLABS, LANES), f32),
            pltpu.SemaphoreType.DMA((2,)),
            pltpu.SemaphoreType.DMA(()),
        ],
    )
    return pl.pallas_call(
        functools.partial(_dispatch_kernel, n_blocks=n_blocks),
        grid_spec=grid_spec,
        out_shape=jax.ShapeDtypeStruct((n_blocks * MOE_TILE * SLABS, LANES), f32),
        compiler_params=pltpu.CompilerParams(
            dimension_semantics=("arbitrary",), vmem_limit_bytes=VMEM_LIMIT),
        name="moe_dispatch",
    )(block_valid, n_zero, tile_table, top_i, rank, xn)


def _expert_kernel(be_ref, seg_ref, nxt_ref, bv_ref, nb_ref, x_ref, wgu_hbm, bgu_ref, wdn_hbm, bdn_ref, y_ref,
                   wgu_f32, wdn_f32, wgu_bf, wdn_bf, sems, *, ff_chunk, cast_rows):
    i = pl.program_id(0)

    @pl.when(i >= nb_ref[0])
    def _():
        y_ref[...] = jnp.zeros_like(y_ref)

    def fetch(e, slot):
        return (pltpu.make_async_copy(wgu_hbm.at[e], wgu_f32.at[slot], sems.at[slot, 0]),
                pltpu.make_async_copy(wdn_hbm.at[e], wdn_f32.at[slot], sems.at[slot, 1]))

    @pl.when((i < nb_ref[0]) & ((i == 0) | (be_ref[i] != be_ref[jnp.maximum(i - 1, 0)])))
    def _():
        slot = seg_ref[i] % 2

        @pl.when(i == 0)
        def _():
            for c in fetch(be_ref[0], 0):
                c.start()

        for c in fetch(be_ref[i], slot):
            c.wait()

        @pl.when(nxt_ref[i] >= 0)
        def _():
            for c in fetch(nxt_ref[i], 1 - slot):
                c.start()

        def cast(j, carry):
            r0 = pl.multiple_of(j * cast_rows, cast_rows)
            wgu_bf[pl.ds(r0, cast_rows), :] = wgu_f32[slot, pl.ds(r0, cast_rows), :].astype(bf16)
            wdn_bf[pl.ds(r0, cast_rows), :] = wdn_f32[slot, pl.ds(r0, cast_rows), :].astype(bf16)
            return carry
        lax.fori_loop(0, D_MODEL // cast_rows, cast, 0)

    def mlp(rows):
        xb = jnp.concatenate(
            [_load_slab_cols(x_ref, 0, rows, s).astype(bf16) for s in range(SLABS)], axis=1)
        hid = []
        for c0 in range(0, D_FF, ff_chunk):
            gate = _dot(xb, wgu_bf[:, c0:c0 + ff_chunk]) + bgu_ref[0, :, c0:c0 + ff_chunk]
            up = (_dot(xb, wgu_bf[:, D_FF + c0:D_FF + c0 + ff_chunk])
                  + bgu_ref[0, :, D_FF + c0:D_FF + c0 + ff_chunk])
            gate = jnp.minimum(gate, SWIGLU_LIMIT)
            up = jnp.clip(up, -SWIGLU_LIMIT, SWIGLU_LIMIT)
            hid.append(((up + 1.0) * (gate * _sigmoid(gate * SWIGLU_ALPHA))).astype(bf16))
        hid = jnp.concatenate(hid, axis=1)
        for n0 in range(0, D_MODEL, ff_chunk):
            y = _dot(hid, wdn_bf[:, n0:n0 + ff_chunk]) + bdn_ref[0, :, n0:n0 + ff_chunk]
            for s in range(n0 // LANES, (n0 + ff_chunk) // LANES):
                y_ref[pl.ds(s, rows, stride=SLABS), :] = y[:, s * LANES - n0:(s + 1) * LANES - n0]

    half = MOE_TILE // 2

    @pl.when((i < nb_ref[0]) & (bv_ref[i] > half))
    def _():
        mlp(MOE_TILE)

    @pl.when((i < nb_ref[0]) & (bv_ref[i] <= half))
    def _():
        mlp(half)
        y_ref[pl.ds(half * SLABS, half * SLABS), :] = jnp.zeros((half * SLABS, LANES), f32)


def _experts(block_e, block_seg, next_e, block_valid, n_used, xs, wgu, bgu, wdn, bdn):
    n_blocks = block_e.shape[0]
    grid_spec = pltpu.PrefetchScalarGridSpec(
        num_scalar_prefetch=5,
        grid=(n_blocks,),
        in_specs=[
            pl.BlockSpec((MOE_TILE * SLABS, LANES),
                         lambda i, be, sg, nx, bv, nb: (jnp.minimum(i, nb[0] - 1), 0)),
            pl.BlockSpec(memory_space=pl.ANY),
            pl.BlockSpec((1, 1, 2 * D_FF), lambda i, be, sg, nx, bv, nb: (be[i], 0, 0)),
            pl.BlockSpec(memory_space=pl.ANY),
            pl.BlockSpec((1, 1, D_MODEL), lambda i, be, sg, nx, bv, nb: (be[i], 0, 0)),
        ],
        out_specs=pl.BlockSpec((MOE_TILE * SLABS, LANES), lambda i, be, sg, nx, bv, nb: (i, 0)),
        scratch_shapes=[
            pltpu.VMEM((2, D_MODEL, 2 * D_FF), f32),
            pltpu.VMEM((2, D_FF, D_MODEL), f32),
            pltpu.VMEM((D_MODEL, 2 * D_FF), bf16),
            pltpu.VMEM((D_FF, D_MODEL), bf16),
            pltpu.SemaphoreType.DMA((2, 2)),
        ],
    )
    return pl.pallas_call(
        functools.partial(_expert_kernel, ff_chunk=256, cast_rows=128),
        grid_spec=grid_spec,
        out_shape=jax.ShapeDtypeStruct((n_blocks * MOE_TILE * SLABS, LANES), f32),
        compiler_params=pltpu.CompilerParams(
            dimension_semantics=("arbitrary",), vmem_limit_bytes=VMEM_LIMIT),
        name="moe_experts",
    )(block_e, block_seg, next_e, block_valid, n_used, xs, wgu, bgu, wdn, bdn)


def _combine_kernel(tab_cur, tab_nxt, ys_hbm, h_ref, tg_ref, tic_ref, rankc_ref, pp_ref, ps_ref,
                    gple_ref, wpg_ref, wpp_ref, gfin_ref, yp_ref, ys_ref, stage, sems, *, n_prompt_tiles):
    i = pl.program_id(0)
    nt = pl.num_programs(0)
    n = COMBINE_TILE * TOP_K
    slot_rows = n * SLABS

    def gather(tab_ref, slot):
        slot_base = slot * slot_rows
        for bit in range(RUN_BITS - 1, -1, -1):
            size = (1 << bit) * SLABS

            def issue(p, carry, bit=bit, size=size):
                dst0 = pl.multiple_of(tab_ref[0, 0, RUN_SRC + bit * N_EXPERTS + p], SLABS)
                src0 = pl.multiple_of(tab_ref[0, 0, RUN_DST + bit * N_EXPERTS + p], SLABS)
                pltpu.make_async_copy(ys_hbm.at[pl.ds(src0, size)], stage.at[pl.ds(slot_base + dst0, size)],
                                      sems.at[slot]).start(priority=bit % 2)
                return carry

            lax.fori_loop(0, tab_ref[0, 0, RUN_COUNT + bit], issue, 0)

    @pl.when(i == 0)
    def _():
        gather(tab_cur, 0)

    @pl.when(i + 1 < nt)
    def _():
        gather(tab_nxt, (i + 1) % 2)

    slot = i % 2
    base = pl.multiple_of(slot * slot_rows, slot_rows)
    pltpu.make_async_copy(ys_hbm.at[pl.ds(0, slot_rows)], stage.at[pl.ds(base, slot_rows)],
                          sems.at[slot]).wait()

    tic = tic_ref[...]
    shift = jnp.zeros(tic.shape, i32)
    for e in range(N_EXPERTS):
        shift = jnp.where(tic == e, tab_cur[0, 0, RUN_SHIFT + e], shift)
    staged = rankc_ref[...] + shift
    tg = tg_ref[...]
    staged_k = [jnp.broadcast_to(staged[:, k:k + 1], (COMBINE_TILE, LANES)) for k in range(TOP_K)]
    gate_k = [jnp.broadcast_to(tg[:, k:k + 1], (COMBINE_TILE, LANES)) for k in range(TOP_K)]
    lane = lax.broadcasted_iota(i32, (COMBINE_TILE, LANES), 1)
    groups = []
    for c in range(n // LANES):
        g = jnp.zeros((COMBINE_TILE, LANES), f32)
        for k in range(TOP_K):
            g = jnp.where(lane + c * LANES == staged_k[k], gate_k[k], g)
        groups.append(g)
    gates = jnp.concatenate(groups, axis=1)
    gates_hi = gates.astype(bf16)
    gates_lo = (gates - gates_hi.astype(f32)).astype(bf16)
    rows = jnp.concatenate(
        [_load_slab_cols(stage, base, n, s).astype(bf16) for s in range(SLABS)], axis=1)
    h = h_ref[...] + _dot(gates_hi, rows) + _dot(gates_lo, rows)
    gate = _sigmoid(_dot(_rms(h, gple_ref[...]).astype(bf16), wpg_ref[...]))
    p = _select_rows(i < n_prompt_tiles, pp_ref, ps_ref).astype(bf16)
    h = h + gate * _dot(p, wpp_ref[...])
    y = _rms(h, gfin_ref[...])

    @pl.when(i < n_prompt_tiles)
    def _():
        yp_ref[...] = y

    @pl.when(i >= n_prompt_tiles)
    def _():
        ys_ref[...] = y


def _combine(tile_table, ys, h1, tg, tic, rankc, pp, ps, gple, wpg, wpp, gfin):
    t = h1.shape[0]
    nt = t // COMBINE_TILE
    npt = pp.shape[0] // COMBINE_TILE
    n = COMBINE_TILE * TOP_K
    row = lambda w: pl.BlockSpec((COMBINE_TILE, w), lambda i: (i, 0))
    full = lambda shape: pl.BlockSpec(shape, lambda i: (0,) * len(shape))
    return pl.pallas_call(
        functools.partial(_combine_kernel, n_prompt_tiles=npt),
        grid=(nt,),
        in_specs=[
            pl.BlockSpec((1, 1, RUN_TABLE), lambda i: (i, 0, 0), memory_space=pltpu.SMEM),
            pl.BlockSpec((1, 1, RUN_TABLE), lambda i: (jnp.minimum(i + 1, nt - 1), 0, 0),
                         memory_space=pltpu.SMEM),
            pl.BlockSpec(memory_space=pl.ANY),
            row(D_MODEL), row(LANES), row(LANES), row(LANES),
            pl.BlockSpec((COMBINE_TILE, PLE_DIM), lambda i: (jnp.minimum(i, npt - 1), 0)),
            pl.BlockSpec((COMBINE_TILE, PLE_DIM), lambda i: (jnp.maximum(i - npt, 0), 0)),
            full((1, D_MODEL)), full((D_MODEL, D_MODEL)), full((PLE_DIM, D_MODEL)),
            full((1, D_MODEL)),
        ],
        out_specs=[
            pl.BlockSpec((COMBINE_TILE, D_MODEL), lambda i: (jnp.minimum(i, npt - 1), 0)),
            pl.BlockSpec((COMBINE_TILE, D_MODEL), lambda i: (jnp.maximum(i - npt, 0), 0)),
        ],
        out_shape=[
            jax.ShapeDtypeStruct((pp.shape[0], D_MODEL), f32),
            jax.ShapeDtypeStruct((ps.shape[0], D_MODEL), f32),
        ],
        scratch_shapes=[
            pltpu.VMEM((2 * n * SLABS, LANES), f32),
            pltpu.SemaphoreType.DMA((2,)),
        ],
        compiler_params=pltpu.CompilerParams(
            dimension_semantics=("arbitrary",), vmem_limit_bytes=VMEM_LIMIT),
        name="moe_combine_ple",
    )(tile_table, tile_table, ys, h1, tg, tic, rankc, pp, ps, gple, wpg, wpp, gfin)


def _routing_tables(top_i, counts, part_counts, n_blocks):
    padded = (counts + MOE_TILE - 1) // MOE_TILE * MOE_TILE
    pend = jnp.cumsum(padded)
    pstart = pend - padded
    block_rows = jnp.arange(n_blocks, dtype=i32) * MOE_TILE
    block_e = jnp.minimum(
        jnp.sum((block_rows[:, None] >= pend[None, :]).astype(i32), axis=1), N_EXPERTS - 1)
    experts = jnp.arange(N_EXPERTS, dtype=i32)
    of_block = lambda table: jnp.sum(jnp.where(block_e[:, None] == experts[None, :], table[None, :], 0), axis=1)
    block_valid = jnp.clip(of_block(pstart + counts) - block_rows, 0, MOE_TILE)
    n_used = pend[-1:] // MOE_TILE
    owns = padded > 0
    later = owns[None, :] & (experts[None, :] > experts[:, None])
    next_of_e = jnp.where(jnp.any(later, axis=1), jnp.argmax(later, axis=1), -1)
    block_seg = of_block(jnp.cumsum(owns.astype(i32)) - 1)
    next_e = of_block(next_of_e.astype(i32))

    nt = top_i.shape[1] // DISPATCH_TILE
    tile_n = part_counts[:, :, :TOKEN_TILE // DISPATCH_TILE].transpose(0, 2, 1).reshape(nt, N_EXPERTS)
    seen = jnp.cumsum(tile_n, axis=0) - tile_n
    tile_src = jnp.cumsum(tile_n, axis=1) - tile_n
    tile_dst = pstart[None, :] + seen
    bits = jnp.arange(RUN_BITS, dtype=i32)[None, :, None]
    n3 = tile_n[:, None, :]
    has_piece = ((n3 >> bits) & 1) == 1
    covered = (n3 >> (bits + 1)) << (bits + 1)
    earlier = (experts[:, None] < experts[None, :])[None, None]
    place = jnp.sum((has_piece[..., None] & earlier).astype(i32), axis=2)
    listed = has_piece[..., None] & (place[..., None] == experts[None, None, None, :])
    compact = lambda v: jnp.sum(jnp.where(listed, v[..., None], 0), axis=2)
    piece_src = compact((tile_src[:, None, :] + covered) * SLABS)
    piece_dst = compact((tile_dst[:, None, :] + covered) * SLABS)
    piece_cnt = jnp.sum(has_piece.astype(i32), axis=2)
    tile_table = jnp.concatenate(
        [tile_src - seen, jnp.pad(piece_cnt, ((0, 0), (0, N_EXPERTS - RUN_BITS))),
         piece_src.reshape(nt, -1), piece_dst.reshape(nt, -1)], axis=1)
    tables = (block_e, block_seg, next_e, block_valid, n_used)
    return (tile_table.reshape(nt, 1, RUN_TABLE).astype(i32),) + tuple(
        t.astype(i32) for t in tables)


def kernel(x_prompt, x_sample, p_prompt, p_sample, state_gla, state_hgrn, g_mix, w_in, w_gk2, b_gk2,
           gn_gla, lb_logits, gn_hgrn, w_out, g_ffn, w_router, b_router, w_gu, b_gu, w_dn, b_dn,
           g_ple, w_ple_gate, w_ple_proj, g_final):
    batch, seq, _ = x_prompt.shape
    dec_batch, dec_seq, _ = x_sample.shape
    tp, ts = batch * seq, dec_batch * dec_seq
    xp = x_prompt.reshape(tp, D_MODEL)
    xs = x_sample.reshape(ts, D_MODEL)

    wgk = jnp.concatenate(
        [w_gk2[0], jnp.zeros((R1_PAD - GLA_LOWRANK, GLA_KDIM), f32)], axis=0).astype(bf16)
    wr = w_router[0].T.astype(bf16)
    br = b_router[0].reshape(N_EXPERTS, 1)

    z = _in_proj(xp, xs, g_mix[0].reshape(1, D_MODEL), w_in[0])

    mixer_params = (wgk, b_gk2[0].reshape(1, GLA_KDIM), gn_gla[0].reshape(1, GLA_DV),
                    lb_logits[0:2], gn_hgrn[0].reshape(1, HGRN_DV))
    o_p, new_gla_p, new_hgrn_p = _mixer_prompt(
        z, *mixer_params, batch=batch, seq=seq, rows_per_step=512, block_rows=256)
    z_s = jnp.pad(z[tp:].reshape(dec_batch, dec_seq, Z_W), ((0, 0), (0, SAMPLE_PAD_T - dec_seq), (0, 0)))
    o_s, new_gla_s, new_hgrn_s = _mixer_sample(
        z_s.reshape(dec_batch * SAMPLE_PAD_T, Z_W), state_gla, state_hgrn, *mixer_params,
        steps=dec_seq, batch_block=8)
    o_s = o_s.reshape(dec_batch, SAMPLE_PAD_T, D_MODEL)[:, :dec_seq].reshape(ts, D_MODEL).astype(bf16)

    h1, xn, top_i, top_g, rank, top_ic, rank_c, counts, part_counts = _out_proj(
        xp, xs, o_p, o_s, w_out[0].astype(bf16), g_ffn[0].reshape(1, D_MODEL), wr, br)

    t = tp + ts
    n_blocks = -(-(t * TOP_K) // MOE_TILE) + N_EXPERTS
    tile_table, block_e, block_seg, next_e, block_valid, n_used = _routing_tables(
        top_i, counts[:, 0], part_counts, n_blocks)
    x_sorted = _dispatch(block_valid, tile_table, top_i, rank, xn)
    y_sorted = _experts(block_e, block_seg, next_e, block_valid, n_used, x_sorted,
                        w_gu[0], b_gu[0].reshape(N_EXPERTS, 1, 2 * D_FF),
                        w_dn[0], b_dn[0].reshape(N_EXPERTS, 1, D_MODEL))

    y_p, y_s = _combine(tile_table, y_sorted, h1, top_g, top_ic, rank_c, p_prompt[0].reshape(tp, PLE_DIM),
                        p_sample[0].reshape(ts, PLE_DIM), g_ple[0].reshape(1, D_MODEL),
                        w_ple_gate[0].astype(bf16), w_ple_proj[0].astype(bf16), g_final.reshape(1, D_MODEL))

    return (y_p.reshape(batch, seq, D_MODEL), y_s.reshape(dec_batch, dec_seq, D_MODEL),
            new_gla_p, new_hgrn_p, new_gla_s, new_hgrn_s)
```

```python
import functools

import jax
import jax.numpy as jnp
from jax import lax
from jax.experimental import pallas as pl
from jax.experimental.pallas import tpu as pltpu

f32 = jnp.float32
bf16 = jnp.bfloat16
i32 = jnp.int32

D_MODEL = 1024
GLA_HEADS = 4
GLA_DK = 64
GLA_DV = 128
GLA_KDIM = GLA_HEADS * GLA_DK
GLA_WIDTH = GLA_HEADS * GLA_DV
GLA_LOWRANK = 16
GLA_NORMALIZER = 16.0
HGRN_HEADS = 4
HGRN_DK = 128
HGRN_DV = 128
HGRN_FDIM = HGRN_HEADS * HGRN_DK
HGRN_WIDTH = HGRN_HEADS * HGRN_DV
CHUNK = 64
N_EXPERTS = 32
TOP_K = 4
D_FF = 1024
SWIGLU_LIMIT = 7.0
SWIGLU_ALPHA = 1.702
PLE_DIM = 256
EPS = 1e-6

LANES = 128
SAMPLE_PAD_T = 8
SLABS = D_MODEL // LANES
XSLABS = SLABS // 2
ROUTE_ROWS = 8
R1_PAD = LANES
Z_Q1 = 0
Z_K1 = Z_Q1 + GLA_KDIM
Z_V1 = Z_K1 + GLA_KDIM
Z_G1 = Z_V1 + GLA_WIDTH
Z_R1 = Z_G1 + GLA_WIDTH
Z_Q2 = Z_R1 + R1_PAD
Z_F2 = Z_Q2 + HGRN_FDIM
Z_I2 = Z_F2 + HGRN_FDIM
Z_G2 = Z_I2 + HGRN_WIDTH
Z_W = Z_G2 + HGRN_WIDTH

TOKEN_TILE = 512
MOE_TILE = 512
DISPATCH_TILE = 256
COMBINE_TILE = DISPATCH_TILE
RUN_BITS = DISPATCH_TILE.bit_length()
RUN_SHIFT = 0
RUN_COUNT = RUN_SHIFT + N_EXPERTS
RUN_SRC = RUN_COUNT + N_EXPERTS
RUN_DST = RUN_SRC + RUN_BITS * N_EXPERTS
RUN_TABLE = RUN_DST + RUN_BITS * N_EXPERTS
VMEM_LIMIT = 56 * 1024 * 1024


def _rms(x, g):
    return x * lax.rsqrt(jnp.mean(x * x, axis=-1, keepdims=True) + EPS) * g


def _sigmoid(x):
    return 0.5 * jnp.tanh(0.5 * x) + 0.5


def _dot(a, b):
    return jnp.dot(a, b, preferred_element_type=f32)


def _dot_nt(a, b):
    return lax.dot_general(a, b, (((1,), (1,)), ((), ())), preferred_element_type=f32)


def _dot_tn(a, b):
    return lax.dot_general(a, b, (((0,), (0,)), ((), ())), preferred_element_type=f32)


def _cumsum_rows(tri, x):
    hi = x.astype(bf16)
    r1 = x - hi.astype(f32)
    mid = r1.astype(bf16)
    lo = (r1 - mid.astype(f32)).astype(bf16)
    return _dot(tri, hi) + _dot(tri, mid) + _dot(tri, lo)


def _select_rows(is_prompt, a_ref, b_ref):
    return jnp.where(is_prompt, a_ref[...], b_ref[...])


def _inproj_kernel(xp_ref, xs_ref, g_ref, w_hbm, z_ref, w_f32, w_bf, sem,
                   *, n_prompt_tiles, col_chunk, cast_rows):
    @pl.when(pl.program_id(0) == 0)
    def _():
        copy = pltpu.make_async_copy(w_hbm, w_f32, sem)
        copy.start()
        copy.wait()
        r1_end = Z_R1 + GLA_LOWRANK

        def cast(j, carry):
            rows = pl.ds(pl.multiple_of(j * cast_rows, cast_rows), cast_rows)
            w_bf[rows, 0:Z_R1] = w_f32[rows, 0:Z_R1].astype(bf16)
            w_bf[rows, Z_R1:Z_Q2] = jnp.concatenate(
                [w_f32[rows, Z_R1:r1_end], jnp.zeros((cast_rows, R1_PAD - GLA_LOWRANK), f32)],
                axis=1).astype(bf16)
            w_bf[rows, Z_Q2:Z_W] = w_f32[rows, r1_end:].astype(bf16)
            return carry
        lax.fori_loop(0, D_MODEL // cast_rows, cast, 0)

    is_prompt = pl.program_id(0) < n_prompt_tiles
    x = _select_rows(is_prompt, xp_ref, xs_ref)
    a = _rms(x, g_ref[...]).astype(bf16)
    for c0 in range(0, Z_W, col_chunk):
        c1 = min(c0 + col_chunk, Z_W)
        z_ref[:, c0:c1] = _dot(a, w_bf[:, c0:c1]).astype(bf16)


def _in_proj(xp, xs, g, w):
    tp, ts = xp.shape[0], xs.shape[0]
    npt = tp // TOKEN_TILE
    nt = npt + ts // TOKEN_TILE
    return pl.pallas_call(
        functools.partial(_inproj_kernel, n_prompt_tiles=npt, col_chunk=512, cast_rows=64),
        grid=(nt,),
        in_specs=[
            pl.BlockSpec((TOKEN_TILE, D_MODEL), lambda i: (jnp.minimum(i, npt - 1), 0)),
            pl.BlockSpec((TOKEN_TILE, D_MODEL), lambda i: (jnp.maximum(i - npt, 0), 0)),
            pl.BlockSpec((1, D_MODEL), lambda i: (0, 0)),
            pl.BlockSpec(memory_space=pl.ANY),
        ],
        out_specs=pl.BlockSpec((TOKEN_TILE, Z_W), lambda i: (i, 0)),
        out_shape=jax.ShapeDtypeStruct((tp + ts, Z_W), bf16),
        scratch_shapes=[
            pltpu.VMEM(w.shape, f32),
            pltpu.VMEM((D_MODEL, Z_W), bf16),
            pltpu.SemaphoreType.DMA(()),
        ],
        compiler_params=pltpu.CompilerParams(
            dimension_semantics=("arbitrary",), vmem_limit_bytes=VMEM_LIMIT),
        name="in_proj",
    )(xp, xs, g, w)


def _block_consts(r, c):
    shift = c.bit_length() - 1
    row = lax.broadcasted_iota(i32, (r, r), 0)
    col = lax.broadcasted_iota(i32, (r, r), 1)
    causal = (lax.shift_right_logical(row, shift) == lax.shift_right_logical(col, shift)) & (row >= col)
    return causal.astype(bf16), causal


def _subchunk_row(x, c, idx):
    parts = [jnp.broadcast_to(x[j * c + idx:j * c + idx + 1, :], (c, x.shape[1]))
             for j in range(x.shape[0] // c)]
    return jnp.concatenate(parts, axis=0)


def _gated_block(q, k, v, log_a, states, *, c, scale, ref_row, tri, causal, heads_per_group, scan):
    r = q.shape[0]
    nsub = r // c
    groups = q.shape[1] // LANES
    b = _cumsum_rows(tri, log_a)
    b_ref = _subchunk_row(b, c, ref_row)
    b_last = _subchunk_row(b, c, c - 1)
    qs = q * scale
    q_r = qs * jnp.exp(b - b_ref)
    k_r = (k * jnp.exp(b_ref - b)).astype(bf16)
    k_end = k * jnp.exp(b_last - b)
    q_b = qs * jnp.exp(b)
    decay = jnp.exp(b_last)
    vb = v.astype(bf16)
    lane = lax.broadcasted_iota(i32, (r, LANES), 1)
    width = LANES // heads_per_group
    outs, new_states = [], []
    for g in range(groups):
        gl = slice(g * LANES, (g + 1) * LANES)
        masks = [None] if heads_per_group == 1 else [
            (lane >= j * width) & (lane < (j + 1) * width) for j in range(heads_per_group)]
        pick = lambda x, m: x if m is None else jnp.where(m, x, 0.0)
        q_r_h = [pick(q_r[:, gl], m).astype(bf16) for m in masks]
        q_b_h = [pick(q_b[:, gl], m).astype(bf16) for m in masks]
        k_end_h = [pick(k_end[:, gl], m).astype(bf16) for m in masks]
        v_h = [vb[:, (g * heads_per_group + j) * LANES:(g * heads_per_group + j + 1) * LANES]
               for j in range(heads_per_group)]
        o_intra = []
        for j in range(heads_per_group):
            att = _dot_nt(q_r_h[j], k_r[:, gl])
            att = jnp.where(causal, att, 0.0).astype(bf16)
            o_intra.append(_dot(att, v_h[j]))
        upd = []
        for s in range(nsub):
            rs = slice(s * c, (s + 1) * c)
            u = _dot_tn(v_h[0][rs], k_end_h[0][rs])
            for j in range(1, heads_per_group):
                u = u + _dot_tn(v_h[j][rs], k_end_h[j][rs])
            upd.append(u)
        o_inter = [[] for _ in range(heads_per_group)]
        st = states[g] if scan else None
        group_states = []
        for s in range(nsub):
            rs = slice(s * c, (s + 1) * c)
            if not scan:
                st = states[s][g]
            st_b = st.astype(bf16)
            for j in range(heads_per_group):
                o_inter[j].append(_dot_nt(q_b_h[j][rs], st_b))
            st = st * decay[s * c:s * c + 1, gl] + upd[s]
            if not scan:
                group_states.append(st)
        for j in range(heads_per_group):
            outs.append(o_intra[j] + jnp.concatenate(o_inter[j], axis=0))
        new_states.append(st if scan else group_states)
    if not scan:
        new_states = [[new_states[g][s] for g in range(groups)] for s in range(nsub)]
    return outs, new_states


def _head_norm(o, g):
    return o * lax.rsqrt(jnp.mean(o * o, axis=-1, keepdims=True) + EPS) * g


def _mixer_block(zb, st_gla, st_hgrn, wgk, bgk, gng, lb, gnh, *, c, ref_row, tri, causal, valid, scan):
    z = zb.astype(f32)
    gk = _dot(zb[:, Z_R1:Z_R1 + R1_PAD], wgk) + bgk
    log_a = (jnp.minimum(gk, 0.0) - jnp.log(1.0 + jnp.exp(-jnp.abs(gk)))) / GLA_NORMALIZER
    k1 = z[:, Z_K1:Z_K1 + GLA_KDIM]
    if valid is not None:
        log_a = jnp.where(valid, log_a, 0.0)
        k1 = jnp.where(valid, k1, 0.0)
    o1, st_gla = _gated_block(
        z[:, Z_Q1:Z_Q1 + GLA_KDIM], k1, z[:, Z_V1:Z_V1 + GLA_WIDTH], log_a, st_gla,
        c=c, scale=GLA_DK ** -0.5, ref_row=ref_row, tri=tri, causal=causal,
        heads_per_group=LANES // GLA_DK, scan=scan)
    forget = lb + (1.0 - lb) * _sigmoid(z[:, Z_F2:Z_F2 + HGRN_FDIM])
    k2 = 1.0 - forget
    log_f = jnp.log(forget)
    if valid is not None:
        log_f = jnp.where(valid, log_f, 0.0)
        k2 = jnp.where(valid, k2, 0.0)
    q2 = z[:, Z_Q2:Z_Q2 + HGRN_FDIM]
    i2 = z[:, Z_I2:Z_I2 + HGRN_WIDTH]
    o2, st_hgrn = _gated_block(
        q2 * _sigmoid(q2), k2, i2 * _sigmoid(i2), log_f, st_hgrn,
        c=c, scale=1.0, ref_row=ref_row, tri=tri, causal=causal, heads_per_group=1, scan=scan)
    cols = []
    for h in range(GLA_HEADS):
        g1 = z[:, Z_G1 + h * GLA_DV:Z_G1 + (h + 1) * GLA_DV]
        cols.append(_head_norm(o1[h], gng) * (g1 * _sigmoid(g1)))
    for h in range(HGRN_HEADS):
        g2 = z[:, Z_G2 + h * HGRN_DV:Z_G2 + (h + 1) * HGRN_DV]
        cols.append(_head_norm(o2[h], gnh) * _sigmoid(g2))
    return jnp.concatenate(cols, axis=1), st_gla, st_hgrn


def _lower_bound(lbl_ref):
    l = lbl_ref[...]
    m = jnp.max(l, axis=0, keepdims=True)
    e = jnp.exp(l - m)
    return e[0:1, :] / jnp.sum(e, axis=0, keepdims=True)


GLA_GROUPS = GLA_KDIM // LANES


def _gla_state_t(s_ref_heads):
    return s_ref_heads.reshape(LANES, GLA_DV).T


def _gla_state_from_t(st):
    return st.T.reshape(LANES // GLA_DK, GLA_DK, GLA_DV)


def _mixer_prompt_kernel(z_ref, wgk_ref, bgk_ref, gng_ref, lbl_ref, gnh_ref,
                         o_ref, sg_out, sh_out, sg_t, sh_t, *, block_rows):
    j = pl.program_id(1)

    @pl.when(j == 0)
    def _():
        sg_t[...] = jnp.zeros_like(sg_t)
        sh_t[...] = jnp.zeros_like(sh_t)

    tri, causal = _block_consts(block_rows, CHUNK)
    lb = _lower_bound(lbl_ref)
    wgk, bgk, gng, gnh = wgk_ref[...], bgk_ref[...], gng_ref[...], gnh_ref[...]

    def body(blk, carry):
        r0 = pl.multiple_of(blk * block_rows, block_rows)
        o, st_g, st_h = _mixer_block(
            z_ref[pl.ds(r0, block_rows), :], [sg_t[g] for g in range(GLA_GROUPS)],
            [sh_t[h] for h in range(HGRN_HEADS)], wgk, bgk, gng, lb, gnh,
            c=CHUNK, ref_row=CHUNK // 2, tri=tri, causal=causal, valid=None, scan=True)
        o_ref[pl.ds(r0, block_rows), :] = o.astype(bf16)
        for g in range(GLA_GROUPS):
            sg_t[g] = st_g[g]
        for h in range(HGRN_HEADS):
            sh_t[h] = st_h[h]
        return carry

    lax.fori_loop(0, z_ref.shape[0] // block_rows, body, 0)

    @pl.when(j == pl.num_programs(1) - 1)
    def _():
        per = LANES // GLA_DK
        for g in range(GLA_GROUPS):
            sg_out[0, 0, g * per:(g + 1) * per] = _gla_state_from_t(sg_t[g])
        for h in range(HGRN_HEADS):
            sh_out[0, 0, h] = sh_t[h].T


def _mixer_prompt(z, wgk, bgk, gng, lbl, gnh, *, batch, seq, rows_per_step, block_rows):
    nj = seq // rows_per_step
    full = lambda shape: pl.BlockSpec(shape, lambda b, j: (0,) * len(shape))
    return pl.pallas_call(
        functools.partial(_mixer_prompt_kernel, block_rows=block_rows),
        grid=(batch, nj),
        in_specs=[
            pl.BlockSpec((rows_per_step, Z_W), lambda b, j: (b * nj + j, 0)),
            full((R1_PAD, GLA_KDIM)), full((1, GLA_KDIM)), full((1, GLA_DV)),
            full((2, HGRN_FDIM)), full((1, HGRN_DV)),
        ],
        out_specs=[
            pl.BlockSpec((rows_per_step, D_MODEL), lambda b, j: (b * nj + j, 0)),
            pl.BlockSpec((1, 1, GLA_HEADS, GLA_DK, GLA_DV), lambda b, j: (0, b, 0, 0, 0)),
            pl.BlockSpec((1, 1, HGRN_HEADS, HGRN_DK, HGRN_DV), lambda b, j: (0, b, 0, 0, 0)),
        ],
        out_shape=[
            jax.ShapeDtypeStruct((batch * seq, D_MODEL), bf16),
            jax.ShapeDtypeStruct((1, batch, GLA_HEADS, GLA_DK, GLA_DV), f32),
            jax.ShapeDtypeStruct((1, batch, HGRN_HEADS, HGRN_DK, HGRN_DV), f32),
        ],
        scratch_shapes=[
            pltpu.VMEM((GLA_GROUPS, GLA_DV, LANES), f32),
            pltpu.VMEM((HGRN_HEADS, HGRN_DV, HGRN_DK), f32),
        ],
        compiler_params=pltpu.CompilerParams(
            dimension_semantics=("arbitrary", "arbitrary"), vmem_limit_bytes=VMEM_LIMIT),
        name="mixer_prompt",
    )(z, wgk, bgk, gng, lbl, gnh)


def _mixer_sample_kernel(z_ref, sg_in, sh_in, wgk_ref, bgk_ref, gng_ref, lbl_ref, gnh_ref,
                         o_ref, sg_out, sh_out, *, batch_block, steps):
    rows = batch_block * SAMPLE_PAD_T
    tri, causal = _block_consts(rows, SAMPLE_PAD_T)
    step = lax.broadcasted_iota(i32, (rows, 1), 0) & (SAMPLE_PAD_T - 1)
    valid = step < steps
    lb = _lower_bound(lbl_ref)
    per = LANES // GLA_DK
    st_g = [[_gla_state_t(sg_in[0, s, g * per:(g + 1) * per]) for g in range(GLA_GROUPS)]
            for s in range(batch_block)]
    st_h = [[sh_in[0, s, h].T for h in range(HGRN_HEADS)] for s in range(batch_block)]
    o, st_g, st_h = _mixer_block(
        z_ref[...], st_g, st_h, wgk_ref[...], bgk_ref[...], gng_ref[...], lb, gnh_ref[...],
        c=SAMPLE_PAD_T, ref_row=steps // 2, tri=tri, causal=causal, valid=valid, scan=False)
    o_ref[...] = o
    for s in range(batch_block):
        for g in range(GLA_GROUPS):
            sg_out[0, s, g * per:(g + 1) * per] = _gla_state_from_t(st_g[s][g])
        for h in range(HGRN_HEADS):
            sh_out[0, s, h] = st_h[s][h].T


def _mixer_sample(z2, state_gla, state_hgrn, wgk, bgk, gng, lbl, gnh, *, steps, batch_block):
    batch = z2.shape[0] // SAMPLE_PAD_T
    rows = batch_block * SAMPLE_PAD_T
    full = lambda shape: pl.BlockSpec(shape, lambda i: (0,) * len(shape))
    sg_spec = pl.BlockSpec((1, batch_block, GLA_HEADS, GLA_DK, GLA_DV), lambda i: (0, i, 0, 0, 0))
    sh_spec = pl.BlockSpec((1, batch_block, HGRN_HEADS, HGRN_DK, HGRN_DV), lambda i: (0, i, 0, 0, 0))
    return pl.pallas_call(
        functools.partial(_mixer_sample_kernel, batch_block=batch_block, steps=steps),
        grid=(batch // batch_block,),
        in_specs=[
            pl.BlockSpec((rows, Z_W), lambda i: (i, 0)),
            sg_spec, sh_spec,
            full((R1_PAD, GLA_KDIM)), full((1, GLA_KDIM)), full((1, GLA_DV)),
            full((2, HGRN_FDIM)), full((1, HGRN_DV)),
        ],
        out_specs=[pl.BlockSpec((rows, D_MODEL), lambda i: (i, 0)), sg_spec, sh_spec],
        out_shape=[
            jax.ShapeDtypeStruct((batch * SAMPLE_PAD_T, D_MODEL), f32),
            jax.ShapeDtypeStruct(state_gla.shape, f32),
            jax.ShapeDtypeStruct(state_hgrn.shape, f32),
        ],
        compiler_params=pltpu.CompilerParams(
            dimension_semantics=("arbitrary",), vmem_limit_bytes=VMEM_LIMIT),
        name="mixer_sample",
    )(z2, state_gla, state_hgrn, wgk, bgk, gng, lbl, gnh)


def _load_slab_cols(ref, start, rows, s):
    return ref[pl.ds(start + s, rows, stride=SLABS), :]


def _outproj_kernel(xp_ref, xs_ref, op_ref, os_ref, wo_ref, g_ref, wr_ref, br_ref,
                    h_ref, xn_ref, ti_ref, tg_ref, rank_ref, tic_ref, rankc_ref, cnt_ref, part_ref, run_cnt,
                    *, n_prompt_tiles):
    i = pl.program_id(0)

    @pl.when(i == 0)
    def _():
        run_cnt[...] = jnp.zeros_like(run_cnt)

    x = _select_rows(i < n_prompt_tiles, xp_ref, xs_ref)
    o = _select_rows(i < n_prompt_tiles, op_ref, os_ref)
    h = x + _dot(o, wo_ref[...])
    h_ref[...] = h
    xn = _rms(h, g_ref[...])
    xn_ref[...] = xn.astype(bf16)
    l = _dot_nt(wr_ref[...], xn.astype(bf16)) + br_ref[...]
    expert = lax.broadcasted_iota(i32, l.shape, 0)
    neg = jnp.float32(-jnp.inf)
    vals, idxs = [], []
    for _ in range(TOP_K):
        m = jnp.max(l, axis=0, keepdims=True)
        idx = jnp.min(jnp.where(l == m, expert, N_EXPERTS), axis=0, keepdims=True)
        vals.append(m)
        idxs.append(idx)
        l = jnp.where(expert == idx, neg, l)
    es = [jnp.exp(v - vals[0]) for v in vals]
    denom = es[0] + es[1] + es[2] + es[3]

    onehot = jnp.zeros(l.shape, f32)
    for k in range(TOP_K):
        onehot = onehot + (expert == idxs[k]).astype(f32)
    n = l.shape[1]
    earlier = (lax.broadcasted_iota(i32, (n, n), 0) < lax.broadcasted_iota(i32, (n, n), 1)).astype(bf16)
    ahead = _dot(onehot.astype(bf16), earlier) + run_cnt[:, 0:1]
    run_cnt[...] = run_cnt[...] + jnp.sum(onehot, axis=1, keepdims=True)
    cnt_ref[...] = run_cnt[...].astype(i32)
    lane = lax.broadcasted_iota(i32, (N_EXPERTS, LANES), 1)
    parts = jnp.zeros((N_EXPERTS, LANES), f32)
    for p in range(n // DISPATCH_TILE):
        c = jnp.sum(onehot[:, p * DISPATCH_TILE:(p + 1) * DISPATCH_TILE], axis=1, keepdims=True)
        parts = jnp.where(lane == p, c, parts)
    part_ref[0] = parts.astype(i32)

    ranks = [jnp.sum(jnp.where(expert == idxs[k], ahead, 0.0), axis=0, keepdims=True).astype(i32)
             for k in range(TOP_K)]
    pad_i = jnp.zeros((ROUTE_ROWS - TOP_K, n), i32)
    ti_ref[...] = jnp.concatenate(idxs + [pad_i], axis=0)
    rank_ref[...] = jnp.concatenate(ranks + [pad_i], axis=0)
    gate_rows = jnp.concatenate([e / denom for e in es] + [jnp.zeros((LANES - TOP_K, n), f32)], axis=0)
    tg_ref[...] = gate_rows.T
    pad_wide = jnp.zeros((LANES - TOP_K, n), i32)
    tic_ref[...] = jnp.concatenate(idxs + [pad_wide], axis=0).T
    rankc_ref[...] = jnp.concatenate(ranks + [pad_wide], axis=0).T


def _out_proj(xp, xs, o_p, o_s, wo, g, wr, br):
    tp, ts = xp.shape[0], xs.shape[0]
    npt = tp // TOKEN_TILE
    nt = npt + ts // TOKEN_TILE
    t = tp + ts
    row = lambda w: pl.BlockSpec((TOKEN_TILE, w), lambda i: (i, 0))
    route = pl.BlockSpec((ROUTE_ROWS, TOKEN_TILE), lambda i: (0, i))
    full = lambda shape: pl.BlockSpec(shape, lambda i: (0,) * len(shape))
    prompt_rows = pl.BlockSpec((TOKEN_TILE, D_MODEL), lambda i: (jnp.minimum(i, npt - 1), 0))
    sample_rows = pl.BlockSpec((TOKEN_TILE, D_MODEL), lambda i: (jnp.maximum(i - npt, 0), 0))
    return pl.pallas_call(
        functools.partial(_outproj_kernel, n_prompt_tiles=npt),
        grid=(nt,),
        in_specs=[
            prompt_rows, sample_rows, prompt_rows, sample_rows,
            full((D_MODEL, D_MODEL)), full((1, D_MODEL)),
            full((N_EXPERTS, D_MODEL)), full((N_EXPERTS, 1)),
        ],
        out_specs=[row(D_MODEL), row(D_MODEL),
                   route, row(LANES), route, row(LANES), row(LANES), full((N_EXPERTS, LANES)),
                   pl.BlockSpec((1, N_EXPERTS, LANES), lambda i: (i, 0, 0))],
        out_shape=[
            jax.ShapeDtypeStruct((t, D_MODEL), f32),
            jax.ShapeDtypeStruct((t, D_MODEL), bf16),
            jax.ShapeDtypeStruct((ROUTE_ROWS, t), i32),
            jax.ShapeDtypeStruct((t, LANES), f32),
            jax.ShapeDtypeStruct((ROUTE_ROWS, t), i32),
            jax.ShapeDtypeStruct((t, LANES), i32),
            jax.ShapeDtypeStruct((t, LANES), i32),
            jax.ShapeDtypeStruct((N_EXPERTS, LANES), i32),
            jax.ShapeDtypeStruct((nt, N_EXPERTS, LANES), i32),
        ],
        scratch_shapes=[pltpu.VMEM((N_EXPERTS, LANES), f32)],
        compiler_params=pltpu.CompilerParams(
            dimension_semantics=("arbitrary",), vmem_limit_bytes=VMEM_LIMIT),
        name="out_proj_router",
    )(xp, xs, o_p, o_s, wo, g, wr, br)


def _dispatch_kernel(bv_ref, nz_ref, tab_ref, ti_ref, rank_ref, xn_ref, xs_hbm, zeros, stage, sems, zsem,
                     *, n_blocks):
    @pl.when(pl.program_id(0) == 0)
    def _():
        zeros[...] = jnp.zeros_like(zeros)

        def fill(b, carry):
            @pl.when(bv_ref[b] < MOE_TILE)
            def _():
                r0 = pl.multiple_of(b * (MOE_TILE * XSLABS), MOE_TILE * XSLABS)
                pltpu.make_async_copy(zeros, xs_hbm.at[pl.ds(r0, MOE_TILE * XSLABS)], zsem).start()
            return carry

        def drain(b, carry):
            pltpu.make_async_copy(zeros, xs_hbm.at[pl.ds(0, MOE_TILE * XSLABS)], zsem).wait()
            return carry

        lax.fori_loop(0, n_blocks, fill, 0)
        lax.fori_loop(0, nz_ref[0], drain, 0)

    i = pl.program_id(0)
    nt = pl.num_programs(0)
    n = DISPATCH_TILE * TOP_K
    slot_rows = n * XSLABS
    slot = i % 2
    base = pl.multiple_of(slot * slot_rows, slot_rows)

    def wait_rows(s):
        pltpu.make_async_copy(stage.at[pl.ds(0, slot_rows)], xs_hbm.at[pl.ds(0, slot_rows)], sems.at[s]).wait()

    ti = ti_ref[...]
    shift = jnp.zeros(ti.shape, i32)
    for e in range(N_EXPERTS):
        shift = jnp.where(ti == e, tab_ref[0, 0, RUN_SHIFT + e], shift)
    staged = rank_ref[...] + shift
    row = lax.broadcasted_iota(i32, (n, DISPATCH_TILE), 0)
    pick = row == staged[0:1, :]
    for k in range(1, TOP_K):
        pick = pick | (row == staged[k:k + 1, :])
    rows_sorted = _dot(pick.astype(bf16), xn_ref[...])
    bits = lax.bitcast_convert_type(rows_sorted, jnp.uint32)
    half = D_MODEL // 2
    packed = (bits[:, half:] & jnp.uint32(0xFFFF0000)) | (bits[:, :half] >> 16)

    for s in range(XSLABS):
        stage[pl.ds(base + s, n, stride=XSLABS), :] = packed[:, s * LANES:(s + 1) * LANES]

    for bit in range(RUN_BITS - 1, -1, -1):
        size = (1 << bit) * XSLABS

        def issue(p, carry, bit=bit, size=size):
            src0 = pl.multiple_of(tab_ref[0, 0, RUN_SRC + bit * N_EXPERTS + p] * XSLABS, XSLABS)
            dst0 = pl.multiple_of(tab_ref[0, 0, RUN_DST + bit * N_EXPERTS + p] * XSLABS, XSLABS)
            pltpu.make_async_copy(stage.at[pl.ds(base + src0, size)], xs_hbm.at[pl.ds(dst0, size)],
                                  sems.at[slot]).start(priority=bit % 2)
            return carry

        lax.fori_loop(0, tab_ref[0, 0, RUN_COUNT + bit], issue, 0)

    @pl.when(i > 0)
    def _():
        wait_rows(1 - slot)

    @pl.when(i == nt - 1)
    def _():
        wait_rows(slot)


def _dispatch(block_valid, tile_table, top_i, rank, xn):
    t = xn.shape[0]
    nt = t // DISPATCH_TILE
    n_blocks = block_valid.shape[0]
    n_zero = jnp.sum((block_valid < MOE_TILE).astype(i32)).reshape(1)
    route = pl.BlockSpec((ROUTE_ROWS, DISPATCH_TILE), lambda i, bv, nz: (0, i))
    grid_spec = pltpu.PrefetchScalarGridSpec(
        num_scalar_prefetch=2,
        grid=(nt,),
        in_specs=[
            pl.BlockSpec((1, 1, RUN_TABLE), lambda i, bv, nz: (i, 0, 0), memory_space=pltpu.SMEM),
            route, route,
            pl.BlockSpec((DISPATCH_TILE, D_MODEL), lambda i, bv, nz: (i, 0)),
        ],
        out_specs=pl.BlockSpec(memory_space=pl.ANY),
        scratch_shapes=[
            pltpu.VMEM((MOE_TILE * XSLABS, LANES), jnp.uint32),
            pltpu.VMEM((2 * DISPATCH_TILE * TOP_K * XSLABS, LANES), jnp.uint32),
            pltpu.SemaphoreType.DMA((2,)),
            pltpu.SemaphoreType.DMA(()),
        ],
    )
    return pl.pallas_call(
        functools.partial(_dispatch_kernel, n_blocks=n_blocks),
        grid_spec=grid_spec,
        out_shape=jax.ShapeDtypeStruct((n_blocks * MOE_TILE * XSLABS, LANES), jnp.uint32),
        compiler_params=pltpu.CompilerParams(
            dimension_semantics=("arbitrary",), vmem_limit_bytes=VMEM_LIMIT),
        name="moe_dispatch",
    )(block_valid, n_zero, tile_table, top_i, rank, xn)


def _expert_kernel(be_ref, seg_ref, nxt_ref, bv_ref, nb_ref, x_ref, wgu_hbm, bgu_ref, wdn_hbm, bdn_ref, y_ref,
                   wgu_f32, wdn_f32, wgu_bf, wdn_bf, sems, *, ff_chunk, cast_rows):
    i = pl.program_id(0)

    @pl.when(i >= nb_ref[0])
    def _():
        y_ref[...] = jnp.zeros_like(y_ref)

    def fetch(e, slot):
        return (pltpu.make_async_copy(wgu_hbm.at[e], wgu_f32.at[slot], sems.at[slot, 0]),
                pltpu.make_async_copy(wdn_hbm.at[e], wdn_f32.at[slot], sems.at[slot, 1]))

    @pl.when((i < nb_ref[0]) & ((i == 0) | (be_ref[i] != be_ref[jnp.maximum(i - 1, 0)])))
    def _():
        slot = seg_ref[i] % 2

        @pl.when(i == 0)
        def _():
            for c in fetch(be_ref[0], 0):
                c.start()

        for c in fetch(be_ref[i], slot):
            c.wait()

        @pl.when(nxt_ref[i] >= 0)
        def _():
            for c in fetch(nxt_ref[i], 1 - slot):
                c.start()

        def cast(j, carry):
            r0 = pl.multiple_of(j * cast_rows, cast_rows)
            wgu_bf[pl.ds(r0, cast_rows), :] = wgu_f32[slot, pl.ds(r0, cast_rows), :].astype(bf16)
            wdn_bf[pl.ds(r0, cast_rows), :] = wdn_f32[slot, pl.ds(r0, cast_rows), :].astype(bf16)
            return carry
        lax.fori_loop(0, D_MODEL // cast_rows, cast, 0)

    def mlp(rows):
        words = [x_ref[pl.ds(s, rows, stride=XSLABS), :] for s in range(XSLABS)]
        unpack = lambda w: lax.bitcast_convert_type(w, f32).astype(bf16)
        xb = jnp.concatenate([unpack(w << 16) for w in words]
                             + [unpack(w & jnp.uint32(0xFFFF0000)) for w in words], axis=1)
        hid = []
        for c0 in range(0, D_FF, ff_chunk):
            gate = _dot(xb, wgu_bf[:, c0:c0 + ff_chunk]) + bgu_ref[0, :, c0:c0 + ff_chunk]
            up = (_dot(xb, wgu_bf[:, D_FF + c0:D_FF + c0 + ff_chunk])
                  + bgu_ref[0, :, D_FF + c0:D_FF + c0 + ff_chunk])
            gate = jnp.minimum(gate, SWIGLU_LIMIT)
            up = jnp.clip(up, -SWIGLU_LIMIT, SWIGLU_LIMIT)
            hid.append(((up + 1.0) * (gate * _sigmoid(gate * SWIGLU_ALPHA))).astype(bf16))
        hid = jnp.concatenate(hid, axis=1)
        for n0 in range(0, D_MODEL, ff_chunk):
            y = _dot(hid, wdn_bf[:, n0:n0 + ff_chunk]) + bdn_ref[0, :, n0:n0 + ff_chunk]
            for s in range(n0 // LANES, (n0 + ff_chunk) // LANES):
                y_ref[pl.ds(s, rows, stride=SLABS), :] = y[:, s * LANES - n0:(s + 1) * LANES - n0]

    half = MOE_TILE // 2

    @pl.when((i < nb_ref[0]) & (bv_ref[i] > half))
    def _():
        mlp(MOE_TILE)

    @pl.when((i < nb_ref[0]) & (bv_ref[i] <= half))
    def _():
        mlp(half)
        y_ref[pl.ds(half * SLABS, half * SLABS), :] = jnp.zeros((half * SLABS, LANES), f32)


def _experts(block_e, block_seg, next_e, block_valid, n_used, xs, wgu, bgu, wdn, bdn):
    n_blocks = block_e.shape[0]
    grid_spec = pltpu.PrefetchScalarGridSpec(
        num_scalar_prefetch=5,
        grid=(n_blocks,),
        in_specs=[
            pl.BlockSpec((MOE_TILE * XSLABS, LANES),
                         lambda i, be, sg, nx, bv, nb: (jnp.minimum(i, nb[0] - 1), 0)),
            pl.BlockSpec(memory_space=pl.ANY),
            pl.BlockSpec((1, 1, 2 * D_FF), lambda i, be, sg, nx, bv, nb: (be[i], 0, 0)),
            pl.BlockSpec(memory_space=pl.ANY),
            pl.BlockSpec((1, 1, D_MODEL), lambda i, be, sg, nx, bv, nb: (be[i], 0, 0)),
        ],
        out_specs=pl.BlockSpec((MOE_TILE * SLABS, LANES), lambda i, be, sg, nx, bv, nb: (i, 0)),
        scratch_shapes=[
            pltpu.VMEM((2, D_MODEL, 2 * D_FF), f32),
            pltpu.VMEM((2, D_FF, D_MODEL), f32),
            pltpu.VMEM((D_MODEL, 2 * D_FF), bf16),
            pltpu.VMEM((D_FF, D_MODEL), bf16),
            pltpu.SemaphoreType.DMA((2, 2)),
        ],
    )
    return pl.pallas_call(
        functools.partial(_expert_kernel, ff_chunk=256, cast_rows=128),
        grid_spec=grid_spec,
        out_shape=jax.ShapeDtypeStruct((n_blocks * MOE_TILE * SLABS, LANES), f32),
        compiler_params=pltpu.CompilerParams(
            dimension_semantics=("arbitrary",), vmem_limit_bytes=VMEM_LIMIT),
        name="moe_experts",
    )(block_e, block_seg, next_e, block_valid, n_used, xs, wgu, bgu, wdn, bdn)


def _combine_kernel(tab_cur, tab_nxt, ys_hbm, h_ref, tg_ref, tic_ref, rankc_ref, pp_ref, ps_ref,
                    gple_ref, wpg_ref, wpp_ref, gfin_ref, yp_ref, ys_ref, stage, sems, *, n_prompt_tiles):
    i = pl.program_id(0)
    nt = pl.num_programs(0)
    n = COMBINE_TILE * TOP_K
    slot_rows = n * SLABS

    def gather(tab_ref, slot):
        slot_base = slot * slot_rows
        for bit in range(RUN_BITS - 1, -1, -1):
            size = (1 << bit) * SLABS

            def issue(p, carry, bit=bit, size=size):
                dst0 = pl.multiple_of(tab_ref[0, 0, RUN_SRC + bit * N_EXPERTS + p] * SLABS, SLABS)
                src0 = pl.multiple_of(tab_ref[0, 0, RUN_DST + bit * N_EXPERTS + p] * SLABS, SLABS)
                pltpu.make_async_copy(ys_hbm.at[pl.ds(src0, size)], stage.at[pl.ds(slot_base + dst0, size)],
                                      sems.at[slot]).start(priority=bit % 2)
                return carry

            lax.fori_loop(0, tab_ref[0, 0, RUN_COUNT + bit], issue, 0)

    @pl.when(i == 0)
    def _():
        gather(tab_cur, 0)

    @pl.when(i + 1 < nt)
    def _():
        gather(tab_nxt, (i + 1) % 2)

    slot = i % 2
    base = pl.multiple_of(slot * slot_rows, slot_rows)
    pltpu.make_async_copy(ys_hbm.at[pl.ds(0, slot_rows)], stage.at[pl.ds(base, slot_rows)],
                          sems.at[slot]).wait()

    tic = tic_ref[...]
    shift = jnp.zeros(tic.shape, i32)
    for e in range(N_EXPERTS):
        shift = jnp.where(tic == e, tab_cur[0, 0, RUN_SHIFT + e], shift)
    staged = rankc_ref[...] + shift
    tg = tg_ref[...]
    staged_k = [jnp.broadcast_to(staged[:, k:k + 1], (COMBINE_TILE, LANES)) for k in range(TOP_K)]
    gate_k = [jnp.broadcast_to(tg[:, k:k + 1], (COMBINE_TILE, LANES)) for k in range(TOP_K)]
    lane = lax.broadcasted_iota(i32, (COMBINE_TILE, LANES), 1)
    groups = []
    for c in range(n // LANES):
        g = jnp.zeros((COMBINE_TILE, LANES), f32)
        for k in range(TOP_K):
            g = jnp.where(lane + c * LANES == staged_k[k], gate_k[k], g)
        groups.append(g)
    gates = jnp.concatenate(groups, axis=1)
    gates_hi = gates.astype(bf16)
    gates_lo = (gates - gates_hi.astype(f32)).astype(bf16)
    rows = jnp.concatenate(
        [_load_slab_cols(stage, base, n, s).astype(bf16) for s in range(SLABS)], axis=1)
    h = h_ref[...] + _dot(gates_hi, rows) + _dot(gates_lo, rows)
    gate = _sigmoid(_dot(_rms(h, gple_ref[...]).astype(bf16), wpg_ref[...]))
    p = _select_rows(i < n_prompt_tiles, pp_ref, ps_ref).astype(bf16)
    h = h + gate * _dot(p, wpp_ref[...])
    y = _rms(h, gfin_ref[...])

    @pl.when(i < n_prompt_tiles)
    def _():
        yp_ref[...] = y

    @pl.when(i >= n_prompt_tiles)
    def _():
        ys_ref[...] = y


def _combine(tile_table, ys, h1, tg, tic, rankc, pp, ps, gple, wpg, wpp, gfin):
    t = h1.shape[0]
    nt = t // COMBINE_TILE
    npt = pp.shape[0] // COMBINE_TILE
    n = COMBINE_TILE * TOP_K
    row = lambda w: pl.BlockSpec((COMBINE_TILE, w), lambda i: (i, 0))
    full = lambda shape: pl.BlockSpec(shape, lambda i: (0,) * len(shape))
    return pl.pallas_call(
        functools.partial(_combine_kernel, n_prompt_tiles=npt),
        grid=(nt,),
        in_specs=[
            pl.BlockSpec((1, 1, RUN_TABLE), lambda i: (i, 0, 0), memory_space=pltpu.SMEM),
            pl.BlockSpec((1, 1, RUN_TABLE), lambda i: (jnp.minimum(i + 1, nt - 1), 0, 0),
                         memory_space=pltpu.SMEM),
            pl.BlockSpec(memory_space=pl.ANY),
            row(D_MODEL), row(LANES), row(LANES), row(LANES),
            pl.BlockSpec((COMBINE_TILE, PLE_DIM), lambda i: (jnp.minimum(i, npt - 1), 0)),
            pl.BlockSpec((COMBINE_TILE, PLE_DIM), lambda i: (jnp.maximum(i - npt, 0), 0)),
            full((1, D_MODEL)), full((D_MODEL, D_MODEL)), full((PLE_DIM, D_MODEL)),
            full((1, D_MODEL)),
        ],
        out_specs=[
            pl.BlockSpec((COMBINE_TILE, D_MODEL), lambda i: (jnp.minimum(i, npt - 1), 0)),
            pl.BlockSpec((COMBINE_TILE, D_MODEL), lambda i: (jnp.maximum(i - npt, 0), 0)),
        ],
        out_shape=[
            jax.ShapeDtypeStruct((pp.shape[0], D_MODEL), f32),
            jax.ShapeDtypeStruct((ps.shape[0], D_MODEL), f32),
        ],
        scratch_shapes=[
            pltpu.VMEM((2 * n * SLABS, LANES), f32),
            pltpu.SemaphoreType.DMA((2,)),
        ],
        compiler_params=pltpu.CompilerParams(
            dimension_semantics=("arbitrary",), vmem_limit_bytes=VMEM_LIMIT),
        name="moe_combine_ple",
    )(tile_table, tile_table, ys, h1, tg, tic, rankc, pp, ps, gple, wpg, wpp, gfin)


def _routing_tables(top_i, counts, part_counts, n_blocks):
    padded = (counts + MOE_TILE - 1) // MOE_TILE * MOE_TILE
    pend = jnp.cumsum(padded)
    pstart = pend - padded
    block_rows = jnp.arange(n_blocks, dtype=i32) * MOE_TILE
    block_e = jnp.minimum(
        jnp.sum((block_rows[:, None] >= pend[None, :]).astype(i32), axis=1), N_EXPERTS - 1)
    experts = jnp.arange(N_EXPERTS, dtype=i32)
    of_block = lambda table: jnp.sum(jnp.where(block_e[:, None] == experts[None, :], table[None, :], 0), axis=1)
    block_valid = jnp.clip(of_block(pstart + counts) - block_rows, 0, MOE_TILE)
    n_used = pend[-1:] // MOE_TILE
    owns = padded > 0
    later = owns[None, :] & (experts[None, :] > experts[:, None])
    next_of_e = jnp.where(jnp.any(later, axis=1), jnp.argmax(later, axis=1), -1)
    block_seg = of_block(jnp.cumsum(owns.astype(i32)) - 1)
    next_e = of_block(next_of_e.astype(i32))

    nt = top_i.shape[1] // DISPATCH_TILE
    tile_n = part_counts[:, :, :TOKEN_TILE // DISPATCH_TILE].transpose(0, 2, 1).reshape(nt, N_EXPERTS)
    seen = jnp.cumsum(tile_n, axis=0) - tile_n
    tile_src = jnp.cumsum(tile_n, axis=1) - tile_n
    tile_dst = pstart[None, :] + seen
    bits = jnp.arange(RUN_BITS, dtype=i32)[None, :, None]
    n3 = tile_n[:, None, :]
    has_piece = ((n3 >> bits) & 1) == 1
    covered = (n3 >> (bits + 1)) << (bits + 1)
    earlier = (experts[:, None] < experts[None, :])[None, None]
    place = jnp.sum((has_piece[..., None] & earlier).astype(i32), axis=2)
    listed = has_piece[..., None] & (place[..., None] == experts[None, None, None, :])
    compact = lambda v: jnp.sum(jnp.where(listed, v[..., None], 0), axis=2)
    piece_src = compact(tile_src[:, None, :] + covered)
    piece_dst = compact(tile_dst[:, None, :] + covered)
    piece_cnt = jnp.sum(has_piece.astype(i32), axis=2)
    tile_table = jnp.concatenate(
        [tile_src - seen, jnp.pad(piece_cnt, ((0, 0), (0, N_EXPERTS - RUN_BITS))),
         piece_src.reshape(nt, -1), piece_dst.reshape(nt, -1)], axis=1)
    tables = (block_e, block_seg, next_e, block_valid, n_used)
    return (tile_table.reshape(nt, 1, RUN_TABLE).astype(i32),) + tuple(
        t.astype(i32) for t in tables)


def kernel(x_prompt, x_sample, p_prompt, p_sample, state_gla, state_hgrn, g_mix, w_in, w_gk2, b_gk2,
           gn_gla, lb_logits, gn_hgrn, w_out, g_ffn, w_router, b_router, w_gu, b_gu, w_dn, b_dn,
           g_ple, w_ple_gate, w_ple_proj, g_final):
    batch, seq, _ = x_prompt.shape
    dec_batch, dec_seq, _ = x_sample.shape
    tp, ts = batch * seq, dec_batch * dec_seq
    xp = x_prompt.reshape(tp, D_MODEL)
    xs = x_sample.reshape(ts, D_MODEL)

    wgk = jnp.concatenate(
        [w_gk2[0], jnp.zeros((R1_PAD - GLA_LOWRANK, GLA_KDIM), f32)], axis=0).astype(bf16)
    wr = w_router[0].T.astype(bf16)
    br = b_router[0].reshape(N_EXPERTS, 1)

    z = _in_proj(xp, xs, g_mix[0].reshape(1, D_MODEL), w_in[0])

    mixer_params = (wgk, b_gk2[0].reshape(1, GLA_KDIM), gn_gla[0].reshape(1, GLA_DV),
                    lb_logits[0:2], gn_hgrn[0].reshape(1, HGRN_DV))
    o_p, new_gla_p, new_hgrn_p = _mixer_prompt(
        z, *mixer_params, batch=batch, seq=seq, rows_per_step=512, block_rows=256)
    z_s = jnp.pad(z[tp:].reshape(dec_batch, dec_seq, Z_W), ((0, 0), (0, SAMPLE_PAD_T - dec_seq), (0, 0)))
    o_s, new_gla_s, new_hgrn_s = _mixer_sample(
        z_s.reshape(dec_batch * SAMPLE_PAD_T, Z_W), state_gla, state_hgrn, *mixer_params,
        steps=dec_seq, batch_block=8)
    o_s = o_s.reshape(dec_batch, SAMPLE_PAD_T, D_MODEL)[:, :dec_seq].reshape(ts, D_MODEL).astype(bf16)

    h1, xn, top_i, top_g, rank, top_ic, rank_c, counts, part_counts = _out_proj(
        xp, xs, o_p, o_s, w_out[0].astype(bf16), g_ffn[0].reshape(1, D_MODEL), wr, br)

    t = tp + ts
    n_blocks = -(-(t * TOP_K) // MOE_TILE) + N_EXPERTS
    tile_table, block_e, block_seg, next_e, block_valid, n_used = _routing_tables(
        top_i, counts[:, 0], part_counts, n_blocks)
    x_sorted = _dispatch(block_valid, tile_table, top_i, rank, xn)
    y_sorted = _experts(block_e, block_seg, next_e, block_valid, n_used, x_sorted,
                        w_gu[0], b_gu[0].reshape(N_EXPERTS, 1, 2 * D_FF),
                        w_dn[0], b_dn[0].reshape(N_EXPERTS, 1, D_MODEL))

    y_p, y_s = _combine(tile_table, y_sorted, h1, top_g, top_ic, rank_c, p_prompt[0].reshape(tp, PLE_DIM),
                        p_sample[0].reshape(ts, PLE_DIM), g_ple[0].reshape(1, D_MODEL),
                        w_ple_gate[0].astype(bf16), w_ple_proj[0].astype(bf16), g_final.reshape(1, D_MODEL))

    return (y_p.reshape(batch, seq, D_MODEL), y_s.reshape(dec_batch, dec_seq, D_MODEL),
            new_gla_p, new_hgrn_p, new_gla_s, new_hgrn_s)
```

```python
import functools

import jax
import jax.numpy as jnp
from jax import lax
from jax.experimental import pallas as pl
from jax.experimental.pallas import tpu as pltpu

f32 = jnp.float32
bf16 = jnp.bfloat16
i32 = jnp.int32

D_MODEL = 1024
GLA_HEADS = 4
GLA_DK = 64
GLA_DV = 128
GLA_KDIM = GLA_HEADS * GLA_DK
GLA_WIDTH = GLA_HEADS * GLA_DV
GLA_LOWRANK = 16
GLA_NORMALIZER = 16.0
HGRN_HEADS = 4
HGRN_DK = 128
HGRN_DV = 128
HGRN_FDIM = HGRN_HEADS * HGRN_DK
HGRN_WIDTH = HGRN_HEADS * HGRN_DV
CHUNK = 64
N_EXPERTS = 32
TOP_K = 4
D_FF = 1024
SWIGLU_LIMIT = 7.0
SWIGLU_ALPHA = 1.702
PLE_DIM = 256
EPS = 1e-6

LANES = 128
SAMPLE_PAD_T = 8
XSLABS = D_MODEL // (2 * LANES)
ROUTE_ROWS = 8
R1_PAD = LANES
Z_Q1 = 0
Z_K1 = Z_Q1 + GLA_KDIM
Z_V1 = Z_K1 + GLA_KDIM
Z_G1 = Z_V1 + GLA_WIDTH
Z_R1 = Z_G1 + GLA_WIDTH
Z_Q2 = Z_R1 + R1_PAD
Z_F2 = Z_Q2 + HGRN_FDIM
Z_I2 = Z_F2 + HGRN_FDIM
Z_G2 = Z_I2 + HGRN_WIDTH
Z_W = Z_G2 + HGRN_WIDTH

TOKEN_TILE = 512
MOE_TILE = 512
DISPATCH_TILE = 256
COMBINE_TILE = DISPATCH_TILE
RUN_BITS = DISPATCH_TILE.bit_length()
RUN_SHIFT = 0
RUN_COUNT = RUN_SHIFT + N_EXPERTS
RUN_SRC = RUN_COUNT + N_EXPERTS
RUN_DST = RUN_SRC + RUN_BITS * N_EXPERTS
RUN_TABLE = RUN_DST + RUN_BITS * N_EXPERTS
VMEM_LIMIT = 56 * 1024 * 1024


def _rms(x, g):
    return x * lax.rsqrt(jnp.mean(x * x, axis=-1, keepdims=True) + EPS) * g


def _sigmoid(x):
    return 0.5 * jnp.tanh(0.5 * x) + 0.5


def _dot(a, b):
    return jnp.dot(a, b, preferred_element_type=f32)


def _dot_nt(a, b):
    return lax.dot_general(a, b, (((1,), (1,)), ((), ())), preferred_element_type=f32)


def _dot_tn(a, b):
    return lax.dot_general(a, b, (((0,), (0,)), ((), ())), preferred_element_type=f32)


def _cumsum_rows(tri, x):
    hi = x.astype(bf16)
    r1 = x - hi.astype(f32)
    mid = r1.astype(bf16)
    lo = (r1 - mid.astype(f32)).astype(bf16)
    return _dot(tri, hi) + _dot(tri, mid) + _dot(tri, lo)


def _select_rows(is_prompt, a_ref, b_ref):
    return jnp.where(is_prompt, a_ref[...], b_ref[...])


def _inproj_kernel(xp_ref, xs_ref, g_ref, w_hbm, z_ref, w_f32, w_bf, sem,
                   *, n_prompt_tiles, col_chunk, cast_rows):
    @pl.when(pl.program_id(0) == 0)
    def _():
        copy = pltpu.make_async_copy(w_hbm, w_f32, sem)
        copy.start()
        copy.wait()
        r1_end = Z_R1 + GLA_LOWRANK

        def cast(j, carry):
            rows = pl.ds(pl.multiple_of(j * cast_rows, cast_rows), cast_rows)
            w_bf[rows, 0:Z_R1] = w_f32[rows, 0:Z_R1].astype(bf16)
            w_bf[rows, Z_R1:Z_Q2] = jnp.concatenate(
                [w_f32[rows, Z_R1:r1_end], jnp.zeros((cast_rows, R1_PAD - GLA_LOWRANK), f32)],
                axis=1).astype(bf16)
            w_bf[rows, Z_Q2:Z_W] = w_f32[rows, r1_end:].astype(bf16)
            return carry
        lax.fori_loop(0, D_MODEL // cast_rows, cast, 0)

    is_prompt = pl.program_id(0) < n_prompt_tiles
    x = _select_rows(is_prompt, xp_ref, xs_ref)
    a = _rms(x, g_ref[...]).astype(bf16)
    for c0 in range(0, Z_W, col_chunk):
        c1 = min(c0 + col_chunk, Z_W)
        z_ref[:, c0:c1] = _dot(a, w_bf[:, c0:c1]).astype(bf16)


def _in_proj(xp, xs, g, w):
    tp, ts = xp.shape[0], xs.shape[0]
    npt = tp // TOKEN_TILE
    nt = npt + ts // TOKEN_TILE
    return pl.pallas_call(
        functools.partial(_inproj_kernel, n_prompt_tiles=npt, col_chunk=512, cast_rows=64),
        grid=(nt,),
        in_specs=[
            pl.BlockSpec((TOKEN_TILE, D_MODEL), lambda i: (jnp.minimum(i, npt - 1), 0)),
            pl.BlockSpec((TOKEN_TILE, D_MODEL), lambda i: (jnp.maximum(i - npt, 0), 0)),
            pl.BlockSpec((1, D_MODEL), lambda i: (0, 0)),
            pl.BlockSpec(memory_space=pl.ANY),
        ],
        out_specs=pl.BlockSpec((TOKEN_TILE, Z_W), lambda i: (i, 0)),
        out_shape=jax.ShapeDtypeStruct((tp + ts, Z_W), bf16),
        scratch_shapes=[
            pltpu.VMEM(w.shape, f32),
            pltpu.VMEM((D_MODEL, Z_W), bf16),
            pltpu.SemaphoreType.DMA(()),
        ],
        compiler_params=pltpu.CompilerParams(
            dimension_semantics=("arbitrary",), vmem_limit_bytes=VMEM_LIMIT),
        name="in_proj",
    )(xp, xs, g, w)


def _block_consts(r, c):
    shift = c.bit_length() - 1
    row = lax.broadcasted_iota(i32, (r, r), 0)
    col = lax.broadcasted_iota(i32, (r, r), 1)
    causal = (lax.shift_right_logical(row, shift) == lax.shift_right_logical(col, shift)) & (row >= col)
    return causal.astype(bf16), causal


def _subchunk_row(x, c, idx):
    parts = [jnp.broadcast_to(x[j * c + idx:j * c + idx + 1, :], (c, x.shape[1]))
             for j in range(x.shape[0] // c)]
    return jnp.concatenate(parts, axis=0)


def _gated_block(q, k, v, log_a, states, *, c, scale, ref_row, tri, causal, heads_per_group, scan):
    r = q.shape[0]
    nsub = r // c
    groups = q.shape[1] // LANES
    b = _cumsum_rows(tri, log_a)
    b_ref = _subchunk_row(b, c, ref_row)
    b_last = _subchunk_row(b, c, c - 1)
    qs = q * scale
    q_r = qs * jnp.exp(b - b_ref)
    k_r = (k * jnp.exp(b_ref - b)).astype(bf16)
    k_end = k * jnp.exp(b_last - b)
    q_b = qs * jnp.exp(b)
    decay = jnp.exp(b_last)
    vb = v.astype(bf16)
    lane = lax.broadcasted_iota(i32, (r, LANES), 1)
    width = LANES // heads_per_group
    outs, new_states = [], []
    for g in range(groups):
        gl = slice(g * LANES, (g + 1) * LANES)
        masks = [None] if heads_per_group == 1 else [
            (lane >= j * width) & (lane < (j + 1) * width) for j in range(heads_per_group)]
        pick = lambda x, m: x if m is None else jnp.where(m, x, 0.0)
        q_r_h = [pick(q_r[:, gl], m).astype(bf16) for m in masks]
        q_b_h = [pick(q_b[:, gl], m).astype(bf16) for m in masks]
        k_end_h = [pick(k_end[:, gl], m).astype(bf16) for m in masks]
        v_h = [vb[:, (g * heads_per_group + j) * LANES:(g * heads_per_group + j + 1) * LANES]
               for j in range(heads_per_group)]
        o_intra = []
        for j in range(heads_per_group):
            att = _dot_nt(q_r_h[j], k_r[:, gl])
            att = jnp.where(causal, att, 0.0).astype(bf16)
            o_intra.append(_dot(att, v_h[j]))
        upd = []
        for s in range(nsub):
            rs = slice(s * c, (s + 1) * c)
            u = _dot_tn(v_h[0][rs], k_end_h[0][rs])
            for j in range(1, heads_per_group):
                u = u + _dot_tn(v_h[j][rs], k_end_h[j][rs])
            upd.append(u)
        o_inter = [[] for _ in range(heads_per_group)]
        st = states[g] if scan else None
        group_states = []
        for s in range(nsub):
            rs = slice(s * c, (s + 1) * c)
            if not scan:
                st = states[s][g]
            st_b = st.astype(bf16)
            for j in range(heads_per_group):
                o_inter[j].append(_dot_nt(q_b_h[j][rs], st_b))
            st = st * decay[s * c:s * c + 1, gl] + upd[s]
            if not scan:
                group_states.append(st)
        for j in range(heads_per_group):
            outs.append(o_intra[j] + jnp.concatenate(o_inter[j], axis=0))
        new_states.append(st if scan else group_states)
    if not scan:
        new_states = [[new_states[g][s] for g in range(groups)] for s in range(nsub)]
    return outs, new_states


def _head_norm(o, g):
    return o * lax.rsqrt(jnp.mean(o * o, axis=-1, keepdims=True) + EPS) * g


def _mixer_block(zb, st_gla, st_hgrn, wgk, bgk, gng, lb, gnh, *, c, ref_row, tri, causal, valid, scan):
    z = zb.astype(f32)
    gk = _dot(zb[:, Z_R1:Z_R1 + R1_PAD], wgk) + bgk
    log_a = (jnp.minimum(gk, 0.0) - jnp.log(1.0 + jnp.exp(-jnp.abs(gk)))) / GLA_NORMALIZER
    k1 = z[:, Z_K1:Z_K1 + GLA_KDIM]
    if valid is not None:
        log_a = jnp.where(valid, log_a, 0.0)
        k1 = jnp.where(valid, k1, 0.0)
    o1, st_gla = _gated_block(
        z[:, Z_Q1:Z_Q1 + GLA_KDIM], k1, z[:, Z_V1:Z_V1 + GLA_WIDTH], log_a, st_gla,
        c=c, scale=GLA_DK ** -0.5, ref_row=ref_row, tri=tri, causal=causal,
        heads_per_group=LANES // GLA_DK, scan=scan)
    forget = lb + (1.0 - lb) * _sigmoid(z[:, Z_F2:Z_F2 + HGRN_FDIM])
    k2 = 1.0 - forget
    log_f = jnp.log(forget)
    if valid is not None:
        log_f = jnp.where(valid, log_f, 0.0)
        k2 = jnp.where(valid, k2, 0.0)
    q2 = z[:, Z_Q2:Z_Q2 + HGRN_FDIM]
    i2 = z[:, Z_I2:Z_I2 + HGRN_WIDTH]
    o2, st_hgrn = _gated_block(
        q2 * _sigmoid(q2), k2, i2 * _sigmoid(i2), log_f, st_hgrn,
        c=c, scale=1.0, ref_row=ref_row, tri=tri, causal=causal, heads_per_group=1, scan=scan)
    cols = []
    for h in range(GLA_HEADS):
        g1 = z[:, Z_G1 + h * GLA_DV:Z_G1 + (h + 1) * GLA_DV]
        cols.append(_head_norm(o1[h], gng) * (g1 * _sigmoid(g1)))
    for h in range(HGRN_HEADS):
        g2 = z[:, Z_G2 + h * HGRN_DV:Z_G2 + (h + 1) * HGRN_DV]
        cols.append(_head_norm(o2[h], gnh) * _sigmoid(g2))
    return jnp.concatenate(cols, axis=1), st_gla, st_hgrn


def _lower_bound(lbl_ref):
    l = lbl_ref[...]
    m = jnp.max(l, axis=0, keepdims=True)
    e = jnp.exp(l - m)
    return e[0:1, :] / jnp.sum(e, axis=0, keepdims=True)


GLA_GROUPS = GLA_KDIM // LANES


def _gla_state_t(s_ref_heads):
    return s_ref_heads.reshape(LANES, GLA_DV).T


def _gla_state_from_t(st):
    return st.T.reshape(LANES // GLA_DK, GLA_DK, GLA_DV)


def _mixer_prompt_kernel(z_ref, wgk_ref, bgk_ref, gng_ref, lbl_ref, gnh_ref,
                         o_ref, sg_out, sh_out, sg_t, sh_t, *, block_rows):
    j = pl.program_id(1)

    @pl.when(j == 0)
    def _():
        sg_t[...] = jnp.zeros_like(sg_t)
        sh_t[...] = jnp.zeros_like(sh_t)

    tri, causal = _block_consts(block_rows, CHUNK)
    lb = _lower_bound(lbl_ref)
    wgk, bgk, gng, gnh = wgk_ref[...], bgk_ref[...], gng_ref[...], gnh_ref[...]

    def body(blk, carry):
        r0 = pl.multiple_of(blk * block_rows, block_rows)
        o, st_g, st_h = _mixer_block(
            z_ref[pl.ds(r0, block_rows), :], [sg_t[g] for g in range(GLA_GROUPS)],
            [sh_t[h] for h in range(HGRN_HEADS)], wgk, bgk, gng, lb, gnh,
            c=CHUNK, ref_row=CHUNK // 2, tri=tri, causal=causal, valid=None, scan=True)
        o_ref[pl.ds(r0, block_rows), :] = o.astype(bf16)
        for g in range(GLA_GROUPS):
            sg_t[g] = st_g[g]
        for h in range(HGRN_HEADS):
            sh_t[h] = st_h[h]
        return carry

    lax.fori_loop(0, z_ref.shape[0] // block_rows, body, 0)

    @pl.when(j == pl.num_programs(1) - 1)
    def _():
        per = LANES // GLA_DK
        for g in range(GLA_GROUPS):
            sg_out[0, 0, g * per:(g + 1) * per] = _gla_state_from_t(sg_t[g])
        for h in range(HGRN_HEADS):
            sh_out[0, 0, h] = sh_t[h].T


def _mixer_prompt(z, wgk, bgk, gng, lbl, gnh, *, batch, seq, rows_per_step, block_rows):
    nj = seq // rows_per_step
    full = lambda shape: pl.BlockSpec(shape, lambda b, j: (0,) * len(shape))
    return pl.pallas_call(
        functools.partial(_mixer_prompt_kernel, block_rows=block_rows),
        grid=(batch, nj),
        in_specs=[
            pl.BlockSpec((rows_per_step, Z_W), lambda b, j: (b * nj + j, 0)),
            full((R1_PAD, GLA_KDIM)), full((1, GLA_KDIM)), full((1, GLA_DV)),
            full((2, HGRN_FDIM)), full((1, HGRN_DV)),
        ],
        out_specs=[
            pl.BlockSpec((rows_per_step, D_MODEL), lambda b, j: (b * nj + j, 0)),
            pl.BlockSpec((1, 1, GLA_HEADS, GLA_DK, GLA_DV), lambda b, j: (0, b, 0, 0, 0)),
            pl.BlockSpec((1, 1, HGRN_HEADS, HGRN_DK, HGRN_DV), lambda b, j: (0, b, 0, 0, 0)),
        ],
        out_shape=[
            jax.ShapeDtypeStruct((batch * seq, D_MODEL), bf16),
            jax.ShapeDtypeStruct((1, batch, GLA_HEADS, GLA_DK, GLA_DV), f32),
            jax.ShapeDtypeStruct((1, batch, HGRN_HEADS, HGRN_DK, HGRN_DV), f32),
        ],
        scratch_shapes=[
            pltpu.VMEM((GLA_GROUPS, GLA_DV, LANES), f32),
            pltpu.VMEM((HGRN_HEADS, HGRN_DV, HGRN_DK), f32),
        ],
        compiler_params=pltpu.CompilerParams(
            dimension_semantics=("arbitrary", "arbitrary"), vmem_limit_bytes=VMEM_LIMIT),
        name="mixer_prompt",
    )(z, wgk, bgk, gng, lbl, gnh)


def _mixer_sample_kernel(z_ref, sg_in, sh_in, wgk_ref, bgk_ref, gng_ref, lbl_ref, gnh_ref,
                         o_ref, sg_out, sh_out, *, batch_block, steps):
    rows = batch_block * SAMPLE_PAD_T
    tri, causal = _block_consts(rows, SAMPLE_PAD_T)
    step = lax.broadcasted_iota(i32, (rows, 1), 0) & (SAMPLE_PAD_T - 1)
    valid = step < steps
    lb = _lower_bound(lbl_ref)
    per = LANES // GLA_DK
    st_g = [[_gla_state_t(sg_in[0, s, g * per:(g + 1) * per]) for g in range(GLA_GROUPS)]
            for s in range(batch_block)]
    st_h = [[sh_in[0, s, h].T for h in range(HGRN_HEADS)] for s in range(batch_block)]
    o, st_g, st_h = _mixer_block(
        z_ref[...], st_g, st_h, wgk_ref[...], bgk_ref[...], gng_ref[...], lb, gnh_ref[...],
        c=SAMPLE_PAD_T, ref_row=steps // 2, tri=tri, causal=causal, valid=valid, scan=False)
    o_ref[...] = o
    for s in range(batch_block):
        for g in range(GLA_GROUPS):
            sg_out[0, s, g * per:(g + 1) * per] = _gla_state_from_t(st_g[s][g])
        for h in range(HGRN_HEADS):
            sh_out[0, s, h] = st_h[s][h].T


def _mixer_sample(z2, state_gla, state_hgrn, wgk, bgk, gng, lbl, gnh, *, steps, batch_block):
    batch = z2.shape[0] // SAMPLE_PAD_T
    rows = batch_block * SAMPLE_PAD_T
    full = lambda shape: pl.BlockSpec(shape, lambda i: (0,) * len(shape))
    sg_spec = pl.BlockSpec((1, batch_block, GLA_HEADS, GLA_DK, GLA_DV), lambda i: (0, i, 0, 0, 0))
    sh_spec = pl.BlockSpec((1, batch_block, HGRN_HEADS, HGRN_DK, HGRN_DV), lambda i: (0, i, 0, 0, 0))
    return pl.pallas_call(
        functools.partial(_mixer_sample_kernel, batch_block=batch_block, steps=steps),
        grid=(batch // batch_block,),
        in_specs=[
            pl.BlockSpec((rows, Z_W), lambda i: (i, 0)),
            sg_spec, sh_spec,
            full((R1_PAD, GLA_KDIM)), full((1, GLA_KDIM)), full((1, GLA_DV)),
            full((2, HGRN_FDIM)), full((1, HGRN_DV)),
        ],
        out_specs=[pl.BlockSpec((rows, D_MODEL), lambda i: (i, 0)), sg_spec, sh_spec],
        out_shape=[
            jax.ShapeDtypeStruct((batch * SAMPLE_PAD_T, D_MODEL), f32),
            jax.ShapeDtypeStruct(state_gla.shape, f32),
            jax.ShapeDtypeStruct(state_hgrn.shape, f32),
        ],
        compiler_params=pltpu.CompilerParams(
            dimension_semantics=("arbitrary",), vmem_limit_bytes=VMEM_LIMIT),
        name="mixer_sample",
    )(z2, state_gla, state_hgrn, wgk, bgk, gng, lbl, gnh)


HIGH_HALF = 0xFFFF0000


def _pack_rows(x):
    bits = lax.bitcast_convert_type(x, jnp.uint32)
    return [(bits[:, (2 * s + 1) * LANES:(2 * s + 2) * LANES] & jnp.uint32(HIGH_HALF))
            | (bits[:, 2 * s * LANES:(2 * s + 1) * LANES] >> 16) for s in range(x.shape[1] // (2 * LANES))]


def _unpack_rows(ref, start, rows):
    cols = []
    for s in range(XSLABS):
        w = ref[pl.ds(start + s, rows, stride=XSLABS), :]
        cols.append(lax.bitcast_convert_type(w << 16, f32).astype(bf16))
        cols.append(lax.bitcast_convert_type(w & jnp.uint32(HIGH_HALF), f32).astype(bf16))
    return jnp.concatenate(cols, axis=1)


def _outproj_kernel(xp_ref, xs_ref, op_ref, os_ref, wo_ref, g_ref, wr_ref, br_ref,
                    h_ref, xn_ref, ti_ref, tg_ref, rank_ref, tic_ref, rankc_ref, cnt_ref, part_ref, run_cnt,
                    *, n_prompt_tiles):
    i = pl.program_id(0)

    @pl.when(i == 0)
    def _():
        run_cnt[...] = jnp.zeros_like(run_cnt)

    x = _select_rows(i < n_prompt_tiles, xp_ref, xs_ref)
    o = _select_rows(i < n_prompt_tiles, op_ref, os_ref)
    h = x + _dot(o, wo_ref[...])
    h_ref[...] = h
    xn = _rms(h, g_ref[...])
    xn_ref[...] = xn.astype(bf16)
    l = _dot_nt(wr_ref[...], xn.astype(bf16)) + br_ref[...]
    expert = lax.broadcasted_iota(i32, l.shape, 0)
    neg = jnp.float32(-jnp.inf)
    vals, idxs = [], []
    for _ in range(TOP_K):
        m = jnp.max(l, axis=0, keepdims=True)
        idx = jnp.min(jnp.where(l == m, expert, N_EXPERTS), axis=0, keepdims=True)
        vals.append(m)
        idxs.append(idx)
        l = jnp.where(expert == idx, neg, l)
    es = [jnp.exp(v - vals[0]) for v in vals]
    denom = es[0] + es[1] + es[2] + es[3]

    onehot = jnp.zeros(l.shape, f32)
    for k in range(TOP_K):
        onehot = onehot + (expert == idxs[k]).astype(f32)
    n = l.shape[1]
    earlier = (lax.broadcasted_iota(i32, (n, n), 0) < lax.broadcasted_iota(i32, (n, n), 1)).astype(bf16)
    ahead = _dot(onehot.astype(bf16), earlier) + run_cnt[:, 0:1]
    run_cnt[...] = run_cnt[...] + jnp.sum(onehot, axis=1, keepdims=True)
    cnt_ref[...] = run_cnt[...].astype(i32)
    lane = lax.broadcasted_iota(i32, (N_EXPERTS, LANES), 1)
    parts = jnp.zeros((N_EXPERTS, LANES), f32)
    for p in range(n // DISPATCH_TILE):
        c = jnp.sum(onehot[:, p * DISPATCH_TILE:(p + 1) * DISPATCH_TILE], axis=1, keepdims=True)
        parts = jnp.where(lane == p, c, parts)
    part_ref[0] = parts.astype(i32)

    ranks = [jnp.sum(jnp.where(expert == idxs[k], ahead, 0.0), axis=0, keepdims=True).astype(i32)
             for k in range(TOP_K)]
    pad_i = jnp.zeros((ROUTE_ROWS - TOP_K, n), i32)
    ti_ref[...] = jnp.concatenate(idxs + [pad_i], axis=0)
    rank_ref[...] = jnp.concatenate(ranks + [pad_i], axis=0)
    gate_rows = jnp.concatenate([e / denom for e in es] + [jnp.zeros((LANES - TOP_K, n), f32)], axis=0)
    tg_ref[...] = gate_rows.T
    pad_wide = jnp.zeros((LANES - TOP_K, n), i32)
    tic_ref[...] = jnp.concatenate(idxs + [pad_wide], axis=0).T
    rankc_ref[...] = jnp.concatenate(ranks + [pad_wide], axis=0).T


def _out_proj(xp, xs, o_p, o_s, wo, g, wr, br):
    tp, ts = xp.shape[0], xs.shape[0]
    npt = tp // TOKEN_TILE
    nt = npt + ts // TOKEN_TILE
    t = tp + ts
    row = lambda w: pl.BlockSpec((TOKEN_TILE, w), lambda i: (i, 0))
    route = pl.BlockSpec((ROUTE_ROWS, TOKEN_TILE), lambda i: (0, i))
    full = lambda shape: pl.BlockSpec(shape, lambda i: (0,) * len(shape))
    prompt_rows = pl.BlockSpec((TOKEN_TILE, D_MODEL), lambda i: (jnp.minimum(i, npt - 1), 0))
    sample_rows = pl.BlockSpec((TOKEN_TILE, D_MODEL), lambda i: (jnp.maximum(i - npt, 0), 0))
    return pl.pallas_call(
        functools.partial(_outproj_kernel, n_prompt_tiles=npt),
        grid=(nt,),
        in_specs=[
            prompt_rows, sample_rows, prompt_rows, sample_rows,
            full((D_MODEL, D_MODEL)), full((1, D_MODEL)),
            full((N_EXPERTS, D_MODEL)), full((N_EXPERTS, 1)),
        ],
        out_specs=[row(D_MODEL), row(D_MODEL),
                   route, row(LANES), route, row(LANES), row(LANES), full((N_EXPERTS, LANES)),
                   pl.BlockSpec((1, N_EXPERTS, LANES), lambda i: (i, 0, 0))],
        out_shape=[
            jax.ShapeDtypeStruct((t, D_MODEL), f32),
            jax.ShapeDtypeStruct((t, D_MODEL), bf16),
            jax.ShapeDtypeStruct((ROUTE_ROWS, t), i32),
            jax.ShapeDtypeStruct((t, LANES), f32),
            jax.ShapeDtypeStruct((ROUTE_ROWS, t), i32),
            jax.ShapeDtypeStruct((t, LANES), i32),
            jax.ShapeDtypeStruct((t, LANES), i32),
            jax.ShapeDtypeStruct((N_EXPERTS, LANES), i32),
            jax.ShapeDtypeStruct((nt, N_EXPERTS, LANES), i32),
        ],
        scratch_shapes=[pltpu.VMEM((N_EXPERTS, LANES), f32)],
        compiler_params=pltpu.CompilerParams(
            dimension_semantics=("arbitrary",), vmem_limit_bytes=VMEM_LIMIT),
        name="out_proj_router",
    )(xp, xs, o_p, o_s, wo, g, wr, br)


def _dispatch_kernel(bv_ref, nz_ref, tab_ref, ti_ref, rank_ref, xn_ref, xs_hbm, zeros, stage, sems, zsem,
                     *, n_blocks):
    @pl.when(pl.program_id(0) == 0)
    def _():
        zeros[...] = jnp.zeros_like(zeros)

        def fill(b, carry):
            @pl.when(bv_ref[b] < MOE_TILE)
            def _():
                r0 = pl.multiple_of(b * (MOE_TILE * XSLABS), MOE_TILE * XSLABS)
                pltpu.make_async_copy(zeros, xs_hbm.at[pl.ds(r0, MOE_TILE * XSLABS)], zsem).start()
            return carry

        def drain(b, carry):
            pltpu.make_async_copy(zeros, xs_hbm.at[pl.ds(0, MOE_TILE * XSLABS)], zsem).wait()
            return carry

        lax.fori_loop(0, n_blocks, fill, 0)
        lax.fori_loop(0, nz_ref[0], drain, 0)

    i = pl.program_id(0)
    nt = pl.num_programs(0)
    n = DISPATCH_TILE * TOP_K
    slot_rows = n * XSLABS
    slot = i % 2
    base = pl.multiple_of(slot * slot_rows, slot_rows)

    def wait_rows(s):
        pltpu.make_async_copy(stage.at[pl.ds(0, slot_rows)], xs_hbm.at[pl.ds(0, slot_rows)], sems.at[s]).wait()

    ti = ti_ref[...]
    shift = jnp.zeros(ti.shape, i32)
    for e in range(N_EXPERTS):
        shift = jnp.where(ti == e, tab_ref[0, 0, RUN_SHIFT + e], shift)
    staged = rank_ref[...] + shift
    row = lax.broadcasted_iota(i32, (n, DISPATCH_TILE), 0)
    pick = row == staged[0:1, :]
    for k in range(1, TOP_K):
        pick = pick | (row == staged[k:k + 1, :])
    rows_sorted = _dot(pick.astype(bf16), xn_ref[...])
    for s, words in enumerate(_pack_rows(rows_sorted)):
        stage[pl.ds(base + s, n, stride=XSLABS), :] = words

    for bit in range(RUN_BITS - 1, -1, -1):
        size = (1 << bit) * XSLABS

        def issue(p, carry, bit=bit, size=size):
            src0 = pl.multiple_of(tab_ref[0, 0, RUN_SRC + bit * N_EXPERTS + p] * XSLABS, XSLABS)
            dst0 = pl.multiple_of(tab_ref[0, 0, RUN_DST + bit * N_EXPERTS + p] * XSLABS, XSLABS)
            pltpu.make_async_copy(stage.at[pl.ds(base + src0, size)], xs_hbm.at[pl.ds(dst0, size)],
                                  sems.at[slot]).start(priority=bit % 2)
            return carry

        lax.fori_loop(0, tab_ref[0, 0, RUN_COUNT + bit], issue, 0)

    @pl.when(i > 0)
    def _():
        wait_rows(1 - slot)

    @pl.when(i == nt - 1)
    def _():
        wait_rows(slot)


def _dispatch(block_valid, tile_table, top_i, rank, xn):
    t = xn.shape[0]
    nt = t // DISPATCH_TILE
    n_blocks = block_valid.shape[0]
    n_zero = jnp.sum((block_valid < MOE_TILE).astype(i32)).reshape(1)
    route = pl.BlockSpec((ROUTE_ROWS, DISPATCH_TILE), lambda i, bv, nz: (0, i))
    grid_spec = pltpu.PrefetchScalarGridSpec(
        num_scalar_prefetch=2,
        grid=(nt,),
        in_specs=[
            pl.BlockSpec((1, 1, RUN_TABLE), lambda i, bv, nz: (i, 0, 0), memory_space=pltpu.SMEM),
            route, route,
            pl.BlockSpec((DISPATCH_TILE, D_MODEL), lambda i, bv, nz: (i, 0)),
        ],
        out_specs=pl.BlockSpec(memory_space=pl.ANY),
        scratch_shapes=[
            pltpu.VMEM((MOE_TILE * XSLABS, LANES), jnp.uint32),
            pltpu.VMEM((2 * DISPATCH_TILE * TOP_K * XSLABS, LANES), jnp.uint32),
            pltpu.SemaphoreType.DMA((2,)),
            pltpu.SemaphoreType.DMA(()),
        ],
    )
    return pl.pallas_call(
        functools.partial(_dispatch_kernel, n_blocks=n_blocks),
        grid_spec=grid_spec,
        out_shape=jax.ShapeDtypeStruct((n_blocks * MOE_TILE * XSLABS, LANES), jnp.uint32),
        compiler_params=pltpu.CompilerParams(
            dimension_semantics=("arbitrary",), vmem_limit_bytes=VMEM_LIMIT),
        name="moe_dispatch",
    )(block_valid, n_zero, tile_table, top_i, rank, xn)


def _expert_kernel(be_ref, seg_ref, nxt_ref, bv_ref, nb_ref, x_ref, wgu_hbm, bgu_ref, wdn_hbm, bdn_ref, y_ref,
                   wgu_f32, wdn_f32, wgu_bf, wdn_bf, sems, *, ff_chunk, cast_rows):
    i = pl.program_id(0)

    @pl.when(i >= nb_ref[0])
    def _():
        y_ref[...] = jnp.zeros_like(y_ref)

    def fetch(e, slot):
        return (pltpu.make_async_copy(wgu_hbm.at[e], wgu_f32.at[slot], sems.at[slot, 0]),
                pltpu.make_async_copy(wdn_hbm.at[e], wdn_f32.at[slot], sems.at[slot, 1]))

    @pl.when((i < nb_ref[0]) & ((i == 0) | (be_ref[i] != be_ref[jnp.maximum(i - 1, 0)])))
    def _():
        slot = seg_ref[i] % 2

        @pl.when(i == 0)
        def _():
            for c in fetch(be_ref[0], 0):
                c.start()

        for c in fetch(be_ref[i], slot):
            c.wait()

        @pl.when(nxt_ref[i] >= 0)
        def _():
            for c in fetch(nxt_ref[i], 1 - slot):
                c.start()

        def cast(j, carry):
            r0 = pl.multiple_of(j * cast_rows, cast_rows)
            wgu_bf[pl.ds(r0, cast_rows), :] = wgu_f32[slot, pl.ds(r0, cast_rows), :].astype(bf16)
            wdn_bf[pl.ds(r0, cast_rows), :] = wdn_f32[slot, pl.ds(r0, cast_rows), :].astype(bf16)
            return carry
        lax.fori_loop(0, D_MODEL // cast_rows, cast, 0)

    def mlp(rows):
        xb = _unpack_rows(x_ref, 0, rows)
        hid = []
        for c0 in range(0, D_FF, ff_chunk):
            gate = _dot(xb, wgu_bf[:, c0:c0 + ff_chunk]) + bgu_ref[0, :, c0:c0 + ff_chunk]
            up = (_dot(xb, wgu_bf[:, D_FF + c0:D_FF + c0 + ff_chunk])
                  + bgu_ref[0, :, D_FF + c0:D_FF + c0 + ff_chunk])
            gate = jnp.minimum(gate, SWIGLU_LIMIT)
            up = jnp.clip(up, -SWIGLU_LIMIT, SWIGLU_LIMIT)
            hid.append(((up + 1.0) * (gate * _sigmoid(gate * SWIGLU_ALPHA))).astype(bf16))
        hid = jnp.concatenate(hid, axis=1)
        group = 2 * LANES
        for n0 in range(0, D_MODEL, group):
            y = _dot(hid, wdn_bf[:, n0:n0 + group]) + bdn_ref[0, :, n0:n0 + group]
            (words,) = _pack_rows(y.astype(bf16).astype(f32))
            y_ref[pl.ds(n0 // group, rows, stride=XSLABS), :] = words

    half = MOE_TILE // 2

    @pl.when((i < nb_ref[0]) & (bv_ref[i] > half))
    def _():
        mlp(MOE_TILE)

    @pl.when((i < nb_ref[0]) & (bv_ref[i] <= half))
    def _():
        mlp(half)
        y_ref[pl.ds(half * XSLABS, half * XSLABS), :] = jnp.zeros((half * XSLABS, LANES), jnp.uint32)


def _experts(block_e, block_seg, next_e, block_valid, n_used, xs, wgu, bgu, wdn, bdn):
    n_blocks = block_e.shape[0]
    grid_spec = pltpu.PrefetchScalarGridSpec(
        num_scalar_prefetch=5,
        grid=(n_blocks,),
        in_specs=[
            pl.BlockSpec((MOE_TILE * XSLABS, LANES),
                         lambda i, be, sg, nx, bv, nb: (jnp.minimum(i, nb[0] - 1), 0)),
            pl.BlockSpec(memory_space=pl.ANY),
            pl.BlockSpec((1, 1, 2 * D_FF), lambda i, be, sg, nx, bv, nb: (be[i], 0, 0)),
            pl.BlockSpec(memory_space=pl.ANY),
            pl.BlockSpec((1, 1, D_MODEL), lambda i, be, sg, nx, bv, nb: (be[i], 0, 0)),
        ],
        out_specs=pl.BlockSpec((MOE_TILE * XSLABS, LANES), lambda i, be, sg, nx, bv, nb: (i, 0)),
        scratch_shapes=[
            pltpu.VMEM((2, D_MODEL, 2 * D_FF), f32),
            pltpu.VMEM((2, D_FF, D_MODEL), f32),
            pltpu.VMEM((D_MODEL, 2 * D_FF), bf16),
            pltpu.VMEM((D_FF, D_MODEL), bf16),
            pltpu.SemaphoreType.DMA((2, 2)),
        ],
    )
    return pl.pallas_call(
        functools.partial(_expert_kernel, ff_chunk=256, cast_rows=128),
        grid_spec=grid_spec,
        out_shape=jax.ShapeDtypeStruct((n_blocks * MOE_TILE * XSLABS, LANES), jnp.uint32),
        compiler_params=pltpu.CompilerParams(
            dimension_semantics=("arbitrary",), vmem_limit_bytes=VMEM_LIMIT),
        name="moe_experts",
    )(block_e, block_seg, next_e, block_valid, n_used, xs, wgu, bgu, wdn, bdn)


def _combine_kernel(tab_cur, tab_nxt, ys_hbm, h_ref, tg_ref, tic_ref, rankc_ref, pp_ref, ps_ref,
                    gple_ref, wpg_ref, wpp_ref, gfin_ref, yp_ref, ys_ref, stage, sems, *, n_prompt_tiles):
    i = pl.program_id(0)
    nt = pl.num_programs(0)
    n = COMBINE_TILE * TOP_K
    slot_rows = n * XSLABS

    def gather(tab_ref, slot):
        slot_base = slot * slot_rows
        for bit in range(RUN_BITS - 1, -1, -1):
            size = (1 << bit) * XSLABS

            def issue(p, carry, bit=bit, size=size):
                dst0 = pl.multiple_of(tab_ref[0, 0, RUN_SRC + bit * N_EXPERTS + p] * XSLABS, XSLABS)
                src0 = pl.multiple_of(tab_ref[0, 0, RUN_DST + bit * N_EXPERTS + p] * XSLABS, XSLABS)
                pltpu.make_async_copy(ys_hbm.at[pl.ds(src0, size)], stage.at[pl.ds(slot_base + dst0, size)],
                                      sems.at[slot]).start(priority=bit % 2)
                return carry

            lax.fori_loop(0, tab_ref[0, 0, RUN_COUNT + bit], issue, 0)

    @pl.when(i == 0)
    def _():
        gather(tab_cur, 0)

    @pl.when(i + 1 < nt)
    def _():
        gather(tab_nxt, (i + 1) % 2)

    slot = i % 2
    base = pl.multiple_of(slot * slot_rows, slot_rows)
    pltpu.make_async_copy(ys_hbm.at[pl.ds(0, slot_rows)], stage.at[pl.ds(base, slot_rows)],
                          sems.at[slot]).wait()

    tic = tic_ref[...]
    shift = jnp.zeros(tic.shape, i32)
    for e in range(N_EXPERTS):
        shift = jnp.where(tic == e, tab_cur[0, 0, RUN_SHIFT + e], shift)
    staged = rankc_ref[...] + shift
    tg = tg_ref[...]
    staged_k = [jnp.broadcast_to(staged[:, k:k + 1], (COMBINE_TILE, LANES)) for k in range(TOP_K)]
    gate_k = [jnp.broadcast_to(tg[:, k:k + 1], (COMBINE_TILE, LANES)) for k in range(TOP_K)]
    lane = lax.broadcasted_iota(i32, (COMBINE_TILE, LANES), 1)
    groups = []
    for c in range(n // LANES):
        g = jnp.zeros((COMBINE_TILE, LANES), f32)
        for k in range(TOP_K):
            g = jnp.where(lane + c * LANES == staged_k[k], gate_k[k], g)
        groups.append(g)
    gates = jnp.concatenate(groups, axis=1)
    gates_hi = gates.astype(bf16)
    gates_lo = (gates - gates_hi.astype(f32)).astype(bf16)
    rows = _unpack_rows(stage, base, n)
    h = h_ref[...] + _dot(gates_hi, rows) + _dot(gates_lo, rows)
    gate = _sigmoid(_dot(_rms(h, gple_ref[...]).astype(bf16), wpg_ref[...]))
    p = _select_rows(i < n_prompt_tiles, pp_ref, ps_ref).astype(bf16)
    h = h + gate * _dot(p, wpp_ref[...])
    y = _rms(h, gfin_ref[...])

    @pl.when(i < n_prompt_tiles)
    def _():
        yp_ref[...] = y

    @pl.when(i >= n_prompt_tiles)
    def _():
        ys_ref[...] = y


def _combine(tile_table, ys, h1, tg, tic, rankc, pp, ps, gple, wpg, wpp, gfin):
    t = h1.shape[0]
    nt = t // COMBINE_TILE
    npt = pp.shape[0] // COMBINE_TILE
    n = COMBINE_TILE * TOP_K
    row = lambda w: pl.BlockSpec((COMBINE_TILE, w), lambda i: (i, 0))
    full = lambda shape: pl.BlockSpec(shape, lambda i: (0,) * len(shape))
    return pl.pallas_call(
        functools.partial(_combine_kernel, n_prompt_tiles=npt),
        grid=(nt,),
        in_specs=[
            pl.BlockSpec((1, 1, RUN_TABLE), lambda i: (i, 0, 0), memory_space=pltpu.SMEM),
            pl.BlockSpec((1, 1, RUN_TABLE), lambda i: (jnp.minimum(i + 1, nt - 1), 0, 0),
                         memory_space=pltpu.SMEM),
            pl.BlockSpec(memory_space=pl.ANY),
            row(D_MODEL), row(LANES), row(LANES), row(LANES),
            pl.BlockSpec((COMBINE_TILE, PLE_DIM), lambda i: (jnp.minimum(i, npt - 1), 0)),
            pl.BlockSpec((COMBINE_TILE, PLE_DIM), lambda i: (jnp.maximum(i - npt, 0), 0)),
            full((1, D_MODEL)), full((D_MODEL, D_MODEL)), full((PLE_DIM, D_MODEL)),
            full((1, D_MODEL)),
        ],
        out_specs=[
            pl.BlockSpec((COMBINE_TILE, D_MODEL), lambda i: (jnp.minimum(i, npt - 1), 0)),
            pl.BlockSpec((COMBINE_TILE, D_MODEL), lambda i: (jnp.maximum(i - npt, 0), 0)),
        ],
        out_shape=[
            jax.ShapeDtypeStruct((pp.shape[0], D_MODEL), f32),
            jax.ShapeDtypeStruct((ps.shape[0], D_MODEL), f32),
        ],
        scratch_shapes=[
            pltpu.VMEM((2 * n * XSLABS, LANES), jnp.uint32),
            pltpu.SemaphoreType.DMA((2,)),
        ],
        compiler_params=pltpu.CompilerParams(
            dimension_semantics=("arbitrary",), vmem_limit_bytes=VMEM_LIMIT),
        name="moe_combine_ple",
    )(tile_table, tile_table, ys, h1, tg, tic, rankc, pp, ps, gple, wpg, wpp, gfin)


def _routing_tables(top_i, counts, part_counts, n_blocks):
    padded = (counts + MOE_TILE - 1) // MOE_TILE * MOE_TILE
    pend = jnp.cumsum(padded)
    pstart = pend - padded
    block_rows = jnp.arange(n_blocks, dtype=i32) * MOE_TILE
    block_e = jnp.minimum(
        jnp.sum((block_rows[:, None] >= pend[None, :]).astype(i32), axis=1), N_EXPERTS - 1)
    experts = jnp.arange(N_EXPERTS, dtype=i32)
    of_block = lambda table: jnp.sum(jnp.where(block_e[:, None] == experts[None, :], table[None, :], 0), axis=1)
    block_valid = jnp.clip(of_block(pstart + counts) - block_rows, 0, MOE_TILE)
    n_used = pend[-1:] // MOE_TILE
    owns = padded > 0
    later = owns[None, :] & (experts[None, :] > experts[:, None])
    next_of_e = jnp.where(jnp.any(later, axis=1), jnp.argmax(later, axis=1), -1)
    block_seg = of_block(jnp.cumsum(owns.astype(i32)) - 1)
    next_e = of_block(next_of_e.astype(i32))

    nt = top_i.shape[1] // DISPATCH_TILE
    tile_n = part_counts[:, :, :TOKEN_TILE // DISPATCH_TILE].transpose(0, 2, 1).reshape(nt, N_EXPERTS)
    seen = jnp.cumsum(tile_n, axis=0) - tile_n
    tile_src = jnp.cumsum(tile_n, axis=1) - tile_n
    tile_dst = pstart[None, :] + seen
    bits = jnp.arange(RUN_BITS, dtype=i32)[None, :, None]
    n3 = tile_n[:, None, :]
    has_piece = ((n3 >> bits) & 1) == 1
    covered = (n3 >> (bits + 1)) << (bits + 1)
    earlier = (experts[:, None] < experts[None, :])[None, None]
    place = jnp.sum((has_piece[..., None] & earlier).astype(i32), axis=2)
    listed = has_piece[..., None] & (place[..., None] == experts[None, None, None, :])
    compact = lambda v: jnp.sum(jnp.where(listed, v[..., None], 0), axis=2)
    piece_src = compact(tile_src[:, None, :] + covered)
    piece_dst = compact(tile_dst[:, None, :] + covered)
    piece_cnt = jnp.sum(has_piece.astype(i32), axis=2)
    tile_table = jnp.concatenate(
        [tile_src - seen, jnp.pad(piece_cnt, ((0, 0), (0, N_EXPERTS - RUN_BITS))),
         piece_src.reshape(nt, -1), piece_dst.reshape(nt, -1)], axis=1)
    tables = (block_e, block_seg, next_e, block_valid, n_used)
    return (tile_table.reshape(nt, 1, RUN_TABLE).astype(i32),) + tuple(
        t.astype(i32) for t in tables)


def kernel(x_prompt, x_sample, p_prompt, p_sample, state_gla, state_hgrn, g_mix, w_in, w_gk2, b_gk2,
           gn_gla, lb_logits, gn_hgrn, w_out, g_ffn, w_router, b_router, w_gu, b_gu, w_dn, b_dn,
           g_ple, w_ple_gate, w_ple_proj, g_final):
    batch, seq, _ = x_prompt.shape
    dec_batch, dec_seq, _ = x_sample.shape
    tp, ts = batch * seq, dec_batch * dec_seq
    xp = x_prompt.reshape(tp, D_MODEL)
    xs = x_sample.reshape(ts, D_MODEL)

    wgk = jnp.concatenate(
        [w_gk2[0], jnp.zeros((R1_PAD - GLA_LOWRANK, GLA_KDIM), f32)], axis=0).astype(bf16)
    wr = w_router[0].T.astype(bf16)
    br = b_router[0].reshape(N_EXPERTS, 1)

    z = _in_proj(xp, xs, g_mix[0].reshape(1, D_MODEL), w_in[0])

    mixer_params = (wgk, b_gk2[0].reshape(1, GLA_KDIM), gn_gla[0].reshape(1, GLA_DV),
                    lb_logits[0:2], gn_hgrn[0].reshape(1, HGRN_DV))
    o_p, new_gla_p, new_hgrn_p = _mixer_prompt(
        z, *mixer_params, batch=batch, seq=seq, rows_per_step=512, block_rows=256)
    z_s = jnp.pad(z[tp:].reshape(dec_batch, dec_seq, Z_W), ((0, 0), (0, SAMPLE_PAD_T - dec_seq), (0, 0)))
    o_s, new_gla_s, new_hgrn_s = _mixer_sample(
        z_s.reshape(dec_batch * SAMPLE_PAD_T, Z_W), state_gla, state_hgrn, *mixer_params,
        steps=dec_seq, batch_block=8)
    o_s = o_s.reshape(dec_batch, SAMPLE_PAD_T, D_MODEL)[:, :dec_seq].reshape(ts, D_MODEL).astype(bf16)

    h1, xn, top_i, top_g, rank, top_ic, rank_c, counts, part_counts = _out_proj(
        xp, xs, o_p, o_s, w_out[0].astype(bf16), g_ffn[0].reshape(1, D_MODEL), wr, br)

    t = tp + ts
    n_blocks = -(-(t * TOP_K) // MOE_TILE) + N_EXPERTS
    tile_table, block_e, block_seg, next_e, block_valid, n_used = _routing_tables(
        top_i, counts[:, 0], part_counts, n_blocks)
    x_sorted = _dispatch(block_valid, tile_table, top_i, rank, xn)
    y_sorted = _experts(block_e, block_seg, next_e, block_valid, n_used, x_sorted,
                        w_gu[0], b_gu[0].reshape(N_EXPERTS, 1, 2 * D_FF),
                        w_dn[0], b_dn[0].reshape(N_EXPERTS, 1, D_MODEL))

    y_p, y_s = _combine(tile_table, y_sorted, h1, top_g, top_ic, rank_c, p_prompt[0].reshape(tp, PLE_DIM),
                        p_sample[0].reshape(ts, PLE_DIM), g_ple[0].reshape(1, D_MODEL),
                        w_ple_gate[0].astype(bf16), w_ple_proj[0].astype(bf16), g_final.reshape(1, D_MODEL))

    return (y_p.reshape(batch, seq, D_MODEL), y_s.reshape(dec_batch, dec_seq, D_MODEL),
            new_gla_p, new_hgrn_p, new_gla_s, new_hgrn_s)
```

```python
import functools

import jax
import jax.numpy as jnp
from jax import lax
from jax.experimental import pallas as pl
from jax.experimental.pallas import tpu as pltpu

f32 = jnp.float32
bf16 = jnp.bfloat16
i32 = jnp.int32

D_MODEL = 1024
GLA_HEADS = 4
GLA_DK = 64
GLA_DV = 128
GLA_KDIM = GLA_HEADS * GLA_DK
GLA_WIDTH = GLA_HEADS * GLA_DV
GLA_LOWRANK = 16
GLA_NORMALIZER = 16.0
HGRN_HEADS = 4
HGRN_DK = 128
HGRN_DV = 128
HGRN_FDIM = HGRN_HEADS * HGRN_DK
HGRN_WIDTH = HGRN_HEADS * HGRN_DV
CHUNK = 64
N_EXPERTS = 32
TOP_K = 4
D_FF = 1024
SWIGLU_LIMIT = 7.0
SWIGLU_ALPHA = 1.702
PLE_DIM = 256
EPS = 1e-6

LANES = 128
SAMPLE_PAD_T = 8
XSLABS = D_MODEL // (2 * LANES)
ROUTE_ROWS = 8
R1_PAD = LANES
Z_Q1 = 0
Z_K1 = Z_Q1 + GLA_KDIM
Z_V1 = Z_K1 + GLA_KDIM
Z_G1 = Z_V1 + GLA_WIDTH
Z_R1 = Z_G1 + GLA_WIDTH
Z_Q2 = Z_R1 + R1_PAD
Z_F2 = Z_Q2 + HGRN_FDIM
Z_I2 = Z_F2 + HGRN_FDIM
Z_G2 = Z_I2 + HGRN_WIDTH
Z_W = Z_G2 + HGRN_WIDTH

TOKEN_TILE = 512
MOE_TILE = 512
DISPATCH_TILE = 256
COMBINE_TILE = DISPATCH_TILE
RUN_BITS = DISPATCH_TILE.bit_length()
RUN_SHIFT = 0
RUN_COUNT = RUN_SHIFT + N_EXPERTS
RUN_SRC = RUN_COUNT + N_EXPERTS
RUN_DST = RUN_SRC + RUN_BITS * N_EXPERTS
RUN_TABLE = RUN_DST + RUN_BITS * N_EXPERTS
VMEM_LIMIT = 56 * 1024 * 1024


def _rms(x, g):
    return x * lax.rsqrt(jnp.mean(x * x, axis=-1, keepdims=True) + EPS) * g


def _sigmoid(x):
    return 0.5 * jnp.tanh(0.5 * x) + 0.5


def _dot(a, b):
    return jnp.dot(a, b, preferred_element_type=f32)


def _dot_nt(a, b):
    return lax.dot_general(a, b, (((1,), (1,)), ((), ())), preferred_element_type=f32)


def _dot_tn(a, b):
    return lax.dot_general(a, b, (((0,), (0,)), ((), ())), preferred_element_type=f32)


def _cumsum_rows(tri, x):
    hi = x.astype(bf16)
    r1 = x - hi.astype(f32)
    mid = r1.astype(bf16)
    lo = (r1 - mid.astype(f32)).astype(bf16)
    return _dot(tri, hi) + _dot(tri, mid) + _dot(tri, lo)


def _select_rows(is_prompt, a_ref, b_ref):
    return jnp.where(is_prompt, a_ref[...], b_ref[...])


def _inproj_kernel(xp_ref, xs_ref, g_ref, w_hbm, z_ref, w_f32, w_bf, sem,
                   *, n_prompt_tiles, col_chunk, cast_rows):
    @pl.when(pl.program_id(0) == 0)
    def _():
        copy = pltpu.make_async_copy(w_hbm, w_f32, sem)
        copy.start()
        copy.wait()
        r1_end = Z_R1 + GLA_LOWRANK

        def cast(j, carry):
            rows = pl.ds(pl.multiple_of(j * cast_rows, cast_rows), cast_rows)
            w_bf[rows, 0:Z_R1] = w_f32[rows, 0:Z_R1].astype(bf16)
            w_bf[rows, Z_R1:Z_Q2] = jnp.concatenate(
                [w_f32[rows, Z_R1:r1_end], jnp.zeros((cast_rows, R1_PAD - GLA_LOWRANK), f32)],
                axis=1).astype(bf16)
            w_bf[rows, Z_Q2:Z_W] = w_f32[rows, r1_end:].astype(bf16)
            return carry
        lax.fori_loop(0, D_MODEL // cast_rows, cast, 0)

    is_prompt = pl.program_id(0) < n_prompt_tiles
    x = _select_rows(is_prompt, xp_ref, xs_ref)
    a = _rms(x, g_ref[...]).astype(bf16)
    for c0 in range(0, Z_W, col_chunk):
        c1 = min(c0 + col_chunk, Z_W)
        z_ref[:, c0:c1] = _dot(a, w_bf[:, c0:c1]).astype(bf16)


def _in_proj(xp, xs, g, w):
    tp, ts = xp.shape[0], xs.shape[0]
    npt = tp // TOKEN_TILE
    nt = npt + ts // TOKEN_TILE
    return pl.pallas_call(
        functools.partial(_inproj_kernel, n_prompt_tiles=npt, col_chunk=512, cast_rows=64),
        grid=(nt,),
        in_specs=[
            pl.BlockSpec((TOKEN_TILE, D_MODEL), lambda i: (jnp.minimum(i, npt - 1), 0)),
            pl.BlockSpec((TOKEN_TILE, D_MODEL), lambda i: (jnp.maximum(i - npt, 0), 0)),
            pl.BlockSpec((1, D_MODEL), lambda i: (0, 0)),
            pl.BlockSpec(memory_space=pl.ANY),
        ],
        out_specs=pl.BlockSpec((TOKEN_TILE, Z_W), lambda i: (i, 0)),
        out_shape=jax.ShapeDtypeStruct((tp + ts, Z_W), bf16),
        scratch_shapes=[
            pltpu.VMEM(w.shape, f32),
            pltpu.VMEM((D_MODEL, Z_W), bf16),
            pltpu.SemaphoreType.DMA(()),
        ],
        compiler_params=pltpu.CompilerParams(
            dimension_semantics=("arbitrary",), vmem_limit_bytes=VMEM_LIMIT),
        name="in_proj",
    )(xp, xs, g, w)


def _block_consts(r, c):
    shift = c.bit_length() - 1
    row = lax.broadcasted_iota(i32, (r, r), 0)
    col = lax.broadcasted_iota(i32, (r, r), 1)
    causal = (lax.shift_right_logical(row, shift) == lax.shift_right_logical(col, shift)) & (row >= col)
    return causal.astype(bf16), causal


def _subchunk_row(x, c, idx):
    parts = [jnp.broadcast_to(x[j * c + idx:j * c + idx + 1, :], (c, x.shape[1]))
             for j in range(x.shape[0] // c)]
    return jnp.concatenate(parts, axis=0)


def _gated_block(q, k, v, log_a, states, *, c, scale, ref_row, tri, causal, heads_per_group, scan):
    r = q.shape[0]
    nsub = r // c
    groups = q.shape[1] // LANES
    b = _cumsum_rows(tri, log_a)
    b_ref = _subchunk_row(b, c, ref_row)
    b_last = _subchunk_row(b, c, c - 1)
    qs = q * scale
    q_r = qs * jnp.exp(b - b_ref)
    k_r = (k * jnp.exp(b_ref - b)).astype(bf16)
    k_end = k * jnp.exp(b_last - b)
    q_b = qs * jnp.exp(b)
    decay = jnp.exp(b_last)
    vb = v.astype(bf16)
    lane = lax.broadcasted_iota(i32, (r, LANES), 1)
    width = LANES // heads_per_group
    outs, new_states = [], []
    for g in range(groups):
        gl = slice(g * LANES, (g + 1) * LANES)
        masks = [None] if heads_per_group == 1 else [
            (lane >= j * width) & (lane < (j + 1) * width) for j in range(heads_per_group)]
        pick = lambda x, m: x if m is None else jnp.where(m, x, 0.0)
        q_r_h = [pick(q_r[:, gl], m).astype(bf16) for m in masks]
        q_b_h = [pick(q_b[:, gl], m).astype(bf16) for m in masks]
        k_end_h = [pick(k_end[:, gl], m).astype(bf16) for m in masks]
        v_h = [vb[:, (g * heads_per_group + j) * LANES:(g * heads_per_group + j + 1) * LANES]
               for j in range(heads_per_group)]
        o_intra = []
        for j in range(heads_per_group):
            att = _dot_nt(q_r_h[j], k_r[:, gl])
            att = jnp.where(causal, att, 0.0).astype(bf16)
            o_intra.append(_dot(att, v_h[j]))
        upd = []
        for s in range(nsub):
            rs = slice(s * c, (s + 1) * c)
            u = _dot_tn(v_h[0][rs], k_end_h[0][rs])
            for j in range(1, heads_per_group):
                u = u + _dot_tn(v_h[j][rs], k_end_h[j][rs])
            upd.append(u)
        o_inter = [[] for _ in range(heads_per_group)]
        st = states[g] if scan else None
        group_states = []
        for s in range(nsub):
            rs = slice(s * c, (s + 1) * c)
            if not scan:
                st = states[s][g]
            st_b = st.astype(bf16)
            for j in range(heads_per_group):
                o_inter[j].append(_dot_nt(q_b_h[j][rs], st_b))
            st = st * decay[s * c:s * c + 1, gl] + upd[s]
            if not scan:
                group_states.append(st)
        for j in range(heads_per_group):
            outs.append(o_intra[j] + jnp.concatenate(o_inter[j], axis=0))
        new_states.append(st if scan else group_states)
    if not scan:
        new_states = [[new_states[g][s] for g in range(groups)] for s in range(nsub)]
    return outs, new_states


def _head_norm(o, g):
    return o * lax.rsqrt(jnp.mean(o * o, axis=-1, keepdims=True) + EPS) * g


def _mixer_block(zb, st_gla, st_hgrn, wgk, bgk, gng, lb, gnh, *, c, ref_row, tri, causal, valid, scan):
    z = zb.astype(f32)
    gk = _dot(zb[:, Z_R1:Z_R1 + R1_PAD], wgk) + bgk
    log_a = (jnp.minimum(gk, 0.0) - jnp.log(1.0 + jnp.exp(-jnp.abs(gk)))) / GLA_NORMALIZER
    k1 = z[:, Z_K1:Z_K1 + GLA_KDIM]
    if valid is not None:
        log_a = jnp.where(valid, log_a, 0.0)
        k1 = jnp.where(valid, k1, 0.0)
    o1, st_gla = _gated_block(
        z[:, Z_Q1:Z_Q1 + GLA_KDIM], k1, z[:, Z_V1:Z_V1 + GLA_WIDTH], log_a, st_gla,
        c=c, scale=GLA_DK ** -0.5, ref_row=ref_row, tri=tri, causal=causal,
        heads_per_group=LANES // GLA_DK, scan=scan)
    forget = lb + (1.0 - lb) * _sigmoid(z[:, Z_F2:Z_F2 + HGRN_FDIM])
    k2 = 1.0 - forget
    log_f = jnp.log(forget)
    if valid is not None:
        log_f = jnp.where(valid, log_f, 0.0)
        k2 = jnp.where(valid, k2, 0.0)
    q2 = z[:, Z_Q2:Z_Q2 + HGRN_FDIM]
    i2 = z[:, Z_I2:Z_I2 + HGRN_WIDTH]
    o2, st_hgrn = _gated_block(
        q2 * _sigmoid(q2), k2, i2 * _sigmoid(i2), log_f, st_hgrn,
        c=c, scale=1.0, ref_row=ref_row, tri=tri, causal=causal, heads_per_group=1, scan=scan)
    cols = []
    for h in range(GLA_HEADS):
        g1 = z[:, Z_G1 + h * GLA_DV:Z_G1 + (h + 1) * GLA_DV]
        cols.append(_head_norm(o1[h], gng) * (g1 * _sigmoid(g1)))
    for h in range(HGRN_HEADS):
        g2 = z[:, Z_G2 + h * HGRN_DV:Z_G2 + (h + 1) * HGRN_DV]
        cols.append(_head_norm(o2[h], gnh) * _sigmoid(g2))
    return jnp.concatenate(cols, axis=1), st_gla, st_hgrn


def _lower_bound(lbl_ref):
    l = lbl_ref[...]
    m = jnp.max(l, axis=0, keepdims=True)
    e = jnp.exp(l - m)
    return e[0:1, :] / jnp.sum(e, axis=0, keepdims=True)


GLA_GROUPS = GLA_KDIM // LANES


def _gla_state_t(s_ref_heads):
    return s_ref_heads.reshape(LANES, GLA_DV).T


def _gla_state_from_t(st):
    return st.T.reshape(LANES // GLA_DK, GLA_DK, GLA_DV)


def _mixer_prompt_kernel(z_ref, wgk_ref, bgk_ref, gng_ref, lbl_ref, gnh_ref,
                         o_ref, sg_out, sh_out, sg_t, sh_t, *, block_rows):
    j = pl.program_id(1)

    @pl.when(j == 0)
    def _():
        sg_t[...] = jnp.zeros_like(sg_t)
        sh_t[...] = jnp.zeros_like(sh_t)

    tri, causal = _block_consts(block_rows, CHUNK)
    lb = _lower_bound(lbl_ref)
    wgk, bgk, gng, gnh = wgk_ref[...], bgk_ref[...], gng_ref[...], gnh_ref[...]

    def body(blk, carry):
        r0 = pl.multiple_of(blk * block_rows, block_rows)
        o, st_g, st_h = _mixer_block(
            z_ref[pl.ds(r0, block_rows), :], [sg_t[g] for g in range(GLA_GROUPS)],
            [sh_t[h] for h in range(HGRN_HEADS)], wgk, bgk, gng, lb, gnh,
            c=CHUNK, ref_row=CHUNK // 2, tri=tri, causal=causal, valid=None, scan=True)
        o_ref[pl.ds(r0, block_rows), :] = o.astype(bf16)
        for g in range(GLA_GROUPS):
            sg_t[g] = st_g[g]
        for h in range(HGRN_HEADS):
            sh_t[h] = st_h[h]
        return carry

    lax.fori_loop(0, z_ref.shape[0] // block_rows, body, 0)

    @pl.when(j == pl.num_programs(1) - 1)
    def _():
        per = LANES // GLA_DK
        for g in range(GLA_GROUPS):
            sg_out[0, 0, g * per:(g + 1) * per] = _gla_state_from_t(sg_t[g])
        for h in range(HGRN_HEADS):
            sh_out[0, 0, h] = sh_t[h].T


def _mixer_prompt(z, wgk, bgk, gng, lbl, gnh, *, batch, seq, rows_per_step, block_rows):
    nj = seq // rows_per_step
    full = lambda shape: pl.BlockSpec(shape, lambda b, j: (0,) * len(shape))
    return pl.pallas_call(
        functools.partial(_mixer_prompt_kernel, block_rows=block_rows),
        grid=(batch, nj),
        in_specs=[
            pl.BlockSpec((rows_per_step, Z_W), lambda b, j: (b * nj + j, 0)),
            full((R1_PAD, GLA_KDIM)), full((1, GLA_KDIM)), full((1, GLA_DV)),
            full((2, HGRN_FDIM)), full((1, HGRN_DV)),
        ],
        out_specs=[
            pl.BlockSpec((rows_per_step, D_MODEL), lambda b, j: (b * nj + j, 0)),
            pl.BlockSpec((1, 1, GLA_HEADS, GLA_DK, GLA_DV), lambda b, j: (0, b, 0, 0, 0)),
            pl.BlockSpec((1, 1, HGRN_HEADS, HGRN_DK, HGRN_DV), lambda b, j: (0, b, 0, 0, 0)),
        ],
        out_shape=[
            jax.ShapeDtypeStruct((batch * seq, D_MODEL), bf16),
            jax.ShapeDtypeStruct((1, batch, GLA_HEADS, GLA_DK, GLA_DV), f32),
            jax.ShapeDtypeStruct((1, batch, HGRN_HEADS, HGRN_DK, HGRN_DV), f32),
        ],
        scratch_shapes=[
            pltpu.VMEM((GLA_GROUPS, GLA_DV, LANES), f32),
            pltpu.VMEM((HGRN_HEADS, HGRN_DV, HGRN_DK), f32),
        ],
        compiler_params=pltpu.CompilerParams(
            dimension_semantics=("arbitrary", "arbitrary"), vmem_limit_bytes=VMEM_LIMIT),
        name="mixer_prompt",
    )(z, wgk, bgk, gng, lbl, gnh)


def _mixer_sample_kernel(z_ref, sg_in, sh_in, wgk_ref, bgk_ref, gng_ref, lbl_ref, gnh_ref,
                         o_ref, sg_out, sh_out, *, batch_block, steps):
    rows = batch_block * SAMPLE_PAD_T
    tri, causal = _block_consts(rows, SAMPLE_PAD_T)
    step = lax.broadcasted_iota(i32, (rows, 1), 0) & (SAMPLE_PAD_T - 1)
    valid = step < steps
    lb = _lower_bound(lbl_ref)
    per = LANES // GLA_DK
    st_g = [[_gla_state_t(sg_in[0, s, g * per:(g + 1) * per]) for g in range(GLA_GROUPS)]
            for s in range(batch_block)]
    st_h = [[sh_in[0, s, h].T for h in range(HGRN_HEADS)] for s in range(batch_block)]
    o, st_g, st_h = _mixer_block(
        z_ref[...], st_g, st_h, wgk_ref[...], bgk_ref[...], gng_ref[...], lb, gnh_ref[...],
        c=SAMPLE_PAD_T, ref_row=steps // 2, tri=tri, causal=causal, valid=valid, scan=False)
    o_ref[...] = o
    for s in range(batch_block):
        for g in range(GLA_GROUPS):
            sg_out[0, s, g * per:(g + 1) * per] = _gla_state_from_t(st_g[s][g])
        for h in range(HGRN_HEADS):
            sh_out[0, s, h] = st_h[s][h].T


def _mixer_sample(z2, state_gla, state_hgrn, wgk, bgk, gng, lbl, gnh, *, steps, batch_block):
    batch = z2.shape[0] // SAMPLE_PAD_T
    rows = batch_block * SAMPLE_PAD_T
    full = lambda shape: pl.BlockSpec(shape, lambda i: (0,) * len(shape))
    sg_spec = pl.BlockSpec((1, batch_block, GLA_HEADS, GLA_DK, GLA_DV), lambda i: (0, i, 0, 0, 0))
    sh_spec = pl.BlockSpec((1, batch_block, HGRN_HEADS, HGRN_DK, HGRN_DV), lambda i: (0, i, 0, 0, 0))
    return pl.pallas_call(
        functools.partial(_mixer_sample_kernel, batch_block=batch_block, steps=steps),
        grid=(batch // batch_block,),
        in_specs=[
            pl.BlockSpec((rows, Z_W), lambda i: (i, 0)),
            sg_spec, sh_spec,
            full((R1_PAD, GLA_KDIM)), full((1, GLA_KDIM)), full((1, GLA_DV)),
            full((2, HGRN_FDIM)), full((1, HGRN_DV)),
        ],
        out_specs=[pl.BlockSpec((rows, D_MODEL), lambda i: (i, 0)), sg_spec, sh_spec],
        out_shape=[
            jax.ShapeDtypeStruct((batch * SAMPLE_PAD_T, D_MODEL), f32),
            jax.ShapeDtypeStruct(state_gla.shape, f32),
            jax.ShapeDtypeStruct(state_hgrn.shape, f32),
        ],
        compiler_params=pltpu.CompilerParams(
            dimension_semantics=("arbitrary",), vmem_limit_bytes=VMEM_LIMIT),
        name="mixer_sample",
    )(z2, state_gla, state_hgrn, wgk, bgk, gng, lbl, gnh)


HIGH_HALF = 0xFFFF0000


def _pack_rows(x):
    bits = lax.bitcast_convert_type(x, jnp.uint32)
    return [(bits[:, (2 * s + 1) * LANES:(2 * s + 2) * LANES] & jnp.uint32(HIGH_HALF))
            | (bits[:, 2 * s * LANES:(2 * s + 1) * LANES] >> 16) for s in range(x.shape[1] // (2 * LANES))]


def _unpack_rows(ref, start, rows):
    cols = []
    for s in range(XSLABS):
        w = ref[pl.ds(start + s, rows, stride=XSLABS), :]
        cols.append(lax.bitcast_convert_type(w << 16, f32).astype(bf16))
        cols.append(lax.bitcast_convert_type(w & jnp.uint32(HIGH_HALF), f32).astype(bf16))
    return jnp.concatenate(cols, axis=1)


def _outproj_kernel(xp_ref, xs_ref, op_ref, os_ref, wo_ref, g_ref, wr_ref, br_ref,
                    h_ref, xn_ref, ti_ref, tg_ref, rank_ref, tic_ref, rankc_ref, cnt_ref, part_ref, run_cnt,
                    *, n_prompt_tiles):
    i = pl.program_id(0)

    @pl.when(i == 0)
    def _():
        run_cnt[...] = jnp.zeros_like(run_cnt)

    x = _select_rows(i < n_prompt_tiles, xp_ref, xs_ref)
    o = _select_rows(i < n_prompt_tiles, op_ref, os_ref)
    h = x + _dot(o, wo_ref[...])
    h_ref[...] = h
    xn = _rms(h, g_ref[...])
    xn_ref[...] = xn.astype(bf16)
    l = _dot_nt(wr_ref[...], xn.astype(bf16)) + br_ref[...]
    expert = lax.broadcasted_iota(i32, l.shape, 0)
    neg = jnp.float32(-jnp.inf)
    vals, idxs = [], []
    for _ in range(TOP_K):
        m = jnp.max(l, axis=0, keepdims=True)
        idx = jnp.min(jnp.where(l == m, expert, N_EXPERTS), axis=0, keepdims=True)
        vals.append(m)
        idxs.append(idx)
        l = jnp.where(expert == idx, neg, l)
    es = [jnp.exp(v - vals[0]) for v in vals]
    denom = es[0] + es[1] + es[2] + es[3]

    onehot = jnp.zeros(l.shape, f32)
    for k in range(TOP_K):
        onehot = onehot + (expert == idxs[k]).astype(f32)
    n = l.shape[1]
    earlier = (lax.broadcasted_iota(i32, (n, n), 0) < lax.broadcasted_iota(i32, (n, n), 1)).astype(bf16)
    ahead = _dot(onehot.astype(bf16), earlier) + run_cnt[:, 0:1]
    run_cnt[...] = run_cnt[...] + jnp.sum(onehot, axis=1, keepdims=True)
    cnt_ref[...] = run_cnt[...].astype(i32)
    lane = lax.broadcasted_iota(i32, (N_EXPERTS, LANES), 1)
    parts = jnp.zeros((N_EXPERTS, LANES), f32)
    for p in range(n // DISPATCH_TILE):
        c = jnp.sum(onehot[:, p * DISPATCH_TILE:(p + 1) * DISPATCH_TILE], axis=1, keepdims=True)
        parts = jnp.where(lane == p, c, parts)
    part_ref[0] = parts.astype(i32)

    ranks = [jnp.sum(jnp.where(expert == idxs[k], ahead, 0.0), axis=0, keepdims=True).astype(i32)
             for k in range(TOP_K)]
    pad_i = jnp.zeros((ROUTE_ROWS - TOP_K, n), i32)
    ti_ref[...] = jnp.concatenate(idxs + [pad_i], axis=0)
    rank_ref[...] = jnp.concatenate(ranks + [pad_i], axis=0)
    gate_rows = jnp.concatenate([e / denom for e in es] + [jnp.zeros((LANES - TOP_K, n), f32)], axis=0)
    tg_ref[...] = gate_rows.T
    pad_wide = jnp.zeros((LANES - TOP_K, n), i32)
    tic_ref[...] = jnp.concatenate(idxs + [pad_wide], axis=0).T
    rankc_ref[...] = jnp.concatenate(ranks + [pad_wide], axis=0).T


def _out_proj(xp, xs, o_p, o_s, wo, g, wr, br):
    tp, ts = xp.shape[0], xs.shape[0]
    npt = tp // TOKEN_TILE
    nt = npt + ts // TOKEN_TILE
    t = tp + ts
    row = lambda w: pl.BlockSpec((TOKEN_TILE, w), lambda i: (i, 0))
    route = pl.BlockSpec((ROUTE_ROWS, TOKEN_TILE), lambda i: (0, i))
    full = lambda shape: pl.BlockSpec(shape, lambda i: (0,) * len(shape))
    prompt_rows = pl.BlockSpec((TOKEN_TILE, D_MODEL), lambda i: (jnp.minimum(i, npt - 1), 0))
    sample_rows = pl.BlockSpec((TOKEN_TILE, D_MODEL), lambda i: (jnp.maximum(i - npt, 0), 0))
    return pl.pallas_call(
        functools.partial(_outproj_kernel, n_prompt_tiles=npt),
        grid=(nt,),
        in_specs=[
            prompt_rows, sample_rows, prompt_rows, sample_rows,
            full((D_MODEL, D_MODEL)), full((1, D_MODEL)),
            full((N_EXPERTS, D_MODEL)), full((N_EXPERTS, 1)),
        ],
        out_specs=[row(D_MODEL), row(D_MODEL),
                   route, row(LANES), route, row(LANES), row(LANES), full((N_EXPERTS, LANES)),
                   pl.BlockSpec((1, N_EXPERTS, LANES), lambda i: (i, 0, 0))],
        out_shape=[
            jax.ShapeDtypeStruct((t, D_MODEL), f32),
            jax.ShapeDtypeStruct((t, D_MODEL), bf16),
            jax.ShapeDtypeStruct((ROUTE_ROWS, t), i32),
            jax.ShapeDtypeStruct((t, LANES), f32),
            jax.ShapeDtypeStruct((ROUTE_ROWS, t), i32),
            jax.ShapeDtypeStruct((t, LANES), i32),
            jax.ShapeDtypeStruct((t, LANES), i32),
            jax.ShapeDtypeStruct((N_EXPERTS, LANES), i32),
            jax.ShapeDtypeStruct((nt, N_EXPERTS, LANES), i32),
        ],
        scratch_shapes=[pltpu.VMEM((N_EXPERTS, LANES), f32)],
        compiler_params=pltpu.CompilerParams(
            dimension_semantics=("arbitrary",), vmem_limit_bytes=VMEM_LIMIT),
        name="out_proj_router",
    )(xp, xs, o_p, o_s, wo, g, wr, br)


def _dispatch_kernel(bv_ref, nz_ref, tab_ref, ti_ref, rank_ref, xn_ref, xs_hbm, zeros, stage, sems, zsem,
                     *, n_blocks):
    @pl.when(pl.program_id(0) == 0)
    def _():
        zeros[...] = jnp.zeros_like(zeros)

        def fill(b, carry):
            @pl.when(bv_ref[b] < MOE_TILE)
            def _():
                r0 = pl.multiple_of(b * (MOE_TILE * XSLABS), MOE_TILE * XSLABS)
                pltpu.make_async_copy(zeros, xs_hbm.at[pl.ds(r0, MOE_TILE * XSLABS)], zsem).start()
            return carry

        def drain(b, carry):
            pltpu.make_async_copy(zeros, xs_hbm.at[pl.ds(0, MOE_TILE * XSLABS)], zsem).wait()
            return carry

        lax.fori_loop(0, n_blocks, fill, 0)
        lax.fori_loop(0, nz_ref[0], drain, 0)

    i = pl.program_id(0)
    nt = pl.num_programs(0)
    n = DISPATCH_TILE * TOP_K
    slot_rows = n * XSLABS
    slot = i % 2
    base = pl.multiple_of(slot * slot_rows, slot_rows)

    def wait_rows(s):
        pltpu.make_async_copy(stage.at[pl.ds(0, slot_rows)], xs_hbm.at[pl.ds(0, slot_rows)], sems.at[s]).wait()

    ti = ti_ref[...]
    shift = jnp.zeros(ti.shape, i32)
    for e in range(N_EXPERTS):
        shift = jnp.where(ti == e, tab_ref[0, 0, RUN_SHIFT + e], shift)
    staged = rank_ref[...] + shift
    row = lax.broadcasted_iota(i32, (n, DISPATCH_TILE), 0)
    pick = row == staged[0:1, :]
    for k in range(1, TOP_K):
        pick = pick | (row == staged[k:k + 1, :])
    rows_sorted = _dot(pick.astype(bf16), xn_ref[...])
    for s, words in enumerate(_pack_rows(rows_sorted)):
        stage[pl.ds(base + s, n, stride=XSLABS), :] = words

    for bit in range(RUN_BITS - 1, -1, -1):
        size = (1 << bit) * XSLABS

        def issue(p, carry, bit=bit, size=size):
            src0 = pl.multiple_of(tab_ref[0, 0, RUN_SRC + bit * N_EXPERTS + p] * XSLABS, XSLABS)
            dst0 = pl.multiple_of(tab_ref[0, 0, RUN_DST + bit * N_EXPERTS + p] * XSLABS, XSLABS)
            pltpu.make_async_copy(stage.at[pl.ds(base + src0, size)], xs_hbm.at[pl.ds(dst0, size)],
                                  sems.at[slot]).start(priority=bit % 2)
            return carry

        lax.fori_loop(0, tab_ref[0, 0, RUN_COUNT + bit], issue, 0)

    @pl.when(i > 0)
    def _():
        wait_rows(1 - slot)

    @pl.when(i == nt - 1)
    def _():
        wait_rows(slot)


def _dispatch(block_valid, tile_table, top_i, rank, xn):
    t = xn.shape[0]
    nt = t // DISPATCH_TILE
    n_blocks = block_valid.shape[0]
    n_zero = jnp.sum((block_valid < MOE_TILE).astype(i32)).reshape(1)
    route = pl.BlockSpec((ROUTE_ROWS, DISPATCH_TILE), lambda i, bv, nz: (0, i))
    grid_spec = pltpu.PrefetchScalarGridSpec(
        num_scalar_prefetch=2,
        grid=(nt,),
        in_specs=[
            pl.BlockSpec((1, 1, RUN_TABLE), lambda i, bv, nz: (i, 0, 0), memory_space=pltpu.SMEM),
            route, route,
            pl.BlockSpec((DISPATCH_TILE, D_MODEL), lambda i, bv, nz: (i, 0)),
        ],
        out_specs=pl.BlockSpec(memory_space=pl.ANY),
        scratch_shapes=[
            pltpu.VMEM((MOE_TILE * XSLABS, LANES), jnp.uint32),
            pltpu.VMEM((2 * DISPATCH_TILE * TOP_K * XSLABS, LANES), jnp.uint32),
            pltpu.SemaphoreType.DMA((2,)),
            pltpu.SemaphoreType.DMA(()),
        ],
    )
    return pl.pallas_call(
        functools.partial(_dispatch_kernel, n_blocks=n_blocks),
        grid_spec=grid_spec,
        out_shape=jax.ShapeDtypeStruct((n_blocks * MOE_TILE * XSLABS, LANES), jnp.uint32),
        compiler_params=pltpu.CompilerParams(
            dimension_semantics=("arbitrary",), vmem_limit_bytes=VMEM_LIMIT),
        name="moe_dispatch",
    )(block_valid, n_zero, tile_table, top_i, rank, xn)


def _expert_kernel(be_ref, seg_ref, nxt_ref, bv_ref, nb_ref, x_ref, wgu_hbm, bgu_ref, wdn_hbm, bdn_ref, y_ref,
                   wgu_f32, wdn_f32, wgu_bf, wdn_bf, sems, *, ff_chunk, cast_rows):
    i = pl.program_id(0)

    @pl.when(i >= nb_ref[0])
    def _():
        y_ref[...] = jnp.zeros_like(y_ref)

    def fetch(e, slot):
        return (pltpu.make_async_copy(wgu_hbm.at[e], wgu_f32.at[slot], sems.at[slot, 0]),
                pltpu.make_async_copy(wdn_hbm.at[e], wdn_f32.at[slot], sems.at[slot, 1]))

    @pl.when((i < nb_ref[0]) & ((i == 0) | (be_ref[i] != be_ref[jnp.maximum(i - 1, 0)])))
    def _():
        slot = seg_ref[i] % 2

        @pl.when(i == 0)
        def _():
            for c in fetch(be_ref[0], 0):
                c.start()

        for c in fetch(be_ref[i], slot):
            c.wait()

        @pl.when(nxt_ref[i] >= 0)
        def _():
            for c in fetch(nxt_ref[i], 1 - slot):
                c.start()

        def cast(j, carry):
            r0 = pl.multiple_of(j * cast_rows, cast_rows)
            wgu_bf[pl.ds(r0, cast_rows), :] = wgu_f32[slot, pl.ds(r0, cast_rows), :].astype(bf16)
            wdn_bf[pl.ds(r0, cast_rows), :] = wdn_f32[slot, pl.ds(r0, cast_rows), :].astype(bf16)
            return carry
        lax.fori_loop(0, D_MODEL // cast_rows, cast, 0)

    def mlp(rows):
        xb = _unpack_rows(x_ref, 0, rows)
        hid = []
        for c0 in range(0, D_FF, ff_chunk):
            gate = _dot(xb, wgu_bf[:, c0:c0 + ff_chunk]) + bgu_ref[0, :, c0:c0 + ff_chunk]
            up = (_dot(xb, wgu_bf[:, D_FF + c0:D_FF + c0 + ff_chunk])
                  + bgu_ref[0, :, D_FF + c0:D_FF + c0 + ff_chunk])
            gate = jnp.minimum(gate, SWIGLU_LIMIT)
            up = jnp.clip(up, -SWIGLU_LIMIT, SWIGLU_LIMIT)
            hid.append(((up + 1.0) * (gate * _sigmoid(gate * SWIGLU_ALPHA))).astype(bf16))
        hid = jnp.concatenate(hid, axis=1)
        group = 2 * LANES
        for n0 in range(0, D_MODEL, group):
            y = _dot(hid, wdn_bf[:, n0:n0 + group]) + bdn_ref[0, :, n0:n0 + group]
            (words,) = _pack_rows(y.astype(bf16).astype(f32))
            y_ref[pl.ds(n0 // group, rows, stride=XSLABS), :] = words

    quarter = MOE_TILE // 4
    for q in range(1, 5):
        rows = q * quarter

        @pl.when((i < nb_ref[0]) & (bv_ref[i] > rows - quarter) & (bv_ref[i] <= rows))
        def _(rows=rows):
            mlp(rows)
            if rows < MOE_TILE:
                rest = (MOE_TILE - rows) * XSLABS
                y_ref[pl.ds(rows * XSLABS, rest), :] = jnp.zeros((rest, LANES), jnp.uint32)


def _experts(block_e, block_seg, next_e, block_valid, n_used, xs, wgu, bgu, wdn, bdn):
    n_blocks = block_e.shape[0]
    grid_spec = pltpu.PrefetchScalarGridSpec(
        num_scalar_prefetch=5,
        grid=(n_blocks,),
        in_specs=[
            pl.BlockSpec((MOE_TILE * XSLABS, LANES),
                         lambda i, be, sg, nx, bv, nb: (jnp.minimum(i, nb[0] - 1), 0)),
            pl.BlockSpec(memory_space=pl.ANY),
            pl.BlockSpec((1, 1, 2 * D_FF), lambda i, be, sg, nx, bv, nb: (be[i], 0, 0)),
            pl.BlockSpec(memory_space=pl.ANY),
            pl.BlockSpec((1, 1, D_MODEL), lambda i, be, sg, nx, bv, nb: (be[i], 0, 0)),
        ],
        out_specs=pl.BlockSpec((MOE_TILE * XSLABS, LANES), lambda i, be, sg, nx, bv, nb: (i, 0)),
        scratch_shapes=[
            pltpu.VMEM((2, D_MODEL, 2 * D_FF), f32),
            pltpu.VMEM((2, D_FF, D_MODEL), f32),
            pltpu.VMEM((D_MODEL, 2 * D_FF), bf16),
            pltpu.VMEM((D_FF, D_MODEL), bf16),
            pltpu.SemaphoreType.DMA((2, 2)),
        ],
    )
    return pl.pallas_call(
        functools.partial(_expert_kernel, ff_chunk=256, cast_rows=128),
        grid_spec=grid_spec,
        out_shape=jax.ShapeDtypeStruct((n_blocks * MOE_TILE * XSLABS, LANES), jnp.uint32),
        compiler_params=pltpu.CompilerParams(
            dimension_semantics=("arbitrary",), vmem_limit_bytes=VMEM_LIMIT),
        name="moe_experts",
    )(block_e, block_seg, next_e, block_valid, n_used, xs, wgu, bgu, wdn, bdn)


def _combine_kernel(tab_cur, tab_nxt, ys_hbm, h_ref, tg_ref, tic_ref, rankc_ref, pp_ref, ps_ref,
                    gple_ref, wpg_ref, wpp_ref, gfin_ref, yp_ref, ys_ref, stage, sems, *, n_prompt_tiles):
    i = pl.program_id(0)
    nt = pl.num_programs(0)
    n = COMBINE_TILE * TOP_K
    slot_rows = n * XSLABS

    def gather(tab_ref, slot):
        slot_base = slot * slot_rows
        for bit in range(RUN_BITS - 1, -1, -1):
            size = (1 << bit) * XSLABS

            def issue(p, carry, bit=bit, size=size):
                dst0 = pl.multiple_of(tab_ref[0, 0, RUN_SRC + bit * N_EXPERTS + p] * XSLABS, XSLABS)
                src0 = pl.multiple_of(tab_ref[0, 0, RUN_DST + bit * N_EXPERTS + p] * XSLABS, XSLABS)
                pltpu.make_async_copy(ys_hbm.at[pl.ds(src0, size)], stage.at[pl.ds(slot_base + dst0, size)],
                                      sems.at[slot]).start(priority=bit % 2)
                return carry

            lax.fori_loop(0, tab_ref[0, 0, RUN_COUNT + bit], issue, 0)

    @pl.when(i == 0)
    def _():
        gather(tab_cur, 0)

    @pl.when(i + 1 < nt)
    def _():
        gather(tab_nxt, (i + 1) % 2)

    slot = i % 2
    base = pl.multiple_of(slot * slot_rows, slot_rows)
    pltpu.make_async_copy(ys_hbm.at[pl.ds(0, slot_rows)], stage.at[pl.ds(base, slot_rows)],
                          sems.at[slot]).wait()

    tic = tic_ref[...]
    shift = jnp.zeros(tic.shape, i32)
    for e in range(N_EXPERTS):
        shift = jnp.where(tic == e, tab_cur[0, 0, RUN_SHIFT + e], shift)
    staged = rankc_ref[...] + shift
    tg = tg_ref[...]
    staged_k = [jnp.broadcast_to(staged[:, k:k + 1], (COMBINE_TILE, LANES)) for k in range(TOP_K)]
    gate_k = [jnp.broadcast_to(tg[:, k:k + 1], (COMBINE_TILE, LANES)) for k in range(TOP_K)]
    lane = lax.broadcasted_iota(i32, (COMBINE_TILE, LANES), 1)
    groups = []
    for c in range(n // LANES):
        g = jnp.zeros((COMBINE_TILE, LANES), f32)
        for k in range(TOP_K):
            g = jnp.where(lane + c * LANES == staged_k[k], gate_k[k], g)
        groups.append(g)
    gates = jnp.concatenate(groups, axis=1)
    gates_hi = gates.astype(bf16)
    gates_lo = (gates - gates_hi.astype(f32)).astype(bf16)
    rows = _unpack_rows(stage, base, n)
    h = h_ref[...] + _dot(gates_hi, rows) + _dot(gates_lo, rows)
    gate = _sigmoid(_dot(_rms(h, gple_ref[...]).astype(bf16), wpg_ref[...]))
    p = _select_rows(i < n_prompt_tiles, pp_ref, ps_ref).astype(bf16)
    h = h + gate * _dot(p, wpp_ref[...])
    y = _rms(h, gfin_ref[...])

    @pl.when(i < n_prompt_tiles)
    def _():
        yp_ref[...] = y

    @pl.when(i >= n_prompt_tiles)
    def _():
        ys_ref[...] = y


def _combine(tile_table, ys, h1, tg, tic, rankc, pp, ps, gple, wpg, wpp, gfin):
    t = h1.shape[0]
    nt = t // COMBINE_TILE
    npt = pp.shape[0] // COMBINE_TILE
    n = COMBINE_TILE * TOP_K
    row = lambda w: pl.BlockSpec((COMBINE_TILE, w), lambda i: (i, 0))
    full = lambda shape: pl.BlockSpec(shape, lambda i: (0,) * len(shape))
    return pl.pallas_call(
        functools.partial(_combine_kernel, n_prompt_tiles=npt),
        grid=(nt,),
        in_specs=[
            pl.BlockSpec((1, 1, RUN_TABLE), lambda i: (i, 0, 0), memory_space=pltpu.SMEM),
            pl.BlockSpec((1, 1, RUN_TABLE), lambda i: (jnp.minimum(i + 1, nt - 1), 0, 0),
                         memory_space=pltpu.SMEM),
            pl.BlockSpec(memory_space=pl.ANY),
            row(D_MODEL), row(LANES), row(LANES), row(LANES),
            pl.BlockSpec((COMBINE_TILE, PLE_DIM), lambda i: (jnp.minimum(i, npt - 1), 0)),
            pl.BlockSpec((COMBINE_TILE, PLE_DIM), lambda i: (jnp.maximum(i - npt, 0), 0)),
            full((1, D_MODEL)), full((D_MODEL, D_MODEL)), full((PLE_DIM, D_MODEL)),
            full((1, D_MODEL)),
        ],
        out_specs=[
            pl.BlockSpec((COMBINE_TILE, D_MODEL), lambda i: (jnp.minimum(i, npt - 1), 0)),
            pl.BlockSpec((COMBINE_TILE, D_MODEL), lambda i: (jnp.maximum(i - npt, 0), 0)),
        ],
        out_shape=[
            jax.ShapeDtypeStruct((pp.shape[0], D_MODEL), f32),
            jax.ShapeDtypeStruct((ps.shape[0], D_MODEL), f32),
        ],
        scratch_shapes=[
            pltpu.VMEM((2 * n * XSLABS, LANES), jnp.uint32),
            pltpu.SemaphoreType.DMA((2,)),
        ],
        compiler_params=pltpu.CompilerParams(
            dimension_semantics=("arbitrary",), vmem_limit_bytes=VMEM_LIMIT),
        name="moe_combine_ple",
    )(tile_table, tile_table, ys, h1, tg, tic, rankc, pp, ps, gple, wpg, wpp, gfin)


def _routing_tables(top_i, counts, part_counts, n_blocks):
    padded = (counts + MOE_TILE - 1) // MOE_TILE * MOE_TILE
    pend = jnp.cumsum(padded)
    pstart = pend - padded
    block_rows = jnp.arange(n_blocks, dtype=i32) * MOE_TILE
    block_e = jnp.minimum(
        jnp.sum((block_rows[:, None] >= pend[None, :]).astype(i32), axis=1), N_EXPERTS - 1)
    experts = jnp.arange(N_EXPERTS, dtype=i32)
    of_block = lambda table: jnp.sum(jnp.where(block_e[:, None] == experts[None, :], table[None, :], 0), axis=1)
    block_valid = jnp.clip(of_block(pstart + counts) - block_rows, 0, MOE_TILE)
    n_used = pend[-1:] // MOE_TILE
    owns = padded > 0
    later = owns[None, :] & (experts[None, :] > experts[:, None])
    next_of_e = jnp.where(jnp.any(later, axis=1), jnp.argmax(later, axis=1), -1)
    block_seg = of_block(jnp.cumsum(owns.astype(i32)) - 1)
    next_e = of_block(next_of_e.astype(i32))

    nt = top_i.shape[1] // DISPATCH_TILE
    tile_n = part_counts[:, :, :TOKEN_TILE // DISPATCH_TILE].transpose(0, 2, 1).reshape(nt, N_EXPERTS)
    seen = jnp.cumsum(tile_n, axis=0) - tile_n
    tile_src = jnp.cumsum(tile_n, axis=1) - tile_n
    tile_dst = pstart[None, :] + seen
    bits = jnp.arange(RUN_BITS, dtype=i32)[None, :, None]
    n3 = tile_n[:, None, :]
    has_piece = ((n3 >> bits) & 1) == 1
    covered = (n3 >> (bits + 1)) << (bits + 1)
    earlier = (experts[:, None] < experts[None, :])[None, None]
    place = jnp.sum((has_piece[..., None] & earlier).astype(i32), axis=2)
    listed = has_piece[..., None] & (place[..., None] == experts[None, None, None, :])
    compact = lambda v: jnp.sum(jnp.where(listed, v[..., None], 0), axis=2)
    piece_src = compact(tile_src[:, None, :] + covered)
    piece_dst = compact(tile_dst[:, None, :] + covered)
    piece_cnt = jnp.sum(has_piece.astype(i32), axis=2)
    tile_table = jnp.concatenate(
        [tile_src - seen, jnp.pad(piece_cnt, ((0, 0), (0, N_EXPERTS - RUN_BITS))),
         piece_src.reshape(nt, -1), piece_dst.reshape(nt, -1)], axis=1)
    tables = (block_e, block_seg, next_e, block_valid, n_used)
    return (tile_table.reshape(nt, 1, RUN_TABLE).astype(i32),) + tuple(
        t.astype(i32) for t in tables)


def kernel(x_prompt, x_sample, p_prompt, p_sample, state_gla, state_hgrn, g_mix, w_in, w_gk2, b_gk2,
           gn_gla, lb_logits, gn_hgrn, w_out, g_ffn, w_router, b_router, w_gu, b_gu, w_dn, b_dn,
           g_ple, w_ple_gate, w_ple_proj, g_final):
    batch, seq, _ = x_prompt.shape
    dec_batch, dec_seq, _ = x_sample.shape
    tp, ts = batch * seq, dec_batch * dec_seq
    xp = x_prompt.reshape(tp, D_MODEL)
    xs = x_sample.reshape(ts, D_MODEL)

    wgk = jnp.concatenate(
        [w_gk2[0], jnp.zeros((R1_PAD - GLA_LOWRANK, GLA_KDIM), f32)], axis=0).astype(bf16)
    wr = w_router[0].T.astype(bf16)
    br = b_router[0].reshape(N_EXPERTS, 1)

    z = _in_proj(xp, xs, g_mix[0].reshape(1, D_MODEL), w_in[0])

    mixer_params = (wgk, b_gk2[0].reshape(1, GLA_KDIM), gn_gla[0].reshape(1, GLA_DV),
                    lb_logits[0:2], gn_hgrn[0].reshape(1, HGRN_DV))
    o_p, new_gla_p, new_hgrn_p = _mixer_prompt(
        z, *mixer_params, batch=batch, seq=seq, rows_per_step=512, block_rows=256)
    z_s = jnp.pad(z[tp:].reshape(dec_batch, dec_seq, Z_W), ((0, 0), (0, SAMPLE_PAD_T - dec_seq), (0, 0)))
    o_s, new_gla_s, new_hgrn_s = _mixer_sample(
        z_s.reshape(dec_batch * SAMPLE_PAD_T, Z_W), state_gla, state_hgrn, *mixer_params,
        steps=dec_seq, batch_block=8)
    o_s = o_s.reshape(dec_batch, SAMPLE_PAD_T, D_MODEL)[:, :dec_seq].reshape(ts, D_MODEL).astype(bf16)

    h1, xn, top_i, top_g, rank, top_ic, rank_c, counts, part_counts = _out_proj(
        xp, xs, o_p, o_s, w_out[0].astype(bf16), g_ffn[0].reshape(1, D_MODEL), wr, br)

    t = tp + ts
    n_blocks = -(-(t * TOP_K) // MOE_TILE) + N_EXPERTS
    tile_table, block_e, block_seg, next_e, block_valid, n_used = _routing_tables(
        top_i, counts[:, 0], part_counts, n_blocks)
    x_sorted = _dispatch(block_valid, tile_table, top_i, rank, xn)
    y_sorted = _experts(block_e, block_seg, next_e, block_valid, n_used, x_sorted,
                        w_gu[0], b_gu[0].reshape(N_EXPERTS, 1, 2 * D_FF),
                        w_dn[0], b_dn[0].reshape(N_EXPERTS, 1, D_MODEL))

    y_p, y_s = _combine(tile_table, y_sorted, h1, top_g, top_ic, rank_c, p_prompt[0].reshape(tp, PLE_DIM),
                        p_sample[0].reshape(ts, PLE_DIM), g_ple[0].reshape(1, D_MODEL),
                        w_ple_gate[0].astype(bf16), w_ple_proj[0].astype(bf16), g_final.reshape(1, D_MODEL))

    return (y_p.reshape(batch, seq, D_MODEL), y_s.reshape(dec_batch, dec_seq, D_MODEL),
            new_gla_p, new_hgrn_p, new_gla_s, new_hgrn_s)
```

```python
import functools

import jax
import jax.numpy as jnp
from jax import lax
from jax.experimental import pallas as pl
from jax.experimental.pallas import tpu as pltpu

f32 = jnp.float32
bf16 = jnp.bfloat16
i32 = jnp.int32

D_MODEL = 1024
GLA_HEADS = 4
GLA_DK = 64
GLA_DV = 128
GLA_KDIM = GLA_HEADS * GLA_DK
GLA_WIDTH = GLA_HEADS * GLA_DV
GLA_LOWRANK = 16
GLA_NORMALIZER = 16.0
HGRN_HEADS = 4
HGRN_DK = 128
HGRN_DV = 128
HGRN_FDIM = HGRN_HEADS * HGRN_DK
HGRN_WIDTH = HGRN_HEADS * HGRN_DV
CHUNK = 64
N_EXPERTS = 32
TOP_K = 4
D_FF = 1024
SWIGLU_LIMIT = 7.0
SWIGLU_ALPHA = 1.702
PLE_DIM = 256
EPS = 1e-6

LANES = 128
SAMPLE_PAD_T = 8
XSLABS = D_MODEL // (2 * LANES)
ROUTE_ROWS = 8
R1_PAD = LANES
Z_Q1 = 0
Z_K1 = Z_Q1 + GLA_KDIM
Z_V1 = Z_K1 + GLA_KDIM
Z_G1 = Z_V1 + GLA_WIDTH
Z_R1 = Z_G1 + GLA_WIDTH
Z_Q2 = Z_R1 + R1_PAD
Z_F2 = Z_Q2 + HGRN_FDIM
Z_I2 = Z_F2 + HGRN_FDIM
Z_G2 = Z_I2 + HGRN_WIDTH
Z_W = Z_G2 + HGRN_WIDTH

TOKEN_TILE = 512
MOE_TILE = 512
DISPATCH_TILE = 256
COMBINE_TILE = DISPATCH_TILE
RUN_BITS = DISPATCH_TILE.bit_length()
RUN_SHIFT = 0
RUN_COUNT = RUN_SHIFT + N_EXPERTS
RUN_SRC = RUN_COUNT + N_EXPERTS
RUN_DST = RUN_SRC + RUN_BITS * N_EXPERTS
RUN_TABLE = RUN_DST + RUN_BITS * N_EXPERTS
VMEM_LIMIT = 56 * 1024 * 1024


def _rms(x, g):
    return x * lax.rsqrt(jnp.mean(x * x, axis=-1, keepdims=True) + EPS) * g


def _sigmoid(x):
    return 0.5 * jnp.tanh(0.5 * x) + 0.5


def _dot(a, b):
    return jnp.dot(a, b, preferred_element_type=f32)


def _dot_nt(a, b):
    return lax.dot_general(a, b, (((1,), (1,)), ((), ())), preferred_element_type=f32)


def _dot_tn(a, b):
    return lax.dot_general(a, b, (((0,), (0,)), ((), ())), preferred_element_type=f32)


def _cumsum_rows(tri, x):
    hi = x.astype(bf16)
    r1 = x - hi.astype(f32)
    mid = r1.astype(bf16)
    lo = (r1 - mid.astype(f32)).astype(bf16)
    return _dot(tri, hi) + _dot(tri, mid) + _dot(tri, lo)


def _select_rows(is_prompt, a_ref, b_ref):
    return jnp.where(is_prompt, a_ref[...], b_ref[...])


def _inproj_kernel(xp_ref, xs_ref, g_ref, w_hbm, z_ref, w_f32, w_bf, sem,
                   *, n_prompt_tiles, col_chunk, cast_rows):
    @pl.when(pl.program_id(0) == 0)
    def _():
        copy = pltpu.make_async_copy(w_hbm, w_f32, sem)
        copy.start()
        copy.wait()
        r1_end = Z_R1 + GLA_LOWRANK

        def cast(j, carry):
            rows = pl.ds(pl.multiple_of(j * cast_rows, cast_rows), cast_rows)
            w_bf[rows, 0:Z_R1] = w_f32[rows, 0:Z_R1].astype(bf16)
            w_bf[rows, Z_R1:Z_Q2] = jnp.concatenate(
                [w_f32[rows, Z_R1:r1_end], jnp.zeros((cast_rows, R1_PAD - GLA_LOWRANK), f32)],
                axis=1).astype(bf16)
            w_bf[rows, Z_Q2:Z_W] = w_f32[rows, r1_end:].astype(bf16)
            return carry
        lax.fori_loop(0, D_MODEL // cast_rows, cast, 0)

    is_prompt = pl.program_id(0) < n_prompt_tiles
    x = _select_rows(is_prompt, xp_ref, xs_ref)
    a = _rms(x, g_ref[...]).astype(bf16)
    for c0 in range(0, Z_W, col_chunk):
        c1 = min(c0 + col_chunk, Z_W)
        z_ref[:, c0:c1] = _dot(a, w_bf[:, c0:c1]).astype(bf16)


def _in_proj(xp, xs, g, w):
    tp, ts = xp.shape[0], xs.shape[0]
    npt = tp // TOKEN_TILE
    nt = npt + ts // TOKEN_TILE
    return pl.pallas_call(
        functools.partial(_inproj_kernel, n_prompt_tiles=npt, col_chunk=512, cast_rows=64),
        grid=(nt,),
        in_specs=[
            pl.BlockSpec((TOKEN_TILE, D_MODEL), lambda i: (jnp.minimum(i, npt - 1), 0)),
            pl.BlockSpec((TOKEN_TILE, D_MODEL), lambda i: (jnp.maximum(i - npt, 0), 0)),
            pl.BlockSpec((1, D_MODEL), lambda i: (0, 0)),
            pl.BlockSpec(memory_space=pl.ANY),
        ],
        out_specs=pl.BlockSpec((TOKEN_TILE, Z_W), lambda i: (i, 0)),
        out_shape=jax.ShapeDtypeStruct((tp + ts, Z_W), bf16),
        scratch_shapes=[
            pltpu.VMEM(w.shape, f32),
            pltpu.VMEM((D_MODEL, Z_W), bf16),
            pltpu.SemaphoreType.DMA(()),
        ],
        compiler_params=pltpu.CompilerParams(
            dimension_semantics=("arbitrary",), vmem_limit_bytes=VMEM_LIMIT),
        name="in_proj",
    )(xp, xs, g, w)


def _block_consts(r, c):
    shift = c.bit_length() - 1
    row = lax.broadcasted_iota(i32, (r, r), 0)
    col = lax.broadcasted_iota(i32, (r, r), 1)
    causal = (lax.shift_right_logical(row, shift) == lax.shift_right_logical(col, shift)) & (row >= col)
    return causal.astype(bf16), causal


def _subchunk_row(x, c, idx):
    parts = [jnp.broadcast_to(x[j * c + idx:j * c + idx + 1, :], (c, x.shape[1]))
             for j in range(x.shape[0] // c)]
    return jnp.concatenate(parts, axis=0)


def _gated_block(q, k, v, log_a, states, *, c, scale, ref_row, tri, causal, heads_per_group, scan):
    r = q.shape[0]
    nsub = r // c
    groups = q.shape[1] // LANES
    b = _cumsum_rows(tri, log_a)
    b_ref = _subchunk_row(b, c, ref_row)
    b_last = _subchunk_row(b, c, c - 1)
    qs = q * scale
    q_r = qs * jnp.exp(b - b_ref)
    k_r32 = k * jnp.exp(b_ref - b)
    k_r = k_r32.astype(bf16)
    k_end = k_r32 * jnp.exp(b_last - b_ref)
    q_b = q_r * jnp.exp(b_ref)
    decay = jnp.exp(b_last)
    vb = v.astype(bf16)
    lane = lax.broadcasted_iota(i32, (r, LANES), 1)
    width = LANES // heads_per_group
    outs, new_states = [], []
    for g in range(groups):
        gl = slice(g * LANES, (g + 1) * LANES)
        masks = [None] if heads_per_group == 1 else [
            (lane >= j * width) & (lane < (j + 1) * width) for j in range(heads_per_group)]
        pick = lambda x, m: x if m is None else jnp.where(m, x, 0.0)
        q_r_h = [pick(q_r[:, gl], m).astype(bf16) for m in masks]
        q_b_h = [pick(q_b[:, gl], m).astype(bf16) for m in masks]
        k_end_h = [pick(k_end[:, gl], m).astype(bf16) for m in masks]
        v_h = [vb[:, (g * heads_per_group + j) * LANES:(g * heads_per_group + j + 1) * LANES]
               for j in range(heads_per_group)]
        o_intra = []
        for j in range(heads_per_group):
            att = _dot_nt(q_r_h[j], k_r[:, gl])
            att = jnp.where(causal, att, 0.0).astype(bf16)
            o_intra.append(_dot(att, v_h[j]))
        upd = []
        for s in range(nsub):
            rs = slice(s * c, (s + 1) * c)
            u = _dot_tn(v_h[0][rs], k_end_h[0][rs])
            for j in range(1, heads_per_group):
                u = u + _dot_tn(v_h[j][rs], k_end_h[j][rs])
            upd.append(u)
        o_inter = [[] for _ in range(heads_per_group)]
        st = states[g] if scan else None
        group_states = []
        for s in range(nsub):
            rs = slice(s * c, (s + 1) * c)
            if not scan:
                st = states[s][g]
            st_b = st.astype(bf16)
            for j in range(heads_per_group):
                o_inter[j].append(_dot_nt(q_b_h[j][rs], st_b))
            st = st * decay[s * c:s * c + 1, gl] + upd[s]
            if not scan:
                group_states.append(st)
        for j in range(heads_per_group):
            outs.append(o_intra[j] + jnp.concatenate(o_inter[j], axis=0))
        new_states.append(st if scan else group_states)
    if not scan:
        new_states = [[new_states[g][s] for g in range(groups)] for s in range(nsub)]
    return outs, new_states


def _head_norm(o, g):
    return o * lax.rsqrt(jnp.mean(o * o, axis=-1, keepdims=True) + EPS) * g


def _mixer_block(zb, st_gla, st_hgrn, wgk, bgk, gng, lb, gnh, *, c, ref_row, tri, causal, valid, scan):
    z = zb.astype(f32)
    gk = _dot(zb[:, Z_R1:Z_R1 + R1_PAD], wgk) + bgk
    log_a = (jnp.minimum(gk, 0.0) - jnp.log(1.0 + jnp.exp(-jnp.abs(gk)))) / GLA_NORMALIZER
    k1 = z[:, Z_K1:Z_K1 + GLA_KDIM]
    if valid is not None:
        log_a = jnp.where(valid, log_a, 0.0)
        k1 = jnp.where(valid, k1, 0.0)
    o1, st_gla = _gated_block(
        z[:, Z_Q1:Z_Q1 + GLA_KDIM], k1, zb[:, Z_V1:Z_V1 + GLA_WIDTH], log_a, st_gla,
        c=c, scale=GLA_DK ** -0.5, ref_row=ref_row, tri=tri, causal=causal,
        heads_per_group=LANES // GLA_DK, scan=scan)
    forget = lb + (1.0 - lb) * _sigmoid(z[:, Z_F2:Z_F2 + HGRN_FDIM])
    k2 = 1.0 - forget
    log_f = jnp.log(forget)
    if valid is not None:
        log_f = jnp.where(valid, log_f, 0.0)
        k2 = jnp.where(valid, k2, 0.0)
    q2 = z[:, Z_Q2:Z_Q2 + HGRN_FDIM]
    i2 = z[:, Z_I2:Z_I2 + HGRN_WIDTH]
    o2, st_hgrn = _gated_block(
        q2 * _sigmoid(q2), k2, i2 * _sigmoid(i2), log_f, st_hgrn,
        c=c, scale=1.0, ref_row=ref_row, tri=tri, causal=causal, heads_per_group=1, scan=scan)
    cols = []
    for h in range(GLA_HEADS):
        g1 = z[:, Z_G1 + h * GLA_DV:Z_G1 + (h + 1) * GLA_DV]
        cols.append(_head_norm(o1[h], gng) * (g1 * _sigmoid(g1)))
    for h in range(HGRN_HEADS):
        g2 = z[:, Z_G2 + h * HGRN_DV:Z_G2 + (h + 1) * HGRN_DV]
        cols.append(_head_norm(o2[h], gnh) * _sigmoid(g2))
    return jnp.concatenate(cols, axis=1), st_gla, st_hgrn


def _lower_bound(lbl_ref):
    l = lbl_ref[...]
    m = jnp.max(l, axis=0, keepdims=True)
    e = jnp.exp(l - m)
    return e[0:1, :] / jnp.sum(e, axis=0, keepdims=True)


GLA_GROUPS = GLA_KDIM // LANES


def _gla_state_t(s_ref_heads):
    return s_ref_heads.reshape(LANES, GLA_DV).T


def _gla_state_from_t(st):
    return st.T.reshape(LANES // GLA_DK, GLA_DK, GLA_DV)


def _mixer_prompt_kernel(z_ref, wgk_ref, bgk_ref, gng_ref, lbl_ref, gnh_ref,
                         o_ref, sg_out, sh_out, sg_t, sh_t, *, block_rows):
    j = pl.program_id(1)

    @pl.when(j == 0)
    def _():
        sg_t[...] = jnp.zeros_like(sg_t)
        sh_t[...] = jnp.zeros_like(sh_t)

    tri, causal = _block_consts(block_rows, CHUNK)
    lb = _lower_bound(lbl_ref)
    wgk, bgk, gng, gnh = wgk_ref[...], bgk_ref[...], gng_ref[...], gnh_ref[...]

    def body(blk, carry):
        r0 = pl.multiple_of(blk * block_rows, block_rows)
        o, st_g, st_h = _mixer_block(
            z_ref[pl.ds(r0, block_rows), :], [sg_t[g] for g in range(GLA_GROUPS)],
            [sh_t[h] for h in range(HGRN_HEADS)], wgk, bgk, gng, lb, gnh,
            c=CHUNK, ref_row=CHUNK // 2, tri=tri, causal=causal, valid=None, scan=True)
        o_ref[pl.ds(r0, block_rows), :] = o.astype(bf16)
        for g in range(GLA_GROUPS):
            sg_t[g] = st_g[g]
        for h in range(HGRN_HEADS):
            sh_t[h] = st_h[h]
        return carry

    lax.fori_loop(0, z_ref.shape[0] // block_rows, body, 0)

    @pl.when(j == pl.num_programs(1) - 1)
    def _():
        per = LANES // GLA_DK
        for g in range(GLA_GROUPS):
            sg_out[0, 0, g * per:(g + 1) * per] = _gla_state_from_t(sg_t[g])
        for h in range(HGRN_HEADS):
            sh_out[0, 0, h] = sh_t[h].T


def _mixer_prompt(z, wgk, bgk, gng, lbl, gnh, *, batch, seq, rows_per_step, block_rows):
    nj = seq // rows_per_step
    full = lambda shape: pl.BlockSpec(shape, lambda b, j: (0,) * len(shape))
    return pl.pallas_call(
        functools.partial(_mixer_prompt_kernel, block_rows=block_rows),
        grid=(batch, nj),
        in_specs=[
            pl.BlockSpec((rows_per_step, Z_W), lambda b, j: (b * nj + j, 0)),
            full((R1_PAD, GLA_KDIM)), full((1, GLA_KDIM)), full((1, GLA_DV)),
            full((2, HGRN_FDIM)), full((1, HGRN_DV)),
        ],
        out_specs=[
            pl.BlockSpec((rows_per_step, D_MODEL), lambda b, j: (b * nj + j, 0)),
            pl.BlockSpec((1, 1, GLA_HEADS, GLA_DK, GLA_DV), lambda b, j: (0, b, 0, 0, 0)),
            pl.BlockSpec((1, 1, HGRN_HEADS, HGRN_DK, HGRN_DV), lambda b, j: (0, b, 0, 0, 0)),
        ],
        out_shape=[
            jax.ShapeDtypeStruct((batch * seq, D_MODEL), bf16),
            jax.ShapeDtypeStruct((1, batch, GLA_HEADS, GLA_DK, GLA_DV), f32),
            jax.ShapeDtypeStruct((1, batch, HGRN_HEADS, HGRN_DK, HGRN_DV), f32),
        ],
        scratch_shapes=[
            pltpu.VMEM((GLA_GROUPS, GLA_DV, LANES), f32),
            pltpu.VMEM((HGRN_HEADS, HGRN_DV, HGRN_DK), f32),
        ],
        compiler_params=pltpu.CompilerParams(
            dimension_semantics=("arbitrary", "arbitrary"), vmem_limit_bytes=VMEM_LIMIT),
        name="mixer_prompt",
    )(z, wgk, bgk, gng, lbl, gnh)


def _mixer_sample_kernel(z_ref, sg_in, sh_in, wgk_ref, bgk_ref, gng_ref, lbl_ref, gnh_ref,
                         o_ref, sg_out, sh_out, *, batch_block, steps):
    rows = batch_block * SAMPLE_PAD_T
    tri, causal = _block_consts(rows, SAMPLE_PAD_T)
    step = lax.broadcasted_iota(i32, (rows, 1), 0) & (SAMPLE_PAD_T - 1)
    valid = step < steps
    lb = _lower_bound(lbl_ref)
    per = LANES // GLA_DK
    st_g = [[_gla_state_t(sg_in[0, s, g * per:(g + 1) * per]) for g in range(GLA_GROUPS)]
            for s in range(batch_block)]
    st_h = [[sh_in[0, s, h].T for h in range(HGRN_HEADS)] for s in range(batch_block)]
    o, st_g, st_h = _mixer_block(
        z_ref[...], st_g, st_h, wgk_ref[...], bgk_ref[...], gng_ref[...], lb, gnh_ref[...],
        c=SAMPLE_PAD_T, ref_row=steps // 2, tri=tri, causal=causal, valid=valid, scan=False)
    o_ref[...] = o
    for s in range(batch_block):
        for g in range(GLA_GROUPS):
            sg_out[0, s, g * per:(g + 1) * per] = _gla_state_from_t(st_g[s][g])
        for h in range(HGRN_HEADS):
            sh_out[0, s, h] = st_h[s][h].T


def _mixer_sample(z2, state_gla, state_hgrn, wgk, bgk, gng, lbl, gnh, *, steps, batch_block):
    batch = z2.shape[0] // SAMPLE_PAD_T
    rows = batch_block * SAMPLE_PAD_T
    full = lambda shape: pl.BlockSpec(shape, lambda i: (0,) * len(shape))
    sg_spec = pl.BlockSpec((1, batch_block, GLA_HEADS, GLA_DK, GLA_DV), lambda i: (0, i, 0, 0, 0))
    sh_spec = pl.BlockSpec((1, batch_block, HGRN_HEADS, HGRN_DK, HGRN_DV), lambda i: (0, i, 0, 0, 0))
    return pl.pallas_call(
        functools.partial(_mixer_sample_kernel, batch_block=batch_block, steps=steps),
        grid=(batch // batch_block,),
        in_specs=[
            pl.BlockSpec((rows, Z_W), lambda i: (i, 0)),
            sg_spec, sh_spec,
            full((R1_PAD, GLA_KDIM)), full((1, GLA_KDIM)), full((1, GLA_DV)),
            full((2, HGRN_FDIM)), full((1, HGRN_DV)),
        ],
        out_specs=[pl.BlockSpec((rows, D_MODEL), lambda i: (i, 0)), sg_spec, sh_spec],
        out_shape=[
            jax.ShapeDtypeStruct((batch * SAMPLE_PAD_T, D_MODEL), f32),
            jax.ShapeDtypeStruct(state_gla.shape, f32),
            jax.ShapeDtypeStruct(state_hgrn.shape, f32),
        ],
        compiler_params=pltpu.CompilerParams(
            dimension_semantics=("arbitrary",), vmem_limit_bytes=VMEM_LIMIT),
        name="mixer_sample",
    )(z2, state_gla, state_hgrn, wgk, bgk, gng, lbl, gnh)


HIGH_HALF = 0xFFFF0000


def _pack_rows(x):
    bits = lax.bitcast_convert_type(x, jnp.uint32)
    return [(bits[:, (2 * s + 1) * LANES:(2 * s + 2) * LANES] & jnp.uint32(HIGH_HALF))
            | (bits[:, 2 * s * LANES:(2 * s + 1) * LANES] >> 16) for s in range(x.shape[1] // (2 * LANES))]


def _unpack_rows(ref, start, rows):
    cols = []
    for s in range(XSLABS):
        w = ref[pl.ds(start + s, rows, stride=XSLABS), :]
        cols.append(lax.bitcast_convert_type(w << 16, f32).astype(bf16))
        cols.append(lax.bitcast_convert_type(w & jnp.uint32(HIGH_HALF), f32).astype(bf16))
    return jnp.concatenate(cols, axis=1)


def _outproj_kernel(xp_ref, xs_ref, op_ref, os_ref, wo_ref, g_ref, wr_ref, br_ref,
                    h_ref, xn_ref, ti_ref, tg_ref, rank_ref, tic_ref, rankc_ref, cnt_ref, part_ref, run_cnt,
                    *, n_prompt_tiles):
    i = pl.program_id(0)

    @pl.when(i == 0)
    def _():
        run_cnt[...] = jnp.zeros_like(run_cnt)

    x = _select_rows(i < n_prompt_tiles, xp_ref, xs_ref)
    o = _select_rows(i < n_prompt_tiles, op_ref, os_ref)
    h = x + _dot(o, wo_ref[...])
    h_ref[...] = h
    xn = _rms(h, g_ref[...])
    xn_ref[...] = xn.astype(bf16)
    l = _dot_nt(wr_ref[...], xn.astype(bf16)) + br_ref[...]
    expert = lax.broadcasted_iota(i32, l.shape, 0)
    neg = jnp.float32(-jnp.inf)
    vals, idxs = [], []
    for _ in range(TOP_K):
        m = jnp.max(l, axis=0, keepdims=True)
        idx = jnp.min(jnp.where(l == m, expert, N_EXPERTS), axis=0, keepdims=True)
        vals.append(m)
        idxs.append(idx)
        l = jnp.where(expert == idx, neg, l)
    es = [jnp.exp(v - vals[0]) for v in vals]
    denom = es[0] + es[1] + es[2] + es[3]

    onehot = jnp.zeros(l.shape, f32)
    for k in range(TOP_K):
        onehot = onehot + (expert == idxs[k]).astype(f32)
    n = l.shape[1]
    earlier = (lax.broadcasted_iota(i32, (n, n), 0) < lax.broadcasted_iota(i32, (n, n), 1)).astype(bf16)
    ahead = _dot(onehot.astype(bf16), earlier) + run_cnt[:, 0:1]
    run_cnt[...] = run_cnt[...] + jnp.sum(onehot, axis=1, keepdims=True)
    cnt_ref[...] = run_cnt[...].astype(i32)
    lane = lax.broadcasted_iota(i32, (N_EXPERTS, LANES), 1)
    parts = jnp.zeros((N_EXPERTS, LANES), f32)
    for p in range(n // DISPATCH_TILE):
        c = jnp.sum(onehot[:, p * DISPATCH_TILE:(p + 1) * DISPATCH_TILE], axis=1, keepdims=True)
        parts = jnp.where(lane == p, c, parts)
    part_ref[0] = parts.astype(i32)

    ranks = [jnp.sum(jnp.where(expert == idxs[k], ahead, 0.0), axis=0, keepdims=True).astype(i32)
             for k in range(TOP_K)]
    pad_i = jnp.zeros((ROUTE_ROWS - TOP_K, n), i32)
    ti_ref[...] = jnp.concatenate(idxs + [pad_i], axis=0)
    rank_ref[...] = jnp.concatenate(ranks + [pad_i], axis=0)
    gate_rows = jnp.concatenate([e / denom for e in es] + [jnp.zeros((LANES - TOP_K, n), f32)], axis=0)
    tg_ref[...] = gate_rows.T
    pad_wide = jnp.zeros((LANES - TOP_K, n), i32)
    tic_ref[...] = jnp.concatenate(idxs + [pad_wide], axis=0).T
    rankc_ref[...] = jnp.concatenate(ranks + [pad_wide], axis=0).T


def _out_proj(xp, xs, o_p, o_s, wo, g, wr, br):
    tp, ts = xp.shape[0], xs.shape[0]
    npt = tp // TOKEN_TILE
    nt = npt + ts // TOKEN_TILE
    t = tp + ts
    row = lambda w: pl.BlockSpec((TOKEN_TILE, w), lambda i: (i, 0))
    route = pl.BlockSpec((ROUTE_ROWS, TOKEN_TILE), lambda i: (0, i))
    full = lambda shape: pl.BlockSpec(shape, lambda i: (0,) * len(shape))
    prompt_rows = pl.BlockSpec((TOKEN_TILE, D_MODEL), lambda i: (jnp.minimum(i, npt - 1), 0))
    sample_rows = pl.BlockSpec((TOKEN_TILE, D_MODEL), lambda i: (jnp.maximum(i - npt, 0), 0))
    return pl.pallas_call(
        functools.partial(_outproj_kernel, n_prompt_tiles=npt),
        grid=(nt,),
        in_specs=[
            prompt_rows, sample_rows, prompt_rows, sample_rows,
            full((D_MODEL, D_MODEL)), full((1, D_MODEL)),
            full((N_EXPERTS, D_MODEL)), full((N_EXPERTS, 1)),
        ],
        out_specs=[row(D_MODEL), row(D_MODEL),
                   route, row(LANES), route, row(LANES), row(LANES), full((N_EXPERTS, LANES)),
                   pl.BlockSpec((1, N_EXPERTS, LANES), lambda i: (i, 0, 0))],
        out_shape=[
            jax.ShapeDtypeStruct((t, D_MODEL), f32),
            jax.ShapeDtypeStruct((t, D_MODEL), bf16),
            jax.ShapeDtypeStruct((ROUTE_ROWS, t), i32),
            jax.ShapeDtypeStruct((t, LANES), f32),
            jax.ShapeDtypeStruct((ROUTE_ROWS, t), i32),
            jax.ShapeDtypeStruct((t, LANES), i32),
            jax.ShapeDtypeStruct((t, LANES), i32),
            jax.ShapeDtypeStruct((N_EXPERTS, LANES), i32),
            jax.ShapeDtypeStruct((nt, N_EXPERTS, LANES), i32),
        ],
        scratch_shapes=[pltpu.VMEM((N_EXPERTS, LANES), f32)],
        compiler_params=pltpu.CompilerParams(
            dimension_semantics=("arbitrary",), vmem_limit_bytes=VMEM_LIMIT),
        name="out_proj_router",
    )(xp, xs, o_p, o_s, wo, g, wr, br)


def _dispatch_kernel(bv_ref, nz_ref, tab_ref, ti_ref, rank_ref, xn_ref, xs_hbm, zeros, stage, sems, zsem,
                     *, n_blocks):
    @pl.when(pl.program_id(0) == 0)
    def _():
        zeros[...] = jnp.zeros_like(zeros)

        def fill(b, carry):
            @pl.when(bv_ref[b] < MOE_TILE)
            def _():
                r0 = pl.multiple_of(b * (MOE_TILE * XSLABS), MOE_TILE * XSLABS)
                pltpu.make_async_copy(zeros, xs_hbm.at[pl.ds(r0, MOE_TILE * XSLABS)], zsem).start()
            return carry

        def drain(b, carry):
            pltpu.make_async_copy(zeros, xs_hbm.at[pl.ds(0, MOE_TILE * XSLABS)], zsem).wait()
            return carry

        lax.fori_loop(0, n_blocks, fill, 0)
        lax.fori_loop(0, nz_ref[0], drain, 0)

    i = pl.program_id(0)
    nt = pl.num_programs(0)
    n = DISPATCH_TILE * TOP_K
    slot_rows = n * XSLABS
    slot = i % 2
    base = pl.multiple_of(slot * slot_rows, slot_rows)

    def wait_rows(s):
        pltpu.make_async_copy(stage.at[pl.ds(0, slot_rows)], xs_hbm.at[pl.ds(0, slot_rows)], sems.at[s]).wait()

    ti = ti_ref[...]
    shift = jnp.zeros(ti.shape, i32)
    for e in range(N_EXPERTS):
        shift = jnp.where(ti == e, tab_ref[0, 0, RUN_SHIFT + e], shift)
    staged = rank_ref[...] + shift
    row = lax.broadcasted_iota(i32, (n, DISPATCH_TILE), 0)
    pick = row == staged[0:1, :]
    for k in range(1, TOP_K):
        pick = pick | (row == staged[k:k + 1, :])
    rows_sorted = _dot(pick.astype(bf16), xn_ref[...])
    for s, words in enumerate(_pack_rows(rows_sorted)):
        stage[pl.ds(base + s, n, stride=XSLABS), :] = words

    for bit in range(RUN_BITS - 1, -1, -1):
        size = (1 << bit) * XSLABS

        def issue(p, carry, bit=bit, size=size):
            src0 = pl.multiple_of(tab_ref[0, 0, RUN_SRC + bit * N_EXPERTS + p] * XSLABS, XSLABS)
            dst0 = pl.multiple_of(tab_ref[0, 0, RUN_DST + bit * N_EXPERTS + p] * XSLABS, XSLABS)
            pltpu.make_async_copy(stage.at[pl.ds(base + src0, size)], xs_hbm.at[pl.ds(dst0, size)],
                                  sems.at[slot]).start(priority=bit % 2)
            return carry

        lax.fori_loop(0, tab_ref[0, 0, RUN_COUNT + bit], issue, 0)

    @pl.when(i > 0)
    def _():
        wait_rows(1 - slot)

    @pl.when(i == nt - 1)
    def _():
        wait_rows(slot)


def _dispatch(block_valid, tile_table, top_i, rank, xn):
    t = xn.shape[0]
    nt = t // DISPATCH_TILE
    n_blocks = block_valid.shape[0]
    n_zero = jnp.sum((block_valid < MOE_TILE).astype(i32)).reshape(1)
    route = pl.BlockSpec((ROUTE_ROWS, DISPATCH_TILE), lambda i, bv, nz: (0, i))
    grid_spec = pltpu.PrefetchScalarGridSpec(
        num_scalar_prefetch=2,
        grid=(nt,),
        in_specs=[
            pl.BlockSpec((1, 1, RUN_TABLE), lambda i, bv, nz: (i, 0, 0), memory_space=pltpu.SMEM),
            route, route,
            pl.BlockSpec((DISPATCH_TILE, D_MODEL), lambda i, bv, nz: (i, 0)),
        ],
        out_specs=pl.BlockSpec(memory_space=pl.ANY),
        scratch_shapes=[
            pltpu.VMEM((MOE_TILE * XSLABS, LANES), jnp.uint32),
            pltpu.VMEM((2 * DISPATCH_TILE * TOP_K * XSLABS, LANES), jnp.uint32),
            pltpu.SemaphoreType.DMA((2,)),
            pltpu.SemaphoreType.DMA(()),
        ],
    )
    return pl.pallas_call(
        functools.partial(_dispatch_kernel, n_blocks=n_blocks),
        grid_spec=grid_spec,
        out_shape=jax.ShapeDtypeStruct((n_blocks * MOE_TILE * XSLABS, LANES), jnp.uint32),
        compiler_params=pltpu.CompilerParams(
            dimension_semantics=("arbitrary",), vmem_limit_bytes=VMEM_LIMIT),
        name="moe_dispatch",
    )(block_valid, n_zero, tile_table, top_i, rank, xn)


def _expert_kernel(be_ref, seg_ref, nxt_ref, bv_ref, nb_ref, x_ref, wgu_hbm, bgu_ref, wdn_hbm, bdn_ref, y_ref,
                   wgu_f32, wdn_f32, wgu_bf, wdn_bf, sems, *, ff_chunk, cast_rows):
    i = pl.program_id(0)

    @pl.when(i >= nb_ref[0])
    def _():
        y_ref[...] = jnp.zeros_like(y_ref)

    def fetch(e, slot):
        return (pltpu.make_async_copy(wgu_hbm.at[e], wgu_f32.at[slot], sems.at[slot, 0]),
                pltpu.make_async_copy(wdn_hbm.at[e], wdn_f32.at[slot], sems.at[slot, 1]))

    @pl.when((i < nb_ref[0]) & ((i == 0) | (be_ref[i] != be_ref[jnp.maximum(i - 1, 0)])))
    def _():
        slot = seg_ref[i] % 2

        @pl.when(i == 0)
        def _():
            for c in fetch(be_ref[0], 0):
                c.start()

        for c in fetch(be_ref[i], slot):
            c.wait()

        @pl.when(nxt_ref[i] >= 0)
        def _():
            for c in fetch(nxt_ref[i], 1 - slot):
                c.start()

        def cast(j, carry):
            r0 = pl.multiple_of(j * cast_rows, cast_rows)
            wgu_bf[pl.ds(r0, cast_rows), :] = wgu_f32[slot, pl.ds(r0, cast_rows), :].astype(bf16)
            wdn_bf[pl.ds(r0, cast_rows), :] = wdn_f32[slot, pl.ds(r0, cast_rows), :].astype(bf16)
            return carry
        lax.fori_loop(0, D_MODEL // cast_rows, cast, 0)

    def mlp(rows):
        xb = _unpack_rows(x_ref, 0, rows)
        hid = []
        for c0 in range(0, D_FF, ff_chunk):
            gate = _dot(xb, wgu_bf[:, c0:c0 + ff_chunk]) + bgu_ref[0, :, c0:c0 + ff_chunk]
            up = (_dot(xb, wgu_bf[:, D_FF + c0:D_FF + c0 + ff_chunk])
                  + bgu_ref[0, :, D_FF + c0:D_FF + c0 + ff_chunk])
            gate = jnp.minimum(gate, SWIGLU_LIMIT)
            up = jnp.clip(up, -SWIGLU_LIMIT, SWIGLU_LIMIT)
            hid.append(((up + 1.0) * (gate * _sigmoid(gate * SWIGLU_ALPHA))).astype(bf16))
        hid = jnp.concatenate(hid, axis=1)
        group = 2 * LANES
        for n0 in range(0, D_MODEL, group):
            y = _dot(hid, wdn_bf[:, n0:n0 + group]) + bdn_ref[0, :, n0:n0 + group]
            (words,) = _pack_rows(y.astype(bf16).astype(f32))
            y_ref[pl.ds(n0 // group, rows, stride=XSLABS), :] = words

    quarter = MOE_TILE // 4
    for q in range(1, 5):
        rows = q * quarter

        @pl.when((i < nb_ref[0]) & (bv_ref[i] > rows - quarter) & (bv_ref[i] <= rows))
        def _(rows=rows):
            mlp(rows)
            if rows < MOE_TILE:
                rest = (MOE_TILE - rows) * XSLABS
                y_ref[pl.ds(rows * XSLABS, rest), :] = jnp.zeros((rest, LANES), jnp.uint32)


def _experts(block_e, block_seg, next_e, block_valid, n_used, xs, wgu, bgu, wdn, bdn):
    n_blocks = block_e.shape[0]
    grid_spec = pltpu.PrefetchScalarGridSpec(
        num_scalar_prefetch=5,
        grid=(n_blocks,),
        in_specs=[
            pl.BlockSpec((MOE_TILE * XSLABS, LANES),
                         lambda i, be, sg, nx, bv, nb: (jnp.minimum(i, nb[0] - 1), 0)),
            pl.BlockSpec(memory_space=pl.ANY),
            pl.BlockSpec((1, 1, 2 * D_FF), lambda i, be, sg, nx, bv, nb: (be[i], 0, 0)),
            pl.BlockSpec(memory_space=pl.ANY),
            pl.BlockSpec((1, 1, D_MODEL), lambda i, be, sg, nx, bv, nb: (be[i], 0, 0)),
        ],
        out_specs=pl.BlockSpec((MOE_TILE * XSLABS, LANES), lambda i, be, sg, nx, bv, nb: (i, 0)),
        scratch_shapes=[
            pltpu.VMEM((2, D_MODEL, 2 * D_FF), f32),
            pltpu.VMEM((2, D_FF, D_MODEL), f32),
            pltpu.VMEM((D_MODEL, 2 * D_FF), bf16),
            pltpu.VMEM((D_FF, D_MODEL), bf16),
            pltpu.SemaphoreType.DMA((2, 2)),
        ],
    )
    return pl.pallas_call(
        functools.partial(_expert_kernel, ff_chunk=256, cast_rows=128),
        grid_spec=grid_spec,
        out_shape=jax.ShapeDtypeStruct((n_blocks * MOE_TILE * XSLABS, LANES), jnp.uint32),
        compiler_params=pltpu.CompilerParams(
            dimension_semantics=("arbitrary",), vmem_limit_bytes=VMEM_LIMIT),
        name="moe_experts",
    )(block_e, block_seg, next_e, block_valid, n_used, xs, wgu, bgu, wdn, bdn)


def _combine_kernel(tab_cur, tab_nxt, ys_hbm, h_ref, tg_ref, tic_ref, rankc_ref, pp_ref, ps_ref,
                    gple_ref, wpg_ref, wpp_ref, gfin_ref, yp_ref, ys_ref, stage, sems, *, n_prompt_tiles):
    i = pl.program_id(0)
    nt = pl.num_programs(0)
    n = COMBINE_TILE * TOP_K
    slot_rows = n * XSLABS

    def gather(tab_ref, slot):
        slot_base = slot * slot_rows
        for bit in range(RUN_BITS - 1, -1, -1):
            size = (1 << bit) * XSLABS

            def issue(p, carry, bit=bit, size=size):
                dst0 = pl.multiple_of(tab_ref[0, 0, RUN_SRC + bit * N_EXPERTS + p] * XSLABS, XSLABS)
                src0 = pl.multiple_of(tab_ref[0, 0, RUN_DST + bit * N_EXPERTS + p] * XSLABS, XSLABS)
                pltpu.make_async_copy(ys_hbm.at[pl.ds(src0, size)], stage.at[pl.ds(slot_base + dst0, size)],
                                      sems.at[slot]).start(priority=bit % 2)
                return carry

            lax.fori_loop(0, tab_ref[0, 0, RUN_COUNT + bit], issue, 0)

    @pl.when(i == 0)
    def _():
        gather(tab_cur, 0)

    @pl.when(i + 1 < nt)
    def _():
        gather(tab_nxt, (i + 1) % 2)

    slot = i % 2
    base = pl.multiple_of(slot * slot_rows, slot_rows)
    pltpu.make_async_copy(ys_hbm.at[pl.ds(0, slot_rows)], stage.at[pl.ds(base, slot_rows)],
                          sems.at[slot]).wait()

    tic = tic_ref[...]
    shift = jnp.zeros(tic.shape, i32)
    for e in range(N_EXPERTS):
        shift = jnp.where(tic == e, tab_cur[0, 0, RUN_SHIFT + e], shift)
    staged = rankc_ref[...] + shift
    tg = tg_ref[...]
    staged_k = [jnp.broadcast_to(staged[:, k:k + 1], (COMBINE_TILE, LANES)) for k in range(TOP_K)]
    gate_k = [jnp.broadcast_to(tg[:, k:k + 1], (COMBINE_TILE, LANES)) for k in range(TOP_K)]
    lane = lax.broadcasted_iota(i32, (COMBINE_TILE, LANES), 1)
    groups = []
    for c in range(n // LANES):
        g = jnp.zeros((COMBINE_TILE, LANES), f32)
        for k in range(TOP_K):
            g = jnp.where(lane + c * LANES == staged_k[k], gate_k[k], g)
        groups.append(g)
    gates = jnp.concatenate(groups, axis=1)
    gates_hi = gates.astype(bf16)
    gates_lo = (gates - gates_hi.astype(f32)).astype(bf16)
    rows = _unpack_rows(stage, base, n)
    h = h_ref[...] + _dot(gates_hi, rows) + _dot(gates_lo, rows)
    gate = _sigmoid(_dot(_rms(h, gple_ref[...]).astype(bf16), wpg_ref[...]))
    p = _select_rows(i < n_prompt_tiles, pp_ref, ps_ref).astype(bf16)
    h = h + gate * _dot(p, wpp_ref[...])
    y = _rms(h, gfin_ref[...])

    @pl.when(i < n_prompt_tiles)
    def _():
        yp_ref[...] = y

    @pl.when(i >= n_prompt_tiles)
    def _():
        ys_ref[...] = y


def _combine(tile_table, ys, h1, tg, tic, rankc, pp, ps, gple, wpg, wpp, gfin):
    t = h1.shape[0]
    nt = t // COMBINE_TILE
    npt = pp.shape[0] // COMBINE_TILE
    n = COMBINE_TILE * TOP_K
    row = lambda w: pl.BlockSpec((COMBINE_TILE, w), lambda i: (i, 0))
    full = lambda shape: pl.BlockSpec(shape, lambda i: (0,) * len(shape))
    return pl.pallas_call(
        functools.partial(_combine_kernel, n_prompt_tiles=npt),
        grid=(nt,),
        in_specs=[
            pl.BlockSpec((1, 1, RUN_TABLE), lambda i: (i, 0, 0), memory_space=pltpu.SMEM),
            pl.BlockSpec((1, 1, RUN_TABLE), lambda i: (jnp.minimum(i + 1, nt - 1), 0, 0),
                         memory_space=pltpu.SMEM),
            pl.BlockSpec(memory_space=pl.ANY),
            row(D_MODEL), row(LANES), row(LANES), row(LANES),
            pl.BlockSpec((COMBINE_TILE, PLE_DIM), lambda i: (jnp.minimum(i, npt - 1), 0)),
            pl.BlockSpec((COMBINE_TILE, PLE_DIM), lambda i: (jnp.maximum(i - npt, 0), 0)),
            full((1, D_MODEL)), full((D_MODEL, D_MODEL)), full((PLE_DIM, D_MODEL)),
            full((1, D_MODEL)),
        ],
        out_specs=[
            pl.BlockSpec((COMBINE_TILE, D_MODEL), lambda i: (jnp.minimum(i, npt - 1), 0)),
            pl.BlockSpec((COMBINE_TILE, D_MODEL), lambda i: (jnp.maximum(i - npt, 0), 0)),
        ],
        out_shape=[
            jax.ShapeDtypeStruct((pp.shape[0], D_MODEL), f32),
            jax.ShapeDtypeStruct((ps.shape[0], D_MODEL), f32),
        ],
        scratch_shapes=[
            pltpu.VMEM((2 * n * XSLABS, LANES), jnp.uint32),
            pltpu.SemaphoreType.DMA((2,)),
        ],
        compiler_params=pltpu.CompilerParams(
            dimension_semantics=("arbitrary",), vmem_limit_bytes=VMEM_LIMIT),
        name="moe_combine_ple",
    )(tile_table, tile_table, ys, h1, tg, tic, rankc, pp, ps, gple, wpg, wpp, gfin)


def _routing_tables(top_i, counts, part_counts, n_blocks):
    padded = (counts + MOE_TILE - 1) // MOE_TILE * MOE_TILE
    pend = jnp.cumsum(padded)
    pstart = pend - padded
    block_rows = jnp.arange(n_blocks, dtype=i32) * MOE_TILE
    block_e = jnp.minimum(
        jnp.sum((block_rows[:, None] >= pend[None, :]).astype(i32), axis=1), N_EXPERTS - 1)
    experts = jnp.arange(N_EXPERTS, dtype=i32)
    of_block = lambda table: jnp.sum(jnp.where(block_e[:, None] == experts[None, :], table[None, :], 0), axis=1)
    block_valid = jnp.clip(of_block(pstart + counts) - block_rows, 0, MOE_TILE)
    n_used = pend[-1:] // MOE_TILE
    owns = padded > 0
    later = owns[None, :] & (experts[None, :] > experts[:, None])
    next_of_e = jnp.where(jnp.any(later, axis=1), jnp.argmax(later, axis=1), -1)
    block_seg = of_block(jnp.cumsum(owns.astype(i32)) - 1)
    next_e = of_block(next_of_e.astype(i32))

    nt = top_i.shape[1] // DISPATCH_TILE
    tile_n = part_counts[:, :, :TOKEN_TILE // DISPATCH_TILE].transpose(0, 2, 1).reshape(nt, N_EXPERTS)
    seen = jnp.cumsum(tile_n, axis=0) - tile_n
    tile_src = jnp.cumsum(tile_n, axis=1) - tile_n
    tile_dst = pstart[None, :] + seen
    bits = jnp.arange(RUN_BITS, dtype=i32)[None, :, None]
    n3 = tile_n[:, None, :]
    has_piece = ((n3 >> bits) & 1) == 1
    covered = (n3 >> (bits + 1)) << (bits + 1)
    earlier = (experts[:, None] < experts[None, :])[None, None]
    place = jnp.sum((has_piece[..., None] & earlier).astype(i32), axis=2)
    listed = has_piece[..., None] & (place[..., None] == experts[None, None, None, :])
    compact = lambda v: jnp.sum(jnp.where(listed, v[..., None], 0), axis=2)
    piece_src = compact(tile_src[:, None, :] + covered)
    piece_dst = compact(tile_dst[:, None, :] + covered)
    piece_cnt = jnp.sum(has_piece.astype(i32), axis=2)
    tile_table = jnp.concatenate(
        [tile_src - seen, jnp.pad(piece_cnt, ((0, 0), (0, N_EXPERTS - RUN_BITS))),
         piece_src.reshape(nt, -1), piece_dst.reshape(nt, -1)], axis=1)
    tables = (block_e, block_seg, next_e, block_valid, n_used)
    return (tile_table.reshape(nt, 1, RUN_TABLE).astype(i32),) + tuple(
        t.astype(i32) for t in tables)


def kernel(x_prompt, x_sample, p_prompt, p_sample, state_gla, state_hgrn, g_mix, w_in, w_gk2, b_gk2,
           gn_gla, lb_logits, gn_hgrn, w_out, g_ffn, w_router, b_router, w_gu, b_gu, w_dn, b_dn,
           g_ple, w_ple_gate, w_ple_proj, g_final):
    batch, seq, _ = x_prompt.shape
    dec_batch, dec_seq, _ = x_sample.shape
    tp, ts = batch * seq, dec_batch * dec_seq
    xp = x_prompt.reshape(tp, D_MODEL)
    xs = x_sample.reshape(ts, D_MODEL)

    wgk = jnp.concatenate(
        [w_gk2[0], jnp.zeros((R1_PAD - GLA_LOWRANK, GLA_KDIM), f32)], axis=0).astype(bf16)
    wr = w_router[0].T.astype(bf16)
    br = b_router[0].reshape(N_EXPERTS, 1)

    z = _in_proj(xp, xs, g_mix[0].reshape(1, D_MODEL), w_in[0])

    mixer_params = (wgk, b_gk2[0].reshape(1, GLA_KDIM), gn_gla[0].reshape(1, GLA_DV),
                    lb_logits[0:2], gn_hgrn[0].reshape(1, HGRN_DV))
    o_p, new_gla_p, new_hgrn_p = _mixer_prompt(
        z, *mixer_params, batch=batch, seq=seq, rows_per_step=512, block_rows=256)
    z_s = jnp.pad(z[tp:].reshape(dec_batch, dec_seq, Z_W), ((0, 0), (0, SAMPLE_PAD_T - dec_seq), (0, 0)))
    o_s, new_gla_s, new_hgrn_s = _mixer_sample(
        z_s.reshape(dec_batch * SAMPLE_PAD_T, Z_W), state_gla, state_hgrn, *mixer_params,
        steps=dec_seq, batch_block=8)
    o_s = o_s.reshape(dec_batch, SAMPLE_PAD_T, D_MODEL)[:, :dec_seq].reshape(ts, D_MODEL).astype(bf16)

    h1, xn, top_i, top_g, rank, top_ic, rank_c, counts, part_counts = _out_proj(
        xp, xs, o_p, o_s, w_out[0].astype(bf16), g_ffn[0].reshape(1, D_MODEL), wr, br)

    t = tp + ts
    n_blocks = -(-(t * TOP_K) // MOE_TILE) + N_EXPERTS
    tile_table, block_e, block_seg, next_e, block_valid, n_used = _routing_tables(
        top_i, counts[:, 0], part_counts, n_blocks)
    x_sorted = _dispatch(block_valid, tile_table, top_i, rank, xn)
    y_sorted = _experts(block_e, block_seg, next_e, block_valid, n_used, x_sorted,
                        w_gu[0], b_gu[0].reshape(N_EXPERTS, 1, 2 * D_FF),
                        w_dn[0], b_dn[0].reshape(N_EXPERTS, 1, D_MODEL))

    y_p, y_s = _combine(tile_table, y_sorted, h1, top_g, top_ic, rank_c, p_prompt[0].reshape(tp, PLE_DIM),
                        p_sample[0].reshape(ts, PLE_DIM), g_ple[0].reshape(1, D_MODEL),
                        w_ple_gate[0].astype(bf16), w_ple_proj[0].astype(bf16), g_final.reshape(1, D_MODEL))

    return (y_p.reshape(batch, seq, D_MODEL), y_s.reshape(dec_batch, dec_seq, D_MODEL),
            new_gla_p, new_hgrn_p, new_gla_s, new_hgrn_s)
```

```python
import functools

import jax
import jax.numpy as jnp
from jax import lax
from jax.experimental import pallas as pl
from jax.experimental.pallas import tpu as pltpu

f32 = jnp.float32
bf16 = jnp.bfloat16
i32 = jnp.int32

D_MODEL = 1024
GLA_HEADS = 4
GLA_DK = 64
GLA_DV = 128
GLA_KDIM = GLA_HEADS * GLA_DK
GLA_WIDTH = GLA_HEADS * GLA_DV
GLA_LOWRANK = 16
GLA_NORMALIZER = 16.0
HGRN_HEADS = 4
HGRN_DK = 128
HGRN_DV = 128
HGRN_FDIM = HGRN_HEADS * HGRN_DK
HGRN_WIDTH = HGRN_HEADS * HGRN_DV
CHUNK = 64
N_EXPERTS = 32
TOP_K = 4
D_FF = 1024
SWIGLU_LIMIT = 7.0
SWIGLU_ALPHA = 1.702
PLE_DIM = 256
EPS = 1e-6

LANES = 128
SAMPLE_PAD_T = 8
XSLABS = D_MODEL // (2 * LANES)
ROUTE_ROWS = 8
R1_PAD = LANES
Z_Q1 = 0
Z_K1 = Z_Q1 + GLA_KDIM
Z_V1 = Z_K1 + GLA_KDIM
Z_G1 = Z_V1 + GLA_WIDTH
Z_R1 = Z_G1 + GLA_WIDTH
Z_Q2 = Z_R1 + R1_PAD
Z_F2 = Z_Q2 + HGRN_FDIM
Z_I2 = Z_F2 + HGRN_FDIM
Z_G2 = Z_I2 + HGRN_WIDTH
Z_W = Z_G2 + HGRN_WIDTH

TOKEN_TILE = 512
MOE_TILE = 512
DISPATCH_TILE = 256
COMBINE_TILE = DISPATCH_TILE
RUN_BITS = DISPATCH_TILE.bit_length()
RUN_SHIFT = 0
RUN_COUNT = RUN_SHIFT + N_EXPERTS
RUN_SRC = RUN_COUNT + N_EXPERTS
RUN_DST = RUN_SRC + RUN_BITS * N_EXPERTS
RUN_TABLE = RUN_DST + RUN_BITS * N_EXPERTS
VMEM_LIMIT = 56 * 1024 * 1024


def _rms(x, g):
    return x * lax.rsqrt(jnp.mean(x * x, axis=-1, keepdims=True) + EPS) * g


def _sigmoid(x):
    return 0.5 * jnp.tanh(0.5 * x) + 0.5


def _dot(a, b):
    return jnp.dot(a, b, preferred_element_type=f32)


def _dot_nt(a, b):
    return lax.dot_general(a, b, (((1,), (1,)), ((), ())), preferred_element_type=f32)


def _dot_tn(a, b):
    return lax.dot_general(a, b, (((0,), (0,)), ((), ())), preferred_element_type=f32)


def _cumsum_rows(tri, x):
    hi = x.astype(bf16)
    r1 = x - hi.astype(f32)
    mid = r1.astype(bf16)
    lo = (r1 - mid.astype(f32)).astype(bf16)
    return _dot(tri, hi) + _dot(tri, mid) + _dot(tri, lo)


def _select_rows(is_prompt, a_ref, b_ref):
    return jnp.where(is_prompt, a_ref[...], b_ref[...])


def _inproj_kernel(xp_ref, xs_ref, g_ref, w_hbm, z_ref, w_f32, w_bf, sem,
                   *, n_prompt_tiles, col_chunk):
    @pl.when(pl.program_id(0) == 0)
    def _():
        copy = pltpu.make_async_copy(w_hbm, w_f32, sem)
        copy.start()
        copy.wait()
        for c0 in range(0, Z_W, LANES):
            if c0 < Z_R1:
                cols = w_f32[c0:c0 + LANES, :]
            elif c0 == Z_R1:
                cols = jnp.concatenate(
                    [w_f32[Z_R1:Z_R1 + GLA_LOWRANK, :],
                     jnp.zeros((R1_PAD - GLA_LOWRANK, D_MODEL), f32)], axis=0)
            else:
                s0 = c0 - (R1_PAD - GLA_LOWRANK)
                cols = w_f32[s0:s0 + LANES, :]
            w_bf[:, c0:c0 + LANES] = cols.T.astype(bf16)

    is_prompt = pl.program_id(0) < n_prompt_tiles
    x = _select_rows(is_prompt, xp_ref, xs_ref)
    a = _rms(x, g_ref[...]).astype(bf16)
    for c0 in range(0, Z_W, col_chunk):
        c1 = min(c0 + col_chunk, Z_W)
        z_ref[:, c0:c1] = _dot(a, w_bf[:, c0:c1]).astype(bf16)


def _in_proj(xp, xs, g, w):
    tp, ts = xp.shape[0], xs.shape[0]
    npt = tp // TOKEN_TILE
    nt = npt + ts // TOKEN_TILE
    return pl.pallas_call(
        functools.partial(_inproj_kernel, n_prompt_tiles=npt, col_chunk=512),
        grid=(nt,),
        in_specs=[
            pl.BlockSpec((TOKEN_TILE, D_MODEL), lambda i: (jnp.minimum(i, npt - 1), 0)),
            pl.BlockSpec((TOKEN_TILE, D_MODEL), lambda i: (jnp.maximum(i - npt, 0), 0)),
            pl.BlockSpec((1, D_MODEL), lambda i: (0, 0)),
            pl.BlockSpec(memory_space=pl.ANY),
        ],
        out_specs=pl.BlockSpec((TOKEN_TILE, Z_W), lambda i: (i, 0)),
        out_shape=jax.ShapeDtypeStruct((tp + ts, Z_W), bf16),
        scratch_shapes=[
            pltpu.VMEM(w.shape, f32),
            pltpu.VMEM((D_MODEL, Z_W), bf16),
            pltpu.SemaphoreType.DMA(()),
        ],
        compiler_params=pltpu.CompilerParams(
            dimension_semantics=("arbitrary",), vmem_limit_bytes=VMEM_LIMIT),
        name="in_proj",
    )(xp, xs, g, w)


def _block_consts(r, c):
    shift = c.bit_length() - 1
    row = lax.broadcasted_iota(i32, (r, r), 0)
    col = lax.broadcasted_iota(i32, (r, r), 1)
    causal = (lax.shift_right_logical(row, shift) == lax.shift_right_logical(col, shift)) & (row >= col)
    return causal.astype(bf16), causal


def _subchunk_row(x, c, idx):
    parts = [jnp.broadcast_to(x[j * c + idx:j * c + idx + 1, :], (c, x.shape[1]))
             for j in range(x.shape[0] // c)]
    return jnp.concatenate(parts, axis=0)


def _gated_block(q, k, v, log_a, states, *, c, scale, ref_row, tri, causal, heads_per_group, scan):
    r = q.shape[0]
    nsub = r // c
    groups = q.shape[1] // LANES
    b = _cumsum_rows(tri, log_a)
    b_ref = _subchunk_row(b, c, ref_row)
    b_last = _subchunk_row(b, c, c - 1)
    qs = q * scale
    q_r = qs * jnp.exp(b - b_ref)
    k_r32 = k * jnp.exp(b_ref - b)
    k_r = k_r32.astype(bf16)
    k_end = k_r32 * jnp.exp(b_last - b_ref)
    q_b = q_r * jnp.exp(b_ref)
    decay = jnp.exp(b_last)
    vb = v.astype(bf16)
    lane = lax.broadcasted_iota(i32, (r, LANES), 1)
    width = LANES // heads_per_group
    outs, new_states = [], []
    for g in range(groups):
        gl = slice(g * LANES, (g + 1) * LANES)
        masks = [None] if heads_per_group == 1 else [
            (lane >= j * width) & (lane < (j + 1) * width) for j in range(heads_per_group)]
        pick = lambda x, m: x if m is None else jnp.where(m, x, 0.0)
        q_r_h = [pick(q_r[:, gl], m).astype(bf16) for m in masks]
        q_b_h = [pick(q_b[:, gl], m).astype(bf16) for m in masks]
        k_end_h = [pick(k_end[:, gl], m).astype(bf16) for m in masks]
        v_h = [vb[:, (g * heads_per_group + j) * LANES:(g * heads_per_group + j + 1) * LANES]
               for j in range(heads_per_group)]
        o_intra = []
        for j in range(heads_per_group):
            att = _dot_nt(q_r_h[j], k_r[:, gl])
            att = jnp.where(causal, att, 0.0).astype(bf16)
            o_intra.append(_dot(att, v_h[j]))
        upd = []
        for s in range(nsub):
            rs = slice(s * c, (s + 1) * c)
            u = _dot_tn(v_h[0][rs], k_end_h[0][rs])
            for j in range(1, heads_per_group):
                u = u + _dot_tn(v_h[j][rs], k_end_h[j][rs])
            upd.append(u)
        o_inter = [[] for _ in range(heads_per_group)]
        st = states[g] if scan else None
        group_states = []
        for s in range(nsub):
            rs = slice(s * c, (s + 1) * c)
            if not scan:
                st = states[s][g]
            st_b = st.astype(bf16)
            for j in range(heads_per_group):
                o_inter[j].append(_dot_nt(q_b_h[j][rs], st_b))
            st = st * decay[s * c:s * c + 1, gl] + upd[s]
            if not scan:
                group_states.append(st)
        for j in range(heads_per_group):
            outs.append(o_intra[j] + jnp.concatenate(o_inter[j], axis=0))
        new_states.append(st if scan else group_states)
    if not scan:
        new_states = [[new_states[g][s] for g in range(groups)] for s in range(nsub)]
    return outs, new_states


def _head_norm(o, g):
    return o * lax.rsqrt(jnp.mean(o * o, axis=-1, keepdims=True) + EPS) * g


def _mixer_block(zb, st_gla, st_hgrn, wgk, bgk, gng, lb, gnh, *, c, ref_row, tri, causal, valid, scan):
    z = zb.astype(f32)
    gk = _dot(zb[:, Z_R1:Z_R1 + R1_PAD], wgk) + bgk
    log_a = (jnp.minimum(gk, 0.0) - jnp.log(1.0 + jnp.exp(-jnp.abs(gk)))) / GLA_NORMALIZER
    k1 = z[:, Z_K1:Z_K1 + GLA_KDIM]
    if valid is not None:
        log_a = jnp.where(valid, log_a, 0.0)
        k1 = jnp.where(valid, k1, 0.0)
    o1, st_gla = _gated_block(
        z[:, Z_Q1:Z_Q1 + GLA_KDIM], k1, zb[:, Z_V1:Z_V1 + GLA_WIDTH], log_a, st_gla,
        c=c, scale=GLA_DK ** -0.5, ref_row=ref_row, tri=tri, causal=causal,
        heads_per_group=LANES // GLA_DK, scan=scan)
    forget = lb + (1.0 - lb) * _sigmoid(z[:, Z_F2:Z_F2 + HGRN_FDIM])
    k2 = 1.0 - forget
    log_f = jnp.log(forget)
    if valid is not None:
        log_f = jnp.where(valid, log_f, 0.0)
        k2 = jnp.where(valid, k2, 0.0)
    q2 = z[:, Z_Q2:Z_Q2 + HGRN_FDIM]
    i2 = z[:, Z_I2:Z_I2 + HGRN_WIDTH]
    o2, st_hgrn = _gated_block(
        q2 * _sigmoid(q2), k2, i2 * _sigmoid(i2), log_f, st_hgrn,
        c=c, scale=1.0, ref_row=ref_row, tri=tri, causal=causal, heads_per_group=1, scan=scan)
    cols = []
    for h in range(GLA_HEADS):
        g1 = z[:, Z_G1 + h * GLA_DV:Z_G1 + (h + 1) * GLA_DV]
        cols.append(_head_norm(o1[h], gng) * (g1 * _sigmoid(g1)))
    for h in range(HGRN_HEADS):
        g2 = z[:, Z_G2 + h * HGRN_DV:Z_G2 + (h + 1) * HGRN_DV]
        cols.append(_head_norm(o2[h], gnh) * _sigmoid(g2))
    return jnp.concatenate(cols, axis=1), st_gla, st_hgrn


def _lower_bound(lbl_ref):
    l = lbl_ref[...]
    m = jnp.max(l, axis=0, keepdims=True)
    e = jnp.exp(l - m)
    return e[0:1, :] / jnp.sum(e, axis=0, keepdims=True)


GLA_GROUPS = GLA_KDIM // LANES


def _gla_state_t(s_ref_heads):
    return s_ref_heads.reshape(LANES, GLA_DV).T


def _gla_state_from_t(st):
    return st.T.reshape(LANES // GLA_DK, GLA_DK, GLA_DV)


def _mixer_prompt_kernel(z_ref, wgk_ref, bgk_ref, gng_ref, lbl_ref, gnh_ref,
                         o_ref, sg_out, sh_out, sg_t, sh_t, *, block_rows):
    j = pl.program_id(1)

    @pl.when(j == 0)
    def _():
        sg_t[...] = jnp.zeros_like(sg_t)
        sh_t[...] = jnp.zeros_like(sh_t)

    tri, causal = _block_consts(block_rows, CHUNK)
    lb = _lower_bound(lbl_ref)
    wgk, bgk, gng, gnh = wgk_ref[...], bgk_ref[...], gng_ref[...], gnh_ref[...]

    def body(blk, carry):
        r0 = pl.multiple_of(blk * block_rows, block_rows)
        o, st_g, st_h = _mixer_block(
            z_ref[pl.ds(r0, block_rows), :], [sg_t[g] for g in range(GLA_GROUPS)],
            [sh_t[h] for h in range(HGRN_HEADS)], wgk, bgk, gng, lb, gnh,
            c=CHUNK, ref_row=CHUNK // 2, tri=tri, causal=causal, valid=None, scan=True)
        o_ref[pl.ds(r0, block_rows), :] = o.astype(bf16)
        for g in range(GLA_GROUPS):
            sg_t[g] = st_g[g]
        for h in range(HGRN_HEADS):
            sh_t[h] = st_h[h]
        return carry

    lax.fori_loop(0, z_ref.shape[0] // block_rows, body, 0)

    @pl.when(j == pl.num_programs(1) - 1)
    def _():
        per = LANES // GLA_DK
        for g in range(GLA_GROUPS):
            sg_out[0, 0, g * per:(g + 1) * per] = _gla_state_from_t(sg_t[g])
        for h in range(HGRN_HEADS):
            sh_out[0, 0, h] = sh_t[h].T


def _mixer_prompt(z, wgk, bgk, gng, lbl, gnh, *, batch, seq, rows_per_step, block_rows):
    nj = seq // rows_per_step
    full = lambda shape: pl.BlockSpec(shape, lambda b, j: (0,) * len(shape))
    return pl.pallas_call(
        functools.partial(_mixer_prompt_kernel, block_rows=block_rows),
        grid=(batch, nj),
        in_specs=[
            pl.BlockSpec((rows_per_step, Z_W), lambda b, j: (b * nj + j, 0)),
            full((R1_PAD, GLA_KDIM)), full((1, GLA_KDIM)), full((1, GLA_DV)),
            full((2, HGRN_FDIM)), full((1, HGRN_DV)),
        ],
        out_specs=[
            pl.BlockSpec((rows_per_step, D_MODEL), lambda b, j: (b * nj + j, 0)),
            pl.BlockSpec((1, 1, GLA_HEADS, GLA_DK, GLA_DV), lambda b, j: (0, b, 0, 0, 0)),
            pl.BlockSpec((1, 1, HGRN_HEADS, HGRN_DK, HGRN_DV), lambda b, j: (0, b, 0, 0, 0)),
        ],
        out_shape=[
            jax.ShapeDtypeStruct((batch * seq, D_MODEL), bf16),
            jax.ShapeDtypeStruct((1, batch, GLA_HEADS, GLA_DK, GLA_DV), f32),
            jax.ShapeDtypeStruct((1, batch, HGRN_HEADS, HGRN_DK, HGRN_DV), f32),
        ],
        scratch_shapes=[
            pltpu.VMEM((GLA_GROUPS, GLA_DV, LANES), f32),
            pltpu.VMEM((HGRN_HEADS, HGRN_DV, HGRN_DK), f32),
        ],
        compiler_params=pltpu.CompilerParams(
            dimension_semantics=("arbitrary", "arbitrary"), vmem_limit_bytes=VMEM_LIMIT),
        name="mixer_prompt",
    )(z, wgk, bgk, gng, lbl, gnh)


def _mixer_sample_kernel(z_ref, sg_in, sh_in, wgk_ref, bgk_ref, gng_ref, lbl_ref, gnh_ref,
                         o_ref, sg_out, sh_out, *, batch_block, steps):
    rows = batch_block * SAMPLE_PAD_T
    tri, causal = _block_consts(rows, SAMPLE_PAD_T)
    step = lax.broadcasted_iota(i32, (rows, 1), 0) & (SAMPLE_PAD_T - 1)
    valid = step < steps
    lb = _lower_bound(lbl_ref)
    per = LANES // GLA_DK
    st_g = [[_gla_state_t(sg_in[0, s, g * per:(g + 1) * per]) for g in range(GLA_GROUPS)]
            for s in range(batch_block)]
    st_h = [[sh_in[0, s, h].T for h in range(HGRN_HEADS)] for s in range(batch_block)]
    o, st_g, st_h = _mixer_block(
        z_ref[...], st_g, st_h, wgk_ref[...], bgk_ref[...], gng_ref[...], lb, gnh_ref[...],
        c=SAMPLE_PAD_T, ref_row=steps // 2, tri=tri, causal=causal, valid=valid, scan=False)
    o_ref[...] = o
    for s in range(batch_block):
        for g in range(GLA_GROUPS):
            sg_out[0, s, g * per:(g + 1) * per] = _gla_state_from_t(st_g[s][g])
        for h in range(HGRN_HEADS):
            sh_out[0, s, h] = st_h[s][h].T


def _mixer_sample(z2, state_gla, state_hgrn, wgk, bgk, gng, lbl, gnh, *, steps, batch_block):
    batch = z2.shape[0] // SAMPLE_PAD_T
    rows = batch_block * SAMPLE_PAD_T
    full = lambda shape: pl.BlockSpec(shape, lambda i: (0,) * len(shape))
    sg_spec = pl.BlockSpec((1, batch_block, GLA_HEADS, GLA_DK, GLA_DV), lambda i: (0, i, 0, 0, 0))
    sh_spec = pl.BlockSpec((1, batch_block, HGRN_HEADS, HGRN_DK, HGRN_DV), lambda i: (0, i, 0, 0, 0))
    return pl.pallas_call(
        functools.partial(_mixer_sample_kernel, batch_block=batch_block, steps=steps),
        grid=(batch // batch_block,),
        in_specs=[
            pl.BlockSpec((rows, Z_W), lambda i: (i, 0)),
            sg_spec, sh_spec,
            full((R1_PAD, GLA_KDIM)), full((1, GLA_KDIM)), full((1, GLA_DV)),
            full((2, HGRN_FDIM)), full((1, HGRN_DV)),
        ],
        out_specs=[pl.BlockSpec((rows, D_MODEL), lambda i: (i, 0)), sg_spec, sh_spec],
        out_shape=[
            jax.ShapeDtypeStruct((batch * SAMPLE_PAD_T, D_MODEL), f32),
            jax.ShapeDtypeStruct(state_gla.shape, f32),
            jax.ShapeDtypeStruct(state_hgrn.shape, f32),
        ],
        compiler_params=pltpu.CompilerParams(
            dimension_semantics=("arbitrary",), vmem_limit_bytes=VMEM_LIMIT),
        name="mixer_sample",
    )(z2, state_gla, state_hgrn, wgk, bgk, gng, lbl, gnh)


HIGH_HALF = 0xFFFF0000


def _pack_rows(x):
    bits = lax.bitcast_convert_type(x, jnp.uint32)
    return [(bits[:, (2 * s + 1) * LANES:(2 * s + 2) * LANES] & jnp.uint32(HIGH_HALF))
            | (bits[:, 2 * s * LANES:(2 * s + 1) * LANES] >> 16) for s in range(x.shape[1] // (2 * LANES))]


def _unpack_rows(ref, start, rows):
    cols = []
    for s in range(XSLABS):
        w = ref[pl.ds(start + s, rows, stride=XSLABS), :]
        cols.append(lax.bitcast_convert_type(w << 16, f32).astype(bf16))
        cols.append(lax.bitcast_convert_type(w & jnp.uint32(HIGH_HALF), f32).astype(bf16))
    return jnp.concatenate(cols, axis=1)


def _outproj_kernel(xp_ref, xs_ref, op_ref, os_ref, wo_ref, g_ref, wr_ref, br_ref,
                    h_ref, xn_ref, ti_ref, tg_ref, rank_ref, tic_ref, rankc_ref, cnt_ref, part_ref, run_cnt,
                    *, n_prompt_tiles):
    i = pl.program_id(0)

    @pl.when(i == 0)
    def _():
        run_cnt[...] = jnp.zeros_like(run_cnt)

    x = _select_rows(i < n_prompt_tiles, xp_ref, xs_ref)
    o = _select_rows(i < n_prompt_tiles, op_ref, os_ref)
    h = x + _dot(o, wo_ref[...])
    h_ref[...] = h
    xn = _rms(h, g_ref[...])
    xn_ref[...] = xn.astype(bf16)
    l = _dot_nt(wr_ref[...], xn.astype(bf16)) + br_ref[...]
    expert = lax.broadcasted_iota(i32, l.shape, 0)
    neg = jnp.float32(-jnp.inf)
    vals, idxs = [], []
    for _ in range(TOP_K):
        m = jnp.max(l, axis=0, keepdims=True)
        idx = jnp.min(jnp.where(l == m, expert, N_EXPERTS), axis=0, keepdims=True)
        vals.append(m)
        idxs.append(idx)
        l = jnp.where(expert == idx, neg, l)
    es = [jnp.exp(v - vals[0]) for v in vals]
    denom = es[0] + es[1] + es[2] + es[3]

    onehot = jnp.zeros(l.shape, f32)
    for k in range(TOP_K):
        onehot = onehot + (expert == idxs[k]).astype(f32)
    n = l.shape[1]
    earlier = (lax.broadcasted_iota(i32, (n, n), 0) < lax.broadcasted_iota(i32, (n, n), 1)).astype(bf16)
    ahead = _dot(onehot.astype(bf16), earlier) + run_cnt[:, 0:1]
    run_cnt[...] = run_cnt[...] + jnp.sum(onehot, axis=1, keepdims=True)
    cnt_ref[...] = run_cnt[...].astype(i32)
    lane = lax.broadcasted_iota(i32, (N_EXPERTS, LANES), 1)
    parts = jnp.zeros((N_EXPERTS, LANES), f32)
    for p in range(n // DISPATCH_TILE):
        c = jnp.sum(onehot[:, p * DISPATCH_TILE:(p + 1) * DISPATCH_TILE], axis=1, keepdims=True)
        parts = jnp.where(lane == p, c, parts)
    part_ref[0] = parts.astype(i32)

    ranks = [jnp.sum(jnp.where(expert == idxs[k], ahead, 0.0), axis=0, keepdims=True).astype(i32)
             for k in range(TOP_K)]
    pad_i = jnp.zeros((ROUTE_ROWS - TOP_K, n), i32)
    ti_ref[...] = jnp.concatenate(idxs + [pad_i], axis=0)
    rank_ref[...] = jnp.concatenate(ranks + [pad_i], axis=0)
    gate_rows = jnp.concatenate([e / denom for e in es] + [jnp.zeros((LANES - TOP_K, n), f32)], axis=0)
    tg_ref[...] = gate_rows.T
    pad_wide = jnp.zeros((LANES - TOP_K, n), i32)
    tic_ref[...] = jnp.concatenate(idxs + [pad_wide], axis=0).T
    rankc_ref[...] = jnp.concatenate(ranks + [pad_wide], axis=0).T


def _out_proj(xp, xs, o_p, o_s, wo, g, wr, br):
    tp, ts = xp.shape[0], xs.shape[0]
    npt = tp // TOKEN_TILE
    nt = npt + ts // TOKEN_TILE
    t = tp + ts
    row = lambda w: pl.BlockSpec((TOKEN_TILE, w), lambda i: (i, 0))
    route = pl.BlockSpec((ROUTE_ROWS, TOKEN_TILE), lambda i: (0, i))
    full = lambda shape: pl.BlockSpec(shape, lambda i: (0,) * len(shape))
    prompt_rows = pl.BlockSpec((TOKEN_TILE, D_MODEL), lambda i: (jnp.minimum(i, npt - 1), 0))
    sample_rows = pl.BlockSpec((TOKEN_TILE, D_MODEL), lambda i: (jnp.maximum(i - npt, 0), 0))
    return pl.pallas_call(
        functools.partial(_outproj_kernel, n_prompt_tiles=npt),
        grid=(nt,),
        in_specs=[
            prompt_rows, sample_rows, prompt_rows, sample_rows,
            full((D_MODEL, D_MODEL)), full((1, D_MODEL)),
            full((N_EXPERTS, D_MODEL)), full((N_EXPERTS, 1)),
        ],
        out_specs=[row(D_MODEL), row(D_MODEL),
                   route, row(LANES), route, row(LANES), row(LANES), full((N_EXPERTS, LANES)),
                   pl.BlockSpec((1, N_EXPERTS, LANES), lambda i: (i, 0, 0))],
        out_shape=[
            jax.ShapeDtypeStruct((t, D_MODEL), f32),
            jax.ShapeDtypeStruct((t, D_MODEL), bf16),
            jax.ShapeDtypeStruct((ROUTE_ROWS, t), i32),
            jax.ShapeDtypeStruct((t, LANES), f32),
            jax.ShapeDtypeStruct((ROUTE_ROWS, t), i32),
            jax.ShapeDtypeStruct((t, LANES), i32),
            jax.ShapeDtypeStruct((t, LANES), i32),
            jax.ShapeDtypeStruct((N_EXPERTS, LANES), i32),
            jax.ShapeDtypeStruct((nt, N_EXPERTS, LANES), i32),
        ],
        scratch_shapes=[pltpu.VMEM((N_EXPERTS, LANES), f32)],
        compiler_params=pltpu.CompilerParams(
            dimension_semantics=("arbitrary",), vmem_limit_bytes=VMEM_LIMIT),
        name="out_proj_router",
    )(xp, xs, o_p, o_s, wo, g, wr, br)


def _dispatch_kernel(bv_ref, nz_ref, tab_ref, ti_ref, rank_ref, xn_ref, xs_hbm, zeros, stage, sems, zsem,
                     *, n_blocks):
    @pl.when(pl.program_id(0) == 0)
    def _():
        zeros[...] = jnp.zeros_like(zeros)

        def fill(b, carry):
            @pl.when(bv_ref[b] < MOE_TILE)
            def _():
                r0 = pl.multiple_of(b * (MOE_TILE * XSLABS), MOE_TILE * XSLABS)
                pltpu.make_async_copy(zeros, xs_hbm.at[pl.ds(r0, MOE_TILE * XSLABS)], zsem).start()
            return carry

        def drain(b, carry):
            pltpu.make_async_copy(zeros, xs_hbm.at[pl.ds(0, MOE_TILE * XSLABS)], zsem).wait()
            return carry

        lax.fori_loop(0, n_blocks, fill, 0)
        lax.fori_loop(0, nz_ref[0], drain, 0)

    i = pl.program_id(0)
    nt = pl.num_programs(0)
    n = DISPATCH_TILE * TOP_K
    slot_rows = n * XSLABS
    slot = i % 2
    base = pl.multiple_of(slot * slot_rows, slot_rows)

    def wait_rows(s):
        pltpu.make_async_copy(stage.at[pl.ds(0, slot_rows)], xs_hbm.at[pl.ds(0, slot_rows)], sems.at[s]).wait()

    ti = ti_ref[...]
    shift = jnp.zeros(ti.shape, i32)
    for e in range(N_EXPERTS):
        shift = jnp.where(ti == e, tab_ref[0, 0, RUN_SHIFT + e], shift)
    staged = rank_ref[...] + shift
    row = lax.broadcasted_iota(i32, (n, DISPATCH_TILE), 0)
    pick = row == staged[0:1, :]
    for k in range(1, TOP_K):
        pick = pick | (row == staged[k:k + 1, :])
    rows_sorted = _dot(pick.astype(bf16), xn_ref[...])
    for s, words in enumerate(_pack_rows(rows_sorted)):
        stage[pl.ds(base + s, n, stride=XSLABS), :] = words

    for bit in range(RUN_BITS - 1, -1, -1):
        size = (1 << bit) * XSLABS

        def issue(p, carry, bit=bit, size=size):
            src0 = pl.multiple_of(tab_ref[0, 0, RUN_SRC + bit * N_EXPERTS + p] * XSLABS, XSLABS)
            dst0 = pl.multiple_of(tab_ref[0, 0, RUN_DST + bit * N_EXPERTS + p] * XSLABS, XSLABS)
            pltpu.make_async_copy(stage.at[pl.ds(base + src0, size)], xs_hbm.at[pl.ds(dst0, size)],
                                  sems.at[slot]).start(priority=bit % 2)
            return carry

        lax.fori_loop(0, tab_ref[0, 0, RUN_COUNT + bit], issue, 0)

    @pl.when(i > 0)
    def _():
        wait_rows(1 - slot)

    @pl.when(i == nt - 1)
    def _():
        wait_rows(slot)


def _dispatch(block_valid, tile_table, top_i, rank, xn):
    t = xn.shape[0]
    nt = t // DISPATCH_TILE
    n_blocks = block_valid.shape[0]
    n_zero = jnp.sum((block_valid < MOE_TILE).astype(i32)).reshape(1)
    route = pl.BlockSpec((ROUTE_ROWS, DISPATCH_TILE), lambda i, bv, nz: (0, i))
    grid_spec = pltpu.PrefetchScalarGridSpec(
        num_scalar_prefetch=2,
        grid=(nt,),
        in_specs=[
            pl.BlockSpec((1, 1, RUN_TABLE), lambda i, bv, nz: (i, 0, 0), memory_space=pltpu.SMEM),
            route, route,
            pl.BlockSpec((DISPATCH_TILE, D_MODEL), lambda i, bv, nz: (i, 0)),
        ],
        out_specs=pl.BlockSpec(memory_space=pl.ANY),
        scratch_shapes=[
            pltpu.VMEM((MOE_TILE * XSLABS, LANES), jnp.uint32),
            pltpu.VMEM((2 * DISPATCH_TILE * TOP_K * XSLABS, LANES), jnp.uint32),
            pltpu.SemaphoreType.DMA((2,)),
            pltpu.SemaphoreType.DMA(()),
        ],
    )
    return pl.pallas_call(
        functools.partial(_dispatch_kernel, n_blocks=n_blocks),
        grid_spec=grid_spec,
        out_shape=jax.ShapeDtypeStruct((n_blocks * MOE_TILE * XSLABS, LANES), jnp.uint32),
        compiler_params=pltpu.CompilerParams(
            dimension_semantics=("arbitrary",), vmem_limit_bytes=VMEM_LIMIT),
        name="moe_dispatch",
    )(block_valid, n_zero, tile_table, top_i, rank, xn)


def _expert_kernel(be_ref, seg_ref, nxt_ref, bv_ref, nb_ref, x_ref, wgu_hbm, bgu_ref, wdn_hbm, bdn_ref, y_ref,
                   wgu_f32, wdn_f32, wgu_bf, wdn_bf, sems, *, ff_chunk, cast_rows):
    i = pl.program_id(0)

    @pl.when(i >= nb_ref[0])
    def _():
        y_ref[...] = jnp.zeros_like(y_ref)

    def fetch(e, slot):
        return (pltpu.make_async_copy(wgu_hbm.at[e], wgu_f32.at[slot], sems.at[slot, 0]),
                pltpu.make_async_copy(wdn_hbm.at[e], wdn_f32.at[slot], sems.at[slot, 1]))

    @pl.when((i < nb_ref[0]) & ((i == 0) | (be_ref[i] != be_ref[jnp.maximum(i - 1, 0)])))
    def _():
        slot = seg_ref[i] % 2

        @pl.when(i == 0)
        def _():
            for c in fetch(be_ref[0], 0):
                c.start()

        for c in fetch(be_ref[i], slot):
            c.wait()

        @pl.when(nxt_ref[i] >= 0)
        def _():
            for c in fetch(nxt_ref[i], 1 - slot):
                c.start()

        def cast(j, carry):
            r0 = pl.multiple_of(j * cast_rows, cast_rows)
            wgu_bf[pl.ds(r0, cast_rows), :] = wgu_f32[slot, pl.ds(r0, cast_rows), :].astype(bf16)
            wdn_bf[pl.ds(r0, cast_rows), :] = wdn_f32[slot, pl.ds(r0, cast_rows), :].astype(bf16)
            return carry
        lax.fori_loop(0, D_MODEL // cast_rows, cast, 0)

    def mlp(rows):
        xb = _unpack_rows(x_ref, 0, rows)
        hid = []
        for c0 in range(0, D_FF, ff_chunk):
            gate = _dot(xb, wgu_bf[:, c0:c0 + ff_chunk]) + bgu_ref[0, :, c0:c0 + ff_chunk]
            up = (_dot(xb, wgu_bf[:, D_FF + c0:D_FF + c0 + ff_chunk])
                  + bgu_ref[0, :, D_FF + c0:D_FF + c0 + ff_chunk])
            gate = jnp.minimum(gate, SWIGLU_LIMIT)
            up = jnp.clip(up, -SWIGLU_LIMIT, SWIGLU_LIMIT)
            hid.append(((up + 1.0) * (gate * _sigmoid(gate * SWIGLU_ALPHA))).astype(bf16))
        hid = jnp.concatenate(hid, axis=1)
        group = 2 * LANES
        for n0 in range(0, D_MODEL, group):
            y = _dot(hid, wdn_bf[:, n0:n0 + group]) + bdn_ref[0, :, n0:n0 + group]
            (words,) = _pack_rows(y.astype(bf16).astype(f32))
            y_ref[pl.ds(n0 // group, rows, stride=XSLABS), :] = words

    quarter = MOE_TILE // 4
    for q in range(1, 5):
        rows = q * quarter

        @pl.when((i < nb_ref[0]) & (bv_ref[i] > rows - quarter) & (bv_ref[i] <= rows))
        def _(rows=rows):
            mlp(rows)
            if rows < MOE_TILE:
                rest = (MOE_TILE - rows) * XSLABS
                y_ref[pl.ds(rows * XSLABS, rest), :] = jnp.zeros((rest, LANES), jnp.uint32)


def _experts(block_e, block_seg, next_e, block_valid, n_used, xs, wgu, bgu, wdn, bdn):
    n_blocks = block_e.shape[0]
    grid_spec = pltpu.PrefetchScalarGridSpec(
        num_scalar_prefetch=5,
        grid=(n_blocks,),
        in_specs=[
            pl.BlockSpec((MOE_TILE * XSLABS, LANES),
                         lambda i, be, sg, nx, bv, nb: (jnp.minimum(i, nb[0] - 1), 0)),
            pl.BlockSpec(memory_space=pl.ANY),
            pl.BlockSpec((1, 1, 2 * D_FF), lambda i, be, sg, nx, bv, nb: (be[i], 0, 0)),
            pl.BlockSpec(memory_space=pl.ANY),
            pl.BlockSpec((1, 1, D_MODEL), lambda i, be, sg, nx, bv, nb: (be[i], 0, 0)),
        ],
        out_specs=pl.BlockSpec((MOE_TILE * XSLABS, LANES), lambda i, be, sg, nx, bv, nb: (i, 0)),
        scratch_shapes=[
            pltpu.VMEM((2, D_MODEL, 2 * D_FF), f32),
            pltpu.VMEM((2, D_FF, D_MODEL), f32),
            pltpu.VMEM((D_MODEL, 2 * D_FF), bf16),
            pltpu.VMEM((D_FF, D_MODEL), bf16),
            pltpu.SemaphoreType.DMA((2, 2)),
        ],
    )
    return pl.pallas_call(
        functools.partial(_expert_kernel, ff_chunk=256, cast_rows=128),
        grid_spec=grid_spec,
        out_shape=jax.ShapeDtypeStruct((n_blocks * MOE_TILE * XSLABS, LANES), jnp.uint32),
        compiler_params=pltpu.CompilerParams(
            dimension_semantics=("arbitrary",), vmem_limit_bytes=VMEM_LIMIT),
        name="moe_experts",
    )(block_e, block_seg, next_e, block_valid, n_used, xs, wgu, bgu, wdn, bdn)


def _combine_kernel(tab_cur, tab_nxt, ys_hbm, h_ref, tg_ref, tic_ref, rankc_ref, pp_ref, ps_ref,
                    gple_ref, wpg_ref, wpp_ref, gfin_ref, yp_ref, ys_ref, stage, sems, *, n_prompt_tiles):
    i = pl.program_id(0)
    nt = pl.num_programs(0)
    n = COMBINE_TILE * TOP_K
    slot_rows = n * XSLABS

    def gather(tab_ref, slot):
        slot_base = slot * slot_rows
        for bit in range(RUN_BITS - 1, -1, -1):
            size = (1 << bit) * XSLABS

            def issue(p, carry, bit=bit, size=size):
                dst0 = pl.multiple_of(tab_ref[0, 0, RUN_SRC + bit * N_EXPERTS + p] * XSLABS, XSLABS)
                src0 = pl.multiple_of(tab_ref[0, 0, RUN_DST + bit * N_EXPERTS + p] * XSLABS, XSLABS)
                pltpu.make_async_copy(ys_hbm.at[pl.ds(src0, size)], stage.at[pl.ds(slot_base + dst0, size)],
                                      sems.at[slot]).start(priority=bit % 2)
                return carry

            lax.fori_loop(0, tab_ref[0, 0, RUN_COUNT + bit], issue, 0)

    @pl.when(i == 0)
    def _():
        gather(tab_cur, 0)

    @pl.when(i + 1 < nt)
    def _():
        gather(tab_nxt, (i + 1) % 2)

    slot = i % 2
    base = pl.multiple_of(slot * slot_rows, slot_rows)
    pltpu.make_async_copy(ys_hbm.at[pl.ds(0, slot_rows)], stage.at[pl.ds(base, slot_rows)],
                          sems.at[slot]).wait()

    tic = tic_ref[...]
    shift = jnp.zeros(tic.shape, i32)
    for e in range(N_EXPERTS):
        shift = jnp.where(tic == e, tab_cur[0, 0, RUN_SHIFT + e], shift)
    staged = rankc_ref[...] + shift
    tg = tg_ref[...]
    staged_k = [jnp.broadcast_to(staged[:, k:k + 1], (COMBINE_TILE, LANES)) for k in range(TOP_K)]
    gate_k = [jnp.broadcast_to(tg[:, k:k + 1], (COMBINE_TILE, LANES)) for k in range(TOP_K)]
    lane = lax.broadcasted_iota(i32, (COMBINE_TILE, LANES), 1)
    groups = []
    for c in range(n // LANES):
        g = jnp.zeros((COMBINE_TILE, LANES), f32)
        for k in range(TOP_K):
            g = jnp.where(lane + c * LANES == staged_k[k], gate_k[k], g)
        groups.append(g)
    gates = jnp.concatenate(groups, axis=1)
    gates_hi = gates.astype(bf16)
    gates_lo = (gates - gates_hi.astype(f32)).astype(bf16)
    rows = _unpack_rows(stage, base, n)
    h = h_ref[...] + _dot(gates_hi, rows) + _dot(gates_lo, rows)
    gate = _sigmoid(_dot(_rms(h, gple_ref[...]).astype(bf16), wpg_ref[...]))
    p = _select_rows(i < n_prompt_tiles, pp_ref, ps_ref).astype(bf16)
    h = h + gate * _dot(p, wpp_ref[...])
    y = _rms(h, gfin_ref[...])

    @pl.when(i < n_prompt_tiles)
    def _():
        yp_ref[...] = y

    @pl.when(i >= n_prompt_tiles)
    def _():
        ys_ref[...] = y


def _combine(tile_table, ys, h1, tg, tic, rankc, pp, ps, gple, wpg, wpp, gfin):
    t = h1.shape[0]
    nt = t // COMBINE_TILE
    npt = pp.shape[0] // COMBINE_TILE
    n = COMBINE_TILE * TOP_K
    row = lambda w: pl.BlockSpec((COMBINE_TILE, w), lambda i: (i, 0))
    full = lambda shape: pl.BlockSpec(shape, lambda i: (0,) * len(shape))
    return pl.pallas_call(
        functools.partial(_combine_kernel, n_prompt_tiles=npt),
        grid=(nt,),
        in_specs=[
            pl.BlockSpec((1, 1, RUN_TABLE), lambda i: (i, 0, 0), memory_space=pltpu.SMEM),
            pl.BlockSpec((1, 1, RUN_TABLE), lambda i: (jnp.minimum(i + 1, nt - 1), 0, 0),
                         memory_space=pltpu.SMEM),
            pl.BlockSpec(memory_space=pl.ANY),
            row(D_MODEL), row(LANES), row(LANES), row(LANES),
            pl.BlockSpec((COMBINE_TILE, PLE_DIM), lambda i: (jnp.minimum(i, npt - 1), 0)),
            pl.BlockSpec((COMBINE_TILE, PLE_DIM), lambda i: (jnp.maximum(i - npt, 0), 0)),
            full((1, D_MODEL)), full((D_MODEL, D_MODEL)), full((PLE_DIM, D_MODEL)),
            full((1, D_MODEL)),
        ],
        out_specs=[
            pl.BlockSpec((COMBINE_TILE, D_MODEL), lambda i: (jnp.minimum(i, npt - 1), 0)),
            pl.BlockSpec((COMBINE_TILE, D_MODEL), lambda i: (jnp.maximum(i - npt, 0), 0)),
        ],
        out_shape=[
            jax.ShapeDtypeStruct((pp.shape[0], D_MODEL), f32),
            jax.ShapeDtypeStruct((ps.shape[0], D_MODEL), f32),
        ],
        scratch_shapes=[
            pltpu.VMEM((2 * n * XSLABS, LANES), jnp.uint32),
            pltpu.SemaphoreType.DMA((2,)),
        ],
        compiler_params=pltpu.CompilerParams(
            dimension_semantics=("arbitrary",), vmem_limit_bytes=VMEM_LIMIT),
        name="moe_combine_ple",
    )(tile_table, tile_table, ys, h1, tg, tic, rankc, pp, ps, gple, wpg, wpp, gfin)


def _routing_tables(top_i, counts, part_counts, n_blocks):
    padded = (counts + MOE_TILE - 1) // MOE_TILE * MOE_TILE
    pend = jnp.cumsum(padded)
    pstart = pend - padded
    block_rows = jnp.arange(n_blocks, dtype=i32) * MOE_TILE
    block_e = jnp.minimum(
        jnp.sum((block_rows[:, None] >= pend[None, :]).astype(i32), axis=1), N_EXPERTS - 1)
    experts = jnp.arange(N_EXPERTS, dtype=i32)
    of_block = lambda table: jnp.sum(jnp.where(block_e[:, None] == experts[None, :], table[None, :], 0), axis=1)
    block_valid = jnp.clip(of_block(pstart + counts) - block_rows, 0, MOE_TILE)
    n_used = pend[-1:] // MOE_TILE
    owns = padded > 0
    later = owns[None, :] & (experts[None, :] > experts[:, None])
    next_of_e = jnp.where(jnp.any(later, axis=1), jnp.argmax(later, axis=1), -1)
    block_seg = of_block(jnp.cumsum(owns.astype(i32)) - 1)
    next_e = of_block(next_of_e.astype(i32))

    nt = top_i.shape[1] // DISPATCH_TILE
    tile_n = part_counts[:, :, :TOKEN_TILE // DISPATCH_TILE].transpose(0, 2, 1).reshape(nt, N_EXPERTS)
    seen = jnp.cumsum(tile_n, axis=0) - tile_n
    tile_src = jnp.cumsum(tile_n, axis=1) - tile_n
    tile_dst = pstart[None, :] + seen
    bits = jnp.arange(RUN_BITS, dtype=i32)[None, :, None]
    n3 = tile_n[:, None, :]
    has_piece = ((n3 >> bits) & 1) == 1
    covered = (n3 >> (bits + 1)) << (bits + 1)
    earlier = (experts[:, None] < experts[None, :])[None, None]
    place = jnp.sum((has_piece[..., None] & earlier).astype(i32), axis=2)
    listed = has_piece[..., None] & (place[..., None] == experts[None, None, None, :])
    compact = lambda v: jnp.sum(jnp.where(listed, v[..., None], 0), axis=2)
    piece_src = compact(tile_src[:, None, :] + covered)
    piece_dst = compact(tile_dst[:, None, :] + covered)
    piece_cnt = jnp.sum(has_piece.astype(i32), axis=2)
    tile_table = jnp.concatenate(
        [tile_src - seen, jnp.pad(piece_cnt, ((0, 0), (0, N_EXPERTS - RUN_BITS))),
         piece_src.reshape(nt, -1), piece_dst.reshape(nt, -1)], axis=1)
    tables = (block_e, block_seg, next_e, block_valid, n_used)
    return (tile_table.reshape(nt, 1, RUN_TABLE).astype(i32),) + tuple(
        t.astype(i32) for t in tables)


def kernel(x_prompt, x_sample, p_prompt, p_sample, state_gla, state_hgrn, g_mix, w_in, w_gk2, b_gk2,
           gn_gla, lb_logits, gn_hgrn, w_out, g_ffn, w_router, b_router, w_gu, b_gu, w_dn, b_dn,
           g_ple, w_ple_gate, w_ple_proj, g_final):
    batch, seq, _ = x_prompt.shape
    dec_batch, dec_seq, _ = x_sample.shape
    tp, ts = batch * seq, dec_batch * dec_seq
    xp = x_prompt.reshape(tp, D_MODEL)
    xs = x_sample.reshape(ts, D_MODEL)

    wgk = jnp.concatenate(
        [w_gk2[0], jnp.zeros((R1_PAD - GLA_LOWRANK, GLA_KDIM), f32)], axis=0).astype(bf16)
    wr = w_router[0].T.astype(bf16)
    br = b_router[0].reshape(N_EXPERTS, 1)

    z = _in_proj(xp, xs, g_mix[0].reshape(1, D_MODEL), w_in[0].T)

    mixer_params = (wgk, b_gk2[0].reshape(1, GLA_KDIM), gn_gla[0].reshape(1, GLA_DV),
                    lb_logits[0:2], gn_hgrn[0].reshape(1, HGRN_DV))
    o_p, new_gla_p, new_hgrn_p = _mixer_prompt(
        z, *mixer_params, batch=batch, seq=seq, rows_per_step=512, block_rows=256)
    z_s = jnp.pad(z[tp:].reshape(dec_batch, dec_seq, Z_W), ((0, 0), (0, SAMPLE_PAD_T - dec_seq), (0, 0)))
    o_s, new_gla_s, new_hgrn_s = _mixer_sample(
        z_s.reshape(dec_batch * SAMPLE_PAD_T, Z_W), state_gla, state_hgrn, *mixer_params,
        steps=dec_seq, batch_block=8)
    o_s = o_s.reshape(dec_batch, SAMPLE_PAD_T, D_MODEL)[:, :dec_seq].reshape(ts, D_MODEL).astype(bf16)

    h1, xn, top_i, top_g, rank, top_ic, rank_c, counts, part_counts = _out_proj(
        xp, xs, o_p, o_s, w_out[0].astype(bf16), g_ffn[0].reshape(1, D_MODEL), wr, br)

    t = tp + ts
    n_blocks = -(-(t * TOP_K) // MOE_TILE) + N_EXPERTS
    tile_table, block_e, block_seg, next_e, block_valid, n_used = _routing_tables(
        top_i, counts[:, 0], part_counts, n_blocks)
    x_sorted = _dispatch(block_valid, tile_table, top_i, rank, xn)
    y_sorted = _experts(block_e, block_seg, next_e, block_valid, n_used, x_sorted,
                        w_gu[0], b_gu[0].reshape(N_EXPERTS, 1, 2 * D_FF),
                        w_dn[0], b_dn[0].reshape(N_EXPERTS, 1, D_MODEL))

    y_p, y_s = _combine(tile_table, y_sorted, h1, top_g, top_ic, rank_c, p_prompt[0].reshape(tp, PLE_DIM),
                        p_sample[0].reshape(ts, PLE_DIM), g_ple[0].reshape(1, D_MODEL),
                        w_ple_gate[0].astype(bf16), w_ple_proj[0].astype(bf16), g_final.reshape(1, D_MODEL))

    return (y_p.reshape(batch, seq, D_MODEL), y_s.reshape(dec_batch, dec_seq, D_MODEL),
            new_gla_p, new_hgrn_p, new_gla_s, new_hgrn_s)
```

```python
import functools

import jax
import jax.numpy as jnp
from jax import lax
from jax.experimental import pallas as pl
from jax.experimental.pallas import tpu as pltpu

f32 = jnp.float32
bf16 = jnp.bfloat16
i32 = jnp.int32

D_MODEL = 1024
GLA_HEADS = 4
GLA_DK = 64
GLA_DV = 128
GLA_KDIM = GLA_HEADS * GLA_DK
GLA_WIDTH = GLA_HEADS * GLA_DV
GLA_LOWRANK = 16
GLA_NORMALIZER = 16.0
HGRN_HEADS = 4
HGRN_DK = 128
HGRN_DV = 128
HGRN_FDIM = HGRN_HEADS * HGRN_DK
HGRN_WIDTH = HGRN_HEADS * HGRN_DV
CHUNK = 64
N_EXPERTS = 32
TOP_K = 4
D_FF = 1024
SWIGLU_LIMIT = 7.0
SWIGLU_ALPHA = 1.702
PLE_DIM = 256
EPS = 1e-6

LANES = 128
SAMPLE_PAD_T = 8
XSLABS = D_MODEL // (2 * LANES)
ROUTE_ROWS = 8
R1_PAD = LANES
Z_Q1 = 0
Z_K1 = Z_Q1 + GLA_KDIM
Z_V1 = Z_K1 + GLA_KDIM
Z_G1 = Z_V1 + GLA_WIDTH
Z_R1 = Z_G1 + GLA_WIDTH
Z_Q2 = Z_R1 + R1_PAD
Z_F2 = Z_Q2 + HGRN_FDIM
Z_I2 = Z_F2 + HGRN_FDIM
Z_G2 = Z_I2 + HGRN_WIDTH
Z_W = Z_G2 + HGRN_WIDTH

TOKEN_TILE = 512
MOE_TILE = 512
DISPATCH_TILE = 256
COMBINE_TILE = DISPATCH_TILE
RUN_BITS = DISPATCH_TILE.bit_length()
RUN_SHIFT = 0
RUN_COUNT = RUN_SHIFT + N_EXPERTS
RUN_SRC = RUN_COUNT + N_EXPERTS
RUN_DST = RUN_SRC + RUN_BITS * N_EXPERTS
RUN_TABLE = RUN_DST + RUN_BITS * N_EXPERTS
VMEM_LIMIT = 56 * 1024 * 1024


def _rms(x, g):
    return x * lax.rsqrt(jnp.mean(x * x, axis=-1, keepdims=True) + EPS) * g


def _sigmoid(x):
    return 0.5 * jnp.tanh(0.5 * x) + 0.5


def _dot(a, b):
    return jnp.dot(a, b, preferred_element_type=f32)


def _dot_nt(a, b):
    return lax.dot_general(a, b, (((1,), (1,)), ((), ())), preferred_element_type=f32)


def _dot_tn(a, b):
    return lax.dot_general(a, b, (((0,), (0,)), ((), ())), preferred_element_type=f32)


def _cumsum_rows(tri, x):
    hi = x.astype(bf16)
    r1 = x - hi.astype(f32)
    mid = r1.astype(bf16)
    lo = (r1 - mid.astype(f32)).astype(bf16)
    return _dot(tri, hi) + _dot(tri, mid) + _dot(tri, lo)


def _select_rows(is_prompt, a_ref, b_ref):
    return jnp.where(is_prompt, a_ref[...], b_ref[...])


def _inproj_kernel(xp_ref, xs_ref, g_ref, w_hbm, z_ref, w_f32, w_bf, sem,
                   *, n_prompt_tiles, col_chunk):
    @pl.when(pl.program_id(0) == 0)
    def _():
        copy = pltpu.make_async_copy(w_hbm, w_f32, sem)
        copy.start()
        copy.wait()
        for c0 in range(0, Z_W, LANES):
            if c0 < Z_R1:
                cols = w_f32[c0:c0 + LANES, :]
            elif c0 == Z_R1:
                cols = jnp.concatenate(
                    [w_f32[Z_R1:Z_R1 + GLA_LOWRANK, :],
                     jnp.zeros((R1_PAD - GLA_LOWRANK, D_MODEL), f32)], axis=0)
            else:
                s0 = c0 - (R1_PAD - GLA_LOWRANK)
                cols = w_f32[s0:s0 + LANES, :]
            w_bf[:, c0:c0 + LANES] = cols.T.astype(bf16)

    is_prompt = pl.program_id(0) < n_prompt_tiles
    x = _select_rows(is_prompt, xp_ref, xs_ref)
    a = _rms(x, g_ref[...]).astype(bf16)
    for c0 in range(0, Z_W, col_chunk):
        c1 = min(c0 + col_chunk, Z_W)
        z_ref[:, c0:c1] = _dot(a, w_bf[:, c0:c1]).astype(bf16)


def _in_proj(xp, xs, g, w):
    tp, ts = xp.shape[0], xs.shape[0]
    npt = tp // TOKEN_TILE
    nt = npt + ts // TOKEN_TILE
    return pl.pallas_call(
        functools.partial(_inproj_kernel, n_prompt_tiles=npt, col_chunk=512),
        grid=(nt,),
        in_specs=[
            pl.BlockSpec((TOKEN_TILE, D_MODEL), lambda i: (jnp.minimum(i, npt - 1), 0)),
            pl.BlockSpec((TOKEN_TILE, D_MODEL), lambda i: (jnp.maximum(i - npt, 0), 0)),
            pl.BlockSpec((1, D_MODEL), lambda i: (0, 0)),
            pl.BlockSpec(memory_space=pl.ANY),
        ],
        out_specs=pl.BlockSpec((TOKEN_TILE, Z_W), lambda i: (i, 0)),
        out_shape=jax.ShapeDtypeStruct((tp + ts, Z_W), bf16),
        scratch_shapes=[
            pltpu.VMEM(w.shape, f32),
            pltpu.VMEM((D_MODEL, Z_W), bf16),
            pltpu.SemaphoreType.DMA(()),
        ],
        compiler_params=pltpu.CompilerParams(
            dimension_semantics=("arbitrary",), vmem_limit_bytes=VMEM_LIMIT),
        name="in_proj",
    )(xp, xs, g, w)


def _block_consts(r, c):
    shift = c.bit_length() - 1
    row = lax.broadcasted_iota(i32, (r, r), 0)
    col = lax.broadcasted_iota(i32, (r, r), 1)
    causal = (lax.shift_right_logical(row, shift) == lax.shift_right_logical(col, shift)) & (row >= col)
    return causal.astype(bf16), causal


def _subchunk_row(x, c, idx):
    parts = [jnp.broadcast_to(x[j * c + idx:j * c + idx + 1, :], (c, x.shape[1]))
             for j in range(x.shape[0] // c)]
    return jnp.concatenate(parts, axis=0)


def _gated_block(q, k, v, log_a, states, *, c, scale, ref_row, tri, causal, heads_per_group, scan):
    r = q.shape[0]
    nsub = r // c
    groups = q.shape[1] // LANES
    b = _cumsum_rows(tri, log_a)
    b_ref = _subchunk_row(b, c, ref_row)
    b_last = _subchunk_row(b, c, c - 1)
    qs = q * scale
    q_r = qs * jnp.exp(b - b_ref)
    k_r32 = k * jnp.exp(b_ref - b)
    k_r = k_r32.astype(bf16)
    k_end = k_r32 * jnp.exp(b_last - b_ref)
    q_b = q_r * jnp.exp(b_ref)
    decay = jnp.exp(b_last)
    vb = v.astype(bf16)
    lane = lax.broadcasted_iota(i32, (r, LANES), 1)
    width = LANES // heads_per_group
    outs, new_states = [], []
    for g in range(groups):
        gl = slice(g * LANES, (g + 1) * LANES)
        masks = [None] if heads_per_group == 1 else [
            (lane >= j * width) & (lane < (j + 1) * width) for j in range(heads_per_group)]
        pick = lambda x, m: x if m is None else jnp.where(m, x, 0.0)
        q_r_h = [pick(q_r[:, gl], m).astype(bf16) for m in masks]
        q_b_h = [pick(q_b[:, gl], m).astype(bf16) for m in masks]
        k_end_h = [pick(k_end[:, gl], m).astype(bf16) for m in masks]
        v_h = [vb[:, (g * heads_per_group + j) * LANES:(g * heads_per_group + j + 1) * LANES]
               for j in range(heads_per_group)]
        o_intra = []
        for j in range(heads_per_group):
            att = _dot_nt(q_r_h[j], k_r[:, gl])
            att = jnp.where(causal, att, 0.0).astype(bf16)
            o_intra.append(_dot(att, v_h[j]))
        upd = []
        for s in range(nsub):
            rs = slice(s * c, (s + 1) * c)
            u = _dot_tn(v_h[0][rs], k_end_h[0][rs])
            for j in range(1, heads_per_group):
                u = u + _dot_tn(v_h[j][rs], k_end_h[j][rs])
            upd.append(u)
        o_inter = [[] for _ in range(heads_per_group)]
        st = states[g] if scan else None
        group_states = []
        for s in range(nsub):
            rs = slice(s * c, (s + 1) * c)
            if not scan:
                st = states[s][g]
            st_b = st.astype(bf16)
            for j in range(heads_per_group):
                o_inter[j].append(_dot_nt(q_b_h[j][rs], st_b))
            st = st * decay[s * c:s * c + 1, gl] + upd[s]
            if not scan:
                group_states.append(st)
        for j in range(heads_per_group):
            outs.append(o_intra[j] + jnp.concatenate(o_inter[j], axis=0))
        new_states.append(st if scan else group_states)
    if not scan:
        new_states = [[new_states[g][s] for g in range(groups)] for s in range(nsub)]
    return outs, new_states


def _head_norm(o, g):
    return o * lax.rsqrt(jnp.mean(o * o, axis=-1, keepdims=True) + EPS) * g


def _mixer_block(zb, st_gla, st_hgrn, wgk, bgk, gng, lb, gnh, *, c, ref_row, tri, causal, valid, scan):
    z = zb.astype(f32)
    gk = _dot(zb[:, Z_R1:Z_R1 + R1_PAD], wgk) + bgk
    log_a = (jnp.minimum(gk, 0.0) - jnp.log(1.0 + jnp.exp(-jnp.abs(gk)))) / GLA_NORMALIZER
    k1 = z[:, Z_K1:Z_K1 + GLA_KDIM]
    if valid is not None:
        log_a = jnp.where(valid, log_a, 0.0)
        k1 = jnp.where(valid, k1, 0.0)
    o1, st_gla = _gated_block(
        z[:, Z_Q1:Z_Q1 + GLA_KDIM], k1, zb[:, Z_V1:Z_V1 + GLA_WIDTH], log_a, st_gla,
        c=c, scale=GLA_DK ** -0.5, ref_row=ref_row, tri=tri, causal=causal,
        heads_per_group=LANES // GLA_DK, scan=scan)
    forget = lb + (1.0 - lb) * _sigmoid(z[:, Z_F2:Z_F2 + HGRN_FDIM])
    k2 = 1.0 - forget
    log_f = jnp.log(forget)
    if valid is not None:
        log_f = jnp.where(valid, log_f, 0.0)
        k2 = jnp.where(valid, k2, 0.0)
    q2 = z[:, Z_Q2:Z_Q2 + HGRN_FDIM]
    i2 = z[:, Z_I2:Z_I2 + HGRN_WIDTH]
    o2, st_hgrn = _gated_block(
        q2 * _sigmoid(q2), k2, i2 * _sigmoid(i2), log_f, st_hgrn,
        c=c, scale=1.0, ref_row=ref_row, tri=tri, causal=causal, heads_per_group=1, scan=scan)
    cols = []
    for h in range(GLA_HEADS):
        g1 = z[:, Z_G1 + h * GLA_DV:Z_G1 + (h + 1) * GLA_DV]
        cols.append(_head_norm(o1[h], gng) * (g1 * _sigmoid(g1)))
    for h in range(HGRN_HEADS):
        g2 = z[:, Z_G2 + h * HGRN_DV:Z_G2 + (h + 1) * HGRN_DV]
        cols.append(_head_norm(o2[h], gnh) * _sigmoid(g2))
    return jnp.concatenate(cols, axis=1), st_gla, st_hgrn


def _lower_bound(lbl_ref):
    l = lbl_ref[...]
    m = jnp.max(l, axis=0, keepdims=True)
    e = jnp.exp(l - m)
    return e[0:1, :] / jnp.sum(e, axis=0, keepdims=True)


GLA_GROUPS = GLA_KDIM // LANES


def _gla_state_t(s_ref_heads):
    return s_ref_heads.reshape(LANES, GLA_DV).T


def _gla_state_from_t(st):
    return st.T.reshape(LANES // GLA_DK, GLA_DK, GLA_DV)


def _mixer_prompt_kernel(z_ref, wgk_ref, bgk_ref, gng_ref, lbl_ref, gnh_ref,
                         o_ref, sg_out, sh_out, sg_t, sh_t, *, block_rows):
    j = pl.program_id(1)

    @pl.when(j == 0)
    def _():
        sg_t[...] = jnp.zeros_like(sg_t)
        sh_t[...] = jnp.zeros_like(sh_t)

    tri, causal = _block_consts(block_rows, CHUNK)
    lb = _lower_bound(lbl_ref)
    wgk, bgk, gng, gnh = wgk_ref[...], bgk_ref[...], gng_ref[...], gnh_ref[...]

    def body(blk, carry):
        r0 = pl.multiple_of(blk * block_rows, block_rows)
        o, st_g, st_h = _mixer_block(
            z_ref[pl.ds(r0, block_rows), :], [sg_t[g] for g in range(GLA_GROUPS)],
            [sh_t[h] for h in range(HGRN_HEADS)], wgk, bgk, gng, lb, gnh,
            c=CHUNK, ref_row=CHUNK // 2, tri=tri, causal=causal, valid=None, scan=True)
        o_ref[pl.ds(r0, block_rows), :] = o.astype(bf16)
        for g in range(GLA_GROUPS):
            sg_t[g] = st_g[g]
        for h in range(HGRN_HEADS):
            sh_t[h] = st_h[h]
        return carry

    lax.fori_loop(0, z_ref.shape[0] // block_rows, body, 0)

    @pl.when(j == pl.num_programs(1) - 1)
    def _():
        per = LANES // GLA_DK
        for g in range(GLA_GROUPS):
            sg_out[0, 0, g * per:(g + 1) * per] = _gla_state_from_t(sg_t[g])
        for h in range(HGRN_HEADS):
            sh_out[0, 0, h] = sh_t[h].T


def _mixer_prompt(z, wgk, bgk, gng, lbl, gnh, *, batch, seq, rows_per_step, block_rows):
    nj = seq // rows_per_step
    full = lambda shape: pl.BlockSpec(shape, lambda b, j: (0,) * len(shape))
    return pl.pallas_call(
        functools.partial(_mixer_prompt_kernel, block_rows=block_rows),
        grid=(batch, nj),
        in_specs=[
            pl.BlockSpec((rows_per_step, Z_W), lambda b, j: (b * nj + j, 0)),
            full((R1_PAD, GLA_KDIM)), full((1, GLA_KDIM)), full((1, GLA_DV)),
            full((2, HGRN_FDIM)), full((1, HGRN_DV)),
        ],
        out_specs=[
            pl.BlockSpec((rows_per_step, D_MODEL), lambda b, j: (b * nj + j, 0)),
            pl.BlockSpec((1, 1, GLA_HEADS, GLA_DK, GLA_DV), lambda b, j: (0, b, 0, 0, 0)),
            pl.BlockSpec((1, 1, HGRN_HEADS, HGRN_DK, HGRN_DV), lambda b, j: (0, b, 0, 0, 0)),
        ],
        out_shape=[
            jax.ShapeDtypeStruct((batch * seq, D_MODEL), bf16),
            jax.ShapeDtypeStruct((1, batch, GLA_HEADS, GLA_DK, GLA_DV), f32),
            jax.ShapeDtypeStruct((1, batch, HGRN_HEADS, HGRN_DK, HGRN_DV), f32),
        ],
        scratch_shapes=[
            pltpu.VMEM((GLA_GROUPS, GLA_DV, LANES), f32),
            pltpu.VMEM((HGRN_HEADS, HGRN_DV, HGRN_DK), f32),
        ],
        compiler_params=pltpu.CompilerParams(
            dimension_semantics=("arbitrary", "arbitrary"), vmem_limit_bytes=VMEM_LIMIT),
        name="mixer_prompt",
    )(z, wgk, bgk, gng, lbl, gnh)


def _mixer_sample_kernel(z_ref, sg_in, sh_in, wgk_ref, bgk_ref, gng_ref, lbl_ref, gnh_ref,
                         o_ref, sg_out, sh_out, *, batch_block, steps):
    rows = batch_block * SAMPLE_PAD_T
    tri, causal = _block_consts(rows, SAMPLE_PAD_T)
    step = lax.broadcasted_iota(i32, (rows, 1), 0) & (SAMPLE_PAD_T - 1)
    valid = step < steps
    lb = _lower_bound(lbl_ref)
    per = LANES // GLA_DK
    st_g = [[_gla_state_t(sg_in[0, s, g * per:(g + 1) * per]) for g in range(GLA_GROUPS)]
            for s in range(batch_block)]
    st_h = [[sh_in[0, s, h].T for h in range(HGRN_HEADS)] for s in range(batch_block)]
    packed = batch_block * steps
    pad_shift = SAMPLE_PAD_T.bit_length() - 1
    r = lax.broadcasted_iota(i32, (rows, packed), 0)
    col = lax.broadcasted_iota(i32, (rows, packed), 1)
    t = r & (SAMPLE_PAD_T - 1)
    spread = ((t < steps) & (col == lax.shift_right_logical(r, pad_shift) * steps + t)).astype(bf16)
    r = lax.broadcasted_iota(i32, (packed, rows), 1)
    col = lax.broadcasted_iota(i32, (packed, rows), 0)
    t = r & (SAMPLE_PAD_T - 1)
    gather = ((t < steps) & (col == lax.shift_right_logical(r, pad_shift) * steps + t)).astype(bf16)
    z = _dot(spread, z_ref[...]).astype(bf16)
    o, st_g, st_h = _mixer_block(
        z, st_g, st_h, wgk_ref[...], bgk_ref[...], gng_ref[...], lb, gnh_ref[...],
        c=SAMPLE_PAD_T, ref_row=steps // 2, tri=tri, causal=causal, valid=valid, scan=False)
    o_ref[...] = _dot(gather, o.astype(bf16)).astype(bf16)
    for s in range(batch_block):
        for g in range(GLA_GROUPS):
            sg_out[0, s, g * per:(g + 1) * per] = _gla_state_from_t(st_g[s][g])
        for h in range(HGRN_HEADS):
            sh_out[0, s, h] = st_h[s][h].T


def _mixer_sample(z, state_gla, state_hgrn, wgk, bgk, gng, lbl, gnh, *, first_row, steps, batch_block):
    batch = state_gla.shape[1]
    rows = batch_block * steps
    first_block = first_row // rows
    full = lambda shape: pl.BlockSpec(shape, lambda i: (0,) * len(shape))
    sg_spec = pl.BlockSpec((1, batch_block, GLA_HEADS, GLA_DK, GLA_DV), lambda i: (0, i, 0, 0, 0))
    sh_spec = pl.BlockSpec((1, batch_block, HGRN_HEADS, HGRN_DK, HGRN_DV), lambda i: (0, i, 0, 0, 0))
    return pl.pallas_call(
        functools.partial(_mixer_sample_kernel, batch_block=batch_block, steps=steps),
        grid=(batch // batch_block,),
        in_specs=[
            pl.BlockSpec((rows, Z_W), lambda i: (first_block + i, 0)),
            sg_spec, sh_spec,
            full((R1_PAD, GLA_KDIM)), full((1, GLA_KDIM)), full((1, GLA_DV)),
            full((2, HGRN_FDIM)), full((1, HGRN_DV)),
        ],
        out_specs=[pl.BlockSpec((rows, D_MODEL), lambda i: (i, 0)), sg_spec, sh_spec],
        out_shape=[
            jax.ShapeDtypeStruct((batch * steps, D_MODEL), bf16),
            jax.ShapeDtypeStruct(state_gla.shape, f32),
            jax.ShapeDtypeStruct(state_hgrn.shape, f32),
        ],
        compiler_params=pltpu.CompilerParams(
            dimension_semantics=("arbitrary",), vmem_limit_bytes=VMEM_LIMIT),
        name="mixer_sample",
    )(z, state_gla, state_hgrn, wgk, bgk, gng, lbl, gnh)


HIGH_HALF = 0xFFFF0000


def _pack_rows(x):
    bits = lax.bitcast_convert_type(x, jnp.uint32)
    return [(bits[:, (2 * s + 1) * LANES:(2 * s + 2) * LANES] & jnp.uint32(HIGH_HALF))
            | (bits[:, 2 * s * LANES:(2 * s + 1) * LANES] >> 16) for s in range(x.shape[1] // (2 * LANES))]


def _unpack_rows(ref, start, rows):
    cols = []
    for s in range(XSLABS):
        w = ref[pl.ds(start + s, rows, stride=XSLABS), :]
        cols.append(lax.bitcast_convert_type(w << 16, f32).astype(bf16))
        cols.append(lax.bitcast_convert_type(w & jnp.uint32(HIGH_HALF), f32).astype(bf16))
    return jnp.concatenate(cols, axis=1)


def _outproj_kernel(xp_ref, xs_ref, op_ref, os_ref, wo_ref, g_ref, wr_ref, br_ref,
                    h_ref, xn_ref, ti_ref, tg_ref, rank_ref, tic_ref, rankc_ref, cnt_ref, part_ref, run_cnt,
                    *, n_prompt_tiles):
    i = pl.program_id(0)

    @pl.when(i == 0)
    def _():
        run_cnt[...] = jnp.zeros_like(run_cnt)

    x = _select_rows(i < n_prompt_tiles, xp_ref, xs_ref)
    o = _select_rows(i < n_prompt_tiles, op_ref, os_ref)
    h = x + _dot(o, wo_ref[...])
    h_ref[...] = h
    xn = _rms(h, g_ref[...])
    xn_ref[...] = xn.astype(bf16)
    l = _dot_nt(wr_ref[...], xn.astype(bf16)) + br_ref[...]
    expert = lax.broadcasted_iota(i32, l.shape, 0)
    neg = jnp.float32(-jnp.inf)
    vals, idxs = [], []
    for _ in range(TOP_K):
        m = jnp.max(l, axis=0, keepdims=True)
        idx = jnp.min(jnp.where(l == m, expert, N_EXPERTS), axis=0, keepdims=True)
        vals.append(m)
        idxs.append(idx)
        l = jnp.where(expert == idx, neg, l)
    es = [jnp.exp(v - vals[0]) for v in vals]
    denom = es[0] + es[1] + es[2] + es[3]

    onehot = jnp.zeros(l.shape, f32)
    for k in range(TOP_K):
        onehot = onehot + (expert == idxs[k]).astype(f32)
    n = l.shape[1]
    earlier = (lax.broadcasted_iota(i32, (n, n), 0) < lax.broadcasted_iota(i32, (n, n), 1)).astype(bf16)
    ahead = _dot(onehot.astype(bf16), earlier) + run_cnt[:, 0:1]
    run_cnt[...] = run_cnt[...] + jnp.sum(onehot, axis=1, keepdims=True)
    cnt_ref[...] = run_cnt[...].astype(i32)
    lane = lax.broadcasted_iota(i32, (N_EXPERTS, LANES), 1)
    parts = jnp.zeros((N_EXPERTS, LANES), f32)
    for p in range(n // DISPATCH_TILE):
        c = jnp.sum(onehot[:, p * DISPATCH_TILE:(p + 1) * DISPATCH_TILE], axis=1, keepdims=True)
        parts = jnp.where(lane == p, c, parts)
    part_ref[0] = parts.astype(i32)

    ranks = [jnp.sum(jnp.where(expert == idxs[k], ahead, 0.0), axis=0, keepdims=True).astype(i32)
             for k in range(TOP_K)]
    pad_i = jnp.zeros((ROUTE_ROWS - TOP_K, n), i32)
    ti_ref[...] = jnp.concatenate(idxs + [pad_i], axis=0)
    rank_ref[...] = jnp.concatenate(ranks + [pad_i], axis=0)
    gate_rows = jnp.concatenate([e / denom for e in es] + [jnp.zeros((LANES - TOP_K, n), f32)], axis=0)
    tg_ref[...] = gate_rows.T
    pad_wide = jnp.zeros((LANES - TOP_K, n), i32)
    tic_ref[...] = jnp.concatenate(idxs + [pad_wide], axis=0).T
    rankc_ref[...] = jnp.concatenate(ranks + [pad_wide], axis=0).T


def _out_proj(xp, xs, o_p, o_s, wo, g, wr, br):
    tp, ts = xp.shape[0], xs.shape[0]
    npt = tp // TOKEN_TILE
    nt = npt + ts // TOKEN_TILE
    t = tp + ts
    row = lambda w: pl.BlockSpec((TOKEN_TILE, w), lambda i: (i, 0))
    route = pl.BlockSpec((ROUTE_ROWS, TOKEN_TILE), lambda i: (0, i))
    full = lambda shape: pl.BlockSpec(shape, lambda i: (0,) * len(shape))
    prompt_rows = pl.BlockSpec((TOKEN_TILE, D_MODEL), lambda i: (jnp.minimum(i, npt - 1), 0))
    sample_rows = pl.BlockSpec((TOKEN_TILE, D_MODEL), lambda i: (jnp.maximum(i - npt, 0), 0))
    return pl.pallas_call(
        functools.partial(_outproj_kernel, n_prompt_tiles=npt),
        grid=(nt,),
        in_specs=[
            prompt_rows, sample_rows, prompt_rows, sample_rows,
            full((D_MODEL, D_MODEL)), full((1, D_MODEL)),
            full((N_EXPERTS, D_MODEL)), full((N_EXPERTS, 1)),
        ],
        out_specs=[row(D_MODEL), row(D_MODEL),
                   route, row(LANES), route, row(LANES), row(LANES), full((N_EXPERTS, LANES)),
                   pl.BlockSpec((1, N_EXPERTS, LANES), lambda i: (i, 0, 0))],
        out_shape=[
            jax.ShapeDtypeStruct((t, D_MODEL), f32),
            jax.ShapeDtypeStruct((t, D_MODEL), bf16),
            jax.ShapeDtypeStruct((ROUTE_ROWS, t), i32),
            jax.ShapeDtypeStruct((t, LANES), f32),
            jax.ShapeDtypeStruct((ROUTE_ROWS, t), i32),
            jax.ShapeDtypeStruct((t, LANES), i32),
            jax.ShapeDtypeStruct((t, LANES), i32),
            jax.ShapeDtypeStruct((N_EXPERTS, LANES), i32),
            jax.ShapeDtypeStruct((nt, N_EXPERTS, LANES), i32),
        ],
        scratch_shapes=[pltpu.VMEM((N_EXPERTS, LANES), f32)],
        compiler_params=pltpu.CompilerParams(
            dimension_semantics=("arbitrary",), vmem_limit_bytes=VMEM_LIMIT),
        name="out_proj_router",
    )(xp, xs, o_p, o_s, wo, g, wr, br)


def _dispatch_kernel(bv_ref, nz_ref, tab_ref, ti_ref, rank_ref, xn_ref, xs_hbm, zeros, stage, sems, zsem,
                     *, n_blocks):
    @pl.when(pl.program_id(0) == 0)
    def _():
        zeros[...] = jnp.zeros_like(zeros)

        def fill(b, carry):
            @pl.when(bv_ref[b] < MOE_TILE)
            def _():
                r0 = pl.multiple_of(b * (MOE_TILE * XSLABS), MOE_TILE * XSLABS)
                pltpu.make_async_copy(zeros, xs_hbm.at[pl.ds(r0, MOE_TILE * XSLABS)], zsem).start()
            return carry

        def drain(b, carry):
            pltpu.make_async_copy(zeros, xs_hbm.at[pl.ds(0, MOE_TILE * XSLABS)], zsem).wait()
            return carry

        lax.fori_loop(0, n_blocks, fill, 0)
        lax.fori_loop(0, nz_ref[0], drain, 0)

    i = pl.program_id(0)
    nt = pl.num_programs(0)
    n = DISPATCH_TILE * TOP_K
    slot_rows = n * XSLABS
    slot = i % 2
    base = pl.multiple_of(slot * slot_rows, slot_rows)

    def wait_rows(s):
        pltpu.make_async_copy(stage.at[pl.ds(0, slot_rows)], xs_hbm.at[pl.ds(0, slot_rows)], sems.at[s]).wait()

    ti = ti_ref[...]
    shift = jnp.zeros(ti.shape, i32)
    for e in range(N_EXPERTS):
        shift = jnp.where(ti == e, tab_ref[0, 0, RUN_SHIFT + e], shift)
    staged = rank_ref[...] + shift
    row = lax.broadcasted_iota(i32, (n, DISPATCH_TILE), 0)
    pick = row == staged[0:1, :]
    for k in range(1, TOP_K):
        pick = pick | (row == staged[k:k + 1, :])
    rows_sorted = _dot(pick.astype(bf16), xn_ref[...])
    for s, words in enumerate(_pack_rows(rows_sorted)):
        stage[pl.ds(base + s, n, stride=XSLABS), :] = words

    for bit in range(RUN_BITS - 1, -1, -1):
        size = (1 << bit) * XSLABS

        def issue(p, carry, bit=bit, size=size):
            src0 = pl.multiple_of(tab_ref[0, 0, RUN_SRC + bit * N_EXPERTS + p] * XSLABS, XSLABS)
            dst0 = pl.multiple_of(tab_ref[0, 0, RUN_DST + bit * N_EXPERTS + p] * XSLABS, XSLABS)
            pltpu.make_async_copy(stage.at[pl.ds(base + src0, size)], xs_hbm.at[pl.ds(dst0, size)],
                                  sems.at[slot]).start(priority=bit % 2)
            return carry

        lax.fori_loop(0, tab_ref[0, 0, RUN_COUNT + bit], issue, 0)

    @pl.when(i > 0)
    def _():
        wait_rows(1 - slot)

    @pl.when(i == nt - 1)
    def _():
        wait_rows(slot)


def _dispatch(block_valid, tile_table, top_i, rank, xn):
    t = xn.shape[0]
    nt = t // DISPATCH_TILE
    n_blocks = block_valid.shape[0]
    n_zero = jnp.sum((block_valid < MOE_TILE).astype(i32)).reshape(1)
    route = pl.BlockSpec((ROUTE_ROWS, DISPATCH_TILE), lambda i, bv, nz: (0, i))
    grid_spec = pltpu.PrefetchScalarGridSpec(
        num_scalar_prefetch=2,
        grid=(nt,),
        in_specs=[
            pl.BlockSpec((1, 1, RUN_TABLE), lambda i, bv, nz: (i, 0, 0), memory_space=pltpu.SMEM),
            route, route,
            pl.BlockSpec((DISPATCH_TILE, D_MODEL), lambda i, bv, nz: (i, 0)),
        ],
        out_specs=pl.BlockSpec(memory_space=pl.ANY),
        scratch_shapes=[
            pltpu.VMEM((MOE_TILE * XSLABS, LANES), jnp.uint32),
            pltpu.VMEM((2 * DISPATCH_TILE * TOP_K * XSLABS, LANES), jnp.uint32),
            pltpu.SemaphoreType.DMA((2,)),
            pltpu.SemaphoreType.DMA(()),
        ],
    )
    return pl.pallas_call(
        functools.partial(_dispatch_kernel, n_blocks=n_blocks),
        grid_spec=grid_spec,
        out_shape=jax.ShapeDtypeStruct((n_blocks * MOE_TILE * XSLABS, LANES), jnp.uint32),
        compiler_params=pltpu.CompilerParams(
            dimension_semantics=("arbitrary",), vmem_limit_bytes=VMEM_LIMIT),
        name="moe_dispatch",
    )(block_valid, n_zero, tile_table, top_i, rank, xn)


def _expert_kernel(be_ref, seg_ref, nxt_ref, bv_ref, nb_ref, x_ref, wgu_hbm, bgu_ref, wdn_hbm, bdn_ref, y_ref,
                   wgu_f32, wdn_f32, wgu_bf, wdn_bf, sems, *, ff_chunk, cast_rows):
    i = pl.program_id(0)

    @pl.when(i >= nb_ref[0])
    def _():
        y_ref[...] = jnp.zeros_like(y_ref)

    def fetch(e, slot):
        return (pltpu.make_async_copy(wgu_hbm.at[e], wgu_f32.at[slot], sems.at[slot, 0]),
                pltpu.make_async_copy(wdn_hbm.at[e], wdn_f32.at[slot], sems.at[slot, 1]))

    @pl.when((i < nb_ref[0]) & ((i == 0) | (be_ref[i] != be_ref[jnp.maximum(i - 1, 0)])))
    def _():
        slot = seg_ref[i] % 2

        @pl.when(i == 0)
        def _():
            for c in fetch(be_ref[0], 0):
                c.start()

        for c in fetch(be_ref[i], slot):
            c.wait()

        @pl.when(nxt_ref[i] >= 0)
        def _():
            for c in fetch(nxt_ref[i], 1 - slot):
                c.start()

        def cast(j, carry):
            r0 = pl.multiple_of(j * cast_rows, cast_rows)
            wgu_bf[pl.ds(r0, cast_rows), :] = wgu_f32[slot, pl.ds(r0, cast_rows), :].astype(bf16)
            wdn_bf[pl.ds(r0, cast_rows), :] = wdn_f32[slot, pl.ds(r0, cast_rows), :].astype(bf16)
            return carry
        lax.fori_loop(0, D_MODEL // cast_rows, cast, 0)

    def mlp(rows):
        xb = _unpack_rows(x_ref, 0, rows)
        hid = []
        for c0 in range(0, D_FF, ff_chunk):
            gate = _dot(xb, wgu_bf[:, c0:c0 + ff_chunk]) + bgu_ref[0, :, c0:c0 + ff_chunk]
            up = (_dot(xb, wgu_bf[:, D_FF + c0:D_FF + c0 + ff_chunk])
                  + bgu_ref[0, :, D_FF + c0:D_FF + c0 + ff_chunk])
            gate = jnp.minimum(gate, SWIGLU_LIMIT)
            up = jnp.clip(up, -SWIGLU_LIMIT, SWIGLU_LIMIT)
            hid.append(((up + 1.0) * (gate * _sigmoid(gate * SWIGLU_ALPHA))).astype(bf16))
        hid = jnp.concatenate(hid, axis=1)
        group = 2 * LANES
        for n0 in range(0, D_MODEL, group):
            y = _dot(hid, wdn_bf[:, n0:n0 + group]) + bdn_ref[0, :, n0:n0 + group]
            (words,) = _pack_rows(y.astype(bf16).astype(f32))
            y_ref[pl.ds(n0 // group, rows, stride=XSLABS), :] = words

    quarter = MOE_TILE // 4
    for q in range(1, 5):
        rows = q * quarter

        @pl.when((i < nb_ref[0]) & (bv_ref[i] > rows - quarter) & (bv_ref[i] <= rows))
        def _(rows=rows):
            mlp(rows)
            if rows < MOE_TILE:
                rest = (MOE_TILE - rows) * XSLABS
                y_ref[pl.ds(rows * XSLABS, rest), :] = jnp.zeros((rest, LANES), jnp.uint32)


def _experts(block_e, block_seg, next_e, block_valid, n_used, xs, wgu, bgu, wdn, bdn):
    n_blocks = block_e.shape[0]
    grid_spec = pltpu.PrefetchScalarGridSpec(
        num_scalar_prefetch=5,
        grid=(n_blocks,),
        in_specs=[
            pl.BlockSpec((MOE_TILE * XSLABS, LANES),
                         lambda i, be, sg, nx, bv, nb: (jnp.minimum(i, nb[0] - 1), 0)),
            pl.BlockSpec(memory_space=pl.ANY),
            pl.BlockSpec((1, 1, 2 * D_FF), lambda i, be, sg, nx, bv, nb: (be[i], 0, 0)),
            pl.BlockSpec(memory_space=pl.ANY),
            pl.BlockSpec((1, 1, D_MODEL), lambda i, be, sg, nx, bv, nb: (be[i], 0, 0)),
        ],
        out_specs=pl.BlockSpec((MOE_TILE * XSLABS, LANES), lambda i, be, sg, nx, bv, nb: (i, 0)),
        scratch_shapes=[
            pltpu.VMEM((2, D_MODEL, 2 * D_FF), f32),
            pltpu.VMEM((2, D_FF, D_MODEL), f32),
            pltpu.VMEM((D_MODEL, 2 * D_FF), bf16),
            pltpu.VMEM((D_FF, D_MODEL), bf16),
            pltpu.SemaphoreType.DMA((2, 2)),
        ],
    )
    return pl.pallas_call(
        functools.partial(_expert_kernel, ff_chunk=256, cast_rows=128),
        grid_spec=grid_spec,
        out_shape=jax.ShapeDtypeStruct((n_blocks * MOE_TILE * XSLABS, LANES), jnp.uint32),
        compiler_params=pltpu.CompilerParams(
            dimension_semantics=("arbitrary",), vmem_limit_bytes=VMEM_LIMIT),
        name="moe_experts",
    )(block_e, block_seg, next_e, block_valid, n_used, xs, wgu, bgu, wdn, bdn)


def _combine_kernel(tab_cur, tab_nxt, ys_hbm, h_ref, tg_ref, tic_ref, rankc_ref, pp_ref, ps_ref,
                    gple_ref, wpg_ref, wpp_ref, gfin_ref, yp_ref, ys_ref, stage, sems, *, n_prompt_tiles):
    i = pl.program_id(0)
    nt = pl.num_programs(0)
    n = COMBINE_TILE * TOP_K
    slot_rows = n * XSLABS

    def gather(tab_ref, slot):
        slot_base = slot * slot_rows
        for bit in range(RUN_BITS - 1, -1, -1):
            size = (1 << bit) * XSLABS

            def issue(p, carry, bit=bit, size=size):
                dst0 = pl.multiple_of(tab_ref[0, 0, RUN_SRC + bit * N_EXPERTS + p] * XSLABS, XSLABS)
                src0 = pl.multiple_of(tab_ref[0, 0, RUN_DST + bit * N_EXPERTS + p] * XSLABS, XSLABS)
                pltpu.make_async_copy(ys_hbm.at[pl.ds(src0, size)], stage.at[pl.ds(slot_base + dst0, size)],
                                      sems.at[slot]).start(priority=bit % 2)
                return carry

            lax.fori_loop(0, tab_ref[0, 0, RUN_COUNT + bit], issue, 0)

    @pl.when(i == 0)
    def _():
        gather(tab_cur, 0)

    @pl.when(i + 1 < nt)
    def _():
        gather(tab_nxt, (i + 1) % 2)

    slot = i % 2
    base = pl.multiple_of(slot * slot_rows, slot_rows)
    pltpu.make_async_copy(ys_hbm.at[pl.ds(0, slot_rows)], stage.at[pl.ds(base, slot_rows)],
                          sems.at[slot]).wait()

    tic = tic_ref[...]
    shift = jnp.zeros(tic.shape, i32)
    for e in range(N_EXPERTS):
        shift = jnp.where(tic == e, tab_cur[0, 0, RUN_SHIFT + e], shift)
    staged = rankc_ref[...] + shift
    tg = tg_ref[...]
    staged_k = [jnp.broadcast_to(staged[:, k:k + 1], (COMBINE_TILE, LANES)) for k in range(TOP_K)]
    gate_k = [jnp.broadcast_to(tg[:, k:k + 1], (COMBINE_TILE, LANES)) for k in range(TOP_K)]
    lane = lax.broadcasted_iota(i32, (COMBINE_TILE, LANES), 1)
    groups = []
    for c in range(n // LANES):
        g = jnp.zeros((COMBINE_TILE, LANES), f32)
        for k in range(TOP_K):
            g = jnp.where(lane + c * LANES == staged_k[k], gate_k[k], g)
        groups.append(g)
    gates = jnp.concatenate(groups, axis=1)
    gates_hi = gates.astype(bf16)
    gates_lo = (gates - gates_hi.astype(f32)).astype(bf16)
    rows = _unpack_rows(stage, base, n)
    h = h_ref[...] + _dot(gates_hi, rows) + _dot(gates_lo, rows)
    gate = _sigmoid(_dot(_rms(h, gple_ref[...]).astype(bf16), wpg_ref[...]))
    p = _select_rows(i < n_prompt_tiles, pp_ref, ps_ref).astype(bf16)
    h = h + gate * _dot(p, wpp_ref[...])
    y = _rms(h, gfin_ref[...])

    @pl.when(i < n_prompt_tiles)
    def _():
        yp_ref[...] = y

    @pl.when(i >= n_prompt_tiles)
    def _():
        ys_ref[...] = y


def _combine(tile_table, ys, h1, tg, tic, rankc, pp, ps, gple, wpg, wpp, gfin):
    t = h1.shape[0]
    nt = t // COMBINE_TILE
    npt = pp.shape[0] // COMBINE_TILE
    n = COMBINE_TILE * TOP_K
    row = lambda w: pl.BlockSpec((COMBINE_TILE, w), lambda i: (i, 0))
    full = lambda shape: pl.BlockSpec(shape, lambda i: (0,) * len(shape))
    return pl.pallas_call(
        functools.partial(_combine_kernel, n_prompt_tiles=npt),
        grid=(nt,),
        in_specs=[
            pl.BlockSpec((1, 1, RUN_TABLE), lambda i: (i, 0, 0), memory_space=pltpu.SMEM),
            pl.BlockSpec((1, 1, RUN_TABLE), lambda i: (jnp.minimum(i + 1, nt - 1), 0, 0),
                         memory_space=pltpu.SMEM),
            pl.BlockSpec(memory_space=pl.ANY),
            row(D_MODEL), row(LANES), row(LANES), row(LANES),
            pl.BlockSpec((COMBINE_TILE, PLE_DIM), lambda i: (jnp.minimum(i, npt - 1), 0)),
            pl.BlockSpec((COMBINE_TILE, PLE_DIM), lambda i: (jnp.maximum(i - npt, 0), 0)),
            full((1, D_MODEL)), full((D_MODEL, D_MODEL)), full((PLE_DIM, D_MODEL)),
            full((1, D_MODEL)),
        ],
        out_specs=[
            pl.BlockSpec((COMBINE_TILE, D_MODEL), lambda i: (jnp.minimum(i, npt - 1), 0)),
            pl.BlockSpec((COMBINE_TILE, D_MODEL), lambda i: (jnp.maximum(i - npt, 0), 0)),
        ],
        out_shape=[
            jax.ShapeDtypeStruct((pp.shape[0], D_MODEL), f32),
            jax.ShapeDtypeStruct((ps.shape[0], D_MODEL), f32),
        ],
        scratch_shapes=[
            pltpu.VMEM((2 * n * XSLABS, LANES), jnp.uint32),
            pltpu.SemaphoreType.DMA((2,)),
        ],
        compiler_params=pltpu.CompilerParams(
            dimension_semantics=("arbitrary",), vmem_limit_bytes=VMEM_LIMIT),
        name="moe_combine_ple",
    )(tile_table, tile_table, ys, h1, tg, tic, rankc, pp, ps, gple, wpg, wpp, gfin)


def _routing_tables(top_i, counts, part_counts, n_blocks):
    padded = (counts + MOE_TILE - 1) // MOE_TILE * MOE_TILE
    pend = jnp.cumsum(padded)
    pstart = pend - padded
    block_rows = jnp.arange(n_blocks, dtype=i32) * MOE_TILE
    block_e = jnp.minimum(
        jnp.sum((block_rows[:, None] >= pend[None, :]).astype(i32), axis=1), N_EXPERTS - 1)
    experts = jnp.arange(N_EXPERTS, dtype=i32)
    of_block = lambda table: jnp.sum(jnp.where(block_e[:, None] == experts[None, :], table[None, :], 0), axis=1)
    block_valid = jnp.clip(of_block(pstart + counts) - block_rows, 0, MOE_TILE)
    n_used = pend[-1:] // MOE_TILE
    owns = padded > 0
    later = owns[None, :] & (experts[None, :] > experts[:, None])
    next_of_e = jnp.where(jnp.any(later, axis=1), jnp.argmax(later, axis=1), -1)
    block_seg = of_block(jnp.cumsum(owns.astype(i32)) - 1)
    next_e = of_block(next_of_e.astype(i32))

    nt = top_i.shape[1] // DISPATCH_TILE
    tile_n = part_counts[:, :, :TOKEN_TILE // DISPATCH_TILE].transpose(0, 2, 1).reshape(nt, N_EXPERTS)
    seen = jnp.cumsum(tile_n, axis=0) - tile_n
    tile_src = jnp.cumsum(tile_n, axis=1) - tile_n
    tile_dst = pstart[None, :] + seen
    bits = jnp.arange(RUN_BITS, dtype=i32)[None, :, None]
    n3 = tile_n[:, None, :]
    has_piece = ((n3 >> bits) & 1) == 1
    covered = (n3 >> (bits + 1)) << (bits + 1)
    earlier = (experts[:, None] < experts[None, :])[None, None]
    place = jnp.sum((has_piece[..., None] & earlier).astype(i32), axis=2)
    listed = has_piece[..., None] & (place[..., None] == experts[None, None, None, :])
    compact = lambda v: jnp.sum(jnp.where(listed, v[..., None], 0), axis=2)
    piece_src = compact(tile_src[:, None, :] + covered)
    piece_dst = compact(tile_dst[:, None, :] + covered)
    piece_cnt = jnp.sum(has_piece.astype(i32), axis=2)
    tile_table = jnp.concatenate(
        [tile_src - seen, jnp.pad(piece_cnt, ((0, 0), (0, N_EXPERTS - RUN_BITS))),
         piece_src.reshape(nt, -1), piece_dst.reshape(nt, -1)], axis=1)
    tables = (block_e, block_seg, next_e, block_valid, n_used)
    return (tile_table.reshape(nt, 1, RUN_TABLE).astype(i32),) + tuple(
        t.astype(i32) for t in tables)


def kernel(x_prompt, x_sample, p_prompt, p_sample, state_gla, state_hgrn, g_mix, w_in, w_gk2, b_gk2,
           gn_gla, lb_logits, gn_hgrn, w_out, g_ffn, w_router, b_router, w_gu, b_gu, w_dn, b_dn,
           g_ple, w_ple_gate, w_ple_proj, g_final):
    batch, seq, _ = x_prompt.shape
    dec_batch, dec_seq, _ = x_sample.shape
    tp, ts = batch * seq, dec_batch * dec_seq
    xp = x_prompt.reshape(tp, D_MODEL)
    xs = x_sample.reshape(ts, D_MODEL)

    wgk = jnp.concatenate(
        [w_gk2[0], jnp.zeros((R1_PAD - GLA_LOWRANK, GLA_KDIM), f32)], axis=0).astype(bf16)
    wr = w_router[0].T.astype(bf16)
    br = b_router[0].reshape(N_EXPERTS, 1)

    z = _in_proj(xp, xs, g_mix[0].reshape(1, D_MODEL), w_in[0].T)

    mixer_params = (wgk, b_gk2[0].reshape(1, GLA_KDIM), gn_gla[0].reshape(1, GLA_DV),
                    lb_logits[0:2], gn_hgrn[0].reshape(1, HGRN_DV))
    o_p, new_gla_p, new_hgrn_p = _mixer_prompt(
        z, *mixer_params, batch=batch, seq=seq, rows_per_step=512, block_rows=256)
    o_s, new_gla_s, new_hgrn_s = _mixer_sample(
        z, state_gla, state_hgrn, *mixer_params, first_row=tp, steps=dec_seq, batch_block=8)

    h1, xn, top_i, top_g, rank, top_ic, rank_c, counts, part_counts = _out_proj(
        xp, xs, o_p, o_s, w_out[0].astype(bf16), g_ffn[0].reshape(1, D_MODEL), wr, br)

    t = tp + ts
    n_blocks = -(-(t * TOP_K) // MOE_TILE) + N_EXPERTS
    tile_table, block_e, block_seg, next_e, block_valid, n_used = _routing_tables(
        top_i, counts[:, 0], part_counts, n_blocks)
    x_sorted = _dispatch(block_valid, tile_table, top_i, rank, xn)
    y_sorted = _experts(block_e, block_seg, next_e, block_valid, n_used, x_sorted,
                        w_gu[0], b_gu[0].reshape(N_EXPERTS, 1, 2 * D_FF),
                        w_dn[0], b_dn[0].reshape(N_EXPERTS, 1, D_MODEL))

    y_p, y_s = _combine(tile_table, y_sorted, h1, top_g, top_ic, rank_c, p_prompt[0].reshape(tp, PLE_DIM),
                        p_sample[0].reshape(ts, PLE_DIM), g_ple[0].reshape(1, D_MODEL),
                        w_ple_gate[0].astype(bf16), w_ple_proj[0].astype(bf16), g_final.reshape(1, D_MODEL))

    return (y_p.reshape(batch, seq, D_MODEL), y_s.reshape(dec_batch, dec_seq, D_MODEL),
            new_gla_p, new_hgrn_p, new_gla_s, new_hgrn_s)
```

```python
import functools

import jax
import jax.numpy as jnp
from jax import lax
from jax.experimental import pallas as pl
from jax.experimental.pallas import tpu as pltpu

f32 = jnp.float32
bf16 = jnp.bfloat16
i32 = jnp.int32

D_MODEL = 1024
GLA_HEADS = 4
GLA_DK = 64
GLA_DV = 128
GLA_KDIM = GLA_HEADS * GLA_DK
GLA_WIDTH = GLA_HEADS * GLA_DV
GLA_LOWRANK = 16
GLA_NORMALIZER = 16.0
HGRN_HEADS = 4
HGRN_DK = 128
HGRN_DV = 128
HGRN_FDIM = HGRN_HEADS * HGRN_DK
HGRN_WIDTH = HGRN_HEADS * HGRN_DV
CHUNK = 64
N_EXPERTS = 32
TOP_K = 4
D_FF = 1024
SWIGLU_LIMIT = 7.0
SWIGLU_ALPHA = 1.702
PLE_DIM = 256
EPS = 1e-6

LANES = 128
SAMPLE_PAD_T = 8
XSLABS = D_MODEL // (2 * LANES)
ROUTE_ROWS = 8
R1_PAD = LANES
Z_Q1 = 0
Z_K1 = Z_Q1 + GLA_KDIM
Z_V1 = Z_K1 + GLA_KDIM
Z_G1 = Z_V1 + GLA_WIDTH
Z_R1 = Z_G1 + GLA_WIDTH
Z_Q2 = Z_R1 + R1_PAD
Z_F2 = Z_Q2 + HGRN_FDIM
Z_I2 = Z_F2 + HGRN_FDIM
Z_G2 = Z_I2 + HGRN_WIDTH
Z_W = Z_G2 + HGRN_WIDTH

TOKEN_TILE = 512
MOE_TILE = 512
DISPATCH_TILE = 256
COMBINE_TILE = DISPATCH_TILE
RUN_BITS = DISPATCH_TILE.bit_length()
RUN_SHIFT = 0
RUN_COUNT = RUN_SHIFT + N_EXPERTS
RUN_SRC = RUN_COUNT + N_EXPERTS
RUN_DST = RUN_SRC + RUN_BITS * N_EXPERTS
RUN_TABLE = RUN_DST + RUN_BITS * N_EXPERTS
VMEM_LIMIT = 56 * 1024 * 1024


def _rms(x, g):
    return x * lax.rsqrt(jnp.mean(x * x, axis=-1, keepdims=True) + EPS) * g


def _sigmoid(x):
    return 0.5 * jnp.tanh(0.5 * x) + 0.5


def _dot(a, b):
    return jnp.dot(a, b, preferred_element_type=f32)


def _dot_nt(a, b):
    return lax.dot_general(a, b, (((1,), (1,)), ((), ())), preferred_element_type=f32)


def _dot_tn(a, b):
    return lax.dot_general(a, b, (((0,), (0,)), ((), ())), preferred_element_type=f32)


def _cumsum_rows(tri, x):
    hi = x.astype(bf16)
    r1 = x - hi.astype(f32)
    mid = r1.astype(bf16)
    lo = (r1 - mid.astype(f32)).astype(bf16)
    return _dot(tri, hi) + _dot(tri, mid) + _dot(tri, lo)


def _select_rows(is_prompt, a_ref, b_ref):
    return jnp.where(is_prompt, a_ref[...], b_ref[...])


def _inproj_kernel(xp_ref, xs_ref, g_ref, w_hbm, z_ref, w_f32, w_bf, sem,
                   *, n_prompt_tiles, col_chunk):
    @pl.when(pl.program_id(0) == 0)
    def _():
        copy = pltpu.make_async_copy(w_hbm, w_f32, sem)
        copy.start()
        copy.wait()
        for c0 in range(0, Z_W, LANES):
            if c0 < Z_R1:
                cols = w_f32[c0:c0 + LANES, :]
            elif c0 == Z_R1:
                cols = jnp.concatenate(
                    [w_f32[Z_R1:Z_R1 + GLA_LOWRANK, :],
                     jnp.zeros((R1_PAD - GLA_LOWRANK, D_MODEL), f32)], axis=0)
            else:
                s0 = c0 - (R1_PAD - GLA_LOWRANK)
                cols = w_f32[s0:s0 + LANES, :]
            w_bf[:, c0:c0 + LANES] = cols.T.astype(bf16)

    is_prompt = pl.program_id(0) < n_prompt_tiles
    x = _select_rows(is_prompt, xp_ref, xs_ref)
    a = _rms(x, g_ref[...]).astype(bf16)
    for c0 in range(0, Z_W, col_chunk):
        c1 = min(c0 + col_chunk, Z_W)
        z_ref[:, c0:c1] = _dot(a, w_bf[:, c0:c1]).astype(bf16)


def _in_proj(xp, xs, g, w):
    tp, ts = xp.shape[0], xs.shape[0]
    npt = tp // TOKEN_TILE
    nt = npt + ts // TOKEN_TILE
    return pl.pallas_call(
        functools.partial(_inproj_kernel, n_prompt_tiles=npt, col_chunk=512),
        grid=(nt,),
        in_specs=[
            pl.BlockSpec((TOKEN_TILE, D_MODEL), lambda i: (jnp.minimum(i, npt - 1), 0)),
            pl.BlockSpec((TOKEN_TILE, D_MODEL), lambda i: (jnp.maximum(i - npt, 0), 0)),
            pl.BlockSpec((1, D_MODEL), lambda i: (0, 0)),
            pl.BlockSpec(memory_space=pl.ANY),
        ],
        out_specs=pl.BlockSpec((TOKEN_TILE, Z_W), lambda i: (i, 0)),
        out_shape=jax.ShapeDtypeStruct((tp + ts, Z_W), bf16),
        scratch_shapes=[
            pltpu.VMEM(w.shape, f32),
            pltpu.VMEM((D_MODEL, Z_W), bf16),
            pltpu.SemaphoreType.DMA(()),
        ],
        compiler_params=pltpu.CompilerParams(
            dimension_semantics=("arbitrary",), vmem_limit_bytes=VMEM_LIMIT),
        name="in_proj",
    )(xp, xs, g, w)


def _block_consts(r, c):
    shift = c.bit_length() - 1
    row = lax.broadcasted_iota(i32, (r, r), 0)
    col = lax.broadcasted_iota(i32, (r, r), 1)
    causal = (lax.shift_right_logical(row, shift) == lax.shift_right_logical(col, shift)) & (row >= col)
    return causal.astype(bf16), causal


def _subchunk_row(x, c, idx):
    parts = [jnp.broadcast_to(x[j * c + idx:j * c + idx + 1, :], (c, x.shape[1]))
             for j in range(x.shape[0] // c)]
    return jnp.concatenate(parts, axis=0)


def _gated_block(q, k, v, log_a, states, *, c, scale, ref_row, tri, causal, heads_per_group, scan):
    r = q.shape[0]
    nsub = r // c
    groups = q.shape[1] // LANES
    b = _cumsum_rows(tri, log_a)
    b_ref = _subchunk_row(b, c, ref_row)
    b_last = _subchunk_row(b, c, c - 1)
    qs = q * scale
    q_r = qs * jnp.exp(b - b_ref)
    k_r32 = k * jnp.exp(b_ref - b)
    k_r = k_r32.astype(bf16)
    k_end = k_r32 * jnp.exp(b_last - b_ref)
    q_b = q_r * jnp.exp(b_ref)
    decay = jnp.exp(b_last)
    vb = v.astype(bf16)
    lane = lax.broadcasted_iota(i32, (r, LANES), 1)
    width = LANES // heads_per_group
    outs, new_states = [], []
    for g in range(groups):
        gl = slice(g * LANES, (g + 1) * LANES)
        masks = [None] if heads_per_group == 1 else [
            (lane >= j * width) & (lane < (j + 1) * width) for j in range(heads_per_group)]
        pick = lambda x, m: x if m is None else jnp.where(m, x, 0.0)
        q_r_h = [pick(q_r[:, gl], m).astype(bf16) for m in masks]
        q_b_h = [pick(q_b[:, gl], m).astype(bf16) for m in masks]
        k_end_h = [pick(k_end[:, gl], m).astype(bf16) for m in masks]
        v_h = [vb[:, (g * heads_per_group + j) * LANES:(g * heads_per_group + j + 1) * LANES]
               for j in range(heads_per_group)]
        o_intra = []
        for j in range(heads_per_group):
            att = _dot_nt(q_r_h[j], k_r[:, gl])
            att = jnp.where(causal, att, 0.0).astype(bf16)
            o_intra.append(_dot(att, v_h[j]))
        upd = []
        for s in range(nsub):
            rs = slice(s * c, (s + 1) * c)
            u = _dot_tn(v_h[0][rs], k_end_h[0][rs])
            for j in range(1, heads_per_group):
                u = u + _dot_tn(v_h[j][rs], k_end_h[j][rs])
            upd.append(u)
        o_inter = [[] for _ in range(heads_per_group)]
        st = states[g] if scan else None
        group_states = []
        for s in range(nsub):
            rs = slice(s * c, (s + 1) * c)
            if not scan:
                st = states[s][g]
            st_b = st.astype(bf16)
            for j in range(heads_per_group):
                o_inter[j].append(_dot_nt(q_b_h[j][rs], st_b))
            st = st * decay[s * c:s * c + 1, gl] + upd[s]
            if not scan:
                group_states.append(st)
        for j in range(heads_per_group):
            outs.append(o_intra[j] + jnp.concatenate(o_inter[j], axis=0))
        new_states.append(st if scan else group_states)
    if not scan:
        new_states = [[new_states[g][s] for g in range(groups)] for s in range(nsub)]
    return outs, new_states


def _head_norm(o, g):
    return o * lax.rsqrt(jnp.mean(o * o, axis=-1, keepdims=True) + EPS) * g


def _mixer_block(zb, st_gla, st_hgrn, wgk, bgk, gng, lb, gnh, *, c, ref_row, tri, causal, valid, scan):
    z = zb.astype(f32)
    gk = _dot(zb[:, Z_R1:Z_R1 + R1_PAD], wgk) + bgk
    log_a = (jnp.minimum(gk, 0.0) - jnp.log(1.0 + jnp.exp(-jnp.abs(gk)))) / GLA_NORMALIZER
    k1 = z[:, Z_K1:Z_K1 + GLA_KDIM]
    if valid is not None:
        log_a = jnp.where(valid, log_a, 0.0)
        k1 = jnp.where(valid, k1, 0.0)
    o1, st_gla = _gated_block(
        z[:, Z_Q1:Z_Q1 + GLA_KDIM], k1, zb[:, Z_V1:Z_V1 + GLA_WIDTH], log_a, st_gla,
        c=c, scale=GLA_DK ** -0.5, ref_row=ref_row, tri=tri, causal=causal,
        heads_per_group=LANES // GLA_DK, scan=scan)
    forget = lb + (1.0 - lb) * _sigmoid(z[:, Z_F2:Z_F2 + HGRN_FDIM])
    k2 = 1.0 - forget
    log_f = jnp.log(forget)
    if valid is not None:
        log_f = jnp.where(valid, log_f, 0.0)
        k2 = jnp.where(valid, k2, 0.0)
    q2 = z[:, Z_Q2:Z_Q2 + HGRN_FDIM]
    i2 = z[:, Z_I2:Z_I2 + HGRN_WIDTH]
    o2, st_hgrn = _gated_block(
        q2 * _sigmoid(q2), k2, i2 * _sigmoid(i2), log_f, st_hgrn,
        c=c, scale=1.0, ref_row=ref_row, tri=tri, causal=causal, heads_per_group=1, scan=scan)
    cols = []
    for h in range(GLA_HEADS):
        g1 = z[:, Z_G1 + h * GLA_DV:Z_G1 + (h + 1) * GLA_DV]
        cols.append(_head_norm(o1[h], gng) * (g1 * _sigmoid(g1)))
    for h in range(HGRN_HEADS):
        g2 = z[:, Z_G2 + h * HGRN_DV:Z_G2 + (h + 1) * HGRN_DV]
        cols.append(_head_norm(o2[h], gnh) * _sigmoid(g2))
    return jnp.concatenate(cols, axis=1), st_gla, st_hgrn


def _lower_bound(lbl_ref):
    l = lbl_ref[...]
    m = jnp.max(l, axis=0, keepdims=True)
    e = jnp.exp(l - m)
    return e[0:1, :] / jnp.sum(e, axis=0, keepdims=True)


GLA_GROUPS = GLA_KDIM // LANES


def _gla_state_t(s_ref_heads):
    return s_ref_heads.reshape(LANES, GLA_DV).T


def _gla_state_from_t(st):
    return st.T.reshape(LANES // GLA_DK, GLA_DK, GLA_DV)


def _mixer_prompt_kernel(z_ref, wgk_ref, bgk_ref, gng_ref, lbl_ref, gnh_ref,
                         o_ref, sg_out, sh_out, sg_t, sh_t, *, block_rows):
    j = pl.program_id(1)

    @pl.when(j == 0)
    def _():
        sg_t[...] = jnp.zeros_like(sg_t)
        sh_t[...] = jnp.zeros_like(sh_t)

    tri, causal = _block_consts(block_rows, CHUNK)
    lb = _lower_bound(lbl_ref)
    wgk, bgk, gng, gnh = wgk_ref[...], bgk_ref[...], gng_ref[...], gnh_ref[...]

    def body(blk, carry):
        r0 = pl.multiple_of(blk * block_rows, block_rows)
        o, st_g, st_h = _mixer_block(
            z_ref[pl.ds(r0, block_rows), :], [sg_t[g] for g in range(GLA_GROUPS)],
            [sh_t[h] for h in range(HGRN_HEADS)], wgk, bgk, gng, lb, gnh,
            c=CHUNK, ref_row=CHUNK // 2, tri=tri, causal=causal, valid=None, scan=True)
        o_ref[pl.ds(r0, block_rows), :] = o.astype(bf16)
        for g in range(GLA_GROUPS):
            sg_t[g] = st_g[g]
        for h in range(HGRN_HEADS):
            sh_t[h] = st_h[h]
        return carry

    lax.fori_loop(0, z_ref.shape[0] // block_rows, body, 0)

    @pl.when(j == pl.num_programs(1) - 1)
    def _():
        per = LANES // GLA_DK
        for g in range(GLA_GROUPS):
            sg_out[0, 0, g * per:(g + 1) * per] = _gla_state_from_t(sg_t[g])
        for h in range(HGRN_HEADS):
            sh_out[0, 0, h] = sh_t[h].T


def _mixer_prompt(z, wgk, bgk, gng, lbl, gnh, *, batch, seq, rows_per_step, block_rows):
    nj = seq // rows_per_step
    full = lambda shape: pl.BlockSpec(shape, lambda b, j: (0,) * len(shape))
    return pl.pallas_call(
        functools.partial(_mixer_prompt_kernel, block_rows=block_rows),
        grid=(batch, nj),
        in_specs=[
            pl.BlockSpec((rows_per_step, Z_W), lambda b, j: (b * nj + j, 0)),
            full((R1_PAD, GLA_KDIM)), full((1, GLA_KDIM)), full((1, GLA_DV)),
            full((2, HGRN_FDIM)), full((1, HGRN_DV)),
        ],
        out_specs=[
            pl.BlockSpec((rows_per_step, D_MODEL), lambda b, j: (b * nj + j, 0)),
            pl.BlockSpec((1, 1, GLA_HEADS, GLA_DK, GLA_DV), lambda b, j: (0, b, 0, 0, 0)),
            pl.BlockSpec((1, 1, HGRN_HEADS, HGRN_DK, HGRN_DV), lambda b, j: (0, b, 0, 0, 0)),
        ],
        out_shape=[
            jax.ShapeDtypeStruct((batch * seq, D_MODEL), bf16),
            jax.ShapeDtypeStruct((1, batch, GLA_HEADS, GLA_DK, GLA_DV), f32),
            jax.ShapeDtypeStruct((1, batch, HGRN_HEADS, HGRN_DK, HGRN_DV), f32),
        ],
        scratch_shapes=[
            pltpu.VMEM((GLA_GROUPS, GLA_DV, LANES), f32),
            pltpu.VMEM((HGRN_HEADS, HGRN_DV, HGRN_DK), f32),
        ],
        compiler_params=pltpu.CompilerParams(
            dimension_semantics=("arbitrary", "arbitrary"), vmem_limit_bytes=VMEM_LIMIT),
        name="mixer_prompt",
    )(z, wgk, bgk, gng, lbl, gnh)


def _mixer_sample_kernel(z_ref, sg_in, sh_in, wgk_ref, bgk_ref, gng_ref, lbl_ref, gnh_ref,
                         o_ref, sg_out, sh_out, *, batch_block, steps):
    rows = batch_block * SAMPLE_PAD_T
    tri, causal = _block_consts(rows, SAMPLE_PAD_T)
    step = lax.broadcasted_iota(i32, (rows, 1), 0) & (SAMPLE_PAD_T - 1)
    valid = step < steps
    lb = _lower_bound(lbl_ref)
    per = LANES // GLA_DK
    st_g = [[_gla_state_t(sg_in[0, s, g * per:(g + 1) * per]) for g in range(GLA_GROUPS)]
            for s in range(batch_block)]
    st_h = [[sh_in[0, s, h].T for h in range(HGRN_HEADS)] for s in range(batch_block)]
    packed = batch_block * steps
    pad_shift = SAMPLE_PAD_T.bit_length() - 1
    r = lax.broadcasted_iota(i32, (rows, packed), 0)
    col = lax.broadcasted_iota(i32, (rows, packed), 1)
    t = r & (SAMPLE_PAD_T - 1)
    spread = ((t < steps) & (col == lax.shift_right_logical(r, pad_shift) * steps + t)).astype(bf16)
    r = lax.broadcasted_iota(i32, (packed, rows), 1)
    col = lax.broadcasted_iota(i32, (packed, rows), 0)
    t = r & (SAMPLE_PAD_T - 1)
    gather = ((t < steps) & (col == lax.shift_right_logical(r, pad_shift) * steps + t)).astype(bf16)
    z = _dot(spread, z_ref[...]).astype(bf16)
    o, st_g, st_h = _mixer_block(
        z, st_g, st_h, wgk_ref[...], bgk_ref[...], gng_ref[...], lb, gnh_ref[...],
        c=SAMPLE_PAD_T, ref_row=steps // 2, tri=tri, causal=causal, valid=valid, scan=False)
    o_ref[...] = _dot(gather, o.astype(bf16)).astype(bf16)
    for s in range(batch_block):
        for g in range(GLA_GROUPS):
            sg_out[0, s, g * per:(g + 1) * per] = _gla_state_from_t(st_g[s][g])
        for h in range(HGRN_HEADS):
            sh_out[0, s, h] = st_h[s][h].T


def _mixer_sample(z, state_gla, state_hgrn, wgk, bgk, gng, lbl, gnh, *, first_row, steps, batch_block):
    batch = state_gla.shape[1]
    rows = batch_block * steps
    first_block = first_row // rows
    full = lambda shape: pl.BlockSpec(shape, lambda i: (0,) * len(shape))
    sg_spec = pl.BlockSpec((1, batch_block, GLA_HEADS, GLA_DK, GLA_DV), lambda i: (0, i, 0, 0, 0))
    sh_spec = pl.BlockSpec((1, batch_block, HGRN_HEADS, HGRN_DK, HGRN_DV), lambda i: (0, i, 0, 0, 0))
    return pl.pallas_call(
        functools.partial(_mixer_sample_kernel, batch_block=batch_block, steps=steps),
        grid=(batch // batch_block,),
        in_specs=[
            pl.BlockSpec((rows, Z_W), lambda i: (first_block + i, 0)),
            sg_spec, sh_spec,
            full((R1_PAD, GLA_KDIM)), full((1, GLA_KDIM)), full((1, GLA_DV)),
            full((2, HGRN_FDIM)), full((1, HGRN_DV)),
        ],
        out_specs=[pl.BlockSpec((rows, D_MODEL), lambda i: (i, 0)), sg_spec, sh_spec],
        out_shape=[
            jax.ShapeDtypeStruct((batch * steps, D_MODEL), bf16),
            jax.ShapeDtypeStruct(state_gla.shape, f32),
            jax.ShapeDtypeStruct(state_hgrn.shape, f32),
        ],
        compiler_params=pltpu.CompilerParams(
            dimension_semantics=("arbitrary",), vmem_limit_bytes=VMEM_LIMIT),
        name="mixer_sample",
    )(z, state_gla, state_hgrn, wgk, bgk, gng, lbl, gnh)


HIGH_HALF = 0xFFFF0000


def _pack_rows(x):
    bits = lax.bitcast_convert_type(x, jnp.uint32)
    return [(bits[:, (2 * s + 1) * LANES:(2 * s + 2) * LANES] & jnp.uint32(HIGH_HALF))
            | (bits[:, 2 * s * LANES:(2 * s + 1) * LANES] >> 16) for s in range(x.shape[1] // (2 * LANES))]


def _unpack_rows(ref, start, rows):
    cols = []
    for s in range(XSLABS):
        w = ref[pl.ds(start + s, rows, stride=XSLABS), :]
        cols.append(lax.bitcast_convert_type(w << 16, f32).astype(bf16))
        cols.append(lax.bitcast_convert_type(w & jnp.uint32(HIGH_HALF), f32).astype(bf16))
    return jnp.concatenate(cols, axis=1)


def _outproj_kernel(xp_ref, xs_ref, op_ref, os_ref, wo_ref, g_ref, wr_ref, br_ref,
                    h_ref, xn_ref, ti_ref, tg_ref, rank_ref, tic_ref, rankc_ref, cnt_ref, part_ref, run_cnt,
                    *, n_prompt_tiles):
    i = pl.program_id(0)

    @pl.when(i == 0)
    def _():
        run_cnt[...] = jnp.zeros_like(run_cnt)

    x = _select_rows(i < n_prompt_tiles, xp_ref, xs_ref)
    o = _select_rows(i < n_prompt_tiles, op_ref, os_ref)
    h = x + _dot(o, wo_ref[...])
    h_ref[...] = h
    xn = _rms(h, g_ref[...])
    xn_ref[...] = xn.astype(bf16)
    l = _dot_nt(wr_ref[...], xn.astype(bf16)) + br_ref[...]
    expert = lax.broadcasted_iota(i32, l.shape, 0)
    neg = jnp.float32(-jnp.inf)
    vals, idxs = [], []
    for _ in range(TOP_K):
        m = jnp.max(l, axis=0, keepdims=True)
        idx = jnp.min(jnp.where(l == m, expert, N_EXPERTS), axis=0, keepdims=True)
        vals.append(m)
        idxs.append(idx)
        l = jnp.where(expert == idx, neg, l)
    es = [jnp.exp(v - vals[0]) for v in vals]
    denom = es[0] + es[1] + es[2] + es[3]

    onehot = jnp.zeros(l.shape, f32)
    for k in range(TOP_K):
        onehot = onehot + (expert == idxs[k]).astype(f32)
    n = l.shape[1]
    earlier = (lax.broadcasted_iota(i32, (n, n), 0) < lax.broadcasted_iota(i32, (n, n), 1)).astype(bf16)
    ahead = _dot(onehot.astype(bf16), earlier) + run_cnt[:, 0:1]
    run_cnt[...] = run_cnt[...] + jnp.sum(onehot, axis=1, keepdims=True)
    cnt_ref[...] = run_cnt[...].astype(i32)
    lane = lax.broadcasted_iota(i32, (N_EXPERTS, LANES), 1)
    parts = jnp.zeros((N_EXPERTS, LANES), f32)
    for p in range(n // DISPATCH_TILE):
        c = jnp.sum(onehot[:, p * DISPATCH_TILE:(p + 1) * DISPATCH_TILE], axis=1, keepdims=True)
        parts = jnp.where(lane == p, c, parts)
    part_ref[0] = parts.astype(i32)

    ranks = [jnp.sum(jnp.where(expert == idxs[k], ahead, 0.0), axis=0, keepdims=True).astype(i32)
             for k in range(TOP_K)]
    pad_i = jnp.zeros((ROUTE_ROWS - TOP_K, n), i32)
    ti_ref[...] = jnp.concatenate(idxs + [pad_i], axis=0)
    rank_ref[...] = jnp.concatenate(ranks + [pad_i], axis=0)
    gate_rows = jnp.concatenate([e / denom for e in es] + [jnp.zeros((LANES - TOP_K, n), f32)], axis=0)
    tg_ref[...] = gate_rows.T
    pad_wide = jnp.zeros((LANES - TOP_K, n), i32)
    tic_ref[...] = jnp.concatenate(idxs + [pad_wide], axis=0).T
    rankc_ref[...] = jnp.concatenate(ranks + [pad_wide], axis=0).T


def _out_proj(xp, xs, o_p, o_s, wo, g, wr, br):
    tp, ts = xp.shape[0], xs.shape[0]
    npt = tp // TOKEN_TILE
    nt = npt + ts // TOKEN_TILE
    t = tp + ts
    row = lambda w: pl.BlockSpec((TOKEN_TILE, w), lambda i: (i, 0))
    route = pl.BlockSpec((ROUTE_ROWS, TOKEN_TILE), lambda i: (0, i))
    full = lambda shape: pl.BlockSpec(shape, lambda i: (0,) * len(shape))
    prompt_rows = pl.BlockSpec((TOKEN_TILE, D_MODEL), lambda i: (jnp.minimum(i, npt - 1), 0))
    sample_rows = pl.BlockSpec((TOKEN_TILE, D_MODEL), lambda i: (jnp.maximum(i - npt, 0), 0))
    return pl.pallas_call(
        functools.partial(_outproj_kernel, n_prompt_tiles=npt),
        grid=(nt,),
        in_specs=[
            prompt_rows, sample_rows, prompt_rows, sample_rows,
            full((D_MODEL, D_MODEL)), full((1, D_MODEL)),
            full((N_EXPERTS, D_MODEL)), full((N_EXPERTS, 1)),
        ],
        out_specs=[row(D_MODEL), row(D_MODEL),
                   route, row(LANES), route, row(LANES), row(LANES), full((N_EXPERTS, LANES)),
                   pl.BlockSpec((1, N_EXPERTS, LANES), lambda i: (i, 0, 0))],
        out_shape=[
            jax.ShapeDtypeStruct((t, D_MODEL), f32),
            jax.ShapeDtypeStruct((t, D_MODEL), bf16),
            jax.ShapeDtypeStruct((ROUTE_ROWS, t), i32),
            jax.ShapeDtypeStruct((t, LANES), f32),
            jax.ShapeDtypeStruct((ROUTE_ROWS, t), i32),
            jax.ShapeDtypeStruct((t, LANES), i32),
            jax.ShapeDtypeStruct((t, LANES), i32),
            jax.ShapeDtypeStruct((N_EXPERTS, LANES), i32),
            jax.ShapeDtypeStruct((nt, N_EXPERTS, LANES), i32),
        ],
        scratch_shapes=[pltpu.VMEM((N_EXPERTS, LANES), f32)],
        compiler_params=pltpu.CompilerParams(
            dimension_semantics=("arbitrary",), vmem_limit_bytes=VMEM_LIMIT),
        name="out_proj_router",
    )(xp, xs, o_p, o_s, wo, g, wr, br)


def _dispatch_kernel(bv_ref, nz_ref, tab_ref, ti_ref, rank_ref, xn_ref, xs_hbm, zeros, stage, sems, zsem,
                     *, n_blocks):
    @pl.when(pl.program_id(0) == 0)
    def _():
        zeros[...] = jnp.zeros_like(zeros)

        def fill(b, carry):
            @pl.when(bv_ref[b] < MOE_TILE)
            def _():
                r0 = pl.multiple_of(b * (MOE_TILE * XSLABS), MOE_TILE * XSLABS)
                pltpu.make_async_copy(zeros, xs_hbm.at[pl.ds(r0, MOE_TILE * XSLABS)], zsem).start()
            return carry

        def drain(b, carry):
            pltpu.make_async_copy(zeros, xs_hbm.at[pl.ds(0, MOE_TILE * XSLABS)], zsem).wait()
            return carry

        lax.fori_loop(0, n_blocks, fill, 0)
        lax.fori_loop(0, nz_ref[0], drain, 0)

    i = pl.program_id(0)
    nt = pl.num_programs(0)
    n = DISPATCH_TILE * TOP_K
    slot_rows = n * XSLABS
    slot = i % 2
    base = pl.multiple_of(slot * slot_rows, slot_rows)

    def wait_rows(s):
        pltpu.make_async_copy(stage.at[pl.ds(0, slot_rows)], xs_hbm.at[pl.ds(0, slot_rows)], sems.at[s]).wait()

    ti = ti_ref[...]
    shift = jnp.zeros(ti.shape, i32)
    for e in range(N_EXPERTS):
        shift = jnp.where(ti == e, tab_ref[0, 0, RUN_SHIFT + e], shift)
    staged = rank_ref[...] + shift
    row = lax.broadcasted_iota(i32, (n, DISPATCH_TILE), 0)
    pick = row == staged[0:1, :]
    for k in range(1, TOP_K):
        pick = pick | (row == staged[k:k + 1, :])
    rows_sorted = _dot(pick.astype(bf16), xn_ref[...])
    for s, words in enumerate(_pack_rows(rows_sorted)):
        stage[pl.ds(base + s, n, stride=XSLABS), :] = words

    for bit in range(RUN_BITS - 1, -1, -1):
        size = (1 << bit) * XSLABS

        def issue(p, carry, bit=bit, size=size):
            src0 = pl.multiple_of(tab_ref[0, 0, RUN_SRC + bit * N_EXPERTS + p] * XSLABS, XSLABS)
            dst0 = pl.multiple_of(tab_ref[0, 0, RUN_DST + bit * N_EXPERTS + p] * XSLABS, XSLABS)
            pltpu.make_async_copy(stage.at[pl.ds(base + src0, size)], xs_hbm.at[pl.ds(dst0, size)],
                                  sems.at[slot]).start(priority=bit % 2)
            return carry

        lax.fori_loop(0, tab_ref[0, 0, RUN_COUNT + bit], issue, 0)

    @pl.when(i > 0)
    def _():
        wait_rows(1 - slot)

    @pl.when(i == nt - 1)
    def _():
        wait_rows(slot)


def _dispatch(block_valid, tile_table, top_i, rank, xn):
    t = xn.shape[0]
    nt = t // DISPATCH_TILE
    n_blocks = block_valid.shape[0]
    n_zero = jnp.sum((block_valid < MOE_TILE).astype(i32)).reshape(1)
    route = pl.BlockSpec((ROUTE_ROWS, DISPATCH_TILE), lambda i, bv, nz: (0, i))
    grid_spec = pltpu.PrefetchScalarGridSpec(
        num_scalar_prefetch=2,
        grid=(nt,),
        in_specs=[
            pl.BlockSpec((1, 1, RUN_TABLE), lambda i, bv, nz: (i, 0, 0), memory_space=pltpu.SMEM),
            route, route,
            pl.BlockSpec((DISPATCH_TILE, D_MODEL), lambda i, bv, nz: (i, 0)),
        ],
        out_specs=pl.BlockSpec(memory_space=pl.ANY),
        scratch_shapes=[
            pltpu.VMEM((MOE_TILE * XSLABS, LANES), jnp.uint32),
            pltpu.VMEM((2 * DISPATCH_TILE * TOP_K * XSLABS, LANES), jnp.uint32),
            pltpu.SemaphoreType.DMA((2,)),
            pltpu.SemaphoreType.DMA(()),
        ],
    )
    return pl.pallas_call(
        functools.partial(_dispatch_kernel, n_blocks=n_blocks),
        grid_spec=grid_spec,
        out_shape=jax.ShapeDtypeStruct((n_blocks * MOE_TILE * XSLABS, LANES), jnp.uint32),
        compiler_params=pltpu.CompilerParams(
            dimension_semantics=("arbitrary",), vmem_limit_bytes=VMEM_LIMIT),
        name="moe_dispatch",
    )(block_valid, n_zero, tile_table, top_i, rank, xn)


def _expert_kernel(be_ref, seg_ref, nxt_ref, bv_ref, nb_ref, x_ref, wgu_hbm, bgu_ref, wdn_hbm, bdn_ref, y_ref,
                   wgu_f32, wdn_f32, wgu_bf, wdn_bf, sems, *, ff_chunk, cast_rows):
    i = pl.program_id(0)

    @pl.when(i >= nb_ref[0])
    def _():
        y_ref[...] = jnp.zeros_like(y_ref)

    def fetch(e, slot):
        return (pltpu.make_async_copy(wgu_hbm.at[e], wgu_f32.at[slot], sems.at[slot, 0]),
                pltpu.make_async_copy(wdn_hbm.at[e], wdn_f32.at[slot], sems.at[slot, 1]))

    @pl.when((i < nb_ref[0]) & ((i == 0) | (be_ref[i] != be_ref[jnp.maximum(i - 1, 0)])))
    def _():
        slot = seg_ref[i] % 2

        @pl.when(i == 0)
        def _():
            for c in fetch(be_ref[0], 0):
                c.start()

        for c in fetch(be_ref[i], slot):
            c.wait()

        @pl.when(nxt_ref[i] >= 0)
        def _():
            for c in fetch(nxt_ref[i], 1 - slot):
                c.start()

        def cast(j, carry):
            r0 = pl.multiple_of(j * cast_rows, cast_rows)
            wgu_bf[pl.ds(r0, cast_rows), :] = wgu_f32[slot, pl.ds(r0, cast_rows), :].astype(bf16)
            wdn_bf[pl.ds(r0, cast_rows), :] = wdn_f32[slot, pl.ds(r0, cast_rows), :].astype(bf16)
            return carry
        lax.fori_loop(0, D_MODEL // cast_rows, cast, 0)

    def mlp(rows):
        xb = _unpack_rows(x_ref, 0, rows)
        hid = []
        for c0 in range(0, D_FF, ff_chunk):
            gate = _dot(xb, wgu_bf[:, c0:c0 + ff_chunk]) + bgu_ref[0, :, c0:c0 + ff_chunk]
            up = (_dot(xb, wgu_bf[:, D_FF + c0:D_FF + c0 + ff_chunk])
                  + bgu_ref[0, :, D_FF + c0:D_FF + c0 + ff_chunk])
            gate = jnp.minimum(gate, SWIGLU_LIMIT)
            up = jnp.clip(up, -SWIGLU_LIMIT, SWIGLU_LIMIT)
            hid.append(((up + 1.0) * (gate * _sigmoid(gate * SWIGLU_ALPHA))).astype(bf16))
        hid = jnp.concatenate(hid, axis=1)
        group = 2 * LANES
        for n0 in range(0, D_MODEL, group):
            y = _dot(hid, wdn_bf[:, n0:n0 + group]) + bdn_ref[0, :, n0:n0 + group]
            (words,) = _pack_rows(y.astype(bf16).astype(f32))
            y_ref[pl.ds(n0 // group, rows, stride=XSLABS), :] = words

    quarter = MOE_TILE // 4
    for q in range(1, 5):
        rows = q * quarter

        @pl.when((i < nb_ref[0]) & (bv_ref[i] > rows - quarter) & (bv_ref[i] <= rows))
        def _(rows=rows):
            mlp(rows)
            if rows < MOE_TILE:
                rest = (MOE_TILE - rows) * XSLABS
                y_ref[pl.ds(rows * XSLABS, rest), :] = jnp.zeros((rest, LANES), jnp.uint32)


def _experts(block_e, block_seg, next_e, block_valid, n_used, xs, wgu, bgu, wdn, bdn):
    n_blocks = block_e.shape[0]
    grid_spec = pltpu.PrefetchScalarGridSpec(
        num_scalar_prefetch=5,
        grid=(n_blocks,),
        in_specs=[
            pl.BlockSpec((MOE_TILE * XSLABS, LANES),
                         lambda i, be, sg, nx, bv, nb: (jnp.minimum(i, nb[0] - 1), 0)),
            pl.BlockSpec(memory_space=pl.ANY),
            pl.BlockSpec((1, 1, 2 * D_FF), lambda i, be, sg, nx, bv, nb: (be[i], 0, 0)),
            pl.BlockSpec(memory_space=pl.ANY),
            pl.BlockSpec((1, 1, D_MODEL), lambda i, be, sg, nx, bv, nb: (be[i], 0, 0)),
        ],
        out_specs=pl.BlockSpec((MOE_TILE * XSLABS, LANES), lambda i, be, sg, nx, bv, nb: (i, 0)),
        scratch_shapes=[
            pltpu.VMEM((2, D_MODEL, 2 * D_FF), f32),
            pltpu.VMEM((2, D_FF, D_MODEL), f32),
            pltpu.VMEM((D_MODEL, 2 * D_FF), bf16),
            pltpu.VMEM((D_FF, D_MODEL), bf16),
            pltpu.SemaphoreType.DMA((2, 2)),
        ],
    )
    return pl.pallas_call(
        functools.partial(_expert_kernel, ff_chunk=256, cast_rows=128),
        grid_spec=grid_spec,
        out_shape=jax.ShapeDtypeStruct((n_blocks * MOE_TILE * XSLABS, LANES), jnp.uint32),
        compiler_params=pltpu.CompilerParams(
            dimension_semantics=("arbitrary",), vmem_limit_bytes=VMEM_LIMIT),
        name="moe_experts",
    )(block_e, block_seg, next_e, block_valid, n_used, xs, wgu, bgu, wdn, bdn)


def _combine_kernel(tab_cur, tab_nxt, ys_hbm, h_ref, tg_ref, tic_ref, rankc_ref, pp_ref, ps_ref,
                    gple_ref, wpg_ref, wpp_ref, gfin_ref, yp_ref, ys_ref, stage, sems, *, n_prompt_tiles):
    i = pl.program_id(0)
    nt = pl.num_programs(0)
    n = COMBINE_TILE * TOP_K
    slot_rows = n * XSLABS

    def gather(tab_ref, slot):
        slot_base = slot * slot_rows
        for bit in range(RUN_BITS - 1, -1, -1):
            size = (1 << bit) * XSLABS

            def issue(p, carry, bit=bit, size=size):
                dst0 = pl.multiple_of(tab_ref[0, 0, RUN_SRC + bit * N_EXPERTS + p] * XSLABS, XSLABS)
                src0 = pl.multiple_of(tab_ref[0, 0, RUN_DST + bit * N_EXPERTS + p] * XSLABS, XSLABS)
                pltpu.make_async_copy(ys_hbm.at[pl.ds(src0, size)], stage.at[pl.ds(slot_base + dst0, size)],
                                      sems.at[slot]).start(priority=bit % 2)
                return carry

            lax.fori_loop(0, tab_ref[0, 0, RUN_COUNT + bit], issue, 0)

    @pl.when(i == 0)
    def _():
        gather(tab_cur, 0)

    @pl.when(i + 1 < nt)
    def _():
        gather(tab_nxt, (i + 1) % 2)

    slot = i % 2
    base = pl.multiple_of(slot * slot_rows, slot_rows)
    pltpu.make_async_copy(ys_hbm.at[pl.ds(0, slot_rows)], stage.at[pl.ds(base, slot_rows)],
                          sems.at[slot]).wait()

    tic = tic_ref[...]
    level = [tab_cur[0, 0, RUN_SHIFT + e] for e in range(N_EXPERTS)]
    bit = 0
    while len(level) > 1:
        odd = (tic & (1 << bit)) != 0
        level = [jnp.where(odd, level[2 * j + 1], level[2 * j]) for j in range(len(level) // 2)]
        bit += 1
    staged = rankc_ref[...] + level[0]
    tg = tg_ref[...]
    staged_k = [jnp.broadcast_to(staged[:, k:k + 1], (COMBINE_TILE, LANES)) for k in range(TOP_K)]
    gate_k = [jnp.broadcast_to(tg[:, k:k + 1], (COMBINE_TILE, LANES)) for k in range(TOP_K)]
    lane = lax.broadcasted_iota(i32, (COMBINE_TILE, LANES), 1)
    groups = []
    for c in range(n // LANES):
        g = jnp.zeros((COMBINE_TILE, LANES), f32)
        for k in range(TOP_K):
            g = jnp.where(lane + c * LANES == staged_k[k], gate_k[k], g)
        groups.append(g)
    gates = jnp.concatenate(groups, axis=1)
    gates_hi = gates.astype(bf16)
    gates_lo = (gates - gates_hi.astype(f32)).astype(bf16)
    rows = _unpack_rows(stage, base, n)
    h = h_ref[...] + _dot(gates_hi, rows) + _dot(gates_lo, rows)
    gate = _sigmoid(_dot(_rms(h, gple_ref[...]).astype(bf16), wpg_ref[...]))
    p = _select_rows(i < n_prompt_tiles, pp_ref, ps_ref).astype(bf16)
    h = h + gate * _dot(p, wpp_ref[...])
    y = _rms(h, gfin_ref[...])

    @pl.when(i < n_prompt_tiles)
    def _():
        yp_ref[...] = y

    @pl.when(i >= n_prompt_tiles)
    def _():
        ys_ref[...] = y


def _combine(tile_table, ys, h1, tg, tic, rankc, pp, ps, gple, wpg, wpp, gfin):
    t = h1.shape[0]
    nt = t // COMBINE_TILE
    npt = pp.shape[0] // COMBINE_TILE
    n = COMBINE_TILE * TOP_K
    row = lambda w: pl.BlockSpec((COMBINE_TILE, w), lambda i: (i, 0))
    full = lambda shape: pl.BlockSpec(shape, lambda i: (0,) * len(shape))
    return pl.pallas_call(
        functools.partial(_combine_kernel, n_prompt_tiles=npt),
        grid=(nt,),
        in_specs=[
            pl.BlockSpec((1, 1, RUN_TABLE), lambda i: (i, 0, 0), memory_space=pltpu.SMEM),
            pl.BlockSpec((1, 1, RUN_TABLE), lambda i: (jnp.minimum(i + 1, nt - 1), 0, 0),
                         memory_space=pltpu.SMEM),
            pl.BlockSpec(memory_space=pl.ANY),
            row(D_MODEL), row(LANES), row(LANES), row(LANES),
            pl.BlockSpec((COMBINE_TILE, PLE_DIM), lambda i: (jnp.minimum(i, npt - 1), 0)),
            pl.BlockSpec((COMBINE_TILE, PLE_DIM), lambda i: (jnp.maximum(i - npt, 0), 0)),
            full((1, D_MODEL)), full((D_MODEL, D_MODEL)), full((PLE_DIM, D_MODEL)),
            full((1, D_MODEL)),
        ],
        out_specs=[
            pl.BlockSpec((COMBINE_TILE, D_MODEL), lambda i: (jnp.minimum(i, npt - 1), 0)),
            pl.BlockSpec((COMBINE_TILE, D_MODEL), lambda i: (jnp.maximum(i - npt, 0), 0)),
        ],
        out_shape=[
            jax.ShapeDtypeStruct((pp.shape[0], D_MODEL), f32),
            jax.ShapeDtypeStruct((ps.shape[0], D_MODEL), f32),
        ],
        scratch_shapes=[
            pltpu.VMEM((2 * n * XSLABS, LANES), jnp.uint32),
            pltpu.SemaphoreType.DMA((2,)),
        ],
        compiler_params=pltpu.CompilerParams(
            dimension_semantics=("arbitrary",), vmem_limit_bytes=VMEM_LIMIT),
        name="moe_combine_ple",
    )(tile_table, tile_table, ys, h1, tg, tic, rankc, pp, ps, gple, wpg, wpp, gfin)


def _routing_tables(top_i, counts, part_counts, n_blocks):
    padded = (counts + MOE_TILE - 1) // MOE_TILE * MOE_TILE
    pend = jnp.cumsum(padded)
    pstart = pend - padded
    block_rows = jnp.arange(n_blocks, dtype=i32) * MOE_TILE
    block_e = jnp.minimum(
        jnp.sum((block_rows[:, None] >= pend[None, :]).astype(i32), axis=1), N_EXPERTS - 1)
    experts = jnp.arange(N_EXPERTS, dtype=i32)
    of_block = lambda table: jnp.sum(jnp.where(block_e[:, None] == experts[None, :], table[None, :], 0), axis=1)
    block_valid = jnp.clip(of_block(pstart + counts) - block_rows, 0, MOE_TILE)
    n_used = pend[-1:] // MOE_TILE
    owns = padded > 0
    later = owns[None, :] & (experts[None, :] > experts[:, None])
    next_of_e = jnp.where(jnp.any(later, axis=1), jnp.argmax(later, axis=1), -1)
    block_seg = of_block(jnp.cumsum(owns.astype(i32)) - 1)
    next_e = of_block(next_of_e.astype(i32))

    nt = top_i.shape[1] // DISPATCH_TILE
    tile_n = part_counts[:, :, :TOKEN_TILE // DISPATCH_TILE].transpose(0, 2, 1).reshape(nt, N_EXPERTS)
    seen = jnp.cumsum(tile_n, axis=0) - tile_n
    tile_src = jnp.cumsum(tile_n, axis=1) - tile_n
    tile_dst = pstart[None, :] + seen
    bits = jnp.arange(RUN_BITS, dtype=i32)[None, :, None]
    n3 = tile_n[:, None, :]
    has_piece = ((n3 >> bits) & 1) == 1
    covered = (n3 >> (bits + 1)) << (bits + 1)
    earlier = (experts[:, None] < experts[None, :])[None, None]
    place = jnp.sum((has_piece[..., None] & earlier).astype(i32), axis=2)
    listed = has_piece[..., None] & (place[..., None] == experts[None, None, None, :])
    compact = lambda v: jnp.sum(jnp.where(listed, v[..., None], 0), axis=2)
    piece_src = compact(tile_src[:, None, :] + covered)
    piece_dst = compact(tile_dst[:, None, :] + covered)
    piece_cnt = jnp.sum(has_piece.astype(i32), axis=2)
    tile_table = jnp.concatenate(
        [tile_src - seen, jnp.pad(piece_cnt, ((0, 0), (0, N_EXPERTS - RUN_BITS))),
         piece_src.reshape(nt, -1), piece_dst.reshape(nt, -1)], axis=1)
    tables = (block_e, block_seg, next_e, block_valid, n_used)
    return (tile_table.reshape(nt, 1, RUN_TABLE).astype(i32),) + tuple(
        t.astype(i32) for t in tables)


def kernel(x_prompt, x_sample, p_prompt, p_sample, state_gla, state_hgrn, g_mix, w_in, w_gk2, b_gk2,
           gn_gla, lb_logits, gn_hgrn, w_out, g_ffn, w_router, b_router, w_gu, b_gu, w_dn, b_dn,
           g_ple, w_ple_gate, w_ple_proj, g_final):
    batch, seq, _ = x_prompt.shape
    dec_batch, dec_seq, _ = x_sample.shape
    tp, ts = batch * seq, dec_batch * dec_seq
    xp = x_prompt.reshape(tp, D_MODEL)
    xs = x_sample.reshape(ts, D_MODEL)

    wgk = jnp.concatenate(
        [w_gk2[0], jnp.zeros((R1_PAD - GLA_LOWRANK, GLA_KDIM), f32)], axis=0).astype(bf16)
    wr = w_router[0].T.astype(bf16)
    br = b_router[0].reshape(N_EXPERTS, 1)

    z = _in_proj(xp, xs, g_mix[0].reshape(1, D_MODEL), w_in[0].T)

    mixer_params = (wgk, b_gk2[0].reshape(1, GLA_KDIM), gn_gla[0].reshape(1, GLA_DV),
                    lb_logits[0:2], gn_hgrn[0].reshape(1, HGRN_DV))
    o_p, new_gla_p, new_hgrn_p = _mixer_prompt(
        z, *mixer_params, batch=batch, seq=seq, rows_per_step=512, block_rows=256)
    o_s, new_gla_s, new_hgrn_s = _mixer_sample(
        z, state_gla, state_hgrn, *mixer_params, first_row=tp, steps=dec_seq, batch_block=8)

    h1, xn, top_i, top_g, rank, top_ic, rank_c, counts, part_counts = _out_proj(
        xp, xs, o_p, o_s, w_out[0].astype(bf16), g_ffn[0].reshape(1, D_MODEL), wr, br)

    t = tp + ts
    n_blocks = -(-(t * TOP_K) // MOE_TILE) + N_EXPERTS
    tile_table, block_e, block_seg, next_e, block_valid, n_used = _routing_tables(
        top_i, counts[:, 0], part_counts, n_blocks)
    x_sorted = _dispatch(block_valid, tile_table, top_i, rank, xn)
    y_sorted = _experts(block_e, block_seg, next_e, block_valid, n_used, x_sorted,
                        w_gu[0], b_gu[0].reshape(N_EXPERTS, 1, 2 * D_FF),
                        w_dn[0], b_dn[0].reshape(N_EXPERTS, 1, D_MODEL))

    y_p, y_s = _combine(tile_table, y_sorted, h1, top_g, top_ic, rank_c, p_prompt[0].reshape(tp, PLE_DIM),
                        p_sample[0].reshape(ts, PLE_DIM), g_ple[0].reshape(1, D_MODEL),
                        w_ple_gate[0].astype(bf16), w_ple_proj[0].astype(bf16), g_final.reshape(1, D_MODEL))

    return (y_p.reshape(batch, seq, D_MODEL), y_s.reshape(dec_batch, dec_seq, D_MODEL),
            new_gla_p, new_hgrn_p, new_gla_s, new_hgrn_s)
```
